```python
import jax, jax.numpy as jnp
from jax import lax
import numpy as np

D_MODEL = 1024
BATCH = 16
SEQ = 256
DEPTH = 4
DEC_BATCH = 4
DEC_SEQ = 4096
PAST_LEN = 256

GRID_W = 64
N_EVEN = (DEPTH + 1) // 2
N_ODD = DEPTH // 2
RET_HEADS = 4
RET_DK = 64
RET_DV = 128
RET_CHUNK = 128
CONV_CH = 512
CONV_WIDTH = 31
CONV_PAD = CONV_WIDTH // 2
GLA_HEADS = 4
GLA_DK = 128
GLA_DV = 256
GLA_RANK = 16
GLA_TAU = 16.0
GLA_CHUNK = 64
N_EXPERTS = 32
TOP_K = 4
D_FF = 1024
SWIGLU_LIMIT = 7.0
SWIGLU_ALPHA = 1.702
MOE_BLOCK = 128
ROPE_THETA = 10000.0
EPS = 1e-6
N_MOD = 6
EVEN_IN = 2 * RET_HEADS * RET_DK + 2 * RET_HEADS * RET_DV + 2 * CONV_CH
EVEN_MIX = RET_HEADS * RET_DV + CONV_CH
ODD_IN = 2 * GLA_HEADS * GLA_DK + 2 * GLA_HEADS * GLA_DV + 2 * GLA_RANK
ODD_MIX = GLA_HEADS * GLA_DV

kernel_name = 'hybrid_retconv_gla_moe_prefix_step'

F32 = jnp.float32


def rms_norm(x, g):
    xf = x.astype(F32)
    y = xf * lax.rsqrt(jnp.mean(xf * xf, axis=-1, keepdims=True) + EPS)
    return (y * g.astype(F32)).astype(x.dtype)


def layer_norm(x, g, b):
    xf = x.astype(F32)
    mu = jnp.mean(xf, axis=-1, keepdims=True)
    var = jnp.mean(jnp.square(xf - mu), axis=-1, keepdims=True)
    return ((xf - mu) * lax.rsqrt(var + EPS) * g.astype(F32) + b.astype(F32)).astype(x.dtype)


def modulate(x, shift, scale):
    return x * (1.0 + scale) + shift


def to_heads(t, d):
    B, L, _ = t.shape
    return t.reshape(B, L, -1, d).transpose(0, 2, 1, 3).astype(F32)


def head_rms(o, g):
    o = o * lax.rsqrt(jnp.mean(o * o, axis=-1, keepdims=True) + EPS)
    B, H, L, dv = o.shape
    return o.transpose(0, 2, 1, 3).reshape(B, L, H * dv) * g.astype(F32)


def to_chunks(t, C):
    B, H, L, d = t.shape
    return t.reshape(B, H, L // C, C, d).transpose(2, 0, 1, 3, 4)


def from_chunks(t):
    n, B, H, C, d = t.shape
    return t.transpose(1, 2, 0, 3, 4).reshape(B, H, n * C, d)


def flip_seq(t):
    return jnp.flip(t, axis=2)


def grid_rope(L):
    rows = L // GRID_W
    row = jnp.repeat(jnp.arange(rows, dtype=F32), GRID_W)
    col = jnp.tile(jnp.arange(GRID_W, dtype=F32), rows)
    n_f = RET_DK // 4
    freqs = ROPE_THETA ** (-jnp.arange(n_f, dtype=F32) / n_f)
    ang = jnp.concatenate([row[:, None] * freqs, col[:, None] * freqs], axis=-1)
    return jnp.cos(ang), jnp.sin(ang)


def apply_rope(x, cos, sin):
    B, H, L, d = x.shape
    xr = x.reshape(B, H, L, d // 2, 2)
    x0, x1 = xr[..., 0], xr[..., 1]
    out = jnp.stack([x0 * cos - x1 * sin, x0 * sin + x1 * cos], axis=-1)
    return out.reshape(B, H, L, d)


def retention_scan(q, k, v, log_gamma, s0, strict):
    C = RET_CHUNK
    pos = jnp.arange(C, dtype=F32)
    diff = pos[:, None] - pos[None, :]
    mask = (diff > 0) if strict else (diff >= 0)
    lg = log_gamma[:, None, None]
    d_intra = jnp.where(mask, jnp.exp(lg * jnp.where(mask, diff, 0.0)), 0.0)
    d_q = jnp.exp(log_gamma[:, None] * (pos + 1.0))[..., None]
    d_k = jnp.exp(log_gamma[:, None] * (C - 1.0 - pos))[..., None]
    d_s = jnp.exp(log_gamma * C)[:, None, None]

    def step(s, blk):
        qc, kc, vc = blk
        att = jnp.einsum('bhid,bhjd->bhij', qc, kc) * d_intra
        o = jnp.einsum('bhij,bhjv->bhiv', att, vc) + jnp.einsum('bhid,bhdv->bhiv', qc, s) * d_q
        s = s * d_s + jnp.einsum('bhjd,bhjv->bhdv', kc * d_k, vc)
        return s, o

    s_fin, o = lax.scan(step, s0, (to_chunks(q, C), to_chunks(k, C), to_chunks(v, C)))
    return from_chunks(o), s_fin


def gla_scan(q, k, v, log_a, s0, strict):
    C = GLA_CHUNK
    idx = jnp.arange(C)
    mask = (idx[:, None] > idx[None, :]) if strict else (idx[:, None] >= idx[None, :])
    mask5 = mask[:, :, None]

    def step(s, blk):
        qc, kc, vc, gc = blk
        b = jnp.cumsum(gc, axis=2)
        b_last = b[:, :, -1:, :]
        rel = jnp.exp(jnp.where(mask5, b[:, :, :, None, :] - b[:, :, None, :, :], -jnp.inf))
        att = jnp.einsum('bhid,bhjd,bhijd->bhij', qc, kc, rel)
        o = jnp.einsum('bhij,bhjv->bhiv', att, vc) + jnp.einsum('bhid,bhdv->bhiv', qc * jnp.exp(b), s)
        s = s * jnp.exp(b_last[:, :, 0, :, None]) + jnp.einsum('bhjd,bhjv->bhdv', kc * jnp.exp(b_last - b), vc)
        return s, o

    s_fin, o = lax.scan(step, s0, (to_chunks(q, C), to_chunks(k, C), to_chunks(v, C), to_chunks(log_a, C)))
    return from_chunks(o), s_fin


def dwconv(u, w, b):
    out = lax.conv_general_dilated(u, w[:, None, :], window_strides=(1,), padding=[(CONV_PAD, CONV_PAD)],
                                   dimension_numbers=('NWC', 'WIO', 'NWC'), feature_group_count=u.shape[-1])
    return out + b


def even_mixer(h, w_in, decay, gn, cw, cb, lng, lnb, w_out, s0, rope):
    B, L, _ = h.shape
    qd, vd = RET_HEADS * RET_DK, RET_HEADS * RET_DV
    proj = h @ w_in
    q, k, v, g, a, ga = jnp.split(proj, [qd, 2 * qd, 2 * qd + vd, 2 * qd + 2 * vd, 2 * qd + 2 * vd + CONV_CH], axis=-1)
    q = to_heads(q, RET_DK)
    k = to_heads(k, RET_DK) * (RET_DK ** -0.5)
    v = to_heads(v, RET_DV)
    if rope is not None:
        q = apply_rope(q, *rope)
        k = apply_rope(k, *rope)
    log_gamma = -jnp.exp(decay.astype(F32))
    s0 = s0.astype(F32)
    o_f, s_f = retention_scan(q, k, v, log_gamma[0], s0[:, 0], False)
    o_b, s_b = retention_scan(flip_seq(q), flip_seq(k), flip_seq(v), log_gamma[1], s0[:, 1], True)
    ret = head_rms(o_f + flip_seq(o_b), gn) * jax.nn.silu(g.astype(F32))
    u = a * jax.nn.sigmoid(ga)
    u = jax.nn.silu(layer_norm(dwconv(u, cw, cb), lng, lnb))
    out = jnp.concatenate([ret.astype(h.dtype), u], axis=-1) @ w_out
    return out, jnp.stack([s_f, s_b], axis=1)


def odd_mixer(h, w_in, w_a2, b_a2, gn, w_out, s0):
    B, L, _ = h.shape
    qd, vd = GLA_HEADS * GLA_DK, GLA_HEADS * GLA_DV
    proj = h @ w_in
    q, k, v, r, alr = jnp.split(proj, [qd, 2 * qd, 2 * qd + vd, 2 * qd + 2 * vd], axis=-1)
    q = to_heads(q, GLA_DK) * (GLA_DK ** -0.5)
    k = to_heads(k, GLA_DK)
    v = to_heads(v, GLA_DV)
    z = jnp.einsum('bldr,drk->dblk', alr.reshape(B, L, 2, GLA_RANK).astype(F32), w_a2.astype(F32)) \
        + b_a2.astype(F32)[:, None, None, :]
    log_a = jax.nn.log_sigmoid(z) / GLA_TAU
    log_a = log_a.reshape(2, B, L, GLA_HEADS, GLA_DK).transpose(0, 1, 3, 2, 4)
    s0 = s0.astype(F32)
    o_f, s_f = gla_scan(q, k, v, log_a[0], s0[:, 0], False)
    o_b, s_b = gla_scan(flip_seq(q), flip_seq(k), flip_seq(v), flip_seq(log_a[1]), s0[:, 1], True)
    y = head_rms(o_f + flip_seq(o_b), gn) * jax.nn.silu(r.astype(F32))
    return y.astype(h.dtype) @ w_out, jnp.stack([s_f, s_b], axis=1)


def moe(x, router_w, router_b, w_gu, b_gu, w_down, b_down):
    B, L, D = x.shape
    T = B * L
    xt = x.reshape(T, D)
    logits = (xt @ router_w + router_b).astype(F32)
    top_val, top_idx = lax.top_k(logits, TOP_K)
    gates = jax.nn.softmax(top_val, axis=-1)
    flat_e = top_idx.reshape(-1)
    flat_g = gates.reshape(-1)
    order = jnp.argsort(flat_e, stable=True)
    sorted_e = flat_e[order]
    sorted_tok = (order // TOP_K).astype(jnp.int32)
    counts = jnp.bincount(flat_e, length=N_EXPERTS)
    padded = (counts + MOE_BLOCK - 1) // MOE_BLOCK * MOE_BLOCK
    pad_end = jnp.cumsum(padded)
    pad_start = pad_end - padded
    grp_start = jnp.cumsum(counts) - counts
    dest = pad_start[sorted_e] + jnp.arange(T * TOP_K) - grp_start[sorted_e]
    n_blocks = -(-(T * TOP_K + N_EXPERTS * (MOE_BLOCK - 1)) // MOE_BLOCK)
    n_rows = n_blocks * MOE_BLOCK
    row_tok = jnp.full((n_rows,), T, jnp.int32).at[dest].set(sorted_tok)
    x_pad = jnp.concatenate([xt, jnp.zeros((1, D), xt.dtype)], axis=0)
    xb = x_pad[row_tok].reshape(n_blocks, MOE_BLOCK, D)
    block_e = jnp.minimum(jnp.searchsorted(pad_end, jnp.arange(n_blocks) * MOE_BLOCK, side='right'), N_EXPERTS - 1)

    def expert_block(args):
        xblk, e = args
        gu = xblk @ w_gu[e] + b_gu[e]
        gate = jnp.minimum(gu[:, :D_FF], SWIGLU_LIMIT)
        up = jnp.clip(gu[:, D_FF:], -SWIGLU_LIMIT, SWIGLU_LIMIT)
        hdn = gate * jax.nn.sigmoid(SWIGLU_ALPHA * gate) * (up + 1.0)
        return hdn @ w_down[e] + b_down[e]

    yb = lax.map(expert_block, (xb, block_e)).reshape(n_rows, D)
    y_sorted = yb[dest] * flat_g[order][:, None].astype(yb.dtype)
    out = jax.ops.segment_sum(y_sorted, sorted_tok, num_segments=T)
    return out.reshape(B, L, D)


def setup_inputs(seed: int = 0) -> dict:
    key = jax.random.key(seed)
    ks = jax.random.split(key, 32)
    D = D_MODEL

    def nrm(k, shape, s):
        return jax.random.normal(k, shape, F32) * s

    decay_init = jnp.asarray(np.log(-np.log(1.0 - 2.0 ** (-5.0 - np.arange(RET_HEADS)))), F32)
    return {
        'x_prompt': nrm(ks[0], (BATCH, SEQ, D), 1.0),
        'x_sample': nrm(ks[1], (DEC_BATCH, DEC_SEQ, D), 1.0),
        'state_ret': nrm(ks[2], (DEC_BATCH, N_EVEN, 2, RET_HEADS, RET_DK, RET_DV), 0.1),
        'state_gla': nrm(ks[3], (DEC_BATCH, N_ODD, 2, GLA_HEADS, GLA_DK, GLA_DV), 0.1),
        'c': nrm(ks[4], (DEC_BATCH, D), 1.0),
        'c_ctx': nrm(ks[5], (D,), 1.0),
        'w_mod': nrm(ks[6], (DEPTH, D, N_MOD * D), 0.5 * D ** -0.5),
        'b_mod': nrm(ks[7], (DEPTH, N_MOD * D), 0.01),
        'norm1_g': 1.0 + nrm(ks[8], (DEPTH, D), 0.02),
        'norm2_g': 1.0 + nrm(ks[9], (DEPTH, D), 0.02),
        'final_g': 1.0 + nrm(ks[10], (D,), 0.02),
        'even_w_in': nrm(ks[11], (N_EVEN, D, EVEN_IN), D ** -0.5),
        'ret_decay': decay_init[None, None, :] + nrm(ks[12], (N_EVEN, 2, RET_HEADS), 0.05),
        'ret_gn': 1.0 + nrm(ks[13], (N_EVEN, RET_HEADS * RET_DV), 0.02),
        'conv_w': nrm(ks[14], (N_EVEN, CONV_WIDTH, CONV_CH), CONV_WIDTH ** -0.5),
        'conv_b': nrm(ks[15], (N_EVEN, CONV_CH), 0.01),
        'conv_ln_g': 1.0 + nrm(ks[16], (N_EVEN, CONV_CH), 0.02),
        'conv_ln_b': nrm(ks[17], (N_EVEN, CONV_CH), 0.01),
        'even_w_out': nrm(ks[18], (N_EVEN, EVEN_MIX, D), EVEN_MIX ** -0.5),
        'odd_w_in': nrm(ks[19], (N_ODD, D, ODD_IN), D ** -0.5),
        'gla_w_a2': nrm(ks[20], (N_ODD, 2, GLA_RANK, GLA_HEADS * GLA_DK), GLA_RANK ** -0.5),
        'gla_b_a2': nrm(ks[21], (N_ODD, 2, GLA_HEADS * GLA_DK), 0.5),
        'gla_gn': 1.0 + nrm(ks[22], (N_ODD, GLA_HEADS * GLA_DV), 0.02),
        'odd_w_out': nrm(ks[23], (N_ODD, ODD_MIX, D), ODD_MIX ** -0.5),
        'router_w': nrm(ks[24], (DEPTH, D, N_EXPERTS), D ** -0.5),
        'router_b': nrm(ks[25], (DEPTH, N_EXPERTS), 0.01),
        'exp_w_gu': nrm(ks[26], (DEPTH, N_EXPERTS, D, 2 * D_FF), D ** -0.5),
        'exp_b_gu': nrm(ks[27], (DEPTH, N_EXPERTS, 2 * D_FF), 0.01),
        'exp_w_down': nrm(ks[28], (DEPTH, N_EXPERTS, D_FF, D), D_FF ** -0.5),
        'exp_b_down': nrm(ks[29], (DEPTH, N_EXPERTS, D), 0.01),
    }


def reference(x_prompt, x_sample, state_ret, state_gla, c, c_ctx, w_mod, b_mod, norm1_g, norm2_g, final_g,
              even_w_in, ret_decay, ret_gn, conv_w, conv_b, conv_ln_g, conv_ln_b, even_w_out,
              odd_w_in, gla_w_a2, gla_b_a2, gla_gn, odd_w_out,
              router_w, router_b, exp_w_gu, exp_b_gu, exp_w_down, exp_b_down):
    rope = grid_rope(x_sample.shape[1])
    xp, xs = x_prompt, x_sample
    Bp = xp.shape[0]
    new_ret, new_gla = [], []
    for l in range(DEPTH):
        mod_c = (jax.nn.silu(c_ctx) @ w_mod[l] + b_mod[l]).reshape(N_MOD, D_MODEL)
        mod_s = (jax.nn.silu(c) @ w_mod[l] + b_mod[l]).reshape(c.shape[0], 1, N_MOD, D_MODEL)
        hp = modulate(rms_norm(xp, norm1_g[l]), mod_c[0], mod_c[1])
        hs = modulate(rms_norm(xs, norm1_g[l]), mod_s[:, :, 0], mod_s[:, :, 1])
        if l % 2 == 0:
            e = l // 2
            zero = jnp.zeros((Bp, 2, RET_HEADS, RET_DK, RET_DV), F32)
            mp, sp = even_mixer(hp, even_w_in[e], ret_decay[e], ret_gn[e], conv_w[e], conv_b[e],
                                conv_ln_g[e], conv_ln_b[e], even_w_out[e], zero, None)
            ms, _ = even_mixer(hs, even_w_in[e], ret_decay[e], ret_gn[e], conv_w[e], conv_b[e],
                               conv_ln_g[e], conv_ln_b[e], even_w_out[e], state_ret[:, e], rope)
            new_ret.append(sp)
        else:
            o = l // 2
            zero = jnp.zeros((Bp, 2, GLA_HEADS, GLA_DK, GLA_DV), F32)
            mp, sp = odd_mixer(hp, odd_w_in[o], gla_w_a2[o], gla_b_a2[o], gla_gn[o], odd_w_out[o], zero)
            ms, _ = odd_mixer(hs, odd_w_in[o], gla_w_a2[o], gla_b_a2[o], gla_gn[o], odd_w_out[o], state_gla[:, o])
            new_gla.append(sp)
        xp = xp + mod_c[2] * mp
        xs = xs + mod_s[:, :, 2] * ms
        hp = modulate(rms_norm(xp, norm2_g[l]), mod_c[3], mod_c[4])
        hs = modulate(rms_norm(xs, norm2_g[l]), mod_s[:, :, 3], mod_s[:, :, 4])
        xp = xp + mod_c[5] * moe(hp, router_w[l], router_b[l], exp_w_gu[l], exp_b_gu[l], exp_w_down[l], exp_b_down[l])
        xs = xs + mod_s[:, :, 5] * moe(hs, router_w[l], router_b[l], exp_w_gu[l], exp_b_gu[l], exp_w_down[l], exp_b_down[l])
    y_prompt = rms_norm(xp, final_g)
    y_sample = rms_norm(xs, final_g)
    new_state_ret = jnp.stack(new_ret, axis=1)
    new_state_gla = jnp.stack(new_gla, axis=1)
    return (y_prompt, y_sample, new_state_ret, new_state_gla)
```

```python
import functools

import jax
import jax.numpy as jnp
from jax import lax
from jax.experimental import pallas as pl
from jax.experimental.pallas import tpu as pltpu

F32 = jnp.float32
BF16 = jnp.bfloat16

D_MODEL = 1024
BATCH = 16
SEQ = 256
DEPTH = 4
DEC_BATCH = 4
DEC_SEQ = 4096
GRID_W = 64
RET_HEADS = 4
RET_DK = 64
RET_DV = 128
RET_CHUNK = 128
CONV_CH = 512
CONV_WIDTH = 31
CONV_PAD = CONV_WIDTH // 2
GLA_HEADS = 4
GLA_DK = 128
GLA_DV = 256
GLA_RANK = 16
GLA_TAU = 16.0
GLA_CHUNK = 64
GLA_SUB = 16
N_EXPERTS = 32
TOP_K = 4
D_FF = 1024
SWIGLU_LIMIT = 7.0
SWIGLU_ALPHA = 1.702
MOE_BLOCK = 128
ROPE_THETA = 10000.0
EPS = 1e-6
N_MOD = 6

RB = 256
NT_PROMPT = BATCH * SEQ
NT = NT_PROMPT + DEC_BATCH * DEC_SEQ
NBLK = NT // RB
N_PROMPT_BLK = NT_PROMPT // RB
SAMPLE_BLK = DEC_SEQ // RB
NSEQ = BATCH + DEC_BATCH
MOD_ROWS = 8
HALO = 16
VMEM_LIMIT = 48 * 1024 * 1024

assert SEQ == RB and DEC_SEQ % RB == 0 and CONV_PAD < HALO


def _seq_of_block(i):
    return jnp.where(i < N_PROMPT_BLK, i, N_PROMPT_BLK + (i - N_PROMPT_BLK) // SAMPLE_BLK)


def _is_first_block(i):
    return jnp.logical_or(i < N_PROMPT_BLK, (i - N_PROMPT_BLK) % SAMPLE_BLK == 0)


def _is_last_block(i):
    return jnp.logical_or(i < N_PROMPT_BLK, (i - N_PROMPT_BLK) % SAMPLE_BLK == SAMPLE_BLK - 1)


def _mod_row(i):
    return jnp.where(i < N_PROMPT_BLK, 0, 1 + (i - N_PROMPT_BLK) // SAMPLE_BLK)


def _rope_block(i):
    return jnp.where(i < N_PROMPT_BLK, 0, 1 + (i - N_PROMPT_BLK) % SAMPLE_BLK)


def _dot(a, b):
    return jnp.dot(a, b, preferred_element_type=F32)


def _dot_nt(a, b):
    return lax.dot_general(a, b, (((1,), (1,)), ((), ())), preferred_element_type=F32)


def _dot_tn(a, b):
    return lax.dot_general(a, b, (((0,), (0,)), ((), ())), preferred_element_type=F32)


def _split2(a):
    hi = a.astype(BF16)
    lo = (a - hi.astype(F32)).astype(BF16)
    return hi, lo


def _dot_hi(a, b):
    a_hi, a_lo = _split2(a)
    b_hi, b_lo = _split2(b)
    return _dot(a_hi, b_hi) + (_dot(a_hi, b_lo) + _dot(a_lo, b_hi))


def _silu(x):
    return x * (1.0 / (1.0 + jnp.exp(-x)))


def _sigmoid(x):
    return 1.0 / (1.0 + jnp.exp(-x))


def _params(n_axes=1, vmem=VMEM_LIMIT):
    return pltpu.CompilerParams(dimension_semantics=("arbitrary",) * n_axes, vmem_limit_bytes=vmem)


MOD_TN = 1536


def _mod_kernel(c_ref, w_ref, b_ref, o_ref):
    s = _silu(c_ref[...]).astype(BF16)
    o_ref[...] = _dot(s, w_ref[...].astype(BF16)) + b_ref[...]


def _modulation(cvec, w_mod, b_mod):
    n = N_MOD * D_MODEL
    return pl.pallas_call(
        _mod_kernel,
        grid=(DEPTH, n // MOD_TN),
        in_specs=[pl.BlockSpec((MOD_ROWS, D_MODEL), lambda l, j: (0, 0)),
                  pl.BlockSpec((None, D_MODEL, MOD_TN), lambda l, j: (l, 0, j)),
                  pl.BlockSpec((None, 1, MOD_TN), lambda l, j: (l, 0, j))],
        out_specs=pl.BlockSpec((None, MOD_ROWS, MOD_TN), lambda l, j: (l, 0, j)),
        out_shape=jax.ShapeDtypeStruct((DEPTH, MOD_ROWS, n), F32),
        compiler_params=_params(2),
        name="modulation",
    )(cvec, w_mod, b_mod.reshape(DEPTH, 1, n))


def _norm_mod(x, g_ref, shift_ref, scale_ref, row):
    y = x * lax.rsqrt(jnp.mean(x * x, axis=-1, keepdims=True) + EPS) * g_ref[...]
    return y * (1.0 + scale_ref[pl.ds(row, 1), :]) + shift_ref[pl.ds(row, 1), :]


def _inproj_kernel(x_ref, g_ref, shift_ref, scale_ref, w_ref, *o_refs, widths):
    row = _mod_row(pl.program_id(0))
    hb = _norm_mod(x_ref[...], g_ref, shift_ref, scale_ref, row).astype(BF16)
    off = 0
    for o_ref, width in zip(o_refs, widths):
        o_ref[...] = _dot(hb, w_ref[:, off:off + width])
        off += width


def _inproj(x, g, shift, scale, w_bf, widths):
    n_in = w_bf.shape[1]
    row_spec = lambda width: pl.BlockSpec((RB, width), lambda i: (i, 0))
    full = lambda shape: pl.BlockSpec(shape, lambda i: (0,) * len(shape))
    return pl.pallas_call(
        functools.partial(_inproj_kernel, widths=widths),
        grid=(NBLK,),
        in_specs=[row_spec(D_MODEL), full((1, D_MODEL)), full((MOD_ROWS, D_MODEL)),
                  full((MOD_ROWS, D_MODEL)), full((D_MODEL, n_in))],
        out_specs=[row_spec(width) for width in widths],
        out_shape=[jax.ShapeDtypeStruct((NT, width), F32) for width in widths],
        compiler_params=_params(),
        name="inproj",
    )(x, g.reshape(1, D_MODEL), shift, scale, w_bf)


RC = RET_CHUNK
RET_PAIR = 2 * RET_DK


def _rope(x, cos, sin_signed):
    lane = lax.broadcasted_iota(jnp.int32, x.shape, 1)
    swapped = jnp.where(lane % 2 == 0, pltpu.roll(x, x.shape[1] - 1, 1), pltpu.roll(x, 1, 1))
    return x * cos + swapped * sin_signed


def _ret_kernel(decay_ref, q_ref, k_ref, v_ref, cos_ref, sin_ref, s0_ref, *rest, reverse):
    if reverse:
        of_ref, g_ref, gn_ref, o_ref, sfin_ref, st_ref, dm_ref, dq_ref, dk_ref, ds_ref = rest
    else:
        o_ref, sfin_ref, st_ref, dm_ref, dq_ref, dk_ref, ds_ref = rest
    step = pl.program_id(0)
    blk = NBLK - 1 - step if reverse else step
    direction = 1 if reverse else 0

    @pl.when(step == 0)
    def _():
        row = lax.broadcasted_iota(jnp.int32, (RC, RC), 0).astype(F32)
        col = lax.broadcasted_iota(jnp.int32, (RC, RC), 1).astype(F32)
        for h in range(RET_HEADS):
            lg = -jnp.exp(jnp.full((RC, RC), decay_ref[direction, h], F32))
            if reverse:
                diff = col - row
                mask = diff > 0
                q_pow = RC - row
                k_pow = row
            else:
                diff = row - col
                mask = diff >= 0
                q_pow = row + 1.0
                k_pow = RC - 1.0 - row
            dm_ref[h] = jnp.where(mask, jnp.exp(lg * jnp.where(mask, diff, 0.0)), 0.0)
            dq_ref[h] = jnp.exp(lg * q_pow)
            dk_ref[h] = jnp.exp(lg * k_pow)
            ds_ref[h] = jnp.exp(lg * RC)

    starts = _is_last_block(blk) if reverse else _is_first_block(blk)

    @pl.when(starts)
    def _():
        st_ref[...] = jnp.zeros_like(st_ref)
        for h in range(RET_HEADS):
            off = (h % 2) * RET_DK
            st_ref[h, off:off + RET_DK, :] = s0_ref[h]

    lane = lax.broadcasted_iota(jnp.int32, (1, RET_PAIR), 1)
    chunks = range(RB // RC)
    for c in (reversed(chunks) if reverse else chunks):
        rows = slice(c * RC, (c + 1) * RC)
        cos = cos_ref[rows, :]
        sin = sin_ref[rows, :]
        for p in range(RET_HEADS // 2):
            cols = slice(p * RET_PAIR, (p + 1) * RET_PAIR)
            q2 = _rope(q_ref[rows, cols], cos, sin)
            k2 = _rope(k_ref[rows, cols] * (RET_DK ** -0.5), cos, sin)
            for h in (2 * p, 2 * p + 1):
                head_mask = (lane // RET_DK == h % 2).astype(F32)
                vh = v_ref[rows, h * RET_DV:(h + 1) * RET_DV].astype(BF16)
                qm = (q2 * head_mask).astype(BF16)
                km = k2 * head_mask
                att = _dot_nt(qm, km.astype(BF16)) * dm_ref[h]
                o = _dot(att.astype(BF16), vh) + _dot(qm, st_ref[h].astype(BF16)) * dq_ref[h]
                st_ref[h] = st_ref[h] * ds_ref[h] + _dot_tn((km * dk_ref[h]).astype(BF16), vh)
                out_cols = slice(h * RET_DV, (h + 1) * RET_DV)
                if reverse:
                    o = o + of_ref[rows, out_cols]
                    o = o * lax.rsqrt(jnp.mean(o * o, axis=-1, keepdims=True) + EPS)
                    o = o * gn_ref[:, out_cols] * _silu(g_ref[rows, out_cols])
                    o_ref[rows, out_cols] = o.astype(o_ref.dtype)
                else:
                    o_ref[rows, out_cols] = o

    ends = _is_first_block(blk) if reverse else _is_last_block(blk)

    @pl.when(ends)
    def _():
        for h in range(RET_HEADS):
            off = (h % 2) * RET_DK
            sfin_ref[h] = st_ref[h, off:off + RET_DK, :]


def _retention(decay, q, k, v, cos_tab, sin_tab, s0, *, reverse, o_fwd=None, g=None, gn=None):
    bmap = (lambda j: NBLK - 1 - j) if reverse else (lambda j: j)
    direction = 1 if reverse else 0
    qd, vd = RET_HEADS * RET_DK, RET_HEADS * RET_DV
    row_spec = lambda width: pl.BlockSpec((RB, width), lambda j: (bmap(j), 0))
    state_spec = pl.BlockSpec((None, None, RET_HEADS, RET_DK, RET_DV),
                              lambda j: (_seq_of_block(bmap(j)), direction, 0, 0, 0))
    in_specs = [pl.BlockSpec(memory_space=pltpu.SMEM), row_spec(qd), row_spec(qd), row_spec(vd),
                pl.BlockSpec((RB, RET_PAIR), lambda j: (_rope_block(bmap(j)), 0)),
                pl.BlockSpec((RB, RET_PAIR), lambda j: (_rope_block(bmap(j)), 0)),
                state_spec]
    args = [decay, q, k, v, cos_tab, sin_tab, s0]
    if reverse:
        in_specs += [row_spec(vd), row_spec(vd), pl.BlockSpec((1, vd), lambda j: (0, 0))]
        args += [o_fwd, g, gn.reshape(1, vd)]
    tile = pltpu.VMEM((RET_HEADS, RC, RC), F32)
    return pl.pallas_call(
        functools.partial(_ret_kernel, reverse=reverse),
        grid=(NBLK,),
        in_specs=in_specs,
        out_specs=[row_spec(vd),
                   pl.BlockSpec((None, RET_HEADS, RET_DK, RET_DV),
                                lambda j: (_seq_of_block(bmap(j)), 0, 0, 0))],
        out_shape=[jax.ShapeDtypeStruct((NT, vd), BF16 if reverse else F32),
                   jax.ShapeDtypeStruct((NSEQ, RET_HEADS, RET_DK, RET_DV), F32)],
        scratch_shapes=[tile, tile, tile, tile, tile],
        compiler_params=_params(),
        name="retention_bwd" if reverse else "retention_fwd",
    )(*args)


CONV_RT = 32
CONV_CT = 128


def _conv_kernel(a_ref, ga_ref, ap_ref, gap_ref, an_ref, gan_ref, cw_ref, cb_ref, lng_ref, lnb_ref,
                 o_ref, u_ref, y_ref):
    blk = pl.program_id(0)
    keep_prev = jnp.where(_is_first_block(blk), 0.0, 1.0)
    keep_next = jnp.where(_is_last_block(blk), 0.0, 1.0)
    u_ref[0:HALO, :] = ap_ref[...] * _sigmoid(gap_ref[...]) * keep_prev
    u_ref[HALO:HALO + RB, :] = a_ref[...] * _sigmoid(ga_ref[...])
    u_ref[HALO + RB:HALO + RB + HALO, :] = an_ref[...] * _sigmoid(gan_ref[...]) * keep_next
    for ct in range(CONV_CH // CONV_CT):
        cols = slice(ct * CONV_CT, (ct + 1) * CONV_CT)
        for rt in range(RB // CONV_RT):
            base = HALO - CONV_PAD + rt * CONV_RT
            acc = jnp.zeros((CONV_RT, CONV_CT), F32)
            for w in range(CONV_WIDTH):
                acc = acc + u_ref[base + w:base + w + CONV_RT, cols] * cw_ref[w:w + 1, cols]
            y_ref[rt * CONV_RT:(rt + 1) * CONV_RT, cols] = acc + cb_ref[:, cols]
    y = y_ref[...]
    mu = jnp.mean(y, axis=-1, keepdims=True)
    var = jnp.mean(jnp.square(y - mu), axis=-1, keepdims=True)
    o_ref[...] = _silu((y - mu) * lax.rsqrt(var + EPS) * lng_ref[...] + lnb_ref[...]).astype(o_ref.dtype)


def _conv_module(a, ga, cw, cb, lng, lnb):
    per_blk = RB // HALO
    n_halo = NT // HALO
    row_spec = pl.BlockSpec((RB, CONV_CH), lambda i: (i, 0))
    prev_spec = pl.BlockSpec((HALO, CONV_CH), lambda i: (jnp.maximum(i * per_blk - 1, 0), 0))
    next_spec = pl.BlockSpec((HALO, CONV_CH), lambda i: (jnp.minimum((i + 1) * per_blk, n_halo - 1), 0))
    vec = pl.BlockSpec((1, CONV_CH), lambda i: (0, 0))
    return pl.pallas_call(
        _conv_kernel,
        grid=(NBLK,),
        in_specs=[row_spec, row_spec, prev_spec, prev_spec, next_spec, next_spec,
                  pl.BlockSpec((CONV_WIDTH, CONV_CH), lambda i: (0, 0)), vec, vec, vec],
        out_specs=row_spec,
        out_shape=jax.ShapeDtypeStruct((NT, CONV_CH), BF16),
        scratch_shapes=[pltpu.VMEM((RB + 2 * HALO, CONV_CH), F32), pltpu.VMEM((RB, CONV_CH), F32)],
        compiler_params=_params(),
        name="conv_module",
    )(a, ga, a, ga, a, ga, cw, cb.reshape(1, -1), lng.reshape(1, -1), lnb.reshape(1, -1))


GC = GLA_CHUNK
GLA_NSUB = GC // GLA_SUB


def _split3(a):
    p1 = a.astype(BF16)
    r1 = a - p1.astype(F32)
    p2 = r1.astype(BF16)
    p3 = (r1 - p2.astype(F32)).astype(BF16)
    return p1, p2, p3


def _gla_kernel(alr_ref, q_ref, k_ref, v_ref, wa_ref, ba_ref, s0_ref, *rest, reverse):
    if reverse:
        of_ref, r_ref, gn_ref, o_ref, sfin_ref, st_ref, ob_ref = rest
    else:
        o_ref, sfin_ref, st_ref = rest
        ob_ref = o_ref
    step = pl.program_id(0)
    blk = NBLK - 1 - step if reverse else step
    starts = _is_last_block(blk) if reverse else _is_first_block(blk)

    @pl.when(starts)
    def _():
        st_ref[...] = s0_ref[...]

    row = lax.broadcasted_iota(jnp.int32, (GC, GC), 0)
    col = lax.broadcasted_iota(jnp.int32, (GC, GC), 1)
    tri = jnp.where(col >= row if reverse else col <= row, 1.0, 0.0).astype(BF16)

    def chunk(t, carry):
        c = (RB // GC - 1 - t) if reverse else t
        rows = pl.ds(pl.multiple_of(c * GC, GC), GC)
        z = _dot_hi(alr_ref[rows, :], wa_ref[...]) + ba_ref[...]
        log_a = (jnp.minimum(z, 0.0) - jnp.log(1.0 + jnp.exp(-jnp.abs(z)))) * (1.0 / GLA_TAU)
        g1, g2, g3 = _split3(log_a)
        b_all = _dot(tri, g1) + (_dot(tri, g2) + _dot(tri, g3))
        for h in range(GLA_HEADS):
            kcols = slice(h * GLA_DK, (h + 1) * GLA_DK)
            vcols = slice(h * GLA_DV, (h + 1) * GLA_DV)
            b = b_all[:, kcols]
            qh = q_ref[rows, kcols] * (GLA_DK ** -0.5)
            kh = k_ref[rows, kcols]
            vh = v_ref[rows, vcols].astype(BF16)
            st = st_ref[h]
            edge = b[0:1, :] if reverse else b[GC - 1:GC, :]
            o = _dot_nt((qh * jnp.exp(b)).astype(BF16), st.astype(BF16))
            parts = []
            for s in range(GLA_NSUB):
                lo, hi = s * GLA_SUB, (s + 1) * GLA_SUB
                if reverse:
                    ref_b = b[hi:hi + 1, :] if s < GLA_NSUB - 1 else jnp.zeros((1, GLA_DK), F32)
                    k_lo, k_hi = lo, GC
                else:
                    ref_b = b[lo - 1:lo, :] if s > 0 else jnp.zeros((1, GLA_DK), F32)
                    k_lo, k_hi = 0, hi
                qs = (qh[lo:hi] * jnp.exp(b[lo:hi] - ref_b)).astype(BF16)
                ks = (kh[k_lo:k_hi] * jnp.exp(ref_b - b[k_lo:k_hi])).astype(BF16)
                att = _dot_nt(qs, ks)
                ri = lax.broadcasted_iota(jnp.int32, att.shape, 0) + lo
                ci = lax.broadcasted_iota(jnp.int32, att.shape, 1) + k_lo
                att = jnp.where(ci > ri if reverse else ci <= ri, att, 0.0)
                parts.append(_dot(att.astype(BF16), vh[k_lo:k_hi]))
            o = o + jnp.concatenate(parts, axis=0)
            ke = (kh * jnp.exp(edge - b)).astype(BF16)
            st_ref[h] = st * jnp.exp(edge) + _dot_tn(vh, ke)
            ob_ref[rows, vcols] = o
        return carry

    lax.fori_loop(0, RB // GC, chunk, 0)

    if reverse:
        for h in range(GLA_HEADS):
            vcols = slice(h * GLA_DV, (h + 1) * GLA_DV)
            o = ob_ref[:, vcols] + of_ref[:, vcols]
            o = o * lax.rsqrt(jnp.mean(o * o, axis=-1, keepdims=True) + EPS)
            o_ref[:, vcols] = (o * gn_ref[:, vcols] * _silu(r_ref[:, vcols])).astype(o_ref.dtype)

    ends = _is_first_block(blk) if reverse else _is_last_block(blk)

    @pl.when(ends)
    def _():
        sfin_ref[...] = st_ref[...]


def _gla(alr, q, k, v, wa, ba, s0t, *, reverse, o_fwd=None, r=None, gn=None):
    bmap = (lambda j: NBLK - 1 - j) if reverse else (lambda j: j)
    direction = 1 if reverse else 0
    qd, vd = GLA_HEADS * GLA_DK, GLA_HEADS * GLA_DV
    row_spec = lambda width: pl.BlockSpec((RB, width), lambda j: (bmap(j), 0))
    state_spec = pl.BlockSpec((None, None, GLA_HEADS, GLA_DV, GLA_DK),
                              lambda j: (_seq_of_block(bmap(j)), direction, 0, 0, 0))
    in_specs = [row_spec(2 * GLA_RANK), row_spec(qd), row_spec(qd), row_spec(vd),
                pl.BlockSpec((2 * GLA_RANK, qd), lambda j: (0, 0)),
                pl.BlockSpec((1, qd), lambda j: (0, 0)), state_spec]
    args = [alr, q, k, v, wa, ba, s0t]
    scratch = [pltpu.VMEM((GLA_HEADS, GLA_DV, GLA_DK), F32)]
    if reverse:
        in_specs += [row_spec(vd), row_spec(vd), pl.BlockSpec((1, vd), lambda j: (0, 0))]
        args += [o_fwd, r, gn.reshape(1, vd)]
        scratch += [pltpu.VMEM((RB, vd), F32)]
    return pl.pallas_call(
        functools.partial(_gla_kernel, reverse=reverse),
        grid=(NBLK,),
        in_specs=in_specs,
        out_specs=[row_spec(vd),
                   pl.BlockSpec((None, GLA_HEADS, GLA_DV, GLA_DK),
                                lambda j: (_seq_of_block(bmap(j)), 0, 0, 0))],
        out_shape=[jax.ShapeDtypeStruct((NT, vd), BF16 if reverse else F32),
                   jax.ShapeDtypeStruct((NSEQ, GLA_HEADS, GLA_DV, GLA_DK), F32)],
        scratch_shapes=scratch,
        compiler_params=_params(),
        name="gla_bwd" if reverse else "gla_fwd",
    )(*args)


def _outproj_kernel(*refs, n_mix):
    mix_refs = refs[:n_mix]
    w_refs = refs[n_mix:2 * n_mix]
    (x_ref, gate_ref, g2_ref, shift_ref, scale_ref, rw_ref, rb_ref,
     x1_ref, h2_ref, logit_ref) = refs[2 * n_mix:]
    row = _mod_row(pl.program_id(0))
    m = _dot(mix_refs[0][...], w_refs[0][...])
    for mix_ref, w_ref in zip(mix_refs[1:], w_refs[1:]):
        m = m + _dot(mix_ref[...], w_ref[...])
    x1 = x_ref[...] + gate_ref[pl.ds(row, 1), :] * m
    x1_ref[...] = x1
    h2 = _norm_mod(x1, g2_ref, shift_ref, scale_ref, row)
    h2_ref[...] = h2
    logit_ref[...] = _dot_hi(h2, rw_ref[...]) + rb_ref[...]


def _outproj(mixes, w_parts, x, gate, g2, shift, scale, rw, rb):
    n_mix = len(mixes)
    row_spec = lambda width: pl.BlockSpec((RB, width), lambda i: (i, 0))
    full = lambda shape: pl.BlockSpec(shape, lambda i: (0,) * len(shape))
    mod_spec = full((MOD_ROWS, D_MODEL))
    return pl.pallas_call(
        functools.partial(_outproj_kernel, n_mix=n_mix),
        grid=(NBLK,),
        in_specs=[row_spec(m.shape[1]) for m in mixes] + [full(w.shape) for w in w_parts]
        + [row_spec(D_MODEL), mod_spec, full((1, D_MODEL)), mod_spec, mod_spec,
           full((D_MODEL, N_EXPERTS)), full((1, N_EXPERTS))],
        out_specs=[row_spec(D_MODEL), row_spec(D_MODEL), row_spec(N_EXPERTS)],
        out_shape=[jax.ShapeDtypeStruct((NT, D_MODEL), F32), jax.ShapeDtypeStruct((NT, D_MODEL), F32),
                   jax.ShapeDtypeStruct((NT, N_EXPERTS), F32)],
        compiler_params=_params(),
        name="outproj",
    )(*mixes, *w_parts, x, gate, g2.reshape(1, D_MODEL), shift, scale, rw, rb.reshape(1, N_EXPERTS))


def _moe_kernel(be_ref, x_ref, wgu_ref, bgu_ref, wd_ref, bd_ref, y_ref, wgu_bf, wd_bf):
    i = pl.program_id(0)
    changed = jnp.logical_or(i == 0, be_ref[i] != be_ref[jnp.maximum(i - 1, 0)])

    @pl.when(changed)
    def _():
        wgu_bf[...] = wgu_ref[...].astype(BF16)
        wd_bf[...] = wd_ref[...].astype(BF16)

    gu = _dot(x_ref[...].astype(BF16), wgu_bf[...]) + bgu_ref[...]
    gate = jnp.minimum(gu[:, :D_FF], SWIGLU_LIMIT)
    up = jnp.clip(gu[:, D_FF:], -SWIGLU_LIMIT, SWIGLU_LIMIT)
    hdn = gate * _sigmoid(SWIGLU_ALPHA * gate) * (up + 1.0)
    y_ref[...] = _dot(hdn.astype(BF16), wd_bf[...]) + bd_ref[...]


def _moe_experts(block_e, xs, w_gu, b_gu, w_down, b_down):
    n_rows = xs.shape[0]
    n_blocks = n_rows // MOE_BLOCK
    grid_spec = pltpu.PrefetchScalarGridSpec(
        num_scalar_prefetch=1,
        grid=(n_blocks,),
        in_specs=[pl.BlockSpec((MOE_BLOCK, D_MODEL), lambda i, be: (i, 0)),
                  pl.BlockSpec((None, D_MODEL, 2 * D_FF), lambda i, be: (be[i], 0, 0)),
                  pl.BlockSpec((None, 1, 2 * D_FF), lambda i, be: (be[i], 0, 0)),
                  pl.BlockSpec((None, D_FF, D_MODEL), lambda i, be: (be[i], 0, 0)),
                  pl.BlockSpec((None, 1, D_MODEL), lambda i, be: (be[i], 0, 0))],
        out_specs=pl.BlockSpec((MOE_BLOCK, D_MODEL), lambda i, be: (i, 0)),
        scratch_shapes=[pltpu.VMEM((D_MODEL, 2 * D_FF), BF16), pltpu.VMEM((D_FF, D_MODEL), BF16)],
    )
    return pl.pallas_call(
        _moe_kernel,
        grid_spec=grid_spec,
        out_shape=jax.ShapeDtypeStruct((n_rows, D_MODEL), F32),
        compiler_params=_params(),
        name="moe_experts",
    )(block_e, xs, w_gu, b_gu.reshape(N_EXPERTS, 1, -1), w_down, b_down.reshape(N_EXPERTS, 1, -1))


def _combine_kernel(x1_ref, yg_ref, gw_ref, gate_ref, o_ref):
    row = _mod_row(pl.program_id(0))
    acc = yg_ref[:, 0:D_MODEL] * gw_ref[:, 0:1]
    for kk in range(1, TOP_K):
        acc = acc + yg_ref[:, kk * D_MODEL:(kk + 1) * D_MODEL] * gw_ref[:, kk:kk + 1]
    o_ref[...] = x1_ref[...] + gate_ref[pl.ds(row, 1), :] * acc


def _combine(x1, yg, gw, gate):
    row_spec = lambda width: pl.BlockSpec((RB, width), lambda i: (i, 0))
    return pl.pallas_call(
        _combine_kernel,
        grid=(NBLK,),
        in_specs=[row_spec(D_MODEL), row_spec(TOP_K * D_MODEL), row_spec(TOP_K),
                  pl.BlockSpec((MOD_ROWS, D_MODEL), lambda i: (0, 0))],
        out_specs=row_spec(D_MODEL),
        out_shape=jax.ShapeDtypeStruct((NT, D_MODEL), F32),
        compiler_params=_params(),
        name="moe_combine",
    )(x1, yg, gw, gate)


def _final_norm_kernel(x_ref, g_ref, o_ref):
    x = x_ref[...]
    o_ref[...] = x * lax.rsqrt(jnp.mean(x * x, axis=-1, keepdims=True) + EPS) * g_ref[...]


def _final_norm(x, g):
    row_spec = pl.BlockSpec((RB, D_MODEL), lambda i: (i, 0))
    return pl.pallas_call(
        _final_norm_kernel,
        grid=(NBLK,),
        in_specs=[row_spec, pl.BlockSpec((1, D_MODEL), lambda i: (0, 0))],
        out_specs=row_spec,
        out_shape=jax.ShapeDtypeStruct((NT, D_MODEL), F32),
        compiler_params=_params(),
        name="final_norm",
    )(x, g.reshape(1, D_MODEL))


def _routing_plan(logits):
    t = logits.shape[0]
    top_val, top_idx = lax.top_k(logits, TOP_K)
    gates = jax.nn.softmax(top_val, axis=-1)
    flat_e = top_idx.reshape(-1)
    order = jnp.argsort(flat_e, stable=True)
    sorted_e = flat_e[order]
    sorted_tok = (order // TOP_K).astype(jnp.int32)
    counts = jnp.bincount(flat_e, length=N_EXPERTS)
    padded = (counts + MOE_BLOCK - 1) // MOE_BLOCK * MOE_BLOCK
    pad_end = jnp.cumsum(padded)
    pad_start = pad_end - padded
    grp_start = jnp.cumsum(counts) - counts
    dest_sorted = pad_start[sorted_e] + jnp.arange(t * TOP_K) - grp_start[sorted_e]
    n_blocks = -(-(t * TOP_K + N_EXPERTS * (MOE_BLOCK - 1)) // MOE_BLOCK)
    n_rows = n_blocks * MOE_BLOCK
    row_tok = jnp.full((n_rows,), t, jnp.int32).at[dest_sorted].set(sorted_tok)
    block_e = jnp.minimum(jnp.searchsorted(pad_end, jnp.arange(n_blocks) * MOE_BLOCK, side='right'),
                          N_EXPERTS - 1).astype(jnp.int32)
    dest = jnp.zeros((t * TOP_K,), jnp.int32).at[order].set(dest_sorted.astype(jnp.int32))
    return row_tok, block_e, dest, gates


def _rope_tables():
    rows = DEC_SEQ // GRID_W
    row = jnp.repeat(jnp.arange(rows, dtype=F32), GRID_W)
    col = jnp.tile(jnp.arange(GRID_W, dtype=F32), rows)
    n_f = RET_DK // 4
    freqs = ROPE_THETA ** (-jnp.arange(n_f, dtype=F32) / n_f)
    ang = jnp.concatenate([row[:, None] * freqs, col[:, None] * freqs], axis=-1)
    cos = jnp.repeat(jnp.cos(ang), 2, axis=-1)
    sin = jnp.repeat(jnp.sin(ang), 2, axis=-1) * jnp.tile(jnp.asarray([-1.0, 1.0], F32), RET_DK // 2)
    cos = jnp.concatenate([jnp.ones((RB, RET_DK), F32), cos], axis=0)
    sin = jnp.concatenate([jnp.zeros((RB, RET_DK), F32), sin], axis=0)
    return jnp.tile(cos, (1, 2)), jnp.tile(sin, (1, 2))


def kernel(x_prompt, x_sample, state_ret, state_gla, c, c_ctx, w_mod, b_mod, norm1_g, norm2_g, final_g, even_w_in, ret_decay, ret_gn, conv_w, conv_b, conv_ln_g, conv_ln_b, even_w_out, odd_w_in, gla_w_a2, gla_b_a2, gla_gn, odd_w_out, router_w, router_b, exp_w_gu, exp_b_gu, exp_w_down, exp_b_down):
    x = jnp.concatenate([x_prompt.reshape(NT_PROMPT, D_MODEL), x_sample.reshape(-1, D_MODEL)], axis=0)
    cvec = jnp.concatenate([c_ctx[None, :], c, jnp.zeros((MOD_ROWS - 1 - DEC_BATCH, D_MODEL), F32)], axis=0)
    mods = _modulation(cvec, w_mod, b_mod).reshape(DEPTH, MOD_ROWS, N_MOD, D_MODEL)
    cos_tab, sin_tab = _rope_tables()
    new_ret, new_gla = [], []
    for l in range(DEPTH):
        mod = [mods[l, :, j, :] for j in range(N_MOD)]
        if l % 2 == 0:
            e = l // 2
            qd, vd = RET_HEADS * RET_DK, RET_HEADS * RET_DV
            q, k, v, g, a, ga = _inproj(x, norm1_g[l], mod[0], mod[1], even_w_in[e].astype(BF16),
                                        (qd, qd, vd, vd, CONV_CH, CONV_CH))
            s0 = jnp.concatenate([jnp.zeros((BATCH,) + state_ret.shape[2:], F32), state_ret[:, e]], axis=0)
            o_f, s_f = _retention(ret_decay[e], q, k, v, cos_tab, sin_tab, s0, reverse=False)
            ret, s_b = _retention(ret_decay[e], q, k, v, cos_tab, sin_tab, s0, reverse=True,
                                  o_fwd=o_f, g=g, gn=ret_gn[e])
            u = _conv_module(a, ga, conv_w[e], conv_b[e], conv_ln_g[e], conv_ln_b[e])
            w_out = even_w_out[e].astype(BF16)
            mixes, w_parts = [ret, u], [w_out[:vd], w_out[vd:]]
            new_ret.append(jnp.stack([s_f[:BATCH], s_b[:BATCH]], axis=1))
        else:
            o = l // 2
            qd, vd = GLA_HEADS * GLA_DK, GLA_HEADS * GLA_DV
            q, k, v, r, alr = _inproj(x, norm1_g[l], mod[0], mod[1], odd_w_in[o].astype(BF16),
                                      (qd, qd, vd, vd, 2 * GLA_RANK))
            s0t = jnp.concatenate([jnp.zeros((BATCH,) + state_gla.shape[2:], F32), state_gla[:, o]], axis=0)
            s0t = jnp.swapaxes(s0t, -1, -2)
            zeros = jnp.zeros((GLA_RANK, qd), F32)
            wa_f = jnp.concatenate([gla_w_a2[o, 0], zeros], axis=0)
            wa_b = jnp.concatenate([zeros, gla_w_a2[o, 1]], axis=0)
            o_f, s_f = _gla(alr, q, k, v, wa_f, gla_b_a2[o, 0].reshape(1, qd), s0t, reverse=False)
            y, s_b = _gla(alr, q, k, v, wa_b, gla_b_a2[o, 1].reshape(1, qd), s0t, reverse=True,
                          o_fwd=o_f, r=r, gn=gla_gn[o])
            mixes, w_parts = [y], [odd_w_out[o].astype(BF16)]
            new_gla.append(jnp.swapaxes(jnp.stack([s_f[:BATCH], s_b[:BATCH]], axis=1), -1, -2))
        x1, h2, logits = _outproj(mixes, w_parts, x, mod[2], norm2_g[l], mod[3], mod[4],
                                  router_w[l], router_b[l])
        row_tok, block_e, dest, gates = _routing_plan(logits)
        xs = jnp.concatenate([h2, jnp.zeros((1, D_MODEL), F32)], axis=0)[row_tok]
        yb = _moe_experts(block_e, xs, exp_w_gu[l], exp_b_gu[l], exp_w_down[l], exp_b_down[l])
        yg = yb[dest].reshape(NT, TOP_K * D_MODEL)
        x = _combine(x1, yg, gates, mod[5])
    y = _final_norm(x, final_g)
    y_prompt = y[:NT_PROMPT].reshape(BATCH, SEQ, D_MODEL)
    y_sample = y[NT_PROMPT:].reshape(DEC_BATCH, DEC_SEQ, D_MODEL)
    return (y_prompt, y_sample, jnp.stack(new_ret, axis=1), jnp.stack(new_gla, axis=1))
```

```python
import functools

import jax
import jax.numpy as jnp
from jax import lax
from jax.experimental import pallas as pl
from jax.experimental.pallas import tpu as pltpu
from jax.experimental.pallas import tpu_sc as plsc

F32 = jnp.float32
BF16 = jnp.bfloat16

D_MODEL = 1024
BATCH = 16
SEQ = 256
DEPTH = 4
DEC_BATCH = 4
DEC_SEQ = 4096
GRID_W = 64
RET_HEADS = 4
RET_DK = 64
RET_DV = 128
RET_CHUNK = 128
CONV_CH = 512
CONV_WIDTH = 31
CONV_PAD = CONV_WIDTH // 2
GLA_HEADS = 4
GLA_DK = 128
GLA_DV = 256
GLA_RANK = 16
GLA_TAU = 16.0
GLA_CHUNK = 64
GLA_SUB = 16
N_EXPERTS = 32
TOP_K = 4
D_FF = 1024
SWIGLU_LIMIT = 7.0
SWIGLU_ALPHA = 1.702
MOE_BLOCK = 128
ROPE_THETA = 10000.0
EPS = 1e-6
N_MOD = 6

RB = 256
NT_PROMPT = BATCH * SEQ
NT = NT_PROMPT + DEC_BATCH * DEC_SEQ
NBLK = NT // RB
N_PROMPT_BLK = NT_PROMPT // RB
SAMPLE_BLK = DEC_SEQ // RB
NSEQ = BATCH + DEC_BATCH
MOD_ROWS = 8
HALO = 16
VMEM_LIMIT = 48 * 1024 * 1024

assert SEQ == RB and DEC_SEQ % RB == 0 and CONV_PAD < HALO


def _seq_of_block(i):
    return jnp.where(i < N_PROMPT_BLK, i, N_PROMPT_BLK + (i - N_PROMPT_BLK) // SAMPLE_BLK)


def _is_first_block(i):
    return jnp.logical_or(i < N_PROMPT_BLK, (i - N_PROMPT_BLK) % SAMPLE_BLK == 0)


def _is_last_block(i):
    return jnp.logical_or(i < N_PROMPT_BLK, (i - N_PROMPT_BLK) % SAMPLE_BLK == SAMPLE_BLK - 1)


def _mod_row(i):
    return jnp.where(i < N_PROMPT_BLK, 0, 1 + (i - N_PROMPT_BLK) // SAMPLE_BLK)


def _rope_block(i):
    return jnp.where(i < N_PROMPT_BLK, 0, 1 + (i - N_PROMPT_BLK) % SAMPLE_BLK)


def _dot(a, b):
    return jnp.dot(a, b, preferred_element_type=F32)


def _dot_nt(a, b):
    return lax.dot_general(a, b, (((1,), (1,)), ((), ())), preferred_element_type=F32)


def _dot_tn(a, b):
    return lax.dot_general(a, b, (((0,), (0,)), ((), ())), preferred_element_type=F32)


def _split2(a):
    hi = a.astype(BF16)
    lo = (a - hi.astype(F32)).astype(BF16)
    return hi, lo


def _dot_hi(a, b):
    a_hi, a_lo = _split2(a)
    b_hi, b_lo = _split2(b)
    return _dot(a_hi, b_hi) + (_dot(a_hi, b_lo) + _dot(a_lo, b_hi))


def _silu(x):
    return x * (1.0 / (1.0 + jnp.exp(-x)))


def _sigmoid(x):
    return 1.0 / (1.0 + jnp.exp(-x))


def _params(n_axes=1, vmem=VMEM_LIMIT):
    return pltpu.CompilerParams(dimension_semantics=("arbitrary",) * n_axes, vmem_limit_bytes=vmem)


MOD_TN = 1536


def _mod_kernel(c_ref, w_ref, b_ref, o_ref):
    s = _silu(c_ref[...]).astype(BF16)
    o_ref[...] = _dot(s, w_ref[...].astype(BF16)) + b_ref[...]


def _modulation(cvec, w_mod, b_mod):
    n = N_MOD * D_MODEL
    return pl.pallas_call(
        _mod_kernel,
        grid=(DEPTH, n // MOD_TN),
        in_specs=[pl.BlockSpec((MOD_ROWS, D_MODEL), lambda l, j: (0, 0)),
                  pl.BlockSpec((None, D_MODEL, MOD_TN), lambda l, j: (l, 0, j)),
                  pl.BlockSpec((None, 1, MOD_TN), lambda l, j: (l, 0, j))],
        out_specs=pl.BlockSpec((None, MOD_ROWS, MOD_TN), lambda l, j: (l, 0, j)),
        out_shape=jax.ShapeDtypeStruct((DEPTH, MOD_ROWS, n), F32),
        compiler_params=_params(2),
        name="modulation",
    )(cvec, w_mod, b_mod.reshape(DEPTH, 1, n))


def _norm_mod(x, g_ref, shift_ref, scale_ref, row):
    y = x * lax.rsqrt(jnp.mean(x * x, axis=-1, keepdims=True) + EPS) * g_ref[...]
    return y * (1.0 + scale_ref[pl.ds(row, 1), :]) + shift_ref[pl.ds(row, 1), :]


def _inproj_kernel(x_ref, g_ref, shift_ref, scale_ref, w_ref, *o_refs, widths):
    row = _mod_row(pl.program_id(0))
    hb = _norm_mod(x_ref[...], g_ref, shift_ref, scale_ref, row).astype(BF16)
    off = 0
    for o_ref, width in zip(o_refs, widths):
        o_ref[...] = _dot(hb, w_ref[:, off:off + width])
        off += width


def _inproj(x, g, shift, scale, w_bf, widths):
    n_in = w_bf.shape[1]
    row_spec = lambda width: pl.BlockSpec((RB, width), lambda i: (i, 0))
    full = lambda shape: pl.BlockSpec(shape, lambda i: (0,) * len(shape))
    return pl.pallas_call(
        functools.partial(_inproj_kernel, widths=widths),
        grid=(NBLK,),
        in_specs=[row_spec(D_MODEL), full((1, D_MODEL)), full((MOD_ROWS, D_MODEL)),
                  full((MOD_ROWS, D_MODEL)), full((D_MODEL, n_in))],
        out_specs=[row_spec(width) for width in widths],
        out_shape=[jax.ShapeDtypeStruct((NT, width), F32) for width in widths],
        compiler_params=_params(),
        name="inproj",
    )(x, g.reshape(1, D_MODEL), shift, scale, w_bf)


RC = RET_CHUNK
RET_PAIR = 2 * RET_DK


def _rope(x, cos, sin_signed):
    lane = lax.broadcasted_iota(jnp.int32, x.shape, 1)
    swapped = jnp.where(lane % 2 == 0, pltpu.roll(x, x.shape[1] - 1, 1), pltpu.roll(x, 1, 1))
    return x * cos + swapped * sin_signed


def _ret_kernel(decay_ref, q_ref, k_ref, v_ref, cos_ref, sin_ref, s0_ref, *rest, reverse):
    if reverse:
        of_ref, g_ref, gn_ref, o_ref, sfin_ref, st_ref, dm_ref, dq_ref, dk_ref, ds_ref = rest
    else:
        o_ref, sfin_ref, st_ref, dm_ref, dq_ref, dk_ref, ds_ref = rest
    step = pl.program_id(0)
    blk = NBLK - 1 - step if reverse else step
    direction = 1 if reverse else 0

    @pl.when(step == 0)
    def _():
        row = lax.broadcasted_iota(jnp.int32, (RC, RC), 0).astype(F32)
        col = lax.broadcasted_iota(jnp.int32, (RC, RC), 1).astype(F32)
        for h in range(RET_HEADS):
            lg = -jnp.exp(jnp.full((RC, RC), decay_ref[direction, h], F32))
            if reverse:
                diff = col - row
                mask = diff > 0
                q_pow = RC - row
                k_pow = row
            else:
                diff = row - col
                mask = diff >= 0
                q_pow = row + 1.0
                k_pow = RC - 1.0 - row
            dm_ref[h] = jnp.where(mask, jnp.exp(lg * jnp.where(mask, diff, 0.0)), 0.0)
            dq_ref[h] = jnp.exp(lg * q_pow)
            dk_ref[h] = jnp.exp(lg * k_pow)
            ds_ref[h] = jnp.exp(lg * RC)

    starts = _is_last_block(blk) if reverse else _is_first_block(blk)

    @pl.when(starts)
    def _():
        st_ref[...] = jnp.zeros_like(st_ref)
        for h in range(RET_HEADS):
            off = (h % 2) * RET_DK
            st_ref[h, off:off + RET_DK, :] = s0_ref[h]

    lane = lax.broadcasted_iota(jnp.int32, (1, RET_PAIR), 1)
    chunks = range(RB // RC)
    for c in (reversed(chunks) if reverse else chunks):
        rows = slice(c * RC, (c + 1) * RC)
        cos = cos_ref[rows, :]
        sin = sin_ref[rows, :]
        for p in range(RET_HEADS // 2):
            cols = slice(p * RET_PAIR, (p + 1) * RET_PAIR)
            q2 = _rope(q_ref[rows, cols], cos, sin)
            k2 = _rope(k_ref[rows, cols] * (RET_DK ** -0.5), cos, sin)
            for h in (2 * p, 2 * p + 1):
                head_mask = (lane // RET_DK == h % 2).astype(F32)
                vh = v_ref[rows, h * RET_DV:(h + 1) * RET_DV].astype(BF16)
                qm = (q2 * head_mask).astype(BF16)
                km = k2 * head_mask
                att = _dot_nt(qm, km.astype(BF16)) * dm_ref[h]
                o = _dot(att.astype(BF16), vh) + _dot(qm, st_ref[h].astype(BF16)) * dq_ref[h]
                st_ref[h] = st_ref[h] * ds_ref[h] + _dot_tn((km * dk_ref[h]).astype(BF16), vh)
                out_cols = slice(h * RET_DV, (h + 1) * RET_DV)
                if reverse:
                    o = o + of_ref[rows, out_cols]
                    o = o * lax.rsqrt(jnp.mean(o * o, axis=-1, keepdims=True) + EPS)
                    o = o * gn_ref[:, out_cols] * _silu(g_ref[rows, out_cols])
                    o_ref[rows, out_cols] = o.astype(o_ref.dtype)
                else:
                    o_ref[rows, out_cols] = o

    ends = _is_first_block(blk) if reverse else _is_last_block(blk)

    @pl.when(ends)
    def _():
        for h in range(RET_HEADS):
            off = (h % 2) * RET_DK
            sfin_ref[h] = st_ref[h, off:off + RET_DK, :]


def _retention(decay, q, k, v, cos_tab, sin_tab, s0, *, reverse, o_fwd=None, g=None, gn=None):
    bmap = (lambda j: NBLK - 1 - j) if reverse else (lambda j: j)
    direction = 1 if reverse else 0
    qd, vd = RET_HEADS * RET_DK, RET_HEADS * RET_DV
    row_spec = lambda width: pl.BlockSpec((RB, width), lambda j: (bmap(j), 0))
    state_spec = pl.BlockSpec((None, None, RET_HEADS, RET_DK, RET_DV),
                              lambda j: (_seq_of_block(bmap(j)), direction, 0, 0, 0))
    in_specs = [pl.BlockSpec(memory_space=pltpu.SMEM), row_spec(qd), row_spec(qd), row_spec(vd),
                pl.BlockSpec((RB, RET_PAIR), lambda j: (_rope_block(bmap(j)), 0)),
                pl.BlockSpec((RB, RET_PAIR), lambda j: (_rope_block(bmap(j)), 0)),
                state_spec]
    args = [decay, q, k, v, cos_tab, sin_tab, s0]
    if reverse:
        in_specs += [row_spec(vd), row_spec(vd), pl.BlockSpec((1, vd), lambda j: (0, 0))]
        args += [o_fwd, g, gn.reshape(1, vd)]
    tile = pltpu.VMEM((RET_HEADS, RC, RC), F32)
    return pl.pallas_call(
        functools.partial(_ret_kernel, reverse=reverse),
        grid=(NBLK,),
        in_specs=in_specs,
        out_specs=[row_spec(vd),
                   pl.BlockSpec((None, RET_HEADS, RET_DK, RET_DV),
                                lambda j: (_seq_of_block(bmap(j)), 0, 0, 0))],
        out_shape=[jax.ShapeDtypeStruct((NT, vd), BF16 if reverse else F32),
                   jax.ShapeDtypeStruct((NSEQ, RET_HEADS, RET_DK, RET_DV), F32)],
        scratch_shapes=[tile, tile, tile, tile, tile],
        compiler_params=_params(),
        name="retention_bwd" if reverse else "retention_fwd",
    )(*args)


CONV_RT = 32
CONV_CT = 128


def _conv_kernel(a_ref, ga_ref, ap_ref, gap_ref, an_ref, gan_ref, cw_ref, cb_ref, lng_ref, lnb_ref,
                 o_ref, u_ref, y_ref):
    blk = pl.program_id(0)
    keep_prev = jnp.where(_is_first_block(blk), 0.0, 1.0)
    keep_next = jnp.where(_is_last_block(blk), 0.0, 1.0)
    u_ref[0:HALO, :] = ap_ref[...] * _sigmoid(gap_ref[...]) * keep_prev
    u_ref[HALO:HALO + RB, :] = a_ref[...] * _sigmoid(ga_ref[...])
    u_ref[HALO + RB:HALO + RB + HALO, :] = an_ref[...] * _sigmoid(gan_ref[...]) * keep_next
    for ct in range(CONV_CH // CONV_CT):
        cols = slice(ct * CONV_CT, (ct + 1) * CONV_CT)
        for rt in range(RB // CONV_RT):
            base = HALO - CONV_PAD + rt * CONV_RT
            acc = jnp.zeros((CONV_RT, CONV_CT), F32)
            for w in range(CONV_WIDTH):
                acc = acc + u_ref[base + w:base + w + CONV_RT, cols] * cw_ref[w:w + 1, cols]
            y_ref[rt * CONV_RT:(rt + 1) * CONV_RT, cols] = acc + cb_ref[:, cols]
    y = y_ref[...]
    mu = jnp.mean(y, axis=-1, keepdims=True)
    var = jnp.mean(jnp.square(y - mu), axis=-1, keepdims=True)
    o_ref[...] = _silu((y - mu) * lax.rsqrt(var + EPS) * lng_ref[...] + lnb_ref[...]).astype(o_ref.dtype)


def _conv_module(a, ga, cw, cb, lng, lnb):
    per_blk = RB // HALO
    n_halo = NT // HALO
    row_spec = pl.BlockSpec((RB, CONV_CH), lambda i: (i, 0))
    prev_spec = pl.BlockSpec((HALO, CONV_CH), lambda i: (jnp.maximum(i * per_blk - 1, 0), 0))
    next_spec = pl.BlockSpec((HALO, CONV_CH), lambda i: (jnp.minimum((i + 1) * per_blk, n_halo - 1), 0))
    vec = pl.BlockSpec((1, CONV_CH), lambda i: (0, 0))
    return pl.pallas_call(
        _conv_kernel,
        grid=(NBLK,),
        in_specs=[row_spec, row_spec, prev_spec, prev_spec, next_spec, next_spec,
                  pl.BlockSpec((CONV_WIDTH, CONV_CH), lambda i: (0, 0)), vec, vec, vec],
        out_specs=row_spec,
        out_shape=jax.ShapeDtypeStruct((NT, CONV_CH), BF16),
        scratch_shapes=[pltpu.VMEM((RB + 2 * HALO, CONV_CH), F32), pltpu.VMEM((RB, CONV_CH), F32)],
        compiler_params=_params(),
        name="conv_module",
    )(a, ga, a, ga, a, ga, cw, cb.reshape(1, -1), lng.reshape(1, -1), lnb.reshape(1, -1))


GC = GLA_CHUNK
GLA_NSUB = GC // GLA_SUB


def _split3(a):
    p1 = a.astype(BF16)
    r1 = a - p1.astype(F32)
    p2 = r1.astype(BF16)
    p3 = (r1 - p2.astype(F32)).astype(BF16)
    return p1, p2, p3


def _gla_kernel(alr_ref, q_ref, k_ref, v_ref, wa_ref, ba_ref, s0_ref, *rest, reverse):
    if reverse:
        of_ref, r_ref, gn_ref, o_ref, sfin_ref, st_ref, ob_ref = rest
    else:
        o_ref, sfin_ref, st_ref = rest
        ob_ref = o_ref
    step = pl.program_id(0)
    blk = NBLK - 1 - step if reverse else step
    starts = _is_last_block(blk) if reverse else _is_first_block(blk)

    @pl.when(starts)
    def _():
        st_ref[...] = s0_ref[...]

    row = lax.broadcasted_iota(jnp.int32, (GC, GC), 0)
    col = lax.broadcasted_iota(jnp.int32, (GC, GC), 1)
    tri = jnp.where(col >= row if reverse else col <= row, 1.0, 0.0).astype(BF16)

    def chunk(t, carry):
        c = (RB // GC - 1 - t) if reverse else t
        rows = pl.ds(pl.multiple_of(c * GC, GC), GC)
        z = _dot_hi(alr_ref[rows, :], wa_ref[...]) + ba_ref[...]
        log_a = (jnp.minimum(z, 0.0) - jnp.log(1.0 + jnp.exp(-jnp.abs(z)))) * (1.0 / GLA_TAU)
        g1, g2, g3 = _split3(log_a)
        b_all = _dot(tri, g1) + (_dot(tri, g2) + _dot(tri, g3))
        for h in range(GLA_HEADS):
            kcols = slice(h * GLA_DK, (h + 1) * GLA_DK)
            vcols = slice(h * GLA_DV, (h + 1) * GLA_DV)
            b = b_all[:, kcols]
            qh = q_ref[rows, kcols] * (GLA_DK ** -0.5)
            kh = k_ref[rows, kcols]
            vh = v_ref[rows, vcols].astype(BF16)
            st = st_ref[h]
            edge = b[0:1, :] if reverse else b[GC - 1:GC, :]
            o = _dot_nt((qh * jnp.exp(b)).astype(BF16), st.astype(BF16))
            parts = []
            for s in range(GLA_NSUB):
                lo, hi = s * GLA_SUB, (s + 1) * GLA_SUB
                if reverse:
                    ref_b = b[hi:hi + 1, :] if s < GLA_NSUB - 1 else jnp.zeros((1, GLA_DK), F32)
                    k_lo, k_hi = lo, GC
                else:
                    ref_b = b[lo - 1:lo, :] if s > 0 else jnp.zeros((1, GLA_DK), F32)
                    k_lo, k_hi = 0, hi
                qs = (qh[lo:hi] * jnp.exp(b[lo:hi] - ref_b)).astype(BF16)
                ks = (kh[k_lo:k_hi] * jnp.exp(ref_b - b[k_lo:k_hi])).astype(BF16)
                att = _dot_nt(qs, ks)
                ri = lax.broadcasted_iota(jnp.int32, att.shape, 0) + lo
                ci = lax.broadcasted_iota(jnp.int32, att.shape, 1) + k_lo
                att = jnp.where(ci > ri if reverse else ci <= ri, att, 0.0)
                parts.append(_dot(att.astype(BF16), vh[k_lo:k_hi]))
            o = o + jnp.concatenate(parts, axis=0)
            ke = (kh * jnp.exp(edge - b)).astype(BF16)
            st_ref[h] = st * jnp.exp(edge) + _dot_tn(vh, ke)
            ob_ref[rows, vcols] = o
        return carry

    lax.fori_loop(0, RB // GC, chunk, 0)

    if reverse:
        for h in range(GLA_HEADS):
            vcols = slice(h * GLA_DV, (h + 1) * GLA_DV)
            o = ob_ref[:, vcols] + of_ref[:, vcols]
            o = o * lax.rsqrt(jnp.mean(o * o, axis=-1, keepdims=True) + EPS)
            o_ref[:, vcols] = (o * gn_ref[:, vcols] * _silu(r_ref[:, vcols])).astype(o_ref.dtype)

    ends = _is_first_block(blk) if reverse else _is_last_block(blk)

    @pl.when(ends)
    def _():
        sfin_ref[...] = st_ref[...]


def _gla(alr, q, k, v, wa, ba, s0t, *, reverse, o_fwd=None, r=None, gn=None):
    bmap = (lambda j: NBLK - 1 - j) if reverse else (lambda j: j)
    direction = 1 if reverse else 0
    qd, vd = GLA_HEADS * GLA_DK, GLA_HEADS * GLA_DV
    row_spec = lambda width: pl.BlockSpec((RB, width), lambda j: (bmap(j), 0))
    state_spec = pl.BlockSpec((None, None, GLA_HEADS, GLA_DV, GLA_DK),
                              lambda j: (_seq_of_block(bmap(j)), direction, 0, 0, 0))
    in_specs = [row_spec(2 * GLA_RANK), row_spec(qd), row_spec(qd), row_spec(vd),
                pl.BlockSpec((2 * GLA_RANK, qd), lambda j: (0, 0)),
                pl.BlockSpec((1, qd), lambda j: (0, 0)), state_spec]
    args = [alr, q, k, v, wa, ba, s0t]
    scratch = [pltpu.VMEM((GLA_HEADS, GLA_DV, GLA_DK), F32)]
    if reverse:
        in_specs += [row_spec(vd), row_spec(vd), pl.BlockSpec((1, vd), lambda j: (0, 0))]
        args += [o_fwd, r, gn.reshape(1, vd)]
        scratch += [pltpu.VMEM((RB, vd), F32)]
    return pl.pallas_call(
        functools.partial(_gla_kernel, reverse=reverse),
        grid=(NBLK,),
        in_specs=in_specs,
        out_specs=[row_spec(vd),
                   pl.BlockSpec((None, GLA_HEADS, GLA_DV, GLA_DK),
                                lambda j: (_seq_of_block(bmap(j)), 0, 0, 0))],
        out_shape=[jax.ShapeDtypeStruct((NT, vd), BF16 if reverse else F32),
                   jax.ShapeDtypeStruct((NSEQ, GLA_HEADS, GLA_DV, GLA_DK), F32)],
        scratch_shapes=scratch,
        compiler_params=_params(),
        name="gla_bwd" if reverse else "gla_fwd",
    )(*args)


def _outproj_kernel(*refs, n_mix):
    mix_refs = refs[:n_mix]
    w_refs = refs[n_mix:2 * n_mix]
    (x_ref, gate_ref, g2_ref, shift_ref, scale_ref, rw_ref, rb_ref,
     x1_ref, h2_ref, idx_ref, rank_ref, gw_ref, cnt_ref, carry_ref) = refs[2 * n_mix:]
    step = pl.program_id(0)
    row = _mod_row(step)
    m = _dot(mix_refs[0][...], w_refs[0][...])
    for mix_ref, w_ref in zip(mix_refs[1:], w_refs[1:]):
        m = m + _dot(mix_ref[...], w_ref[...])
    x1 = x_ref[...] + gate_ref[pl.ds(row, 1), :] * m
    x1_ref[...] = x1
    h2 = _norm_mod(x1, g2_ref, shift_ref, scale_ref, row)
    h2_ref[...] = h2
    logits = _dot_hi(h2, rw_ref[...]) + rb_ref[...]

    @pl.when(step == 0)
    def _():
        carry_ref[...] = jnp.zeros_like(carry_ref)

    lane = lax.broadcasted_iota(jnp.int32, logits.shape, 1)
    work = logits
    onehots, top_vals = [], []
    for kk in range(TOP_K):
        top = jnp.max(work, axis=-1, keepdims=True)
        first = jnp.min(jnp.where(work == top, lane, N_EXPERTS), axis=-1, keepdims=True)
        onehot = lane == first
        idx_ref[:, kk:kk + 1] = first
        onehots.append(onehot)
        top_vals.append(top)
        work = jnp.where(onehot, -jnp.inf, work)
    exps = [jnp.exp(v - top_vals[0]) for v in top_vals]
    denom = exps[0]
    for e in exps[1:]:
        denom = denom + e
    for kk in range(TOP_K):
        gw_ref[:, kk:kk + 1] = exps[kk] / denom

    sel = jnp.zeros(logits.shape, F32)
    for onehot in onehots:
        sel = sel + jnp.where(onehot, 1.0, 0.0)
    r_i = lax.broadcasted_iota(jnp.int32, (RB, RB), 0)
    c_i = lax.broadcasted_iota(jnp.int32, (RB, RB), 1)
    earlier = jnp.where(c_i < r_i, 1.0, 0.0).astype(BF16)
    pos = _dot(earlier, sel.astype(BF16)) + carry_ref[...]
    for kk in range(TOP_K):
        rank = jnp.sum(jnp.where(onehots[kk], pos, 0.0), axis=-1, keepdims=True)
        rank_ref[:, kk:kk + 1] = rank.astype(jnp.int32)
    carry = carry_ref[...] + jnp.sum(sel, axis=0, keepdims=True)
    carry_ref[...] = carry
    cnt_ref[...] = carry.astype(jnp.int32)


def _outproj(mixes, w_parts, x, gate, g2, shift, scale, rw, rb):
    n_mix = len(mixes)
    row_spec = lambda width: pl.BlockSpec((RB, width), lambda i: (i, 0))
    full = lambda shape: pl.BlockSpec(shape, lambda i: (0,) * len(shape))
    mod_spec = full((MOD_ROWS, D_MODEL))
    return pl.pallas_call(
        functools.partial(_outproj_kernel, n_mix=n_mix),
        grid=(NBLK,),
        in_specs=[row_spec(m.shape[1]) for m in mixes] + [full(w.shape) for w in w_parts]
        + [row_spec(D_MODEL), mod_spec, full((1, D_MODEL)), mod_spec, mod_spec,
           full((D_MODEL, N_EXPERTS)), full((1, N_EXPERTS))],
        out_specs=[row_spec(D_MODEL), row_spec(D_MODEL), row_spec(TOP_K), row_spec(TOP_K), row_spec(TOP_K),
                   full((1, N_EXPERTS))],
        out_shape=[jax.ShapeDtypeStruct((NT, D_MODEL), F32), jax.ShapeDtypeStruct((NT, D_MODEL), F32),
                   jax.ShapeDtypeStruct((NT, TOP_K), jnp.int32), jax.ShapeDtypeStruct((NT, TOP_K), jnp.int32),
                   jax.ShapeDtypeStruct((NT, TOP_K), F32), jax.ShapeDtypeStruct((1, N_EXPERTS), jnp.int32)],
        scratch_shapes=[pltpu.VMEM((1, N_EXPERTS), F32)],
        compiler_params=_params(),
        name="outproj",
    )(*mixes, *w_parts, x, gate, g2.reshape(1, D_MODEL), shift, scale, rw, rb.reshape(1, N_EXPERTS))


TM = 256
MOE_NBLK = NT * TOP_K // TM + N_EXPERTS
MOE_ROWS = MOE_NBLK * TM


def _moe_kernel(be_ref, nv_ref, x_ref, wgu_ref, bgu_ref, wd_ref, bd_ref, y_ref, wgu_bf, wd_bf):
    i = pl.program_id(0)
    n_valid = nv_ref[i]

    @pl.when(n_valid > 0)
    def _():
        changed = jnp.logical_or(i == 0, be_ref[i] != be_ref[jnp.maximum(i - 1, 0)])

        @pl.when(changed)
        def _():
            wgu_bf[...] = wgu_ref[...].astype(BF16)
            wd_bf[...] = wd_ref[...].astype(BF16)

        rows = lax.broadcasted_iota(jnp.int32, (TM, 1), 0)
        x = jnp.where(rows < n_valid, x_ref[...], 0.0).astype(BF16)
        gu = _dot(x, wgu_bf[...]) + bgu_ref[...]
        gate = jnp.minimum(gu[:, :D_FF], SWIGLU_LIMIT)
        up = jnp.clip(gu[:, D_FF:], -SWIGLU_LIMIT, SWIGLU_LIMIT)
        hdn = gate * _sigmoid(SWIGLU_ALPHA * gate) * (up + 1.0)
        y_ref[...] = _dot(hdn.astype(BF16), wd_bf[...]) + bd_ref[...]


def _moe_experts(layer, block_e, n_valid, xs, w_gu, b_gu, w_down, b_down):
    grid_spec = pltpu.PrefetchScalarGridSpec(
        num_scalar_prefetch=2,
        grid=(MOE_NBLK,),
        in_specs=[pl.BlockSpec((TM, D_MODEL), lambda i, be, nv: (i, 0)),
                  pl.BlockSpec((None, None, D_MODEL, 2 * D_FF), lambda i, be, nv: (layer, be[i], 0, 0)),
                  pl.BlockSpec((None, None, 1, 2 * D_FF), lambda i, be, nv: (layer, be[i], 0, 0)),
                  pl.BlockSpec((None, None, D_FF, D_MODEL), lambda i, be, nv: (layer, be[i], 0, 0)),
                  pl.BlockSpec((None, None, 1, D_MODEL), lambda i, be, nv: (layer, be[i], 0, 0))],
        out_specs=pl.BlockSpec((TM, D_MODEL), lambda i, be, nv: (i, 0)),
        scratch_shapes=[pltpu.VMEM((D_MODEL, 2 * D_FF), BF16), pltpu.VMEM((D_FF, D_MODEL), BF16)],
    )
    return pl.pallas_call(
        _moe_kernel,
        grid_spec=grid_spec,
        out_shape=jax.ShapeDtypeStruct((MOE_ROWS, D_MODEL), F32),
        compiler_params=_params(),
        name="moe_experts",
    )(block_e, n_valid, xs, w_gu, b_gu.reshape(DEPTH, N_EXPERTS, 1, -1), w_down,
      b_down.reshape(DEPTH, N_EXPERTS, 1, -1))


SC_WORKERS = 32
SC_WIN = 32


def _sc_mesh():
    return plsc.VectorSubcoreMesh(core_axis_name="core", subcore_axis_name="subcore")


def _sc_worker():
    return lax.axis_index("core") * (SC_WORKERS // 2) + lax.axis_index("subcore")


def _sc_scatter_rows(x, dest_t, n_rows):
    n, width = x.shape
    kk = dest_t.shape[0]
    per = n // SC_WORKERS
    n_win = per // SC_WIN
    assert per * SC_WORKERS == n and n_win * SC_WIN == per and n_win % 2 == 0

    @pl.kernel(out_type=jax.ShapeDtypeStruct((n_rows, width), x.dtype), mesh=_sc_mesh(),
               scratch_types=[pltpu.VMEM((kk, per), jnp.int32), pltpu.VMEM((SC_WIN, width), x.dtype),
                              pltpu.VMEM((SC_WIN, width), x.dtype), pltpu.SemaphoreType.DMA((4,))])
    def scatter(x_hbm, i_hbm, o_hbm, idx_v, buf0, buf1, sems):
        base = _sc_worker() * per
        pltpu.sync_copy(i_hbm.at[:, pl.ds(base, per)], idx_v)

        def get(j, buf, s):
            return pltpu.make_async_copy(x_hbm.at[pl.ds(base + j * SC_WIN, SC_WIN)], buf, sems.at[s])

        def put(j, q, buf, s):
            return pltpu.make_async_copy(buf, o_hbm.at[idx_v.at[q, pl.ds(j * SC_WIN, SC_WIN)]], sems.at[s])

        get(0, buf0, 0).start()

        @pl.loop(0, n_win, step=2)
        def _(j):
            get(j, buf0, 0).wait()

            @pl.when(j > 0)
            def _():
                for q in range(kk):
                    put(j - 1, q, buf1, 3).wait()

            get(j + 1, buf1, 1).start()
            for q in range(kk):
                put(j, q, buf0, 2).start()
            get(j + 1, buf1, 1).wait()
            for q in range(kk):
                put(j, q, buf0, 2).wait()

            @pl.when(j + 2 < n_win)
            def _():
                get(j + 2, buf0, 0).start()

            for q in range(kk):
                put(j + 1, q, buf1, 3).start()

        for q in range(kk):
            put(n_win - 1, q, buf1, 3).wait()

    return scatter(x, dest_t)


def _sc_gather_rows(y, idx):
    n = idx.shape[0]
    width = y.shape[1]
    per = n // SC_WORKERS
    n_win = per // SC_WIN
    assert per * SC_WORKERS == n and n_win * SC_WIN == per and n_win % 2 == 0

    @pl.kernel(out_type=jax.ShapeDtypeStruct((n, width), y.dtype), mesh=_sc_mesh(),
               scratch_types=[pltpu.VMEM((per,), jnp.int32), pltpu.VMEM((SC_WIN, width), y.dtype),
                              pltpu.VMEM((SC_WIN, width), y.dtype), pltpu.SemaphoreType.DMA((4,))])
    def gather(y_hbm, i_hbm, o_hbm, idx_v, buf0, buf1, sems):
        base = _sc_worker() * per
        pltpu.sync_copy(i_hbm.at[pl.ds(base, per)], idx_v)

        def get(j, buf, s):
            return pltpu.make_async_copy(y_hbm.at[idx_v.at[pl.ds(j * SC_WIN, SC_WIN)]], buf, sems.at[s])

        def put(j, buf, s):
            return pltpu.make_async_copy(buf, o_hbm.at[pl.ds(base + j * SC_WIN, SC_WIN)], sems.at[s])

        get(0, buf0, 0).start()

        @pl.loop(0, n_win, step=2)
        def _(j):
            get(j, buf0, 0).wait()

            @pl.when(j > 0)
            def _():
                put(j - 1, buf1, 3).wait()

            get(j + 1, buf1, 1).start()
            put(j, buf0, 2).start()
            get(j + 1, buf1, 1).wait()
            put(j, buf0, 2).wait()

            @pl.when(j + 2 < n_win)
            def _():
                get(j + 2, buf0, 0).start()

            put(j + 1, buf1, 3).start()

        put(n_win - 1, buf1, 3).wait()

    return gather(y, idx)


def _combine_kernel(x1_ref, yg_ref, gw_ref, gate_ref, o_ref):
    row = _mod_row(pl.program_id(0))
    acc = yg_ref[:, 0:D_MODEL] * gw_ref[:, 0:1]
    for kk in range(1, TOP_K):
        acc = acc + yg_ref[:, kk * D_MODEL:(kk + 1) * D_MODEL] * gw_ref[:, kk:kk + 1]
    o_ref[...] = x1_ref[...] + gate_ref[pl.ds(row, 1), :] * acc


def _combine(x1, yg, gw, gate):
    row_spec = lambda width: pl.BlockSpec((RB, width), lambda i: (i, 0))
    return pl.pallas_call(
        _combine_kernel,
        grid=(NBLK,),
        in_specs=[row_spec(D_MODEL), row_spec(TOP_K * D_MODEL), row_spec(TOP_K),
                  pl.BlockSpec((MOD_ROWS, D_MODEL), lambda i: (0, 0))],
        out_specs=row_spec(D_MODEL),
        out_shape=jax.ShapeDtypeStruct((NT, D_MODEL), F32),
        compiler_params=_params(),
        name="moe_combine",
    )(x1, yg, gw, gate)


def _final_norm_kernel(x_ref, g_ref, o_ref):
    x = x_ref[...]
    o_ref[...] = x * lax.rsqrt(jnp.mean(x * x, axis=-1, keepdims=True) + EPS) * g_ref[...]


def _final_norm(x, g):
    row_spec = pl.BlockSpec((RB, D_MODEL), lambda i: (i, 0))
    return pl.pallas_call(
        _final_norm_kernel,
        grid=(NBLK,),
        in_specs=[row_spec, pl.BlockSpec((1, D_MODEL), lambda i: (0, 0))],
        out_specs=row_spec,
        out_shape=jax.ShapeDtypeStruct((NT, D_MODEL), F32),
        compiler_params=_params(),
        name="final_norm",
    )(x, g.reshape(1, D_MODEL))


def _routing_plan(counts, idx4, rank4):
    counts = counts.reshape(N_EXPERTS)
    padded = (counts + TM - 1) // TM * TM
    pad_end = jnp.cumsum(padded)
    pad_start = pad_end - padded
    blk_row = jnp.arange(MOE_NBLK, dtype=jnp.int32) * TM
    n_used = pad_end[-1] // TM
    block_e = jnp.minimum(jnp.searchsorted(pad_end, blk_row, side='right'), N_EXPERTS - 1).astype(jnp.int32)
    n_valid = jnp.clip(counts[block_e] - (blk_row - pad_start[block_e]), 0, TM).astype(jnp.int32)
    used = jnp.arange(MOE_NBLK) < n_used
    block_e = jnp.where(used, block_e, block_e[n_used - 1])
    n_valid = jnp.where(used, n_valid, 0)
    dest4 = (pad_start[idx4] + rank4).astype(jnp.int32)
    return block_e, n_valid, dest4


def _rope_tables():
    rows = DEC_SEQ // GRID_W
    row = jnp.repeat(jnp.arange(rows, dtype=F32), GRID_W)
    col = jnp.tile(jnp.arange(GRID_W, dtype=F32), rows)
    n_f = RET_DK // 4
    freqs = ROPE_THETA ** (-jnp.arange(n_f, dtype=F32) / n_f)
    ang = jnp.concatenate([row[:, None] * freqs, col[:, None] * freqs], axis=-1)
    cos = jnp.repeat(jnp.cos(ang), 2, axis=-1)
    sin = jnp.repeat(jnp.sin(ang), 2, axis=-1) * jnp.tile(jnp.asarray([-1.0, 1.0], F32), RET_DK // 2)
    cos = jnp.concatenate([jnp.ones((RB, RET_DK), F32), cos], axis=0)
    sin = jnp.concatenate([jnp.zeros((RB, RET_DK), F32), sin], axis=0)
    return jnp.tile(cos, (1, 2)), jnp.tile(sin, (1, 2))


def kernel(x_prompt, x_sample, state_ret, state_gla, c, c_ctx, w_mod, b_mod, norm1_g, norm2_g, final_g, even_w_in, ret_decay, ret_gn, conv_w, conv_b, conv_ln_g, conv_ln_b, even_w_out, odd_w_in, gla_w_a2, gla_b_a2, gla_gn, odd_w_out, router_w, router_b, exp_w_gu, exp_b_gu, exp_w_down, exp_b_down):
    x = jnp.concatenate([x_prompt.reshape(NT_PROMPT, D_MODEL), x_sample.reshape(-1, D_MODEL)], axis=0)
    cvec = jnp.concatenate([c_ctx[None, :], c, jnp.zeros((MOD_ROWS - 1 - DEC_BATCH, D_MODEL), F32)], axis=0)
    mods = _modulation(cvec, w_mod, b_mod).reshape(DEPTH, MOD_ROWS, N_MOD, D_MODEL)
    cos_tab, sin_tab = _rope_tables()
    new_ret, new_gla = [], []
    for l in range(DEPTH):
        mod = [mods[l, :, j, :] for j in range(N_MOD)]
        if l % 2 == 0:
            e = l // 2
            qd, vd = RET_HEADS * RET_DK, RET_HEADS * RET_DV
            q, k, v, g, a, ga = _inproj(x, norm1_g[l], mod[0], mod[1], even_w_in[e].astype(BF16),
                                        (qd, qd, vd, vd, CONV_CH, CONV_CH))
            s0 = jnp.concatenate([jnp.zeros((BATCH,) + state_ret.shape[2:], F32), state_ret[:, e]], axis=0)
            o_f, s_f = _retention(ret_decay[e], q, k, v, cos_tab, sin_tab, s0, reverse=False)
            ret, s_b = _retention(ret_decay[e], q, k, v, cos_tab, sin_tab, s0, reverse=True,
                                  o_fwd=o_f, g=g, gn=ret_gn[e])
            u = _conv_module(a, ga, conv_w[e], conv_b[e], conv_ln_g[e], conv_ln_b[e])
            w_out = even_w_out[e].astype(BF16)
            mixes, w_parts = [ret, u], [w_out[:vd], w_out[vd:]]
            new_ret.append(jnp.stack([s_f[:BATCH], s_b[:BATCH]], axis=1))
        else:
            o = l // 2
            qd, vd = GLA_HEADS * GLA_DK, GLA_HEADS * GLA_DV
            q, k, v, r, alr = _inproj(x, norm1_g[l], mod[0], mod[1], odd_w_in[o].astype(BF16),
                                      (qd, qd, vd, vd, 2 * GLA_RANK))
            s0t = jnp.concatenate([jnp.zeros((BATCH,) + state_gla.shape[2:], F32), state_gla[:, o]], axis=0)
            s0t = jnp.swapaxes(s0t, -1, -2)
            zeros = jnp.zeros((GLA_RANK, qd), F32)
            wa_f = jnp.concatenate([gla_w_a2[o, 0], zeros], axis=0)
            wa_b = jnp.concatenate([zeros, gla_w_a2[o, 1]], axis=0)
            o_f, s_f = _gla(alr, q, k, v, wa_f, gla_b_a2[o, 0].reshape(1, qd), s0t, reverse=False)
            y, s_b = _gla(alr, q, k, v, wa_b, gla_b_a2[o, 1].reshape(1, qd), s0t, reverse=True,
                          o_fwd=o_f, r=r, gn=gla_gn[o])
            mixes, w_parts = [y], [odd_w_out[o].astype(BF16)]
            new_gla.append(jnp.swapaxes(jnp.stack([s_f[:BATCH], s_b[:BATCH]], axis=1), -1, -2))
        x1, h2, idx4, rank4, gates, counts = _outproj(mixes, w_parts, x, mod[2], norm2_g[l], mod[3], mod[4],
                                                      router_w[l], router_b[l])
        block_e, n_valid, dest4 = _routing_plan(counts, idx4, rank4)
        xs = _sc_scatter_rows(h2, dest4.T, MOE_ROWS)
        yb = _moe_experts(l, block_e, n_valid, xs, exp_w_gu, exp_b_gu, exp_w_down, exp_b_down)
        yg = _sc_gather_rows(yb, dest4.reshape(NT * TOP_K)).reshape(NT, TOP_K * D_MODEL)
        x = _combine(x1, yg, gates, mod[5])
    y = _final_norm(x, final_g)
    y_prompt = y[:NT_PROMPT].reshape(BATCH, SEQ, D_MODEL)
    y_sample = y[NT_PROMPT:].reshape(DEC_BATCH, DEC_SEQ, D_MODEL)
    return (y_prompt, y_sample, jnp.stack(new_ret, axis=1), jnp.stack(new_gla, axis=1))
```

```python
import functools

import jax
import jax.numpy as jnp
from jax import lax
from jax.experimental import pallas as pl
from jax.experimental.pallas import tpu as pltpu
from jax.experimental.pallas import tpu_sc as plsc

F32 = jnp.float32
BF16 = jnp.bfloat16

D_MODEL = 1024
BATCH = 16
SEQ = 256
DEPTH = 4
DEC_BATCH = 4
DEC_SEQ = 4096
GRID_W = 64
RET_HEADS = 4
RET_DK = 64
RET_DV = 128
RET_CHUNK = 128
CONV_CH = 512
CONV_WIDTH = 31
CONV_PAD = CONV_WIDTH // 2
GLA_HEADS = 4
GLA_DK = 128
GLA_DV = 256
GLA_RANK = 16
GLA_TAU = 16.0
GLA_CHUNK = 64
GLA_SUB = 16
N_EXPERTS = 32
TOP_K = 4
D_FF = 1024
SWIGLU_LIMIT = 7.0
SWIGLU_ALPHA = 1.702
MOE_BLOCK = 128
ROPE_THETA = 10000.0
EPS = 1e-6
N_MOD = 6

RB = 256
NT_PROMPT = BATCH * SEQ
NT = NT_PROMPT + DEC_BATCH * DEC_SEQ
NBLK = NT // RB
N_PROMPT_BLK = NT_PROMPT // RB
SAMPLE_BLK = DEC_SEQ // RB
NSEQ = BATCH + DEC_BATCH
MOD_ROWS = 8
HALO = 16
VMEM_LIMIT = 48 * 1024 * 1024

assert SEQ == RB and DEC_SEQ % RB == 0 and CONV_PAD < HALO


def _seq_of_block(i):
    return jnp.where(i < N_PROMPT_BLK, i, N_PROMPT_BLK + (i - N_PROMPT_BLK) // SAMPLE_BLK)


def _is_first_block(i):
    return jnp.logical_or(i < N_PROMPT_BLK, (i - N_PROMPT_BLK) % SAMPLE_BLK == 0)


def _is_last_block(i):
    return jnp.logical_or(i < N_PROMPT_BLK, (i - N_PROMPT_BLK) % SAMPLE_BLK == SAMPLE_BLK - 1)


def _mod_row(i):
    return jnp.where(i < N_PROMPT_BLK, 0, 1 + (i - N_PROMPT_BLK) // SAMPLE_BLK)


def _rope_block(i):
    return jnp.where(i < N_PROMPT_BLK, 0, 1 + (i - N_PROMPT_BLK) % SAMPLE_BLK)


def _dot(a, b):
    return jnp.dot(a, b, preferred_element_type=F32)


def _dot_nt(a, b):
    return lax.dot_general(a, b, (((1,), (1,)), ((), ())), preferred_element_type=F32)


def _dot_tn(a, b):
    return lax.dot_general(a, b, (((0,), (0,)), ((), ())), preferred_element_type=F32)


def _split2(a):
    hi = a.astype(BF16)
    lo = (a - hi.astype(F32)).astype(BF16)
    return hi, lo


def _dot_hi(a, b):
    a_hi, a_lo = _split2(a)
    b_hi, b_lo = _split2(b)
    return _dot(a_hi, b_hi) + (_dot(a_hi, b_lo) + _dot(a_lo, b_hi))


def _silu(x):
    return x * (1.0 / (1.0 + jnp.exp(-x)))


def _sigmoid(x):
    return 1.0 / (1.0 + jnp.exp(-x))


def _params(n_axes=1, vmem=VMEM_LIMIT):
    return pltpu.CompilerParams(dimension_semantics=("arbitrary",) * n_axes, vmem_limit_bytes=vmem)


MOD_TN = 1536


def _mod_kernel(c_ref, w_ref, b_ref, o_ref):
    s = _silu(c_ref[...]).astype(BF16)
    o_ref[...] = _dot(s, w_ref[...].astype(BF16)) + b_ref[...]


def _modulation(cvec, w_mod, b_mod):
    n = N_MOD * D_MODEL
    return pl.pallas_call(
        _mod_kernel,
        grid=(DEPTH, n // MOD_TN),
        in_specs=[pl.BlockSpec((MOD_ROWS, D_MODEL), lambda l, j: (0, 0)),
                  pl.BlockSpec((None, D_MODEL, MOD_TN), lambda l, j: (l, 0, j)),
                  pl.BlockSpec((None, 1, MOD_TN), lambda l, j: (l, 0, j))],
        out_specs=pl.BlockSpec((None, MOD_ROWS, MOD_TN), lambda l, j: (l, 0, j)),
        out_shape=jax.ShapeDtypeStruct((DEPTH, MOD_ROWS, n), F32),
        compiler_params=_params(2),
        name="modulation",
    )(cvec, w_mod, b_mod.reshape(DEPTH, 1, n))


def _norm_mod(x, g_ref, shift_ref, scale_ref, row):
    y = x * lax.rsqrt(jnp.mean(x * x, axis=-1, keepdims=True) + EPS) * g_ref[...]
    return y * (1.0 + scale_ref[pl.ds(row, 1), :]) + shift_ref[pl.ds(row, 1), :]


def _inproj_kernel(x_ref, g_ref, shift_ref, scale_ref, w_ref, *o_refs, widths):
    row = _mod_row(pl.program_id(0))
    hb = _norm_mod(x_ref[...], g_ref, shift_ref, scale_ref, row).astype(BF16)
    off = 0
    for o_ref, width in zip(o_refs, widths):
        o_ref[...] = _dot(hb, w_ref[:, off:off + width])
        off += width


def _inproj(x, g, shift, scale, w_bf, widths):
    n_in = w_bf.shape[1]
    row_spec = lambda width: pl.BlockSpec((RB, width), lambda i: (i, 0))
    full = lambda shape: pl.BlockSpec(shape, lambda i: (0,) * len(shape))
    return pl.pallas_call(
        functools.partial(_inproj_kernel, widths=widths),
        grid=(NBLK,),
        in_specs=[row_spec(D_MODEL), full((1, D_MODEL)), full((MOD_ROWS, D_MODEL)),
                  full((MOD_ROWS, D_MODEL)), full((D_MODEL, n_in))],
        out_specs=[row_spec(width) for width in widths],
        out_shape=[jax.ShapeDtypeStruct((NT, width), F32) for width in widths],
        compiler_params=_params(),
        name="inproj",
    )(x, g.reshape(1, D_MODEL), shift, scale, w_bf)


RC = RET_CHUNK
RET_PAIR = 2 * RET_DK


def _rope(x, cos, sin_signed):
    lane = lax.broadcasted_iota(jnp.int32, x.shape, 1)
    swapped = jnp.where(lane % 2 == 0, pltpu.roll(x, x.shape[1] - 1, 1), pltpu.roll(x, 1, 1))
    return x * cos + swapped * sin_signed


def _ret_kernel(decay_ref, q_ref, k_ref, v_ref, cos_ref, sin_ref, s0_ref, *rest, reverse):
    if reverse:
        of_ref, g_ref, gn_ref, o_ref, sfin_ref, st_ref, dm_ref, dq_ref, dk_ref, ds_ref = rest
    else:
        o_ref, sfin_ref, st_ref, dm_ref, dq_ref, dk_ref, ds_ref = rest
    step = pl.program_id(0)
    blk = NBLK - 1 - step if reverse else step
    direction = 1 if reverse else 0

    @pl.when(step == 0)
    def _():
        row = lax.broadcasted_iota(jnp.int32, (RC, RC), 0).astype(F32)
        col = lax.broadcasted_iota(jnp.int32, (RC, RC), 1).astype(F32)
        for h in range(RET_HEADS):
            lg = -jnp.exp(jnp.full((RC, RC), decay_ref[direction, h], F32))
            if reverse:
                diff = col - row
                mask = diff > 0
                q_pow = RC - row
                k_pow = row
            else:
                diff = row - col
                mask = diff >= 0
                q_pow = row + 1.0
                k_pow = RC - 1.0 - row
            dm_ref[h] = jnp.where(mask, jnp.exp(lg * jnp.where(mask, diff, 0.0)), 0.0)
            dq_ref[h] = jnp.exp(lg * q_pow)
            dk_ref[h] = jnp.exp(lg * k_pow)
            ds_ref[h] = jnp.exp(lg * RC)

    starts = _is_last_block(blk) if reverse else _is_first_block(blk)

    @pl.when(starts)
    def _():
        st_ref[...] = jnp.zeros_like(st_ref)
        for h in range(RET_HEADS):
            off = (h % 2) * RET_DK
            st_ref[h, off:off + RET_DK, :] = s0_ref[h]

    lane = lax.broadcasted_iota(jnp.int32, (1, RET_PAIR), 1)
    chunks = range(RB // RC)
    for c in (reversed(chunks) if reverse else chunks):
        rows = slice(c * RC, (c + 1) * RC)
        cos = cos_ref[rows, :]
        sin = sin_ref[rows, :]
        for p in range(RET_HEADS // 2):
            cols = slice(p * RET_PAIR, (p + 1) * RET_PAIR)
            q2 = _rope(q_ref[rows, cols], cos, sin)
            k2 = _rope(k_ref[rows, cols] * (RET_DK ** -0.5), cos, sin)
            for h in (2 * p, 2 * p + 1):
                head_mask = (lane // RET_DK == h % 2).astype(F32)
                vh = v_ref[rows, h * RET_DV:(h + 1) * RET_DV].astype(BF16)
                qm = (q2 * head_mask).astype(BF16)
                km = k2 * head_mask
                att = _dot_nt(qm, km.astype(BF16)) * dm_ref[h]
                o = _dot(att.astype(BF16), vh) + _dot(qm, st_ref[h].astype(BF16)) * dq_ref[h]
                st_ref[h] = st_ref[h] * ds_ref[h] + _dot_tn((km * dk_ref[h]).astype(BF16), vh)
                out_cols = slice(h * RET_DV, (h + 1) * RET_DV)
                if reverse:
                    o = o + of_ref[rows, out_cols]
                    o = o * lax.rsqrt(jnp.mean(o * o, axis=-1, keepdims=True) + EPS)
                    o = o * gn_ref[:, out_cols] * _silu(g_ref[rows, out_cols])
                    o_ref[rows, out_cols] = o.astype(o_ref.dtype)
                else:
                    o_ref[rows, out_cols] = o

    ends = _is_first_block(blk) if reverse else _is_last_block(blk)

    @pl.when(ends)
    def _():
        for h in range(RET_HEADS):
            off = (h % 2) * RET_DK
            sfin_ref[h] = st_ref[h, off:off + RET_DK, :]


def _retention(decay, q, k, v, cos_tab, sin_tab, s0, *, reverse, o_fwd=None, g=None, gn=None):
    bmap = (lambda j: NBLK - 1 - j) if reverse else (lambda j: j)
    direction = 1 if reverse else 0
    qd, vd = RET_HEADS * RET_DK, RET_HEADS * RET_DV
    row_spec = lambda width: pl.BlockSpec((RB, width), lambda j: (bmap(j), 0))
    state_spec = pl.BlockSpec((None, None, RET_HEADS, RET_DK, RET_DV),
                              lambda j: (_seq_of_block(bmap(j)), direction, 0, 0, 0))
    in_specs = [pl.BlockSpec(memory_space=pltpu.SMEM), row_spec(qd), row_spec(qd), row_spec(vd),
                pl.BlockSpec((RB, RET_PAIR), lambda j: (_rope_block(bmap(j)), 0)),
                pl.BlockSpec((RB, RET_PAIR), lambda j: (_rope_block(bmap(j)), 0)),
                state_spec]
    args = [decay, q, k, v, cos_tab, sin_tab, s0]
    if reverse:
        in_specs += [row_spec(vd), row_spec(vd), pl.BlockSpec((1, vd), lambda j: (0, 0))]
        args += [o_fwd, g, gn.reshape(1, vd)]
    tile = pltpu.VMEM((RET_HEADS, RC, RC), F32)
    return pl.pallas_call(
        functools.partial(_ret_kernel, reverse=reverse),
        grid=(NBLK,),
        in_specs=in_specs,
        out_specs=[row_spec(vd),
                   pl.BlockSpec((None, RET_HEADS, RET_DK, RET_DV),
                                lambda j: (_seq_of_block(bmap(j)), 0, 0, 0))],
        out_shape=[jax.ShapeDtypeStruct((NT, vd), BF16 if reverse else F32),
                   jax.ShapeDtypeStruct((NSEQ, RET_HEADS, RET_DK, RET_DV), F32)],
        scratch_shapes=[tile, tile, tile, tile, tile],
        compiler_params=_params(),
        name="retention_bwd" if reverse else "retention_fwd",
    )(*args)


CONV_RT = 32
CONV_CT = 128


def _conv_kernel(a_ref, ga_ref, ap_ref, gap_ref, an_ref, gan_ref, cw_ref, cb_ref, lng_ref, lnb_ref,
                 o_ref, u_ref, y_ref):
    blk = pl.program_id(0)
    keep_prev = jnp.where(_is_first_block(blk), 0.0, 1.0)
    keep_next = jnp.where(_is_last_block(blk), 0.0, 1.0)
    u_ref[0:HALO, :] = ap_ref[...] * _sigmoid(gap_ref[...]) * keep_prev
    u_ref[HALO:HALO + RB, :] = a_ref[...] * _sigmoid(ga_ref[...])
    u_ref[HALO + RB:HALO + RB + HALO, :] = an_ref[...] * _sigmoid(gan_ref[...]) * keep_next
    for ct in range(CONV_CH // CONV_CT):
        cols = slice(ct * CONV_CT, (ct + 1) * CONV_CT)
        for rt in range(RB // CONV_RT):
            base = HALO - CONV_PAD + rt * CONV_RT
            acc = jnp.zeros((CONV_RT, CONV_CT), F32)
            for w in range(CONV_WIDTH):
                acc = acc + u_ref[base + w:base + w + CONV_RT, cols] * cw_ref[w:w + 1, cols]
            y_ref[rt * CONV_RT:(rt + 1) * CONV_RT, cols] = acc + cb_ref[:, cols]
    y = y_ref[...]
    mu = jnp.mean(y, axis=-1, keepdims=True)
    var = jnp.mean(jnp.square(y - mu), axis=-1, keepdims=True)
    o_ref[...] = _silu((y - mu) * lax.rsqrt(var + EPS) * lng_ref[...] + lnb_ref[...]).astype(o_ref.dtype)


def _conv_module(a, ga, cw, cb, lng, lnb):
    per_blk = RB // HALO
    n_halo = NT // HALO
    row_spec = pl.BlockSpec((RB, CONV_CH), lambda i: (i, 0))
    prev_spec = pl.BlockSpec((HALO, CONV_CH), lambda i: (jnp.maximum(i * per_blk - 1, 0), 0))
    next_spec = pl.BlockSpec((HALO, CONV_CH), lambda i: (jnp.minimum((i + 1) * per_blk, n_halo - 1), 0))
    vec = pl.BlockSpec((1, CONV_CH), lambda i: (0, 0))
    return pl.pallas_call(
        _conv_kernel,
        grid=(NBLK,),
        in_specs=[row_spec, row_spec, prev_spec, prev_spec, next_spec, next_spec,
                  pl.BlockSpec((CONV_WIDTH, CONV_CH), lambda i: (0, 0)), vec, vec, vec],
        out_specs=row_spec,
        out_shape=jax.ShapeDtypeStruct((NT, CONV_CH), BF16),
        scratch_shapes=[pltpu.VMEM((RB + 2 * HALO, CONV_CH), F32), pltpu.VMEM((RB, CONV_CH), F32)],
        compiler_params=_params(),
        name="conv_module",
    )(a, ga, a, ga, a, ga, cw, cb.reshape(1, -1), lng.reshape(1, -1), lnb.reshape(1, -1))


GC = GLA_CHUNK
GLA_NSUB = GC // GLA_SUB


def _split3(a):
    p1 = a.astype(BF16)
    r1 = a - p1.astype(F32)
    p2 = r1.astype(BF16)
    p3 = (r1 - p2.astype(F32)).astype(BF16)
    return p1, p2, p3


def _gla_kernel(alr_ref, q_ref, k_ref, v_ref, wa_ref, ba_ref, s0_ref, *rest, reverse):
    if reverse:
        of_ref, r_ref, gn_ref, o_ref, sfin_ref, st_ref, ob_ref = rest
    else:
        o_ref, sfin_ref, st_ref = rest
        ob_ref = o_ref
    step = pl.program_id(0)
    blk = NBLK - 1 - step if reverse else step
    starts = _is_last_block(blk) if reverse else _is_first_block(blk)

    @pl.when(starts)
    def _():
        st_ref[...] = s0_ref[...]

    row = lax.broadcasted_iota(jnp.int32, (GC, GC), 0)
    col = lax.broadcasted_iota(jnp.int32, (GC, GC), 1)
    tri = jnp.where(col >= row if reverse else col <= row, 1.0, 0.0).astype(BF16)

    def chunk(t, carry):
        c = (RB // GC - 1 - t) if reverse else t
        rows = pl.ds(pl.multiple_of(c * GC, GC), GC)
        z = _dot_hi(alr_ref[rows, :], wa_ref[...]) + ba_ref[...]
        log_a = (jnp.minimum(z, 0.0) - jnp.log(1.0 + jnp.exp(-jnp.abs(z)))) * (1.0 / GLA_TAU)
        g1, g2, g3 = _split3(log_a)
        b_all = _dot(tri, g1) + (_dot(tri, g2) + _dot(tri, g3))
        for h in range(GLA_HEADS):
            kcols = slice(h * GLA_DK, (h + 1) * GLA_DK)
            vcols = slice(h * GLA_DV, (h + 1) * GLA_DV)
            b = b_all[:, kcols]
            qh = q_ref[rows, kcols] * (GLA_DK ** -0.5)
            kh = k_ref[rows, kcols]
            vh = v_ref[rows, vcols].astype(BF16)
            st = st_ref[h]
            edge = b[0:1, :] if reverse else b[GC - 1:GC, :]
            o = _dot_nt((qh * jnp.exp(b)).astype(BF16), st.astype(BF16))
            parts = []
            for s in range(GLA_NSUB):
                lo, hi = s * GLA_SUB, (s + 1) * GLA_SUB
                if reverse:
                    ref_b = b[hi:hi + 1, :] if s < GLA_NSUB - 1 else jnp.zeros((1, GLA_DK), F32)
                    k_lo, k_hi = lo, GC
                else:
                    ref_b = b[lo - 1:lo, :] if s > 0 else jnp.zeros((1, GLA_DK), F32)
                    k_lo, k_hi = 0, hi
                qs = (qh[lo:hi] * jnp.exp(b[lo:hi] - ref_b)).astype(BF16)
                ks = (kh[k_lo:k_hi] * jnp.exp(ref_b - b[k_lo:k_hi])).astype(BF16)
                att = _dot_nt(qs, ks)
                ri = lax.broadcasted_iota(jnp.int32, att.shape, 0) + lo
                ci = lax.broadcasted_iota(jnp.int32, att.shape, 1) + k_lo
                att = jnp.where(ci > ri if reverse else ci <= ri, att, 0.0)
                parts.append(_dot(att.astype(BF16), vh[k_lo:k_hi]))
            o = o + jnp.concatenate(parts, axis=0)
            ke = (kh * jnp.exp(edge - b)).astype(BF16)
            st_ref[h] = st * jnp.exp(edge) + _dot_tn(vh, ke)
            ob_ref[rows, vcols] = o
        return carry

    lax.fori_loop(0, RB // GC, chunk, 0)

    if reverse:
        for h in range(GLA_HEADS):
            vcols = slice(h * GLA_DV, (h + 1) * GLA_DV)
            o = ob_ref[:, vcols] + of_ref[:, vcols]
            o = o * lax.rsqrt(jnp.mean(o * o, axis=-1, keepdims=True) + EPS)
            o_ref[:, vcols] = (o * gn_ref[:, vcols] * _silu(r_ref[:, vcols])).astype(o_ref.dtype)

    ends = _is_first_block(blk) if reverse else _is_last_block(blk)

    @pl.when(ends)
    def _():
        sfin_ref[...] = st_ref[...]


def _gla(alr, q, k, v, wa, ba, s0t, *, reverse, o_fwd=None, r=None, gn=None):
    bmap = (lambda j: NBLK - 1 - j) if reverse else (lambda j: j)
    direction = 1 if reverse else 0
    qd, vd = GLA_HEADS * GLA_DK, GLA_HEADS * GLA_DV
    row_spec = lambda width: pl.BlockSpec((RB, width), lambda j: (bmap(j), 0))
    state_spec = pl.BlockSpec((None, None, GLA_HEADS, GLA_DV, GLA_DK),
                              lambda j: (_seq_of_block(bmap(j)), direction, 0, 0, 0))
    in_specs = [row_spec(2 * GLA_RANK), row_spec(qd), row_spec(qd), row_spec(vd),
                pl.BlockSpec((2 * GLA_RANK, qd), lambda j: (0, 0)),
                pl.BlockSpec((1, qd), lambda j: (0, 0)), state_spec]
    args = [alr, q, k, v, wa, ba, s0t]
    scratch = [pltpu.VMEM((GLA_HEADS, GLA_DV, GLA_DK), F32)]
    if reverse:
        in_specs += [row_spec(vd), row_spec(vd), pl.BlockSpec((1, vd), lambda j: (0, 0))]
        args += [o_fwd, r, gn.reshape(1, vd)]
        scratch += [pltpu.VMEM((RB, vd), F32)]
    return pl.pallas_call(
        functools.partial(_gla_kernel, reverse=reverse),
        grid=(NBLK,),
        in_specs=in_specs,
        out_specs=[row_spec(vd),
                   pl.BlockSpec((None, GLA_HEADS, GLA_DV, GLA_DK),
                                lambda j: (_seq_of_block(bmap(j)), 0, 0, 0))],
        out_shape=[jax.ShapeDtypeStruct((NT, vd), BF16 if reverse else F32),
                   jax.ShapeDtypeStruct((NSEQ, GLA_HEADS, GLA_DV, GLA_DK), F32)],
        scratch_shapes=scratch,
        compiler_params=_params(),
        name="gla_bwd" if reverse else "gla_fwd",
    )(*args)


def _outproj_kernel(*refs, n_mix):
    mix_refs = refs[:n_mix]
    w_refs = refs[n_mix:2 * n_mix]
    (x_ref, gate_ref, g2_ref, shift_ref, scale_ref, rw_ref, rb_ref,
     x1_ref, h2_ref, idx_ref, rank_ref, gw_ref, cnt_ref, carry_ref) = refs[2 * n_mix:]
    step = pl.program_id(0)
    row = _mod_row(step)
    m = _dot(mix_refs[0][...], w_refs[0][...])
    for mix_ref, w_ref in zip(mix_refs[1:], w_refs[1:]):
        m = m + _dot(mix_ref[...], w_ref[...])
    x1 = x_ref[...] + gate_ref[pl.ds(row, 1), :] * m
    x1_ref[...] = x1
    h2 = _norm_mod(x1, g2_ref, shift_ref, scale_ref, row)
    h2_ref[...] = h2
    logits = _dot_hi(h2, rw_ref[...]) + rb_ref[...]

    @pl.when(step == 0)
    def _():
        carry_ref[...] = jnp.zeros_like(carry_ref)

    lane = lax.broadcasted_iota(jnp.int32, logits.shape, 1)
    work = logits
    onehots, top_vals = [], []
    for kk in range(TOP_K):
        top = jnp.max(work, axis=-1, keepdims=True)
        first = jnp.min(jnp.where(work == top, lane, N_EXPERTS), axis=-1, keepdims=True)
        onehot = lane == first
        idx_ref[:, kk:kk + 1] = first
        onehots.append(onehot)
        top_vals.append(top)
        work = jnp.where(onehot, -jnp.inf, work)
    exps = [jnp.exp(v - top_vals[0]) for v in top_vals]
    denom = exps[0]
    for e in exps[1:]:
        denom = denom + e
    for kk in range(TOP_K):
        gw_ref[:, kk:kk + 1] = exps[kk] / denom

    sel = jnp.zeros(logits.shape, F32)
    for onehot in onehots:
        sel = sel + jnp.where(onehot, 1.0, 0.0)
    r_i = lax.broadcasted_iota(jnp.int32, (RB, RB), 0)
    c_i = lax.broadcasted_iota(jnp.int32, (RB, RB), 1)
    earlier = jnp.where(c_i < r_i, 1.0, 0.0).astype(BF16)
    pos = _dot(earlier, sel.astype(BF16)) + carry_ref[...]
    for kk in range(TOP_K):
        rank = jnp.sum(jnp.where(onehots[kk], pos, 0.0), axis=-1, keepdims=True)
        rank_ref[:, kk:kk + 1] = rank.astype(jnp.int32)
    carry = carry_ref[...] + jnp.sum(sel, axis=0, keepdims=True)
    carry_ref[...] = carry
    cnt_ref[...] = carry.astype(jnp.int32)


def _outproj(mixes, w_parts, x, gate, g2, shift, scale, rw, rb):
    n_mix = len(mixes)
    row_spec = lambda width: pl.BlockSpec((RB, width), lambda i: (i, 0))
    full = lambda shape: pl.BlockSpec(shape, lambda i: (0,) * len(shape))
    mod_spec = full((MOD_ROWS, D_MODEL))
    return pl.pallas_call(
        functools.partial(_outproj_kernel, n_mix=n_mix),
        grid=(NBLK,),
        in_specs=[row_spec(m.shape[1]) for m in mixes] + [full(w.shape) for w in w_parts]
        + [row_spec(D_MODEL), mod_spec, full((1, D_MODEL)), mod_spec, mod_spec,
           full((D_MODEL, N_EXPERTS)), full((1, N_EXPERTS))],
        out_specs=[row_spec(D_MODEL), row_spec(D_MODEL), row_spec(TOP_K), row_spec(TOP_K), row_spec(TOP_K),
                   full((1, N_EXPERTS))],
        out_shape=[jax.ShapeDtypeStruct((NT, D_MODEL), F32), jax.ShapeDtypeStruct((NT, D_MODEL), F32),
                   jax.ShapeDtypeStruct((NT, TOP_K), jnp.int32), jax.ShapeDtypeStruct((NT, TOP_K), jnp.int32),
                   jax.ShapeDtypeStruct((NT, TOP_K), F32), jax.ShapeDtypeStruct((1, N_EXPERTS), jnp.int32)],
        scratch_shapes=[pltpu.VMEM((1, N_EXPERTS), F32)],
        compiler_params=_params(),
        name="outproj",
    )(*mixes, *w_parts, x, gate, g2.reshape(1, D_MODEL), shift, scale, rw, rb.reshape(1, N_EXPERTS))


TM = 256
MOE_NBLK = NT * TOP_K // TM + N_EXPERTS
MOE_ROWS = MOE_NBLK * TM


def _moe_kernel(be_ref, nv_ref, x_ref, wgu_ref, bgu_ref, wd_ref, bd_ref, y_ref, wgu_bf, wd_bf):
    i = pl.program_id(0)
    n_valid = nv_ref[i]

    @pl.when(n_valid > 0)
    def _():
        changed = jnp.logical_or(i == 0, be_ref[i] != be_ref[jnp.maximum(i - 1, 0)])

        @pl.when(changed)
        def _():
            wgu_bf[...] = wgu_ref[...].astype(BF16)
            wd_bf[...] = wd_ref[...].astype(BF16)

        rows = lax.broadcasted_iota(jnp.int32, (TM, 1), 0)
        x = jnp.where(rows < n_valid, x_ref[...], 0.0).astype(BF16)
        gu = _dot(x, wgu_bf[...]) + bgu_ref[...]
        gate = jnp.minimum(gu[:, :D_FF], SWIGLU_LIMIT)
        up = jnp.clip(gu[:, D_FF:], -SWIGLU_LIMIT, SWIGLU_LIMIT)
        hdn = gate * _sigmoid(SWIGLU_ALPHA * gate) * (up + 1.0)
        y_ref[...] = _dot(hdn.astype(BF16), wd_bf[...]) + bd_ref[...]


def _moe_experts(layer, block_e, n_valid, xs, w_gu, b_gu, w_down, b_down):
    grid_spec = pltpu.PrefetchScalarGridSpec(
        num_scalar_prefetch=2,
        grid=(MOE_NBLK,),
        in_specs=[pl.BlockSpec((TM, D_MODEL), lambda i, be, nv: (i, 0)),
                  pl.BlockSpec((None, None, D_MODEL, 2 * D_FF), lambda i, be, nv: (layer, be[i], 0, 0)),
                  pl.BlockSpec((None, None, 1, 2 * D_FF), lambda i, be, nv: (layer, be[i], 0, 0)),
                  pl.BlockSpec((None, None, D_FF, D_MODEL), lambda i, be, nv: (layer, be[i], 0, 0)),
                  pl.BlockSpec((None, None, 1, D_MODEL), lambda i, be, nv: (layer, be[i], 0, 0))],
        out_specs=pl.BlockSpec((TM, D_MODEL), lambda i, be, nv: (i, 0)),
        scratch_shapes=[pltpu.VMEM((D_MODEL, 2 * D_FF), BF16), pltpu.VMEM((D_FF, D_MODEL), BF16)],
    )
    return pl.pallas_call(
        _moe_kernel,
        grid_spec=grid_spec,
        out_shape=jax.ShapeDtypeStruct((MOE_ROWS, D_MODEL), F32),
        compiler_params=_params(),
        name="moe_experts",
    )(block_e, n_valid, xs, w_gu, b_gu.reshape(DEPTH, N_EXPERTS, 1, -1), w_down,
      b_down.reshape(DEPTH, N_EXPERTS, 1, -1))


SC_WORKERS = 32
SC_WIN = 32


def _sc_mesh():
    return plsc.VectorSubcoreMesh(core_axis_name="core", subcore_axis_name="subcore")


def _sc_worker():
    return lax.axis_index("core") * (SC_WORKERS // 2) + lax.axis_index("subcore")


def _sc_scatter_rows(x, dest_t, n_rows):
    n, width = x.shape
    kk = dest_t.shape[0]
    per = n // SC_WORKERS
    n_win = per // SC_WIN
    assert per * SC_WORKERS == n and n_win * SC_WIN == per and n_win % 2 == 0

    @pl.kernel(out_type=jax.ShapeDtypeStruct((n_rows, width), x.dtype), mesh=_sc_mesh(),
               scratch_types=[pltpu.VMEM((kk, per), jnp.int32), pltpu.VMEM((SC_WIN, width), x.dtype),
                              pltpu.VMEM((SC_WIN, width), x.dtype), pltpu.SemaphoreType.DMA((4,))])
    def scatter(x_hbm, i_hbm, o_hbm, idx_v, buf0, buf1, sems):
        base = _sc_worker() * per
        pltpu.sync_copy(i_hbm.at[:, pl.ds(base, per)], idx_v)

        def get(j, buf, s):
            return pltpu.make_async_copy(x_hbm.at[pl.ds(base + j * SC_WIN, SC_WIN)], buf, sems.at[s])

        def put(j, q, buf, s):
            return pltpu.make_async_copy(buf, o_hbm.at[idx_v.at[q, pl.ds(j * SC_WIN, SC_WIN)]], sems.at[s])

        get(0, buf0, 0).start()

        @pl.loop(0, n_win, step=2)
        def _(j):
            get(j, buf0, 0).wait()

            @pl.when(j > 0)
            def _():
                for q in range(kk):
                    put(j - 1, q, buf1, 3).wait()

            get(j + 1, buf1, 1).start()
            for q in range(kk):
                put(j, q, buf0, 2).start()
            get(j + 1, buf1, 1).wait()
            for q in range(kk):
                put(j, q, buf0, 2).wait()

            @pl.when(j + 2 < n_win)
            def _():
                get(j + 2, buf0, 0).start()

            for q in range(kk):
                put(j + 1, q, buf1, 3).start()

        for q in range(kk):
            put(n_win - 1, q, buf1, 3).wait()

    return scatter(x, dest_t)


def _sc_gather_rows(y, idx):
    n = idx.shape[0]
    width = y.shape[1]
    per = n // SC_WORKERS
    n_win = per // SC_WIN
    assert per * SC_WORKERS == n and n_win * SC_WIN == per and n_win % 2 == 0

    @pl.kernel(out_type=jax.ShapeDtypeStruct((n, width), y.dtype), mesh=_sc_mesh(),
               scratch_types=[pltpu.VMEM((per,), jnp.int32), pltpu.VMEM((SC_WIN, width), y.dtype),
                              pltpu.VMEM((SC_WIN, width), y.dtype), pltpu.SemaphoreType.DMA((4,))])
    def gather(y_hbm, i_hbm, o_hbm, idx_v, buf0, buf1, sems):
        base = _sc_worker() * per
        pltpu.sync_copy(i_hbm.at[pl.ds(base, per)], idx_v)

        def get(j, buf, s):
            return pltpu.make_async_copy(y_hbm.at[idx_v.at[pl.ds(j * SC_WIN, SC_WIN)]], buf, sems.at[s])

        def put(j, buf, s):
            return pltpu.make_async_copy(buf, o_hbm.at[pl.ds(base + j * SC_WIN, SC_WIN)], sems.at[s])

        get(0, buf0, 0).start()

        @pl.loop(0, n_win, step=2)
        def _(j):
            get(j, buf0, 0).wait()

            @pl.when(j > 0)
            def _():
                put(j - 1, buf1, 3).wait()

            get(j + 1, buf1, 1).start()
            put(j, buf0, 2).start()
            get(j + 1, buf1, 1).wait()
            put(j, buf0, 2).wait()

            @pl.when(j + 2 < n_win)
            def _():
                get(j + 2, buf0, 0).start()

            put(j + 1, buf1, 3).start()

        put(n_win - 1, buf1, 3).wait()

    return gather(y, idx)


def _combine_kernel(x1_ref, *rest):
    yg_refs = rest[:TOP_K]
    gw_ref, gate_ref, o_ref = rest[TOP_K:]
    row = _mod_row(pl.program_id(0))
    acc = yg_refs[0][...] * gw_ref[:, 0:1]
    for kk in range(1, TOP_K):
        acc = acc + yg_refs[kk][...] * gw_ref[:, kk:kk + 1]
    o_ref[...] = x1_ref[...] + gate_ref[pl.ds(row, 1), :] * acc


def _combine(x1, yg, gw, gate):
    row_spec = lambda width: pl.BlockSpec((RB, width), lambda i: (i, 0))
    slot_spec = lambda k: pl.BlockSpec((RB, D_MODEL), lambda i: (k * NBLK + i, 0))
    return pl.pallas_call(
        _combine_kernel,
        grid=(NBLK,),
        in_specs=[row_spec(D_MODEL)] + [slot_spec(k) for k in range(TOP_K)]
        + [row_spec(TOP_K), pl.BlockSpec((MOD_ROWS, D_MODEL), lambda i: (0, 0))],
        out_specs=row_spec(D_MODEL),
        out_shape=jax.ShapeDtypeStruct((NT, D_MODEL), F32),
        compiler_params=_params(),
        name="moe_combine",
    )(x1, *([yg] * TOP_K), gw, gate)


def _final_norm_kernel(x_ref, g_ref, o_ref):
    x = x_ref[...]
    o_ref[...] = x * lax.rsqrt(jnp.mean(x * x, axis=-1, keepdims=True) + EPS) * g_ref[...]


def _final_norm(x, g):
    row_spec = pl.BlockSpec((RB, D_MODEL), lambda i: (i, 0))
    return pl.pallas_call(
        _final_norm_kernel,
        grid=(NBLK,),
        in_specs=[row_spec, pl.BlockSpec((1, D_MODEL), lambda i: (0, 0))],
        out_specs=row_spec,
        out_shape=jax.ShapeDtypeStruct((NT, D_MODEL), F32),
        compiler_params=_params(),
        name="final_norm",
    )(x, g.reshape(1, D_MODEL))


def _routing_plan(counts, idx4, rank4):
    counts = counts.reshape(N_EXPERTS)
    padded = (counts + TM - 1) // TM * TM
    pad_end = jnp.cumsum(padded)
    pad_start = pad_end - padded
    blk_row = jnp.arange(MOE_NBLK, dtype=jnp.int32) * TM
    n_used = pad_end[-1] // TM
    block_e = jnp.sum((pad_end[None, :] <= blk_row[:, None]).astype(jnp.int32), axis=1)
    block_e = jnp.minimum(block_e, N_EXPERTS - 1)
    n_valid = jnp.clip(counts[block_e] - (blk_row - pad_start[block_e]), 0, TM).astype(jnp.int32)
    used = jnp.arange(MOE_NBLK) < n_used
    block_e = jnp.where(used, block_e, block_e[n_used - 1])
    n_valid = jnp.where(used, n_valid, 0)
    dest4 = (pad_start[idx4] + rank4).astype(jnp.int32)
    return block_e, n_valid, dest4


def _rope_tables():
    rows = DEC_SEQ // GRID_W
    row = jnp.repeat(jnp.arange(rows, dtype=F32), GRID_W)
    col = jnp.tile(jnp.arange(GRID_W, dtype=F32), rows)
    n_f = RET_DK // 4
    freqs = ROPE_THETA ** (-jnp.arange(n_f, dtype=F32) / n_f)
    ang = jnp.concatenate([row[:, None] * freqs, col[:, None] * freqs], axis=-1)
    cos = jnp.repeat(jnp.cos(ang), 2, axis=-1)
    sin = jnp.repeat(jnp.sin(ang), 2, axis=-1) * jnp.tile(jnp.asarray([-1.0, 1.0], F32), RET_DK // 2)
    cos = jnp.concatenate([jnp.ones((RB, RET_DK), F32), cos], axis=0)
    sin = jnp.concatenate([jnp.zeros((RB, RET_DK), F32), sin], axis=0)
    return jnp.tile(cos, (1, 2)), jnp.tile(sin, (1, 2))


def kernel(x_prompt, x_sample, state_ret, state_gla, c, c_ctx, w_mod, b_mod, norm1_g, norm2_g, final_g, even_w_in, ret_decay, ret_gn, conv_w, conv_b, conv_ln_g, conv_ln_b, even_w_out, odd_w_in, gla_w_a2, gla_b_a2, gla_gn, odd_w_out, router_w, router_b, exp_w_gu, exp_b_gu, exp_w_down, exp_b_down):
    x = jnp.concatenate([x_prompt.reshape(NT_PROMPT, D_MODEL), x_sample.reshape(-1, D_MODEL)], axis=0)
    cvec = jnp.concatenate([c_ctx[None, :], c, jnp.zeros((MOD_ROWS - 1 - DEC_BATCH, D_MODEL), F32)], axis=0)
    mods = _modulation(cvec, w_mod, b_mod).reshape(DEPTH, MOD_ROWS, N_MOD, D_MODEL)
    cos_tab, sin_tab = _rope_tables()
    new_ret, new_gla = [], []
    for l in range(DEPTH):
        mod = [mods[l, :, j, :] for j in range(N_MOD)]
        if l % 2 == 0:
            e = l // 2
            qd, vd = RET_HEADS * RET_DK, RET_HEADS * RET_DV
            q, k, v, g, a, ga = _inproj(x, norm1_g[l], mod[0], mod[1], even_w_in[e].astype(BF16),
                                        (qd, qd, vd, vd, CONV_CH, CONV_CH))
            s0 = jnp.concatenate([jnp.zeros((BATCH,) + state_ret.shape[2:], F32), state_ret[:, e]], axis=0)
            o_f, s_f = _retention(ret_decay[e], q, k, v, cos_tab, sin_tab, s0, reverse=False)
            ret, s_b = _retention(ret_decay[e], q, k, v, cos_tab, sin_tab, s0, reverse=True,
                                  o_fwd=o_f, g=g, gn=ret_gn[e])
            u = _conv_module(a, ga, conv_w[e], conv_b[e], conv_ln_g[e], conv_ln_b[e])
            w_out = even_w_out[e].astype(BF16)
            mixes, w_parts = [ret, u], [w_out[:vd], w_out[vd:]]
            new_ret.append(jnp.stack([s_f[:BATCH], s_b[:BATCH]], axis=1))
        else:
            o = l // 2
            qd, vd = GLA_HEADS * GLA_DK, GLA_HEADS * GLA_DV
            q, k, v, r, alr = _inproj(x, norm1_g[l], mod[0], mod[1], odd_w_in[o].astype(BF16),
                                      (qd, qd, vd, vd, 2 * GLA_RANK))
            s0t = jnp.concatenate([jnp.zeros((BATCH,) + state_gla.shape[2:], F32), state_gla[:, o]], axis=0)
            s0t = jnp.swapaxes(s0t, -1, -2)
            zeros = jnp.zeros((GLA_RANK, qd), F32)
            wa_f = jnp.concatenate([gla_w_a2[o, 0], zeros], axis=0)
            wa_b = jnp.concatenate([zeros, gla_w_a2[o, 1]], axis=0)
            o_f, s_f = _gla(alr, q, k, v, wa_f, gla_b_a2[o, 0].reshape(1, qd), s0t, reverse=False)
            y, s_b = _gla(alr, q, k, v, wa_b, gla_b_a2[o, 1].reshape(1, qd), s0t, reverse=True,
                          o_fwd=o_f, r=r, gn=gla_gn[o])
            mixes, w_parts = [y], [odd_w_out[o].astype(BF16)]
            new_gla.append(jnp.swapaxes(jnp.stack([s_f[:BATCH], s_b[:BATCH]], axis=1), -1, -2))
        x1, h2, idx4, rank4, gates, counts = _outproj(mixes, w_parts, x, mod[2], norm2_g[l], mod[3], mod[4],
                                                      router_w[l], router_b[l])
        block_e, n_valid, dest4 = _routing_plan(counts, idx4, rank4)
        dest_t = dest4.T
        xs = _sc_scatter_rows(h2, dest_t, MOE_ROWS)
        yb = _moe_experts(l, block_e, n_valid, xs, exp_w_gu, exp_b_gu, exp_w_down, exp_b_down)
        yg = _sc_gather_rows(yb, dest_t.reshape(TOP_K * NT))
        x = _combine(x1, yg, gates, mod[5])
    y = _final_norm(x, final_g)
    y_prompt = y[:NT_PROMPT].reshape(BATCH, SEQ, D_MODEL)
    y_sample = y[NT_PROMPT:].reshape(DEC_BATCH, DEC_SEQ, D_MODEL)
    return (y_prompt, y_sample, jnp.stack(new_ret, axis=1), jnp.stack(new_gla, axis=1))
```

```python
import functools

import jax
import jax.numpy as jnp
from jax import lax
from jax.experimental import pallas as pl
from jax.experimental.pallas import tpu as pltpu
from jax.experimental.pallas import tpu_sc as plsc

F32 = jnp.float32
BF16 = jnp.bfloat16

D_MODEL = 1024
BATCH = 16
SEQ = 256
DEPTH = 4
DEC_BATCH = 4
DEC_SEQ = 4096
GRID_W = 64
RET_HEADS = 4
RET_DK = 64
RET_DV = 128
RET_CHUNK = 128
CONV_CH = 512
CONV_WIDTH = 31
CONV_PAD = CONV_WIDTH // 2
GLA_HEADS = 4
GLA_DK = 128
GLA_DV = 256
GLA_RANK = 16
GLA_TAU = 16.0
GLA_CHUNK = 64
GLA_SUB = 16
N_EXPERTS = 32
TOP_K = 4
D_FF = 1024
SWIGLU_LIMIT = 7.0
SWIGLU_ALPHA = 1.702
MOE_BLOCK = 128
ROPE_THETA = 10000.0
EPS = 1e-6
N_MOD = 6

RB = 256
NT_PROMPT = BATCH * SEQ
NT = NT_PROMPT + DEC_BATCH * DEC_SEQ
NBLK = NT // RB
N_PROMPT_BLK = NT_PROMPT // RB
SAMPLE_BLK = DEC_SEQ // RB
NSEQ = BATCH + DEC_BATCH
MOD_ROWS = 8
HALO = 16
VMEM_LIMIT = 48 * 1024 * 1024

assert SEQ == RB and DEC_SEQ % RB == 0 and CONV_PAD < HALO


def _seq_of_block(i):
    return jnp.where(i < N_PROMPT_BLK, i, N_PROMPT_BLK + (i - N_PROMPT_BLK) // SAMPLE_BLK)


def _is_first_block(i):
    return jnp.logical_or(i < N_PROMPT_BLK, (i - N_PROMPT_BLK) % SAMPLE_BLK == 0)


def _is_last_block(i):
    return jnp.logical_or(i < N_PROMPT_BLK, (i - N_PROMPT_BLK) % SAMPLE_BLK == SAMPLE_BLK - 1)


def _mod_row(i):
    return jnp.where(i < N_PROMPT_BLK, 0, 1 + (i - N_PROMPT_BLK) // SAMPLE_BLK)


def _rope_block(i):
    return jnp.where(i < N_PROMPT_BLK, 0, 1 + (i - N_PROMPT_BLK) % SAMPLE_BLK)


def _dot(a, b):
    return jnp.dot(a, b, preferred_element_type=F32)


def _dot_nt(a, b):
    return lax.dot_general(a, b, (((1,), (1,)), ((), ())), preferred_element_type=F32)


def _dot_tn(a, b):
    return lax.dot_general(a, b, (((0,), (0,)), ((), ())), preferred_element_type=F32)


def _split2(a):
    hi = a.astype(BF16)
    lo = (a - hi.astype(F32)).astype(BF16)
    return hi, lo


def _dot_hi(a, b):
    a_hi, a_lo = _split2(a)
    b_hi, b_lo = _split2(b)
    return _dot(a_hi, b_hi) + (_dot(a_hi, b_lo) + _dot(a_lo, b_hi))


def _silu(x):
    return x * (1.0 / (1.0 + jnp.exp(-x)))


def _sigmoid(x):
    return 1.0 / (1.0 + jnp.exp(-x))


def _params(n_axes=1, vmem=VMEM_LIMIT):
    return pltpu.CompilerParams(dimension_semantics=("arbitrary",) * n_axes, vmem_limit_bytes=vmem)


MOD_TN = 1536


def _mod_kernel(c_ref, w_ref, b_ref, o_ref):
    s = _silu(c_ref[...]).astype(BF16)
    o_ref[...] = _dot(s, w_ref[...].astype(BF16)) + b_ref[...]


def _modulation(cvec, w_mod, b_mod):
    n = N_MOD * D_MODEL
    return pl.pallas_call(
        _mod_kernel,
        grid=(DEPTH, n // MOD_TN),
        in_specs=[pl.BlockSpec((MOD_ROWS, D_MODEL), lambda l, j: (0, 0)),
                  pl.BlockSpec((None, D_MODEL, MOD_TN), lambda l, j: (l, 0, j)),
                  pl.BlockSpec((None, 1, MOD_TN), lambda l, j: (l, 0, j))],
        out_specs=pl.BlockSpec((None, MOD_ROWS, MOD_TN), lambda l, j: (l, 0, j)),
        out_shape=jax.ShapeDtypeStruct((DEPTH, MOD_ROWS, n), F32),
        compiler_params=_params(2),
        name="modulation",
    )(cvec, w_mod, b_mod.reshape(DEPTH, 1, n))


def _norm_mod(x, g_ref, shift_ref, scale_ref, row):
    y = x * lax.rsqrt(jnp.mean(x * x, axis=-1, keepdims=True) + EPS) * g_ref[...]
    return y * (1.0 + scale_ref[pl.ds(row, 1), :]) + shift_ref[pl.ds(row, 1), :]


def _inproj_kernel(x_ref, g_ref, shift_ref, scale_ref, w_ref, *o_refs, widths):
    row = _mod_row(pl.program_id(0))
    hb = _norm_mod(x_ref[...], g_ref, shift_ref, scale_ref, row).astype(BF16)
    off = 0
    for o_ref, width in zip(o_refs, widths):
        o_ref[...] = _dot(hb, w_ref[:, off:off + width])
        off += width


def _inproj(x, g, shift, scale, w_bf, widths):
    n_in = w_bf.shape[1]
    row_spec = lambda width: pl.BlockSpec((RB, width), lambda i: (i, 0))
    full = lambda shape: pl.BlockSpec(shape, lambda i: (0,) * len(shape))
    return pl.pallas_call(
        functools.partial(_inproj_kernel, widths=widths),
        grid=(NBLK,),
        in_specs=[row_spec(D_MODEL), full((1, D_MODEL)), full((MOD_ROWS, D_MODEL)),
                  full((MOD_ROWS, D_MODEL)), full((D_MODEL, n_in))],
        out_specs=[row_spec(width) for width in widths],
        out_shape=[jax.ShapeDtypeStruct((NT, width), F32) for width in widths],
        compiler_params=_params(),
        name="inproj",
    )(x, g.reshape(1, D_MODEL), shift, scale, w_bf)


RC = RET_CHUNK
RET_PAIR = 2 * RET_DK


def _rope(x, cos, sin_signed):
    lane = lax.broadcasted_iota(jnp.int32, x.shape, 1)
    swapped = jnp.where(lane % 2 == 0, pltpu.roll(x, x.shape[1] - 1, 1), pltpu.roll(x, 1, 1))
    return x * cos + swapped * sin_signed


def _ret_kernel(decay_ref, q_ref, k_ref, v_ref, cos_ref, sin_ref, s0_ref, *rest, reverse):
    if reverse:
        of_ref, g_ref, gn_ref, o_ref, sfin_ref, st_ref, dm_ref, dq_ref, dk_ref, ds_ref = rest
    else:
        o_ref, sfin_ref, st_ref, dm_ref, dq_ref, dk_ref, ds_ref = rest
    step = pl.program_id(0)
    blk = NBLK - 1 - step if reverse else step
    direction = 1 if reverse else 0

    @pl.when(step == 0)
    def _():
        row = lax.broadcasted_iota(jnp.int32, (RC, RC), 0).astype(F32)
        col = lax.broadcasted_iota(jnp.int32, (RC, RC), 1).astype(F32)
        for h in range(RET_HEADS):
            lg = -jnp.exp(jnp.full((RC, RC), decay_ref[direction, h], F32))
            if reverse:
                diff = col - row
                mask = diff > 0
                q_pow = RC - row
                k_pow = row
            else:
                diff = row - col
                mask = diff >= 0
                q_pow = row + 1.0
                k_pow = RC - 1.0 - row
            dm_ref[h] = jnp.where(mask, jnp.exp(lg * jnp.where(mask, diff, 0.0)), 0.0)
            dq_ref[h] = jnp.exp(lg * q_pow)
            dk_ref[h] = jnp.exp(lg * k_pow)
            ds_ref[h] = jnp.exp(lg * RC)

    starts = _is_last_block(blk) if reverse else _is_first_block(blk)

    @pl.when(starts)
    def _():
        st_ref[...] = jnp.zeros_like(st_ref)
        for h in range(RET_HEADS):
            off = (h % 2) * RET_DK
            st_ref[h, off:off + RET_DK, :] = s0_ref[h]

    lane = lax.broadcasted_iota(jnp.int32, (1, RET_PAIR), 1)
    chunks = range(RB // RC)
    for c in (reversed(chunks) if reverse else chunks):
        rows = slice(c * RC, (c + 1) * RC)
        cos = cos_ref[rows, :]
        sin = sin_ref[rows, :]
        for p in range(RET_HEADS // 2):
            cols = slice(p * RET_PAIR, (p + 1) * RET_PAIR)
            q2 = _rope(q_ref[rows, cols], cos, sin)
            k2 = _rope(k_ref[rows, cols] * (RET_DK ** -0.5), cos, sin)
            for h in (2 * p, 2 * p + 1):
                head_mask = (lane // RET_DK == h % 2).astype(F32)
                vh = v_ref[rows, h * RET_DV:(h + 1) * RET_DV].astype(BF16)
                qm = (q2 * head_mask).astype(BF16)
                km = k2 * head_mask
                att = _dot_nt(qm, km.astype(BF16)) * dm_ref[h]
                o = _dot(att.astype(BF16), vh) + _dot(qm, st_ref[h].astype(BF16)) * dq_ref[h]
                st_ref[h] = st_ref[h] * ds_ref[h] + _dot_tn((km * dk_ref[h]).astype(BF16), vh)
                out_cols = slice(h * RET_DV, (h + 1) * RET_DV)
                if reverse:
                    o = o + of_ref[rows, out_cols]
                    o = o * lax.rsqrt(jnp.mean(o * o, axis=-1, keepdims=True) + EPS)
                    o = o * gn_ref[:, out_cols] * _silu(g_ref[rows, out_cols])
                    o_ref[rows, out_cols] = o.astype(o_ref.dtype)
                else:
                    o_ref[rows, out_cols] = o

    ends = _is_first_block(blk) if reverse else _is_last_block(blk)

    @pl.when(ends)
    def _():
        for h in range(RET_HEADS):
            off = (h % 2) * RET_DK
            sfin_ref[h] = st_ref[h, off:off + RET_DK, :]


def _retention(decay, q, k, v, cos_tab, sin_tab, s0, *, reverse, o_fwd=None, g=None, gn=None):
    bmap = (lambda j: NBLK - 1 - j) if reverse else (lambda j: j)
    direction = 1 if reverse else 0
    qd, vd = RET_HEADS * RET_DK, RET_HEADS * RET_DV
    row_spec = lambda width: pl.BlockSpec((RB, width), lambda j: (bmap(j), 0))
    state_spec = pl.BlockSpec((None, None, RET_HEADS, RET_DK, RET_DV),
                              lambda j: (_seq_of_block(bmap(j)), direction, 0, 0, 0))
    in_specs = [pl.BlockSpec(memory_space=pltpu.SMEM), row_spec(qd), row_spec(qd), row_spec(vd),
                pl.BlockSpec((RB, RET_PAIR), lambda j: (_rope_block(bmap(j)), 0)),
                pl.BlockSpec((RB, RET_PAIR), lambda j: (_rope_block(bmap(j)), 0)),
                state_spec]
    args = [decay, q, k, v, cos_tab, sin_tab, s0]
    if reverse:
        in_specs += [row_spec(vd), row_spec(vd), pl.BlockSpec((1, vd), lambda j: (0, 0))]
        args += [o_fwd, g, gn.reshape(1, vd)]
    tile = pltpu.VMEM((RET_HEADS, RC, RC), F32)
    return pl.pallas_call(
        functools.partial(_ret_kernel, reverse=reverse),
        grid=(NBLK,),
        in_specs=in_specs,
        out_specs=[row_spec(vd),
                   pl.BlockSpec((None, RET_HEADS, RET_DK, RET_DV),
                                lambda j: (_seq_of_block(bmap(j)), 0, 0, 0))],
        out_shape=[jax.ShapeDtypeStruct((NT, vd), BF16 if reverse else F32),
                   jax.ShapeDtypeStruct((NSEQ, RET_HEADS, RET_DK, RET_DV), F32)],
        scratch_shapes=[tile, tile, tile, tile, tile],
        compiler_params=_params(),
        name="retention_bwd" if reverse else "retention_fwd",
    )(*args)


CONV_RT = 32
CONV_CT = 128


def _conv_kernel(a_ref, ga_ref, ap_ref, gap_ref, an_ref, gan_ref, cw_ref, cb_ref, lng_ref, lnb_ref,
                 o_ref, u_ref, y_ref):
    blk = pl.program_id(0)
    keep_prev = jnp.where(_is_first_block(blk), 0.0, 1.0)
    keep_next = jnp.where(_is_last_block(blk), 0.0, 1.0)
    u_ref[0:HALO, :] = ap_ref[...] * _sigmoid(gap_ref[...]) * keep_prev
    u_ref[HALO:HALO + RB, :] = a_ref[...] * _sigmoid(ga_ref[...])
    u_ref[HALO + RB:HALO + RB + HALO, :] = an_ref[...] * _sigmoid(gan_ref[...]) * keep_next
    for ct in range(CONV_CH // CONV_CT):
        cols = slice(ct * CONV_CT, (ct + 1) * CONV_CT)
        for rt in range(RB // CONV_RT):
            base = HALO - CONV_PAD + rt * CONV_RT
            acc = jnp.zeros((CONV_RT, CONV_CT), F32)
            for w in range(CONV_WIDTH):
                acc = acc + u_ref[base + w:base + w + CONV_RT, cols] * cw_ref[w:w + 1, cols]
            y_ref[rt * CONV_RT:(rt + 1) * CONV_RT, cols] = acc + cb_ref[:, cols]
    y = y_ref[...]
    mu = jnp.mean(y, axis=-1, keepdims=True)
    var = jnp.mean(jnp.square(y - mu), axis=-1, keepdims=True)
    o_ref[...] = _silu((y - mu) * lax.rsqrt(var + EPS) * lng_ref[...] + lnb_ref[...]).astype(o_ref.dtype)


def _conv_module(a, ga, cw, cb, lng, lnb):
    per_blk = RB // HALO
    n_halo = NT // HALO
    row_spec = pl.BlockSpec((RB, CONV_CH), lambda i: (i, 0))
    prev_spec = pl.BlockSpec((HALO, CONV_CH), lambda i: (jnp.maximum(i * per_blk - 1, 0), 0))
    next_spec = pl.BlockSpec((HALO, CONV_CH), lambda i: (jnp.minimum((i + 1) * per_blk, n_halo - 1), 0))
    vec = pl.BlockSpec((1, CONV_CH), lambda i: (0, 0))
    return pl.pallas_call(
        _conv_kernel,
        grid=(NBLK,),
        in_specs=[row_spec, row_spec, prev_spec, prev_spec, next_spec, next_spec,
                  pl.BlockSpec((CONV_WIDTH, CONV_CH), lambda i: (0, 0)), vec, vec, vec],
        out_specs=row_spec,
        out_shape=jax.ShapeDtypeStruct((NT, CONV_CH), BF16),
        scratch_shapes=[pltpu.VMEM((RB + 2 * HALO, CONV_CH), F32), pltpu.VMEM((RB, CONV_CH), F32)],
        compiler_params=_params(),
        name="conv_module",
    )(a, ga, a, ga, a, ga, cw, cb.reshape(1, -1), lng.reshape(1, -1), lnb.reshape(1, -1))


GC = GLA_CHUNK
GLA_NSUB = GC // GLA_SUB


def _split3(a):
    p1 = a.astype(BF16)
    r1 = a - p1.astype(F32)
    p2 = r1.astype(BF16)
    p3 = (r1 - p2.astype(F32)).astype(BF16)
    return p1, p2, p3


def _gla_kernel(alr_ref, q_ref, k_ref, v_ref, wa_ref, ba_ref, s0_ref, *rest, reverse):
    if reverse:
        of_ref, r_ref, gn_ref, o_ref, sfin_ref, st_ref, b_ref, ob_ref = rest
    else:
        o_ref, sfin_ref, st_ref, b_ref = rest
        ob_ref = o_ref
    step = pl.program_id(0)
    blk = NBLK - 1 - step if reverse else step
    starts = _is_last_block(blk) if reverse else _is_first_block(blk)

    @pl.when(starts)
    def _():
        st_ref[...] = s0_ref[...]

    z = _dot_hi(alr_ref[...], wa_ref[...]) + ba_ref[...]
    log_a = (jnp.minimum(z, 0.0) - jnp.log(1.0 + jnp.exp(-jnp.abs(z)))) * (1.0 / GLA_TAU)
    row = lax.broadcasted_iota(jnp.int32, (RB, RB), 0)
    col = lax.broadcasted_iota(jnp.int32, (RB, RB), 1)
    ordered = col >= row if reverse else col <= row
    tri = jnp.where(jnp.logical_and(row // GC == col // GC, ordered), 1.0, 0.0).astype(BF16)
    g1, g2, g3 = _split3(log_a)
    b_ref[...] = _dot(tri, g1) + (_dot(tri, g2) + _dot(tri, g3))

    c_row = lax.broadcasted_iota(jnp.int32, (GC, 1), 0)
    a_row = lax.broadcasted_iota(jnp.int32, (GC, GC), 0)
    a_col = lax.broadcasted_iota(jnp.int32, (GC, GC), 1)
    att_mask = a_col > a_row if reverse else a_col <= a_row
    chunks = range(RB // GC)
    for c in (reversed(chunks) if reverse else chunks):
        rows = slice(c * GC, (c + 1) * GC)
        for h in range(GLA_HEADS):
            kcols = slice(h * GLA_DK, (h + 1) * GLA_DK)
            vcols = slice(h * GLA_DV, (h + 1) * GLA_DV)
            b = b_ref[rows, kcols]
            qh = q_ref[rows, kcols] * (GLA_DK ** -0.5)
            kh = k_ref[rows, kcols]
            vh = v_ref[rows, vcols].astype(BF16)
            st = st_ref[h]
            edge = b[0:1, :] if reverse else b[GC - 1:GC, :]
            bounds = []
            for s in range(GLA_NSUB):
                if reverse:
                    hi = (s + 1) * GLA_SUB
                    bounds.append(b[hi:hi + 1, :] if s < GLA_NSUB - 1 else jnp.zeros((1, GLA_DK), F32))
                else:
                    lo = s * GLA_SUB
                    bounds.append(b[lo - 1:lo, :] if s > 0 else jnp.zeros((1, GLA_DK), F32))
            own = jnp.concatenate([jnp.broadcast_to(bd, (GLA_SUB, GLA_DK)) for bd in bounds], axis=0)
            q_own = qh * jnp.exp(b - own)
            q_parts, k_parts = [], []
            for s, bd in enumerate(bounds):
                q_parts.append(jnp.where(c_row // GLA_SUB == s, q_own, 0.0))
                reach = c_row >= s * GLA_SUB if reverse else c_row < (s + 1) * GLA_SUB
                k_parts.append(kh * jnp.exp(jnp.where(reach, bd - b, -jnp.inf)))
            q_bd = jnp.concatenate(q_parts, axis=1).astype(BF16)
            k_cat = jnp.concatenate(k_parts, axis=1).astype(BF16)
            att = jnp.where(att_mask, _dot_nt(q_bd, k_cat), 0.0)
            o = _dot(att.astype(BF16), vh) + _dot_nt((qh * jnp.exp(b)).astype(BF16), st.astype(BF16))
            ke = (kh * jnp.exp(edge - b)).astype(BF16)
            st_ref[h] = st * jnp.exp(edge) + _dot_tn(vh, ke)
            ob_ref[rows, vcols] = o

    if reverse:
        for h in range(GLA_HEADS):
            vcols = slice(h * GLA_DV, (h + 1) * GLA_DV)
            o = ob_ref[:, vcols] + of_ref[:, vcols]
            o = o * lax.rsqrt(jnp.mean(o * o, axis=-1, keepdims=True) + EPS)
            o_ref[:, vcols] = (o * gn_ref[:, vcols] * _silu(r_ref[:, vcols])).astype(o_ref.dtype)

    ends = _is_first_block(blk) if reverse else _is_last_block(blk)

    @pl.when(ends)
    def _():
        sfin_ref[...] = st_ref[...]


def _gla(alr, q, k, v, wa, ba, s0t, *, reverse, o_fwd=None, r=None, gn=None):
    bmap = (lambda j: NBLK - 1 - j) if reverse else (lambda j: j)
    direction = 1 if reverse else 0
    qd, vd = GLA_HEADS * GLA_DK, GLA_HEADS * GLA_DV
    row_spec = lambda width: pl.BlockSpec((RB, width), lambda j: (bmap(j), 0))
    state_spec = pl.BlockSpec((None, None, GLA_HEADS, GLA_DV, GLA_DK),
                              lambda j: (_seq_of_block(bmap(j)), direction, 0, 0, 0))
    in_specs = [row_spec(2 * GLA_RANK), row_spec(qd), row_spec(qd), row_spec(vd),
                pl.BlockSpec((2 * GLA_RANK, qd), lambda j: (0, 0)),
                pl.BlockSpec((1, qd), lambda j: (0, 0)), state_spec]
    args = [alr, q, k, v, wa, ba, s0t]
    scratch = [pltpu.VMEM((GLA_HEADS, GLA_DV, GLA_DK), F32), pltpu.VMEM((RB, qd), F32)]
    if reverse:
        in_specs += [row_spec(vd), row_spec(vd), pl.BlockSpec((1, vd), lambda j: (0, 0))]
        args += [o_fwd, r, gn.reshape(1, vd)]
        scratch += [pltpu.VMEM((RB, vd), F32)]
    return pl.pallas_call(
        functools.partial(_gla_kernel, reverse=reverse),
        grid=(NBLK,),
        in_specs=in_specs,
        out_specs=[row_spec(vd),
                   pl.BlockSpec((None, GLA_HEADS, GLA_DV, GLA_DK),
                                lambda j: (_seq_of_block(bmap(j)), 0, 0, 0))],
        out_shape=[jax.ShapeDtypeStruct((NT, vd), BF16 if reverse else F32),
                   jax.ShapeDtypeStruct((NSEQ, GLA_HEADS, GLA_DV, GLA_DK), F32)],
        scratch_shapes=scratch,
        compiler_params=_params(),
        name="gla_bwd" if reverse else "gla_fwd",
    )(*args)


def _outproj_kernel(*refs, n_mix):
    mix_refs = refs[:n_mix]
    w_refs = refs[n_mix:2 * n_mix]
    (x_ref, gate_ref, g2_ref, shift_ref, scale_ref, rw_ref, rb_ref,
     x1_ref, h2_ref, idx_ref, rank_ref, gw_ref, cnt_ref, carry_ref) = refs[2 * n_mix:]
    step = pl.program_id(0)
    row = _mod_row(step)
    m = _dot(mix_refs[0][...], w_refs[0][...])
    for mix_ref, w_ref in zip(mix_refs[1:], w_refs[1:]):
        m = m + _dot(mix_ref[...], w_ref[...])
    x1 = x_ref[...] + gate_ref[pl.ds(row, 1), :] * m
    x1_ref[...] = x1
    h2 = _norm_mod(x1, g2_ref, shift_ref, scale_ref, row)
    h2_ref[...] = h2
    logits = _dot_hi(h2, rw_ref[...]) + rb_ref[...]

    @pl.when(step == 0)
    def _():
        carry_ref[...] = jnp.zeros_like(carry_ref)

    lane = lax.broadcasted_iota(jnp.int32, logits.shape, 1)
    work = logits
    onehots, top_vals = [], []
    for kk in range(TOP_K):
        top = jnp.max(work, axis=-1, keepdims=True)
        first = jnp.min(jnp.where(work == top, lane, N_EXPERTS), axis=-1, keepdims=True)
        onehot = lane == first
        idx_ref[:, kk:kk + 1] = first
        onehots.append(onehot)
        top_vals.append(top)
        work = jnp.where(onehot, -jnp.inf, work)
    exps = [jnp.exp(v - top_vals[0]) for v in top_vals]
    denom = exps[0]
    for e in exps[1:]:
        denom = denom + e
    for kk in range(TOP_K):
        gw_ref[:, kk:kk + 1] = exps[kk] / denom

    sel = jnp.zeros(logits.shape, F32)
    for onehot in onehots:
        sel = sel + jnp.where(onehot, 1.0, 0.0)
    r_i = lax.broadcasted_iota(jnp.int32, (RB, RB), 0)
    c_i = lax.broadcasted_iota(jnp.int32, (RB, RB), 1)
    earlier = jnp.where(c_i < r_i, 1.0, 0.0).astype(BF16)
    pos = _dot(earlier, sel.astype(BF16)) + carry_ref[...]
    for kk in range(TOP_K):
        rank = jnp.sum(jnp.where(onehots[kk], pos, 0.0), axis=-1, keepdims=True)
        rank_ref[:, kk:kk + 1] = rank.astype(jnp.int32)
    carry = carry_ref[...] + jnp.sum(sel, axis=0, keepdims=True)
    carry_ref[...] = carry
    cnt_ref[...] = carry.astype(jnp.int32)


def _outproj(mixes, w_parts, x, gate, g2, shift, scale, rw, rb):
    n_mix = len(mixes)
    row_spec = lambda width: pl.BlockSpec((RB, width), lambda i: (i, 0))
    full = lambda shape: pl.BlockSpec(shape, lambda i: (0,) * len(shape))
    mod_spec = full((MOD_ROWS, D_MODEL))
    return pl.pallas_call(
        functools.partial(_outproj_kernel, n_mix=n_mix),
        grid=(NBLK,),
        in_specs=[row_spec(m.shape[1]) for m in mixes] + [full(w.shape) for w in w_parts]
        + [row_spec(D_MODEL), mod_spec, full((1, D_MODEL)), mod_spec, mod_spec,
           full((D_MODEL, N_EXPERTS)), full((1, N_EXPERTS))],
        out_specs=[row_spec(D_MODEL), row_spec(D_MODEL), row_spec(TOP_K), row_spec(TOP_K), row_spec(TOP_K),
                   full((1, N_EXPERTS))],
        out_shape=[jax.ShapeDtypeStruct((NT, D_MODEL), F32), jax.ShapeDtypeStruct((NT, D_MODEL), F32),
                   jax.ShapeDtypeStruct((NT, TOP_K), jnp.int32), jax.ShapeDtypeStruct((NT, TOP_K), jnp.int32),
                   jax.ShapeDtypeStruct((NT, TOP_K), F32), jax.ShapeDtypeStruct((1, N_EXPERTS), jnp.int32)],
        scratch_shapes=[pltpu.VMEM((1, N_EXPERTS), F32)],
        compiler_params=_params(),
        name="outproj",
    )(*mixes, *w_parts, x, gate, g2.reshape(1, D_MODEL), shift, scale, rw, rb.reshape(1, N_EXPERTS))


TM = 256
MOE_NBLK = NT * TOP_K // TM + N_EXPERTS
MOE_ROWS = MOE_NBLK * TM


def _moe_kernel(be_ref, nv_ref, nx_ref, x_ref, wgu_hbm, bgu_ref, wd_hbm, bd_ref, y_ref,
                wgu_st, wd_st, wgu_bf, wd_bf, sems, *, layer):
    i = pl.program_id(0)
    n_valid = nv_ref[i]

    def fetch(e):
        return (pltpu.make_async_copy(wgu_hbm.at[layer, e], wgu_st, sems.at[0]),
                pltpu.make_async_copy(wd_hbm.at[layer, e], wd_st, sems.at[1]))

    @pl.when(i == 0)
    def _():
        for cp in fetch(be_ref[0]):
            cp.start()

    @pl.when(n_valid > 0)
    def _():
        e = be_ref[i]
        changed = jnp.logical_or(i == 0, e != be_ref[jnp.maximum(i - 1, 0)])

        @pl.when(changed)
        def _():
            for cp in fetch(e):
                cp.wait()
            wgu_bf[...] = wgu_st[...].astype(BF16)
            wd_bf[...] = wd_st[...].astype(BF16)
            nxt = nx_ref[e]

            @pl.when(nxt >= 0)
            def _():
                for cp in fetch(nxt):
                    cp.start()

        rows = lax.broadcasted_iota(jnp.int32, (TM, 1), 0)
        x = jnp.where(rows < n_valid, x_ref[...], 0.0).astype(BF16)
        gu = _dot(x, wgu_bf[...]) + bgu_ref[...]
        gate = jnp.minimum(gu[:, :D_FF], SWIGLU_LIMIT)
        up = jnp.clip(gu[:, D_FF:], -SWIGLU_LIMIT, SWIGLU_LIMIT)
        hdn = gate * _sigmoid(SWIGLU_ALPHA * gate) * (up + 1.0)
        y_ref[...] = _dot(hdn.astype(BF16), wd_bf[...]) + bd_ref[...]


def _moe_experts(layer, block_e, n_valid, next_e, xs, w_gu, b_gu, w_down, b_down):
    grid_spec = pltpu.PrefetchScalarGridSpec(
        num_scalar_prefetch=3,
        grid=(MOE_NBLK,),
        in_specs=[pl.BlockSpec((TM, D_MODEL), lambda i, be, nv, nx: (i, 0)),
                  pl.BlockSpec(memory_space=pl.ANY),
                  pl.BlockSpec((None, None, 1, 2 * D_FF), lambda i, be, nv, nx: (layer, be[i], 0, 0)),
                  pl.BlockSpec(memory_space=pl.ANY),
                  pl.BlockSpec((None, None, 1, D_MODEL), lambda i, be, nv, nx: (layer, be[i], 0, 0))],
        out_specs=pl.BlockSpec((TM, D_MODEL), lambda i, be, nv, nx: (i, 0)),
        scratch_shapes=[pltpu.VMEM((D_MODEL, 2 * D_FF), F32), pltpu.VMEM((D_FF, D_MODEL), F32),
                        pltpu.VMEM((D_MODEL, 2 * D_FF), BF16), pltpu.VMEM((D_FF, D_MODEL), BF16),
                        pltpu.SemaphoreType.DMA((2,))],
    )
    return pl.pallas_call(
        functools.partial(_moe_kernel, layer=layer),
        grid_spec=grid_spec,
        out_shape=jax.ShapeDtypeStruct((MOE_ROWS, D_MODEL), F32),
        compiler_params=_params(),
        name="moe_experts",
    )(block_e, n_valid, next_e, xs, w_gu, b_gu.reshape(DEPTH, N_EXPERTS, 1, -1), w_down,
      b_down.reshape(DEPTH, N_EXPERTS, 1, -1))


SC_WORKERS = 32
SC_WIN = 32


def _sc_mesh():
    return plsc.VectorSubcoreMesh(core_axis_name="core", subcore_axis_name="subcore")


def _sc_worker():
    return lax.axis_index("core") * (SC_WORKERS // 2) + lax.axis_index("subcore")


def _sc_scatter_rows(x, dest_t, n_rows):
    n, width = x.shape
    kk = dest_t.shape[0]
    per = n // SC_WORKERS
    n_win = per // SC_WIN
    assert per * SC_WORKERS == n and n_win * SC_WIN == per and n_win % 2 == 0

    @pl.kernel(out_type=jax.ShapeDtypeStruct((n_rows, width), x.dtype), mesh=_sc_mesh(),
               scratch_types=[pltpu.VMEM((kk, per), jnp.int32), pltpu.VMEM((SC_WIN, width), x.dtype),
                              pltpu.VMEM((SC_WIN, width), x.dtype), pltpu.SemaphoreType.DMA((4,))])
    def scatter(x_hbm, i_hbm, o_hbm, idx_v, buf0, buf1, sems):
        base = _sc_worker() * per
        pltpu.sync_copy(i_hbm.at[:, pl.ds(base, per)], idx_v)

        def get(j, buf, s):
            return pltpu.make_async_copy(x_hbm.at[pl.ds(base + j * SC_WIN, SC_WIN)], buf, sems.at[s])

        def put(j, q, buf, s):
            return pltpu.make_async_copy(buf, o_hbm.at[idx_v.at[q, pl.ds(j * SC_WIN, SC_WIN)]], sems.at[s])

        get(0, buf0, 0).start()

        @pl.loop(0, n_win, step=2)
        def _(j):
            get(j, buf0, 0).wait()

            @pl.when(j > 0)
            def _():
                for q in range(kk):
                    put(j - 1, q, buf1, 3).wait()

            get(j + 1, buf1, 1).start()
            for q in range(kk):
                put(j, q, buf0, 2).start()
            get(j + 1, buf1, 1).wait()
            for q in range(kk):
                put(j, q, buf0, 2).wait()

            @pl.when(j + 2 < n_win)
            def _():
                get(j + 2, buf0, 0).start()

            for q in range(kk):
                put(j + 1, q, buf1, 3).start()

        for q in range(kk):
            put(n_win - 1, q, buf1, 3).wait()

    return scatter(x, dest_t)


def _sc_gather_rows(y, idx):
    n = idx.shape[0]
    width = y.shape[1]
    per = n // SC_WORKERS
    n_win = per // SC_WIN
    assert per * SC_WORKERS == n and n_win * SC_WIN == per and n_win % 2 == 0

    @pl.kernel(out_type=jax.ShapeDtypeStruct((n, width), y.dtype), mesh=_sc_mesh(),
               scratch_types=[pltpu.VMEM((per,), jnp.int32), pltpu.VMEM((SC_WIN, width), y.dtype),
                              pltpu.VMEM((SC_WIN, width), y.dtype), pltpu.SemaphoreType.DMA((4,))])
    def gather(y_hbm, i_hbm, o_hbm, idx_v, buf0, buf1, sems):
        base = _sc_worker() * per
        pltpu.sync_copy(i_hbm.at[pl.ds(base, per)], idx_v)

        def get(j, buf, s):
            return pltpu.make_async_copy(y_hbm.at[idx_v.at[pl.ds(j * SC_WIN, SC_WIN)]], buf, sems.at[s])

        def put(j, buf, s):
            return pltpu.make_async_copy(buf, o_hbm.at[pl.ds(base + j * SC_WIN, SC_WIN)], sems.at[s])

        get(0, buf0, 0).start()

        @pl.loop(0, n_win, step=2)
        def _(j):
            get(j, buf0, 0).wait()

            @pl.when(j > 0)
            def _():
                put(j - 1, buf1, 3).wait()

            get(j + 1, buf1, 1).start()
            put(j, buf0, 2).start()
            get(j + 1, buf1, 1).wait()
            put(j, buf0, 2).wait()

            @pl.when(j + 2 < n_win)
            def _():
                get(j + 2, buf0, 0).start()

            put(j + 1, buf1, 3).start()

        put(n_win - 1, buf1, 3).wait()

    return gather(y, idx)


def _combine_kernel(x1_ref, *rest):
    yg_refs = rest[:TOP_K]
    gw_ref, gate_ref, o_ref = rest[TOP_K:]
    row = _mod_row(pl.program_id(0))
    acc = yg_refs[0][...] * gw_ref[:, 0:1]
    for kk in range(1, TOP_K):
        acc = acc + yg_refs[kk][...] * gw_ref[:, kk:kk + 1]
    o_ref[...] = x1_ref[...] + gate_ref[pl.ds(row, 1), :] * acc


def _combine(x1, yg, gw, gate):
    row_spec = lambda width: pl.BlockSpec((RB, width), lambda i: (i, 0))
    slot_spec = lambda k: pl.BlockSpec((RB, D_MODEL), lambda i: (k * NBLK + i, 0))
    return pl.pallas_call(
        _combine_kernel,
        grid=(NBLK,),
        in_specs=[row_spec(D_MODEL)] + [slot_spec(k) for k in range(TOP_K)]
        + [row_spec(TOP_K), pl.BlockSpec((MOD_ROWS, D_MODEL), lambda i: (0, 0))],
        out_specs=row_spec(D_MODEL),
        out_shape=jax.ShapeDtypeStruct((NT, D_MODEL), F32),
        compiler_params=_params(),
        name="moe_combine",
    )(x1, *([yg] * TOP_K), gw, gate)


def _final_norm_kernel(x_ref, g_ref, o_ref):
    x = x_ref[...]
    o_ref[...] = x * lax.rsqrt(jnp.mean(x * x, axis=-1, keepdims=True) + EPS) * g_ref[...]


def _final_norm(x, g):
    row_spec = pl.BlockSpec((RB, D_MODEL), lambda i: (i, 0))
    return pl.pallas_call(
        _final_norm_kernel,
        grid=(NBLK,),
        in_specs=[row_spec, pl.BlockSpec((1, D_MODEL), lambda i: (0, 0))],
        out_specs=row_spec,
        out_shape=jax.ShapeDtypeStruct((NT, D_MODEL), F32),
        compiler_params=_params(),
        name="final_norm",
    )(x, g.reshape(1, D_MODEL))


def _routing_plan(counts, idx4, rank4):
    counts = counts.reshape(N_EXPERTS)
    padded = (counts + TM - 1) // TM * TM
    pad_end = jnp.cumsum(padded)
    pad_start = pad_end - padded
    blk_row = jnp.arange(MOE_NBLK, dtype=jnp.int32) * TM
    n_used = pad_end[-1] // TM
    block_e = jnp.sum((pad_end[None, :] <= blk_row[:, None]).astype(jnp.int32), axis=1)
    block_e = jnp.minimum(block_e, N_EXPERTS - 1)
    n_valid = jnp.clip(counts[block_e] - (blk_row - pad_start[block_e]), 0, TM).astype(jnp.int32)
    used = jnp.arange(MOE_NBLK) < n_used
    block_e = jnp.where(used, block_e, block_e[n_used - 1])
    n_valid = jnp.where(used, n_valid, 0)
    dest4 = (pad_start[idx4] + rank4).astype(jnp.int32)
    ids = jnp.arange(N_EXPERTS, dtype=jnp.int32)
    later = jnp.where(jnp.logical_and(counts[None, :] > 0, ids[None, :] > ids[:, None]), ids[None, :], N_EXPERTS)
    next_e = jnp.min(later, axis=1)
    next_e = jnp.where(next_e == N_EXPERTS, -1, next_e).astype(jnp.int32)
    return block_e, n_valid, next_e, dest4


def _rope_tables():
    rows = DEC_SEQ // GRID_W
    row = jnp.repeat(jnp.arange(rows, dtype=F32), GRID_W)
    col = jnp.tile(jnp.arange(GRID_W, dtype=F32), rows)
    n_f = RET_DK // 4
    freqs = ROPE_THETA ** (-jnp.arange(n_f, dtype=F32) / n_f)
    ang = jnp.concatenate([row[:, None] * freqs, col[:, None] * freqs], axis=-1)
    cos = jnp.repeat(jnp.cos(ang), 2, axis=-1)
    sin = jnp.repeat(jnp.sin(ang), 2, axis=-1) * jnp.tile(jnp.asarray([-1.0, 1.0], F32), RET_DK // 2)
    cos = jnp.concatenate([jnp.ones((RB, RET_DK), F32), cos], axis=0)
    sin = jnp.concatenate([jnp.zeros((RB, RET_DK), F32), sin], axis=0)
    return jnp.tile(cos, (1, 2)), jnp.tile(sin, (1, 2))


def kernel(x_prompt, x_sample, state_ret, state_gla, c, c_ctx, w_mod, b_mod, norm1_g, norm2_g, final_g, even_w_in, ret_decay, ret_gn, conv_w, conv_b, conv_ln_g, conv_ln_b, even_w_out, odd_w_in, gla_w_a2, gla_b_a2, gla_gn, odd_w_out, router_w, router_b, exp_w_gu, exp_b_gu, exp_w_down, exp_b_down):
    x = jnp.concatenate([x_prompt.reshape(NT_PROMPT, D_MODEL), x_sample.reshape(-1, D_MODEL)], axis=0)
    cvec = jnp.concatenate([c_ctx[None, :], c, jnp.zeros((MOD_ROWS - 1 - DEC_BATCH, D_MODEL), F32)], axis=0)
    mods = _modulation(cvec, w_mod, b_mod).reshape(DEPTH, MOD_ROWS, N_MOD, D_MODEL)
    cos_tab, sin_tab = _rope_tables()
    new_ret, new_gla = [], []
    for l in range(DEPTH):
        mod = [mods[l, :, j, :] for j in range(N_MOD)]
        if l % 2 == 0:
            e = l // 2
            qd, vd = RET_HEADS * RET_DK, RET_HEADS * RET_DV
            q, k, v, g, a, ga = _inproj(x, norm1_g[l], mod[0], mod[1], even_w_in[e].astype(BF16),
                                        (qd, qd, vd, vd, CONV_CH, CONV_CH))
            s0 = jnp.concatenate([jnp.zeros((BATCH,) + state_ret.shape[2:], F32), state_ret[:, e]], axis=0)
            o_f, s_f = _retention(ret_decay[e], q, k, v, cos_tab, sin_tab, s0, reverse=False)
            ret, s_b = _retention(ret_decay[e], q, k, v, cos_tab, sin_tab, s0, reverse=True,
                                  o_fwd=o_f, g=g, gn=ret_gn[e])
            u = _conv_module(a, ga, conv_w[e], conv_b[e], conv_ln_g[e], conv_ln_b[e])
            w_out = even_w_out[e].astype(BF16)
            mixes, w_parts = [ret, u], [w_out[:vd], w_out[vd:]]
            new_ret.append(jnp.stack([s_f[:BATCH], s_b[:BATCH]], axis=1))
        else:
            o = l // 2
            qd, vd = GLA_HEADS * GLA_DK, GLA_HEADS * GLA_DV
            q, k, v, r, alr = _inproj(x, norm1_g[l], mod[0], mod[1], odd_w_in[o].astype(BF16),
                                      (qd, qd, vd, vd, 2 * GLA_RANK))
            s0t = jnp.concatenate([jnp.zeros((BATCH,) + state_gla.shape[2:], F32), state_gla[:, o]], axis=0)
            s0t = jnp.swapaxes(s0t, -1, -2)
            zeros = jnp.zeros((GLA_RANK, qd), F32)
            wa_f = jnp.concatenate([gla_w_a2[o, 0], zeros], axis=0)
            wa_b = jnp.concatenate([zeros, gla_w_a2[o, 1]], axis=0)
            o_f, s_f = _gla(alr, q, k, v, wa_f, gla_b_a2[o, 0].reshape(1, qd), s0t, reverse=False)
            y, s_b = _gla(alr, q, k, v, wa_b, gla_b_a2[o, 1].reshape(1, qd), s0t, reverse=True,
                          o_fwd=o_f, r=r, gn=gla_gn[o])
            mixes, w_parts = [y], [odd_w_out[o].astype(BF16)]
            new_gla.append(jnp.swapaxes(jnp.stack([s_f[:BATCH], s_b[:BATCH]], axis=1), -1, -2))
        x1, h2, idx4, rank4, gates, counts = _outproj(mixes, w_parts, x, mod[2], norm2_g[l], mod[3], mod[4],
                                                      router_w[l], router_b[l])
        block_e, n_valid, next_e, dest4 = _routing_plan(counts, idx4, rank4)
        dest_t = dest4.T
        xs = _sc_scatter_rows(h2, dest_t, MOE_ROWS)
        yb = _moe_experts(l, block_e, n_valid, next_e, xs, exp_w_gu, exp_b_gu, exp_w_down, exp_b_down)
        yg = _sc_gather_rows(yb, dest_t.reshape(TOP_K * NT))
        x = _combine(x1, yg, gates, mod[5])
    y = _final_norm(x, final_g)
    y_prompt = y[:NT_PROMPT].reshape(BATCH, SEQ, D_MODEL)
    y_sample = y[NT_PROMPT:].reshape(DEC_BATCH, DEC_SEQ, D_MODEL)
    return (y_prompt, y_sample, jnp.stack(new_ret, axis=1), jnp.stack(new_gla, axis=1))
```

```python
import functools

import jax
import jax.numpy as jnp
from jax import lax
from jax.experimental import pallas as pl
from jax.experimental.pallas import tpu as pltpu
from jax.experimental.pallas import tpu_sc as plsc

F32 = jnp.float32
BF16 = jnp.bfloat16

D_MODEL = 1024
BATCH = 16
SEQ = 256
DEPTH = 4
DEC_BATCH = 4
DEC_SEQ = 4096
GRID_W = 64
RET_HEADS = 4
RET_DK = 64
RET_DV = 128
RET_CHUNK = 128
CONV_CH = 512
CONV_WIDTH = 31
CONV_PAD = CONV_WIDTH // 2
GLA_HEADS = 4
GLA_DK = 128
GLA_DV = 256
GLA_RANK = 16
GLA_TAU = 16.0
GLA_CHUNK = 64
GLA_SUB = 16
N_EXPERTS = 32
TOP_K = 4
D_FF = 1024
SWIGLU_LIMIT = 7.0
SWIGLU_ALPHA = 1.702
MOE_BLOCK = 128
ROPE_THETA = 10000.0
EPS = 1e-6
N_MOD = 6

RB = 256
NT_PROMPT = BATCH * SEQ
NT = NT_PROMPT + DEC_BATCH * DEC_SEQ
NBLK = NT // RB
N_PROMPT_BLK = NT_PROMPT // RB
SAMPLE_BLK = DEC_SEQ // RB
NSEQ = BATCH + DEC_BATCH
MOD_ROWS = 8
HALO = 16
VMEM_LIMIT = 48 * 1024 * 1024

assert SEQ == RB and DEC_SEQ % RB == 0 and CONV_PAD < HALO


def _seq_of_block(i):
    return jnp.where(i < N_PROMPT_BLK, i, N_PROMPT_BLK + (i - N_PROMPT_BLK) // SAMPLE_BLK)


def _is_first_block(i):
    return jnp.logical_or(i < N_PROMPT_BLK, (i - N_PROMPT_BLK) % SAMPLE_BLK == 0)


def _is_last_block(i):
    return jnp.logical_or(i < N_PROMPT_BLK, (i - N_PROMPT_BLK) % SAMPLE_BLK == SAMPLE_BLK - 1)


def _mod_row(i):
    return jnp.where(i < N_PROMPT_BLK, 0, 1 + (i - N_PROMPT_BLK) // SAMPLE_BLK)


def _rope_block(i):
    return jnp.where(i < N_PROMPT_BLK, 0, 1 + (i - N_PROMPT_BLK) % SAMPLE_BLK)


def _dot(a, b):
    return jnp.dot(a, b, preferred_element_type=F32)


def _dot_nt(a, b):
    return lax.dot_general(a, b, (((1,), (1,)), ((), ())), preferred_element_type=F32)


def _dot_tn(a, b):
    return lax.dot_general(a, b, (((0,), (0,)), ((), ())), preferred_element_type=F32)


def _split2(a):
    hi = a.astype(BF16)
    lo = (a - hi.astype(F32)).astype(BF16)
    return hi, lo


def _dot_hi(a, b):
    a_hi, a_lo = _split2(a)
    b_hi, b_lo = _split2(b)
    return _dot(a_hi, b_hi) + (_dot(a_hi, b_lo) + _dot(a_lo, b_hi))


def _silu(x):
    return x * (1.0 / (1.0 + jnp.exp(-x)))


def _sigmoid(x):
    return 1.0 / (1.0 + jnp.exp(-x))


def _pack_rows(x):
    n = x.shape[1] // 2
    lo = pltpu.bitcast(x[:, :n].astype(BF16).astype(F32), jnp.uint32)
    hi = pltpu.bitcast(x[:, n:].astype(BF16).astype(F32), jnp.uint32)
    return hi | (lo >> 16)


def _unpack_rows(u):
    lo = pltpu.bitcast(u << 16, F32)
    hi = pltpu.bitcast(u & jnp.uint32(0xFFFF0000), F32)
    return lo, hi


def _params(n_axes=1, vmem=VMEM_LIMIT):
    return pltpu.CompilerParams(dimension_semantics=("arbitrary",) * n_axes, vmem_limit_bytes=vmem)


MOD_TN = 1536


def _mod_kernel(c_ref, w_ref, b_ref, o_ref):
    s = _silu(c_ref[...]).astype(BF16)
    o_ref[...] = _dot(s, w_ref[...].astype(BF16)) + b_ref[...]


def _modulation(cvec, w_mod, b_mod):
    n = N_MOD * D_MODEL
    return pl.pallas_call(
        _mod_kernel,
        grid=(DEPTH, n // MOD_TN),
        in_specs=[pl.BlockSpec((MOD_ROWS, D_MODEL), lambda l, j: (0, 0)),
                  pl.BlockSpec((None, D_MODEL, MOD_TN), lambda l, j: (l, 0, j)),
                  pl.BlockSpec((None, 1, MOD_TN), lambda l, j: (l, 0, j))],
        out_specs=pl.BlockSpec((None, MOD_ROWS, MOD_TN), lambda l, j: (l, 0, j)),
        out_shape=jax.ShapeDtypeStruct((DEPTH, MOD_ROWS, n), F32),
        compiler_params=_params(2),
        name="modulation",
    )(cvec, w_mod, b_mod.reshape(DEPTH, 1, n))


def _norm_mod(x, g_ref, shift_ref, scale_ref, row):
    y = x * lax.rsqrt(jnp.mean(x * x, axis=-1, keepdims=True) + EPS) * g_ref[...]
    return y * (1.0 + scale_ref[pl.ds(row, 1), :]) + shift_ref[pl.ds(row, 1), :]


def _inproj_kernel(x_ref, g_ref, shift_ref, scale_ref, w_ref, *o_refs, widths):
    row = _mod_row(pl.program_id(0))
    hb = _norm_mod(x_ref[...], g_ref, shift_ref, scale_ref, row).astype(BF16)
    off = 0
    for o_ref, width in zip(o_refs, widths):
        o_ref[...] = _dot(hb, w_ref[:, off:off + width])
        off += width


def _inproj(x, g, shift, scale, w_bf, widths):
    n_in = w_bf.shape[1]
    row_spec = lambda width: pl.BlockSpec((RB, width), lambda i: (i, 0))
    full = lambda shape: pl.BlockSpec(shape, lambda i: (0,) * len(shape))
    return pl.pallas_call(
        functools.partial(_inproj_kernel, widths=widths),
        grid=(NBLK,),
        in_specs=[row_spec(D_MODEL), full((1, D_MODEL)), full((MOD_ROWS, D_MODEL)),
                  full((MOD_ROWS, D_MODEL)), full((D_MODEL, n_in))],
        out_specs=[row_spec(width) for width in widths],
        out_shape=[jax.ShapeDtypeStruct((NT, width), F32) for width in widths],
        compiler_params=_params(),
        name="inproj",
    )(x, g.reshape(1, D_MODEL), shift, scale, w_bf)


RC = RET_CHUNK
RET_PAIR = 2 * RET_DK


def _rope(x, cos, sin_signed):
    lane = lax.broadcasted_iota(jnp.int32, x.shape, 1)
    swapped = jnp.where(lane % 2 == 0, pltpu.roll(x, x.shape[1] - 1, 1), pltpu.roll(x, 1, 1))
    return x * cos + swapped * sin_signed


def _ret_kernel(decay_ref, q_ref, k_ref, v_ref, cos_ref, sin_ref, s0_ref, *rest, reverse):
    if reverse:
        of_ref, g_ref, gn_ref, o_ref, sfin_ref, st_ref, dm_ref, dq_ref, dk_ref, ds_ref = rest
    else:
        o_ref, sfin_ref, st_ref, dm_ref, dq_ref, dk_ref, ds_ref = rest
    step = pl.program_id(0)
    blk = NBLK - 1 - step if reverse else step
    direction = 1 if reverse else 0

    @pl.when(step == 0)
    def _():
        row = lax.broadcasted_iota(jnp.int32, (RC, RC), 0).astype(F32)
        col = lax.broadcasted_iota(jnp.int32, (RC, RC), 1).astype(F32)
        for h in range(RET_HEADS):
            lg = -jnp.exp(jnp.full((RC, RC), decay_ref[direction, h], F32))
            if reverse:
                diff = col - row
                mask = diff > 0
                q_pow = RC - row
                k_pow = row
            else:
                diff = row - col
                mask = diff >= 0
                q_pow = row + 1.0
                k_pow = RC - 1.0 - row
            dm_ref[h] = jnp.where(mask, jnp.exp(lg * jnp.where(mask, diff, 0.0)), 0.0)
            dq_ref[h] = jnp.exp(lg * q_pow)
            dk_ref[h] = jnp.exp(lg * k_pow)
            ds_ref[h] = jnp.exp(lg * RC)

    starts = _is_last_block(blk) if reverse else _is_first_block(blk)

    @pl.when(starts)
    def _():
        st_ref[...] = jnp.zeros_like(st_ref)
        for h in range(RET_HEADS):
            off = (h % 2) * RET_DK
            st_ref[h, off:off + RET_DK, :] = s0_ref[h]

    lane = lax.broadcasted_iota(jnp.int32, (1, RET_PAIR), 1)
    chunks = range(RB // RC)
    for c in (reversed(chunks) if reverse else chunks):
        rows = slice(c * RC, (c + 1) * RC)
        cos = cos_ref[rows, :]
        sin = sin_ref[rows, :]
        for p in range(RET_HEADS // 2):
            cols = slice(p * RET_PAIR, (p + 1) * RET_PAIR)
            q2 = _rope(q_ref[rows, cols], cos, sin)
            k2 = _rope(k_ref[rows, cols] * (RET_DK ** -0.5), cos, sin)
            for h in (2 * p, 2 * p + 1):
                head_mask = (lane // RET_DK == h % 2).astype(F32)
                vh = v_ref[rows, h * RET_DV:(h + 1) * RET_DV].astype(BF16)
                qm = (q2 * head_mask).astype(BF16)
                km = k2 * head_mask
                att = _dot_nt(qm, km.astype(BF16)) * dm_ref[h]
                o = _dot(att.astype(BF16), vh) + _dot(qm, st_ref[h].astype(BF16)) * dq_ref[h]
                st_ref[h] = st_ref[h] * ds_ref[h] + _dot_tn((km * dk_ref[h]).astype(BF16), vh)
                out_cols = slice(h * RET_DV, (h + 1) * RET_DV)
                if reverse:
                    o = o + of_ref[rows, out_cols]
                    o = o * lax.rsqrt(jnp.mean(o * o, axis=-1, keepdims=True) + EPS)
                    o = o * gn_ref[:, out_cols] * _silu(g_ref[rows, out_cols])
                    o_ref[rows, out_cols] = o.astype(o_ref.dtype)
                else:
                    o_ref[rows, out_cols] = o

    ends = _is_first_block(blk) if reverse else _is_last_block(blk)

    @pl.when(ends)
    def _():
        for h in range(RET_HEADS):
            off = (h % 2) * RET_DK
            sfin_ref[h] = st_ref[h, off:off + RET_DK, :]


def _retention(decay, q, k, v, cos_tab, sin_tab, s0, *, reverse, o_fwd=None, g=None, gn=None):
    bmap = (lambda j: NBLK - 1 - j) if reverse else (lambda j: j)
    direction = 1 if reverse else 0
    qd, vd = RET_HEADS * RET_DK, RET_HEADS * RET_DV
    row_spec = lambda width: pl.BlockSpec((RB, width), lambda j: (bmap(j), 0))
    state_spec = pl.BlockSpec((None, None, RET_HEADS, RET_DK, RET_DV),
                              lambda j: (_seq_of_block(bmap(j)), direction, 0, 0, 0))
    in_specs = [pl.BlockSpec(memory_space=pltpu.SMEM), row_spec(qd), row_spec(qd), row_spec(vd),
                pl.BlockSpec((RB, RET_PAIR), lambda j: (_rope_block(bmap(j)), 0)),
                pl.BlockSpec((RB, RET_PAIR), lambda j: (_rope_block(bmap(j)), 0)),
                state_spec]
    args = [decay, q, k, v, cos_tab, sin_tab, s0]
    if reverse:
        in_specs += [row_spec(vd), row_spec(vd), pl.BlockSpec((1, vd), lambda j: (0, 0))]
        args += [o_fwd, g, gn.reshape(1, vd)]
    tile = pltpu.VMEM((RET_HEADS, RC, RC), F32)
    return pl.pallas_call(
        functools.partial(_ret_kernel, reverse=reverse),
        grid=(NBLK,),
        in_specs=in_specs,
        out_specs=[row_spec(vd),
                   pl.BlockSpec((None, RET_HEADS, RET_DK, RET_DV),
                                lambda j: (_seq_of_block(bmap(j)), 0, 0, 0))],
        out_shape=[jax.ShapeDtypeStruct((NT, vd), BF16 if reverse else F32),
                   jax.ShapeDtypeStruct((NSEQ, RET_HEADS, RET_DK, RET_DV), F32)],
        scratch_shapes=[tile, tile, tile, tile, tile],
        compiler_params=_params(),
        name="retention_bwd" if reverse else "retention_fwd",
    )(*args)


CONV_RT = 32
CONV_CT = 128


def _conv_kernel(a_ref, ga_ref, ap_ref, gap_ref, an_ref, gan_ref, cw_ref, cb_ref, lng_ref, lnb_ref,
                 o_ref, u_ref, y_ref):
    blk = pl.program_id(0)
    keep_prev = jnp.where(_is_first_block(blk), 0.0, 1.0)
    keep_next = jnp.where(_is_last_block(blk), 0.0, 1.0)
    u_ref[0:HALO, :] = ap_ref[...] * _sigmoid(gap_ref[...]) * keep_prev
    u_ref[HALO:HALO + RB, :] = a_ref[...] * _sigmoid(ga_ref[...])
    u_ref[HALO + RB:HALO + RB + HALO, :] = an_ref[...] * _sigmoid(gan_ref[...]) * keep_next
    for ct in range(CONV_CH // CONV_CT):
        cols = slice(ct * CONV_CT, (ct + 1) * CONV_CT)
        for rt in range(RB // CONV_RT):
            base = HALO - CONV_PAD + rt * CONV_RT
            acc = jnp.zeros((CONV_RT, CONV_CT), F32)
            for w in range(CONV_WIDTH):
                acc = acc + u_ref[base + w:base + w + CONV_RT, cols] * cw_ref[w:w + 1, cols]
            y_ref[rt * CONV_RT:(rt + 1) * CONV_RT, cols] = acc + cb_ref[:, cols]
    y = y_ref[...]
    mu = jnp.mean(y, axis=-1, keepdims=True)
    var = jnp.mean(jnp.square(y - mu), axis=-1, keepdims=True)
    o_ref[...] = _silu((y - mu) * lax.rsqrt(var + EPS) * lng_ref[...] + lnb_ref[...]).astype(o_ref.dtype)


def _conv_module(a, ga, cw, cb, lng, lnb):
    per_blk = RB // HALO
    n_halo = NT // HALO
    row_spec = pl.BlockSpec((RB, CONV_CH), lambda i: (i, 0))
    prev_spec = pl.BlockSpec((HALO, CONV_CH), lambda i: (jnp.maximum(i * per_blk - 1, 0), 0))
    next_spec = pl.BlockSpec((HALO, CONV_CH), lambda i: (jnp.minimum((i + 1) * per_blk, n_halo - 1), 0))
    vec = pl.BlockSpec((1, CONV_CH), lambda i: (0, 0))
    return pl.pallas_call(
        _conv_kernel,
        grid=(NBLK,),
        in_specs=[row_spec, row_spec, prev_spec, prev_spec, next_spec, next_spec,
                  pl.BlockSpec((CONV_WIDTH, CONV_CH), lambda i: (0, 0)), vec, vec, vec],
        out_specs=row_spec,
        out_shape=jax.ShapeDtypeStruct((NT, CONV_CH), BF16),
        scratch_shapes=[pltpu.VMEM((RB + 2 * HALO, CONV_CH), F32), pltpu.VMEM((RB, CONV_CH), F32)],
        compiler_params=_params(),
        name="conv_module",
    )(a, ga, a, ga, a, ga, cw, cb.reshape(1, -1), lng.reshape(1, -1), lnb.reshape(1, -1))


GC = GLA_CHUNK
GLA_NSUB = GC // GLA_SUB


def _split3(a):
    p1 = a.astype(BF16)
    r1 = a - p1.astype(F32)
    p2 = r1.astype(BF16)
    p3 = (r1 - p2.astype(F32)).astype(BF16)
    return p1, p2, p3


def _gla_kernel(alr_ref, q_ref, k_ref, v_ref, wa_ref, ba_ref, s0_ref, *rest, reverse):
    if reverse:
        of_ref, r_ref, gn_ref, o_ref, sfin_ref, st_ref, b_ref, ob_ref = rest
    else:
        o_ref, sfin_ref, st_ref, b_ref = rest
        ob_ref = o_ref
    step = pl.program_id(0)
    blk = NBLK - 1 - step if reverse else step
    starts = _is_last_block(blk) if reverse else _is_first_block(blk)

    @pl.when(starts)
    def _():
        st_ref[...] = s0_ref[...]

    z = _dot_hi(alr_ref[...], wa_ref[...]) + ba_ref[...]
    log_a = (jnp.minimum(z, 0.0) - jnp.log(1.0 + jnp.exp(-jnp.abs(z)))) * (1.0 / GLA_TAU)
    row = lax.broadcasted_iota(jnp.int32, (RB, RB), 0)
    col = lax.broadcasted_iota(jnp.int32, (RB, RB), 1)
    ordered = col >= row if reverse else col <= row
    tri = jnp.where(jnp.logical_and(row // GC == col // GC, ordered), 1.0, 0.0).astype(BF16)
    g1, g2, g3 = _split3(log_a)
    b_ref[...] = _dot(tri, g1) + (_dot(tri, g2) + _dot(tri, g3))

    c_row = lax.broadcasted_iota(jnp.int32, (GC, 1), 0)
    a_row = lax.broadcasted_iota(jnp.int32, (GC, GC), 0)
    a_col = lax.broadcasted_iota(jnp.int32, (GC, GC), 1)
    att_mask = a_col > a_row if reverse else a_col <= a_row
    chunks = range(RB // GC)
    for c in (reversed(chunks) if reverse else chunks):
        rows = slice(c * GC, (c + 1) * GC)
        for h in range(GLA_HEADS):
            kcols = slice(h * GLA_DK, (h + 1) * GLA_DK)
            vcols = slice(h * GLA_DV, (h + 1) * GLA_DV)
            b = b_ref[rows, kcols]
            qh = q_ref[rows, kcols] * (GLA_DK ** -0.5)
            kh = k_ref[rows, kcols]
            vh = v_ref[rows, vcols].astype(BF16)
            st = st_ref[h]
            edge = b[0:1, :] if reverse else b[GC - 1:GC, :]
            bounds = []
            for s in range(GLA_NSUB):
                if reverse:
                    hi = (s + 1) * GLA_SUB
                    bounds.append(b[hi:hi + 1, :] if s < GLA_NSUB - 1 else jnp.zeros((1, GLA_DK), F32))
                else:
                    lo = s * GLA_SUB
                    bounds.append(b[lo - 1:lo, :] if s > 0 else jnp.zeros((1, GLA_DK), F32))
            own = jnp.concatenate([jnp.broadcast_to(bd, (GLA_SUB, GLA_DK)) for bd in bounds], axis=0)
            q_own = qh * jnp.exp(b - own)
            q_parts, k_parts = [], []
            for s, bd in enumerate(bounds):
                q_parts.append(jnp.where(c_row // GLA_SUB == s, q_own, 0.0))
                reach = c_row >= s * GLA_SUB if reverse else c_row < (s + 1) * GLA_SUB
                k_parts.append(kh * jnp.exp(jnp.where(reach, bd - b, -jnp.inf)))
            q_bd = jnp.concatenate(q_parts, axis=1).astype(BF16)
            k_cat = jnp.concatenate(k_parts, axis=1).astype(BF16)
            att = jnp.where(att_mask, _dot_nt(q_bd, k_cat), 0.0)
            o = _dot(att.astype(BF16), vh) + _dot_nt((qh * jnp.exp(b)).astype(BF16), st.astype(BF16))
            ke = (kh * jnp.exp(edge - b)).astype(BF16)
            st_ref[h] = st * jnp.exp(edge) + _dot_tn(vh, ke)
            ob_ref[rows, vcols] = o

    if reverse:
        for h in range(GLA_HEADS):
            vcols = slice(h * GLA_DV, (h + 1) * GLA_DV)
            o = ob_ref[:, vcols] + of_ref[:, vcols]
            o = o * lax.rsqrt(jnp.mean(o * o, axis=-1, keepdims=True) + EPS)
            o_ref[:, vcols] = (o * gn_ref[:, vcols] * _silu(r_ref[:, vcols])).astype(o_ref.dtype)

    ends = _is_first_block(blk) if reverse else _is_last_block(blk)

    @pl.when(ends)
    def _():
        sfin_ref[...] = st_ref[...]


def _gla(alr, q, k, v, wa, ba, s0t, *, reverse, o_fwd=None, r=None, gn=None):
    bmap = (lambda j: NBLK - 1 - j) if reverse else (lambda j: j)
    direction = 1 if reverse else 0
    qd, vd = GLA_HEADS * GLA_DK, GLA_HEADS * GLA_DV
    row_spec = lambda width: pl.BlockSpec((RB, width), lambda j: (bmap(j), 0))
    state_spec = pl.BlockSpec((None, None, GLA_HEADS, GLA_DV, GLA_DK),
                              lambda j: (_seq_of_block(bmap(j)), direction, 0, 0, 0))
    in_specs = [row_spec(2 * GLA_RANK), row_spec(qd), row_spec(qd), row_spec(vd),
                pl.BlockSpec((2 * GLA_RANK, qd), lambda j: (0, 0)),
                pl.BlockSpec((1, qd), lambda j: (0, 0)), state_spec]
    args = [alr, q, k, v, wa, ba, s0t]
    scratch = [pltpu.VMEM((GLA_HEADS, GLA_DV, GLA_DK), F32), pltpu.VMEM((RB, qd), F32)]
    if reverse:
        in_specs += [row_spec(vd), row_spec(vd), pl.BlockSpec((1, vd), lambda j: (0, 0))]
        args += [o_fwd, r, gn.reshape(1, vd)]
        scratch += [pltpu.VMEM((RB, vd), F32)]
    return pl.pallas_call(
        functools.partial(_gla_kernel, reverse=reverse),
        grid=(NBLK,),
        in_specs=in_specs,
        out_specs=[row_spec(vd),
                   pl.BlockSpec((None, GLA_HEADS, GLA_DV, GLA_DK),
                                lambda j: (_seq_of_block(bmap(j)), 0, 0, 0))],
        out_shape=[jax.ShapeDtypeStruct((NT, vd), BF16 if reverse else F32),
                   jax.ShapeDtypeStruct((NSEQ, GLA_HEADS, GLA_DV, GLA_DK), F32)],
        scratch_shapes=scratch,
        compiler_params=_params(),
        name="gla_bwd" if reverse else "gla_fwd",
    )(*args)


def _outproj_kernel(*refs, n_mix):
    mix_refs = refs[:n_mix]
    w_refs = refs[n_mix:2 * n_mix]
    (x_ref, gate_ref, g2_ref, shift_ref, scale_ref, rw_ref, rb_ref,
     x1_ref, h2_ref, idx_ref, rank_ref, gw_ref, cnt_ref, carry_ref) = refs[2 * n_mix:]
    step = pl.program_id(0)
    row = _mod_row(step)
    m = _dot(mix_refs[0][...], w_refs[0][...])
    for mix_ref, w_ref in zip(mix_refs[1:], w_refs[1:]):
        m = m + _dot(mix_ref[...], w_ref[...])
    x1 = x_ref[...] + gate_ref[pl.ds(row, 1), :] * m
    x1_ref[...] = x1
    h2 = _norm_mod(x1, g2_ref, shift_ref, scale_ref, row)
    h2_ref[...] = _pack_rows(h2)
    logits =_dot_hi(h2, rw_ref[...]) + rb_ref[...]

    @pl.when(step == 0)
    def _():
        carry_ref[...] = jnp.zeros_like(carry_ref)

    lane = lax.broadcasted_iota(jnp.int32, logits.shape, 1)
    work = logits
    onehots, top_vals = [], []
    for kk in range(TOP_K):
        top = jnp.max(work, axis=-1, keepdims=True)
        first = jnp.min(jnp.where(work == top, lane, N_EXPERTS), axis=-1, keepdims=True)
        onehot = lane == first
        idx_ref[:, kk:kk + 1] = first
        onehots.append(onehot)
        top_vals.append(top)
        work = jnp.where(onehot, -jnp.inf, work)
    exps = [jnp.exp(v - top_vals[0]) for v in top_vals]
    denom = exps[0]
    for e in exps[1:]:
        denom = denom + e
    for kk in range(TOP_K):
        gw_ref[:, kk:kk + 1] = exps[kk] / denom

    sel = jnp.zeros(logits.shape, F32)
    for onehot in onehots:
        sel = sel + jnp.where(onehot, 1.0, 0.0)
    r_i = lax.broadcasted_iota(jnp.int32, (RB, RB), 0)
    c_i = lax.broadcasted_iota(jnp.int32, (RB, RB), 1)
    earlier = jnp.where(c_i < r_i, 1.0, 0.0).astype(BF16)
    pos = _dot(earlier, sel.astype(BF16)) + carry_ref[...]
    for kk in range(TOP_K):
        rank = jnp.sum(jnp.where(onehots[kk], pos, 0.0), axis=-1, keepdims=True)
        rank_ref[:, kk:kk + 1] = rank.astype(jnp.int32)
    carry = carry_ref[...] + jnp.sum(sel, axis=0, keepdims=True)
    carry_ref[...] = carry
    cnt_ref[...] = carry.astype(jnp.int32)


def _outproj(mixes, w_parts, x, gate, g2, shift, scale, rw, rb):
    n_mix = len(mixes)
    row_spec = lambda width: pl.BlockSpec((RB, width), lambda i: (i, 0))
    full = lambda shape: pl.BlockSpec(shape, lambda i: (0,) * len(shape))
    mod_spec = full((MOD_ROWS, D_MODEL))
    return pl.pallas_call(
        functools.partial(_outproj_kernel, n_mix=n_mix),
        grid=(NBLK,),
        in_specs=[row_spec(m.shape[1]) for m in mixes] + [full(w.shape) for w in w_parts]
        + [row_spec(D_MODEL), mod_spec, full((1, D_MODEL)), mod_spec, mod_spec,
           full((D_MODEL, N_EXPERTS)), full((1, N_EXPERTS))],
        out_specs=[row_spec(D_MODEL), row_spec(D_MODEL // 2), row_spec(TOP_K), row_spec(TOP_K), row_spec(TOP_K),
                   full((1, N_EXPERTS))],
        out_shape=[jax.ShapeDtypeStruct((NT, D_MODEL), F32), jax.ShapeDtypeStruct((NT, D_MODEL // 2), jnp.uint32),
                   jax.ShapeDtypeStruct((NT, TOP_K), jnp.int32), jax.ShapeDtypeStruct((NT, TOP_K), jnp.int32),
                   jax.ShapeDtypeStruct((NT, TOP_K), F32), jax.ShapeDtypeStruct((1, N_EXPERTS), jnp.int32)],
        scratch_shapes=[pltpu.VMEM((1, N_EXPERTS), F32)],
        compiler_params=_params(),
        name="outproj",
    )(*mixes, *w_parts, x, gate, g2.reshape(1, D_MODEL), shift, scale, rw, rb.reshape(1, N_EXPERTS))


TM = 512
TM_SUB = 256
MOE_NBLK = NT * TOP_K // TM + N_EXPERTS
MOE_ROWS = MOE_NBLK * TM
HALF = D_MODEL // 2


def _moe_kernel(be_ref, nv_ref, nx_ref, x_ref, wgu_hbm, bgu_ref, wd_hbm, bd_ref, y_ref,
                wgu_st, wd_st, wgu_bf, wd_bf, sems, *, layer):
    i = pl.program_id(0)
    n_valid = nv_ref[i]

    def fetch(e):
        return (pltpu.make_async_copy(wgu_hbm.at[layer, e], wgu_st, sems.at[0]),
                pltpu.make_async_copy(wd_hbm.at[layer, e], wd_st, sems.at[1]))

    @pl.when(i == 0)
    def _():
        for cp in fetch(be_ref[0]):
            cp.start()

    @pl.when(n_valid > 0)
    def _():
        e = be_ref[i]
        changed = jnp.logical_or(i == 0, e != be_ref[jnp.maximum(i - 1, 0)])

        @pl.when(changed)
        def _():
            for cp in fetch(e):
                cp.wait()
            wgu_bf[...] = wgu_st[...].astype(BF16)
            wd_bf[...] = wd_st[...].astype(BF16)
            nxt = nx_ref[e]

            @pl.when(nxt >= 0)
            def _():
                for cp in fetch(nxt):
                    cp.start()

    for p in range(TM // TM_SUB):
        @pl.when(n_valid > p * TM_SUB)
        def _():
            rows = slice(p * TM_SUB, (p + 1) * TM_SUB)
            row_id = lax.broadcasted_iota(jnp.int32, (TM_SUB, 1), 0) + p * TM_SUB
            x_lo, x_hi = _unpack_rows(jnp.where(row_id < n_valid, x_ref[rows, :], jnp.uint32(0)))
            gu = (_dot(x_lo.astype(BF16), wgu_bf[:HALF, :]) + _dot(x_hi.astype(BF16), wgu_bf[HALF:, :])
                  + bgu_ref[...])
            gate = jnp.minimum(gu[:, :D_FF], SWIGLU_LIMIT)
            up = jnp.clip(gu[:, D_FF:], -SWIGLU_LIMIT, SWIGLU_LIMIT)
            hdn = gate * _sigmoid(SWIGLU_ALPHA * gate) * (up + 1.0)
            y_ref[rows, :] = _pack_rows(_dot(hdn.astype(BF16), wd_bf[...]) + bd_ref[...])


def _moe_experts(layer, block_e, n_valid, next_e, xs, w_gu, b_gu, w_down, b_down):
    grid_spec = pltpu.PrefetchScalarGridSpec(
        num_scalar_prefetch=3,
        grid=(MOE_NBLK,),
        in_specs=[pl.BlockSpec((TM, HALF), lambda i, be, nv, nx: (i, 0)),
                  pl.BlockSpec(memory_space=pl.ANY),
                  pl.BlockSpec((None, None, 1, 2 * D_FF), lambda i, be, nv, nx: (layer, be[i], 0, 0)),
                  pl.BlockSpec(memory_space=pl.ANY),
                  pl.BlockSpec((None, None, 1, D_MODEL), lambda i, be, nv, nx: (layer, be[i], 0, 0))],
        out_specs=pl.BlockSpec((TM, HALF), lambda i, be, nv, nx: (i, 0)),
        scratch_shapes=[pltpu.VMEM((D_MODEL, 2 * D_FF), F32), pltpu.VMEM((D_FF, D_MODEL), F32),
                        pltpu.VMEM((D_MODEL, 2 * D_FF), BF16), pltpu.VMEM((D_FF, D_MODEL), BF16),
                        pltpu.SemaphoreType.DMA((2,))],
    )
    return pl.pallas_call(
        functools.partial(_moe_kernel, layer=layer),
        grid_spec=grid_spec,
        out_shape=jax.ShapeDtypeStruct((MOE_ROWS, HALF), jnp.uint32),
        compiler_params=_params(),
        name="moe_experts",
    )(block_e, n_valid, next_e, xs, w_gu, b_gu.reshape(DEPTH, N_EXPERTS, 1, -1), w_down,
      b_down.reshape(DEPTH, N_EXPERTS, 1, -1))


SC_WORKERS = 32
SC_WIN = 64


def _sc_mesh():
    return plsc.VectorSubcoreMesh(core_axis_name="core", subcore_axis_name="subcore")


def _sc_worker():
    return lax.axis_index("core") * (SC_WORKERS // 2) + lax.axis_index("subcore")


def _sc_scatter_rows(x, dest_t, n_rows):
    n, width = x.shape
    kk = dest_t.shape[0]
    per = n // SC_WORKERS
    n_win = per // SC_WIN
    assert per * SC_WORKERS == n and n_win * SC_WIN == per and n_win % 2 == 0

    @pl.kernel(out_type=jax.ShapeDtypeStruct((n_rows, width), x.dtype), mesh=_sc_mesh(),
               scratch_types=[pltpu.VMEM((kk, per), jnp.int32), pltpu.VMEM((SC_WIN, width), x.dtype),
                              pltpu.VMEM((SC_WIN, width), x.dtype), pltpu.SemaphoreType.DMA((4,))])
    def scatter(x_hbm, i_hbm, o_hbm, idx_v, buf0, buf1, sems):
        base = _sc_worker() * per
        pltpu.sync_copy(i_hbm.at[:, pl.ds(base, per)], idx_v)

        def get(j, buf, s):
            return pltpu.make_async_copy(x_hbm.at[pl.ds(base + j * SC_WIN, SC_WIN)], buf, sems.at[s])

        def put(j, q, buf, s):
            return pltpu.make_async_copy(buf, o_hbm.at[idx_v.at[q, pl.ds(j * SC_WIN, SC_WIN)]], sems.at[s])

        get(0, buf0, 0).start()

        @pl.loop(0, n_win, step=2)
        def _(j):
            get(j, buf0, 0).wait()

            @pl.when(j > 0)
            def _():
                for q in range(kk):
                    put(j - 1, q, buf1, 3).wait()

            get(j + 1, buf1, 1).start()
            for q in range(kk):
                put(j, q, buf0, 2).start()
            get(j + 1, buf1, 1).wait()
            for q in range(kk):
                put(j, q, buf0, 2).wait()

            @pl.when(j + 2 < n_win)
            def _():
                get(j + 2, buf0, 0).start()

            for q in range(kk):
                put(j + 1, q, buf1, 3).start()

        for q in range(kk):
            put(n_win - 1, q, buf1, 3).wait()

    return scatter(x, dest_t)


def _sc_gather_rows(y, idx):
    n = idx.shape[0]
    width = y.shape[1]
    per = n // SC_WORKERS
    n_win = per // SC_WIN
    assert per * SC_WORKERS == n and n_win * SC_WIN == per and n_win % 2 == 0

    @pl.kernel(out_type=jax.ShapeDtypeStruct((n, width), y.dtype), mesh=_sc_mesh(),
               scratch_types=[pltpu.VMEM((per,), jnp.int32), pltpu.VMEM((SC_WIN, width), y.dtype),
                              pltpu.VMEM((SC_WIN, width), y.dtype), pltpu.SemaphoreType.DMA((4,))])
    def gather(y_hbm, i_hbm, o_hbm, idx_v, buf0, buf1, sems):
        base = _sc_worker() * per
        pltpu.sync_copy(i_hbm.at[pl.ds(base, per)], idx_v)

        def get(j, buf, s):
            return pltpu.make_async_copy(y_hbm.at[idx_v.at[pl.ds(j * SC_WIN, SC_WIN)]], buf, sems.at[s])

        def put(j, buf, s):
            return pltpu.make_async_copy(buf, o_hbm.at[pl.ds(base + j * SC_WIN, SC_WIN)], sems.at[s])

        get(0, buf0, 0).start()

        @pl.loop(0, n_win, step=2)
        def _(j):
            get(j, buf0, 0).wait()

            @pl.when(j > 0)
            def _():
                put(j - 1, buf1, 3).wait()

            get(j + 1, buf1, 1).start()
            put(j, buf0, 2).start()
            get(j + 1, buf1, 1).wait()
            put(j, buf0, 2).wait()

            @pl.when(j + 2 < n_win)
            def _():
                get(j + 2, buf0, 0).start()

            put(j + 1, buf1, 3).start()

        put(n_win - 1, buf1, 3).wait()

    return gather(y, idx)


def _combine_kernel(x1_ref, *rest):
    yg_refs = rest[:TOP_K]
    gw_ref, gate_ref, o_ref = rest[TOP_K:]
    row = _mod_row(pl.program_id(0))
    acc_lo, acc_hi = None, None
    for kk in range(TOP_K):
        y_lo, y_hi = _unpack_rows(yg_refs[kk][...])
        w = gw_ref[:, kk:kk + 1]
        acc_lo = y_lo * w if acc_lo is None else acc_lo + y_lo * w
        acc_hi = y_hi * w if acc_hi is None else acc_hi + y_hi * w
    o_ref[:, :HALF] = x1_ref[:, :HALF] + gate_ref[pl.ds(row, 1), :HALF] * acc_lo
    o_ref[:, HALF:] = x1_ref[:, HALF:] + gate_ref[pl.ds(row, 1), HALF:] * acc_hi


def _combine(x1, yg, gw, gate):
    row_spec = lambda width: pl.BlockSpec((RB, width), lambda i: (i, 0))
    slot_spec = lambda k: pl.BlockSpec((RB, HALF), lambda i: (k * NBLK + i, 0))
    return pl.pallas_call(
        _combine_kernel,
        grid=(NBLK,),
        in_specs=[row_spec(D_MODEL)] + [slot_spec(k) for k in range(TOP_K)]
        + [row_spec(TOP_K), pl.BlockSpec((MOD_ROWS, D_MODEL), lambda i: (0, 0))],
        out_specs=row_spec(D_MODEL),
        out_shape=jax.ShapeDtypeStruct((NT, D_MODEL), F32),
        compiler_params=_params(),
        name="moe_combine",
    )(x1, *([yg] * TOP_K), gw, gate)


def _final_norm_kernel(x_ref, g_ref, op_ref, os_ref):
    i = pl.program_id(0)
    x = x_ref[...]
    y = x * lax.rsqrt(jnp.mean(x * x, axis=-1, keepdims=True) + EPS) * g_ref[...]

    @pl.when(i < N_PROMPT_BLK)
    def _():
        op_ref[...] = y

    @pl.when(i >= N_PROMPT_BLK)
    def _():
        os_ref[...] = y


def _final_norm(x, g):
    row_spec = pl.BlockSpec((RB, D_MODEL), lambda i: (i, 0))
    return pl.pallas_call(
        _final_norm_kernel,
        grid=(NBLK,),
        in_specs=[row_spec, pl.BlockSpec((1, D_MODEL), lambda i: (0, 0))],
        out_specs=[pl.BlockSpec((RB, D_MODEL), lambda i: (jnp.minimum(i, N_PROMPT_BLK - 1), 0)),
                   pl.BlockSpec((RB, D_MODEL), lambda i: (jnp.maximum(i - N_PROMPT_BLK, 0), 0))],
        out_shape=[jax.ShapeDtypeStruct((NT_PROMPT, D_MODEL), F32),
                   jax.ShapeDtypeStruct((NT - NT_PROMPT, D_MODEL), F32)],
        compiler_params=_params(),
        name="final_norm",
    )(x, g.reshape(1, D_MODEL))


def _routing_plan(counts, idx4, rank4):
    counts = counts.reshape(N_EXPERTS)
    padded = (counts + TM - 1) // TM * TM
    pad_end = jnp.cumsum(padded)
    pad_start = pad_end - padded
    blk_row = (jnp.arange(MOE_NBLK, dtype=jnp.int32) * TM)[:, None]
    ids = jnp.arange(N_EXPERTS, dtype=jnp.int32)
    owns = jnp.logical_and(pad_start[None, :] <= blk_row, blk_row < pad_end[None, :])
    last_used = jnp.max(jnp.where(counts > 0, ids, 0))
    block_e = jnp.where(jnp.any(owns, axis=1), jnp.sum(jnp.where(owns, ids[None, :], 0), axis=1), last_used)
    block_e = block_e.astype(jnp.int32)
    left = jnp.clip(counts[None, :] - (blk_row - pad_start[None, :]), 0, TM)
    n_valid = jnp.sum(jnp.where(owns, left, 0), axis=1).astype(jnp.int32)
    dest4 = (pad_start[idx4] + rank4).astype(jnp.int32)
    later = jnp.where(jnp.logical_and(counts[None, :] > 0, ids[None, :] > ids[:, None]), ids[None, :], N_EXPERTS)
    next_e = jnp.min(later, axis=1)
    next_e = jnp.where(next_e == N_EXPERTS, -1, next_e).astype(jnp.int32)
    return block_e, n_valid, next_e, dest4


def _rope_tables():
    rows = DEC_SEQ // GRID_W
    row = jnp.repeat(jnp.arange(rows, dtype=F32), GRID_W)
    col = jnp.tile(jnp.arange(GRID_W, dtype=F32), rows)
    n_f = RET_DK // 4
    freqs = ROPE_THETA ** (-jnp.arange(n_f, dtype=F32) / n_f)
    ang = jnp.concatenate([row[:, None] * freqs, col[:, None] * freqs], axis=-1)
    cos = jnp.repeat(jnp.cos(ang), 2, axis=-1)
    sin = jnp.repeat(jnp.sin(ang), 2, axis=-1) * jnp.tile(jnp.asarray([-1.0, 1.0], F32), RET_DK // 2)
    cos = jnp.concatenate([jnp.ones((RB, RET_DK), F32), cos], axis=0)
    sin = jnp.concatenate([jnp.zeros((RB, RET_DK), F32), sin], axis=0)
    return jnp.tile(cos, (1, 2)), jnp.tile(sin, (1, 2))


def kernel(x_prompt, x_sample, state_ret, state_gla, c, c_ctx, w_mod, b_mod, norm1_g, norm2_g, final_g, even_w_in, ret_decay, ret_gn, conv_w, conv_b, conv_ln_g, conv_ln_b, even_w_out, odd_w_in, gla_w_a2, gla_b_a2, gla_gn, odd_w_out, router_w, router_b, exp_w_gu, exp_b_gu, exp_w_down, exp_b_down):
    x = jnp.concatenate([x_prompt.reshape(NT_PROMPT, D_MODEL), x_sample.reshape(-1, D_MODEL)], axis=0)
    cvec = jnp.concatenate([c_ctx[None, :], c, jnp.zeros((MOD_ROWS - 1 - DEC_BATCH, D_MODEL), F32)], axis=0)
    mods = _modulation(cvec, w_mod, b_mod).reshape(DEPTH, MOD_ROWS, N_MOD, D_MODEL)
    cos_tab, sin_tab = _rope_tables()
    new_ret, new_gla = [], []
    for l in range(DEPTH):
        mod = [mods[l, :, j, :] for j in range(N_MOD)]
        if l % 2 == 0:
            e = l // 2
            qd, vd = RET_HEADS * RET_DK, RET_HEADS * RET_DV
            q, k, v, g, a, ga = _inproj(x, norm1_g[l], mod[0], mod[1], even_w_in[e].astype(BF16),
                                        (qd, qd, vd, vd, CONV_CH, CONV_CH))
            s0 = jnp.concatenate([jnp.zeros((BATCH,) + state_ret.shape[2:], F32), state_ret[:, e]], axis=0)
            o_f, s_f = _retention(ret_decay[e], q, k, v, cos_tab, sin_tab, s0, reverse=False)
            ret, s_b = _retention(ret_decay[e], q, k, v, cos_tab, sin_tab, s0, reverse=True,
                                  o_fwd=o_f, g=g, gn=ret_gn[e])
            u = _conv_module(a, ga, conv_w[e], conv_b[e], conv_ln_g[e], conv_ln_b[e])
            w_out = even_w_out[e].astype(BF16)
            mixes, w_parts = [ret, u], [w_out[:vd], w_out[vd:]]
            new_ret.append(jnp.stack([s_f[:BATCH], s_b[:BATCH]], axis=1))
        else:
            o = l // 2
            qd, vd = GLA_HEADS * GLA_DK, GLA_HEADS * GLA_DV
            q, k, v, r, alr = _inproj(x, norm1_g[l], mod[0], mod[1], odd_w_in[o].astype(BF16),
                                      (qd, qd, vd, vd, 2 * GLA_RANK))
            s0t = jnp.concatenate([jnp.zeros((BATCH,) + state_gla.shape[2:], F32), state_gla[:, o]], axis=0)
            s0t = jnp.swapaxes(s0t, -1, -2)
            zeros = jnp.zeros((GLA_RANK, qd), F32)
            wa_f = jnp.concatenate([gla_w_a2[o, 0], zeros], axis=0)
            wa_b = jnp.concatenate([zeros, gla_w_a2[o, 1]], axis=0)
            o_f, s_f = _gla(alr, q, k, v, wa_f, gla_b_a2[o, 0].reshape(1, qd), s0t, reverse=False)
            y, s_b = _gla(alr, q, k, v, wa_b, gla_b_a2[o, 1].reshape(1, qd), s0t, reverse=True,
                          o_fwd=o_f, r=r, gn=gla_gn[o])
            mixes, w_parts = [y], [odd_w_out[o].astype(BF16)]
            new_gla.append(jnp.swapaxes(jnp.stack([s_f[:BATCH], s_b[:BATCH]], axis=1), -1, -2))
        x1, h2, idx4, rank4, gates, counts = _outproj(mixes, w_parts, x, mod[2], norm2_g[l], mod[3], mod[4],
                                                      router_w[l], router_b[l])
        block_e, n_valid, next_e, dest4 = _routing_plan(counts, idx4, rank4)
        dest_t = dest4.T
        xs = _sc_scatter_rows(h2, dest_t, MOE_ROWS)
        yb = _moe_experts(l, block_e, n_valid, next_e, xs, exp_w_gu, exp_b_gu, exp_w_down, exp_b_down)
        yg = _sc_gather_rows(yb, dest_t.reshape(TOP_K * NT))
        x = _combine(x1, yg, gates, mod[5])
    y_prompt, y_sample = _final_norm(x, final_g)
    y_prompt = y_prompt.reshape(BATCH, SEQ, D_MODEL)
    y_sample = y_sample.reshape(DEC_BATCH, DEC_SEQ, D_MODEL)
    return (y_prompt, y_sample, jnp.stack(new_ret, axis=1), jnp.stack(new_gla, axis=1))
```

```python
import functools

import jax
import jax.numpy as jnp
from jax import lax
from jax.experimental import pallas as pl
from jax.experimental.pallas import tpu as pltpu
from jax.experimental.pallas import tpu_sc as plsc

F32 = jnp.float32
BF16 = jnp.bfloat16

D_MODEL = 1024
BATCH = 16
SEQ = 256
DEPTH = 4
DEC_BATCH = 4
DEC_SEQ = 4096
GRID_W = 64
RET_HEADS = 4
RET_DK = 64
RET_DV = 128
RET_CHUNK = 128
CONV_CH = 512
CONV_WIDTH = 31
CONV_PAD = CONV_WIDTH // 2
GLA_HEADS = 4
GLA_DK = 128
GLA_DV = 256
GLA_RANK = 16
GLA_TAU = 16.0
GLA_CHUNK = 64
GLA_SUB = 16
N_EXPERTS = 32
TOP_K = 4
D_FF = 1024
SWIGLU_LIMIT = 7.0
SWIGLU_ALPHA = 1.702
MOE_BLOCK = 128
ROPE_THETA = 10000.0
EPS = 1e-6
N_MOD = 6

RB = 256
NT_PROMPT = BATCH * SEQ
NT = NT_PROMPT + DEC_BATCH * DEC_SEQ
NBLK = NT // RB
N_PROMPT_BLK = NT_PROMPT // RB
SAMPLE_BLK = DEC_SEQ // RB
NSEQ = BATCH + DEC_BATCH
MOD_ROWS = 8
HALO = 16
VMEM_LIMIT = 48 * 1024 * 1024

assert SEQ == RB and DEC_SEQ % RB == 0 and CONV_PAD < HALO


def _seq_of_block(i):
    return jnp.where(i < N_PROMPT_BLK, i, N_PROMPT_BLK + (i - N_PROMPT_BLK) // SAMPLE_BLK)


def _is_first_block(i):
    return jnp.logical_or(i < N_PROMPT_BLK, (i - N_PROMPT_BLK) % SAMPLE_BLK == 0)


def _is_last_block(i):
    return jnp.logical_or(i < N_PROMPT_BLK, (i - N_PROMPT_BLK) % SAMPLE_BLK == SAMPLE_BLK - 1)


def _mod_row(i):
    return jnp.where(i < N_PROMPT_BLK, 0, 1 + (i - N_PROMPT_BLK) // SAMPLE_BLK)


def _rope_block(i):
    return jnp.where(i < N_PROMPT_BLK, 0, 1 + (i - N_PROMPT_BLK) % SAMPLE_BLK)


def _dot(a, b):
    return jnp.dot(a, b, preferred_element_type=F32)


def _dot_nt(a, b):
    return lax.dot_general(a, b, (((1,), (1,)), ((), ())), preferred_element_type=F32)


def _dot_tn(a, b):
    return lax.dot_general(a, b, (((0,), (0,)), ((), ())), preferred_element_type=F32)


def _split2(a):
    hi = a.astype(BF16)
    lo = (a - hi.astype(F32)).astype(BF16)
    return hi, lo


def _dot_hi(a, b):
    a_hi, a_lo = _split2(a)
    b_hi, b_lo = _split2(b)
    return _dot(a_hi, b_hi) + (_dot(a_hi, b_lo) + _dot(a_lo, b_hi))


def _silu(x):
    return x * (1.0 / (1.0 + jnp.exp(-x)))


def _sigmoid(x):
    return 1.0 / (1.0 + jnp.exp(-x))


def _pack_rows(x):
    n = x.shape[1] // 2
    lo = pltpu.bitcast(x[:, :n].astype(BF16).astype(F32), jnp.uint32)
    hi = pltpu.bitcast(x[:, n:].astype(BF16).astype(F32), jnp.uint32)
    return hi | (lo >> 16)


def _unpack_rows(u):
    lo = pltpu.bitcast(u << 16, F32)
    hi = pltpu.bitcast(u & jnp.uint32(0xFFFF0000), F32)
    return lo, hi


def _params(n_axes=1, vmem=VMEM_LIMIT):
    return pltpu.CompilerParams(dimension_semantics=("arbitrary",) * n_axes, vmem_limit_bytes=vmem)


MOD_TN = 1536


def _mod_kernel(c_ref, w_ref, b_ref, o_ref):
    s = _silu(c_ref[...]).astype(BF16)
    o_ref[...] = _dot(s, w_ref[...].astype(BF16)) + b_ref[...]


def _modulation(cvec, w_mod, b_mod):
    n = N_MOD * D_MODEL
    return pl.pallas_call(
        _mod_kernel,
        grid=(DEPTH, n // MOD_TN),
        in_specs=[pl.BlockSpec((MOD_ROWS, D_MODEL), lambda l, j: (0, 0)),
                  pl.BlockSpec((None, D_MODEL, MOD_TN), lambda l, j: (l, 0, j)),
                  pl.BlockSpec((None, 1, MOD_TN), lambda l, j: (l, 0, j))],
        out_specs=pl.BlockSpec((None, MOD_ROWS, MOD_TN), lambda l, j: (l, 0, j)),
        out_shape=jax.ShapeDtypeStruct((DEPTH, MOD_ROWS, n), F32),
        compiler_params=_params(2),
        name="modulation",
    )(cvec, w_mod, b_mod.reshape(DEPTH, 1, n))


def _norm_mod(x, g_ref, shift_ref, scale_ref, row):
    y = x * lax.rsqrt(jnp.mean(x * x, axis=-1, keepdims=True) + EPS) * g_ref[...]
    return y * (1.0 + scale_ref[pl.ds(row, 1), :]) + shift_ref[pl.ds(row, 1), :]


def _inproj_kernel(x_ref, g_ref, shift_ref, scale_ref, w_ref, *o_refs, widths):
    row = _mod_row(pl.program_id(0))
    hb = _norm_mod(x_ref[...], g_ref, shift_ref, scale_ref, row).astype(BF16)
    off = 0
    for o_ref, width in zip(o_refs, widths):
        o_ref[...] = _dot(hb, w_ref[:, off:off + width])
        off += width


def _inproj(x, g, shift, scale, w_bf, widths):
    n_in = w_bf.shape[1]
    row_spec = lambda width: pl.BlockSpec((RB, width), lambda i: (i, 0))
    full = lambda shape: pl.BlockSpec(shape, lambda i: (0,) * len(shape))
    return pl.pallas_call(
        functools.partial(_inproj_kernel, widths=widths),
        grid=(NBLK,),
        in_specs=[row_spec(D_MODEL), full((1, D_MODEL)), full((MOD_ROWS, D_MODEL)),
                  full((MOD_ROWS, D_MODEL)), full((D_MODEL, n_in))],
        out_specs=[row_spec(width) for width in widths],
        out_shape=[jax.ShapeDtypeStruct((NT, width), F32) for width in widths],
        compiler_params=_params(),
        name="inproj",
    )(x, g.reshape(1, D_MODEL), shift, scale, w_bf)


RC = RET_CHUNK
RET_PAIR = 2 * RET_DK


def _rope(x, cos, sin_signed):
    lane = lax.broadcasted_iota(jnp.int32, x.shape, 1)
    swapped = jnp.where(lane % 2 == 0, pltpu.roll(x, x.shape[1] - 1, 1), pltpu.roll(x, 1, 1))
    return x * cos + swapped * sin_signed


def _ret_kernel(decay_ref, q_ref, k_ref, v_ref, cos_ref, sin_ref, s0_ref, *rest, reverse):
    if reverse:
        of_ref, g_ref, gn_ref, o_ref, sfin_ref, st_ref, dm_ref, dq_ref, dk_ref, ds_ref = rest
    else:
        o_ref, sfin_ref, st_ref, dm_ref, dq_ref, dk_ref, ds_ref = rest
    step = pl.program_id(0)
    blk = NBLK - 1 - step if reverse else step
    direction = 1 if reverse else 0

    @pl.when(step == 0)
    def _():
        row = lax.broadcasted_iota(jnp.int32, (RC, RC), 0).astype(F32)
        col = lax.broadcasted_iota(jnp.int32, (RC, RC), 1).astype(F32)
        for h in range(RET_HEADS):
            lg = -jnp.exp(jnp.full((RC, RC), decay_ref[direction, h], F32))
            if reverse:
                diff = col - row
                mask = diff > 0
                q_pow = RC - row
                k_pow = row
            else:
                diff = row - col
                mask = diff >= 0
                q_pow = row + 1.0
                k_pow = RC - 1.0 - row
            dm_ref[h] = jnp.where(mask, jnp.exp(lg * jnp.where(mask, diff, 0.0)), 0.0)
            dq_ref[h] = jnp.exp(lg * q_pow)
            dk_ref[h] = jnp.exp(lg * k_pow)
            ds_ref[h] = jnp.exp(lg * RC)

    starts = _is_last_block(blk) if reverse else _is_first_block(blk)

    @pl.when(starts)
    def _():
        st_ref[...] = jnp.zeros_like(st_ref)
        for h in range(RET_HEADS):
            off = (h % 2) * RET_DK
            st_ref[h, off:off + RET_DK, :] = s0_ref[h]

    lane = lax.broadcasted_iota(jnp.int32, (1, RET_PAIR), 1)
    chunks = range(RB // RC)
    for c in (reversed(chunks) if reverse else chunks):
        rows = slice(c * RC, (c + 1) * RC)
        cos = cos_ref[rows, :]
        sin = sin_ref[rows, :]
        for p in range(RET_HEADS // 2):
            cols = slice(p * RET_PAIR, (p + 1) * RET_PAIR)
            q2 = _rope(q_ref[rows, cols], cos, sin)
            k2 = _rope(k_ref[rows, cols] * (RET_DK ** -0.5), cos, sin)
            for h in (2 * p, 2 * p + 1):
                head_mask = (lane // RET_DK == h % 2).astype(F32)
                vh = v_ref[rows, h * RET_DV:(h + 1) * RET_DV].astype(BF16)
                qm = (q2 * head_mask).astype(BF16)
                km = k2 * head_mask
                att = _dot_nt(qm, km.astype(BF16)) * dm_ref[h]
                o = _dot(att.astype(BF16), vh) + _dot(qm, st_ref[h].astype(BF16)) * dq_ref[h]
                st_ref[h] = st_ref[h] * ds_ref[h] + _dot_tn((km * dk_ref[h]).astype(BF16), vh)
                out_cols = slice(h * RET_DV, (h + 1) * RET_DV)
                if reverse:
                    o = o + of_ref[rows, out_cols]
                    o = o * lax.rsqrt(jnp.mean(o * o, axis=-1, keepdims=True) + EPS)
                    o = o * gn_ref[:, out_cols] * _silu(g_ref[rows, out_cols])
                    o_ref[rows, out_cols] = o.astype(o_ref.dtype)
                else:
                    o_ref[rows, out_cols] = o

    ends = _is_first_block(blk) if reverse else _is_last_block(blk)

    @pl.when(ends)
    def _():
        for h in range(RET_HEADS):
            off = (h % 2) * RET_DK
            sfin_ref[h] = st_ref[h, off:off + RET_DK, :]


def _retention(decay, q, k, v, cos_tab, sin_tab, s0, *, reverse, o_fwd=None, g=None, gn=None):
    bmap = (lambda j: NBLK - 1 - j) if reverse else (lambda j: j)
    direction = 1 if reverse else 0
    qd, vd = RET_HEADS * RET_DK, RET_HEADS * RET_DV
    row_spec = lambda width: pl.BlockSpec((RB, width), lambda j: (bmap(j), 0))
    state_spec = pl.BlockSpec((None, None, RET_HEADS, RET_DK, RET_DV),
                              lambda j: (_seq_of_block(bmap(j)), direction, 0, 0, 0))
    in_specs = [pl.BlockSpec(memory_space=pltpu.SMEM), row_spec(qd), row_spec(qd), row_spec(vd),
                pl.BlockSpec((RB, RET_PAIR), lambda j: (_rope_block(bmap(j)), 0)),
                pl.BlockSpec((RB, RET_PAIR), lambda j: (_rope_block(bmap(j)), 0)),
                state_spec]
    args = [decay, q, k, v, cos_tab, sin_tab, s0]
    if reverse:
        in_specs += [row_spec(vd), row_spec(vd), pl.BlockSpec((1, vd), lambda j: (0, 0))]
        args += [o_fwd, g, gn.reshape(1, vd)]
    tile = pltpu.VMEM((RET_HEADS, RC, RC), F32)
    return pl.pallas_call(
        functools.partial(_ret_kernel, reverse=reverse),
        grid=(NBLK,),
        in_specs=in_specs,
        out_specs=[row_spec(vd),
                   pl.BlockSpec((None, RET_HEADS, RET_DK, RET_DV),
                                lambda j: (_seq_of_block(bmap(j)), 0, 0, 0))],
        out_shape=[jax.ShapeDtypeStruct((NT, vd), BF16 if reverse else F32),
                   jax.ShapeDtypeStruct((NSEQ, RET_HEADS, RET_DK, RET_DV), F32)],
        scratch_shapes=[tile, tile, tile, tile, tile],
        compiler_params=_params(),
        name="retention_bwd" if reverse else "retention_fwd",
    )(*args)


CONV_RT = 32
CONV_CT = 128


def _conv_kernel(a_ref, ga_ref, ap_ref, gap_ref, an_ref, gan_ref, cw_ref, cb_ref, lng_ref, lnb_ref,
                 o_ref, u_ref, y_ref):
    blk = pl.program_id(0)
    keep_prev = jnp.where(_is_first_block(blk), 0.0, 1.0)
    keep_next = jnp.where(_is_last_block(blk), 0.0, 1.0)
    u_ref[0:HALO, :] = ap_ref[...] * _sigmoid(gap_ref[...]) * keep_prev
    u_ref[HALO:HALO + RB, :] = a_ref[...] * _sigmoid(ga_ref[...])
    u_ref[HALO + RB:HALO + RB + HALO, :] = an_ref[...] * _sigmoid(gan_ref[...]) * keep_next
    for ct in range(CONV_CH // CONV_CT):
        cols = slice(ct * CONV_CT, (ct + 1) * CONV_CT)
        for rt in range(RB // CONV_RT):
            base = HALO - CONV_PAD + rt * CONV_RT
            acc = jnp.zeros((CONV_RT, CONV_CT), F32)
            for w in range(CONV_WIDTH):
                acc = acc + u_ref[base + w:base + w + CONV_RT, cols] * cw_ref[w:w + 1, cols]
            y_ref[rt * CONV_RT:(rt + 1) * CONV_RT, cols] = acc + cb_ref[:, cols]
    y = y_ref[...]
    mu = jnp.mean(y, axis=-1, keepdims=True)
    var = jnp.mean(jnp.square(y - mu), axis=-1, keepdims=True)
    o_ref[...] = _silu((y - mu) * lax.rsqrt(var + EPS) * lng_ref[...] + lnb_ref[...]).astype(o_ref.dtype)


def _conv_module(a, ga, cw, cb, lng, lnb):
    per_blk = RB // HALO
    n_halo = NT // HALO
    row_spec = pl.BlockSpec((RB, CONV_CH), lambda i: (i, 0))
    prev_spec = pl.BlockSpec((HALO, CONV_CH), lambda i: (jnp.maximum(i * per_blk - 1, 0), 0))
    next_spec = pl.BlockSpec((HALO, CONV_CH), lambda i: (jnp.minimum((i + 1) * per_blk, n_halo - 1), 0))
    vec = pl.BlockSpec((1, CONV_CH), lambda i: (0, 0))
    return pl.pallas_call(
        _conv_kernel,
        grid=(NBLK,),
        in_specs=[row_spec, row_spec, prev_spec, prev_spec, next_spec, next_spec,
                  pl.BlockSpec((CONV_WIDTH, CONV_CH), lambda i: (0, 0)), vec, vec, vec],
        out_specs=row_spec,
        out_shape=jax.ShapeDtypeStruct((NT, CONV_CH), BF16),
        scratch_shapes=[pltpu.VMEM((RB + 2 * HALO, CONV_CH), F32), pltpu.VMEM((RB, CONV_CH), F32)],
        compiler_params=_params(),
        name="conv_module",
    )(a, ga, a, ga, a, ga, cw, cb.reshape(1, -1), lng.reshape(1, -1), lnb.reshape(1, -1))


GC = GLA_CHUNK
GLA_NSUB = GC // GLA_SUB


def _split3(a):
    p1 = a.astype(BF16)
    r1 = a - p1.astype(F32)
    p2 = r1.astype(BF16)
    p3 = (r1 - p2.astype(F32)).astype(BF16)
    return p1, p2, p3


def _gla_kernel(alr_ref, q_ref, k_ref, v_ref, wa_ref, ba_ref, s0_ref, *rest, reverse):
    if reverse:
        of_ref, r_ref, gn_ref, o_ref, sfin_ref, st_ref, b_ref, ob_ref = rest
    else:
        o_ref, sfin_ref, st_ref, b_ref = rest
        ob_ref = o_ref
    step = pl.program_id(0)
    blk = NBLK - 1 - step if reverse else step
    starts = _is_last_block(blk) if reverse else _is_first_block(blk)

    @pl.when(starts)
    def _():
        st_ref[...] = s0_ref[...]

    z = _dot_hi(alr_ref[...], wa_ref[...]) + ba_ref[...]
    log_a = (jnp.minimum(z, 0.0) - jnp.log(1.0 + jnp.exp(-jnp.abs(z)))) * (1.0 / GLA_TAU)
    row = lax.broadcasted_iota(jnp.int32, (RB, RB), 0)
    col = lax.broadcasted_iota(jnp.int32, (RB, RB), 1)
    ordered = col >= row if reverse else col <= row
    tri = jnp.where(jnp.logical_and(row // GC == col // GC, ordered), 1.0, 0.0).astype(BF16)
    g1, g2, g3 = _split3(log_a)
    b_ref[...] = _dot(tri, g1) + (_dot(tri, g2) + _dot(tri, g3))

    c_row = lax.broadcasted_iota(jnp.int32, (GC, 1), 0)
    a_row = lax.broadcasted_iota(jnp.int32, (GC, GC), 0)
    a_col = lax.broadcasted_iota(jnp.int32, (GC, GC), 1)
    att_mask = a_col > a_row if reverse else a_col <= a_row
    chunks = range(RB // GC)
    for c in (reversed(chunks) if reverse else chunks):
        rows = slice(c * GC, (c + 1) * GC)
        for h in range(GLA_HEADS):
            kcols = slice(h * GLA_DK, (h + 1) * GLA_DK)
            vcols = slice(h * GLA_DV, (h + 1) * GLA_DV)
            b = b_ref[rows, kcols]
            qh = q_ref[rows, kcols] * (GLA_DK ** -0.5)
            kh = k_ref[rows, kcols]
            vh = v_ref[rows, vcols].astype(BF16)
            st = st_ref[h]
            edge = b[0:1, :] if reverse else b[GC - 1:GC, :]
            bounds = []
            for s in range(GLA_NSUB):
                if reverse:
                    hi = (s + 1) * GLA_SUB
                    bounds.append(b[hi:hi + 1, :] if s < GLA_NSUB - 1 else jnp.zeros((1, GLA_DK), F32))
                else:
                    lo = s * GLA_SUB
                    bounds.append(b[lo - 1:lo, :] if s > 0 else jnp.zeros((1, GLA_DK), F32))
            own = jnp.concatenate([jnp.broadcast_to(bd, (GLA_SUB, GLA_DK)) for bd in bounds], axis=0)
            q_own = qh * jnp.exp(b - own)
            q_parts, k_parts = [], []
            for s, bd in enumerate(bounds):
                q_parts.append(jnp.where(c_row // GLA_SUB == s, q_own, 0.0))
                reach = c_row >= s * GLA_SUB if reverse else c_row < (s + 1) * GLA_SUB
                k_parts.append(kh * jnp.exp(jnp.where(reach, bd - b, -jnp.inf)))
            q_bd = jnp.concatenate(q_parts, axis=1).astype(BF16)
            k_cat = jnp.concatenate(k_parts, axis=1).astype(BF16)
            att = jnp.where(att_mask, _dot_nt(q_bd, k_cat), 0.0)
            o = _dot(att.astype(BF16), vh) + _dot_nt((qh * jnp.exp(b)).astype(BF16), st.astype(BF16))
            ke = (kh * jnp.exp(edge - b)).astype(BF16)
            st_ref[h] = st * jnp.exp(edge) + _dot_tn(vh, ke)
            ob_ref[rows, vcols] = o

    if reverse:
        for h in range(GLA_HEADS):
            vcols = slice(h * GLA_DV, (h + 1) * GLA_DV)
            o = ob_ref[:, vcols] + of_ref[:, vcols]
            o = o * lax.rsqrt(jnp.mean(o * o, axis=-1, keepdims=True) + EPS)
            o_ref[:, vcols] = (o * gn_ref[:, vcols] * _silu(r_ref[:, vcols])).astype(o_ref.dtype)

    ends = _is_first_block(blk) if reverse else _is_last_block(blk)

    @pl.when(ends)
    def _():
        sfin_ref[...] = st_ref[...]


def _gla(alr, q, k, v, wa, ba, s0t, *, reverse, o_fwd=None, r=None, gn=None):
    bmap = (lambda j: NBLK - 1 - j) if reverse else (lambda j: j)
    direction = 1 if reverse else 0
    qd, vd = GLA_HEADS * GLA_DK, GLA_HEADS * GLA_DV
    row_spec = lambda width: pl.BlockSpec((RB, width), lambda j: (bmap(j), 0))
    state_spec = pl.BlockSpec((None, None, GLA_HEADS, GLA_DV, GLA_DK),
                              lambda j: (_seq_of_block(bmap(j)), direction, 0, 0, 0))
    in_specs = [row_spec(2 * GLA_RANK), row_spec(qd), row_spec(qd), row_spec(vd),
                pl.BlockSpec((2 * GLA_RANK, qd), lambda j: (0, 0)),
                pl.BlockSpec((1, qd), lambda j: (0, 0)), state_spec]
    args = [alr, q, k, v, wa, ba, s0t]
    scratch = [pltpu.VMEM((GLA_HEADS, GLA_DV, GLA_DK), F32), pltpu.VMEM((RB, qd), F32)]
    if reverse:
        in_specs += [row_spec(vd), row_spec(vd), pl.BlockSpec((1, vd), lambda j: (0, 0))]
        args += [o_fwd, r, gn.reshape(1, vd)]
        scratch += [pltpu.VMEM((RB, vd), F32)]
    return pl.pallas_call(
        functools.partial(_gla_kernel, reverse=reverse),
        grid=(NBLK,),
        in_specs=in_specs,
        out_specs=[row_spec(vd),
                   pl.BlockSpec((None, GLA_HEADS, GLA_DV, GLA_DK),
                                lambda j: (_seq_of_block(bmap(j)), 0, 0, 0))],
        out_shape=[jax.ShapeDtypeStruct((NT, vd), BF16 if reverse else F32),
                   jax.ShapeDtypeStruct((NSEQ, GLA_HEADS, GLA_DV, GLA_DK), F32)],
        scratch_shapes=scratch,
        compiler_params=_params(),
        name="gla_bwd" if reverse else "gla_fwd",
    )(*args)


def _outproj_kernel(*refs, n_mix):
    mix_refs = refs[:n_mix]
    w_refs = refs[n_mix:2 * n_mix]
    (x_ref, gate_ref, g2_ref, shift_ref, scale_ref, rw_ref, rb_ref,
     x1_ref, h2_ref, idx_ref, rank_ref, gw_ref, cnt_ref, carry_ref) = refs[2 * n_mix:]
    step = pl.program_id(0)
    row = _mod_row(step)
    m = _dot(mix_refs[0][...], w_refs[0][...])
    for mix_ref, w_ref in zip(mix_refs[1:], w_refs[1:]):
        m = m + _dot(mix_ref[...], w_ref[...])
    x1 = x_ref[...] + gate_ref[pl.ds(row, 1), :] * m
    x1_ref[...] = x1
    h2 = _norm_mod(x1, g2_ref, shift_ref, scale_ref, row)
    h2_ref[...] = _pack_rows(h2)
    h_hi, h_lo = _split2(h2)
    r_hi, r_lo = _split2(rw_ref[...])
    logits = _dot_nt(r_hi, h_hi) + (_dot_nt(r_hi, h_lo) + _dot_nt(r_lo, h_hi)) + rb_ref[...]

    @pl.when(step == 0)
    def _():
        carry_ref[...] = jnp.zeros_like(carry_ref)

    eid = lax.broadcasted_iota(jnp.int32, logits.shape, 0).astype(F32)
    work = logits
    onehots, top_vals = [], []
    for kk in range(TOP_K):
        top = jnp.max(work, axis=0, keepdims=True)
        first = jnp.min(jnp.where(work == top, eid, float(N_EXPERTS)), axis=0, keepdims=True)
        onehot = eid == first
        idx_ref[kk:kk + 1, :] = first.astype(jnp.int32)
        onehots.append(onehot)
        top_vals.append(top)
        work = jnp.where(onehot, -jnp.inf, work)
    exps = [jnp.exp(v - top_vals[0]) for v in top_vals]
    denom = exps[0]
    for e in exps[1:]:
        denom = denom + e
    slot = lax.broadcasted_iota(jnp.int32, (8, RB), 0)
    gates = jnp.zeros((8, RB), F32)
    for kk in range(TOP_K):
        gates = jnp.where(slot == kk, exps[kk] / denom, gates)
    r_i = lax.broadcasted_iota(jnp.int32, (RB, RB), 0)
    c_i = lax.broadcasted_iota(jnp.int32, (RB, RB), 1)
    eye = jnp.where(r_i == c_i, 1.0, 0.0).astype(BF16)
    g1, g2_, g3 = _split3(gates)
    gw_ref[...] = (_dot_nt(eye, g1) + (_dot_nt(eye, g2_) + _dot_nt(eye, g3)))[:, :TOP_K]

    sel = jnp.zeros(logits.shape, F32)
    for onehot in onehots:
        sel = sel + jnp.where(onehot, 1.0, 0.0)
    sel = sel.astype(BF16)
    before = jnp.where(r_i < c_i, 1.0, 0.0).astype(BF16)
    pos = _dot(sel, before) + carry_ref[...]
    for kk in range(TOP_K):
        rank = jnp.sum(jnp.where(onehots[kk], pos, 0.0), axis=0, keepdims=True)
        rank_ref[kk:kk + 1, :] = rank.astype(jnp.int32)
    carry = carry_ref[...] + _dot(sel, jnp.ones((RB, RB), BF16))
    carry_ref[...] = carry
    cnt_ref[...] = carry.astype(jnp.int32)


def _outproj(mixes, w_parts, x, gate, g2, shift, scale, rw, rb):
    n_mix = len(mixes)
    row_spec = lambda width: pl.BlockSpec((RB, width), lambda i: (i, 0))
    col_spec = pl.BlockSpec((TOP_K, RB), lambda i: (0, i))
    full = lambda shape: pl.BlockSpec(shape, lambda i: (0,) * len(shape))
    mod_spec = full((MOD_ROWS, D_MODEL))
    return pl.pallas_call(
        functools.partial(_outproj_kernel, n_mix=n_mix),
        grid=(NBLK,),
        in_specs=[row_spec(m.shape[1]) for m in mixes] + [full(w.shape) for w in w_parts]
        + [row_spec(D_MODEL), mod_spec, full((1, D_MODEL)), mod_spec, mod_spec,
           full((N_EXPERTS, D_MODEL)), full((N_EXPERTS, 1))],
        out_specs=[row_spec(D_MODEL), row_spec(D_MODEL // 2), col_spec, col_spec, row_spec(TOP_K),
                   full((N_EXPERTS, RB))],
        out_shape=[jax.ShapeDtypeStruct((NT, D_MODEL), F32), jax.ShapeDtypeStruct((NT, D_MODEL // 2), jnp.uint32),
                   jax.ShapeDtypeStruct((TOP_K, NT), jnp.int32), jax.ShapeDtypeStruct((TOP_K, NT), jnp.int32),
                   jax.ShapeDtypeStruct((NT, TOP_K), F32), jax.ShapeDtypeStruct((N_EXPERTS, RB), jnp.int32)],
        scratch_shapes=[pltpu.VMEM((N_EXPERTS, RB), F32)],
        compiler_params=_params(),
        name="outproj",
    )(*mixes, *w_parts, x, gate, g2.reshape(1, D_MODEL), shift, scale, rw.T, rb.reshape(N_EXPERTS, 1))


TM = 512
TM_SUB = 256
MOE_NBLK = NT * TOP_K // TM + N_EXPERTS
MOE_ROWS = MOE_NBLK * TM
HALF = D_MODEL // 2
MOE_TN = 256


def _moe_kernel(be_ref, nv_ref, nx_ref, x_ref, wgu_hbm, bgu_ref, wd_hbm, bd_ref, y_ref,
                wgu_st, wd_st, wgu_bf, wd_bf, hdn_ref, sems, *, layer):
    i = pl.program_id(0)
    n_valid = nv_ref[i]

    def fetch(e):
        return (pltpu.make_async_copy(wgu_hbm.at[layer, e], wgu_st, sems.at[0]),
                pltpu.make_async_copy(wd_hbm.at[layer, e], wd_st, sems.at[1]))

    @pl.when(i == 0)
    def _():
        for cp in fetch(be_ref[0]):
            cp.start()

    @pl.when(n_valid > 0)
    def _():
        e = be_ref[i]
        changed = jnp.logical_or(i == 0, e != be_ref[jnp.maximum(i - 1, 0)])

        @pl.when(changed)
        def _():
            for cp in fetch(e):
                cp.wait()
            wgu_bf[...] = wgu_st[...].astype(BF16)
            wd_bf[...] = wd_st[...].astype(BF16)
            nxt = nx_ref[e]

            @pl.when(nxt >= 0)
            def _():
                for cp in fetch(nxt):
                    cp.start()

    for p in range(TM // TM_SUB):
        @pl.when(n_valid > p * TM_SUB)
        def _():
            rows = slice(p * TM_SUB, (p + 1) * TM_SUB)
            row_id = lax.broadcasted_iota(jnp.int32, (TM_SUB, 1), 0) + p * TM_SUB
            x_lo, x_hi = _unpack_rows(jnp.where(row_id < n_valid, x_ref[rows, :], jnp.uint32(0)))
            x_lo, x_hi = x_lo.astype(BF16), x_hi.astype(BF16)

            def proj(cols):
                return _dot(x_lo, wgu_bf[:HALF, cols]) + _dot(x_hi, wgu_bf[HALF:, cols]) + bgu_ref[:, cols]

            for t in range(D_FF // MOE_TN):
                gate = jnp.minimum(proj(slice(t * MOE_TN, (t + 1) * MOE_TN)), SWIGLU_LIMIT)
                up = jnp.clip(proj(slice(D_FF + t * MOE_TN, D_FF + (t + 1) * MOE_TN)), -SWIGLU_LIMIT, SWIGLU_LIMIT)
                hdn = gate * _sigmoid(SWIGLU_ALPHA * gate) * (up + 1.0)
                hdn_ref[:, t * MOE_TN:(t + 1) * MOE_TN] = hdn.astype(BF16)
            hdn = hdn_ref[...]
            for t in range(HALF // MOE_TN):
                lo = slice(t * MOE_TN, (t + 1) * MOE_TN)
                hi = slice(HALF + t * MOE_TN, HALF + (t + 1) * MOE_TN)
                y_lo = _dot(hdn, wd_bf[:, lo]) + bd_ref[:, lo]
                y_hi = _dot(hdn, wd_bf[:, hi]) + bd_ref[:, hi]
                y_ref[rows, lo] = _pack_rows(jnp.concatenate([y_lo, y_hi], axis=1))


def _moe_experts(layer, block_e, n_valid, next_e, xs, w_gu, b_gu, w_down, b_down):
    grid_spec = pltpu.PrefetchScalarGridSpec(
        num_scalar_prefetch=3,
        grid=(MOE_NBLK,),
        in_specs=[pl.BlockSpec((TM, HALF), lambda i, be, nv, nx: (i, 0)),
                  pl.BlockSpec(memory_space=pl.ANY),
                  pl.BlockSpec((None, None, 1, 2 * D_FF), lambda i, be, nv, nx: (layer, be[i], 0, 0)),
                  pl.BlockSpec(memory_space=pl.ANY),
                  pl.BlockSpec((None, None, 1, D_MODEL), lambda i, be, nv, nx: (layer, be[i], 0, 0))],
        out_specs=pl.BlockSpec((TM, HALF), lambda i, be, nv, nx: (i, 0)),
        scratch_shapes=[pltpu.VMEM((D_MODEL, 2 * D_FF), F32), pltpu.VMEM((D_FF, D_MODEL), F32),
                        pltpu.VMEM((D_MODEL, 2 * D_FF), BF16), pltpu.VMEM((D_FF, D_MODEL), BF16),
                        pltpu.VMEM((TM_SUB, D_FF), BF16), pltpu.SemaphoreType.DMA((2,))],
    )
    return pl.pallas_call(
        functools.partial(_moe_kernel, layer=layer),
        grid_spec=grid_spec,
        out_shape=jax.ShapeDtypeStruct((MOE_ROWS, HALF), jnp.uint32),
        compiler_params=_params(),
        name="moe_experts",
    )(block_e, n_valid, next_e, xs, w_gu, b_gu.reshape(DEPTH, N_EXPERTS, 1, -1), w_down,
      b_down.reshape(DEPTH, N_EXPERTS, 1, -1))


SC_WORKERS = 32
SC_WIN = 64


def _sc_mesh():
    return plsc.VectorSubcoreMesh(core_axis_name="core", subcore_axis_name="subcore")


def _sc_worker():
    return lax.axis_index("core") * (SC_WORKERS // 2) + lax.axis_index("subcore")


def _sc_scatter_rows(x, dest_t, n_rows):
    n, width = x.shape
    kk = dest_t.shape[0]
    per = n // SC_WORKERS
    n_win = per // SC_WIN
    assert per * SC_WORKERS == n and n_win * SC_WIN == per and n_win % 2 == 0

    @pl.kernel(out_type=jax.ShapeDtypeStruct((n_rows, width), x.dtype), mesh=_sc_mesh(),
               scratch_types=[pltpu.VMEM((kk, per), jnp.int32), pltpu.VMEM((SC_WIN, width), x.dtype),
                              pltpu.VMEM((SC_WIN, width), x.dtype), pltpu.SemaphoreType.DMA((4,))])
    def scatter(x_hbm, i_hbm, o_hbm, idx_v, buf0, buf1, sems):
        base = _sc_worker() * per
        pltpu.sync_copy(i_hbm.at[:, pl.ds(base, per)], idx_v)

        def get(j, buf, s):
            return pltpu.make_async_copy(x_hbm.at[pl.ds(base + j * SC_WIN, SC_WIN)], buf, sems.at[s])

        def put(j, q, buf, s):
            return pltpu.make_async_copy(buf, o_hbm.at[idx_v.at[q, pl.ds(j * SC_WIN, SC_WIN)]], sems.at[s])

        get(0, buf0, 0).start()

        @pl.loop(0, n_win, step=2)
        def _(j):
            get(j, buf0, 0).wait()

            @pl.when(j > 0)
            def _():
                for q in range(kk):
                    put(j - 1, q, buf1, 3).wait()

            get(j + 1, buf1, 1).start()
            for q in range(kk):
                put(j, q, buf0, 2).start()
            get(j + 1, buf1, 1).wait()
            for q in range(kk):
                put(j, q, buf0, 2).wait()

            @pl.when(j + 2 < n_win)
            def _():
                get(j + 2, buf0, 0).start()

            for q in range(kk):
                put(j + 1, q, buf1, 3).start()

        for q in range(kk):
            put(n_win - 1, q, buf1, 3).wait()

    return scatter(x, dest_t)


def _sc_gather_rows(y, idx):
    n = idx.shape[0]
    width = y.shape[1]
    per = n // SC_WORKERS
    n_win = per // SC_WIN
    assert per * SC_WORKERS == n and n_win * SC_WIN == per and n_win % 2 == 0

    @pl.kernel(out_type=jax.ShapeDtypeStruct((n, width), y.dtype), mesh=_sc_mesh(),
               scratch_types=[pltpu.VMEM((per,), jnp.int32), pltpu.VMEM((SC_WIN, width), y.dtype),
                              pltpu.VMEM((SC_WIN, width), y.dtype), pltpu.SemaphoreType.DMA((4,))])
    def gather(y_hbm, i_hbm, o_hbm, idx_v, buf0, buf1, sems):
        base = _sc_worker() * per
        pltpu.sync_copy(i_hbm.at[pl.ds(base, per)], idx_v)

        def get(j, buf, s):
            return pltpu.make_async_copy(y_hbm.at[idx_v.at[pl.ds(j * SC_WIN, SC_WIN)]], buf, sems.at[s])

        def put(j, buf, s):
            return pltpu.make_async_copy(buf, o_hbm.at[pl.ds(base + j * SC_WIN, SC_WIN)], sems.at[s])

        get(0, buf0, 0).start()

        @pl.loop(0, n_win, step=2)
        def _(j):
            get(j, buf0, 0).wait()

            @pl.when(j > 0)
            def _():
                put(j - 1, buf1, 3).wait()

            get(j + 1, buf1, 1).start()
            put(j, buf0, 2).start()
            get(j + 1, buf1, 1).wait()
            put(j, buf0, 2).wait()

            @pl.when(j + 2 < n_win)
            def _():
                get(j + 2, buf0, 0).start()

            put(j + 1, buf1, 3).start()

        put(n_win - 1, buf1, 3).wait()

    return gather(y, idx)


def _combine_kernel(x1_ref, *rest):
    yg_refs = rest[:TOP_K]
    gw_ref, gate_ref, o_ref = rest[TOP_K:]
    row = _mod_row(pl.program_id(0))
    acc_lo, acc_hi = None, None
    for kk in range(TOP_K):
        y_lo, y_hi = _unpack_rows(yg_refs[kk][...])
        w = gw_ref[:, kk:kk + 1]
        acc_lo = y_lo * w if acc_lo is None else acc_lo + y_lo * w
        acc_hi = y_hi * w if acc_hi is None else acc_hi + y_hi * w
    o_ref[:, :HALF] = x1_ref[:, :HALF] + gate_ref[pl.ds(row, 1), :HALF] * acc_lo
    o_ref[:, HALF:] = x1_ref[:, HALF:] + gate_ref[pl.ds(row, 1), HALF:] * acc_hi


def _combine(x1, yg, gw, gate):
    row_spec = lambda width: pl.BlockSpec((RB, width), lambda i: (i, 0))
    slot_spec = lambda k: pl.BlockSpec((RB, HALF), lambda i: (k * NBLK + i, 0))
    return pl.pallas_call(
        _combine_kernel,
        grid=(NBLK,),
        in_specs=[row_spec(D_MODEL)] + [slot_spec(k) for k in range(TOP_K)]
        + [row_spec(TOP_K), pl.BlockSpec((MOD_ROWS, D_MODEL), lambda i: (0, 0))],
        out_specs=row_spec(D_MODEL),
        out_shape=jax.ShapeDtypeStruct((NT, D_MODEL), F32),
        compiler_params=_params(),
        name="moe_combine",
    )(x1, *([yg] * TOP_K), gw, gate)


def _final_norm_kernel(x_ref, g_ref, op_ref, os_ref):
    i = pl.program_id(0)
    x = x_ref[...]
    y = x * lax.rsqrt(jnp.mean(x * x, axis=-1, keepdims=True) + EPS) * g_ref[...]

    @pl.when(i < N_PROMPT_BLK)
    def _():
        op_ref[...] = y

    @pl.when(i >= N_PROMPT_BLK)
    def _():
        os_ref[...] = y


def _final_norm(x, g):
    row_spec = pl.BlockSpec((RB, D_MODEL), lambda i: (i, 0))
    return pl.pallas_call(
        _final_norm_kernel,
        grid=(NBLK,),
        in_specs=[row_spec, pl.BlockSpec((1, D_MODEL), lambda i: (0, 0))],
        out_specs=[pl.BlockSpec((RB, D_MODEL), lambda i: (jnp.minimum(i, N_PROMPT_BLK - 1), 0)),
                   pl.BlockSpec((RB, D_MODEL), lambda i: (jnp.maximum(i - N_PROMPT_BLK, 0), 0))],
        out_shape=[jax.ShapeDtypeStruct((NT_PROMPT, D_MODEL), F32),
                   jax.ShapeDtypeStruct((NT - NT_PROMPT, D_MODEL), F32)],
        compiler_params=_params(),
        name="final_norm",
    )(x, g.reshape(1, D_MODEL))


def _routing_plan(counts, idx_t, rank_t):
    counts = counts[:, 0]
    padded = (counts + TM - 1) // TM * TM
    pad_end = jnp.cumsum(padded)
    pad_start = pad_end - padded
    blk_row = (jnp.arange(MOE_NBLK, dtype=jnp.int32) * TM)[:, None]
    ids = jnp.arange(N_EXPERTS, dtype=jnp.int32)
    owns = jnp.logical_and(pad_start[None, :] <= blk_row, blk_row < pad_end[None, :])
    last_used = jnp.max(jnp.where(counts > 0, ids, 0))
    block_e = jnp.where(jnp.any(owns, axis=1), jnp.sum(jnp.where(owns, ids[None, :], 0), axis=1), last_used)
    block_e = block_e.astype(jnp.int32)
    left = jnp.clip(counts[None, :] - (blk_row - pad_start[None, :]), 0, TM)
    n_valid = jnp.sum(jnp.where(owns, left, 0), axis=1).astype(jnp.int32)
    dest_t = (pad_start[idx_t] + rank_t).astype(jnp.int32)
    later = jnp.where(jnp.logical_and(counts[None, :] > 0, ids[None, :] > ids[:, None]), ids[None, :], N_EXPERTS)
    next_e = jnp.min(later, axis=1)
    next_e = jnp.where(next_e == N_EXPERTS, -1, next_e).astype(jnp.int32)
    return block_e, n_valid, next_e, dest_t


def _rope_tables():
    rows = DEC_SEQ // GRID_W
    row = jnp.repeat(jnp.arange(rows, dtype=F32), GRID_W)
    col = jnp.tile(jnp.arange(GRID_W, dtype=F32), rows)
    n_f = RET_DK // 4
    freqs = ROPE_THETA ** (-jnp.arange(n_f, dtype=F32) / n_f)
    ang = jnp.concatenate([row[:, None] * freqs, col[:, None] * freqs], axis=-1)
    cos = jnp.repeat(jnp.cos(ang), 2, axis=-1)
    sin = jnp.repeat(jnp.sin(ang), 2, axis=-1) * jnp.tile(jnp.asarray([-1.0, 1.0], F32), RET_DK // 2)
    cos = jnp.concatenate([jnp.ones((RB, RET_DK), F32), cos], axis=0)
    sin = jnp.concatenate([jnp.zeros((RB, RET_DK), F32), sin], axis=0)
    return jnp.tile(cos, (1, 2)), jnp.tile(sin, (1, 2))


def kernel(x_prompt, x_sample, state_ret, state_gla, c, c_ctx, w_mod, b_mod, norm1_g, norm2_g, final_g, even_w_in, ret_decay, ret_gn, conv_w, conv_b, conv_ln_g, conv_ln_b, even_w_out, odd_w_in, gla_w_a2, gla_b_a2, gla_gn, odd_w_out, router_w, router_b, exp_w_gu, exp_b_gu, exp_w_down, exp_b_down):
    x = jnp.concatenate([x_prompt.reshape(NT_PROMPT, D_MODEL), x_sample.reshape(-1, D_MODEL)], axis=0)
    cvec = jnp.concatenate([c_ctx[None, :], c, jnp.zeros((MOD_ROWS - 1 - DEC_BATCH, D_MODEL), F32)], axis=0)
    mods = _modulation(cvec, w_mod, b_mod).reshape(DEPTH, MOD_ROWS, N_MOD, D_MODEL)
    cos_tab, sin_tab = _rope_tables()
    new_ret, new_gla = [], []
    for l in range(DEPTH):
        mod = [mods[l, :, j, :] for j in range(N_MOD)]
        if l % 2 == 0:
            e = l // 2
            qd, vd = RET_HEADS * RET_DK, RET_HEADS * RET_DV
            q, k, v, g, a, ga = _inproj(x, norm1_g[l], mod[0], mod[1], even_w_in[e].astype(BF16),
                                        (qd, qd, vd, vd, CONV_CH, CONV_CH))
            s0 = jnp.concatenate([jnp.zeros((BATCH,) + state_ret.shape[2:], F32), state_ret[:, e]], axis=0)
            o_f, s_f = _retention(ret_decay[e], q, k, v, cos_tab, sin_tab, s0, reverse=False)
            ret, s_b = _retention(ret_decay[e], q, k, v, cos_tab, sin_tab, s0, reverse=True,
                                  o_fwd=o_f, g=g, gn=ret_gn[e])
            u = _conv_module(a, ga, conv_w[e], conv_b[e], conv_ln_g[e], conv_ln_b[e])
            w_out = even_w_out[e].astype(BF16)
            mixes, w_parts = [ret, u], [w_out[:vd], w_out[vd:]]
            new_ret.append(jnp.stack([s_f[:BATCH], s_b[:BATCH]], axis=1))
        else:
            o = l // 2
            qd, vd = GLA_HEADS * GLA_DK, GLA_HEADS * GLA_DV
            q, k, v, r, alr = _inproj(x, norm1_g[l], mod[0], mod[1], odd_w_in[o].astype(BF16),
                                      (qd, qd, vd, vd, 2 * GLA_RANK))
            s0t = jnp.concatenate([jnp.zeros((BATCH,) + state_gla.shape[2:], F32), state_gla[:, o]], axis=0)
            s0t = jnp.swapaxes(s0t, -1, -2)
            zeros = jnp.zeros((GLA_RANK, qd), F32)
            wa_f = jnp.concatenate([gla_w_a2[o, 0], zeros], axis=0)
            wa_b = jnp.concatenate([zeros, gla_w_a2[o, 1]], axis=0)
            o_f, s_f = _gla(alr, q, k, v, wa_f, gla_b_a2[o, 0].reshape(1, qd), s0t, reverse=False)
            y, s_b = _gla(alr, q, k, v, wa_b, gla_b_a2[o, 1].reshape(1, qd), s0t, reverse=True,
                          o_fwd=o_f, r=r, gn=gla_gn[o])
            mixes, w_parts = [y], [odd_w_out[o].astype(BF16)]
            new_gla.append(jnp.swapaxes(jnp.stack([s_f[:BATCH], s_b[:BATCH]], axis=1), -1, -2))
        x1, h2, idx_t, rank_t, gates, counts = _outproj(mixes, w_parts, x, mod[2], norm2_g[l], mod[3], mod[4],
                                                        router_w[l], router_b[l])
        block_e, n_valid, next_e, dest_t = _routing_plan(counts, idx_t, rank_t)
        xs = _sc_scatter_rows(h2, dest_t, MOE_ROWS)
        yb = _moe_experts(l, block_e, n_valid, next_e, xs, exp_w_gu, exp_b_gu, exp_w_down, exp_b_down)
        yg = _sc_gather_rows(yb, dest_t.reshape(TOP_K * NT))
        x = _combine(x1, yg, gates, mod[5])
    y_prompt, y_sample = _final_norm(x, final_g)
    y_prompt = y_prompt.reshape(BATCH, SEQ, D_MODEL)
    y_sample = y_sample.reshape(DEC_BATCH, DEC_SEQ, D_MODEL)
    return (y_prompt, y_sample, jnp.stack(new_ret, axis=1), jnp.stack(new_gla, axis=1))
```

```python
import functools

import jax
import jax.numpy as jnp
from jax import lax
from jax.experimental import pallas as pl
from jax.experimental.pallas import tpu as pltpu
from jax.experimental.pallas import tpu_sc as plsc

F32 = jnp.float32
BF16 = jnp.bfloat16

D_MODEL = 1024
BATCH = 16
SEQ = 256
DEPTH = 4
DEC_BATCH = 4
DEC_SEQ = 4096
GRID_W = 64
RET_HEADS = 4
RET_DK = 64
RET_DV = 128
RET_CHUNK = 128
CONV_CH = 512
CONV_WIDTH = 31
CONV_PAD = CONV_WIDTH // 2
GLA_HEADS = 4
GLA_DK = 128
GLA_DV = 256
GLA_RANK = 16
GLA_TAU = 16.0
GLA_CHUNK = 64
GLA_SUB = 16
N_EXPERTS = 32
TOP_K = 4
D_FF = 1024
SWIGLU_LIMIT = 7.0
SWIGLU_ALPHA = 1.702
MOE_BLOCK = 128
ROPE_THETA = 10000.0
EPS = 1e-6
N_MOD = 6

RB = 256
NT_PROMPT = BATCH * SEQ
NT = NT_PROMPT + DEC_BATCH * DEC_SEQ
NBLK = NT // RB
N_PROMPT_BLK = NT_PROMPT // RB
SAMPLE_BLK = DEC_SEQ // RB
NSEQ = BATCH + DEC_BATCH
MOD_ROWS = 8
HALO = 16
VMEM_LIMIT = 48 * 1024 * 1024

assert SEQ == RB and DEC_SEQ % RB == 0 and CONV_PAD < HALO


def _seq_of_block(i):
    return jnp.where(i < N_PROMPT_BLK, i, N_PROMPT_BLK + (i - N_PROMPT_BLK) // SAMPLE_BLK)


def _is_first_block(i):
    return jnp.logical_or(i < N_PROMPT_BLK, (i - N_PROMPT_BLK) % SAMPLE_BLK == 0)


def _is_last_block(i):
    return jnp.logical_or(i < N_PROMPT_BLK, (i - N_PROMPT_BLK) % SAMPLE_BLK == SAMPLE_BLK - 1)


def _mod_row(i):
    return jnp.where(i < N_PROMPT_BLK, 0, 1 + (i - N_PROMPT_BLK) // SAMPLE_BLK)


def _rope_block(i):
    return jnp.where(i < N_PROMPT_BLK, 0, 1 + (i - N_PROMPT_BLK) % SAMPLE_BLK)


def _dot(a, b):
    return jnp.dot(a, b, preferred_element_type=F32)


def _dot_nt(a, b):
    return lax.dot_general(a, b, (((1,), (1,)), ((), ())), preferred_element_type=F32)


def _dot_tn(a, b):
    return lax.dot_general(a, b, (((0,), (0,)), ((), ())), preferred_element_type=F32)


def _split2(a):
    hi = a.astype(BF16)
    lo = (a - hi.astype(F32)).astype(BF16)
    return hi, lo


def _dot_hi(a, b):
    a_hi, a_lo = _split2(a)
    b_hi, b_lo = _split2(b)
    return _dot(a_hi, b_hi) + (_dot(a_hi, b_lo) + _dot(a_lo, b_hi))


def _silu(x):
    return x * (1.0 / (1.0 + jnp.exp(-x)))


def _sigmoid(x):
    return 1.0 / (1.0 + jnp.exp(-x))


def _pack_rows(x):
    n = x.shape[1] // 2
    lo = pltpu.bitcast(x[:, :n].astype(BF16).astype(F32), jnp.uint32)
    hi = pltpu.bitcast(x[:, n:].astype(BF16).astype(F32), jnp.uint32)
    return hi | (lo >> 16)


def _unpack_rows(u):
    lo = pltpu.bitcast(u << 16, F32)
    hi = pltpu.bitcast(u & jnp.uint32(0xFFFF0000), F32)
    return lo, hi


def _params(n_axes=1, vmem=VMEM_LIMIT):
    return pltpu.CompilerParams(dimension_semantics=("arbitrary",) * n_axes, vmem_limit_bytes=vmem)


MOD_TN = 1536


def _mod_kernel(c_ref, w_ref, b_ref, o_ref):
    s = _silu(c_ref[...]).astype(BF16)
    o_ref[...] = _dot(s, w_ref[...].astype(BF16)) + b_ref[...]


def _modulation(cvec, w_mod, b_mod):
    n = N_MOD * D_MODEL
    return pl.pallas_call(
        _mod_kernel,
        grid=(DEPTH, n // MOD_TN),
        in_specs=[pl.BlockSpec((MOD_ROWS, D_MODEL), lambda l, j: (0, 0)),
                  pl.BlockSpec((None, D_MODEL, MOD_TN), lambda l, j: (l, 0, j)),
                  pl.BlockSpec((None, 1, MOD_TN), lambda l, j: (l, 0, j))],
        out_specs=pl.BlockSpec((None, MOD_ROWS, MOD_TN), lambda l, j: (l, 0, j)),
        out_shape=jax.ShapeDtypeStruct((DEPTH, MOD_ROWS, n), F32),
        compiler_params=_params(2),
        name="modulation",
    )(cvec, w_mod, b_mod.reshape(DEPTH, 1, n))


def _norm_mod(x, g_ref, shift_ref, scale_ref, row):
    y = x * lax.rsqrt(jnp.mean(x * x, axis=-1, keepdims=True) + EPS) * g_ref[...]
    return y * (1.0 + scale_ref[pl.ds(row, 1), :]) + shift_ref[pl.ds(row, 1), :]


def _inproj_kernel(x_ref, g_ref, shift_ref, scale_ref, w_ref, *o_refs, widths):
    row = _mod_row(pl.program_id(0))
    hb = _norm_mod(x_ref[...], g_ref, shift_ref, scale_ref, row).astype(BF16)
    off = 0
    for o_ref, width in zip(o_refs, widths):
        o_ref[...] = _dot(hb, w_ref[:, off:off + width])
        off += width


def _inproj(x, g, shift, scale, w_bf, widths):
    n_in = w_bf.shape[1]
    row_spec = lambda width: pl.BlockSpec((RB, width), lambda i: (i, 0))
    full = lambda shape: pl.BlockSpec(shape, lambda i: (0,) * len(shape))
    return pl.pallas_call(
        functools.partial(_inproj_kernel, widths=widths),
        grid=(NBLK,),
        in_specs=[row_spec(D_MODEL), full((1, D_MODEL)), full((MOD_ROWS, D_MODEL)),
                  full((MOD_ROWS, D_MODEL)), full((D_MODEL, n_in))],
        out_specs=[row_spec(width) for width in widths],
        out_shape=[jax.ShapeDtypeStruct((NT, width), F32) for width in widths],
        compiler_params=_params(),
        name="inproj",
    )(x, g.reshape(1, D_MODEL), shift, scale, w_bf)


RC = RET_CHUNK
RET_PAIR = 2 * RET_DK


def _rope(x, cos, sin_signed):
    lane = lax.broadcasted_iota(jnp.int32, x.shape, 1)
    swapped = jnp.where(lane % 2 == 0, pltpu.roll(x, x.shape[1] - 1, 1), pltpu.roll(x, 1, 1))
    return x * cos + swapped * sin_signed


def _ret_kernel(decay_ref, q_ref, k_ref, v_ref, cos_ref, sin_ref, s0_ref, *rest, reverse):
    if reverse:
        of_ref, g_ref, gn_ref, o_ref, sfin_ref, st_ref, dm_ref, dq_ref, dk_ref, ds_ref = rest
    else:
        o_ref, sfin_ref, st_ref, dm_ref, dq_ref, dk_ref, ds_ref = rest
    step = pl.program_id(0)
    blk = NBLK - 1 - step if reverse else step
    direction = 1 if reverse else 0

    @pl.when(step == 0)
    def _():
        row = lax.broadcasted_iota(jnp.int32, (RC, RC), 0).astype(F32)
        col = lax.broadcasted_iota(jnp.int32, (RC, RC), 1).astype(F32)
        for h in range(RET_HEADS):
            lg = -jnp.exp(jnp.full((RC, RC), decay_ref[direction, h], F32))
            if reverse:
                diff = col - row
                mask = diff > 0
                q_pow = RC - row
                k_pow = row
            else:
                diff = row - col
                mask = diff >= 0
                q_pow = row + 1.0
                k_pow = RC - 1.0 - row
            dm_ref[h] = jnp.where(mask, jnp.exp(lg * jnp.where(mask, diff, 0.0)), 0.0)
            dq_ref[h] = jnp.exp(lg * q_pow)
            dk_ref[h] = jnp.exp(lg * k_pow)
            ds_ref[h] = jnp.exp(lg * RC)

    starts = _is_last_block(blk) if reverse else _is_first_block(blk)

    @pl.when(starts)
    def _():
        st_ref[...] = jnp.zeros_like(st_ref)
        for h in range(RET_HEADS):
            off = (h % 2) * RET_DK
            st_ref[h, off:off + RET_DK, :] = s0_ref[h]

    lane = lax.broadcasted_iota(jnp.int32, (1, RET_PAIR), 1)
    chunks = range(RB // RC)
    chunk_order = list(reversed(chunks) if reverse else chunks)
    for p in range(RET_HEADS // 2):
        cols = slice(p * RET_PAIR, (p + 1) * RET_PAIR)
        roped = {}
        for c in chunk_order:
            rows = slice(c * RC, (c + 1) * RC)
            cos, sin = cos_ref[rows, :], sin_ref[rows, :]
            roped[c] = (_rope(q_ref[rows, cols], cos, sin), _rope(k_ref[rows, cols] * (RET_DK ** -0.5), cos, sin))
        for h in (2 * p, 2 * p + 1):
            head_mask = (lane // RET_DK == h % 2).astype(F32)
            out_cols = slice(h * RET_DV, (h + 1) * RET_DV)
            st = st_ref[h]
            for c in chunk_order:
                rows = slice(c * RC, (c + 1) * RC)
                q2, k2 = roped[c]
                vh = v_ref[rows, out_cols].astype(BF16)
                qm = (q2 * head_mask).astype(BF16)
                km = k2 * head_mask
                att = _dot_nt(qm, km.astype(BF16)) * dm_ref[h]
                o = _dot(att.astype(BF16), vh) + _dot(qm, st.astype(BF16)) * dq_ref[h]
                st = st * ds_ref[h] + _dot_tn((km * dk_ref[h]).astype(BF16), vh)
                if reverse:
                    o = o + of_ref[rows, out_cols]
                    o = o * lax.rsqrt(jnp.mean(o * o, axis=-1, keepdims=True) + EPS)
                    o = o * gn_ref[:, out_cols] * _silu(g_ref[rows, out_cols])
                    o_ref[rows, out_cols] = o.astype(o_ref.dtype)
                else:
                    o_ref[rows, out_cols] = o
            st_ref[h] = st

    ends = _is_first_block(blk) if reverse else _is_last_block(blk)

    @pl.when(ends)
    def _():
        for h in range(RET_HEADS):
            off = (h % 2) * RET_DK
            sfin_ref[h] = st_ref[h, off:off + RET_DK, :]


def _retention(decay, q, k, v, cos_tab, sin_tab, s0, *, reverse, o_fwd=None, g=None, gn=None):
    bmap = (lambda j: NBLK - 1 - j) if reverse else (lambda j: j)
    direction = 1 if reverse else 0
    qd, vd = RET_HEADS * RET_DK, RET_HEADS * RET_DV
    row_spec = lambda width: pl.BlockSpec((RB, width), lambda j: (bmap(j), 0))
    state_spec = pl.BlockSpec((None, None, RET_HEADS, RET_DK, RET_DV),
                              lambda j: (_seq_of_block(bmap(j)), direction, 0, 0, 0))
    in_specs = [pl.BlockSpec(memory_space=pltpu.SMEM), row_spec(qd), row_spec(qd), row_spec(vd),
                pl.BlockSpec((RB, RET_PAIR), lambda j: (_rope_block(bmap(j)), 0)),
                pl.BlockSpec((RB, RET_PAIR), lambda j: (_rope_block(bmap(j)), 0)),
                state_spec]
    args = [decay, q, k, v, cos_tab, sin_tab, s0]
    if reverse:
        in_specs += [row_spec(vd), row_spec(vd), pl.BlockSpec((1, vd), lambda j: (0, 0))]
        args += [o_fwd, g, gn.reshape(1, vd)]
    tile = pltpu.VMEM((RET_HEADS, RC, RC), F32)
    return pl.pallas_call(
        functools.partial(_ret_kernel, reverse=reverse),
        grid=(NBLK,),
        in_specs=in_specs,
        out_specs=[row_spec(vd),
                   pl.BlockSpec((None, RET_HEADS, RET_DK, RET_DV),
                                lambda j: (_seq_of_block(bmap(j)), 0, 0, 0))],
        out_shape=[jax.ShapeDtypeStruct((NT, vd), BF16 if reverse else F32),
                   jax.ShapeDtypeStruct((NSEQ, RET_HEADS, RET_DK, RET_DV), F32)],
        scratch_shapes=[tile, tile, tile, tile, tile],
        compiler_params=_params(),
        name="retention_bwd" if reverse else "retention_fwd",
    )(*args)


CONV_RT = 32
CONV_CT = 128
CONV_SPAN = RB + 2 * HALO - 8


def _conv_kernel(a_ref, ga_ref, ap_ref, gap_ref, an_ref, gan_ref, cw_ref, cb_ref, lng_ref, lnb_ref,
                 o_ref, u_ref, y_ref, us_ref):
    blk = pl.program_id(0)
    keep_prev = jnp.where(_is_first_block(blk), 0.0, 1.0)
    keep_next = jnp.where(_is_last_block(blk), 0.0, 1.0)
    u_ref[0:HALO, :] = ap_ref[...] * _sigmoid(gap_ref[...]) * keep_prev
    u_ref[HALO:HALO + RB, :] = a_ref[...] * _sigmoid(ga_ref[...])
    u_ref[HALO + RB:HALO + RB + HALO, :] = an_ref[...] * _sigmoid(gan_ref[...]) * keep_next
    for r in range(1, 8):
        us_ref[r - 1] = u_ref[r:r + CONV_SPAN, :]
    for ct in range(CONV_CH // CONV_CT):
        cols = slice(ct * CONV_CT, (ct + 1) * CONV_CT)
        for rt in range(RB // CONV_RT):
            acc = jnp.zeros((CONV_RT, CONV_CT), F32)
            for w in range(CONV_WIDTH):
                tiles, r = divmod(HALO - CONV_PAD + w, 8)
                base = rt * CONV_RT + 8 * tiles
                src = u_ref if r == 0 else us_ref.at[r - 1]
                acc = acc + src[base:base + CONV_RT, cols] * cw_ref[w:w + 1, cols]
            y_ref[rt * CONV_RT:(rt + 1) * CONV_RT, cols] = acc + cb_ref[:, cols]
    y = y_ref[...]
    mu = jnp.mean(y, axis=-1, keepdims=True)
    var = jnp.mean(jnp.square(y - mu), axis=-1, keepdims=True)
    o_ref[...] = _silu((y - mu) * lax.rsqrt(var + EPS) * lng_ref[...] + lnb_ref[...]).astype(o_ref.dtype)


def _conv_module(a, ga, cw, cb, lng, lnb):
    per_blk = RB // HALO
    n_halo = NT // HALO
    row_spec = pl.BlockSpec((RB, CONV_CH), lambda i: (i, 0))
    prev_spec = pl.BlockSpec((HALO, CONV_CH), lambda i: (jnp.maximum(i * per_blk - 1, 0), 0))
    next_spec = pl.BlockSpec((HALO, CONV_CH), lambda i: (jnp.minimum((i + 1) * per_blk, n_halo - 1), 0))
    vec = pl.BlockSpec((1, CONV_CH), lambda i: (0, 0))
    return pl.pallas_call(
        _conv_kernel,
        grid=(NBLK,),
        in_specs=[row_spec, row_spec, prev_spec, prev_spec, next_spec, next_spec,
                  pl.BlockSpec((CONV_WIDTH, CONV_CH), lambda i: (0, 0)), vec, vec, vec],
        out_specs=row_spec,
        out_shape=jax.ShapeDtypeStruct((NT, CONV_CH), BF16),
        scratch_shapes=[pltpu.VMEM((RB + 2 * HALO, CONV_CH), F32), pltpu.VMEM((RB, CONV_CH), F32),
                        pltpu.VMEM((7, CONV_SPAN, CONV_CH), F32)],
        compiler_params=_params(),
        name="conv_module",
    )(a, ga, a, ga, a, ga, cw, cb.reshape(1, -1), lng.reshape(1, -1), lnb.reshape(1, -1))


GC = GLA_CHUNK
GLA_NSUB = GC // GLA_SUB


def _split3(a):
    p1 = a.astype(BF16)
    r1 = a - p1.astype(F32)
    p2 = r1.astype(BF16)
    p3 = (r1 - p2.astype(F32)).astype(BF16)
    return p1, p2, p3


def _gla_kernel(alr_ref, alr_next_ref, q_ref, k_ref, v_ref, wa_ref, ba_ref, s0_ref, *rest, reverse):
    if reverse:
        of_ref, r_ref, gn_ref, o_ref, sfin_ref, st_ref, b2_ref, ob_ref = rest
    else:
        o_ref, sfin_ref, st_ref, b2_ref = rest
        ob_ref = o_ref
    step = pl.program_id(0)
    blk = NBLK - 1 - step if reverse else step
    starts = _is_last_block(blk) if reverse else _is_first_block(blk)

    @pl.when(starts)
    def _():
        st_ref[...] = s0_ref[...]

    def cumulative_gates(alr):
        z = _dot_hi(alr, wa_ref[...]) + ba_ref[...]
        log_a = (jnp.minimum(z, 0.0) - jnp.log(1.0 + jnp.exp(-jnp.abs(z)))) * (1.0 / GLA_TAU)
        row = lax.broadcasted_iota(jnp.int32, (RB, RB), 0)
        col = lax.broadcasted_iota(jnp.int32, (RB, RB), 1)
        ordered = col >= row if reverse else col <= row
        tri = jnp.where(jnp.logical_and(row // GC == col // GC, ordered), 1.0, 0.0).astype(BF16)
        g1, g2, g3 = _split3(log_a)
        return _dot(tri, g1) + (_dot(tri, g2) + _dot(tri, g3))

    slot = step % 2

    @pl.when(step == 0)
    def _():
        b2_ref[0] = cumulative_gates(alr_ref[...])

    b2_ref[1 - slot] = cumulative_gates(alr_next_ref[...])
    b_ref = b2_ref.at[slot]

    c_row = lax.broadcasted_iota(jnp.int32, (GC, 1), 0)
    a_row = lax.broadcasted_iota(jnp.int32, (GC, GC), 0)
    a_col = lax.broadcasted_iota(jnp.int32, (GC, GC), 1)
    att_mask = a_col > a_row if reverse else a_col <= a_row
    chunks = range(RB // GC)
    for h in range(GLA_HEADS):
        kcols = slice(h * GLA_DK, (h + 1) * GLA_DK)
        vcols = slice(h * GLA_DV, (h + 1) * GLA_DV)
        st = st_ref[h]
        for c in (reversed(chunks) if reverse else chunks):
            rows = slice(c * GC, (c + 1) * GC)
            b = b_ref[rows, kcols]
            qh = q_ref[rows, kcols] * (GLA_DK ** -0.5)
            kh = k_ref[rows, kcols]
            vh = v_ref[rows, vcols].astype(BF16)
            edge = b[0:1, :] if reverse else b[GC - 1:GC, :]
            bounds = []
            for s in range(GLA_NSUB):
                if reverse:
                    hi = (s + 1) * GLA_SUB
                    bounds.append(b[hi:hi + 1, :] if s < GLA_NSUB - 1 else jnp.zeros((1, GLA_DK), F32))
                else:
                    lo = s * GLA_SUB
                    bounds.append(b[lo - 1:lo, :] if s > 0 else jnp.zeros((1, GLA_DK), F32))
            own = jnp.concatenate([jnp.broadcast_to(bd, (GLA_SUB, GLA_DK)) for bd in bounds], axis=0)
            q_own = qh * jnp.exp(b - own)
            q_parts, k_parts = [], []
            for s, bd in enumerate(bounds):
                q_parts.append(jnp.where(c_row // GLA_SUB == s, q_own, 0.0))
                reach = c_row >= s * GLA_SUB if reverse else c_row < (s + 1) * GLA_SUB
                k_parts.append(kh * jnp.exp(jnp.where(reach, bd - b, -jnp.inf)))
            q_bd = jnp.concatenate(q_parts, axis=1).astype(BF16)
            k_cat = jnp.concatenate(k_parts, axis=1).astype(BF16)
            att = jnp.where(att_mask, _dot_nt(q_bd, k_cat), 0.0)
            o = _dot(att.astype(BF16), vh) + _dot_nt((qh * jnp.exp(b)).astype(BF16), st.astype(BF16))
            ke = (kh * jnp.exp(edge - b)).astype(BF16)
            st = st * jnp.exp(edge) + _dot_tn(vh, ke)
            ob_ref[rows, vcols] = o
        st_ref[h] = st

    if reverse:
        for h in range(GLA_HEADS):
            vcols = slice(h * GLA_DV, (h + 1) * GLA_DV)
            o = ob_ref[:, vcols] + of_ref[:, vcols]
            o = o * lax.rsqrt(jnp.mean(o * o, axis=-1, keepdims=True) + EPS)
            o_ref[:, vcols] = (o * gn_ref[:, vcols] * _silu(r_ref[:, vcols])).astype(o_ref.dtype)

    ends = _is_first_block(blk) if reverse else _is_last_block(blk)

    @pl.when(ends)
    def _():
        sfin_ref[...] = st_ref[...]


def _gla(alr, q, k, v, wa, ba, s0t, *, reverse, o_fwd=None, r=None, gn=None):
    bmap = (lambda j: NBLK - 1 - j) if reverse else (lambda j: j)
    direction = 1 if reverse else 0
    qd, vd = GLA_HEADS * GLA_DK, GLA_HEADS * GLA_DV
    row_spec = lambda width: pl.BlockSpec((RB, width), lambda j: (bmap(j), 0))
    state_spec = pl.BlockSpec((None, None, GLA_HEADS, GLA_DV, GLA_DK),
                              lambda j: (_seq_of_block(bmap(j)), direction, 0, 0, 0))
    next_spec = pl.BlockSpec((RB, 2 * GLA_RANK), lambda j: (bmap(jnp.minimum(j + 1, NBLK - 1)), 0))
    in_specs = [row_spec(2 * GLA_RANK), next_spec, row_spec(qd), row_spec(qd), row_spec(vd),
                pl.BlockSpec((2 * GLA_RANK, qd), lambda j: (0, 0)),
                pl.BlockSpec((1, qd), lambda j: (0, 0)), state_spec]
    args = [alr, alr, q, k, v, wa, ba, s0t]
    scratch = [pltpu.VMEM((GLA_HEADS, GLA_DV, GLA_DK), F32), pltpu.VMEM((2, RB, qd), F32)]
    if reverse:
        in_specs += [row_spec(vd), row_spec(vd), pl.BlockSpec((1, vd), lambda j: (0, 0))]
        args += [o_fwd, r, gn.reshape(1, vd)]
        scratch += [pltpu.VMEM((RB, vd), F32)]
    return pl.pallas_call(
        functools.partial(_gla_kernel, reverse=reverse),
        grid=(NBLK,),
        in_specs=in_specs,
        out_specs=[row_spec(vd),
                   pl.BlockSpec((None, GLA_HEADS, GLA_DV, GLA_DK),
                                lambda j: (_seq_of_block(bmap(j)), 0, 0, 0))],
        out_shape=[jax.ShapeDtypeStruct((NT, vd), BF16 if reverse else F32),
                   jax.ShapeDtypeStruct((NSEQ, GLA_HEADS, GLA_DV, GLA_DK), F32)],
        scratch_shapes=scratch,
        compiler_params=_params(),
        name="gla_bwd" if reverse else "gla_fwd",
    )(*args)


def _outproj_kernel(*refs, n_mix):
    mix_refs = refs[:n_mix]
    w_refs = refs[n_mix:2 * n_mix]
    (x_ref, gate_ref, g2_ref, shift_ref, scale_ref, rw_ref, rb_ref,
     x1_ref, h2_ref, idx_ref, rank_ref, gw_ref, cnt_ref, carry_ref) = refs[2 * n_mix:]
    step = pl.program_id(0)
    row = _mod_row(step)
    m = _dot(mix_refs[0][...], w_refs[0][...])
    for mix_ref, w_ref in zip(mix_refs[1:], w_refs[1:]):
        m = m + _dot(mix_ref[...], w_ref[...])
    x1 = x_ref[...] + gate_ref[pl.ds(row, 1), :] * m
    x1_ref[...] = x1
    h2 = _norm_mod(x1, g2_ref, shift_ref, scale_ref, row)
    h2_ref[...] = _pack_rows(h2)
    h_hi, h_lo = _split2(h2)
    r_hi, r_lo = _split2(rw_ref[...])
    logits = _dot_nt(r_hi, h_hi) + (_dot_nt(r_hi, h_lo) + _dot_nt(r_lo, h_hi)) + rb_ref[...]

    @pl.when(step == 0)
    def _():
        carry_ref[...] = jnp.zeros_like(carry_ref)

    eid = lax.broadcasted_iota(jnp.int32, logits.shape, 0).astype(F32)
    work = logits
    onehots, top_vals = [], []
    for kk in range(TOP_K):
        top = jnp.max(work, axis=0, keepdims=True)
        first = jnp.min(jnp.where(work == top, eid, float(N_EXPERTS)), axis=0, keepdims=True)
        onehot = eid == first
        idx_ref[kk:kk + 1, :] = first.astype(jnp.int32)
        onehots.append(onehot)
        top_vals.append(top)
        work = jnp.where(onehot, -jnp.inf, work)
    exps = [jnp.exp(v - top_vals[0]) for v in top_vals]
    denom = exps[0]
    for e in exps[1:]:
        denom = denom + e
    slot = lax.broadcasted_iota(jnp.int32, (8, RB), 0)
    gates = jnp.zeros((8, RB), F32)
    for kk in range(TOP_K):
        gates = jnp.where(slot == kk, exps[kk] / denom, gates)
    r_i = lax.broadcasted_iota(jnp.int32, (RB, RB), 0)
    c_i = lax.broadcasted_iota(jnp.int32, (RB, RB), 1)
    eye = jnp.where(r_i == c_i, 1.0, 0.0).astype(BF16)
    g1, g2_, g3 = _split3(gates)
    gw_ref[...] = (_dot_nt(eye, g1) + (_dot_nt(eye, g2_) + _dot_nt(eye, g3)))[:, :TOP_K]

    sel = jnp.zeros(logits.shape, F32)
    for onehot in onehots:
        sel = sel + jnp.where(onehot, 1.0, 0.0)
    sel = sel.astype(BF16)
    before = jnp.where(r_i < c_i, 1.0, 0.0).astype(BF16)
    pos = _dot(sel, before) + carry_ref[...]
    for kk in range(TOP_K):
        rank = jnp.sum(jnp.where(onehots[kk], pos, 0.0), axis=0, keepdims=True)
        rank_ref[kk:kk + 1, :] = rank.astype(jnp.int32)
    carry = carry_ref[...] + _dot(sel, jnp.ones((RB, RB), BF16))
    carry_ref[...] = carry
    cnt_ref[...] = carry.astype(jnp.int32)


def _outproj(mixes, w_parts, x, gate, g2, shift, scale, rw, rb):
    n_mix = len(mixes)
    row_spec = lambda width: pl.BlockSpec((RB, width), lambda i: (i, 0))
    col_spec = pl.BlockSpec((TOP_K, RB), lambda i: (0, i))
    full = lambda shape: pl.BlockSpec(shape, lambda i: (0,) * len(shape))
    mod_spec = full((MOD_ROWS, D_MODEL))
    return pl.pallas_call(
        functools.partial(_outproj_kernel, n_mix=n_mix),
        grid=(NBLK,),
        in_specs=[row_spec(m.shape[1]) for m in mixes] + [full(w.shape) for w in w_parts]
        + [row_spec(D_MODEL), mod_spec, full((1, D_MODEL)), mod_spec, mod_spec,
           full((N_EXPERTS, D_MODEL)), full((N_EXPERTS, 1))],
        out_specs=[row_spec(D_MODEL), row_spec(D_MODEL // 2), col_spec, col_spec, row_spec(TOP_K),
                   full((N_EXPERTS, RB))],
        out_shape=[jax.ShapeDtypeStruct((NT, D_MODEL), F32), jax.ShapeDtypeStruct((NT, D_MODEL // 2), jnp.uint32),
                   jax.ShapeDtypeStruct((TOP_K, NT), jnp.int32), jax.ShapeDtypeStruct((TOP_K, NT), jnp.int32),
                   jax.ShapeDtypeStruct((NT, TOP_K), F32), jax.ShapeDtypeStruct((N_EXPERTS, RB), jnp.int32)],
        scratch_shapes=[pltpu.VMEM((N_EXPERTS, RB), F32)],
        compiler_params=_params(),
        name="outproj",
    )(*mixes, *w_parts, x, gate, g2.reshape(1, D_MODEL), shift, scale, rw.T, rb.reshape(N_EXPERTS, 1))


TM = 512
TM_SUB = 256
MOE_NBLK = NT * TOP_K // TM + N_EXPERTS
MOE_ROWS = MOE_NBLK * TM
HALF = D_MODEL // 2


def _moe_kernel(be_ref, nv_ref, nx_ref, x_ref, wgu_hbm, bgu_ref, wd_hbm, bd_ref, y_ref,
                wgu_st, wd_st, wgu_bf, wd_bf, sems, *, layer):
    i = pl.program_id(0)
    n_valid = nv_ref[i]

    def fetch(e):
        return (pltpu.make_async_copy(wgu_hbm.at[layer, e], wgu_st, sems.at[0]),
                pltpu.make_async_copy(wd_hbm.at[layer, e], wd_st, sems.at[1]))

    @pl.when(i == 0)
    def _():
        for cp in fetch(be_ref[0]):
            cp.start()

    @pl.when(n_valid > 0)
    def _():
        e = be_ref[i]
        changed = jnp.logical_or(i == 0, e != be_ref[jnp.maximum(i - 1, 0)])

        @pl.when(changed)
        def _():
            for cp in fetch(e):
                cp.wait()
            wgu_bf[...] = wgu_st[...].astype(BF16)
            wd_bf[...] = wd_st[...].astype(BF16)
            nxt = nx_ref[e]

            @pl.when(nxt >= 0)
            def _():
                for cp in fetch(nxt):
                    cp.start()

    for p in range(TM // TM_SUB):
        @pl.when(n_valid > p * TM_SUB)
        def _():
            rows = slice(p * TM_SUB, (p + 1) * TM_SUB)
            row_id = lax.broadcasted_iota(jnp.int32, (TM_SUB, 1), 0) + p * TM_SUB
            x_lo, x_hi = _unpack_rows(jnp.where(row_id < n_valid, x_ref[rows, :], jnp.uint32(0)))
            gu = (_dot(x_lo.astype(BF16), wgu_bf[:HALF, :]) + _dot(x_hi.astype(BF16), wgu_bf[HALF:, :])
                  + bgu_ref[...])
            gate = jnp.minimum(gu[:, :D_FF], SWIGLU_LIMIT)
            up = jnp.clip(gu[:, D_FF:], -SWIGLU_LIMIT, SWIGLU_LIMIT)
            hdn = gate * _sigmoid(SWIGLU_ALPHA * gate) * (up + 1.0)
            y_ref[rows, :] = _pack_rows(_dot(hdn.astype(BF16), wd_bf[...]) + bd_ref[...])


def _moe_experts(layer, block_e, n_valid, next_e, xs, w_gu, b_gu, w_down, b_down):
    grid_spec = pltpu.PrefetchScalarGridSpec(
        num_scalar_prefetch=3,
        grid=(MOE_NBLK,),
        in_specs=[pl.BlockSpec((TM, HALF), lambda i, be, nv, nx: (i, 0)),
                  pl.BlockSpec(memory_space=pl.ANY),
                  pl.BlockSpec((None, None, 1, 2 * D_FF), lambda i, be, nv, nx: (layer, be[i], 0, 0)),
                  pl.BlockSpec(memory_space=pl.ANY),
                  pl.BlockSpec((None, None, 1, D_MODEL), lambda i, be, nv, nx: (layer, be[i], 0, 0))],
        out_specs=pl.BlockSpec((TM, HALF), lambda i, be, nv, nx: (i, 0)),
        scratch_shapes=[pltpu.VMEM((D_MODEL, 2 * D_FF), F32), pltpu.VMEM((D_FF, D_MODEL), F32),
                        pltpu.VMEM((D_MODEL, 2 * D_FF), BF16), pltpu.VMEM((D_FF, D_MODEL), BF16),
                        pltpu.SemaphoreType.DMA((2,))],
    )
    return pl.pallas_call(
        functools.partial(_moe_kernel, layer=layer),
        grid_spec=grid_spec,
        out_shape=jax.ShapeDtypeStruct((MOE_ROWS, HALF), jnp.uint32),
        compiler_params=_params(),
        name="moe_experts",
    )(block_e, n_valid, next_e, xs, w_gu, b_gu.reshape(DEPTH, N_EXPERTS, 1, -1), w_down,
      b_down.reshape(DEPTH, N_EXPERTS, 1, -1))


SC_WORKERS = 32
SC_WIN = 64


def _sc_mesh():
    return plsc.VectorSubcoreMesh(core_axis_name="core", subcore_axis_name="subcore")


def _sc_worker():
    return lax.axis_index("core") * (SC_WORKERS // 2) + lax.axis_index("subcore")


def _sc_scatter_rows(x, dest_t, n_rows):
    n, width = x.shape
    kk = dest_t.shape[0]
    per = n // SC_WORKERS
    n_win = per // SC_WIN
    assert per * SC_WORKERS == n and n_win * SC_WIN == per and n_win % 2 == 0

    @pl.kernel(out_type=jax.ShapeDtypeStruct((n_rows, width), x.dtype), mesh=_sc_mesh(),
               scratch_types=[pltpu.VMEM((kk, per), jnp.int32), pltpu.VMEM((SC_WIN, width), x.dtype),
                              pltpu.VMEM((SC_WIN, width), x.dtype), pltpu.SemaphoreType.DMA((4,))])
    def scatter(x_hbm, i_hbm, o_hbm, idx_v, buf0, buf1, sems):
        base = _sc_worker() * per
        pltpu.sync_copy(i_hbm.at[:, pl.ds(base, per)], idx_v)

        def get(j, buf, s):
            return pltpu.make_async_copy(x_hbm.at[pl.ds(base + j * SC_WIN, SC_WIN)], buf, sems.at[s])

        def put(j, q, buf, s):
            return pltpu.make_async_copy(buf, o_hbm.at[idx_v.at[q, pl.ds(j * SC_WIN, SC_WIN)]], sems.at[s])

        get(0, buf0, 0).start()

        @pl.loop(0, n_win, step=2)
        def _(j):
            get(j, buf0, 0).wait()

            @pl.when(j > 0)
            def _():
                for q in range(kk):
                    put(j - 1, q, buf1, 3).wait()

            get(j + 1, buf1, 1).start()
            for q in range(kk):
                put(j, q, buf0, 2).start()
            get(j + 1, buf1, 1).wait()
            for q in range(kk):
                put(j, q, buf0, 2).wait()

            @pl.when(j + 2 < n_win)
            def _():
                get(j + 2, buf0, 0).start()

            for q in range(kk):
                put(j + 1, q, buf1, 3).start()

        for q in range(kk):
            put(n_win - 1, q, buf1, 3).wait()

    return scatter(x, dest_t)


def _sc_gather_rows(y, idx):
    n = idx.shape[0]
    width = y.shape[1]
    per = n // SC_WORKERS
    n_win = per // SC_WIN
    assert per * SC_WORKERS == n and n_win * SC_WIN == per and n_win % 2 == 0

    @pl.kernel(out_type=jax.ShapeDtypeStruct((n, width), y.dtype), mesh=_sc_mesh(),
               scratch_types=[pltpu.VMEM((per,), jnp.int32), pltpu.VMEM((SC_WIN, width), y.dtype),
                              pltpu.VMEM((SC_WIN, width), y.dtype), pltpu.SemaphoreType.DMA((4,))])
    def gather(y_hbm, i_hbm, o_hbm, idx_v, buf0, buf1, sems):
        base = _sc_worker() * per
        pltpu.sync_copy(i_hbm.at[pl.ds(base, per)], idx_v)

        def get(j, buf, s):
            return pltpu.make_async_copy(y_hbm.at[idx_v.at[pl.ds(j * SC_WIN, SC_WIN)]], buf, sems.at[s])

        def put(j, buf, s):
            return pltpu.make_async_copy(buf, o_hbm.at[pl.ds(base + j * SC_WIN, SC_WIN)], sems.at[s])

        get(0, buf0, 0).start()

        @pl.loop(0, n_win, step=2)
        def _(j):
            get(j, buf0, 0).wait()

            @pl.when(j > 0)
            def _():
                put(j - 1, buf1, 3).wait()

            get(j + 1, buf1, 1).start()
            put(j, buf0, 2).start()
            get(j + 1, buf1, 1).wait()
            put(j, buf0, 2).wait()

            @pl.when(j + 2 < n_win)
            def _():
                get(j + 2, buf0, 0).start()

            put(j + 1, buf1, 3).start()

        put(n_win - 1, buf1, 3).wait()

    return gather(y, idx)


def _combine_kernel(x1_ref, *rest):
    yg_refs = rest[:TOP_K]
    gw_ref, gate_ref, o_ref = rest[TOP_K:]
    row = _mod_row(pl.program_id(0))
    acc_lo, acc_hi = None, None
    for kk in range(TOP_K):
        y_lo, y_hi = _unpack_rows(yg_refs[kk][...])
        w = gw_ref[:, kk:kk + 1]
        acc_lo = y_lo * w if acc_lo is None else acc_lo + y_lo * w
        acc_hi = y_hi * w if acc_hi is None else acc_hi + y_hi * w
    o_ref[:, :HALF] = x1_ref[:, :HALF] + gate_ref[pl.ds(row, 1), :HALF] * acc_lo
    o_ref[:, HALF:] = x1_ref[:, HALF:] + gate_ref[pl.ds(row, 1), HALF:] * acc_hi


def _combine(x1, yg, gw, gate):
    row_spec = lambda width: pl.BlockSpec((RB, width), lambda i: (i, 0))
    slot_spec = lambda k: pl.BlockSpec((RB, HALF), lambda i: (k * NBLK + i, 0))
    return pl.pallas_call(
        _combine_kernel,
        grid=(NBLK,),
        in_specs=[row_spec(D_MODEL)] + [slot_spec(k) for k in range(TOP_K)]
        + [row_spec(TOP_K), pl.BlockSpec((MOD_ROWS, D_MODEL), lambda i: (0, 0))],
        out_specs=row_spec(D_MODEL),
        out_shape=jax.ShapeDtypeStruct((NT, D_MODEL), F32),
        compiler_params=_params(),
        name="moe_combine",
    )(x1, *([yg] * TOP_K), gw, gate)


def _final_norm_kernel(x_ref, g_ref, op_ref, os_ref):
    i = pl.program_id(0)
    x = x_ref[...]
    y = x * lax.rsqrt(jnp.mean(x * x, axis=-1, keepdims=True) + EPS) * g_ref[...]

    @pl.when(i < N_PROMPT_BLK)
    def _():
        op_ref[...] = y

    @pl.when(i >= N_PROMPT_BLK)
    def _():
        os_ref[...] = y


def _final_norm(x, g):
    row_spec = pl.BlockSpec((RB, D_MODEL), lambda i: (i, 0))
    return pl.pallas_call(
        _final_norm_kernel,
        grid=(NBLK,),
        in_specs=[row_spec, pl.BlockSpec((1, D_MODEL), lambda i: (0, 0))],
        out_specs=[pl.BlockSpec((RB, D_MODEL), lambda i: (jnp.minimum(i, N_PROMPT_BLK - 1), 0)),
                   pl.BlockSpec((RB, D_MODEL), lambda i: (jnp.maximum(i - N_PROMPT_BLK, 0), 0))],
        out_shape=[jax.ShapeDtypeStruct((NT_PROMPT, D_MODEL), F32),
                   jax.ShapeDtypeStruct((NT - NT_PROMPT, D_MODEL), F32)],
        compiler_params=_params(),
        name="final_norm",
    )(x, g.reshape(1, D_MODEL))


def _routing_plan(counts, idx_t, rank_t):
    counts = counts[:, 0]
    padded = (counts + TM - 1) // TM * TM
    pad_end = jnp.cumsum(padded)
    pad_start = pad_end - padded
    blk_row = (jnp.arange(MOE_NBLK, dtype=jnp.int32) * TM)[:, None]
    ids = jnp.arange(N_EXPERTS, dtype=jnp.int32)
    owns = jnp.logical_and(pad_start[None, :] <= blk_row, blk_row < pad_end[None, :])
    last_used = jnp.max(jnp.where(counts > 0, ids, 0))
    block_e = jnp.where(jnp.any(owns, axis=1), jnp.sum(jnp.where(owns, ids[None, :], 0), axis=1), last_used)
    block_e = block_e.astype(jnp.int32)
    left = jnp.clip(counts[None, :] - (blk_row - pad_start[None, :]), 0, TM)
    n_valid = jnp.sum(jnp.where(owns, left, 0), axis=1).astype(jnp.int32)
    start = jnp.zeros(idx_t.shape, jnp.int32)
    for e in range(N_EXPERTS):
        start = jnp.where(idx_t == e, pad_start[e], start)
    dest_t = (start + rank_t).astype(jnp.int32)
    later = jnp.where(jnp.logical_and(counts[None, :] > 0, ids[None, :] > ids[:, None]), ids[None, :], N_EXPERTS)
    next_e = jnp.min(later, axis=1)
    next_e = jnp.where(next_e == N_EXPERTS, -1, next_e).astype(jnp.int32)
    return block_e, n_valid, next_e, dest_t


def _rope_tables():
    rows = DEC_SEQ // GRID_W
    row = jnp.repeat(jnp.arange(rows, dtype=F32), GRID_W)
    col = jnp.tile(jnp.arange(GRID_W, dtype=F32), rows)
    n_f = RET_DK // 4
    freqs = ROPE_THETA ** (-jnp.arange(n_f, dtype=F32) / n_f)
    ang = jnp.concatenate([row[:, None] * freqs, col[:, None] * freqs], axis=-1)
    cos = jnp.repeat(jnp.cos(ang), 2, axis=-1)
    sin = jnp.repeat(jnp.sin(ang), 2, axis=-1) * jnp.tile(jnp.asarray([-1.0, 1.0], F32), RET_DK // 2)
    cos = jnp.concatenate([jnp.ones((RB, RET_DK), F32), cos], axis=0)
    sin = jnp.concatenate([jnp.zeros((RB, RET_DK), F32), sin], axis=0)
    return jnp.tile(cos, (1, 2)), jnp.tile(sin, (1, 2))


def kernel(x_prompt, x_sample, state_ret, state_gla, c, c_ctx, w_mod, b_mod, norm1_g, norm2_g, final_g, even_w_in, ret_decay, ret_gn, conv_w, conv_b, conv_ln_g, conv_ln_b, even_w_out, odd_w_in, gla_w_a2, gla_b_a2, gla_gn, odd_w_out, router_w, router_b, exp_w_gu, exp_b_gu, exp_w_down, exp_b_down):
    x = jnp.concatenate([x_prompt.reshape(NT_PROMPT, D_MODEL), x_sample.reshape(-1, D_MODEL)], axis=0)
    cvec = jnp.concatenate([c_ctx[None, :], c, jnp.zeros((MOD_ROWS - 1 - DEC_BATCH, D_MODEL), F32)], axis=0)
    mods = _modulation(cvec, w_mod, b_mod).reshape(DEPTH, MOD_ROWS, N_MOD, D_MODEL)
    cos_tab, sin_tab = _rope_tables()
    new_ret, new_gla = [], []
    for l in range(DEPTH):
        mod = [mods[l, :, j, :] for j in range(N_MOD)]
        if l % 2 == 0:
            e = l // 2
            qd, vd = RET_HEADS * RET_DK, RET_HEADS * RET_DV
            q, k, v, g, a, ga = _inproj(x, norm1_g[l], mod[0], mod[1], even_w_in[e].astype(BF16),
                                        (qd, qd, vd, vd, CONV_CH, CONV_CH))
            s0 = jnp.concatenate([jnp.zeros((BATCH,) + state_ret.shape[2:], F32), state_ret[:, e]], axis=0)
            o_f, s_f = _retention(ret_decay[e], q, k, v, cos_tab, sin_tab, s0, reverse=False)
            ret, s_b = _retention(ret_decay[e], q, k, v, cos_tab, sin_tab, s0, reverse=True,
                                  o_fwd=o_f, g=g, gn=ret_gn[e])
            u = _conv_module(a, ga, conv_w[e], conv_b[e], conv_ln_g[e], conv_ln_b[e])
            w_out = even_w_out[e].astype(BF16)
            mixes, w_parts = [ret, u], [w_out[:vd], w_out[vd:]]
            new_ret.append(jnp.stack([s_f[:BATCH], s_b[:BATCH]], axis=1))
        else:
            o = l // 2
            qd, vd = GLA_HEADS * GLA_DK, GLA_HEADS * GLA_DV
            q, k, v, r, alr = _inproj(x, norm1_g[l], mod[0], mod[1], odd_w_in[o].astype(BF16),
                                      (qd, qd, vd, vd, 2 * GLA_RANK))
            s0t = jnp.concatenate([jnp.zeros((BATCH,) + state_gla.shape[2:], F32), state_gla[:, o]], axis=0)
            s0t = jnp.swapaxes(s0t, -1, -2)
            zeros = jnp.zeros((GLA_RANK, qd), F32)
            wa_f = jnp.concatenate([gla_w_a2[o, 0], zeros], axis=0)
            wa_b = jnp.concatenate([zeros, gla_w_a2[o, 1]], axis=0)
            o_f, s_f = _gla(alr, q, k, v, wa_f, gla_b_a2[o, 0].reshape(1, qd), s0t, reverse=False)
            y, s_b = _gla(alr, q, k, v, wa_b, gla_b_a2[o, 1].reshape(1, qd), s0t, reverse=True,
                          o_fwd=o_f, r=r, gn=gla_gn[o])
            mixes, w_parts = [y], [odd_w_out[o].astype(BF16)]
            new_gla.append(jnp.swapaxes(jnp.stack([s_f[:BATCH], s_b[:BATCH]], axis=1), -1, -2))
        x1, h2, idx_t, rank_t, gates, counts = _outproj(mixes, w_parts, x, mod[2], norm2_g[l], mod[3], mod[4],
                                                        router_w[l], router_b[l])
        block_e, n_valid, next_e, dest_t = _routing_plan(counts, idx_t, rank_t)
        xs = _sc_scatter_rows(h2, dest_t, MOE_ROWS)
        yb = _moe_experts(l, block_e, n_valid, next_e, xs, exp_w_gu, exp_b_gu, exp_w_down, exp_b_down)
        yg = _sc_gather_rows(yb, dest_t.reshape(TOP_K * NT))
        x = _combine(x1, yg, gates, mod[5])
    y_prompt, y_sample = _final_norm(x, final_g)
    y_prompt = y_prompt.reshape(BATCH, SEQ, D_MODEL)
    y_sample = y_sample.reshape(DEC_BATCH, DEC_SEQ, D_MODEL)
    return (y_prompt, y_sample, jnp.stack(new_ret, axis=1), jnp.stack(new_gla, axis=1))
```

```python
import functools

import jax
import jax.numpy as jnp
from jax import lax
from jax.experimental import pallas as pl
from jax.experimental.pallas import tpu as pltpu
from jax.experimental.pallas import tpu_sc as plsc

F32 = jnp.float32
BF16 = jnp.bfloat16

D_MODEL = 1024
BATCH = 16
SEQ = 256
DEPTH = 4
DEC_BATCH = 4
DEC_SEQ = 4096
GRID_W = 64
RET_HEADS = 4
RET_DK = 64
RET_DV = 128
RET_CHUNK = 128
CONV_CH = 512
CONV_WIDTH = 31
CONV_PAD = CONV_WIDTH // 2
GLA_HEADS = 4
GLA_DK = 128
GLA_DV = 256
GLA_RANK = 16
GLA_TAU = 16.0
GLA_CHUNK = 64
GLA_SUB = 16
N_EXPERTS = 32
TOP_K = 4
D_FF = 1024
SWIGLU_LIMIT = 7.0
SWIGLU_ALPHA = 1.702
MOE_BLOCK = 128
ROPE_THETA = 10000.0
EPS = 1e-6
N_MOD = 6

RB = 256
NT_PROMPT = BATCH * SEQ
NT = NT_PROMPT + DEC_BATCH * DEC_SEQ
NBLK = NT // RB
N_PROMPT_BLK = NT_PROMPT // RB
SAMPLE_BLK = DEC_SEQ // RB
NSEQ = BATCH + DEC_BATCH
MOD_ROWS = 8
HALO = 16
VMEM_LIMIT = 48 * 1024 * 1024

assert SEQ == RB and DEC_SEQ % RB == 0 and CONV_PAD < HALO


def _seq_of_block(i):
    return jnp.where(i < N_PROMPT_BLK, i, N_PROMPT_BLK + (i - N_PROMPT_BLK) // SAMPLE_BLK)


def _is_first_block(i):
    return jnp.logical_or(i < N_PROMPT_BLK, (i - N_PROMPT_BLK) % SAMPLE_BLK == 0)


def _is_last_block(i):
    return jnp.logical_or(i < N_PROMPT_BLK, (i - N_PROMPT_BLK) % SAMPLE_BLK == SAMPLE_BLK - 1)


def _mod_row(i):
    return jnp.where(i < N_PROMPT_BLK, 0, 1 + (i - N_PROMPT_BLK) // SAMPLE_BLK)


def _rope_block(i):
    return jnp.where(i < N_PROMPT_BLK, 0, 1 + (i - N_PROMPT_BLK) % SAMPLE_BLK)


def _dot(a, b):
    return jnp.dot(a, b, preferred_element_type=F32)


def _dot_nt(a, b):
    return lax.dot_general(a, b, (((1,), (1,)), ((), ())), preferred_element_type=F32)


def _dot_tn(a, b):
    return lax.dot_general(a, b, (((0,), (0,)), ((), ())), preferred_element_type=F32)


def _split2(a):
    hi = a.astype(BF16)
    lo = (a - hi.astype(F32)).astype(BF16)
    return hi, lo


def _dot_hi(a, b):
    a_hi, a_lo = _split2(a)
    b_hi, b_lo = _split2(b)
    return _dot(a_hi, b_hi) + (_dot(a_hi, b_lo) + _dot(a_lo, b_hi))


def _silu(x):
    return x * (1.0 / (1.0 + jnp.exp(-x)))


def _sigmoid(x):
    return 1.0 / (1.0 + jnp.exp(-x))


def _pack_rows(x):
    n = x.shape[1] // 2
    lo = pltpu.bitcast(x[:, :n].astype(BF16).astype(F32), jnp.uint32)
    hi = pltpu.bitcast(x[:, n:].astype(BF16).astype(F32), jnp.uint32)
    return hi | (lo >> 16)


def _unpack_rows(u):
    lo = pltpu.bitcast(u << 16, F32)
    hi = pltpu.bitcast(u & jnp.uint32(0xFFFF0000), F32)
    return lo, hi


def _params(n_axes=1, vmem=VMEM_LIMIT):
    return pltpu.CompilerParams(dimension_semantics=("arbitrary",) * n_axes, vmem_limit_bytes=vmem)


MOD_TN = 1536


def _mod_kernel(c_ref, w_ref, b_ref, o_ref):
    s = _silu(c_ref[...]).astype(BF16)
    o_ref[...] = _dot(s, w_ref[...].astype(BF16)) + b_ref[...]


def _modulation(cvec, w_mod, b_mod):
    n = N_MOD * D_MODEL
    return pl.pallas_call(
        _mod_kernel,
        grid=(DEPTH, n // MOD_TN),
        in_specs=[pl.BlockSpec((MOD_ROWS, D_MODEL), lambda l, j: (0, 0)),
                  pl.BlockSpec((None, D_MODEL, MOD_TN), lambda l, j: (l, 0, j)),
                  pl.BlockSpec((None, 1, MOD_TN), lambda l, j: (l, 0, j))],
        out_specs=pl.BlockSpec((None, MOD_ROWS, MOD_TN), lambda l, j: (l, 0, j)),
        out_shape=jax.ShapeDtypeStruct((DEPTH, MOD_ROWS, n), F32),
        compiler_params=_params(2),
        name="modulation",
    )(cvec, w_mod, b_mod.reshape(DEPTH, 1, n))


def _norm_mod(x, g_ref, shift_ref, scale_ref, row):
    y = x * lax.rsqrt(jnp.mean(x * x, axis=-1, keepdims=True) + EPS) * g_ref[...]
    return y * (1.0 + scale_ref[pl.ds(row, 1), :]) + shift_ref[pl.ds(row, 1), :]


def _inproj_kernel(x_ref, g_ref, shift_ref, scale_ref, w_ref, *o_refs, widths):
    row = _mod_row(pl.program_id(0))
    hb = _norm_mod(x_ref[...], g_ref, shift_ref, scale_ref, row).astype(BF16)
    off = 0
    for o_ref, width in zip(o_refs, widths):
        o_ref[...] = _dot(hb, w_ref[:, off:off + width])
        off += width


def _inproj(x, g, shift, scale, w_bf, widths):
    n_in = w_bf.shape[1]
    row_spec = lambda width: pl.BlockSpec((RB, width), lambda i: (i, 0))
    full = lambda shape: pl.BlockSpec(shape, lambda i: (0,) * len(shape))
    return pl.pallas_call(
        functools.partial(_inproj_kernel, widths=widths),
        grid=(NBLK,),
        in_specs=[row_spec(D_MODEL), full((1, D_MODEL)), full((MOD_ROWS, D_MODEL)),
                  full((MOD_ROWS, D_MODEL)), full((D_MODEL, n_in))],
        out_specs=[row_spec(width) for width in widths],
        out_shape=[jax.ShapeDtypeStruct((NT, width), F32) for width in widths],
        compiler_params=_params(),
        name="inproj",
    )(x, g.reshape(1, D_MODEL), shift, scale, w_bf)


RC = RET_CHUNK
RET_PAIR = 2 * RET_DK


def _rope(x, cos, sin_signed):
    lane = lax.broadcasted_iota(jnp.int32, x.shape, 1)
    swapped = jnp.where(lane % 2 == 0, pltpu.roll(x, x.shape[1] - 1, 1), pltpu.roll(x, 1, 1))
    return x * cos + swapped * sin_signed


def _ret_kernel(decay_ref, q_ref, k_ref, v_ref, cos_ref, sin_ref, s0_ref, *rest, reverse):
    if reverse:
        of_ref, g_ref, gn_ref, o_ref, sfin_ref, st_ref, dm_ref, dq_ref, dk_ref, ds_ref = rest
    else:
        o_ref, sfin_ref, st_ref, dm_ref, dq_ref, dk_ref, ds_ref = rest
    step = pl.program_id(0)
    blk = NBLK - 1 - step if reverse else step
    direction = 1 if reverse else 0

    @pl.when(step == 0)
    def _():
        row = lax.broadcasted_iota(jnp.int32, (RC, RC), 0).astype(F32)
        col = lax.broadcasted_iota(jnp.int32, (RC, RC), 1).astype(F32)
        for h in range(RET_HEADS):
            lg = -jnp.exp(jnp.full((RC, RC), decay_ref[direction, h], F32))
            if reverse:
                diff = col - row
                mask = diff > 0
                q_pow = RC - row
                k_pow = row
            else:
                diff = row - col
                mask = diff >= 0
                q_pow = row + 1.0
                k_pow = RC - 1.0 - row
            dm_ref[h] = jnp.where(mask, jnp.exp(lg * jnp.where(mask, diff, 0.0)), 0.0)
            dq_ref[h] = jnp.exp(lg * q_pow)
            dk_ref[h] = jnp.exp(lg * k_pow)
            ds_ref[h] = jnp.exp(lg * RC)

    starts = _is_last_block(blk) if reverse else _is_first_block(blk)

    @pl.when(starts)
    def _():
        st_ref[...] = jnp.zeros_like(st_ref)
        for h in range(RET_HEADS):
            off = (h % 2) * RET_DK
            st_ref[h, off:off + RET_DK, :] = s0_ref[h]

    lane = lax.broadcasted_iota(jnp.int32, (1, RET_PAIR), 1)
    chunks = range(RB // RC)
    chunk_order = list(reversed(chunks) if reverse else chunks)
    units = [(h, c) for h in range(RET_HEADS) for c in chunk_order]
    rows_of = lambda c: slice(c * RC, (c + 1) * RC)
    vcols = lambda h: slice(h * RET_DV, (h + 1) * RET_DV)
    roped = {}
    for p in range(RET_HEADS // 2):
        cols = slice(p * RET_PAIR, (p + 1) * RET_PAIR)
        for c in chunk_order:
            rows = rows_of(c)
            cos, sin = cos_ref[rows, :], sin_ref[rows, :]
            roped[p, c] = (_rope(q_ref[rows, cols], cos, sin),
                           _rope(k_ref[rows, cols] * (RET_DK ** -0.5), cos, sin))
    qm_u, vh_u, att_u, kd_u = {}, {}, {}, {}
    for h, c in units:
        head_mask = (lane // RET_DK == h % 2).astype(F32)
        q2, k2 = roped[h // 2, c]
        vh_u[h, c] = v_ref[rows_of(c), vcols(h)].astype(BF16)
        qm_u[h, c] = (q2 * head_mask).astype(BF16)
        km = k2 * head_mask
        att_u[h, c] = (_dot_nt(qm_u[h, c], km.astype(BF16)) * dm_ref[h]).astype(BF16)
        kd_u[h, c] = (km * dk_ref[h]).astype(BF16)
    intra_u = {u: _dot(att_u[u], vh_u[u]) for u in units}
    delta_u = {u: _dot_tn(kd_u[u], vh_u[u]) for u in units}
    state_u = {}
    for h in range(RET_HEADS):
        st = st_ref[h]
        for c in chunk_order:
            state_u[h, c] = st.astype(BF16)
            st = st * ds_ref[h] + delta_u[h, c]
        st_ref[h] = st
    for h, c in units:
        rows, out_cols = rows_of(c), vcols(h)
        o = intra_u[h, c] + _dot(qm_u[h, c], state_u[h, c]) * dq_ref[h]
        if reverse:
            o = o + of_ref[rows, out_cols]
            o = o * lax.rsqrt(jnp.mean(o * o, axis=-1, keepdims=True) + EPS)
            o = o * gn_ref[:, out_cols] * _silu(g_ref[rows, out_cols])
            o_ref[rows, out_cols] = o.astype(o_ref.dtype)
        else:
            o_ref[rows, out_cols] = o

    ends = _is_first_block(blk) if reverse else _is_last_block(blk)

    @pl.when(ends)
    def _():
        for h in range(RET_HEADS):
            off = (h % 2) * RET_DK
            sfin_ref[h] = st_ref[h, off:off + RET_DK, :]


def _retention(decay, q, k, v, cos_tab, sin_tab, s0, *, reverse, o_fwd=None, g=None, gn=None):
    bmap = (lambda j: NBLK - 1 - j) if reverse else (lambda j: j)
    direction = 1 if reverse else 0
    qd, vd = RET_HEADS * RET_DK, RET_HEADS * RET_DV
    row_spec = lambda width: pl.BlockSpec((RB, width), lambda j: (bmap(j), 0))
    state_spec = pl.BlockSpec((None, None, RET_HEADS, RET_DK, RET_DV),
                              lambda j: (_seq_of_block(bmap(j)), direction, 0, 0, 0))
    in_specs = [pl.BlockSpec(memory_space=pltpu.SMEM), row_spec(qd), row_spec(qd), row_spec(vd),
                pl.BlockSpec((RB, RET_PAIR), lambda j: (_rope_block(bmap(j)), 0)),
                pl.BlockSpec((RB, RET_PAIR), lambda j: (_rope_block(bmap(j)), 0)),
                state_spec]
    args = [decay, q, k, v, cos_tab, sin_tab, s0]
    if reverse:
        in_specs += [row_spec(vd), row_spec(vd), pl.BlockSpec((1, vd), lambda j: (0, 0))]
        args += [o_fwd, g, gn.reshape(1, vd)]
    tile = pltpu.VMEM((RET_HEADS, RC, RC), F32)
    return pl.pallas_call(
        functools.partial(_ret_kernel, reverse=reverse),
        grid=(NBLK,),
        in_specs=in_specs,
        out_specs=[row_spec(vd),
                   pl.BlockSpec((None, RET_HEADS, RET_DK, RET_DV),
                                lambda j: (_seq_of_block(bmap(j)), 0, 0, 0))],
        out_shape=[jax.ShapeDtypeStruct((NT, vd), BF16 if reverse else F32),
                   jax.ShapeDtypeStruct((NSEQ, RET_HEADS, RET_DK, RET_DV), F32)],
        scratch_shapes=[tile, tile, tile, tile, tile],
        compiler_params=_params(),
        name="retention_bwd" if reverse else "retention_fwd",
    )(*args)


CONV_RT = 32
CONV_CT = 128
CONV_SPAN = RB + 2 * HALO - 8


def _conv_kernel(a_ref, ga_ref, ap_ref, gap_ref, an_ref, gan_ref, cw_ref, cb_ref, lng_ref, lnb_ref,
                 o_ref, u_ref, y_ref, us_ref):
    blk = pl.program_id(0)
    keep_prev = jnp.where(_is_first_block(blk), 0.0, 1.0)
    keep_next = jnp.where(_is_last_block(blk), 0.0, 1.0)
    u_ref[0:HALO, :] = ap_ref[...] * _sigmoid(gap_ref[...]) * keep_prev
    u_ref[HALO:HALO + RB, :] = a_ref[...] * _sigmoid(ga_ref[...])
    u_ref[HALO + RB:HALO + RB + HALO, :] = an_ref[...] * _sigmoid(gan_ref[...]) * keep_next
    for r in range(1, 8):
        us_ref[r - 1] = u_ref[r:r + CONV_SPAN, :]
    for ct in range(CONV_CH // CONV_CT):
        cols = slice(ct * CONV_CT, (ct + 1) * CONV_CT)
        for rt in range(RB // CONV_RT):
            acc = jnp.zeros((CONV_RT, CONV_CT), F32)
            for w in range(CONV_WIDTH):
                tiles, r = divmod(HALO - CONV_PAD + w, 8)
                base = rt * CONV_RT + 8 * tiles
                src = u_ref if r == 0 else us_ref.at[r - 1]
                acc = acc + src[base:base + CONV_RT, cols] * cw_ref[w:w + 1, cols]
            y_ref[rt * CONV_RT:(rt + 1) * CONV_RT, cols] = acc + cb_ref[:, cols]
    y = y_ref[...]
    mu = jnp.mean(y, axis=-1, keepdims=True)
    var = jnp.mean(jnp.square(y - mu), axis=-1, keepdims=True)
    o_ref[...] = _silu((y - mu) * lax.rsqrt(var + EPS) * lng_ref[...] + lnb_ref[...]).astype(o_ref.dtype)


def _conv_module(a, ga, cw, cb, lng, lnb):
    per_blk = RB // HALO
    n_halo = NT // HALO
    row_spec = pl.BlockSpec((RB, CONV_CH), lambda i: (i, 0))
    prev_spec = pl.BlockSpec((HALO, CONV_CH), lambda i: (jnp.maximum(i * per_blk - 1, 0), 0))
    next_spec = pl.BlockSpec((HALO, CONV_CH), lambda i: (jnp.minimum((i + 1) * per_blk, n_halo - 1), 0))
    vec = pl.BlockSpec((1, CONV_CH), lambda i: (0, 0))
    return pl.pallas_call(
        _conv_kernel,
        grid=(NBLK,),
        in_specs=[row_spec, row_spec, prev_spec, prev_spec, next_spec, next_spec,
                  pl.BlockSpec((CONV_WIDTH, CONV_CH), lambda i: (0, 0)), vec, vec, vec],
        out_specs=row_spec,
        out_shape=jax.ShapeDtypeStruct((NT, CONV_CH), BF16),
        scratch_shapes=[pltpu.VMEM((RB + 2 * HALO, CONV_CH), F32), pltpu.VMEM((RB, CONV_CH), F32),
                        pltpu.VMEM((7, CONV_SPAN, CONV_CH), F32)],
        compiler_params=_params(),
        name="conv_module",
    )(a, ga, a, ga, a, ga, cw, cb.reshape(1, -1), lng.reshape(1, -1), lnb.reshape(1, -1))


GC = GLA_CHUNK
GLA_NSUB = GC // GLA_SUB


def _split3(a):
    p1 = a.astype(BF16)
    r1 = a - p1.astype(F32)
    p2 = r1.astype(BF16)
    p3 = (r1 - p2.astype(F32)).astype(BF16)
    return p1, p2, p3


def _gla_kernel(alr_ref, alr_next_ref, q_ref, k_ref, v_ref, wa_ref, ba_ref, s0_ref, *rest, reverse):
    if reverse:
        of_ref, r_ref, gn_ref, o_ref, sfin_ref, st_ref, b2_ref, tri_ref, ob_ref = rest
    else:
        o_ref, sfin_ref, st_ref, b2_ref, tri_ref = rest
        ob_ref = o_ref
    step = pl.program_id(0)
    blk = NBLK - 1 - step if reverse else step
    starts = _is_last_block(blk) if reverse else _is_first_block(blk)

    @pl.when(starts)
    def _():
        st_ref[...] = s0_ref[...]

    def cumulative_gates(alr):
        z = _dot_hi(alr, wa_ref[...]) + ba_ref[...]
        log_a = (jnp.minimum(z, 0.0) - jnp.log(1.0 + jnp.exp(-jnp.abs(z)))) * (1.0 / GLA_TAU)
        g1, g2, g3 = _split3(log_a)
        tri = tri_ref[...]
        return _dot(tri, g1) + (_dot(tri, g2) + _dot(tri, g3))

    slot = step % 2

    @pl.when(step == 0)
    def _():
        row = lax.broadcasted_iota(jnp.int32, (RB, RB), 0)
        col = lax.broadcasted_iota(jnp.int32, (RB, RB), 1)
        ordered = col >= row if reverse else col <= row
        tri_ref[...] = jnp.where(jnp.logical_and(row // GC == col // GC, ordered), 1.0, 0.0).astype(BF16)
        b2_ref[0] = cumulative_gates(alr_ref[...])

    b2_ref[1 - slot] = cumulative_gates(alr_next_ref[...])
    b_ref = b2_ref.at[slot]

    c_row = lax.broadcasted_iota(jnp.int32, (GC, 1), 0)
    a_row = lax.broadcasted_iota(jnp.int32, (GC, GC), 0)
    a_col = lax.broadcasted_iota(jnp.int32, (GC, GC), 1)
    att_mask = a_col > a_row if reverse else a_col <= a_row
    chunks = range(RB // GC)
    chunk_order = list(reversed(chunks) if reverse else chunks)
    units = [(h, c) for h in range(GLA_HEADS) for c in chunk_order]
    kcols = lambda h: slice(h * GLA_DK, (h + 1) * GLA_DK)
    vcols = lambda h: slice(h * GLA_DV, (h + 1) * GLA_DV)
    rows_of = lambda c: slice(c * GC, (c + 1) * GC)

    vh_u, qe_u, ke_u, decay_u, att_u = {}, {}, {}, {}, {}
    for h, c in units:
        rows = rows_of(c)
        b = b_ref[rows, kcols(h)]
        qh = q_ref[rows, kcols(h)] * (GLA_DK ** -0.5)
        kh = k_ref[rows, kcols(h)]
        vh_u[h, c] = v_ref[rows, vcols(h)].astype(BF16)
        edge = b[0:1, :] if reverse else b[GC - 1:GC, :]
        bounds = []
        for s in range(GLA_NSUB):
            if reverse:
                hi = (s + 1) * GLA_SUB
                bounds.append(b[hi:hi + 1, :] if s < GLA_NSUB - 1 else jnp.zeros((1, GLA_DK), F32))
            else:
                lo = s * GLA_SUB
                bounds.append(b[lo - 1:lo, :] if s > 0 else jnp.zeros((1, GLA_DK), F32))
        own = jnp.concatenate([jnp.broadcast_to(bd, (GLA_SUB, GLA_DK)) for bd in bounds], axis=0)
        q_own = qh * jnp.exp(b - own)
        q_parts, k_parts = [], []
        for s, bd in enumerate(bounds):
            q_parts.append(jnp.where(c_row // GLA_SUB == s, q_own, 0.0))
            reach = c_row >= s * GLA_SUB if reverse else c_row < (s + 1) * GLA_SUB
            k_parts.append(kh * jnp.exp(jnp.where(reach, bd - b, -jnp.inf)))
        q_bd = jnp.concatenate(q_parts, axis=1).astype(BF16)
        k_cat = jnp.concatenate(k_parts, axis=1).astype(BF16)
        att_u[h, c] = jnp.where(att_mask, _dot_nt(q_bd, k_cat), 0.0).astype(BF16)
        qe_u[h, c] = (qh * jnp.exp(b)).astype(BF16)
        ke_u[h, c] = (kh * jnp.exp(edge - b)).astype(BF16)
        decay_u[h, c] = jnp.exp(edge)
    intra_u = {u: _dot(att_u[u], vh_u[u]) for u in units}
    delta_u = {u: _dot_tn(vh_u[u], ke_u[u]) for u in units}
    state_u = {}
    for h in range(GLA_HEADS):
        st = st_ref[h]
        for c in chunk_order:
            state_u[h, c] = st.astype(BF16)
            st = st * decay_u[h, c] + delta_u[h, c]
        st_ref[h] = st
    for h, c in units:
        ob_ref[rows_of(c), vcols(h)] = intra_u[h, c] + _dot_nt(qe_u[h, c], state_u[h, c])

    if reverse:
        for h in range(GLA_HEADS):
            cols = vcols(h)
            o = ob_ref[:, cols] + of_ref[:, cols]
            o = o * lax.rsqrt(jnp.mean(o * o, axis=-1, keepdims=True) + EPS)
            o_ref[:, cols] = (o * gn_ref[:, cols] * _silu(r_ref[:, cols])).astype(o_ref.dtype)

    ends = _is_first_block(blk) if reverse else _is_last_block(blk)

    @pl.when(ends)
    def _():
        sfin_ref[...] = st_ref[...]


def _gla(alr, q, k, v, wa, ba, s0t, *, reverse, o_fwd=None, r=None, gn=None):
    bmap = (lambda j: NBLK - 1 - j) if reverse else (lambda j: j)
    direction = 1 if reverse else 0
    qd, vd = GLA_HEADS * GLA_DK, GLA_HEADS * GLA_DV
    row_spec = lambda width: pl.BlockSpec((RB, width), lambda j: (bmap(j), 0))
    state_spec = pl.BlockSpec((None, None, GLA_HEADS, GLA_DV, GLA_DK),
                              lambda j: (_seq_of_block(bmap(j)), direction, 0, 0, 0))
    next_spec = pl.BlockSpec((RB, 2 * GLA_RANK), lambda j: (bmap(jnp.minimum(j + 1, NBLK - 1)), 0))
    in_specs = [row_spec(2 * GLA_RANK), next_spec, row_spec(qd), row_spec(qd), row_spec(vd),
                pl.BlockSpec((2 * GLA_RANK, qd), lambda j: (0, 0)),
                pl.BlockSpec((1, qd), lambda j: (0, 0)), state_spec]
    args = [alr, alr, q, k, v, wa, ba, s0t]
    scratch = [pltpu.VMEM((GLA_HEADS, GLA_DV, GLA_DK), F32), pltpu.VMEM((2, RB, qd), F32),
               pltpu.VMEM((RB, RB), BF16)]
    if reverse:
        in_specs += [row_spec(vd), row_spec(vd), pl.BlockSpec((1, vd), lambda j: (0, 0))]
        args += [o_fwd, r, gn.reshape(1, vd)]
        scratch += [pltpu.VMEM((RB, vd), F32)]
    return pl.pallas_call(
        functools.partial(_gla_kernel, reverse=reverse),
        grid=(NBLK,),
        in_specs=in_specs,
        out_specs=[row_spec(vd),
                   pl.BlockSpec((None, GLA_HEADS, GLA_DV, GLA_DK),
                                lambda j: (_seq_of_block(bmap(j)), 0, 0, 0))],
        out_shape=[jax.ShapeDtypeStruct((NT, vd), BF16 if reverse else F32),
                   jax.ShapeDtypeStruct((NSEQ, GLA_HEADS, GLA_DV, GLA_DK), F32)],
        scratch_shapes=scratch,
        compiler_params=_params(),
        name="gla_bwd" if reverse else "gla_fwd",
    )(*args)


def _outproj_kernel(*refs, n_mix):
    mix_refs = refs[:n_mix]
    w_refs = refs[n_mix:2 * n_mix]
    (x_ref, gate_ref, g2_ref, shift_ref, scale_ref, rw_ref, rb_ref,
     x1_ref, h2_ref, idx_ref, rank_ref, gw_ref, cnt_ref, carry_ref) = refs[2 * n_mix:]
    step = pl.program_id(0)
    row = _mod_row(step)
    m = _dot(mix_refs[0][...], w_refs[0][...])
    for mix_ref, w_ref in zip(mix_refs[1:], w_refs[1:]):
        m = m + _dot(mix_ref[...], w_ref[...])
    x1 = x_ref[...] + gate_ref[pl.ds(row, 1), :] * m
    x1_ref[...] = x1
    h2 = _norm_mod(x1, g2_ref, shift_ref, scale_ref, row)
    h2_ref[...] = _pack_rows(h2)
    h_hi, h_lo = _split2(h2)
    r_hi, r_lo = _split2(rw_ref[...])
    logits = _dot_nt(r_hi, h_hi) + (_dot_nt(r_hi, h_lo) + _dot_nt(r_lo, h_hi)) + rb_ref[...]

    @pl.when(step == 0)
    def _():
        carry_ref[...] = jnp.zeros_like(carry_ref)

    eid = lax.broadcasted_iota(jnp.int32, logits.shape, 0).astype(F32)
    work = logits
    onehots, top_vals = [], []
    for kk in range(TOP_K):
        top = jnp.max(work, axis=0, keepdims=True)
        first = jnp.min(jnp.where(work == top, eid, float(N_EXPERTS)), axis=0, keepdims=True)
        onehot = eid == first
        idx_ref[kk:kk + 1, :] = first.astype(jnp.int32)
        onehots.append(onehot)
        top_vals.append(top)
        work = jnp.where(onehot, -jnp.inf, work)
    exps = [jnp.exp(v - top_vals[0]) for v in top_vals]
    denom = exps[0]
    for e in exps[1:]:
        denom = denom + e
    slot = lax.broadcasted_iota(jnp.int32, (8, RB), 0)
    gates = jnp.zeros((8, RB), F32)
    for kk in range(TOP_K):
        gates = jnp.where(slot == kk, exps[kk] / denom, gates)
    r_i = lax.broadcasted_iota(jnp.int32, (RB, RB), 0)
    c_i = lax.broadcasted_iota(jnp.int32, (RB, RB), 1)
    eye = jnp.where(r_i == c_i, 1.0, 0.0).astype(BF16)
    g1, g2_, g3 = _split3(gates)
    gw_ref[...] = (_dot_nt(eye, g1) + (_dot_nt(eye, g2_) + _dot_nt(eye, g3)))[:, :TOP_K]

    sel = jnp.zeros(logits.shape, F32)
    for onehot in onehots:
        sel = sel + jnp.where(onehot, 1.0, 0.0)
    sel = sel.astype(BF16)
    before = jnp.where(r_i < c_i, 1.0, 0.0).astype(BF16)
    pos = _dot(sel, before) + carry_ref[...]
    for kk in range(TOP_K):
        rank = jnp.sum(jnp.where(onehots[kk], pos, 0.0), axis=0, keepdims=True)
        rank_ref[kk:kk + 1, :] = rank.astype(jnp.int32)
    carry = carry_ref[...] + _dot(sel, jnp.ones((RB, RB), BF16))
    carry_ref[...] = carry
    cnt_ref[...] = carry.astype(jnp.int32)


def _outproj(mixes, w_parts, x, gate, g2, shift, scale, rw, rb):
    n_mix = len(mixes)
    row_spec = lambda width: pl.BlockSpec((RB, width), lambda i: (i, 0))
    col_spec = pl.BlockSpec((TOP_K, RB), lambda i: (0, i))
    full = lambda shape: pl.BlockSpec(shape, lambda i: (0,) * len(shape))
    mod_spec = full((MOD_ROWS, D_MODEL))
    return pl.pallas_call(
        functools.partial(_outproj_kernel, n_mix=n_mix),
        grid=(NBLK,),
        in_specs=[row_spec(m.shape[1]) for m in mixes] + [full(w.shape) for w in w_parts]
        + [row_spec(D_MODEL), mod_spec, full((1, D_MODEL)), mod_spec, mod_spec,
           full((N_EXPERTS, D_MODEL)), full((N_EXPERTS, 1))],
        out_specs=[row_spec(D_MODEL), row_spec(D_MODEL // 2), col_spec, col_spec, row_spec(TOP_K),
                   full((N_EXPERTS, RB))],
        out_shape=[jax.ShapeDtypeStruct((NT, D_MODEL), F32), jax.ShapeDtypeStruct((NT, D_MODEL // 2), jnp.uint32),
                   jax.ShapeDtypeStruct((TOP_K, NT), jnp.int32), jax.ShapeDtypeStruct((TOP_K, NT), jnp.int32),
                   jax.ShapeDtypeStruct((NT, TOP_K), F32), jax.ShapeDtypeStruct((N_EXPERTS, RB), jnp.int32)],
        scratch_shapes=[pltpu.VMEM((N_EXPERTS, RB), F32)],
        compiler_params=_params(),
        name="outproj",
    )(*mixes, *w_parts, x, gate, g2.reshape(1, D_MODEL), shift, scale, rw.T, rb.reshape(N_EXPERTS, 1))


TM = 512
TM_SUB = 256
MOE_NBLK = NT * TOP_K // TM + N_EXPERTS
MOE_ROWS = MOE_NBLK * TM
HALF = D_MODEL // 2


def _moe_kernel(be_ref, nv_ref, nx_ref, x_ref, wgu_hbm, bgu_ref, wd_hbm, bd_ref, y_ref,
                wgu_st, wd_st, wgu_bf, wd_bf, sems, *, layer):
    i = pl.program_id(0)
    n_valid = nv_ref[i]

    def fetch(e):
        return (pltpu.make_async_copy(wgu_hbm.at[layer, e], wgu_st, sems.at[0]),
                pltpu.make_async_copy(wd_hbm.at[layer, e], wd_st, sems.at[1]))

    @pl.when(i == 0)
    def _():
        for cp in fetch(be_ref[0]):
            cp.start()

    @pl.when(n_valid > 0)
    def _():
        e = be_ref[i]
        changed = jnp.logical_or(i == 0, e != be_ref[jnp.maximum(i - 1, 0)])

        @pl.when(changed)
        def _():
            for cp in fetch(e):
                cp.wait()
            wgu_bf[...] = wgu_st[...].astype(BF16)
            wd_bf[...] = wd_st[...].astype(BF16)
            nxt = nx_ref[e]

            @pl.when(nxt >= 0)
            def _():
                for cp in fetch(nxt):
                    cp.start()

    for p in range(TM // TM_SUB):
        @pl.when(n_valid > p * TM_SUB)
        def _():
            rows = slice(p * TM_SUB, (p + 1) * TM_SUB)
            row_id = lax.broadcasted_iota(jnp.int32, (TM_SUB, 1), 0) + p * TM_SUB
            x_lo, x_hi = _unpack_rows(jnp.where(row_id < n_valid, x_ref[rows, :], jnp.uint32(0)))
            gu = (_dot(x_lo.astype(BF16), wgu_bf[:HALF, :]) + _dot(x_hi.astype(BF16), wgu_bf[HALF:, :])
                  + bgu_ref[...])
            gate = jnp.minimum(gu[:, :D_FF], SWIGLU_LIMIT)
            up = jnp.clip(gu[:, D_FF:], -SWIGLU_LIMIT, SWIGLU_LIMIT)
            hdn = gate * _sigmoid(SWIGLU_ALPHA * gate) * (up + 1.0)
            y_ref[rows, :] = _pack_rows(_dot(hdn.astype(BF16), wd_bf[...]) + bd_ref[...])


def _moe_experts(layer, block_e, n_valid, next_e, xs, w_gu, b_gu, w_down, b_down):
    grid_spec = pltpu.PrefetchScalarGridSpec(
        num_scalar_prefetch=3,
        grid=(MOE_NBLK,),
        in_specs=[pl.BlockSpec((TM, HALF), lambda i, be, nv, nx: (i, 0)),
                  pl.BlockSpec(memory_space=pl.ANY),
                  pl.BlockSpec((None, None, 1, 2 * D_FF), lambda i, be, nv, nx: (layer, be[i], 0, 0)),
                  pl.BlockSpec(memory_space=pl.ANY),
                  pl.BlockSpec((None, None, 1, D_MODEL), lambda i, be, nv, nx: (layer, be[i], 0, 0))],
        out_specs=pl.BlockSpec((TM, HALF), lambda i, be, nv, nx: (i, 0)),
        scratch_shapes=[pltpu.VMEM((D_MODEL, 2 * D_FF), F32), pltpu.VMEM((D_FF, D_MODEL), F32),
                        pltpu.VMEM((D_MODEL, 2 * D_FF), BF16), pltpu.VMEM((D_FF, D_MODEL), BF16),
                        pltpu.SemaphoreType.DMA((2,))],
    )
    return pl.pallas_call(
        functools.partial(_moe_kernel, layer=layer),
        grid_spec=grid_spec,
        out_shape=jax.ShapeDtypeStruct((MOE_ROWS, HALF), jnp.uint32),
        compiler_params=_params(),
        name="moe_experts",
    )(block_e, n_valid, next_e, xs, w_gu, b_gu.reshape(DEPTH, N_EXPERTS, 1, -1), w_down,
      b_down.reshape(DEPTH, N_EXPERTS, 1, -1))


SC_WORKERS = 32
SC_WIN = 64


def _sc_mesh():
    return plsc.VectorSubcoreMesh(core_axis_name="core", subcore_axis_name="subcore")


def _sc_worker():
    return lax.axis_index("core") * (SC_WORKERS // 2) + lax.axis_index("subcore")


def _sc_scatter_rows(x, dest_t, n_rows):
    n, width = x.shape
    kk = dest_t.shape[0]
    per = n // SC_WORKERS
    n_win = per // SC_WIN
    assert per * SC_WORKERS == n and n_win * SC_WIN == per and n_win % 2 == 0

    @pl.kernel(out_type=jax.ShapeDtypeStruct((n_rows, width), x.dtype), mesh=_sc_mesh(),
               scratch_types=[pltpu.VMEM((kk, per), jnp.int32), pltpu.VMEM((SC_WIN, width), x.dtype),
                              pltpu.VMEM((SC_WIN, width), x.dtype), pltpu.SemaphoreType.DMA((4,))])
    def scatter(x_hbm, i_hbm, o_hbm, idx_v, buf0, buf1, sems):
        base = _sc_worker() * per
        pltpu.sync_copy(i_hbm.at[:, pl.ds(base, per)], idx_v)

        def get(j, buf, s):
            return pltpu.make_async_copy(x_hbm.at[pl.ds(base + j * SC_WIN, SC_WIN)], buf, sems.at[s])

        def put(j, q, buf, s):
            return pltpu.make_async_copy(buf, o_hbm.at[idx_v.at[q, pl.ds(j * SC_WIN, SC_WIN)]], sems.at[s])

        get(0, buf0, 0).start()

        @pl.loop(0, n_win, step=2)
        def _(j):
            get(j, buf0, 0).wait()

            @pl.when(j > 0)
            def _():
                for q in range(kk):
                    put(j - 1, q, buf1, 3).wait()

            get(j + 1, buf1, 1).start()
            for q in range(kk):
                put(j, q, buf0, 2).start()
            get(j + 1, buf1, 1).wait()
            for q in range(kk):
                put(j, q, buf0, 2).wait()

            @pl.when(j + 2 < n_win)
            def _():
                get(j + 2, buf0, 0).start()

            for q in range(kk):
                put(j + 1, q, buf1, 3).start()

        for q in range(kk):
            put(n_win - 1, q, buf1, 3).wait()

    return scatter(x, dest_t)


def _sc_gather_rows(y, idx):
    n = idx.shape[0]
    width = y.shape[1]
    per = n // SC_WORKERS
    n_win = per // SC_WIN
    assert per * SC_WORKERS == n and n_win * SC_WIN == per and n_win % 2 == 0

    @pl.kernel(out_type=jax.ShapeDtypeStruct((n, width), y.dtype), mesh=_sc_mesh(),
               scratch_types=[pltpu.VMEM((per,), jnp.int32), pltpu.VMEM((SC_WIN, width), y.dtype),
                              pltpu.VMEM((SC_WIN, width), y.dtype), pltpu.SemaphoreType.DMA((4,))])
    def gather(y_hbm, i_hbm, o_hbm, idx_v, buf0, buf1, sems):
        base = _sc_worker() * per
        pltpu.sync_copy(i_hbm.at[pl.ds(base, per)], idx_v)

        def get(j, buf, s):
            return pltpu.make_async_copy(y_hbm.at[idx_v.at[pl.ds(j * SC_WIN, SC_WIN)]], buf, sems.at[s])

        def put(j, buf, s):
            return pltpu.make_async_copy(buf, o_hbm.at[pl.ds(base + j * SC_WIN, SC_WIN)], sems.at[s])

        get(0, buf0, 0).start()

        @pl.loop(0, n_win, step=2)
        def _(j):
            get(j, buf0, 0).wait()

            @pl.when(j > 0)
            def _():
                put(j - 1, buf1, 3).wait()

            get(j + 1, buf1, 1).start()
            put(j, buf0, 2).start()
            get(j + 1, buf1, 1).wait()
            put(j, buf0, 2).wait()

            @pl.when(j + 2 < n_win)
            def _():
                get(j + 2, buf0, 0).start()

            put(j + 1, buf1, 3).start()

        put(n_win - 1, buf1, 3).wait()

    return gather(y, idx)


def _combine_kernel(x1_ref, *rest):
    yg_refs = rest[:TOP_K]
    gw_ref, gate_ref, o_ref = rest[TOP_K:]
    row = _mod_row(pl.program_id(0))
    acc_lo, acc_hi = None, None
    for kk in range(TOP_K):
        y_lo, y_hi = _unpack_rows(yg_refs[kk][...])
        w = gw_ref[:, kk:kk + 1]
        acc_lo = y_lo * w if acc_lo is None else acc_lo + y_lo * w
        acc_hi = y_hi * w if acc_hi is None else acc_hi + y_hi * w
    o_ref[:, :HALF] = x1_ref[:, :HALF] + gate_ref[pl.ds(row, 1), :HALF] * acc_lo
    o_ref[:, HALF:] = x1_ref[:, HALF:] + gate_ref[pl.ds(row, 1), HALF:] * acc_hi


def _combine(x1, yg, gw, gate):
    row_spec = lambda width: pl.BlockSpec((RB, width), lambda i: (i, 0))
    slot_spec = lambda k: pl.BlockSpec((RB, HALF), lambda i: (k * NBLK + i, 0))
    return pl.pallas_call(
        _combine_kernel,
        grid=(NBLK,),
        in_specs=[row_spec(D_MODEL)] + [slot_spec(k) for k in range(TOP_K)]
        + [row_spec(TOP_K), pl.BlockSpec((MOD_ROWS, D_MODEL), lambda i: (0, 0))],
        out_specs=row_spec(D_MODEL),
        out_shape=jax.ShapeDtypeStruct((NT, D_MODEL), F32),
        compiler_params=_params(),
        name="moe_combine",
    )(x1, *([yg] * TOP_K), gw, gate)


def _final_norm_kernel(x_ref, g_ref, op_ref, os_ref):
    i = pl.program_id(0)
    x = x_ref[...]
    y = x * lax.rsqrt(jnp.mean(x * x, axis=-1, keepdims=True) + EPS) * g_ref[...]

    @pl.when(i < N_PROMPT_BLK)
    def _():
        op_ref[...] = y

    @pl.when(i >= N_PROMPT_BLK)
    def _():
        os_ref[...] = y


def _final_norm(x, g):
    row_spec = pl.BlockSpec((RB, D_MODEL), lambda i: (i, 0))
    return pl.pallas_call(
        _final_norm_kernel,
        grid=(NBLK,),
        in_specs=[row_spec, pl.BlockSpec((1, D_MODEL), lambda i: (0, 0))],
        out_specs=[pl.BlockSpec((RB, D_MODEL), lambda i: (jnp.minimum(i, N_PROMPT_BLK - 1), 0)),
                   pl.BlockSpec((RB, D_MODEL), lambda i: (jnp.maximum(i - N_PROMPT_BLK, 0), 0))],
        out_shape=[jax.ShapeDtypeStruct((NT_PROMPT, D_MODEL), F32),
                   jax.ShapeDtypeStruct((NT - NT_PROMPT, D_MODEL), F32)],
        compiler_params=_params(),
        name="final_norm",
    )(x, g.reshape(1, D_MODEL))


def _routing_plan(counts, idx_t, rank_t):
    counts = counts[:, 0]
    padded = (counts + TM - 1) // TM * TM
    pad_end = jnp.cumsum(padded)
    pad_start = pad_end - padded
    blk_row = (jnp.arange(MOE_NBLK, dtype=jnp.int32) * TM)[:, None]
    ids = jnp.arange(N_EXPERTS, dtype=jnp.int32)
    owns = jnp.logical_and(pad_start[None, :] <= blk_row, blk_row < pad_end[None, :])
    last_used = jnp.max(jnp.where(counts > 0, ids, 0))
    block_e = jnp.where(jnp.any(owns, axis=1), jnp.sum(jnp.where(owns, ids[None, :], 0), axis=1), last_used)
    block_e = block_e.astype(jnp.int32)
    left = jnp.clip(counts[None, :] - (blk_row - pad_start[None, :]), 0, TM)
    n_valid = jnp.sum(jnp.where(owns, left, 0), axis=1).astype(jnp.int32)
    start = jnp.zeros(idx_t.shape, jnp.int32)
    for e in range(N_EXPERTS):
        start = jnp.where(idx_t == e, pad_start[e], start)
    dest_t = (start + rank_t).astype(jnp.int32)
    later = jnp.where(jnp.logical_and(counts[None, :] > 0, ids[None, :] > ids[:, None]), ids[None, :], N_EXPERTS)
    next_e = jnp.min(later, axis=1)
    next_e = jnp.where(next_e == N_EXPERTS, -1, next_e).astype(jnp.int32)
    return block_e, n_valid, next_e, dest_t


def _rope_tables():
    rows = DEC_SEQ // GRID_W
    row = jnp.repeat(jnp.arange(rows, dtype=F32), GRID_W)
    col = jnp.tile(jnp.arange(GRID_W, dtype=F32), rows)
    n_f = RET_DK // 4
    freqs = ROPE_THETA ** (-jnp.arange(n_f, dtype=F32) / n_f)
    ang = jnp.concatenate([row[:, None] * freqs, col[:, None] * freqs], axis=-1)
    cos = jnp.repeat(jnp.cos(ang), 2, axis=-1)
    sin = jnp.repeat(jnp.sin(ang), 2, axis=-1) * jnp.tile(jnp.asarray([-1.0, 1.0], F32), RET_DK // 2)
    cos = jnp.concatenate([jnp.ones((RB, RET_DK), F32), cos], axis=0)
    sin = jnp.concatenate([jnp.zeros((RB, RET_DK), F32), sin], axis=0)
    return jnp.tile(cos, (1, 2)), jnp.tile(sin, (1, 2))


def kernel(x_prompt, x_sample, state_ret, state_gla, c, c_ctx, w_mod, b_mod, norm1_g, norm2_g, final_g, even_w_in, ret_decay, ret_gn, conv_w, conv_b, conv_ln_g, conv_ln_b, even_w_out, odd_w_in, gla_w_a2, gla_b_a2, gla_gn, odd_w_out, router_w, router_b, exp_w_gu, exp_b_gu, exp_w_down, exp_b_down):
    x = jnp.concatenate([x_prompt.reshape(NT_PROMPT, D_MODEL), x_sample.reshape(-1, D_MODEL)], axis=0)
    cvec = jnp.concatenate([c_ctx[None, :], c, jnp.zeros((MOD_ROWS - 1 - DEC_BATCH, D_MODEL), F32)], axis=0)
    mods = _modulation(cvec, w_mod, b_mod).reshape(DEPTH, MOD_ROWS, N_MOD, D_MODEL)
    cos_tab, sin_tab = _rope_tables()
    new_ret, new_gla = [], []
    for l in range(DEPTH):
        mod = [mods[l, :, j, :] for j in range(N_MOD)]
        if l % 2 == 0:
            e = l // 2
            qd, vd = RET_HEADS * RET_DK, RET_HEADS * RET_DV
            q, k, v, g, a, ga = _inproj(x, norm1_g[l], mod[0], mod[1], even_w_in[e].astype(BF16),
                                        (qd, qd, vd, vd, CONV_CH, CONV_CH))
            s0 = jnp.concatenate([jnp.zeros((BATCH,) + state_ret.shape[2:], F32), state_ret[:, e]], axis=0)
            o_f, s_f = _retention(ret_decay[e], q, k, v, cos_tab, sin_tab, s0, reverse=False)
            ret, s_b = _retention(ret_decay[e], q, k, v, cos_tab, sin_tab, s0, reverse=True,
                                  o_fwd=o_f, g=g, gn=ret_gn[e])
            u = _conv_module(a, ga, conv_w[e], conv_b[e], conv_ln_g[e], conv_ln_b[e])
            w_out = even_w_out[e].astype(BF16)
            mixes, w_parts = [ret, u], [w_out[:vd], w_out[vd:]]
            new_ret.append(jnp.stack([s_f[:BATCH], s_b[:BATCH]], axis=1))
        else:
            o = l // 2
            qd, vd = GLA_HEADS * GLA_DK, GLA_HEADS * GLA_DV
            q, k, v, r, alr = _inproj(x, norm1_g[l], mod[0], mod[1], odd_w_in[o].astype(BF16),
                                      (qd, qd, vd, vd, 2 * GLA_RANK))
            s0t = jnp.concatenate([jnp.zeros((BATCH,) + state_gla.shape[2:], F32), state_gla[:, o]], axis=0)
            s0t = jnp.swapaxes(s0t, -1, -2)
            zeros = jnp.zeros((GLA_RANK, qd), F32)
            wa_f = jnp.concatenate([gla_w_a2[o, 0], zeros], axis=0)
            wa_b = jnp.concatenate([zeros, gla_w_a2[o, 1]], axis=0)
            o_f, s_f = _gla(alr, q, k, v, wa_f, gla_b_a2[o, 0].reshape(1, qd), s0t, reverse=False)
            y, s_b = _gla(alr, q, k, v, wa_b, gla_b_a2[o, 1].reshape(1, qd), s0t, reverse=True,
                          o_fwd=o_f, r=r, gn=gla_gn[o])
            mixes, w_parts = [y], [odd_w_out[o].astype(BF16)]
            new_gla.append(jnp.swapaxes(jnp.stack([s_f[:BATCH], s_b[:BATCH]], axis=1), -1, -2))
        x1, h2, idx_t, rank_t, gates, counts = _outproj(mixes, w_parts, x, mod[2], norm2_g[l], mod[3], mod[4],
                                                        router_w[l], router_b[l])
        block_e, n_valid, next_e, dest_t = _routing_plan(counts, idx_t, rank_t)
        xs = _sc_scatter_rows(h2, dest_t, MOE_ROWS)
        yb = _moe_experts(l, block_e, n_valid, next_e, xs, exp_w_gu, exp_b_gu, exp_w_down, exp_b_down)
        yg = _sc_gather_rows(yb, dest_t.reshape(TOP_K * NT))
        x = _combine(x1, yg, gates, mod[5])
    y_prompt, y_sample = _final_norm(x, final_g)
    y_prompt = y_prompt.reshape(BATCH, SEQ, D_MODEL)
    y_sample = y_sample.reshape(DEC_BATCH, DEC_SEQ, D_MODEL)
    return (y_prompt, y_sample, jnp.stack(new_ret, axis=1), jnp.stack(new_gla, axis=1))
```

```python
import functools

import jax
import jax.numpy as jnp
from jax import lax
from jax.experimental import pallas as pl
from jax.experimental.pallas import tpu as pltpu
from jax.experimental.pallas import tpu_sc as plsc

F32 = jnp.float32
BF16 = jnp.bfloat16

D_MODEL = 1024
BATCH = 16
SEQ = 256
DEPTH = 4
DEC_BATCH = 4
DEC_SEQ = 4096
GRID_W = 64
RET_HEADS = 4
RET_DK = 64
RET_DV = 128
RET_CHUNK = 128
CONV_CH = 512
CONV_WIDTH = 31
CONV_PAD = CONV_WIDTH // 2
GLA_HEADS = 4
GLA_DK = 128
GLA_DV = 256
GLA_RANK = 16
GLA_TAU = 16.0
GLA_CHUNK = 64
GLA_SUB = 16
N_EXPERTS = 32
TOP_K = 4
D_FF = 1024
SWIGLU_LIMIT = 7.0
SWIGLU_ALPHA = 1.702
MOE_BLOCK = 128
ROPE_THETA = 10000.0
EPS = 1e-6
N_MOD = 6

RB = 256
NT_PROMPT = BATCH * SEQ
NT = NT_PROMPT + DEC_BATCH * DEC_SEQ
NBLK = NT // RB
N_PROMPT_BLK = NT_PROMPT // RB
SAMPLE_BLK = DEC_SEQ // RB
NSEQ = BATCH + DEC_BATCH
MOD_ROWS = 8
HALO = 16
VMEM_LIMIT = 48 * 1024 * 1024

assert SEQ == RB and DEC_SEQ % RB == 0 and CONV_PAD < HALO


def _seq_of_block(i):
    return jnp.where(i < N_PROMPT_BLK, i, N_PROMPT_BLK + (i - N_PROMPT_BLK) // SAMPLE_BLK)


def _is_first_block(i):
    return jnp.logical_or(i < N_PROMPT_BLK, (i - N_PROMPT_BLK) % SAMPLE_BLK == 0)


def _is_last_block(i):
    return jnp.logical_or(i < N_PROMPT_BLK, (i - N_PROMPT_BLK) % SAMPLE_BLK == SAMPLE_BLK - 1)


def _mod_row(i):
    return jnp.where(i < N_PROMPT_BLK, 0, 1 + (i - N_PROMPT_BLK) // SAMPLE_BLK)


def _rope_block(i):
    return jnp.where(i < N_PROMPT_BLK, 0, 1 + (i - N_PROMPT_BLK) % SAMPLE_BLK)


def _dot(a, b):
    return jnp.dot(a, b, preferred_element_type=F32)


def _dot_nt(a, b):
    return lax.dot_general(a, b, (((1,), (1,)), ((), ())), preferred_element_type=F32)


def _dot_tn(a, b):
    return lax.dot_general(a, b, (((0,), (0,)), ((), ())), preferred_element_type=F32)


def _split2(a):
    hi = a.astype(BF16)
    lo = (a - hi.astype(F32)).astype(BF16)
    return hi, lo


def _dot_hi(a, b):
    a_hi, a_lo = _split2(a)
    b_hi, b_lo = _split2(b)
    return _dot(a_hi, b_hi) + (_dot(a_hi, b_lo) + _dot(a_lo, b_hi))


def _silu(x):
    return x * (1.0 / (1.0 + jnp.exp(-x)))


def _sigmoid(x):
    return 1.0 / (1.0 + jnp.exp(-x))


def _pack_rows(x):
    n = x.shape[1] // 2
    lo = pltpu.bitcast(x[:, :n].astype(BF16).astype(F32), jnp.uint32)
    hi = pltpu.bitcast(x[:, n:].astype(BF16).astype(F32), jnp.uint32)
    return hi | (lo >> 16)


def _unpack_rows(u):
    lo = pltpu.bitcast(u << 16, F32)
    hi = pltpu.bitcast(u & jnp.uint32(0xFFFF0000), F32)
    return lo, hi


def _params(n_axes=1, vmem=VMEM_LIMIT):
    return pltpu.CompilerParams(dimension_semantics=("arbitrary",) * n_axes, vmem_limit_bytes=vmem)


MOD_TN = 1536


def _mod_kernel(c_ref, w_ref, b_ref, o_ref):
    s = _silu(c_ref[...]).astype(BF16)
    o_ref[...] = _dot(s, w_ref[...].astype(BF16)) + b_ref[...]


def _modulation(cvec, w_mod, b_mod):
    n = N_MOD * D_MODEL
    return pl.pallas_call(
        _mod_kernel,
        grid=(DEPTH, n // MOD_TN),
        in_specs=[pl.BlockSpec((MOD_ROWS, D_MODEL), lambda l, j: (0, 0)),
                  pl.BlockSpec((None, D_MODEL, MOD_TN), lambda l, j: (l, 0, j)),
                  pl.BlockSpec((None, 1, MOD_TN), lambda l, j: (l, 0, j))],
        out_specs=pl.BlockSpec((None, MOD_ROWS, MOD_TN), lambda l, j: (l, 0, j)),
        out_shape=jax.ShapeDtypeStruct((DEPTH, MOD_ROWS, n), F32),
        compiler_params=_params(2),
        name="modulation",
    )(cvec, w_mod, b_mod.reshape(DEPTH, 1, n))


def _norm_mod(x, g_ref, shift_ref, scale_ref, row):
    y = x * lax.rsqrt(jnp.mean(x * x, axis=-1, keepdims=True) + EPS) * g_ref[...]
    return y * (1.0 + scale_ref[pl.ds(row, 1), :]) + shift_ref[pl.ds(row, 1), :]


def _inproj_kernel(x_ref, g_ref, shift_ref, scale_ref, w_ref, *rest, widths):
    o_refs, w_bf = rest[:-1], rest[-1]
    step = pl.program_id(0)

    @pl.when(step == 0)
    def _():
        w_bf[...] = w_ref[...].astype(BF16)

    row = _mod_row(step)
    hb = _norm_mod(x_ref[...], g_ref, shift_ref, scale_ref, row).astype(BF16)
    off = 0
    for o_ref, width in zip(o_refs, widths):
        o_ref[...] = _dot(hb, w_bf[:, off:off + width])
        off += width


def _inproj(x, g, shift, scale, w_all, index, widths):
    n_in = w_all.shape[2]
    row_spec = lambda width: pl.BlockSpec((RB, width), lambda i: (i, 0))
    full = lambda shape: pl.BlockSpec(shape, lambda i: (0,) * len(shape))
    return pl.pallas_call(
        functools.partial(_inproj_kernel, widths=widths),
        grid=(NBLK,),
        in_specs=[row_spec(D_MODEL), full((1, D_MODEL)), full((MOD_ROWS, D_MODEL)),
                  full((MOD_ROWS, D_MODEL)), pl.BlockSpec((None, D_MODEL, n_in), lambda i: (index, 0, 0))],
        out_specs=[row_spec(width) for width in widths],
        out_shape=[jax.ShapeDtypeStruct((NT, width), F32) for width in widths],
        scratch_shapes=[pltpu.VMEM((D_MODEL, n_in), BF16)],
        compiler_params=_params(),
        name="inproj",
    )(x, g.reshape(1, D_MODEL), shift, scale, w_all)


RC = RET_CHUNK
RET_PAIR = 2 * RET_DK


def _rope(x, cos, sin_signed):
    lane = lax.broadcasted_iota(jnp.int32, x.shape, 1)
    swapped = jnp.where(lane % 2 == 0, pltpu.roll(x, x.shape[1] - 1, 1), pltpu.roll(x, 1, 1))
    return x * cos + swapped * sin_signed


def _ret_kernel(decay_ref, q_ref, k_ref, v_ref, cos_ref, sin_ref, s0_ref, *rest, reverse):
    if reverse:
        of_ref, g_ref, gn_ref, o_ref, sfin_ref, st_ref, dm_ref, dq_ref, dk_ref, ds_ref = rest
    else:
        o_ref, sfin_ref, st_ref, dm_ref, dq_ref, dk_ref, ds_ref = rest
    step = pl.program_id(0)
    blk = NBLK - 1 - step if reverse else step
    direction = 1 if reverse else 0

    @pl.when(step == 0)
    def _():
        row = lax.broadcasted_iota(jnp.int32, (RC, RC), 0).astype(F32)
        col = lax.broadcasted_iota(jnp.int32, (RC, RC), 1).astype(F32)
        for h in range(RET_HEADS):
            lg = -jnp.exp(jnp.full((RC, RC), decay_ref[direction, h], F32))
            if reverse:
                diff = col - row
                mask = diff > 0
                q_pow = RC - row
                k_pow = row
            else:
                diff = row - col
                mask = diff >= 0
                q_pow = row + 1.0
                k_pow = RC - 1.0 - row
            dm_ref[h] = jnp.where(mask, jnp.exp(lg * jnp.where(mask, diff, 0.0)), 0.0)
            dq_ref[h] = jnp.exp(lg * q_pow)
            dk_ref[h] = jnp.exp(lg * k_pow)
            ds_ref[h] = jnp.exp(lg * RC)

    starts = _is_last_block(blk) if reverse else _is_first_block(blk)

    @pl.when(starts)
    def _():
        st_ref[...] = jnp.zeros_like(st_ref)
        for h in range(RET_HEADS):
            off = (h % 2) * RET_DK
            st_ref[h, off:off + RET_DK, :] = s0_ref[h]

    lane = lax.broadcasted_iota(jnp.int32, (1, RET_PAIR), 1)
    chunks = range(RB // RC)
    chunk_order = list(reversed(chunks) if reverse else chunks)
    units = [(h, c) for h in range(RET_HEADS) for c in chunk_order]
    rows_of = lambda c: slice(c * RC, (c + 1) * RC)
    vcols = lambda h: slice(h * RET_DV, (h + 1) * RET_DV)
    roped = {}
    for p in range(RET_HEADS // 2):
        cols = slice(p * RET_PAIR, (p + 1) * RET_PAIR)
        for c in chunk_order:
            rows = rows_of(c)
            cos, sin = cos_ref[rows, :], sin_ref[rows, :]
            roped[p, c] = (_rope(q_ref[rows, cols], cos, sin),
                           _rope(k_ref[rows, cols] * (RET_DK ** -0.5), cos, sin))
    qm_u, vh_u, att_u, kd_u = {}, {}, {}, {}
    for h, c in units:
        head_mask = (lane // RET_DK == h % 2).astype(F32)
        q2, k2 = roped[h // 2, c]
        vh_u[h, c] = v_ref[rows_of(c), vcols(h)].astype(BF16)
        qm_u[h, c] = (q2 * head_mask).astype(BF16)
        km = k2 * head_mask
        att_u[h, c] = (_dot_nt(qm_u[h, c], km.astype(BF16)) * dm_ref[h]).astype(BF16)
        kd_u[h, c] = (km * dk_ref[h]).astype(BF16)
    intra_u = {u: _dot(att_u[u], vh_u[u]) for u in units}
    delta_u = {u: _dot_tn(kd_u[u], vh_u[u]) for u in units}
    state_u = {}
    for h in range(RET_HEADS):
        st = st_ref[h]
        for c in chunk_order:
            state_u[h, c] = st.astype(BF16)
            st = st * ds_ref[h] + delta_u[h, c]
        st_ref[h] = st
    for h, c in units:
        rows, out_cols = rows_of(c), vcols(h)
        o = intra_u[h, c] + _dot(qm_u[h, c], state_u[h, c]) * dq_ref[h]
        if reverse:
            o = o + of_ref[rows, out_cols]
            o = o * lax.rsqrt(jnp.mean(o * o, axis=-1, keepdims=True) + EPS)
            o = o * gn_ref[:, out_cols] * _silu(g_ref[rows, out_cols])
            o_ref[rows, out_cols] = o.astype(o_ref.dtype)
        else:
            o_ref[rows, out_cols] = o

    ends = _is_first_block(blk) if reverse else _is_last_block(blk)

    @pl.when(ends)
    def _():
        for h in range(RET_HEADS):
            off = (h % 2) * RET_DK
            sfin_ref[h] = st_ref[h, off:off + RET_DK, :]


def _retention(decay, q, k, v, cos_tab, sin_tab, s0, *, reverse, o_fwd=None, g=None, gn=None):
    bmap = (lambda j: NBLK - 1 - j) if reverse else (lambda j: j)
    direction = 1 if reverse else 0
    qd, vd = RET_HEADS * RET_DK, RET_HEADS * RET_DV
    row_spec = lambda width: pl.BlockSpec((RB, width), lambda j: (bmap(j), 0))
    state_spec = pl.BlockSpec((None, None, RET_HEADS, RET_DK, RET_DV),
                              lambda j: (_seq_of_block(bmap(j)), direction, 0, 0, 0))
    in_specs = [pl.BlockSpec(memory_space=pltpu.SMEM), row_spec(qd), row_spec(qd), row_spec(vd),
                pl.BlockSpec((RB, RET_PAIR), lambda j: (_rope_block(bmap(j)), 0)),
                pl.BlockSpec((RB, RET_PAIR), lambda j: (_rope_block(bmap(j)), 0)),
                state_spec]
    args = [decay, q, k, v, cos_tab, sin_tab, s0]
    if reverse:
        in_specs += [row_spec(vd), row_spec(vd), pl.BlockSpec((1, vd), lambda j: (0, 0))]
        args += [o_fwd, g, gn.reshape(1, vd)]
    tile = pltpu.VMEM((RET_HEADS, RC, RC), F32)
    return pl.pallas_call(
        functools.partial(_ret_kernel, reverse=reverse),
        grid=(NBLK,),
        in_specs=in_specs,
        out_specs=[row_spec(vd),
                   pl.BlockSpec((None, RET_HEADS, RET_DK, RET_DV),
                                lambda j: (_seq_of_block(bmap(j)), 0, 0, 0))],
        out_shape=[jax.ShapeDtypeStruct((NT, vd), BF16 if reverse else F32),
                   jax.ShapeDtypeStruct((NSEQ, RET_HEADS, RET_DK, RET_DV), F32)],
        scratch_shapes=[tile, tile, tile, tile, tile],
        compiler_params=_params(),
        name="retention_bwd" if reverse else "retention_fwd",
    )(*args)


CONV_RT = 32
CONV_CT = 128
CONV_SPAN = RB + 2 * HALO - 8


def _conv_kernel(a_ref, ga_ref, ap_ref, gap_ref, an_ref, gan_ref, cw_ref, cb_ref, lng_ref, lnb_ref,
                 o_ref, u_ref, y_ref, us_ref):
    blk = pl.program_id(0)
    keep_prev = jnp.where(_is_first_block(blk), 0.0, 1.0)
    keep_next = jnp.where(_is_last_block(blk), 0.0, 1.0)
    u_ref[0:HALO, :] = ap_ref[...] * _sigmoid(gap_ref[...]) * keep_prev
    u_ref[HALO:HALO + RB, :] = a_ref[...] * _sigmoid(ga_ref[...])
    u_ref[HALO + RB:HALO + RB + HALO, :] = an_ref[...] * _sigmoid(gan_ref[...]) * keep_next
    for r in range(1, 8):
        us_ref[r - 1] = u_ref[r:r + CONV_SPAN, :]
    for ct in range(CONV_CH // CONV_CT):
        cols = slice(ct * CONV_CT, (ct + 1) * CONV_CT)
        for rt in range(RB // CONV_RT):
            acc = jnp.zeros((CONV_RT, CONV_CT), F32)
            for w in range(CONV_WIDTH):
                tiles, r = divmod(HALO - CONV_PAD + w, 8)
                base = rt * CONV_RT + 8 * tiles
                src = u_ref if r == 0 else us_ref.at[r - 1]
                acc = acc + src[base:base + CONV_RT, cols] * cw_ref[w:w + 1, cols]
            y_ref[rt * CONV_RT:(rt + 1) * CONV_RT, cols] = acc + cb_ref[:, cols]
    y = y_ref[...]
    mu = jnp.mean(y, axis=-1, keepdims=True)
    var = jnp.mean(jnp.square(y - mu), axis=-1, keepdims=True)
    o_ref[...] = _silu((y - mu) * lax.rsqrt(var + EPS) * lng_ref[...] + lnb_ref[...]).astype(o_ref.dtype)


def _conv_module(a, ga, cw, cb, lng, lnb):
    per_blk = RB // HALO
    n_halo = NT // HALO
    row_spec = pl.BlockSpec((RB, CONV_CH), lambda i: (i, 0))
    prev_spec = pl.BlockSpec((HALO, CONV_CH), lambda i: (jnp.maximum(i * per_blk - 1, 0), 0))
    next_spec = pl.BlockSpec((HALO, CONV_CH), lambda i: (jnp.minimum((i + 1) * per_blk, n_halo - 1), 0))
    vec = pl.BlockSpec((1, CONV_CH), lambda i: (0, 0))
    return pl.pallas_call(
        _conv_kernel,
        grid=(NBLK,),
        in_specs=[row_spec, row_spec, prev_spec, prev_spec, next_spec, next_spec,
                  pl.BlockSpec((CONV_WIDTH, CONV_CH), lambda i: (0, 0)), vec, vec, vec],
        out_specs=row_spec,
        out_shape=jax.ShapeDtypeStruct((NT, CONV_CH), BF16),
        scratch_shapes=[pltpu.VMEM((RB + 2 * HALO, CONV_CH), F32), pltpu.VMEM((RB, CONV_CH), F32),
                        pltpu.VMEM((7, CONV_SPAN, CONV_CH), F32)],
        compiler_params=_params(),
        name="conv_module",
    )(a, ga, a, ga, a, ga, cw, cb.reshape(1, -1), lng.reshape(1, -1), lnb.reshape(1, -1))


GC = GLA_CHUNK
GLA_NSUB = GC // GLA_SUB


def _split3(a):
    p1 = a.astype(BF16)
    r1 = a - p1.astype(F32)
    p2 = r1.astype(BF16)
    p3 = (r1 - p2.astype(F32)).astype(BF16)
    return p1, p2, p3


def _gla_kernel(alr_ref, alr_next_ref, q_ref, k_ref, v_ref, wa_ref, ba_ref, s0_ref, *rest, reverse):
    if reverse:
        of_ref, r_ref, gn_ref, o_ref, sfin_ref, st_ref, b2_ref, tri_ref, ob_ref = rest
    else:
        o_ref, sfin_ref, st_ref, b2_ref, tri_ref = rest
        ob_ref = o_ref
    step = pl.program_id(0)
    blk = NBLK - 1 - step if reverse else step
    starts = _is_last_block(blk) if reverse else _is_first_block(blk)

    @pl.when(starts)
    def _():
        st_ref[...] = s0_ref[...]

    def cumulative_gates(alr):
        z = _dot_hi(alr, wa_ref[...]) + ba_ref[...]
        log_a = (jnp.minimum(z, 0.0) - jnp.log(1.0 + jnp.exp(-jnp.abs(z)))) * (1.0 / GLA_TAU)
        g1, g2, g3 = _split3(log_a)
        tri = tri_ref[...]
        return _dot(tri, g1) + (_dot(tri, g2) + _dot(tri, g3))

    slot = step % 2

    @pl.when(step == 0)
    def _():
        row = lax.broadcasted_iota(jnp.int32, (RB, RB), 0)
        col = lax.broadcasted_iota(jnp.int32, (RB, RB), 1)
        ordered = col >= row if reverse else col <= row
        tri_ref[...] = jnp.where(jnp.logical_and(row // GC == col // GC, ordered), 1.0, 0.0).astype(BF16)
        b2_ref[0] = cumulative_gates(alr_ref[...])

    b2_ref[1 - slot] = cumulative_gates(alr_next_ref[...])
    b_ref = b2_ref.at[slot]

    c_row = lax.broadcasted_iota(jnp.int32, (GC, 1), 0)
    a_row = lax.broadcasted_iota(jnp.int32, (GC, GC), 0)
    a_col = lax.broadcasted_iota(jnp.int32, (GC, GC), 1)
    att_mask = a_col > a_row if reverse else a_col <= a_row
    chunks = range(RB // GC)
    chunk_order = list(reversed(chunks) if reverse else chunks)
    units = [(h, c) for h in range(GLA_HEADS) for c in chunk_order]
    kcols = lambda h: slice(h * GLA_DK, (h + 1) * GLA_DK)
    vcols = lambda h: slice(h * GLA_DV, (h + 1) * GLA_DV)
    rows_of = lambda c: slice(c * GC, (c + 1) * GC)

    vh_u, qe_u, ke_u, decay_u, att_u = {}, {}, {}, {}, {}
    for h, c in units:
        rows = rows_of(c)
        b = b_ref[rows, kcols(h)]
        qh = q_ref[rows, kcols(h)] * (GLA_DK ** -0.5)
        kh = k_ref[rows, kcols(h)]
        vh_u[h, c] = v_ref[rows, vcols(h)].astype(BF16)
        edge = b[0:1, :] if reverse else b[GC - 1:GC, :]
        bounds = []
        for s in range(GLA_NSUB):
            if reverse:
                hi = (s + 1) * GLA_SUB
                bounds.append(b[hi:hi + 1, :] if s < GLA_NSUB - 1 else jnp.zeros((1, GLA_DK), F32))
            else:
                lo = s * GLA_SUB
                bounds.append(b[lo - 1:lo, :] if s > 0 else jnp.zeros((1, GLA_DK), F32))
        own = jnp.concatenate([jnp.broadcast_to(bd, (GLA_SUB, GLA_DK)) for bd in bounds], axis=0)
        q_own = qh * jnp.exp(b - own)
        q_parts, k_parts = [], []
        for s, bd in enumerate(bounds):
            q_parts.append(jnp.where(c_row // GLA_SUB == s, q_own, 0.0))
            reach = c_row >= s * GLA_SUB if reverse else c_row < (s + 1) * GLA_SUB
            k_parts.append(kh * jnp.exp(jnp.where(reach, bd - b, -jnp.inf)))
        q_bd = jnp.concatenate(q_parts, axis=1).astype(BF16)
        k_cat = jnp.concatenate(k_parts, axis=1).astype(BF16)
        att_u[h, c] = jnp.where(att_mask, _dot_nt(q_bd, k_cat), 0.0).astype(BF16)
        qe_u[h, c] = (qh * jnp.exp(b)).astype(BF16)
        ke_u[h, c] = (kh * jnp.exp(edge - b)).astype(BF16)
        decay_u[h, c] = jnp.exp(edge)
    intra_u = {u: _dot(att_u[u], vh_u[u]) for u in units}
    delta_u = {u: _dot_tn(vh_u[u], ke_u[u]) for u in units}
    state_u = {}
    for h in range(GLA_HEADS):
        st = st_ref[h]
        for c in chunk_order:
            state_u[h, c] = st.astype(BF16)
            st = st * decay_u[h, c] + delta_u[h, c]
        st_ref[h] = st
    for h, c in units:
        ob_ref[rows_of(c), vcols(h)] = intra_u[h, c] + _dot_nt(qe_u[h, c], state_u[h, c])

    if reverse:
        for h in range(GLA_HEADS):
            cols = vcols(h)
            o = ob_ref[:, cols] + of_ref[:, cols]
            o = o * lax.rsqrt(jnp.mean(o * o, axis=-1, keepdims=True) + EPS)
            o_ref[:, cols] = (o * gn_ref[:, cols] * _silu(r_ref[:, cols])).astype(o_ref.dtype)

    ends = _is_first_block(blk) if reverse else _is_last_block(blk)

    @pl.when(ends)
    def _():
        sfin_ref[...] = st_ref[...]


def _gla(alr, q, k, v, wa, ba, s0t, *, reverse, o_fwd=None, r=None, gn=None):
    bmap = (lambda j: NBLK - 1 - j) if reverse else (lambda j: j)
    direction = 1 if reverse else 0
    qd, vd = GLA_HEADS * GLA_DK, GLA_HEADS * GLA_DV
    row_spec = lambda width: pl.BlockSpec((RB, width), lambda j: (bmap(j), 0))
    state_spec = pl.BlockSpec((None, None, GLA_HEADS, GLA_DV, GLA_DK),
                              lambda j: (_seq_of_block(bmap(j)), direction, 0, 0, 0))
    next_spec = pl.BlockSpec((RB, 2 * GLA_RANK), lambda j: (bmap(jnp.minimum(j + 1, NBLK - 1)), 0))
    in_specs = [row_spec(2 * GLA_RANK), next_spec, row_spec(qd), row_spec(qd), row_spec(vd),
                pl.BlockSpec((2 * GLA_RANK, qd), lambda j: (0, 0)),
                pl.BlockSpec((1, qd), lambda j: (0, 0)), state_spec]
    args = [alr, alr, q, k, v, wa, ba, s0t]
    scratch = [pltpu.VMEM((GLA_HEADS, GLA_DV, GLA_DK), F32), pltpu.VMEM((2, RB, qd), F32),
               pltpu.VMEM((RB, RB), BF16)]
    if reverse:
        in_specs += [row_spec(vd), row_spec(vd), pl.BlockSpec((1, vd), lambda j: (0, 0))]
        args += [o_fwd, r, gn.reshape(1, vd)]
        scratch += [pltpu.VMEM((RB, vd), F32)]
    return pl.pallas_call(
        functools.partial(_gla_kernel, reverse=reverse),
        grid=(NBLK,),
        in_specs=in_specs,
        out_specs=[row_spec(vd),
                   pl.BlockSpec((None, GLA_HEADS, GLA_DV, GLA_DK),
                                lambda j: (_seq_of_block(bmap(j)), 0, 0, 0))],
        out_shape=[jax.ShapeDtypeStruct((NT, vd), BF16 if reverse else F32),
                   jax.ShapeDtypeStruct((NSEQ, GLA_HEADS, GLA_DV, GLA_DK), F32)],
        scratch_shapes=scratch,
        compiler_params=_params(),
        name="gla_bwd" if reverse else "gla_fwd",
    )(*args)


def _outproj_kernel(*refs, n_mix):
    mix_refs = refs[:n_mix]
    (w_ref, x_ref, gate_ref, g2_ref, shift_ref, scale_ref, rw_ref, rb_ref,
     x1_ref, h2_ref, logit_ref, w_bf) = refs[n_mix:]
    step = pl.program_id(0)

    @pl.when(step == 0)
    def _():
        w_bf[...] = w_ref[...].astype(BF16)

    row = _mod_row(step)
    m, off = None, 0
    for mix_ref in mix_refs:
        width = mix_ref.shape[1]
        part = _dot(mix_ref[...], w_bf[off:off + width, :])
        m = part if m is None else m + part
        off += width
    x1 = x_ref[...] + gate_ref[pl.ds(row, 1), :] * m
    x1_ref[...] = x1
    h2 = _norm_mod(x1, g2_ref, shift_ref, scale_ref, row)
    h2_ref[...] = _pack_rows(h2)
    h_hi, h_lo = _split2(h2)
    r_hi, r_lo = _split2(rw_ref[...])
    logit_ref[...] = _dot_nt(r_hi, h_hi) + (_dot_nt(r_hi, h_lo) + _dot_nt(r_lo, h_hi)) + rb_ref[...]


def _outproj(mixes, w_all, index, x, gate, g2, shift, scale, rw, rb):
    n_mix = len(mixes)
    row_spec = lambda width: pl.BlockSpec((RB, width), lambda i: (i, 0))
    full = lambda shape: pl.BlockSpec(shape, lambda i: (0,) * len(shape))
    mod_spec = full((MOD_ROWS, D_MODEL))
    n_mixed = w_all.shape[1]
    return pl.pallas_call(
        functools.partial(_outproj_kernel, n_mix=n_mix),
        grid=(NBLK,),
        in_specs=[row_spec(m.shape[1]) for m in mixes]
        + [pl.BlockSpec((None, n_mixed, D_MODEL), lambda i: (index, 0, 0)),
           row_spec(D_MODEL), mod_spec, full((1, D_MODEL)), mod_spec, mod_spec,
           full((N_EXPERTS, D_MODEL)), full((N_EXPERTS, 1))],
        out_specs=[row_spec(D_MODEL), row_spec(D_MODEL // 2), pl.BlockSpec((N_EXPERTS, RB), lambda i: (0, i))],
        out_shape=[jax.ShapeDtypeStruct((NT, D_MODEL), F32), jax.ShapeDtypeStruct((NT, D_MODEL // 2), jnp.uint32),
                   jax.ShapeDtypeStruct((N_EXPERTS, NT), F32)],
        scratch_shapes=[pltpu.VMEM((n_mixed, D_MODEL), BF16)],
        compiler_params=_params(),
        name="outproj",
    )(*mixes, w_all, x, gate, g2.reshape(1, D_MODEL), shift, scale, rw.T, rb.reshape(N_EXPERTS, 1))


ROUTE_BLK = 2048
ROUTE_SUB = 256


def _route_kernel(lg_ref, idx_ref, rank_ref, gt_ref, cnt_ref, carry_ref):
    @pl.when(pl.program_id(0) == 0)
    def _():
        carry_ref[...] = jnp.zeros_like(carry_ref)

    logits = lg_ref[...]
    eid = lax.broadcasted_iota(jnp.int32, logits.shape, 0).astype(F32)
    work = logits
    onehots, top_vals = [], []
    for kk in range(TOP_K):
        top = jnp.max(work, axis=0, keepdims=True)
        first = jnp.min(jnp.where(work == top, eid, float(N_EXPERTS)), axis=0, keepdims=True)
        onehot = eid == first
        idx_ref[kk:kk + 1, :] = first.astype(jnp.int32)
        onehots.append(onehot)
        top_vals.append(top)
        work = jnp.where(onehot, -jnp.inf, work)
    exps = [jnp.exp(v - top_vals[0]) for v in top_vals]
    denom = exps[0]
    for e in exps[1:]:
        denom = denom + e
    gt_ref[...] = jnp.zeros_like(gt_ref)
    for kk in range(TOP_K):
        gt_ref[kk:kk + 1, :] = exps[kk] / denom

    sel = jnp.zeros(logits.shape, F32)
    for onehot in onehots:
        sel = sel + jnp.where(onehot, 1.0, 0.0)
    sel = sel.astype(BF16)
    r_i = lax.broadcasted_iota(jnp.int32, (ROUTE_SUB, ROUTE_SUB), 0)
    c_i = lax.broadcasted_iota(jnp.int32, (ROUTE_SUB, ROUTE_SUB), 1)
    before = jnp.where(r_i < c_i, 1.0, 0.0).astype(BF16)
    ones = jnp.ones((ROUTE_SUB, ROUTE_SUB), BF16)
    carry = carry_ref[...]
    for s in range(ROUTE_BLK // ROUTE_SUB):
        cols = slice(s * ROUTE_SUB, (s + 1) * ROUTE_SUB)
        pos = _dot(sel[:, cols], before) + carry
        for kk in range(TOP_K):
            rank = jnp.sum(jnp.where(onehots[kk][:, cols], pos, 0.0), axis=0, keepdims=True)
            rank_ref[kk:kk + 1, cols] = rank.astype(jnp.int32)
        carry = carry + _dot(sel[:, cols], ones)
    carry_ref[...] = carry
    cnt_ref[...] = carry.astype(jnp.int32)


def _route(logits_t):
    col_spec = lambda rows: pl.BlockSpec((rows, ROUTE_BLK), lambda i: (0, i))
    return pl.pallas_call(
        _route_kernel,
        grid=(NT // ROUTE_BLK,),
        in_specs=[col_spec(N_EXPERTS)],
        out_specs=[col_spec(TOP_K), col_spec(TOP_K), col_spec(8),
                   pl.BlockSpec((N_EXPERTS, ROUTE_SUB), lambda i: (0, 0))],
        out_shape=[jax.ShapeDtypeStruct((TOP_K, NT), jnp.int32), jax.ShapeDtypeStruct((TOP_K, NT), jnp.int32),
                   jax.ShapeDtypeStruct((8, NT), F32), jax.ShapeDtypeStruct((N_EXPERTS, ROUTE_SUB), jnp.int32)],
        scratch_shapes=[pltpu.VMEM((N_EXPERTS, ROUTE_SUB), F32)],
        compiler_params=_params(),
        name="route",
    )(logits_t)


TM = 512
TM_SUB = 256
MOE_NBLK = NT * TOP_K // TM + N_EXPERTS
MOE_ROWS = MOE_NBLK * TM
HALF = D_MODEL // 2


def _moe_kernel(be_ref, nv_ref, nx_ref, x_ref, wgu_hbm, bgu_ref, wd_hbm, bd_ref, y_ref,
                wgu_st, wd_st, wgu_bf, wd_bf, sems, *, layer):
    i = pl.program_id(0)
    n_valid = nv_ref[i]

    def fetch(e):
        return (pltpu.make_async_copy(wgu_hbm.at[layer, e], wgu_st, sems.at[0]),
                pltpu.make_async_copy(wd_hbm.at[layer, e], wd_st, sems.at[1]))

    @pl.when(i == 0)
    def _():
        for cp in fetch(be_ref[0]):
            cp.start()

    @pl.when(n_valid > 0)
    def _():
        e = be_ref[i]
        changed = jnp.logical_or(i == 0, e != be_ref[jnp.maximum(i - 1, 0)])

        @pl.when(changed)
        def _():
            for cp in fetch(e):
                cp.wait()
            wgu_bf[...] = wgu_st[...].astype(BF16)
            wd_bf[...] = wd_st[...].astype(BF16)
            nxt = nx_ref[e]

            @pl.when(nxt >= 0)
            def _():
                for cp in fetch(nxt):
                    cp.start()

    for p in range(TM // TM_SUB):
        @pl.when(n_valid > p * TM_SUB)
        def _():
            rows = slice(p * TM_SUB, (p + 1) * TM_SUB)
            row_id = lax.broadcasted_iota(jnp.int32, (TM_SUB, 1), 0) + p * TM_SUB
            x_lo, x_hi = _unpack_rows(jnp.where(row_id < n_valid, x_ref[rows, :], jnp.uint32(0)))
            gu = (_dot(x_lo.astype(BF16), wgu_bf[:HALF, :]) + _dot(x_hi.astype(BF16), wgu_bf[HALF:, :])
                  + bgu_ref[...])
            gate = jnp.minimum(gu[:, :D_FF], SWIGLU_LIMIT)
            up = jnp.clip(gu[:, D_FF:], -SWIGLU_LIMIT, SWIGLU_LIMIT)
            hdn = gate * _sigmoid(SWIGLU_ALPHA * gate) * (up + 1.0)
            y_ref[rows, :] = _pack_rows(_dot(hdn.astype(BF16), wd_bf[...]) + bd_ref[...])


def _moe_experts(layer, block_e, n_valid, next_e, xs, w_gu, b_gu, w_down, b_down):
    grid_spec = pltpu.PrefetchScalarGridSpec(
        num_scalar_prefetch=3,
        grid=(MOE_NBLK,),
        in_specs=[pl.BlockSpec((TM, HALF), lambda i, be, nv, nx: (i, 0)),
                  pl.BlockSpec(memory_space=pl.ANY),
                  pl.BlockSpec((None, None, 1, 2 * D_FF), lambda i, be, nv, nx: (layer, be[i], 0, 0)),
                  pl.BlockSpec(memory_space=pl.ANY),
                  pl.BlockSpec((None, None, 1, D_MODEL), lambda i, be, nv, nx: (layer, be[i], 0, 0))],
        out_specs=pl.BlockSpec((TM, HALF), lambda i, be, nv, nx: (i, 0)),
        scratch_shapes=[pltpu.VMEM((D_MODEL, 2 * D_FF), F32), pltpu.VMEM((D_FF, D_MODEL), F32),
                        pltpu.VMEM((D_MODEL, 2 * D_FF), BF16), pltpu.VMEM((D_FF, D_MODEL), BF16),
                        pltpu.SemaphoreType.DMA((2,))],
    )
    return pl.pallas_call(
        functools.partial(_moe_kernel, layer=layer),
        grid_spec=grid_spec,
        out_shape=jax.ShapeDtypeStruct((MOE_ROWS, HALF), jnp.uint32),
        compiler_params=_params(),
        name="moe_experts",
    )(block_e, n_valid, next_e, xs, w_gu, b_gu.reshape(DEPTH, N_EXPERTS, 1, -1), w_down,
      b_down.reshape(DEPTH, N_EXPERTS, 1, -1))


SC_WORKERS = 32
SC_WIN = 64


def _sc_mesh():
    return plsc.VectorSubcoreMesh(core_axis_name="core", subcore_axis_name="subcore")


def _sc_worker():
    return lax.axis_index("core") * (SC_WORKERS // 2) + lax.axis_index("subcore")


def _sc_scatter_rows(x, dest_t, n_rows):
    n, width = x.shape
    kk = dest_t.shape[0]
    per = n // SC_WORKERS
    n_win = per // SC_WIN
    assert per * SC_WORKERS == n and n_win * SC_WIN == per and n_win % 2 == 0

    @pl.kernel(out_type=jax.ShapeDtypeStruct((n_rows, width), x.dtype), mesh=_sc_mesh(),
               scratch_types=[pltpu.VMEM((kk, per), jnp.int32), pltpu.VMEM((SC_WIN, width), x.dtype),
                              pltpu.VMEM((SC_WIN, width), x.dtype), pltpu.SemaphoreType.DMA((4,))])
    def scatter(x_hbm, i_hbm, o_hbm, idx_v, buf0, buf1, sems):
        base = _sc_worker() * per
        pltpu.sync_copy(i_hbm.at[:, pl.ds(base, per)], idx_v)

        def get(j, buf, s):
            return pltpu.make_async_copy(x_hbm.at[pl.ds(base + j * SC_WIN, SC_WIN)], buf, sems.at[s])

        def put(j, q, buf, s):
            return pltpu.make_async_copy(buf, o_hbm.at[idx_v.at[q, pl.ds(j * SC_WIN, SC_WIN)]], sems.at[s])

        get(0, buf0, 0).start()

        @pl.loop(0, n_win, step=2)
        def _(j):
            get(j, buf0, 0).wait()

            @pl.when(j > 0)
            def _():
                for q in range(kk):
                    put(j - 1, q, buf1, 3).wait()

            get(j + 1, buf1, 1).start()
            for q in range(kk):
                put(j, q, buf0, 2).start()
            get(j + 1, buf1, 1).wait()
            for q in range(kk):
                put(j, q, buf0, 2).wait()

            @pl.when(j + 2 < n_win)
            def _():
                get(j + 2, buf0, 0).start()

            for q in range(kk):
                put(j + 1, q, buf1, 3).start()

        for q in range(kk):
            put(n_win - 1, q, buf1, 3).wait()

    return scatter(x, dest_t)


def _sc_gather_rows(y, idx):
    n = idx.shape[0]
    width = y.shape[1]
    per = n // SC_WORKERS
    n_win = per // SC_WIN
    assert per * SC_WORKERS == n and n_win * SC_WIN == per and n_win % 2 == 0

    @pl.kernel(out_type=jax.ShapeDtypeStruct((n, width), y.dtype), mesh=_sc_mesh(),
               scratch_types=[pltpu.VMEM((per,), jnp.int32), pltpu.VMEM((SC_WIN, width), y.dtype),
                              pltpu.VMEM((SC_WIN, width), y.dtype), pltpu.SemaphoreType.DMA((4,))])
    def gather(y_hbm, i_hbm, o_hbm, idx_v, buf0, buf1, sems):
        base = _sc_worker() * per
        pltpu.sync_copy(i_hbm.at[pl.ds(base, per)], idx_v)

        def get(j, buf, s):
            return pltpu.make_async_copy(y_hbm.at[idx_v.at[pl.ds(j * SC_WIN, SC_WIN)]], buf, sems.at[s])

        def put(j, buf, s):
            return pltpu.make_async_copy(buf, o_hbm.at[pl.ds(base + j * SC_WIN, SC_WIN)], sems.at[s])

        get(0, buf0, 0).start()

        @pl.loop(0, n_win, step=2)
        def _(j):
            get(j, buf0, 0).wait()

            @pl.when(j > 0)
            def _():
                put(j - 1, buf1, 3).wait()

            get(j + 1, buf1, 1).start()
            put(j, buf0, 2).start()
            get(j + 1, buf1, 1).wait()
            put(j, buf0, 2).wait()

            @pl.when(j + 2 < n_win)
            def _():
                get(j + 2, buf0, 0).start()

            put(j + 1, buf1, 3).start()

        put(n_win - 1, buf1, 3).wait()

    return gather(y, idx)


def _combine_kernel(x1_ref, *rest, final):
    yg_refs = rest[:TOP_K]
    if final:
        gt_ref, gate_ref, fg_ref, op_ref, os_ref = rest[TOP_K:]
    else:
        gt_ref, gate_ref, o_ref = rest[TOP_K:]
    i = pl.program_id(0)
    row = _mod_row(i)
    r_i = lax.broadcasted_iota(jnp.int32, (RB, RB), 0)
    c_i = lax.broadcasted_iota(jnp.int32, (RB, RB), 1)
    eye = jnp.where(r_i == c_i, 1.0, 0.0).astype(BF16)
    g1, g2, g3 = _split3(gt_ref[...])
    gw = _dot_nt(eye, g1) + (_dot_nt(eye, g2) + _dot_nt(eye, g3))
    acc_lo, acc_hi = None, None
    for kk in range(TOP_K):
        y_lo, y_hi = _unpack_rows(yg_refs[kk][...])
        w = gw[:, kk:kk + 1]
        acc_lo = y_lo * w if acc_lo is None else acc_lo + y_lo * w
        acc_hi = y_hi * w if acc_hi is None else acc_hi + y_hi * w
    x_lo = x1_ref[:, :HALF] + gate_ref[pl.ds(row, 1), :HALF] * acc_lo
    x_hi = x1_ref[:, HALF:] + gate_ref[pl.ds(row, 1), HALF:] * acc_hi
    if not final:
        o_ref[:, :HALF] = x_lo
        o_ref[:, HALF:] = x_hi
        return
    ms = (jnp.sum(x_lo * x_lo, axis=-1, keepdims=True) + jnp.sum(x_hi * x_hi, axis=-1, keepdims=True)) / D_MODEL
    scale = lax.rsqrt(ms + EPS)

    @pl.when(i < N_PROMPT_BLK)
    def _():
        op_ref[:, :HALF] = x_lo * scale * fg_ref[:, :HALF]
        op_ref[:, HALF:] = x_hi * scale * fg_ref[:, HALF:]

    @pl.when(i >= N_PROMPT_BLK)
    def _():
        os_ref[:, :HALF] = x_lo * scale * fg_ref[:, :HALF]
        os_ref[:, HALF:] = x_hi * scale * fg_ref[:, HALF:]


def _combine(x1, yg, gates_t, gate, final_g=None):
    final = final_g is not None
    row_spec = lambda width: pl.BlockSpec((RB, width), lambda i: (i, 0))
    slot_spec = lambda k: pl.BlockSpec((RB, HALF), lambda i: (k * NBLK + i, 0))
    in_specs = ([row_spec(D_MODEL)] + [slot_spec(k) for k in range(TOP_K)]
                + [pl.BlockSpec((8, RB), lambda i: (0, i)), pl.BlockSpec((MOD_ROWS, D_MODEL), lambda i: (0, 0))])
    args = [x1] + [yg] * TOP_K + [gates_t, gate]
    if final:
        in_specs.append(pl.BlockSpec((1, D_MODEL), lambda i: (0, 0)))
        args.append(final_g.reshape(1, D_MODEL))
        out_specs = [pl.BlockSpec((RB, D_MODEL), lambda i: (jnp.minimum(i, N_PROMPT_BLK - 1), 0)),
                     pl.BlockSpec((RB, D_MODEL), lambda i: (jnp.maximum(i - N_PROMPT_BLK, 0), 0))]
        out_shape = [jax.ShapeDtypeStruct((NT_PROMPT, D_MODEL), F32),
                     jax.ShapeDtypeStruct((NT - NT_PROMPT, D_MODEL), F32)]
    else:
        out_specs = row_spec(D_MODEL)
        out_shape = jax.ShapeDtypeStruct((NT, D_MODEL), F32)
    return pl.pallas_call(
        functools.partial(_combine_kernel, final=final),
        grid=(NBLK,),
        in_specs=in_specs,
        out_specs=out_specs,
        out_shape=out_shape,
        compiler_params=_params(),
        name="moe_combine_final" if final else "moe_combine",
    )(*args)


def _routing_plan(counts, idx_t, rank_t):
    counts = counts[:, 0]
    padded = (counts + TM - 1) // TM * TM
    pad_end = jnp.cumsum(padded)
    pad_start = pad_end - padded
    blk_row = (jnp.arange(MOE_NBLK, dtype=jnp.int32) * TM)[:, None]
    ids = jnp.arange(N_EXPERTS, dtype=jnp.int32)
    owns = jnp.logical_and(pad_start[None, :] <= blk_row, blk_row < pad_end[None, :])
    last_used = jnp.max(jnp.where(counts > 0, ids, 0))
    block_e = jnp.where(jnp.any(owns, axis=1), jnp.sum(jnp.where(owns, ids[None, :], 0), axis=1), last_used)
    block_e = block_e.astype(jnp.int32)
    left = jnp.clip(counts[None, :] - (blk_row - pad_start[None, :]), 0, TM)
    n_valid = jnp.sum(jnp.where(owns, left, 0), axis=1).astype(jnp.int32)
    start = jnp.zeros(idx_t.shape, jnp.int32)
    for e in range(N_EXPERTS):
        start = jnp.where(idx_t == e, pad_start[e], start)
    dest_t = (start + rank_t).astype(jnp.int32)
    later = jnp.where(jnp.logical_and(counts[None, :] > 0, ids[None, :] > ids[:, None]), ids[None, :], N_EXPERTS)
    next_e = jnp.min(later, axis=1)
    next_e = jnp.where(next_e == N_EXPERTS, -1, next_e).astype(jnp.int32)
    return block_e, n_valid, next_e, dest_t


def _rope_tables():
    rows = DEC_SEQ // GRID_W
    row = jnp.repeat(jnp.arange(rows, dtype=F32), GRID_W)
    col = jnp.tile(jnp.arange(GRID_W, dtype=F32), rows)
    n_f = RET_DK // 4
    freqs = ROPE_THETA ** (-jnp.arange(n_f, dtype=F32) / n_f)
    ang = jnp.concatenate([row[:, None] * freqs, col[:, None] * freqs], axis=-1)
    cos = jnp.repeat(jnp.cos(ang), 2, axis=-1)
    sin = jnp.repeat(jnp.sin(ang), 2, axis=-1) * jnp.tile(jnp.asarray([-1.0, 1.0], F32), RET_DK // 2)
    cos = jnp.concatenate([jnp.ones((RB, RET_DK), F32), cos], axis=0)
    sin = jnp.concatenate([jnp.zeros((RB, RET_DK), F32), sin], axis=0)
    return jnp.tile(cos, (1, 2)), jnp.tile(sin, (1, 2))


def kernel(x_prompt, x_sample, state_ret, state_gla, c, c_ctx, w_mod, b_mod, norm1_g, norm2_g, final_g, even_w_in, ret_decay, ret_gn, conv_w, conv_b, conv_ln_g, conv_ln_b, even_w_out, odd_w_in, gla_w_a2, gla_b_a2, gla_gn, odd_w_out, router_w, router_b, exp_w_gu, exp_b_gu, exp_w_down, exp_b_down):
    x = jnp.concatenate([x_prompt.reshape(NT_PROMPT, D_MODEL), x_sample.reshape(-1, D_MODEL)], axis=0)
    cvec = jnp.concatenate([c_ctx[None, :], c, jnp.zeros((MOD_ROWS - 1 - DEC_BATCH, D_MODEL), F32)], axis=0)
    mods = _modulation(cvec, w_mod, b_mod).reshape(DEPTH, MOD_ROWS, N_MOD, D_MODEL)
    cos_tab, sin_tab = _rope_tables()
    new_ret, new_gla = [], []
    for l in range(DEPTH):
        mod = [mods[l, :, j, :] for j in range(N_MOD)]
        if l % 2 == 0:
            e = l // 2
            qd, vd = RET_HEADS * RET_DK, RET_HEADS * RET_DV
            q, k, v, g, a, ga = _inproj(x, norm1_g[l], mod[0], mod[1], even_w_in, e,
                                        (qd, qd, vd, vd, CONV_CH, CONV_CH))
            s0 = jnp.concatenate([jnp.zeros((BATCH,) + state_ret.shape[2:], F32), state_ret[:, e]], axis=0)
            o_f, s_f = _retention(ret_decay[e], q, k, v, cos_tab, sin_tab, s0, reverse=False)
            ret, s_b = _retention(ret_decay[e], q, k, v, cos_tab, sin_tab, s0, reverse=True,
                                  o_fwd=o_f, g=g, gn=ret_gn[e])
            u = _conv_module(a, ga, conv_w[e], conv_b[e], conv_ln_g[e], conv_ln_b[e])
            mixes, w_out, w_index = [ret, u], even_w_out, e
            new_ret.append(jnp.stack([s_f[:BATCH], s_b[:BATCH]], axis=1))
        else:
            o = l // 2
            qd, vd = GLA_HEADS * GLA_DK, GLA_HEADS * GLA_DV
            q, k, v, r, alr = _inproj(x, norm1_g[l], mod[0], mod[1], odd_w_in, o,
                                      (qd, qd, vd, vd, 2 * GLA_RANK))
            s0t = jnp.concatenate([jnp.zeros((BATCH,) + state_gla.shape[2:], F32), state_gla[:, o]], axis=0)
            s0t = jnp.swapaxes(s0t, -1, -2)
            zeros = jnp.zeros((GLA_RANK, qd), F32)
            wa_f = jnp.concatenate([gla_w_a2[o, 0], zeros], axis=0)
            wa_b = jnp.concatenate([zeros, gla_w_a2[o, 1]], axis=0)
            o_f, s_f = _gla(alr, q, k, v, wa_f, gla_b_a2[o, 0].reshape(1, qd), s0t, reverse=False)
            y, s_b = _gla(alr, q, k, v, wa_b, gla_b_a2[o, 1].reshape(1, qd), s0t, reverse=True,
                          o_fwd=o_f, r=r, gn=gla_gn[o])
            mixes, w_out, w_index = [y], odd_w_out, o
            new_gla.append(jnp.swapaxes(jnp.stack([s_f[:BATCH], s_b[:BATCH]], axis=1), -1, -2))
        x1, h2, logits_t = _outproj(mixes, w_out, w_index, x, mod[2], norm2_g[l], mod[3], mod[4],
                                    router_w[l], router_b[l])
        idx_t, rank_t, gates_t, counts = _route(logits_t)
        block_e, n_valid, next_e, dest_t = _routing_plan(counts, idx_t, rank_t)
        xs = _sc_scatter_rows(h2, dest_t, MOE_ROWS)
        yb = _moe_experts(l, block_e, n_valid, next_e, xs, exp_w_gu, exp_b_gu, exp_w_down, exp_b_down)
        yg = _sc_gather_rows(yb, dest_t.reshape(TOP_K * NT))
        if l < DEPTH - 1:
            x = _combine(x1, yg, gates_t, mod[5])
        else:
            y_prompt, y_sample = _combine(x1, yg, gates_t, mod[5], final_g=final_g)
    y_prompt = y_prompt.reshape(BATCH, SEQ, D_MODEL)
    y_sample = y_sample.reshape(DEC_BATCH, DEC_SEQ, D_MODEL)
    return (y_prompt, y_sample, jnp.stack(new_ret, axis=1), jnp.stack(new_gla, axis=1))
```

```python
import functools

import jax
import jax.numpy as jnp
from jax import lax
from jax.experimental import pallas as pl
from jax.experimental.pallas import tpu as pltpu
from jax.experimental.pallas import tpu_sc as plsc

F32 = jnp.float32
BF16 = jnp.bfloat16

D_MODEL = 1024
BATCH = 16
SEQ = 256
DEPTH = 4
DEC_BATCH = 4
DEC_SEQ = 4096
GRID_W = 64
RET_HEADS = 4
RET_DK = 64
RET_DV = 128
RET_CHUNK = 128
CONV_CH = 512
CONV_WIDTH = 31
CONV_PAD = CONV_WIDTH // 2
GLA_HEADS = 4
GLA_DK = 128
GLA_DV = 256
GLA_RANK = 16
GLA_TAU = 16.0
GLA_CHUNK = 64
GLA_SUB = 16
N_EXPERTS = 32
TOP_K = 4
D_FF = 1024
SWIGLU_LIMIT = 7.0
SWIGLU_ALPHA = 1.702
MOE_BLOCK = 128
ROPE_THETA = 10000.0
EPS = 1e-6
N_MOD = 6

RB = 256
NT_PROMPT = BATCH * SEQ
NT = NT_PROMPT + DEC_BATCH * DEC_SEQ
NBLK = NT // RB
N_PROMPT_BLK = NT_PROMPT // RB
SAMPLE_BLK = DEC_SEQ // RB
NSEQ = BATCH + DEC_BATCH
MOD_ROWS = 8
HALO = 16
VMEM_LIMIT = 48 * 1024 * 1024

assert SEQ == RB and DEC_SEQ % RB == 0 and CONV_PAD < HALO


def _seq_of_block(i):
    return jnp.where(i < N_PROMPT_BLK, i, N_PROMPT_BLK + (i - N_PROMPT_BLK) // SAMPLE_BLK)


def _is_first_block(i):
    return jnp.logical_or(i < N_PROMPT_BLK, (i - N_PROMPT_BLK) % SAMPLE_BLK == 0)


def _is_last_block(i):
    return jnp.logical_or(i < N_PROMPT_BLK, (i - N_PROMPT_BLK) % SAMPLE_BLK == SAMPLE_BLK - 1)


def _mod_row(i):
    return jnp.where(i < N_PROMPT_BLK, 0, 1 + (i - N_PROMPT_BLK) // SAMPLE_BLK)


def _rope_block(i):
    return jnp.where(i < N_PROMPT_BLK, 0, 1 + (i - N_PROMPT_BLK) % SAMPLE_BLK)


def _dot(a, b):
    return jnp.dot(a, b, preferred_element_type=F32)


def _dot_nt(a, b):
    return lax.dot_general(a, b, (((1,), (1,)), ((), ())), preferred_element_type=F32)


def _dot_tn(a, b):
    return lax.dot_general(a, b, (((0,), (0,)), ((), ())), preferred_element_type=F32)


def _split2(a):
    hi = a.astype(BF16)
    lo = (a - hi.astype(F32)).astype(BF16)
    return hi, lo


def _dot_hi(a, b):
    a_hi, a_lo = _split2(a)
    b_hi, b_lo = _split2(b)
    return _dot(a_hi, b_hi) + (_dot(a_hi, b_lo) + _dot(a_lo, b_hi))


def _silu(x):
    return x * (1.0 / (1.0 + jnp.exp(-x)))


def _sigmoid(x):
    return 1.0 / (1.0 + jnp.exp(-x))


def _pack_rows(x):
    n = x.shape[1] // 2
    lo = pltpu.bitcast(x[:, :n].astype(BF16).astype(F32), jnp.uint32)
    hi = pltpu.bitcast(x[:, n:].astype(BF16).astype(F32), jnp.uint32)
    return hi | (lo >> 16)


def _unpack_rows(u):
    lo = pltpu.bitcast(u << 16, F32)
    hi = pltpu.bitcast(u & jnp.uint32(0xFFFF0000), F32)
    return lo, hi


def _params(n_axes=1, vmem=VMEM_LIMIT):
    return pltpu.CompilerParams(dimension_semantics=("arbitrary",) * n_axes, vmem_limit_bytes=vmem)


MOD_TN = 1536


def _mod_kernel(c_ref, w_ref, b_ref, o_ref):
    s = _silu(c_ref[...]).astype(BF16)
    o_ref[...] = _dot(s, w_ref[...].astype(BF16)) + b_ref[...]


def _modulation(cvec, w_mod, b_mod):
    n = N_MOD * D_MODEL
    return pl.pallas_call(
        _mod_kernel,
        grid=(DEPTH, n // MOD_TN),
        in_specs=[pl.BlockSpec((MOD_ROWS, D_MODEL), lambda l, j: (0, 0)),
                  pl.BlockSpec((None, D_MODEL, MOD_TN), lambda l, j: (l, 0, j)),
                  pl.BlockSpec((None, 1, MOD_TN), lambda l, j: (l, 0, j))],
        out_specs=pl.BlockSpec((None, MOD_ROWS, MOD_TN), lambda l, j: (l, 0, j)),
        out_shape=jax.ShapeDtypeStruct((DEPTH, MOD_ROWS, n), F32),
        compiler_params=_params(2),
        name="modulation",
    )(cvec, w_mod, b_mod.reshape(DEPTH, 1, n))


def _norm_mod(x, g_ref, shift_ref, scale_ref, row):
    y = x * lax.rsqrt(jnp.mean(x * x, axis=-1, keepdims=True) + EPS) * g_ref[...]
    return y * (1.0 + scale_ref[pl.ds(row, 1), :]) + shift_ref[pl.ds(row, 1), :]


def _inproj_kernel(*refs, widths, split_input):
    if split_input:
        xp_ref, xs_ref, g_ref, shift_ref, scale_ref, w_ref = refs[:6]
        x_out_ref, o_refs, w_bf = refs[6], refs[7:-1], refs[-1]
    else:
        x_ref, g_ref, shift_ref, scale_ref, w_ref = refs[:5]
        o_refs, w_bf = refs[5:-1], refs[-1]
    step = pl.program_id(0)

    @pl.when(step == 0)
    def _():
        w_bf[...] = w_ref[...].astype(BF16)

    row = _mod_row(step)
    if split_input:
        x = jnp.where(step < N_PROMPT_BLK, xp_ref[...], xs_ref[...])
        x_out_ref[...] = x
    else:
        x = x_ref[...]
    hb = _norm_mod(x, g_ref, shift_ref, scale_ref, row).astype(BF16)
    off = 0
    for o_ref, width in zip(o_refs, widths):
        o_ref[...] = _dot(hb, w_bf[:, off:off + width])
        off += width


def _inproj(x, g, shift, scale, w_all, index, widths):
    split_input = isinstance(x, tuple)
    n_in = w_all.shape[2]
    row_spec = lambda width: pl.BlockSpec((RB, width), lambda i: (i, 0))
    full = lambda shape: pl.BlockSpec(shape, lambda i: (0,) * len(shape))
    if split_input:
        x_specs = [pl.BlockSpec((RB, D_MODEL), lambda i: (jnp.minimum(i, N_PROMPT_BLK - 1), 0)),
                   pl.BlockSpec((RB, D_MODEL), lambda i: (jnp.maximum(i - N_PROMPT_BLK, 0), 0))]
        x_args = list(x)
        out_widths = (D_MODEL,) + tuple(widths)
    else:
        x_specs, x_args, out_widths = [row_spec(D_MODEL)], [x], tuple(widths)
    return pl.pallas_call(
        functools.partial(_inproj_kernel, widths=widths, split_input=split_input),
        grid=(NBLK,),
        in_specs=x_specs + [full((1, D_MODEL)), full((MOD_ROWS, D_MODEL)), full((MOD_ROWS, D_MODEL)),
                            pl.BlockSpec((None, D_MODEL, n_in), lambda i: (index, 0, 0))],
        out_specs=[row_spec(width) for width in out_widths],
        out_shape=[jax.ShapeDtypeStruct((NT, width), F32) for width in out_widths],
        scratch_shapes=[pltpu.VMEM((D_MODEL, n_in), BF16)],
        compiler_params=_params(),
        name="inproj",
    )(*x_args, g.reshape(1, D_MODEL), shift, scale, w_all)


RC = RET_CHUNK
RET_PAIR = 2 * RET_DK


def _rope(x, cos, sin_signed):
    lane = lax.broadcasted_iota(jnp.int32, x.shape, 1)
    swapped = jnp.where(lane % 2 == 0, pltpu.roll(x, x.shape[1] - 1, 1), pltpu.roll(x, 1, 1))
    return x * cos + swapped * sin_signed


def _ret_kernel(decay_ref, q_ref, k_ref, v_ref, cos_ref, sin_ref, s0_ref, acc_ref, *rest, reverse):
    del acc_ref
    if reverse:
        of_ref, g_ref, gn_ref, o_ref, sfin_ref, st_ref, dm_ref, dq_ref, dk_ref, ds_ref = rest
    else:
        o_ref, sfin_ref, st_ref, dm_ref, dq_ref, dk_ref, ds_ref = rest
    step = pl.program_id(0)
    blk = NBLK - 1 - step if reverse else step
    direction = 1 if reverse else 0

    @pl.when(step == 0)
    def _():
        row = lax.broadcasted_iota(jnp.int32, (RC, RC), 0).astype(F32)
        col = lax.broadcasted_iota(jnp.int32, (RC, RC), 1).astype(F32)
        for h in range(RET_HEADS):
            lg = -jnp.exp(jnp.full((RC, RC), decay_ref[direction, h], F32))
            if reverse:
                diff = col - row
                mask = diff > 0
                q_pow = RC - row
                k_pow = row
            else:
                diff = row - col
                mask = diff >= 0
                q_pow = row + 1.0
                k_pow = RC - 1.0 - row
            dm_ref[h] = jnp.where(mask, jnp.exp(lg * jnp.where(mask, diff, 0.0)), 0.0)
            dq_ref[h] = jnp.exp(lg * q_pow)
            dk_ref[h] = jnp.exp(lg * k_pow)
            ds_ref[h] = jnp.exp(lg * RC)

    starts = _is_last_block(blk) if reverse else _is_first_block(blk)

    is_prompt = blk < N_PROMPT_BLK

    @pl.when(starts)
    def _():
        st_ref[...] = jnp.zeros_like(st_ref)

    @pl.when(jnp.logical_and(starts, jnp.logical_not(is_prompt)))
    def _():
        for h in range(RET_HEADS):
            off = (h % 2) * RET_DK
            st_ref[h, off:off + RET_DK, :] = s0_ref[h]

    lane = lax.broadcasted_iota(jnp.int32, (1, RET_PAIR), 1)
    chunks = range(RB // RC)
    chunk_order = list(reversed(chunks) if reverse else chunks)
    units = [(h, c) for h in range(RET_HEADS) for c in chunk_order]
    rows_of = lambda c: slice(c * RC, (c + 1) * RC)
    vcols = lambda h: slice(h * RET_DV, (h + 1) * RET_DV)
    roped = {}
    for p in range(RET_HEADS // 2):
        cols = slice(p * RET_PAIR, (p + 1) * RET_PAIR)
        for c in chunk_order:
            rows = rows_of(c)
            cos, sin = cos_ref[rows, :], sin_ref[rows, :]
            roped[p, c] = (_rope(q_ref[rows, cols], cos, sin),
                           _rope(k_ref[rows, cols] * (RET_DK ** -0.5), cos, sin))
    qm_u, vh_u, att_u, kd_u = {}, {}, {}, {}
    for h, c in units:
        head_mask = (lane // RET_DK == h % 2).astype(F32)
        q2, k2 = roped[h // 2, c]
        vh_u[h, c] = v_ref[rows_of(c), vcols(h)].astype(BF16)
        qm_u[h, c] = (q2 * head_mask).astype(BF16)
        km = k2 * head_mask
        att_u[h, c] = (_dot_nt(qm_u[h, c], km.astype(BF16)) * dm_ref[h]).astype(BF16)
        kd_u[h, c] = (km * dk_ref[h]).astype(BF16)
    intra_u = {u: _dot(att_u[u], vh_u[u]) for u in units}
    delta_u = {u: _dot_tn(kd_u[u], vh_u[u]) for u in units}
    state_u = {}
    for h in range(RET_HEADS):
        st = st_ref[h]
        for c in chunk_order:
            state_u[h, c] = st.astype(BF16)
            st = st * ds_ref[h] + delta_u[h, c]
        st_ref[h] = st
    for h, c in units:
        rows, out_cols = rows_of(c), vcols(h)
        o = intra_u[h, c] + _dot(qm_u[h, c], state_u[h, c]) * dq_ref[h]
        if reverse:
            o = o + of_ref[rows, out_cols]
            o = o * lax.rsqrt(jnp.mean(o * o, axis=-1, keepdims=True) + EPS)
            o = o * gn_ref[:, out_cols] * _silu(g_ref[rows, out_cols])
            o_ref[rows, out_cols] = o.astype(o_ref.dtype)
        else:
            o_ref[rows, out_cols] = o

    ends = _is_first_block(blk) if reverse else _is_last_block(blk)

    @pl.when(jnp.logical_and(ends, is_prompt))
    def _():
        for h in range(RET_HEADS):
            off = (h % 2) * RET_DK
            sfin_ref[h] = st_ref[h, off:off + RET_DK, :]


def _sample_seq(blk):
    return jnp.clip(_seq_of_block(blk) - BATCH, 0, DEC_BATCH - 1)


def _prompt_seq(blk):
    return jnp.minimum(_seq_of_block(blk), BATCH - 1)


def _retention(decay, q, k, v, cos_tab, sin_tab, state_in, state_out, layer, *, reverse,
               o_fwd=None, g=None, gn=None):
    bmap = (lambda j: NBLK - 1 - j) if reverse else (lambda j: j)
    direction = 1 if reverse else 0
    qd, vd = RET_HEADS * RET_DK, RET_HEADS * RET_DV
    row_spec = lambda width: pl.BlockSpec((RB, width), lambda j: (bmap(j), 0))
    state_blk = (None, None, None, RET_HEADS, RET_DK, RET_DV)
    in_specs = [pl.BlockSpec(memory_space=pltpu.SMEM), row_spec(qd), row_spec(qd), row_spec(vd),
                pl.BlockSpec((RB, RET_PAIR), lambda j: (_rope_block(bmap(j)), 0)),
                pl.BlockSpec((RB, RET_PAIR), lambda j: (_rope_block(bmap(j)), 0)),
                pl.BlockSpec(state_blk, lambda j: (_sample_seq(bmap(j)), layer, direction, 0, 0, 0)),
                pl.BlockSpec(memory_space=pl.ANY)]
    args = [decay, q, k, v, cos_tab, sin_tab, state_in, state_out]
    if reverse:
        in_specs += [row_spec(vd), row_spec(vd), pl.BlockSpec((1, vd), lambda j: (0, 0))]
        args += [o_fwd, g, gn.reshape(1, vd)]
    tile = pltpu.VMEM((RET_HEADS, RC, RC), F32)
    return pl.pallas_call(
        functools.partial(_ret_kernel, reverse=reverse),
        grid=(NBLK,),
        in_specs=in_specs,
        out_specs=[row_spec(vd),
                   pl.BlockSpec(state_blk, lambda j: (_prompt_seq(bmap(j)), layer, direction, 0, 0, 0))],
        out_shape=[jax.ShapeDtypeStruct((NT, vd), BF16 if reverse else F32),
                   jax.ShapeDtypeStruct(state_out.shape, F32)],
        input_output_aliases={7: 1},
        scratch_shapes=[tile, tile, tile, tile, tile],
        compiler_params=_params(),
        name="retention_bwd" if reverse else "retention_fwd",
    )(*args)


CONV_RT = 32
CONV_CT = 128
CONV_SPAN = RB + 2 * HALO - 8


def _conv_kernel(a_ref, ga_ref, ap_ref, gap_ref, an_ref, gan_ref, cw_ref, cb_ref, lng_ref, lnb_ref,
                 o_ref, u_ref, y_ref, us_ref):
    blk = pl.program_id(0)
    keep_prev = jnp.where(_is_first_block(blk), 0.0, 1.0)
    keep_next = jnp.where(_is_last_block(blk), 0.0, 1.0)
    u_ref[0:HALO, :] = ap_ref[...] * _sigmoid(gap_ref[...]) * keep_prev
    u_ref[HALO:HALO + RB, :] = a_ref[...] * _sigmoid(ga_ref[...])
    u_ref[HALO + RB:HALO + RB + HALO, :] = an_ref[...] * _sigmoid(gan_ref[...]) * keep_next
    for r in range(1, 8):
        us_ref[r - 1] = u_ref[r:r + CONV_SPAN, :]
    for ct in range(CONV_CH // CONV_CT):
        cols = slice(ct * CONV_CT, (ct + 1) * CONV_CT)
        for rt in range(RB // CONV_RT):
            acc = jnp.zeros((CONV_RT, CONV_CT), F32)
            for w in range(CONV_WIDTH):
                tiles, r = divmod(HALO - CONV_PAD + w, 8)
                base = rt * CONV_RT + 8 * tiles
                src = u_ref if r == 0 else us_ref.at[r - 1]
                acc = acc + src[base:base + CONV_RT, cols] * cw_ref[w:w + 1, cols]
            y_ref[rt * CONV_RT:(rt + 1) * CONV_RT, cols] = acc + cb_ref[:, cols]
    y = y_ref[...]
    mu = jnp.mean(y, axis=-1, keepdims=True)
    var = jnp.mean(jnp.square(y - mu), axis=-1, keepdims=True)
    o_ref[...] = _silu((y - mu) * lax.rsqrt(var + EPS) * lng_ref[...] + lnb_ref[...]).astype(o_ref.dtype)


def _conv_module(a, ga, cw, cb, lng, lnb):
    per_blk = RB // HALO
    n_halo = NT // HALO
    row_spec = pl.BlockSpec((RB, CONV_CH), lambda i: (i, 0))
    prev_spec = pl.BlockSpec((HALO, CONV_CH), lambda i: (jnp.maximum(i * per_blk - 1, 0), 0))
    next_spec = pl.BlockSpec((HALO, CONV_CH), lambda i: (jnp.minimum((i + 1) * per_blk, n_halo - 1), 0))
    vec = pl.BlockSpec((1, CONV_CH), lambda i: (0, 0))
    return pl.pallas_call(
        _conv_kernel,
        grid=(NBLK,),
        in_specs=[row_spec, row_spec, prev_spec, prev_spec, next_spec, next_spec,
                  pl.BlockSpec((CONV_WIDTH, CONV_CH), lambda i: (0, 0)), vec, vec, vec],
        out_specs=row_spec,
        out_shape=jax.ShapeDtypeStruct((NT, CONV_CH), BF16),
        scratch_shapes=[pltpu.VMEM((RB + 2 * HALO, CONV_CH), F32), pltpu.VMEM((RB, CONV_CH), F32),
                        pltpu.VMEM((7, CONV_SPAN, CONV_CH), F32)],
        compiler_params=_params(),
        name="conv_module",
    )(a, ga, a, ga, a, ga, cw, cb.reshape(1, -1), lng.reshape(1, -1), lnb.reshape(1, -1))


GC = GLA_CHUNK
GLA_NSUB = GC // GLA_SUB


def _split3(a):
    p1 = a.astype(BF16)
    r1 = a - p1.astype(F32)
    p2 = r1.astype(BF16)
    p3 = (r1 - p2.astype(F32)).astype(BF16)
    return p1, p2, p3


def _gla_kernel(alr_ref, alr_next_ref, q_ref, k_ref, v_ref, wa_ref, ba_ref, s0_ref, acc_ref, *rest, reverse):
    del acc_ref
    if reverse:
        of_ref, r_ref, gn_ref, o_ref, sfin_ref, st_ref, b2_ref, tri_ref, ob_ref = rest
    else:
        o_ref, sfin_ref, st_ref, b2_ref, tri_ref = rest
        ob_ref = o_ref
    step = pl.program_id(0)
    blk = NBLK - 1 - step if reverse else step
    starts = _is_last_block(blk) if reverse else _is_first_block(blk)
    is_prompt = blk < N_PROMPT_BLK

    @pl.when(jnp.logical_and(starts, is_prompt))
    def _():
        st_ref[...] = jnp.zeros_like(st_ref)

    @pl.when(jnp.logical_and(starts, jnp.logical_not(is_prompt)))
    def _():
        for h in range(GLA_HEADS):
            st_ref[h] = s0_ref[h].T

    def cumulative_gates(alr):
        z = _dot_hi(alr, wa_ref[...]) + ba_ref[...]
        log_a = (jnp.minimum(z, 0.0) - jnp.log(1.0 + jnp.exp(-jnp.abs(z)))) * (1.0 / GLA_TAU)
        g1, g2, g3 = _split3(log_a)
        tri = tri_ref[...]
        return _dot(tri, g1) + (_dot(tri, g2) + _dot(tri, g3))

    slot = step % 2

    @pl.when(step == 0)
    def _():
        row = lax.broadcasted_iota(jnp.int32, (RB, RB), 0)
        col = lax.broadcasted_iota(jnp.int32, (RB, RB), 1)
        ordered = col >= row if reverse else col <= row
        tri_ref[...] = jnp.where(jnp.logical_and(row // GC == col // GC, ordered), 1.0, 0.0).astype(BF16)
        b2_ref[0] = cumulative_gates(alr_ref[...])

    b2_ref[1 - slot] = cumulative_gates(alr_next_ref[...])
    b_ref = b2_ref.at[slot]

    c_row = lax.broadcasted_iota(jnp.int32, (GC, 1), 0)
    a_row = lax.broadcasted_iota(jnp.int32, (GC, GC), 0)
    a_col = lax.broadcasted_iota(jnp.int32, (GC, GC), 1)
    att_mask = a_col > a_row if reverse else a_col <= a_row
    chunks = range(RB // GC)
    chunk_order = list(reversed(chunks) if reverse else chunks)
    units = [(h, c) for h in range(GLA_HEADS) for c in chunk_order]
    kcols = lambda h: slice(h * GLA_DK, (h + 1) * GLA_DK)
    vcols = lambda h: slice(h * GLA_DV, (h + 1) * GLA_DV)
    rows_of = lambda c: slice(c * GC, (c + 1) * GC)

    vh_u, qe_u, ke_u, decay_u, att_u = {}, {}, {}, {}, {}
    for h, c in units:
        rows = rows_of(c)
        b = b_ref[rows, kcols(h)]
        qh = q_ref[rows, kcols(h)] * (GLA_DK ** -0.5)
        kh = k_ref[rows, kcols(h)]
        vh_u[h, c] = v_ref[rows, vcols(h)].astype(BF16)
        edge = b[0:1, :] if reverse else b[GC - 1:GC, :]
        bounds = []
        for s in range(GLA_NSUB):
            if reverse:
                hi = (s + 1) * GLA_SUB
                bounds.append(b[hi:hi + 1, :] if s < GLA_NSUB - 1 else jnp.zeros((1, GLA_DK), F32))
            else:
                lo = s * GLA_SUB
                bounds.append(b[lo - 1:lo, :] if s > 0 else jnp.zeros((1, GLA_DK), F32))
        own = jnp.concatenate([jnp.broadcast_to(bd, (GLA_SUB, GLA_DK)) for bd in bounds], axis=0)
        q_own = qh * jnp.exp(b - own)
        q_parts, k_parts = [], []
        for s, bd in enumerate(bounds):
            q_parts.append(jnp.where(c_row // GLA_SUB == s, q_own, 0.0))
            reach = c_row >= s * GLA_SUB if reverse else c_row < (s + 1) * GLA_SUB
            k_parts.append(kh * jnp.exp(jnp.where(reach, bd - b, -jnp.inf)))
        q_bd = jnp.concatenate(q_parts, axis=1).astype(BF16)
        k_cat = jnp.concatenate(k_parts, axis=1).astype(BF16)
        att_u[h, c] = jnp.where(att_mask, _dot_nt(q_bd, k_cat), 0.0).astype(BF16)
        qe_u[h, c] = (qh * jnp.exp(b)).astype(BF16)
        ke_u[h, c] = (kh * jnp.exp(edge - b)).astype(BF16)
        decay_u[h, c] = jnp.exp(edge)
    intra_u = {u: _dot(att_u[u], vh_u[u]) for u in units}
    delta_u = {u: _dot_tn(vh_u[u], ke_u[u]) for u in units}
    state_u = {}
    for h in range(GLA_HEADS):
        st = st_ref[h]
        for c in chunk_order:
            state_u[h, c] = st.astype(BF16)
            st = st * decay_u[h, c] + delta_u[h, c]
        st_ref[h] = st
    for h, c in units:
        ob_ref[rows_of(c), vcols(h)] = intra_u[h, c] + _dot_nt(qe_u[h, c], state_u[h, c])

    if reverse:
        for h in range(GLA_HEADS):
            cols = vcols(h)
            o = ob_ref[:, cols] + of_ref[:, cols]
            o = o * lax.rsqrt(jnp.mean(o * o, axis=-1, keepdims=True) + EPS)
            o_ref[:, cols] = (o * gn_ref[:, cols] * _silu(r_ref[:, cols])).astype(o_ref.dtype)

    ends = _is_first_block(blk) if reverse else _is_last_block(blk)

    @pl.when(jnp.logical_and(ends, is_prompt))
    def _():
        for h in range(GLA_HEADS):
            sfin_ref[h] = st_ref[h].T


def _gla(alr, q, k, v, wa, ba, state_in, state_out, layer, *, reverse, o_fwd=None, r=None, gn=None):
    bmap = (lambda j: NBLK - 1 - j) if reverse else (lambda j: j)
    direction = 1 if reverse else 0
    qd, vd = GLA_HEADS * GLA_DK, GLA_HEADS * GLA_DV
    row_spec = lambda width: pl.BlockSpec((RB, width), lambda j: (bmap(j), 0))
    state_blk = (None, None, None, GLA_HEADS, GLA_DK, GLA_DV)
    next_spec = pl.BlockSpec((RB, 2 * GLA_RANK), lambda j: (bmap(jnp.minimum(j + 1, NBLK - 1)), 0))
    in_specs = [row_spec(2 * GLA_RANK), next_spec, row_spec(qd), row_spec(qd), row_spec(vd),
                pl.BlockSpec((2 * GLA_RANK, qd), lambda j: (0, 0)),
                pl.BlockSpec((1, qd), lambda j: (0, 0)),
                pl.BlockSpec(state_blk, lambda j: (_sample_seq(bmap(j)), layer, direction, 0, 0, 0)),
                pl.BlockSpec(memory_space=pl.ANY)]
    args = [alr, alr, q, k, v, wa, ba, state_in, state_out]
    scratch = [pltpu.VMEM((GLA_HEADS, GLA_DV, GLA_DK), F32), pltpu.VMEM((2, RB, qd), F32),
               pltpu.VMEM((RB, RB), BF16)]
    if reverse:
        in_specs += [row_spec(vd), row_spec(vd), pl.BlockSpec((1, vd), lambda j: (0, 0))]
        args += [o_fwd, r, gn.reshape(1, vd)]
        scratch += [pltpu.VMEM((RB, vd), F32)]
    return pl.pallas_call(
        functools.partial(_gla_kernel, reverse=reverse),
        grid=(NBLK,),
        in_specs=in_specs,
        out_specs=[row_spec(vd),
                   pl.BlockSpec(state_blk, lambda j: (_prompt_seq(bmap(j)), layer, direction, 0, 0, 0))],
        out_shape=[jax.ShapeDtypeStruct((NT, vd), BF16 if reverse else F32),
                   jax.ShapeDtypeStruct(state_out.shape, F32)],
        input_output_aliases={8: 1},
        scratch_shapes=scratch,
        compiler_params=_params(),
        name="gla_bwd" if reverse else "gla_fwd",
    )(*args)


def _outproj_kernel(*refs, n_mix):
    mix_refs = refs[:n_mix]
    (w_ref, x_ref, gate_ref, g2_ref, shift_ref, scale_ref, rw_ref, rb_ref,
     x1_ref, h2_ref, logit_ref, w_bf) = refs[n_mix:]
    step = pl.program_id(0)

    @pl.when(step == 0)
    def _():
        w_bf[...] = w_ref[...].astype(BF16)

    row = _mod_row(step)
    m, off = None, 0
    for mix_ref in mix_refs:
        width = mix_ref.shape[1]
        part = _dot(mix_ref[...], w_bf[off:off + width, :])
        m = part if m is None else m + part
        off += width
    x1 = x_ref[...] + gate_ref[pl.ds(row, 1), :] * m
    x1_ref[...] = x1
    h2 = _norm_mod(x1, g2_ref, shift_ref, scale_ref, row)
    h2_ref[...] = _pack_rows(h2)
    h_hi, h_lo = _split2(h2)
    r_hi, r_lo = _split2(rw_ref[...])
    logit_ref[...] = _dot_nt(r_hi, h_hi) + (_dot_nt(r_hi, h_lo) + _dot_nt(r_lo, h_hi)) + rb_ref[...]


def _outproj(mixes, w_all, index, x, gate, g2, shift, scale, rw, rb):
    n_mix = len(mixes)
    row_spec = lambda width: pl.BlockSpec((RB, width), lambda i: (i, 0))
    full = lambda shape: pl.BlockSpec(shape, lambda i: (0,) * len(shape))
    mod_spec = full((MOD_ROWS, D_MODEL))
    n_mixed = w_all.shape[1]
    return pl.pallas_call(
        functools.partial(_outproj_kernel, n_mix=n_mix),
        grid=(NBLK,),
        in_specs=[row_spec(m.shape[1]) for m in mixes]
        + [pl.BlockSpec((None, n_mixed, D_MODEL), lambda i: (index, 0, 0)),
           row_spec(D_MODEL), mod_spec, full((1, D_MODEL)), mod_spec, mod_spec,
           full((N_EXPERTS, D_MODEL)), full((N_EXPERTS, 1))],
        out_specs=[row_spec(D_MODEL), row_spec(D_MODEL // 2), pl.BlockSpec((N_EXPERTS, RB), lambda i: (0, i))],
        out_shape=[jax.ShapeDtypeStruct((NT, D_MODEL), F32), jax.ShapeDtypeStruct((NT, D_MODEL // 2), jnp.uint32),
                   jax.ShapeDtypeStruct((N_EXPERTS, NT), F32)],
        scratch_shapes=[pltpu.VMEM((n_mixed, D_MODEL), BF16)],
        compiler_params=_params(),
        name="outproj",
    )(*mixes, w_all, x, gate, g2.reshape(1, D_MODEL), shift, scale, rw.T, rb.reshape(N_EXPERTS, 1))


ROUTE_BLK = 2048
ROUTE_SUB = 256


def _route_kernel(lg_ref, idx_ref, rank_ref, gt_ref, cnt_ref, carry_ref):
    @pl.when(pl.program_id(0) == 0)
    def _():
        carry_ref[...] = jnp.zeros_like(carry_ref)

    logits = lg_ref[...]
    eid = lax.broadcasted_iota(jnp.int32, logits.shape, 0).astype(F32)
    work = logits
    onehots, top_vals = [], []
    for kk in range(TOP_K):
        top = jnp.max(work, axis=0, keepdims=True)
        first = jnp.min(jnp.where(work == top, eid, float(N_EXPERTS)), axis=0, keepdims=True)
        onehot = eid == first
        idx_ref[kk:kk + 1, :] = first.astype(jnp.int32)
        onehots.append(onehot)
        top_vals.append(top)
        work = jnp.where(onehot, -jnp.inf, work)
    exps = [jnp.exp(v - top_vals[0]) for v in top_vals]
    denom = exps[0]
    for e in exps[1:]:
        denom = denom + e
    gt_ref[...] = jnp.zeros_like(gt_ref)
    for kk in range(TOP_K):
        gt_ref[kk:kk + 1, :] = exps[kk] / denom

    sel = jnp.zeros(logits.shape, F32)
    for onehot in onehots:
        sel = sel + jnp.where(onehot, 1.0, 0.0)
    sel = sel.astype(BF16)
    r_i = lax.broadcasted_iota(jnp.int32, (ROUTE_SUB, ROUTE_SUB), 0)
    c_i = lax.broadcasted_iota(jnp.int32, (ROUTE_SUB, ROUTE_SUB), 1)
    before = jnp.where(r_i < c_i, 1.0, 0.0).astype(BF16)
    ones = jnp.ones((ROUTE_SUB, ROUTE_SUB), BF16)
    carry = carry_ref[...]
    for s in range(ROUTE_BLK // ROUTE_SUB):
        cols = slice(s * ROUTE_SUB, (s + 1) * ROUTE_SUB)
        pos = _dot(sel[:, cols], before) + carry
        for kk in range(TOP_K):
            rank = jnp.sum(jnp.where(onehots[kk][:, cols], pos, 0.0), axis=0, keepdims=True)
            rank_ref[kk:kk + 1, cols] = rank.astype(jnp.int32)
        carry = carry + _dot(sel[:, cols], ones)
    carry_ref[...] = carry
    cnt_ref[...] = carry.astype(jnp.int32)


def _route(logits_t):
    col_spec = lambda rows: pl.BlockSpec((rows, ROUTE_BLK), lambda i: (0, i))
    return pl.pallas_call(
        _route_kernel,
        grid=(NT // ROUTE_BLK,),
        in_specs=[col_spec(N_EXPERTS)],
        out_specs=[col_spec(TOP_K), col_spec(TOP_K), col_spec(8),
                   pl.BlockSpec((N_EXPERTS, ROUTE_SUB), lambda i: (0, 0))],
        out_shape=[jax.ShapeDtypeStruct((TOP_K, NT), jnp.int32), jax.ShapeDtypeStruct((TOP_K, NT), jnp.int32),
                   jax.ShapeDtypeStruct((8, NT), F32), jax.ShapeDtypeStruct((N_EXPERTS, ROUTE_SUB), jnp.int32)],
        scratch_shapes=[pltpu.VMEM((N_EXPERTS, ROUTE_SUB), F32)],
        compiler_params=_params(),
        name="route",
    )(logits_t)


TM = 512
TM_SUB = 256
MOE_NBLK = NT * TOP_K // TM + N_EXPERTS
MOE_ROWS = MOE_NBLK * TM
HALF = D_MODEL // 2


def _moe_kernel(be_ref, nv_ref, nx_ref, x_ref, wgu_hbm, bgu_ref, wd_hbm, bd_ref, y_ref,
                wgu_st, wd_st, wgu_bf, wd_bf, sems, *, layer):
    i = pl.program_id(0)
    n_valid = nv_ref[i]

    def fetch(e):
        return (pltpu.make_async_copy(wgu_hbm.at[layer, e], wgu_st, sems.at[0]),
                pltpu.make_async_copy(wd_hbm.at[layer, e], wd_st, sems.at[1]))

    @pl.when(i == 0)
    def _():
        for cp in fetch(be_ref[0]):
            cp.start()

    @pl.when(n_valid > 0)
    def _():
        e = be_ref[i]
        changed = jnp.logical_or(i == 0, e != be_ref[jnp.maximum(i - 1, 0)])

        @pl.when(changed)
        def _():
            for cp in fetch(e):
                cp.wait()
            wgu_bf[...] = wgu_st[...].astype(BF16)
            wd_bf[...] = wd_st[...].astype(BF16)
            nxt = nx_ref[e]

            @pl.when(nxt >= 0)
            def _():
                for cp in fetch(nxt):
                    cp.start()

    for p in range(TM // TM_SUB):
        @pl.when(n_valid > p * TM_SUB)
        def _():
            rows = slice(p * TM_SUB, (p + 1) * TM_SUB)
            row_id = lax.broadcasted_iota(jnp.int32, (TM_SUB, 1), 0) + p * TM_SUB
            x_lo, x_hi = _unpack_rows(jnp.where(row_id < n_valid, x_ref[rows, :], jnp.uint32(0)))
            gu = (_dot(x_lo.astype(BF16), wgu_bf[:HALF, :]) + _dot(x_hi.astype(BF16), wgu_bf[HALF:, :])
                  + bgu_ref[...])
            gate = jnp.minimum(gu[:, :D_FF], SWIGLU_LIMIT)
            up = jnp.clip(gu[:, D_FF:], -SWIGLU_LIMIT, SWIGLU_LIMIT)
            hdn = gate * _sigmoid(SWIGLU_ALPHA * gate) * (up + 1.0)
            y_ref[rows, :] = _pack_rows(_dot(hdn.astype(BF16), wd_bf[...]) + bd_ref[...])


def _moe_experts(layer, block_e, n_valid, next_e, xs, w_gu, b_gu, w_down, b_down):
    grid_spec = pltpu.PrefetchScalarGridSpec(
        num_scalar_prefetch=3,
        grid=(MOE_NBLK,),
        in_specs=[pl.BlockSpec((TM, HALF), lambda i, be, nv, nx: (i, 0)),
                  pl.BlockSpec(memory_space=pl.ANY),
                  pl.BlockSpec((None, None, 1, 2 * D_FF), lambda i, be, nv, nx: (layer, be[i], 0, 0)),
                  pl.BlockSpec(memory_space=pl.ANY),
                  pl.BlockSpec((None, None, 1, D_MODEL), lambda i, be, nv, nx: (layer, be[i], 0, 0))],
        out_specs=pl.BlockSpec((TM, HALF), lambda i, be, nv, nx: (i, 0)),
        scratch_shapes=[pltpu.VMEM((D_MODEL, 2 * D_FF), F32), pltpu.VMEM((D_FF, D_MODEL), F32),
                        pltpu.VMEM((D_MODEL, 2 * D_FF), BF16), pltpu.VMEM((D_FF, D_MODEL), BF16),
                        pltpu.SemaphoreType.DMA((2,))],
    )
    return pl.pallas_call(
        functools.partial(_moe_kernel, layer=layer),
        grid_spec=grid_spec,
        out_shape=jax.ShapeDtypeStruct((MOE_ROWS, HALF), jnp.uint32),
        compiler_params=_params(),
        name="moe_experts",
    )(block_e, n_valid, next_e, xs, w_gu, b_gu.reshape(DEPTH, N_EXPERTS, 1, -1), w_down,
      b_down.reshape(DEPTH, N_EXPERTS, 1, -1))


SC_WORKERS = 32
SC_WIN = 64


def _sc_mesh():
    return plsc.VectorSubcoreMesh(core_axis_name="core", subcore_axis_name="subcore")


def _sc_worker():
    return lax.axis_index("core") * (SC_WORKERS // 2) + lax.axis_index("subcore")


def _sc_scatter_rows(x, dest_t, n_rows):
    n, width = x.shape
    kk = dest_t.shape[0]
    per = n // SC_WORKERS
    n_win = per // SC_WIN
    assert per * SC_WORKERS == n and n_win * SC_WIN == per and n_win % 2 == 0

    @pl.kernel(out_type=jax.ShapeDtypeStruct((n_rows, width), x.dtype), mesh=_sc_mesh(),
               scratch_types=[pltpu.VMEM((kk, per), jnp.int32), pltpu.VMEM((SC_WIN, width), x.dtype),
                              pltpu.VMEM((SC_WIN, width), x.dtype), pltpu.SemaphoreType.DMA((4,))])
    def scatter(x_hbm, i_hbm, o_hbm, idx_v, buf0, buf1, sems):
        base = _sc_worker() * per
        pltpu.sync_copy(i_hbm.at[:, pl.ds(base, per)], idx_v)

        def get(j, buf, s):
            return pltpu.make_async_copy(x_hbm.at[pl.ds(base + j * SC_WIN, SC_WIN)], buf, sems.at[s])

        def put(j, q, buf, s):
            return pltpu.make_async_copy(buf, o_hbm.at[idx_v.at[q, pl.ds(j * SC_WIN, SC_WIN)]], sems.at[s])

        get(0, buf0, 0).start()

        @pl.loop(0, n_win, step=2)
        def _(j):
            get(j, buf0, 0).wait()

            @pl.when(j > 0)
            def _():
                for q in range(kk):
                    put(j - 1, q, buf1, 3).wait()

            get(j + 1, buf1, 1).start()
            for q in range(kk):
                put(j, q, buf0, 2).start()
            get(j + 1, buf1, 1).wait()
            for q in range(kk):
                put(j, q, buf0, 2).wait()

            @pl.when(j + 2 < n_win)
            def _():
                get(j + 2, buf0, 0).start()

            for q in range(kk):
                put(j + 1, q, buf1, 3).start()

        for q in range(kk):
            put(n_win - 1, q, buf1, 3).wait()

    return scatter(x, dest_t)


def _sc_gather_rows(y, idx):
    n = idx.shape[0]
    width = y.shape[1]
    per = n // SC_WORKERS
    n_win = per // SC_WIN
    assert per * SC_WORKERS == n and n_win * SC_WIN == per and n_win % 2 == 0

    @pl.kernel(out_type=jax.ShapeDtypeStruct((n, width), y.dtype), mesh=_sc_mesh(),
               scratch_types=[pltpu.VMEM((per,), jnp.int32), pltpu.VMEM((SC_WIN, width), y.dtype),
                              pltpu.VMEM((SC_WIN, width), y.dtype), pltpu.SemaphoreType.DMA((4,))])
    def gather(y_hbm, i_hbm, o_hbm, idx_v, buf0, buf1, sems):
        base = _sc_worker() * per
        pltpu.sync_copy(i_hbm.at[pl.ds(base, per)], idx_v)

        def get(j, buf, s):
            return pltpu.make_async_copy(y_hbm.at[idx_v.at[pl.ds(j * SC_WIN, SC_WIN)]], buf, sems.at[s])

        def put(j, buf, s):
            return pltpu.make_async_copy(buf, o_hbm.at[pl.ds(base + j * SC_WIN, SC_WIN)], sems.at[s])

        get(0, buf0, 0).start()

        @pl.loop(0, n_win, step=2)
        def _(j):
            get(j, buf0, 0).wait()

            @pl.when(j > 0)
            def _():
                put(j - 1, buf1, 3).wait()

            get(j + 1, buf1, 1).start()
            put(j, buf0, 2).start()
            get(j + 1, buf1, 1).wait()
            put(j, buf0, 2).wait()

            @pl.when(j + 2 < n_win)
            def _():
                get(j + 2, buf0, 0).start()

            put(j + 1, buf1, 3).start()

        put(n_win - 1, buf1, 3).wait()

    return gather(y, idx)


def _combine_kernel(x1_ref, *rest, final):
    yg_refs = rest[:TOP_K]
    if final:
        gt_ref, gate_ref, fg_ref, op_ref, os_ref = rest[TOP_K:]
    else:
        gt_ref, gate_ref, o_ref = rest[TOP_K:]
    i = pl.program_id(0)
    row = _mod_row(i)
    r_i = lax.broadcasted_iota(jnp.int32, (RB, RB), 0)
    c_i = lax.broadcasted_iota(jnp.int32, (RB, RB), 1)
    eye = jnp.where(r_i == c_i, 1.0, 0.0).astype(BF16)
    g1, g2, g3 = _split3(gt_ref[...])
    gw = _dot_nt(eye, g1) + (_dot_nt(eye, g2) + _dot_nt(eye, g3))
    acc_lo, acc_hi = None, None
    for kk in range(TOP_K):
        y_lo, y_hi = _unpack_rows(yg_refs[kk][...])
        w = gw[:, kk:kk + 1]
        acc_lo = y_lo * w if acc_lo is None else acc_lo + y_lo * w
        acc_hi = y_hi * w if acc_hi is None else acc_hi + y_hi * w
    x_lo = x1_ref[:, :HALF] + gate_ref[pl.ds(row, 1), :HALF] * acc_lo
    x_hi = x1_ref[:, HALF:] + gate_ref[pl.ds(row, 1), HALF:] * acc_hi
    if not final:
        o_ref[:, :HALF] = x_lo
        o_ref[:, HALF:] = x_hi
        return
    ms = (jnp.sum(x_lo * x_lo, axis=-1, keepdims=True) + jnp.sum(x_hi * x_hi, axis=-1, keepdims=True)) / D_MODEL
    scale = lax.rsqrt(ms + EPS)

    @pl.when(i < N_PROMPT_BLK)
    def _():
        op_ref[:, :HALF] = x_lo * scale * fg_ref[:, :HALF]
        op_ref[:, HALF:] = x_hi * scale * fg_ref[:, HALF:]

    @pl.when(i >= N_PROMPT_BLK)
    def _():
        os_ref[:, :HALF] = x_lo * scale * fg_ref[:, :HALF]
        os_ref[:, HALF:] = x_hi * scale * fg_ref[:, HALF:]


def _combine(x1, yg, gates_t, gate, final_g=None):
    final = final_g is not None
    row_spec = lambda width: pl.BlockSpec((RB, width), lambda i: (i, 0))
    slot_spec = lambda k: pl.BlockSpec((RB, HALF), lambda i: (k * NBLK + i, 0))
    in_specs = ([row_spec(D_MODEL)] + [slot_spec(k) for k in range(TOP_K)]
                + [pl.BlockSpec((8, RB), lambda i: (0, i)), pl.BlockSpec((MOD_ROWS, D_MODEL), lambda i: (0, 0))])
    args = [x1] + [yg] * TOP_K + [gates_t, gate]
    if final:
        in_specs.append(pl.BlockSpec((1, D_MODEL), lambda i: (0, 0)))
        args.append(final_g.reshape(1, D_MODEL))
        out_specs = [pl.BlockSpec((RB, D_MODEL), lambda i: (jnp.minimum(i, N_PROMPT_BLK - 1), 0)),
                     pl.BlockSpec((RB, D_MODEL), lambda i: (jnp.maximum(i - N_PROMPT_BLK, 0), 0))]
        out_shape = [jax.ShapeDtypeStruct((NT_PROMPT, D_MODEL), F32),
                     jax.ShapeDtypeStruct((NT - NT_PROMPT, D_MODEL), F32)]
    else:
        out_specs = row_spec(D_MODEL)
        out_shape = jax.ShapeDtypeStruct((NT, D_MODEL), F32)
    return pl.pallas_call(
        functools.partial(_combine_kernel, final=final),
        grid=(NBLK,),
        in_specs=in_specs,
        out_specs=out_specs,
        out_shape=out_shape,
        compiler_params=_params(),
        name="moe_combine_final" if final else "moe_combine",
    )(*args)


def _routing_plan(counts, idx_t, rank_t):
    counts = counts[:, 0]
    padded = (counts + TM - 1) // TM * TM
    pad_end = jnp.cumsum(padded)
    pad_start = pad_end - padded
    blk_row = (jnp.arange(MOE_NBLK, dtype=jnp.int32) * TM)[:, None]
    ids = jnp.arange(N_EXPERTS, dtype=jnp.int32)
    owns = jnp.logical_and(pad_start[None, :] <= blk_row, blk_row < pad_end[None, :])
    last_used = jnp.max(jnp.where(counts > 0, ids, 0))
    block_e = jnp.where(jnp.any(owns, axis=1), jnp.sum(jnp.where(owns, ids[None, :], 0), axis=1), last_used)
    block_e = block_e.astype(jnp.int32)
    left = jnp.clip(counts[None, :] - (blk_row - pad_start[None, :]), 0, TM)
    n_valid = jnp.sum(jnp.where(owns, left, 0), axis=1).astype(jnp.int32)
    start = jnp.zeros(idx_t.shape, jnp.int32)
    for e in range(N_EXPERTS):
        start = jnp.where(idx_t == e, pad_start[e], start)
    dest_t = (start + rank_t).astype(jnp.int32)
    later = jnp.where(jnp.logical_and(counts[None, :] > 0, ids[None, :] > ids[:, None]), ids[None, :], N_EXPERTS)
    next_e = jnp.min(later, axis=1)
    next_e = jnp.where(next_e == N_EXPERTS, -1, next_e).astype(jnp.int32)
    return block_e, n_valid, next_e, dest_t


def _rope_tables():
    rows = DEC_SEQ // GRID_W
    row = jnp.repeat(jnp.arange(rows, dtype=F32), GRID_W)
    col = jnp.tile(jnp.arange(GRID_W, dtype=F32), rows)
    n_f = RET_DK // 4
    freqs = ROPE_THETA ** (-jnp.arange(n_f, dtype=F32) / n_f)
    ang = jnp.concatenate([row[:, None] * freqs, col[:, None] * freqs], axis=-1)
    cos = jnp.repeat(jnp.cos(ang), 2, axis=-1)
    sin = jnp.repeat(jnp.sin(ang), 2, axis=-1) * jnp.tile(jnp.asarray([-1.0, 1.0], F32), RET_DK // 2)
    cos = jnp.concatenate([jnp.ones((RB, RET_DK), F32), cos], axis=0)
    sin = jnp.concatenate([jnp.zeros((RB, RET_DK), F32), sin], axis=0)
    return jnp.tile(cos, (1, 2)), jnp.tile(sin, (1, 2))


def kernel(x_prompt, x_sample, state_ret, state_gla, c, c_ctx, w_mod, b_mod, norm1_g, norm2_g, final_g, even_w_in, ret_decay, ret_gn, conv_w, conv_b, conv_ln_g, conv_ln_b, even_w_out, odd_w_in, gla_w_a2, gla_b_a2, gla_gn, odd_w_out, router_w, router_b, exp_w_gu, exp_b_gu, exp_w_down, exp_b_down):
    x = (x_prompt.reshape(NT_PROMPT, D_MODEL), x_sample.reshape(NT - NT_PROMPT, D_MODEL))
    cvec = jnp.concatenate([c_ctx[None, :], c, jnp.zeros((MOD_ROWS - 1 - DEC_BATCH, D_MODEL), F32)], axis=0)
    mods = _modulation(cvec, w_mod, b_mod).reshape(DEPTH, MOD_ROWS, N_MOD, D_MODEL)
    cos_tab, sin_tab = _rope_tables()
    new_ret = jnp.zeros((BATCH,) + state_ret.shape[1:], F32)
    new_gla = jnp.zeros((BATCH,) + state_gla.shape[1:], F32)
    for l in range(DEPTH):
        mod = [mods[l, :, j, :] for j in range(N_MOD)]
        if l % 2 == 0:
            e = l // 2
            qd, vd = RET_HEADS * RET_DK, RET_HEADS * RET_DV
            proj = _inproj(x, norm1_g[l], mod[0], mod[1], even_w_in, e, (qd, qd, vd, vd, CONV_CH, CONV_CH))
            if isinstance(x, tuple):
                x, proj = proj[0], proj[1:]
            q, k, v, g, a, ga = proj
            o_f, new_ret = _retention(ret_decay[e], q, k, v, cos_tab, sin_tab, state_ret, new_ret, e,
                                      reverse=False)
            ret, new_ret = _retention(ret_decay[e], q, k, v, cos_tab, sin_tab, state_ret, new_ret, e,
                                      reverse=True, o_fwd=o_f, g=g, gn=ret_gn[e])
            u = _conv_module(a, ga, conv_w[e], conv_b[e], conv_ln_g[e], conv_ln_b[e])
            mixes, w_out, w_index = [ret, u], even_w_out, e
        else:
            o = l // 2
            qd, vd = GLA_HEADS * GLA_DK, GLA_HEADS * GLA_DV
            q, k, v, r, alr = _inproj(x, norm1_g[l], mod[0], mod[1], odd_w_in, o,
                                      (qd, qd, vd, vd, 2 * GLA_RANK))
            zeros = jnp.zeros((GLA_RANK, qd), F32)
            wa_f = jnp.concatenate([gla_w_a2[o, 0], zeros], axis=0)
            wa_b = jnp.concatenate([zeros, gla_w_a2[o, 1]], axis=0)
            o_f, new_gla = _gla(alr, q, k, v, wa_f, gla_b_a2[o, 0].reshape(1, qd), state_gla, new_gla, o,
                                reverse=False)
            y, new_gla = _gla(alr, q, k, v, wa_b, gla_b_a2[o, 1].reshape(1, qd), state_gla, new_gla, o,
                              reverse=True, o_fwd=o_f, r=r, gn=gla_gn[o])
            mixes, w_out, w_index = [y], odd_w_out, o
        x1, h2, logits_t = _outproj(mixes, w_out, w_index, x, mod[2], norm2_g[l], mod[3], mod[4],
                                    router_w[l], router_b[l])
        idx_t, rank_t, gates_t, counts = _route(logits_t)
        block_e, n_valid, next_e, dest_t = _routing_plan(counts, idx_t, rank_t)
        xs = _sc_scatter_rows(h2, dest_t, MOE_ROWS)
        yb = _moe_experts(l, block_e, n_valid, next_e, xs, exp_w_gu, exp_b_gu, exp_w_down, exp_b_down)
        yg = _sc_gather_rows(yb, dest_t.reshape(TOP_K * NT))
        if l < DEPTH - 1:
            x = _combine(x1, yg, gates_t, mod[5])
        else:
            y_prompt, y_sample = _combine(x1, yg, gates_t, mod[5], final_g=final_g)
    y_prompt = y_prompt.reshape(BATCH, SEQ, D_MODEL)
    y_sample = y_sample.reshape(DEC_BATCH, DEC_SEQ, D_MODEL)
    return (y_prompt, y_sample, new_ret, new_gla)
```

```python
import functools

import jax
import jax.numpy as jnp
from jax import lax
from jax.experimental import pallas as pl
from jax.experimental.pallas import tpu as pltpu
from jax.experimental.pallas import tpu_sc as plsc

F32 = jnp.float32
BF16 = jnp.bfloat16

D_MODEL = 1024
BATCH = 16
SEQ = 256
DEPTH = 4
DEC_BATCH = 4
DEC_SEQ = 4096
GRID_W = 64
RET_HEADS = 4
RET_DK = 64
RET_DV = 128
RET_CHUNK = 128
CONV_CH = 512
CONV_WIDTH = 31
CONV_PAD = CONV_WIDTH // 2
GLA_HEADS = 4
GLA_DK = 128
GLA_DV = 256
GLA_RANK = 16
GLA_TAU = 16.0
GLA_CHUNK = 64
GLA_SUB = 16
N_EXPERTS = 32
TOP_K = 4
D_FF = 1024
SWIGLU_LIMIT = 7.0
SWIGLU_ALPHA = 1.702
MOE_BLOCK = 128
ROPE_THETA = 10000.0
EPS = 1e-6
N_MOD = 6

RB = 256
NT_PROMPT = BATCH * SEQ
NT = NT_PROMPT + DEC_BATCH * DEC_SEQ
NBLK = NT // RB
N_PROMPT_BLK = NT_PROMPT // RB
SAMPLE_BLK = DEC_SEQ // RB
NSEQ = BATCH + DEC_BATCH
MOD_ROWS = 8
HALO = 16
VMEM_LIMIT = 48 * 1024 * 1024

assert SEQ == RB and DEC_SEQ % RB == 0 and CONV_PAD < HALO


def _seq_of_block(i):
    return jnp.where(i < N_PROMPT_BLK, i, N_PROMPT_BLK + (i - N_PROMPT_BLK) // SAMPLE_BLK)


def _is_first_block(i):
    return jnp.logical_or(i < N_PROMPT_BLK, (i - N_PROMPT_BLK) % SAMPLE_BLK == 0)


def _is_last_block(i):
    return jnp.logical_or(i < N_PROMPT_BLK, (i - N_PROMPT_BLK) % SAMPLE_BLK == SAMPLE_BLK - 1)


def _mod_row(i):
    return jnp.where(i < N_PROMPT_BLK, 0, 1 + (i - N_PROMPT_BLK) // SAMPLE_BLK)


def _rope_block(i):
    return jnp.where(i < N_PROMPT_BLK, 0, 1 + (i - N_PROMPT_BLK) % SAMPLE_BLK)


def _dot(a, b):
    return jnp.dot(a, b, preferred_element_type=F32)


def _dot_nt(a, b):
    return lax.dot_general(a, b, (((1,), (1,)), ((), ())), preferred_element_type=F32)


def _dot_tn(a, b):
    return lax.dot_general(a, b, (((0,), (0,)), ((), ())), preferred_element_type=F32)


def _split2(a):
    hi = a.astype(BF16)
    lo = (a - hi.astype(F32)).astype(BF16)
    return hi, lo


def _dot_hi(a, b):
    a_hi, a_lo = _split2(a)
    b_hi, b_lo = _split2(b)
    return _dot(a_hi, b_hi) + (_dot(a_hi, b_lo) + _dot(a_lo, b_hi))


def _silu(x):
    return x * (1.0 / (1.0 + jnp.exp(-x)))


def _sigmoid(x):
    return 1.0 / (1.0 + jnp.exp(-x))


def _pack_rows(x):
    n = x.shape[1] // 2
    lo = pltpu.bitcast(x[:, :n].astype(BF16).astype(F32), jnp.uint32)
    hi = pltpu.bitcast(x[:, n:].astype(BF16).astype(F32), jnp.uint32)
    return hi | (lo >> 16)


def _unpack_rows(u):
    lo = pltpu.bitcast(u << 16, F32)
    hi = pltpu.bitcast(u & jnp.uint32(0xFFFF0000), F32)
    return lo, hi


def _params(n_axes=1, vmem=VMEM_LIMIT):
    return pltpu.CompilerParams(dimension_semantics=("arbitrary",) * n_axes, vmem_limit_bytes=vmem)


MOD_TN = 1536


def _mod_kernel(c_ref, w_ref, b_ref, o_ref):
    s = _silu(c_ref[...]).astype(BF16)
    o_ref[...] = _dot(s, w_ref[...].astype(BF16)) + b_ref[...]


def _modulation(cvec, w_mod, b_mod):
    n = N_MOD * D_MODEL
    return pl.pallas_call(
        _mod_kernel,
        grid=(DEPTH, n // MOD_TN),
        in_specs=[pl.BlockSpec((MOD_ROWS, D_MODEL), lambda l, j: (0, 0)),
                  pl.BlockSpec((None, D_MODEL, MOD_TN), lambda l, j: (l, 0, j)),
                  pl.BlockSpec((None, 1, MOD_TN), lambda l, j: (l, 0, j))],
        out_specs=pl.BlockSpec((None, MOD_ROWS, MOD_TN), lambda l, j: (l, 0, j)),
        out_shape=jax.ShapeDtypeStruct((DEPTH, MOD_ROWS, n), F32),
        compiler_params=_params(2),
        name="modulation",
    )(cvec, w_mod, b_mod.reshape(DEPTH, 1, n))


def _norm_mod(x, g_ref, shift_ref, scale_ref, row):
    y = x * lax.rsqrt(jnp.mean(x * x, axis=-1, keepdims=True) + EPS) * g_ref[...]
    return y * (1.0 + scale_ref[pl.ds(row, 1), :]) + shift_ref[pl.ds(row, 1), :]


N_SRC = {"split": 2, "moe": 3 + TOP_K}


def _inproj_kernel(*refs, widths, source):
    n_src = N_SRC[source]
    src = refs[:n_src]
    g_ref, shift_ref, scale_ref, w_ref = refs[n_src:n_src + 4]
    outs, w_bf = refs[n_src + 4:-1], refs[-1]
    step = pl.program_id(0)

    @pl.when(step == 0)
    def _():
        w_bf[...] = w_ref[...].astype(BF16)

    row = _mod_row(step)
    if source == "split":
        x = jnp.where(step < N_PROMPT_BLK, src[0][...], src[1][...])
    else:
        x1_ref, yg_refs, gt_ref, gate_ref = src[0], src[1:1 + TOP_K], src[-2], src[-1]
        x = jnp.concatenate(_combined_rows(x1_ref, yg_refs, gt_ref, gate_ref, row), axis=1)
    outs[0][...] = x
    o_refs = outs[1:]
    hb =_norm_mod(x, g_ref, shift_ref, scale_ref, row).astype(BF16)
    off = 0
    for o_ref, width in zip(o_refs, widths):
        o_ref[...] = _dot(hb, w_bf[:, off:off + width])
        off += width


def _inproj(x, g, shift, scale, w_all, index, widths):
    source, x_args = x[0], list(x[1:])
    n_in = w_all.shape[2]
    row_spec = lambda width: pl.BlockSpec((RB, width), lambda i: (i, 0))
    full = lambda shape: pl.BlockSpec(shape, lambda i: (0,) * len(shape))
    if source == "split":
        x_specs = [pl.BlockSpec((RB, D_MODEL), lambda i: (jnp.minimum(i, N_PROMPT_BLK - 1), 0)),
                   pl.BlockSpec((RB, D_MODEL), lambda i: (jnp.maximum(i - N_PROMPT_BLK, 0), 0))]
    else:
        x1, yg, gates_t, gate = x_args
        x_specs, x_args = _combine_specs(), [x1] + [yg] * TOP_K + [gates_t, gate]
    out_widths = (D_MODEL,) + tuple(widths)
    return pl.pallas_call(
        functools.partial(_inproj_kernel, widths=widths, source=source),
        grid=(NBLK,),
        in_specs=x_specs + [full((1, D_MODEL)), full((MOD_ROWS, D_MODEL)), full((MOD_ROWS, D_MODEL)),
                            pl.BlockSpec((None, D_MODEL, n_in), lambda i: (index, 0, 0))],
        out_specs=[row_spec(width) for width in out_widths],
        out_shape=[jax.ShapeDtypeStruct((NT, width), F32) for width in out_widths],
        scratch_shapes=[pltpu.VMEM((D_MODEL, n_in), BF16)],
        compiler_params=_params(),
        name="inproj",
    )(*x_args, g.reshape(1, D_MODEL), shift, scale, w_all)


RC = RET_CHUNK
RET_PAIR = 2 * RET_DK


def _rope(x, cos, sin_signed):
    lane = lax.broadcasted_iota(jnp.int32, x.shape, 1)
    swapped = jnp.where(lane % 2 == 0, pltpu.roll(x, x.shape[1] - 1, 1), pltpu.roll(x, 1, 1))
    return x * cos + swapped * sin_signed


def _ret_kernel(decay_ref, q_ref, k_ref, v_ref, cos_ref, sin_ref, s0_ref, acc_ref, *rest, reverse):
    del acc_ref
    if reverse:
        of_ref, g_ref, gn_ref, o_ref, sfin_ref, st_ref, dm_ref, dq_ref, dk_ref, ds_ref = rest
    else:
        o_ref, sfin_ref, st_ref, dm_ref, dq_ref, dk_ref, ds_ref = rest
    step = pl.program_id(0)
    blk = NBLK - 1 - step if reverse else step
    direction = 1 if reverse else 0

    @pl.when(step == 0)
    def _():
        row = lax.broadcasted_iota(jnp.int32, (RC, RC), 0).astype(F32)
        col = lax.broadcasted_iota(jnp.int32, (RC, RC), 1).astype(F32)
        for h in range(RET_HEADS):
            lg = -jnp.exp(jnp.full((RC, RC), decay_ref[direction, h], F32))
            if reverse:
                diff = col - row
                mask = diff > 0
                q_pow = RC - row
                k_pow = row
            else:
                diff = row - col
                mask = diff >= 0
                q_pow = row + 1.0
                k_pow = RC - 1.0 - row
            dm_ref[h] = jnp.where(mask, jnp.exp(lg * jnp.where(mask, diff, 0.0)), 0.0)
            dq_ref[h] = jnp.exp(lg * q_pow)
            dk_ref[h] = jnp.exp(lg * k_pow)
            ds_ref[h] = jnp.exp(lg * RC)

    starts = _is_last_block(blk) if reverse else _is_first_block(blk)

    is_prompt = blk < N_PROMPT_BLK

    @pl.when(starts)
    def _():
        st_ref[...] = jnp.zeros_like(st_ref)

    @pl.when(jnp.logical_and(starts, jnp.logical_not(is_prompt)))
    def _():
        for h in range(RET_HEADS):
            off = (h % 2) * RET_DK
            st_ref[h, off:off + RET_DK, :] = s0_ref[h]

    lane = lax.broadcasted_iota(jnp.int32, (1, RET_PAIR), 1)
    chunks = range(RB // RC)
    chunk_order = list(reversed(chunks) if reverse else chunks)
    units = [(h, c) for h in range(RET_HEADS) for c in chunk_order]
    rows_of = lambda c: slice(c * RC, (c + 1) * RC)
    vcols = lambda h: slice(h * RET_DV, (h + 1) * RET_DV)
    roped = {}
    for p in range(RET_HEADS // 2):
        cols = slice(p * RET_PAIR, (p + 1) * RET_PAIR)
        for c in chunk_order:
            rows = rows_of(c)
            cos, sin = cos_ref[rows, :], sin_ref[rows, :]
            roped[p, c] = (_rope(q_ref[rows, cols], cos, sin),
                           _rope(k_ref[rows, cols] * (RET_DK ** -0.5), cos, sin))
    qm_u, vh_u, att_u, kd_u = {}, {}, {}, {}
    for h, c in units:
        head_mask = (lane // RET_DK == h % 2).astype(F32)
        q2, k2 = roped[h // 2, c]
        vh_u[h, c] = v_ref[rows_of(c), vcols(h)].astype(BF16)
        qm_u[h, c] = (q2 * head_mask).astype(BF16)
        km = k2 * head_mask
        att_u[h, c] = (_dot_nt(qm_u[h, c], km.astype(BF16)) * dm_ref[h]).astype(BF16)
        kd_u[h, c] = (km * dk_ref[h]).astype(BF16)
    intra_u = {u: _dot(att_u[u], vh_u[u]) for u in units}
    delta_u = {u: _dot_tn(kd_u[u], vh_u[u]) for u in units}
    state_u = {}
    for h in range(RET_HEADS):
        st = st_ref[h]
        for c in chunk_order:
            state_u[h, c] = st.astype(BF16)
            st = st * ds_ref[h] + delta_u[h, c]
        st_ref[h] = st
    for h, c in units:
        rows, out_cols = rows_of(c), vcols(h)
        o = intra_u[h, c] + _dot(qm_u[h, c], state_u[h, c]) * dq_ref[h]
        if reverse:
            o = o + of_ref[rows, out_cols]
            o = o * lax.rsqrt(jnp.mean(o * o, axis=-1, keepdims=True) + EPS)
            o = o * gn_ref[:, out_cols] * _silu(g_ref[rows, out_cols])
            o_ref[rows, out_cols] = o.astype(o_ref.dtype)
        else:
            o_ref[rows, out_cols] = o

    ends = _is_first_block(blk) if reverse else _is_last_block(blk)

    @pl.when(jnp.logical_and(ends, is_prompt))
    def _():
        for h in range(RET_HEADS):
            off = (h % 2) * RET_DK
            sfin_ref[h] = st_ref[h, off:off + RET_DK, :]


def _sample_seq(blk):
    return jnp.clip(_seq_of_block(blk) - BATCH, 0, DEC_BATCH - 1)


def _prompt_seq(blk):
    return jnp.minimum(_seq_of_block(blk), BATCH - 1)


def _retention(decay, q, k, v, cos_tab, sin_tab, state_in, state_out, layer, *, reverse,
               o_fwd=None, g=None, gn=None):
    bmap = (lambda j: NBLK - 1 - j) if reverse else (lambda j: j)
    direction = 1 if reverse else 0
    qd, vd = RET_HEADS * RET_DK, RET_HEADS * RET_DV
    row_spec = lambda width: pl.BlockSpec((RB, width), lambda j: (bmap(j), 0))
    state_blk = (None, None, None, RET_HEADS, RET_DK, RET_DV)
    in_specs = [pl.BlockSpec(memory_space=pltpu.SMEM), row_spec(qd), row_spec(qd), row_spec(vd),
                pl.BlockSpec((RB, RET_PAIR), lambda j: (_rope_block(bmap(j)), 0)),
                pl.BlockSpec((RB, RET_PAIR), lambda j: (_rope_block(bmap(j)), 0)),
                pl.BlockSpec(state_blk, lambda j: (_sample_seq(bmap(j)), layer, direction, 0, 0, 0)),
                pl.BlockSpec(memory_space=pl.ANY)]
    args = [decay, q, k, v, cos_tab, sin_tab, state_in, state_out]
    if reverse:
        in_specs += [row_spec(vd), row_spec(vd), pl.BlockSpec((1, vd), lambda j: (0, 0))]
        args += [o_fwd, g, gn.reshape(1, vd)]
    tile = pltpu.VMEM((RET_HEADS, RC, RC), F32)
    return pl.pallas_call(
        functools.partial(_ret_kernel, reverse=reverse),
        grid=(NBLK,),
        in_specs=in_specs,
        out_specs=[row_spec(vd),
                   pl.BlockSpec(state_blk, lambda j: (_prompt_seq(bmap(j)), layer, direction, 0, 0, 0))],
        out_shape=[jax.ShapeDtypeStruct((NT, vd), BF16 if reverse else F32),
                   jax.ShapeDtypeStruct(state_out.shape, F32)],
        input_output_aliases={7: 1},
        scratch_shapes=[tile, tile, tile, tile, tile],
        compiler_params=_params(),
        name="retention_bwd" if reverse else "retention_fwd",
    )(*args)


CONV_RT = 32
CONV_CT = 128
CONV_SPAN = RB + 2 * HALO - 8


def _conv_kernel(a_ref, ga_ref, ap_ref, gap_ref, an_ref, gan_ref, cw_ref, cb_ref, lng_ref, lnb_ref,
                 o_ref, u_ref, y_ref, us_ref):
    blk = pl.program_id(0)
    keep_prev = jnp.where(_is_first_block(blk), 0.0, 1.0)
    keep_next = jnp.where(_is_last_block(blk), 0.0, 1.0)
    u_ref[0:HALO, :] = ap_ref[...] * _sigmoid(gap_ref[...]) * keep_prev
    u_ref[HALO:HALO + RB, :] = a_ref[...] * _sigmoid(ga_ref[...])
    u_ref[HALO + RB:HALO + RB + HALO, :] = an_ref[...] * _sigmoid(gan_ref[...]) * keep_next
    for r in range(1, 8):
        us_ref[r - 1] = u_ref[r:r + CONV_SPAN, :]
    for ct in range(CONV_CH // CONV_CT):
        cols = slice(ct * CONV_CT, (ct + 1) * CONV_CT)
        for rt in range(RB // CONV_RT):
            acc = jnp.zeros((CONV_RT, CONV_CT), F32)
            for w in range(CONV_WIDTH):
                tiles, r = divmod(HALO - CONV_PAD + w, 8)
                base = rt * CONV_RT + 8 * tiles
                src = u_ref if r == 0 else us_ref.at[r - 1]
                acc = acc + src[base:base + CONV_RT, cols] * cw_ref[w:w + 1, cols]
            y_ref[rt * CONV_RT:(rt + 1) * CONV_RT, cols] = acc + cb_ref[:, cols]
    y = y_ref[...]
    mu = jnp.mean(y, axis=-1, keepdims=True)
    var = jnp.mean(jnp.square(y - mu), axis=-1, keepdims=True)
    o_ref[...] = _silu((y - mu) * lax.rsqrt(var + EPS) * lng_ref[...] + lnb_ref[...]).astype(o_ref.dtype)


def _conv_module(a, ga, cw, cb, lng, lnb):
    per_blk = RB // HALO
    n_halo = NT // HALO
    row_spec = pl.BlockSpec((RB, CONV_CH), lambda i: (i, 0))
    prev_spec = pl.BlockSpec((HALO, CONV_CH), lambda i: (jnp.maximum(i * per_blk - 1, 0), 0))
    next_spec = pl.BlockSpec((HALO, CONV_CH), lambda i: (jnp.minimum((i + 1) * per_blk, n_halo - 1), 0))
    vec = pl.BlockSpec((1, CONV_CH), lambda i: (0, 0))
    return pl.pallas_call(
        _conv_kernel,
        grid=(NBLK,),
        in_specs=[row_spec, row_spec, prev_spec, prev_spec, next_spec, next_spec,
                  pl.BlockSpec((CONV_WIDTH, CONV_CH), lambda i: (0, 0)), vec, vec, vec],
        out_specs=row_spec,
        out_shape=jax.ShapeDtypeStruct((NT, CONV_CH), BF16),
        scratch_shapes=[pltpu.VMEM((RB + 2 * HALO, CONV_CH), F32), pltpu.VMEM((RB, CONV_CH), F32),
                        pltpu.VMEM((7, CONV_SPAN, CONV_CH), F32)],
        compiler_params=_params(),
        name="conv_module",
    )(a, ga, a, ga, a, ga, cw, cb.reshape(1, -1), lng.reshape(1, -1), lnb.reshape(1, -1))


GC = GLA_CHUNK
GLA_NSUB = GC // GLA_SUB


def _split3(a):
    p1 = a.astype(BF16)
    r1 = a - p1.astype(F32)
    p2 = r1.astype(BF16)
    p3 = (r1 - p2.astype(F32)).astype(BF16)
    return p1, p2, p3


def _gla_kernel(alr_ref, alr_next_ref, q_ref, k_ref, v_ref, wa_ref, ba_ref, s0_ref, acc_ref, *rest, reverse):
    del acc_ref
    if reverse:
        of_ref, r_ref, gn_ref, o_ref, sfin_ref, st_ref, b2_ref, tri_ref, ob_ref = rest
    else:
        o_ref, sfin_ref, st_ref, b2_ref, tri_ref = rest
        ob_ref = o_ref
    step = pl.program_id(0)
    blk = NBLK - 1 - step if reverse else step
    starts = _is_last_block(blk) if reverse else _is_first_block(blk)
    is_prompt = blk < N_PROMPT_BLK

    @pl.when(jnp.logical_and(starts, is_prompt))
    def _():
        st_ref[...] = jnp.zeros_like(st_ref)

    @pl.when(jnp.logical_and(starts, jnp.logical_not(is_prompt)))
    def _():
        for h in range(GLA_HEADS):
            st_ref[h] = s0_ref[h].T

    def cumulative_gates(alr):
        z = _dot_hi(alr, wa_ref[...]) + ba_ref[...]
        log_a = (jnp.minimum(z, 0.0) - jnp.log(1.0 + jnp.exp(-jnp.abs(z)))) * (1.0 / GLA_TAU)
        g1, g2, g3 = _split3(log_a)
        tri = tri_ref[...]
        return _dot(tri, g1) + (_dot(tri, g2) + _dot(tri, g3))

    slot = step % 2

    @pl.when(step == 0)
    def _():
        row = lax.broadcasted_iota(jnp.int32, (RB, RB), 0)
        col = lax.broadcasted_iota(jnp.int32, (RB, RB), 1)
        ordered = col >= row if reverse else col <= row
        tri_ref[...] = jnp.where(jnp.logical_and(row // GC == col // GC, ordered), 1.0, 0.0).astype(BF16)
        b2_ref[0] = cumulative_gates(alr_ref[...])

    b2_ref[1 - slot] = cumulative_gates(alr_next_ref[...])
    b_ref = b2_ref.at[slot]

    c_row = lax.broadcasted_iota(jnp.int32, (GC, 1), 0)
    a_row = lax.broadcasted_iota(jnp.int32, (GC, GC), 0)
    a_col = lax.broadcasted_iota(jnp.int32, (GC, GC), 1)
    att_mask = a_col > a_row if reverse else a_col <= a_row
    chunks = range(RB // GC)
    chunk_order = list(reversed(chunks) if reverse else chunks)
    units = [(h, c) for h in range(GLA_HEADS) for c in chunk_order]
    kcols = lambda h: slice(h * GLA_DK, (h + 1) * GLA_DK)
    vcols = lambda h: slice(h * GLA_DV, (h + 1) * GLA_DV)
    rows_of = lambda c: slice(c * GC, (c + 1) * GC)

    vh_u, qe_u, ke_u, decay_u, att_u = {}, {}, {}, {}, {}
    for h, c in units:
        rows = rows_of(c)
        b = b_ref[rows, kcols(h)]
        qh = q_ref[rows, kcols(h)] * (GLA_DK ** -0.5)
        kh = k_ref[rows, kcols(h)]
        vh_u[h, c] = v_ref[rows, vcols(h)].astype(BF16)
        edge = b[0:1, :] if reverse else b[GC - 1:GC, :]
        bounds = []
        for s in range(GLA_NSUB):
            if reverse:
                hi = (s + 1) * GLA_SUB
                bounds.append(b[hi:hi + 1, :] if s < GLA_NSUB - 1 else jnp.zeros((1, GLA_DK), F32))
            else:
                lo = s * GLA_SUB
                bounds.append(b[lo - 1:lo, :] if s > 0 else jnp.zeros((1, GLA_DK), F32))
        own = jnp.concatenate([jnp.broadcast_to(bd, (GLA_SUB, GLA_DK)) for bd in bounds], axis=0)
        q_own = qh * jnp.exp(b - own)
        q_parts, k_parts = [], []
        for s, bd in enumerate(bounds):
            q_parts.append(jnp.where(c_row // GLA_SUB == s, q_own, 0.0))
            reach = c_row >= s * GLA_SUB if reverse else c_row < (s + 1) * GLA_SUB
            k_parts.append(kh * jnp.exp(jnp.where(reach, bd - b, -jnp.inf)))
        q_bd = jnp.concatenate(q_parts, axis=1).astype(BF16)
        k_cat = jnp.concatenate(k_parts, axis=1).astype(BF16)
        att_u[h, c] = jnp.where(att_mask, _dot_nt(q_bd, k_cat), 0.0).astype(BF16)
        qe_u[h, c] = (qh * jnp.exp(b)).astype(BF16)
        ke_u[h, c] = (kh * jnp.exp(edge - b)).astype(BF16)
        decay_u[h, c] = jnp.exp(edge)
    intra_u = {u: _dot(att_u[u], vh_u[u]) for u in units}
    delta_u = {u: _dot_tn(vh_u[u], ke_u[u]) for u in units}
    state_u = {}
    for h in range(GLA_HEADS):
        st = st_ref[h]
        for c in chunk_order:
            state_u[h, c] = st.astype(BF16)
            st = st * decay_u[h, c] + delta_u[h, c]
        st_ref[h] = st
    for h, c in units:
        ob_ref[rows_of(c), vcols(h)] = intra_u[h, c] + _dot_nt(qe_u[h, c], state_u[h, c])

    if reverse:
        for h in range(GLA_HEADS):
            cols = vcols(h)
            o = ob_ref[:, cols] + of_ref[:, cols]
            o = o * lax.rsqrt(jnp.mean(o * o, axis=-1, keepdims=True) + EPS)
            o_ref[:, cols] = (o * gn_ref[:, cols] * _silu(r_ref[:, cols])).astype(o_ref.dtype)

    ends = _is_first_block(blk) if reverse else _is_last_block(blk)

    @pl.when(jnp.logical_and(ends, is_prompt))
    def _():
        for h in range(GLA_HEADS):
            sfin_ref[h] = st_ref[h].T


def _gla(alr, q, k, v, wa, ba, state_in, state_out, layer, *, reverse, o_fwd=None, r=None, gn=None):
    bmap = (lambda j: NBLK - 1 - j) if reverse else (lambda j: j)
    direction = 1 if reverse else 0
    qd, vd = GLA_HEADS * GLA_DK, GLA_HEADS * GLA_DV
    row_spec = lambda width: pl.BlockSpec((RB, width), lambda j: (bmap(j), 0))
    state_blk = (None, None, None, GLA_HEADS, GLA_DK, GLA_DV)
    next_spec = pl.BlockSpec((RB, 2 * GLA_RANK), lambda j: (bmap(jnp.minimum(j + 1, NBLK - 1)), 0))
    in_specs = [row_spec(2 * GLA_RANK), next_spec, row_spec(qd), row_spec(qd), row_spec(vd),
                pl.BlockSpec((2 * GLA_RANK, qd), lambda j: (0, 0)),
                pl.BlockSpec((1, qd), lambda j: (0, 0)),
                pl.BlockSpec(state_blk, lambda j: (_sample_seq(bmap(j)), layer, direction, 0, 0, 0)),
                pl.BlockSpec(memory_space=pl.ANY)]
    args = [alr, alr, q, k, v, wa, ba, state_in, state_out]
    scratch = [pltpu.VMEM((GLA_HEADS, GLA_DV, GLA_DK), F32), pltpu.VMEM((2, RB, qd), F32),
               pltpu.VMEM((RB, RB), BF16)]
    if reverse:
        in_specs += [row_spec(vd), row_spec(vd), pl.BlockSpec((1, vd), lambda j: (0, 0))]
        args += [o_fwd, r, gn.reshape(1, vd)]
        scratch += [pltpu.VMEM((RB, vd), F32)]
    return pl.pallas_call(
        functools.partial(_gla_kernel, reverse=reverse),
        grid=(NBLK,),
        in_specs=in_specs,
        out_specs=[row_spec(vd),
                   pl.BlockSpec(state_blk, lambda j: (_prompt_seq(bmap(j)), layer, direction, 0, 0, 0))],
        out_shape=[jax.ShapeDtypeStruct((NT, vd), BF16 if reverse else F32),
                   jax.ShapeDtypeStruct(state_out.shape, F32)],
        input_output_aliases={8: 1},
        scratch_shapes=scratch,
        compiler_params=_params(),
        name="gla_bwd" if reverse else "gla_fwd",
    )(*args)


def _outproj_kernel(*refs, n_mix):
    mix_refs = refs[:n_mix]
    (w_ref, x_ref, gate_ref, g2_ref, shift_ref, scale_ref, rw_ref, rb_ref,
     x1_ref, h2_ref, logit_ref, w_bf) = refs[n_mix:]
    step = pl.program_id(0)

    @pl.when(step == 0)
    def _():
        w_bf[...] = w_ref[...].astype(BF16)

    row = _mod_row(step)
    m, off = None, 0
    for mix_ref in mix_refs:
        width = mix_ref.shape[1]
        part = _dot(mix_ref[...], w_bf[off:off + width, :])
        m = part if m is None else m + part
        off += width
    x1 = x_ref[...] + gate_ref[pl.ds(row, 1), :] * m
    x1_ref[...] = x1
    h2 = _norm_mod(x1, g2_ref, shift_ref, scale_ref, row)
    h2_ref[...] = _pack_rows(h2)
    h_hi, h_lo = _split2(h2)
    r_hi, r_lo = _split2(rw_ref[...])
    logit_ref[...] = _dot_nt(r_hi, h_hi) + (_dot_nt(r_hi, h_lo) + _dot_nt(r_lo, h_hi)) + rb_ref[...]


def _outproj(mixes, w_all, index, x, gate, g2, shift, scale, rw, rb):
    n_mix = len(mixes)
    row_spec = lambda width: pl.BlockSpec((RB, width), lambda i: (i, 0))
    full = lambda shape: pl.BlockSpec(shape, lambda i: (0,) * len(shape))
    mod_spec = full((MOD_ROWS, D_MODEL))
    n_mixed = w_all.shape[1]
    return pl.pallas_call(
        functools.partial(_outproj_kernel, n_mix=n_mix),
        grid=(NBLK,),
        in_specs=[row_spec(m.shape[1]) for m in mixes]
        + [pl.BlockSpec((None, n_mixed, D_MODEL), lambda i: (index, 0, 0)),
           row_spec(D_MODEL), mod_spec, full((1, D_MODEL)), mod_spec, mod_spec,
           full((N_EXPERTS, D_MODEL)), full((N_EXPERTS, 1))],
        out_specs=[row_spec(D_MODEL), row_spec(D_MODEL // 2), pl.BlockSpec((N_EXPERTS, RB), lambda i: (0, i))],
        out_shape=[jax.ShapeDtypeStruct((NT, D_MODEL), F32), jax.ShapeDtypeStruct((NT, D_MODEL // 2), jnp.uint32),
                   jax.ShapeDtypeStruct((N_EXPERTS, NT), F32)],
        scratch_shapes=[pltpu.VMEM((n_mixed, D_MODEL), BF16)],
        compiler_params=_params(),
        name="outproj",
    )(*mixes, w_all, x, gate, g2.reshape(1, D_MODEL), shift, scale, rw.T, rb.reshape(N_EXPERTS, 1))


ROUTE_BLK = 2048
ROUTE_SUB = 256


def _route_kernel(lg_ref, idx_ref, rank_ref, gt_ref, cnt_ref, carry_ref):
    @pl.when(pl.program_id(0) == 0)
    def _():
        carry_ref[...] = jnp.zeros_like(carry_ref)

    logits = lg_ref[...]
    eid = lax.broadcasted_iota(jnp.int32, logits.shape, 0).astype(F32)
    work = logits
    onehots, top_vals = [], []
    for kk in range(TOP_K):
        top = jnp.max(work, axis=0, keepdims=True)
        first = jnp.min(jnp.where(work == top, eid, float(N_EXPERTS)), axis=0, keepdims=True)
        onehot = eid == first
        idx_ref[kk:kk + 1, :] = first.astype(jnp.int32)
        onehots.append(onehot)
        top_vals.append(top)
        work = jnp.where(onehot, -jnp.inf, work)
    exps = [jnp.exp(v - top_vals[0]) for v in top_vals]
    denom = exps[0]
    for e in exps[1:]:
        denom = denom + e
    gt_ref[...] = jnp.zeros_like(gt_ref)
    for kk in range(TOP_K):
        gt_ref[kk:kk + 1, :] = exps[kk] / denom

    sel = jnp.zeros(logits.shape, F32)
    for onehot in onehots:
        sel = sel + jnp.where(onehot, 1.0, 0.0)
    sel = sel.astype(BF16)
    r_i = lax.broadcasted_iota(jnp.int32, (ROUTE_SUB, ROUTE_SUB), 0)
    c_i = lax.broadcasted_iota(jnp.int32, (ROUTE_SUB, ROUTE_SUB), 1)
    before = jnp.where(r_i < c_i, 1.0, 0.0).astype(BF16)
    ones = jnp.ones((ROUTE_SUB, ROUTE_SUB), BF16)
    carry = carry_ref[...]
    for s in range(ROUTE_BLK // ROUTE_SUB):
        cols = slice(s * ROUTE_SUB, (s + 1) * ROUTE_SUB)
        pos = _dot(sel[:, cols], before) + carry
        for kk in range(TOP_K):
            rank = jnp.sum(jnp.where(onehots[kk][:, cols], pos, 0.0), axis=0, keepdims=True)
            rank_ref[kk:kk + 1, cols] = rank.astype(jnp.int32)
        carry = carry + _dot(sel[:, cols], ones)
    carry_ref[...] = carry
    cnt_ref[...] = carry.astype(jnp.int32)


def _route(logits_t):
    col_spec = lambda rows: pl.BlockSpec((rows, ROUTE_BLK), lambda i: (0, i))
    return pl.pallas_call(
        _route_kernel,
        grid=(NT // ROUTE_BLK,),
        in_specs=[col_spec(N_EXPERTS)],
        out_specs=[col_spec(TOP_K), col_spec(TOP_K), col_spec(8),
                   pl.BlockSpec((N_EXPERTS, ROUTE_SUB), lambda i: (0, 0))],
        out_shape=[jax.ShapeDtypeStruct((TOP_K, NT), jnp.int32), jax.ShapeDtypeStruct((TOP_K, NT), jnp.int32),
                   jax.ShapeDtypeStruct((8, NT), F32), jax.ShapeDtypeStruct((N_EXPERTS, ROUTE_SUB), jnp.int32)],
        scratch_shapes=[pltpu.VMEM((N_EXPERTS, ROUTE_SUB), F32)],
        compiler_params=_params(),
        name="route",
    )(logits_t)


TM = 512
TM_SUB = 256
MOE_NBLK = NT * TOP_K // TM + N_EXPERTS
MOE_ROWS = MOE_NBLK * TM
HALF = D_MODEL // 2


def _moe_kernel(be_ref, nv_ref, nx_ref, x_ref, wgu_hbm, bgu_ref, wd_hbm, bd_ref, y_ref,
                wgu_st, wd_st, wgu_bf, wd_bf, sems, *, layer):
    i = pl.program_id(0)
    n_valid = nv_ref[i]

    def fetch(e):
        return (pltpu.make_async_copy(wgu_hbm.at[layer, e], wgu_st, sems.at[0]),
                pltpu.make_async_copy(wd_hbm.at[layer, e], wd_st, sems.at[1]))

    @pl.when(i == 0)
    def _():
        for cp in fetch(be_ref[0]):
            cp.start()

    @pl.when(n_valid > 0)
    def _():
        e = be_ref[i]
        changed = jnp.logical_or(i == 0, e != be_ref[jnp.maximum(i - 1, 0)])

        @pl.when(changed)
        def _():
            for cp in fetch(e):
                cp.wait()
            wgu_bf[...] = wgu_st[...].astype(BF16)
            wd_bf[...] = wd_st[...].astype(BF16)
            nxt = nx_ref[e]

            @pl.when(nxt >= 0)
            def _():
                for cp in fetch(nxt):
                    cp.start()

    def ffn_pass(p, masked):
        rows = slice(p * TM_SUB, (p + 1) * TM_SUB)
        x = x_ref[rows, :]
        if masked:
            row_id = lax.broadcasted_iota(jnp.int32, (TM_SUB, 1), 0) + p * TM_SUB
            x = jnp.where(row_id < n_valid, x, jnp.uint32(0))
        x_lo, x_hi = _unpack_rows(x)
        gu = (_dot(x_lo.astype(BF16), wgu_bf[:HALF, :]) + _dot(x_hi.astype(BF16), wgu_bf[HALF:, :])
              + bgu_ref[...])
        gate = jnp.minimum(gu[:, :D_FF], SWIGLU_LIMIT)
        up = jnp.clip(gu[:, D_FF:], -SWIGLU_LIMIT, SWIGLU_LIMIT)
        hdn = gate * _sigmoid(SWIGLU_ALPHA * gate) * (up + 1.0)
        y_ref[rows, :] = _pack_rows(_dot(hdn.astype(BF16), wd_bf[...]) + bd_ref[...])

    @pl.when(n_valid == TM)
    def _():
        for p in range(TM // TM_SUB):
            ffn_pass(p, masked=False)

    for p in range(TM // TM_SUB):
        @pl.when(jnp.logical_and(n_valid < TM, n_valid > p * TM_SUB))
        def _():
            ffn_pass(p, masked=True)


def _moe_experts(layer, block_e, n_valid, next_e, xs, w_gu, b_gu, w_down, b_down):
    grid_spec = pltpu.PrefetchScalarGridSpec(
        num_scalar_prefetch=3,
        grid=(MOE_NBLK,),
        in_specs=[pl.BlockSpec((TM, HALF), lambda i, be, nv, nx: (i, 0)),
                  pl.BlockSpec(memory_space=pl.ANY),
                  pl.BlockSpec((None, None, 1, 2 * D_FF), lambda i, be, nv, nx: (layer, be[i], 0, 0)),
                  pl.BlockSpec(memory_space=pl.ANY),
                  pl.BlockSpec((None, None, 1, D_MODEL), lambda i, be, nv, nx: (layer, be[i], 0, 0))],
        out_specs=pl.BlockSpec((TM, HALF), lambda i, be, nv, nx: (i, 0)),
        scratch_shapes=[pltpu.VMEM((D_MODEL, 2 * D_FF), F32), pltpu.VMEM((D_FF, D_MODEL), F32),
                        pltpu.VMEM((D_MODEL, 2 * D_FF), BF16), pltpu.VMEM((D_FF, D_MODEL), BF16),
                        pltpu.SemaphoreType.DMA((2,))],
    )
    return pl.pallas_call(
        functools.partial(_moe_kernel, layer=layer),
        grid_spec=grid_spec,
        out_shape=jax.ShapeDtypeStruct((MOE_ROWS, HALF), jnp.uint32),
        compiler_params=_params(),
        name="moe_experts",
    )(block_e, n_valid, next_e, xs, w_gu, b_gu.reshape(DEPTH, N_EXPERTS, 1, -1), w_down,
      b_down.reshape(DEPTH, N_EXPERTS, 1, -1))


SC_WORKERS = 32
SC_WIN = 64


def _sc_mesh():
    return plsc.VectorSubcoreMesh(core_axis_name="core", subcore_axis_name="subcore")


def _sc_worker():
    return lax.axis_index("core") * (SC_WORKERS // 2) + lax.axis_index("subcore")


def _sc_scatter_rows(x, dest_t, n_rows):
    n, width = x.shape
    kk = dest_t.shape[0]
    per = n // SC_WORKERS
    n_win = per // SC_WIN
    assert per * SC_WORKERS == n and n_win * SC_WIN == per and n_win % 2 == 0

    @pl.kernel(out_type=jax.ShapeDtypeStruct((n_rows, width), x.dtype), mesh=_sc_mesh(),
               scratch_types=[pltpu.VMEM((kk, per), jnp.int32), pltpu.VMEM((SC_WIN, width), x.dtype),
                              pltpu.VMEM((SC_WIN, width), x.dtype), pltpu.SemaphoreType.DMA((4,))])
    def scatter(x_hbm, i_hbm, o_hbm, idx_v, buf0, buf1, sems):
        base = _sc_worker() * per
        pltpu.sync_copy(i_hbm.at[:, pl.ds(base, per)], idx_v)

        def get(j, buf, s):
            return pltpu.make_async_copy(x_hbm.at[pl.ds(base + j * SC_WIN, SC_WIN)], buf, sems.at[s])

        def put(j, q, buf, s):
            return pltpu.make_async_copy(buf, o_hbm.at[idx_v.at[q, pl.ds(j * SC_WIN, SC_WIN)]], sems.at[s])

        get(0, buf0, 0).start()

        @pl.loop(0, n_win, step=2)
        def _(j):
            get(j, buf0, 0).wait()

            @pl.when(j > 0)
            def _():
                for q in range(kk):
                    put(j - 1, q, buf1, 3).wait()

            get(j + 1, buf1, 1).start()
            for q in range(kk):
                put(j, q, buf0, 2).start()
            get(j + 1, buf1, 1).wait()
            for q in range(kk):
                put(j, q, buf0, 2).wait()

            @pl.when(j + 2 < n_win)
            def _():
                get(j + 2, buf0, 0).start()

            for q in range(kk):
                put(j + 1, q, buf1, 3).start()

        for q in range(kk):
            put(n_win - 1, q, buf1, 3).wait()

    return scatter(x, dest_t)


def _sc_gather_rows(y, idx):
    n = idx.shape[0]
    width = y.shape[1]
    per = n // SC_WORKERS
    n_win = per // SC_WIN
    assert per * SC_WORKERS == n and n_win * SC_WIN == per and n_win % 2 == 0

    @pl.kernel(out_type=jax.ShapeDtypeStruct((n, width), y.dtype), mesh=_sc_mesh(),
               scratch_types=[pltpu.VMEM((per,), jnp.int32), pltpu.VMEM((SC_WIN, width), y.dtype),
                              pltpu.VMEM((SC_WIN, width), y.dtype), pltpu.SemaphoreType.DMA((4,))])
    def gather(y_hbm, i_hbm, o_hbm, idx_v, buf0, buf1, sems):
        base = _sc_worker() * per
        pltpu.sync_copy(i_hbm.at[pl.ds(base, per)], idx_v)

        def get(j, buf, s):
            return pltpu.make_async_copy(y_hbm.at[idx_v.at[pl.ds(j * SC_WIN, SC_WIN)]], buf, sems.at[s])

        def put(j, buf, s):
            return pltpu.make_async_copy(buf, o_hbm.at[pl.ds(base + j * SC_WIN, SC_WIN)], sems.at[s])

        get(0, buf0, 0).start()

        @pl.loop(0, n_win, step=2)
        def _(j):
            get(j, buf0, 0).wait()

            @pl.when(j > 0)
            def _():
                put(j - 1, buf1, 3).wait()

            get(j + 1, buf1, 1).start()
            put(j, buf0, 2).start()
            get(j + 1, buf1, 1).wait()
            put(j, buf0, 2).wait()

            @pl.when(j + 2 < n_win)
            def _():
                get(j + 2, buf0, 0).start()

            put(j + 1, buf1, 3).start()

        put(n_win - 1, buf1, 3).wait()

    return gather(y, idx)


def _combined_rows(x1_ref, yg_refs, gt_ref, gate_ref, row):
    r_i = lax.broadcasted_iota(jnp.int32, (RB, RB), 0)
    c_i = lax.broadcasted_iota(jnp.int32, (RB, RB), 1)
    eye = jnp.where(r_i == c_i, 1.0, 0.0).astype(BF16)
    g1, g2, g3 = _split3(gt_ref[...])
    gw = _dot_nt(eye, g1) + (_dot_nt(eye, g2) + _dot_nt(eye, g3))
    acc_lo, acc_hi = None, None
    for kk in range(TOP_K):
        y_lo, y_hi = _unpack_rows(yg_refs[kk][...])
        w = gw[:, kk:kk + 1]
        acc_lo = y_lo * w if acc_lo is None else acc_lo + y_lo * w
        acc_hi = y_hi * w if acc_hi is None else acc_hi + y_hi * w
    x_lo = x1_ref[:, :HALF] + gate_ref[pl.ds(row, 1), :HALF] * acc_lo
    x_hi = x1_ref[:, HALF:] + gate_ref[pl.ds(row, 1), HALF:] * acc_hi
    return x_lo, x_hi


def _combine_specs():
    slot_spec = lambda k: pl.BlockSpec((RB, HALF), lambda i: (k * NBLK + i, 0))
    return ([pl.BlockSpec((RB, D_MODEL), lambda i: (i, 0))] + [slot_spec(k) for k in range(TOP_K)]
            + [pl.BlockSpec((8, RB), lambda i: (0, i)), pl.BlockSpec((MOD_ROWS, D_MODEL), lambda i: (0, 0))])


def _combine_final_kernel(x1_ref, *rest):
    yg_refs = rest[:TOP_K]
    gt_ref, gate_ref, fg_ref, op_ref, os_ref = rest[TOP_K:]
    i = pl.program_id(0)
    x_lo, x_hi = _combined_rows(x1_ref, yg_refs, gt_ref, gate_ref, _mod_row(i))
    ms = (jnp.sum(x_lo * x_lo, axis=-1, keepdims=True) + jnp.sum(x_hi * x_hi, axis=-1, keepdims=True)) / D_MODEL
    scale = lax.rsqrt(ms + EPS)

    @pl.when(i < N_PROMPT_BLK)
    def _():
        op_ref[:, :HALF] = x_lo * scale * fg_ref[:, :HALF]
        op_ref[:, HALF:] = x_hi * scale * fg_ref[:, HALF:]

    @pl.when(i >= N_PROMPT_BLK)
    def _():
        os_ref[:, :HALF] = x_lo * scale * fg_ref[:, :HALF]
        os_ref[:, HALF:] = x_hi * scale * fg_ref[:, HALF:]


def _combine_final(x1, yg, gates_t, gate, final_g):
    return pl.pallas_call(
        _combine_final_kernel,
        grid=(NBLK,),
        in_specs=_combine_specs() + [pl.BlockSpec((1, D_MODEL), lambda i: (0, 0))],
        out_specs=[pl.BlockSpec((RB, D_MODEL), lambda i: (jnp.minimum(i, N_PROMPT_BLK - 1), 0)),
                   pl.BlockSpec((RB, D_MODEL), lambda i: (jnp.maximum(i - N_PROMPT_BLK, 0), 0))],
        out_shape=[jax.ShapeDtypeStruct((NT_PROMPT, D_MODEL), F32),
                   jax.ShapeDtypeStruct((NT - NT_PROMPT, D_MODEL), F32)],
        compiler_params=_params(),
        name="moe_combine_final",
    )(x1, *([yg] * TOP_K), gates_t, gate, final_g.reshape(1, D_MODEL))


def _routing_plan(counts, idx_t, rank_t):
    counts = counts[:, 0]
    padded = (counts + TM - 1) // TM * TM
    pad_end = jnp.cumsum(padded)
    pad_start = pad_end - padded
    blk_row = (jnp.arange(MOE_NBLK, dtype=jnp.int32) * TM)[:, None]
    ids = jnp.arange(N_EXPERTS, dtype=jnp.int32)
    owns = jnp.logical_and(pad_start[None, :] <= blk_row, blk_row < pad_end[None, :])
    last_used = jnp.max(jnp.where(counts > 0, ids, 0))
    block_e = jnp.where(jnp.any(owns, axis=1), jnp.sum(jnp.where(owns, ids[None, :], 0), axis=1), last_used)
    block_e = block_e.astype(jnp.int32)
    left = jnp.clip(counts[None, :] - (blk_row - pad_start[None, :]), 0, TM)
    n_valid = jnp.sum(jnp.where(owns, left, 0), axis=1).astype(jnp.int32)
    start = jnp.zeros(idx_t.shape, jnp.int32)
    for e in range(N_EXPERTS):
        start = jnp.where(idx_t == e, pad_start[e], start)
    dest_t = (start + rank_t).astype(jnp.int32)
    later = jnp.where(jnp.logical_and(counts[None, :] > 0, ids[None, :] > ids[:, None]), ids[None, :], N_EXPERTS)
    next_e = jnp.min(later, axis=1)
    next_e = jnp.where(next_e == N_EXPERTS, -1, next_e).astype(jnp.int32)
    return block_e, n_valid, next_e, dest_t


def _rope_tables():
    rows = DEC_SEQ // GRID_W
    row = jnp.repeat(jnp.arange(rows, dtype=F32), GRID_W)
    col = jnp.tile(jnp.arange(GRID_W, dtype=F32), rows)
    n_f = RET_DK // 4
    freqs = ROPE_THETA ** (-jnp.arange(n_f, dtype=F32) / n_f)
    ang = jnp.concatenate([row[:, None] * freqs, col[:, None] * freqs], axis=-1)
    cos = jnp.repeat(jnp.cos(ang), 2, axis=-1)
    sin = jnp.repeat(jnp.sin(ang), 2, axis=-1) * jnp.tile(jnp.asarray([-1.0, 1.0], F32), RET_DK // 2)
    cos = jnp.concatenate([jnp.ones((RB, RET_DK), F32), cos], axis=0)
    sin = jnp.concatenate([jnp.zeros((RB, RET_DK), F32), sin], axis=0)
    return jnp.tile(cos, (1, 2)), jnp.tile(sin, (1, 2))


def kernel(x_prompt, x_sample, state_ret, state_gla, c, c_ctx, w_mod, b_mod, norm1_g, norm2_g, final_g, even_w_in, ret_decay, ret_gn, conv_w, conv_b, conv_ln_g, conv_ln_b, even_w_out, odd_w_in, gla_w_a2, gla_b_a2, gla_gn, odd_w_out, router_w, router_b, exp_w_gu, exp_b_gu, exp_w_down, exp_b_down):
    x_src = ("split", x_prompt.reshape(NT_PROMPT, D_MODEL), x_sample.reshape(NT - NT_PROMPT, D_MODEL))
    cvec = jnp.concatenate([c_ctx[None, :], c, jnp.zeros((MOD_ROWS - 1 - DEC_BATCH, D_MODEL), F32)], axis=0)
    mods = _modulation(cvec, w_mod, b_mod).reshape(DEPTH, MOD_ROWS, N_MOD, D_MODEL)
    cos_tab, sin_tab = _rope_tables()
    new_ret = jnp.zeros((BATCH,) + state_ret.shape[1:], F32)
    new_gla = jnp.zeros((BATCH,) + state_gla.shape[1:], F32)
    for l in range(DEPTH):
        mod = [mods[l, :, j, :] for j in range(N_MOD)]
        if l % 2 == 0:
            e = l // 2
            qd, vd = RET_HEADS * RET_DK, RET_HEADS * RET_DV
            x, q, k, v, g, a, ga = _inproj(x_src, norm1_g[l], mod[0], mod[1], even_w_in, e,
                                           (qd, qd, vd, vd, CONV_CH, CONV_CH))
            o_f, new_ret = _retention(ret_decay[e], q, k, v, cos_tab, sin_tab, state_ret, new_ret, e,
                                      reverse=False)
            ret, new_ret = _retention(ret_decay[e], q, k, v, cos_tab, sin_tab, state_ret, new_ret, e,
                                      reverse=True, o_fwd=o_f, g=g, gn=ret_gn[e])
            u = _conv_module(a, ga, conv_w[e], conv_b[e], conv_ln_g[e], conv_ln_b[e])
            mixes, w_out, w_index = [ret, u], even_w_out, e
        else:
            o = l // 2
            qd, vd = GLA_HEADS * GLA_DK, GLA_HEADS * GLA_DV
            x, q, k, v, r, alr = _inproj(x_src, norm1_g[l], mod[0], mod[1], odd_w_in, o,
                                         (qd, qd, vd, vd, 2 * GLA_RANK))
            zeros = jnp.zeros((GLA_RANK, qd), F32)
            wa_f = jnp.concatenate([gla_w_a2[o, 0], zeros], axis=0)
            wa_b = jnp.concatenate([zeros, gla_w_a2[o, 1]], axis=0)
            o_f, new_gla = _gla(alr, q, k, v, wa_f, gla_b_a2[o, 0].reshape(1, qd), state_gla, new_gla, o,
                                reverse=False)
            y, new_gla = _gla(alr, q, k, v, wa_b, gla_b_a2[o, 1].reshape(1, qd), state_gla, new_gla, o,
                              reverse=True, o_fwd=o_f, r=r, gn=gla_gn[o])
            mixes, w_out, w_index = [y], odd_w_out, o
        x1, h2, logits_t = _outproj(mixes, w_out, w_index, x, mod[2], norm2_g[l], mod[3], mod[4],
                                    router_w[l], router_b[l])
        idx_t, rank_t, gates_t, counts = _route(logits_t)
        block_e, n_valid, next_e, dest_t = _routing_plan(counts, idx_t, rank_t)
        xs = _sc_scatter_rows(h2, dest_t, MOE_ROWS)
        yb = _moe_experts(l, block_e, n_valid, next_e, xs, exp_w_gu, exp_b_gu, exp_w_down, exp_b_down)
        yg = _sc_gather_rows(yb, dest_t.reshape(TOP_K * NT))
        x_src = ("moe", x1, yg, gates_t, mod[5])
    y_prompt, y_sample = _combine_final(*x_src[1:], final_g)
    y_prompt = y_prompt.reshape(BATCH, SEQ, D_MODEL)
    y_sample = y_sample.reshape(DEC_BATCH, DEC_SEQ, D_MODEL)
    return (y_prompt, y_sample, new_ret, new_gla)
```

```python
import functools

import jax
import jax.numpy as jnp
from jax import lax
from jax.experimental import pallas as pl
from jax.experimental.pallas import tpu as pltpu
from jax.experimental.pallas import tpu_sc as plsc

F32 = jnp.float32
BF16 = jnp.bfloat16

D_MODEL = 1024
BATCH = 16
SEQ = 256
DEPTH = 4
DEC_BATCH = 4
DEC_SEQ = 4096
GRID_W = 64
RET_HEADS = 4
RET_DK = 64
RET_DV = 128
RET_CHUNK = 128
CONV_CH = 512
CONV_WIDTH = 31
CONV_PAD = CONV_WIDTH // 2
GLA_HEADS = 4
GLA_DK = 128
GLA_DV = 256
GLA_RANK = 16
GLA_TAU = 16.0
GLA_CHUNK = 64
GLA_SUB = 16
N_EXPERTS = 32
TOP_K = 4
D_FF = 1024
SWIGLU_LIMIT = 7.0
SWIGLU_ALPHA = 1.702
MOE_BLOCK = 128
ROPE_THETA = 10000.0
EPS = 1e-6
N_MOD = 6

RB = 256
NT_PROMPT = BATCH * SEQ
NT = NT_PROMPT + DEC_BATCH * DEC_SEQ
NBLK = NT // RB
N_PROMPT_BLK = NT_PROMPT // RB
SAMPLE_BLK = DEC_SEQ // RB
NSEQ = BATCH + DEC_BATCH
MOD_ROWS = 8
HALO = 16
VMEM_LIMIT = 48 * 1024 * 1024

assert SEQ == RB and DEC_SEQ % RB == 0 and CONV_PAD < HALO


def _seq_of_block(i):
    return jnp.where(i < N_PROMPT_BLK, i, N_PROMPT_BLK + (i - N_PROMPT_BLK) // SAMPLE_BLK)


def _is_first_block(i):
    return jnp.logical_or(i < N_PROMPT_BLK, (i - N_PROMPT_BLK) % SAMPLE_BLK == 0)


def _is_last_block(i):
    return jnp.logical_or(i < N_PROMPT_BLK, (i - N_PROMPT_BLK) % SAMPLE_BLK == SAMPLE_BLK - 1)


def _mod_row(i):
    return jnp.where(i < N_PROMPT_BLK, 0, 1 + (i - N_PROMPT_BLK) // SAMPLE_BLK)


def _rope_block(i):
    return jnp.where(i < N_PROMPT_BLK, 0, 1 + (i - N_PROMPT_BLK) % SAMPLE_BLK)


def _dot(a, b):
    return jnp.dot(a, b, preferred_element_type=F32)


def _dot_nt(a, b):
    return lax.dot_general(a, b, (((1,), (1,)), ((), ())), preferred_element_type=F32)


def _dot_tn(a, b):
    return lax.dot_general(a, b, (((0,), (0,)), ((), ())), preferred_element_type=F32)


def _split2(a):
    hi = a.astype(BF16)
    lo = (a - hi.astype(F32)).astype(BF16)
    return hi, lo


def _dot_hi(a, b):
    a_hi, a_lo = _split2(a)
    b_hi, b_lo = _split2(b)
    return _dot(a_hi, b_hi) + (_dot(a_hi, b_lo) + _dot(a_lo, b_hi))


def _silu(x):
    return x * (1.0 / (1.0 + jnp.exp(-x)))


def _sigmoid(x):
    return 1.0 / (1.0 + jnp.exp(-x))


def _pack_rows(x):
    n = x.shape[1] // 2
    lo = pltpu.bitcast(x[:, :n].astype(BF16).astype(F32), jnp.uint32)
    hi = pltpu.bitcast(x[:, n:].astype(BF16).astype(F32), jnp.uint32)
    return hi | (lo >> 16)


def _unpack_rows(u):
    lo = pltpu.bitcast(u << 16, F32)
    hi = pltpu.bitcast(u & jnp.uint32(0xFFFF0000), F32)
    return lo, hi


def _params(n_axes=1, vmem=VMEM_LIMIT):
    return pltpu.CompilerParams(dimension_semantics=("arbitrary",) * n_axes, vmem_limit_bytes=vmem)


MOD_TN = 1536


def _mod_kernel(c_ref, w_ref, b_ref, o_ref):
    s = _silu(c_ref[...]).astype(BF16)
    o_ref[...] = _dot(s, w_ref[...].astype(BF16)) + b_ref[...]


def _modulation(cvec, w_mod, b_mod):
    n = N_MOD * D_MODEL
    return pl.pallas_call(
        _mod_kernel,
        grid=(DEPTH, n // MOD_TN),
        in_specs=[pl.BlockSpec((MOD_ROWS, D_MODEL), lambda l, j: (0, 0)),
                  pl.BlockSpec((None, D_MODEL, MOD_TN), lambda l, j: (l, 0, j)),
                  pl.BlockSpec((None, 1, MOD_TN), lambda l, j: (l, 0, j))],
        out_specs=pl.BlockSpec((None, MOD_ROWS, MOD_TN), lambda l, j: (l, 0, j)),
        out_shape=jax.ShapeDtypeStruct((DEPTH, MOD_ROWS, n), F32),
        compiler_params=_params(2),
        name="modulation",
    )(cvec, w_mod, b_mod.reshape(DEPTH, 1, n))


def _norm_mod(x, g_ref, shift_ref, scale_ref, row):
    y = x * lax.rsqrt(jnp.mean(x * x, axis=-1, keepdims=True) + EPS) * g_ref[...]
    return y * (1.0 + scale_ref[pl.ds(row, 1), :]) + shift_ref[pl.ds(row, 1), :]


N_SRC = {"split": 2, "moe": 3 + TOP_K}


def _inproj_kernel(*refs, widths, source):
    n_src = N_SRC[source]
    src = refs[:n_src]
    g_ref, shift_ref, scale_ref, w_ref = refs[n_src:n_src + 4]
    outs, w_bf = refs[n_src + 4:-1], refs[-1]
    step = pl.program_id(0)

    @pl.when(step == 0)
    def _():
        w_bf[...] = w_ref[...].astype(BF16)

    row = _mod_row(step)
    if source == "moe":
        x1_ref, yg_refs, gt_ref, gate_ref = src[0], src[1:1 + TOP_K], src[-2], src[-1]
        gw = _gate_columns(gt_ref)
    if source == "split":
        x = jnp.where(step < N_PROMPT_BLK, src[0][...], src[1][...])
    else:
        x = jnp.concatenate(_combined_rows(x1_ref, yg_refs, gw, gate_ref, row), axis=1)
    outs[0][...] = x
    hb = _norm_mod(x, g_ref, shift_ref, scale_ref, row).astype(BF16)
    off = 0
    for o_ref, width in zip(outs[1:], widths):
        o_ref[...] = _dot(hb, w_bf[:, off:off + width])
        off += width


def _inproj(x, g, shift, scale, w_all, index, widths):
    source, x_args = x[0], list(x[1:])
    n_in = w_all.shape[2]
    row_spec = lambda width: pl.BlockSpec((RB, width), lambda i: (i, 0))
    full = lambda shape: pl.BlockSpec(shape, lambda i: (0,) * len(shape))
    if source == "split":
        x_specs = [pl.BlockSpec((RB, D_MODEL), lambda i: (jnp.minimum(i, N_PROMPT_BLK - 1), 0)),
                   pl.BlockSpec((RB, D_MODEL), lambda i: (jnp.maximum(i - N_PROMPT_BLK, 0), 0))]
    else:
        x1, yg, gates_t, gate = x_args
        x_specs, x_args = _combine_specs(), [x1] + [yg] * TOP_K + [gates_t, gate]
    out_widths = (D_MODEL,) + tuple(widths)
    return pl.pallas_call(
        functools.partial(_inproj_kernel, widths=widths, source=source),
        grid=(NBLK,),
        in_specs=x_specs + [full((1, D_MODEL)), full((MOD_ROWS, D_MODEL)), full((MOD_ROWS, D_MODEL)),
                            pl.BlockSpec((None, D_MODEL, n_in), lambda i: (index, 0, 0))],
        out_specs=[row_spec(width) for width in out_widths],
        out_shape=[jax.ShapeDtypeStruct((NT, width), F32) for width in out_widths],
        scratch_shapes=[pltpu.VMEM((D_MODEL, n_in), BF16)],
        compiler_params=_params(),
        name="inproj",
    )(*x_args, g.reshape(1, D_MODEL), shift, scale, w_all)


RC = RET_CHUNK
RET_PAIR = 2 * RET_DK


def _rope(x, cos, sin_signed):
    lane = lax.broadcasted_iota(jnp.int32, x.shape, 1)
    swapped = jnp.where(lane % 2 == 0, pltpu.roll(x, x.shape[1] - 1, 1), pltpu.roll(x, 1, 1))
    return x * cos + swapped * sin_signed


def _ret_kernel(decay_ref, q_ref, k_ref, v_ref, cos_ref, sin_ref, s0_ref, acc_ref, *rest, reverse):
    del acc_ref
    if reverse:
        of_ref, g_ref, gn_ref, o_ref, sfin_ref, st_ref, dm_ref, dq_ref, dk_ref, ds_ref = rest
    else:
        o_ref, sfin_ref, st_ref, dm_ref, dq_ref, dk_ref, ds_ref = rest
    step = pl.program_id(0)
    blk = NBLK - 1 - step if reverse else step
    direction = 1 if reverse else 0

    @pl.when(step == 0)
    def _():
        row = lax.broadcasted_iota(jnp.int32, (RC, RC), 0).astype(F32)
        col = lax.broadcasted_iota(jnp.int32, (RC, RC), 1).astype(F32)
        for h in range(RET_HEADS):
            lg = -jnp.exp(jnp.full((RC, RC), decay_ref[direction, h], F32))
            if reverse:
                diff = col - row
                mask = diff > 0
                q_pow = RC - row
                k_pow = row
            else:
                diff = row - col
                mask = diff >= 0
                q_pow = row + 1.0
                k_pow = RC - 1.0 - row
            dm_ref[h] = jnp.where(mask, jnp.exp(lg * jnp.where(mask, diff, 0.0)), 0.0)
            dq_ref[h] = jnp.exp(lg * q_pow)
            dk_ref[h] = jnp.exp(lg * k_pow)
            ds_ref[h] = jnp.exp(lg * RC)

    starts = _is_last_block(blk) if reverse else _is_first_block(blk)

    is_prompt = blk < N_PROMPT_BLK

    @pl.when(starts)
    def _():
        st_ref[...] = jnp.zeros_like(st_ref)

    @pl.when(jnp.logical_and(starts, jnp.logical_not(is_prompt)))
    def _():
        for h in range(RET_HEADS):
            off = (h % 2) * RET_DK
            st_ref[h, off:off + RET_DK, :] = s0_ref[h]

    lane = lax.broadcasted_iota(jnp.int32, (1, RET_PAIR), 1)
    chunks = range(RB // RC)
    chunk_order = list(reversed(chunks) if reverse else chunks)
    units = [(h, c) for h in range(RET_HEADS) for c in chunk_order]
    rows_of = lambda c: slice(c * RC, (c + 1) * RC)
    vcols = lambda h: slice(h * RET_DV, (h + 1) * RET_DV)
    roped = {}
    for p in range(RET_HEADS // 2):
        cols = slice(p * RET_PAIR, (p + 1) * RET_PAIR)
        for c in chunk_order:
            rows = rows_of(c)
            cos, sin = cos_ref[rows, :], sin_ref[rows, :]
            roped[p, c] = (_rope(q_ref[rows, cols], cos, sin),
                           _rope(k_ref[rows, cols] * (RET_DK ** -0.5), cos, sin))
    qm_u, vh_u, att_u, kd_u = {}, {}, {}, {}
    for h, c in units:
        head_mask = (lane // RET_DK == h % 2).astype(F32)
        q2, k2 = roped[h // 2, c]
        vh_u[h, c] = v_ref[rows_of(c), vcols(h)].astype(BF16)
        qm_u[h, c] = (q2 * head_mask).astype(BF16)
        km = k2 * head_mask
        att_u[h, c] = (_dot_nt(qm_u[h, c], km.astype(BF16)) * dm_ref[h]).astype(BF16)
        kd_u[h, c] = (km * dk_ref[h]).astype(BF16)
    intra_u = {u: _dot(att_u[u], vh_u[u]) for u in units}
    delta_u = {u: _dot_tn(kd_u[u], vh_u[u]) for u in units}
    state_u = {}
    for h in range(RET_HEADS):
        st = st_ref[h]
        for c in chunk_order:
            state_u[h, c] = st.astype(BF16)
            st = st * ds_ref[h] + delta_u[h, c]
        st_ref[h] = st
    for h, c in units:
        rows, out_cols = rows_of(c), vcols(h)
        o = intra_u[h, c] + _dot(qm_u[h, c], state_u[h, c]) * dq_ref[h]
        if reverse:
            o = o + of_ref[rows, out_cols]
            o = o * lax.rsqrt(jnp.mean(o * o, axis=-1, keepdims=True) + EPS)
            o = o * gn_ref[:, out_cols] * _silu(g_ref[rows, out_cols])
            o_ref[rows, out_cols] = o.astype(o_ref.dtype)
        else:
            o_ref[rows, out_cols] = o

    ends = _is_first_block(blk) if reverse else _is_last_block(blk)

    @pl.when(jnp.logical_and(ends, is_prompt))
    def _():
        for h in range(RET_HEADS):
            off = (h % 2) * RET_DK
            sfin_ref[h] = st_ref[h, off:off + RET_DK, :]


def _sample_seq(blk):
    return jnp.clip(_seq_of_block(blk) - BATCH, 0, DEC_BATCH - 1)


def _prompt_seq(blk):
    return jnp.minimum(_seq_of_block(blk), BATCH - 1)


def _retention(decay, q, k, v, cos_tab, sin_tab, state_in, state_out, layer, *, reverse,
               o_fwd=None, g=None, gn=None):
    bmap = (lambda j: NBLK - 1 - j) if reverse else (lambda j: j)
    direction = 1 if reverse else 0
    qd, vd = RET_HEADS * RET_DK, RET_HEADS * RET_DV
    row_spec = lambda width: pl.BlockSpec((RB, width), lambda j: (bmap(j), 0))
    state_blk = (None, None, None, RET_HEADS, RET_DK, RET_DV)
    in_specs = [pl.BlockSpec(memory_space=pltpu.SMEM), row_spec(qd), row_spec(qd), row_spec(vd),
                pl.BlockSpec((RB, RET_PAIR), lambda j: (_rope_block(bmap(j)), 0)),
                pl.BlockSpec((RB, RET_PAIR), lambda j: (_rope_block(bmap(j)), 0)),
                pl.BlockSpec(state_blk, lambda j: (_sample_seq(bmap(j)), layer, direction, 0, 0, 0)),
                pl.BlockSpec(memory_space=pl.ANY)]
    args = [decay, q, k, v, cos_tab, sin_tab, state_in, state_out]
    if reverse:
        in_specs += [row_spec(vd), row_spec(vd), pl.BlockSpec((1, vd), lambda j: (0, 0))]
        args += [o_fwd, g, gn.reshape(1, vd)]
    tile = pltpu.VMEM((RET_HEADS, RC, RC), F32)
    return pl.pallas_call(
        functools.partial(_ret_kernel, reverse=reverse),
        grid=(NBLK,),
        in_specs=in_specs,
        out_specs=[row_spec(vd),
                   pl.BlockSpec(state_blk, lambda j: (_prompt_seq(bmap(j)), layer, direction, 0, 0, 0))],
        out_shape=[jax.ShapeDtypeStruct((NT, vd), BF16 if reverse else F32),
                   jax.ShapeDtypeStruct(state_out.shape, F32)],
        input_output_aliases={7: 1},
        scratch_shapes=[tile, tile, tile, tile, tile],
        compiler_params=_params(),
        name="retention_bwd" if reverse else "retention_fwd",
    )(*args)


CONV_RT = 32
CONV_CT = 128
CONV_SPAN = RB + 2 * HALO - 8


def _conv_kernel(a_ref, ga_ref, ap_ref, gap_ref, an_ref, gan_ref, cw_ref, cb_ref, lng_ref, lnb_ref,
                 o_ref, u_ref, y_ref, us_ref):
    blk = pl.program_id(0)
    keep_prev = jnp.where(_is_first_block(blk), 0.0, 1.0)
    keep_next = jnp.where(_is_last_block(blk), 0.0, 1.0)
    u_ref[0:HALO, :] = ap_ref[...] * _sigmoid(gap_ref[...]) * keep_prev
    u_ref[HALO:HALO + RB, :] = a_ref[...] * _sigmoid(ga_ref[...])
    u_ref[HALO + RB:HALO + RB + HALO, :] = an_ref[...] * _sigmoid(gan_ref[...]) * keep_next
    for r in range(1, 8):
        us_ref[r - 1] = u_ref[r:r + CONV_SPAN, :]
    for ct in range(CONV_CH // CONV_CT):
        cols = slice(ct * CONV_CT, (ct + 1) * CONV_CT)
        for rt in range(RB // CONV_RT):
            acc = jnp.zeros((CONV_RT, CONV_CT), F32)
            for w in range(CONV_WIDTH):
                tiles, r = divmod(HALO - CONV_PAD + w, 8)
                base = rt * CONV_RT + 8 * tiles
                src = u_ref if r == 0 else us_ref.at[r - 1]
                acc = acc + src[base:base + CONV_RT, cols] * cw_ref[w:w + 1, cols]
            y_ref[rt * CONV_RT:(rt + 1) * CONV_RT, cols] = acc + cb_ref[:, cols]
    y = y_ref[...]
    mu = jnp.mean(y, axis=-1, keepdims=True)
    var = jnp.mean(jnp.square(y - mu), axis=-1, keepdims=True)
    o_ref[...] = _silu((y - mu) * lax.rsqrt(var + EPS) * lng_ref[...] + lnb_ref[...]).astype(o_ref.dtype)


def _conv_module(a, ga, cw, cb, lng, lnb):
    per_blk = RB // HALO
    n_halo = NT // HALO
    row_spec = pl.BlockSpec((RB, CONV_CH), lambda i: (i, 0))
    prev_spec = pl.BlockSpec((HALO, CONV_CH), lambda i: (jnp.maximum(i * per_blk - 1, 0), 0))
    next_spec = pl.BlockSpec((HALO, CONV_CH), lambda i: (jnp.minimum((i + 1) * per_blk, n_halo - 1), 0))
    vec = pl.BlockSpec((1, CONV_CH), lambda i: (0, 0))
    return pl.pallas_call(
        _conv_kernel,
        grid=(NBLK,),
        in_specs=[row_spec, row_spec, prev_spec, prev_spec, next_spec, next_spec,
                  pl.BlockSpec((CONV_WIDTH, CONV_CH), lambda i: (0, 0)), vec, vec, vec],
        out_specs=row_spec,
        out_shape=jax.ShapeDtypeStruct((NT, CONV_CH), BF16),
        scratch_shapes=[pltpu.VMEM((RB + 2 * HALO, CONV_CH), F32), pltpu.VMEM((RB, CONV_CH), F32),
                        pltpu.VMEM((7, CONV_SPAN, CONV_CH), F32)],
        compiler_params=_params(),
        name="conv_module",
    )(a, ga, a, ga, a, ga, cw, cb.reshape(1, -1), lng.reshape(1, -1), lnb.reshape(1, -1))


GC = GLA_CHUNK
GLA_NSUB = GC // GLA_SUB


def _split3(a):
    p1 = a.astype(BF16)
    r1 = a - p1.astype(F32)
    p2 = r1.astype(BF16)
    p3 = (r1 - p2.astype(F32)).astype(BF16)
    return p1, p2, p3


def _gla_kernel(alr_ref, alr_next_ref, q_ref, k_ref, v_ref, wa_ref, ba_ref, s0_ref, acc_ref, *rest, reverse):
    del acc_ref
    if reverse:
        of_ref, r_ref, gn_ref, o_ref, sfin_ref, st_ref, b2_ref, tri_ref, ob_ref = rest
    else:
        o_ref, sfin_ref, st_ref, b2_ref, tri_ref = rest
        ob_ref = o_ref
    step = pl.program_id(0)
    blk = NBLK - 1 - step if reverse else step
    starts = _is_last_block(blk) if reverse else _is_first_block(blk)
    is_prompt = blk < N_PROMPT_BLK

    @pl.when(jnp.logical_and(starts, is_prompt))
    def _():
        st_ref[...] = jnp.zeros_like(st_ref)

    @pl.when(jnp.logical_and(starts, jnp.logical_not(is_prompt)))
    def _():
        for h in range(GLA_HEADS):
            st_ref[h] = s0_ref[h].T

    def cumulative_gates(alr):
        z = _dot_hi(alr, wa_ref[...]) + ba_ref[...]
        log_a = (jnp.minimum(z, 0.0) - jnp.log(1.0 + jnp.exp(-jnp.abs(z)))) * (1.0 / GLA_TAU)
        g1, g2, g3 = _split3(log_a)
        tri = tri_ref[...]
        return _dot(tri, g1) + (_dot(tri, g2) + _dot(tri, g3))

    slot = step % 2

    @pl.when(step == 0)
    def _():
        row = lax.broadcasted_iota(jnp.int32, (RB, RB), 0)
        col = lax.broadcasted_iota(jnp.int32, (RB, RB), 1)
        ordered = col >= row if reverse else col <= row
        tri_ref[...] = jnp.where(jnp.logical_and(row // GC == col // GC, ordered), 1.0, 0.0).astype(BF16)
        b2_ref[0] = cumulative_gates(alr_ref[...])

    b2_ref[1 - slot] = cumulative_gates(alr_next_ref[...])
    b_ref = b2_ref.at[slot]

    c_row = lax.broadcasted_iota(jnp.int32, (GC, 1), 0)
    a_row = lax.broadcasted_iota(jnp.int32, (GC, GC), 0)
    a_col = lax.broadcasted_iota(jnp.int32, (GC, GC), 1)
    att_mask = a_col > a_row if reverse else a_col <= a_row
    chunks = range(RB // GC)
    chunk_order = list(reversed(chunks) if reverse else chunks)
    units = [(h, c) for h in range(GLA_HEADS) for c in chunk_order]
    kcols = lambda h: slice(h * GLA_DK, (h + 1) * GLA_DK)
    vcols = lambda h: slice(h * GLA_DV, (h + 1) * GLA_DV)
    rows_of = lambda c: slice(c * GC, (c + 1) * GC)

    vh_u, qe_u, ke_u, decay_u, att_u = {}, {}, {}, {}, {}
    for h, c in units:
        rows = rows_of(c)
        b = b_ref[rows, kcols(h)]
        qh = q_ref[rows, kcols(h)] * (GLA_DK ** -0.5)
        kh = k_ref[rows, kcols(h)]
        vh_u[h, c] = v_ref[rows, vcols(h)].astype(BF16)
        edge = b[0:1, :] if reverse else b[GC - 1:GC, :]
        bounds = []
        for s in range(GLA_NSUB):
            if reverse:
                hi = (s + 1) * GLA_SUB
                bounds.append(b[hi:hi + 1, :] if s < GLA_NSUB - 1 else jnp.zeros((1, GLA_DK), F32))
            else:
                lo = s * GLA_SUB
                bounds.append(b[lo - 1:lo, :] if s > 0 else jnp.zeros((1, GLA_DK), F32))
        own = jnp.concatenate([jnp.broadcast_to(bd, (GLA_SUB, GLA_DK)) for bd in bounds], axis=0)
        q_own = qh * jnp.exp(b - own)
        q_parts, k_parts = [], []
        for s, bd in enumerate(bounds):
            q_parts.append(jnp.where(c_row // GLA_SUB == s, q_own, 0.0))
            reach = c_row >= s * GLA_SUB if reverse else c_row < (s + 1) * GLA_SUB
            k_parts.append(kh * jnp.exp(jnp.where(reach, bd - b, -jnp.inf)))
        q_bd = jnp.concatenate(q_parts, axis=1).astype(BF16)
        k_cat = jnp.concatenate(k_parts, axis=1).astype(BF16)
        att_u[h, c] = jnp.where(att_mask, _dot_nt(q_bd, k_cat), 0.0).astype(BF16)
        qe_u[h, c] = (qh * jnp.exp(b)).astype(BF16)
        ke_u[h, c] = (kh * jnp.exp(edge - b)).astype(BF16)
        decay_u[h, c] = jnp.exp(edge)
    intra_u = {u: _dot(att_u[u], vh_u[u]) for u in units}
    delta_u = {u: _dot_tn(vh_u[u], ke_u[u]) for u in units}
    state_u = {}
    for h in range(GLA_HEADS):
        st = st_ref[h]
        for c in chunk_order:
            state_u[h, c] = st.astype(BF16)
            st = st * decay_u[h, c] + delta_u[h, c]
        st_ref[h] = st
    for h, c in units:
        ob_ref[rows_of(c), vcols(h)] = intra_u[h, c] + _dot_nt(qe_u[h, c], state_u[h, c])

    if reverse:
        for h in range(GLA_HEADS):
            cols = vcols(h)
            o = ob_ref[:, cols] + of_ref[:, cols]
            o = o * lax.rsqrt(jnp.mean(o * o, axis=-1, keepdims=True) + EPS)
            o_ref[:, cols] = (o * gn_ref[:, cols] * _silu(r_ref[:, cols])).astype(o_ref.dtype)

    ends = _is_first_block(blk) if reverse else _is_last_block(blk)

    @pl.when(jnp.logical_and(ends, is_prompt))
    def _():
        for h in range(GLA_HEADS):
            sfin_ref[h] = st_ref[h].T


def _gla(alr, q, k, v, wa, ba, state_in, state_out, layer, *, reverse, o_fwd=None, r=None, gn=None):
    bmap = (lambda j: NBLK - 1 - j) if reverse else (lambda j: j)
    direction = 1 if reverse else 0
    qd, vd = GLA_HEADS * GLA_DK, GLA_HEADS * GLA_DV
    row_spec = lambda width: pl.BlockSpec((RB, width), lambda j: (bmap(j), 0))
    state_blk = (None, None, None, GLA_HEADS, GLA_DK, GLA_DV)
    next_spec = pl.BlockSpec((RB, 2 * GLA_RANK), lambda j: (bmap(jnp.minimum(j + 1, NBLK - 1)), 0))
    in_specs = [row_spec(2 * GLA_RANK), next_spec, row_spec(qd), row_spec(qd), row_spec(vd),
                pl.BlockSpec((2 * GLA_RANK, qd), lambda j: (0, 0)),
                pl.BlockSpec((1, qd), lambda j: (0, 0)),
                pl.BlockSpec(state_blk, lambda j: (_sample_seq(bmap(j)), layer, direction, 0, 0, 0)),
                pl.BlockSpec(memory_space=pl.ANY)]
    args = [alr, alr, q, k, v, wa, ba, state_in, state_out]
    scratch = [pltpu.VMEM((GLA_HEADS, GLA_DV, GLA_DK), F32), pltpu.VMEM((2, RB, qd), F32),
               pltpu.VMEM((RB, RB), BF16)]
    if reverse:
        in_specs += [row_spec(vd), row_spec(vd), pl.BlockSpec((1, vd), lambda j: (0, 0))]
        args += [o_fwd, r, gn.reshape(1, vd)]
        scratch += [pltpu.VMEM((RB, vd), F32)]
    return pl.pallas_call(
        functools.partial(_gla_kernel, reverse=reverse),
        grid=(NBLK,),
        in_specs=in_specs,
        out_specs=[row_spec(vd),
                   pl.BlockSpec(state_blk, lambda j: (_prompt_seq(bmap(j)), layer, direction, 0, 0, 0))],
        out_shape=[jax.ShapeDtypeStruct((NT, vd), BF16 if reverse else F32),
                   jax.ShapeDtypeStruct(state_out.shape, F32)],
        input_output_aliases={8: 1},
        scratch_shapes=scratch,
        compiler_params=_params(),
        name="gla_bwd" if reverse else "gla_fwd",
    )(*args)


OUT_SUB = 128


def _outproj_kernel(*refs, n_mix):
    mix_refs = refs[:n_mix]
    (w_ref, x_ref, gate_ref, g2_ref, shift_ref, scale_ref, rw_ref, rb_ref,
     x1_ref, h2_ref, logit_ref, w_bf) = refs[n_mix:]
    step = pl.program_id(0)

    @pl.when(step == 0)
    def _():
        w_bf[...] = w_ref[...].astype(BF16)

    row = _mod_row(step)
    r_hi, r_lo = _split2(rw_ref[...])
    groups = [slice(p * OUT_SUB, (p + 1) * OUT_SUB) for p in range(RB // OUT_SUB)]
    mixed = []
    for rows in groups:
        m, off = None, 0
        for mix_ref in mix_refs:
            width = mix_ref.shape[1]
            part = _dot(mix_ref[rows, :], w_bf[off:off + width, :])
            m = part if m is None else m + part
            off += width
        mixed.append(m)
    normed = []
    for rows, m in zip(groups, mixed):
        x1 = x_ref[rows, :] + gate_ref[pl.ds(row, 1), :] * m
        x1_ref[rows, :] = x1
        h2 = _norm_mod(x1, g2_ref, shift_ref, scale_ref, row)
        h2_ref[rows, :] = _pack_rows(h2)
        normed.append(_split2(h2))
    for rows, (h_hi, h_lo) in zip(groups, normed):
        logit_ref[:, rows] = _dot_nt(r_hi, h_hi) + (_dot_nt(r_hi, h_lo) + _dot_nt(r_lo, h_hi)) + rb_ref[...]


def _outproj(mixes, w_all, index, x, gate, g2, shift, scale, rw, rb):
    n_mix = len(mixes)
    row_spec = lambda width: pl.BlockSpec((RB, width), lambda i: (i, 0))
    full = lambda shape: pl.BlockSpec(shape, lambda i: (0,) * len(shape))
    mod_spec = full((MOD_ROWS, D_MODEL))
    n_mixed = w_all.shape[1]
    return pl.pallas_call(
        functools.partial(_outproj_kernel, n_mix=n_mix),
        grid=(NBLK,),
        in_specs=[row_spec(m.shape[1]) for m in mixes]
        + [pl.BlockSpec((None, n_mixed, D_MODEL), lambda i: (index, 0, 0)),
           row_spec(D_MODEL), mod_spec, full((1, D_MODEL)), mod_spec, mod_spec,
           full((N_EXPERTS, D_MODEL)), full((N_EXPERTS, 1))],
        out_specs=[row_spec(D_MODEL), row_spec(D_MODEL // 2), pl.BlockSpec((N_EXPERTS, RB), lambda i: (0, i))],
        out_shape=[jax.ShapeDtypeStruct((NT, D_MODEL), F32), jax.ShapeDtypeStruct((NT, D_MODEL // 2), jnp.uint32),
                   jax.ShapeDtypeStruct((N_EXPERTS, NT), F32)],
        scratch_shapes=[pltpu.VMEM((n_mixed, D_MODEL), BF16)],
        compiler_params=_params(),
        name="outproj",
    )(*mixes, w_all, x, gate, g2.reshape(1, D_MODEL), shift, scale, rw.T, rb.reshape(N_EXPERTS, 1))


ROUTE_BLK = 2048
ROUTE_SUB = 256


def _route_kernel(lg_ref, idx_ref, rank_ref, gt_ref, cnt_ref, carry_ref):
    @pl.when(pl.program_id(0) == 0)
    def _():
        carry_ref[...] = jnp.zeros_like(carry_ref)

    logits = lg_ref[...]
    eid = lax.broadcasted_iota(jnp.int32, logits.shape, 0).astype(F32)
    work = logits
    onehots, top_vals = [], []
    for kk in range(TOP_K):
        top = jnp.max(work, axis=0, keepdims=True)
        first = jnp.min(jnp.where(work == top, eid, float(N_EXPERTS)), axis=0, keepdims=True)
        onehot = eid == first
        idx_ref[kk:kk + 1, :] = first.astype(jnp.int32)
        onehots.append(onehot)
        top_vals.append(top)
        work = jnp.where(onehot, -jnp.inf, work)
    exps = [jnp.exp(v - top_vals[0]) for v in top_vals]
    denom = exps[0]
    for e in exps[1:]:
        denom = denom + e
    gt_ref[...] = jnp.zeros_like(gt_ref)
    for kk in range(TOP_K):
        gt_ref[kk:kk + 1, :] = exps[kk] / denom

    sel = jnp.zeros(logits.shape, F32)
    for onehot in onehots:
        sel = sel + jnp.where(onehot, 1.0, 0.0)
    sel = sel.astype(BF16)
    r_i = lax.broadcasted_iota(jnp.int32, (ROUTE_SUB, ROUTE_SUB), 0)
    c_i = lax.broadcasted_iota(jnp.int32, (ROUTE_SUB, ROUTE_SUB), 1)
    before = jnp.where(r_i < c_i, 1.0, 0.0).astype(BF16)
    ones = jnp.ones((ROUTE_SUB, ROUTE_SUB), BF16)
    carry = carry_ref[...]
    for s in range(ROUTE_BLK // ROUTE_SUB):
        cols = slice(s * ROUTE_SUB, (s + 1) * ROUTE_SUB)
        pos = _dot(sel[:, cols], before) + carry
        for kk in range(TOP_K):
            rank = jnp.sum(jnp.where(onehots[kk][:, cols], pos, 0.0), axis=0, keepdims=True)
            rank_ref[kk:kk + 1, cols] = rank.astype(jnp.int32)
        carry = carry + _dot(sel[:, cols], ones)
    carry_ref[...] = carry
    cnt_ref[...] = carry.astype(jnp.int32)


def _route(logits_t):
    col_spec = lambda rows: pl.BlockSpec((rows, ROUTE_BLK), lambda i: (0, i))
    return pl.pallas_call(
        _route_kernel,
        grid=(NT // ROUTE_BLK,),
        in_specs=[col_spec(N_EXPERTS)],
        out_specs=[col_spec(TOP_K), col_spec(TOP_K), col_spec(8),
                   pl.BlockSpec((N_EXPERTS, ROUTE_SUB), lambda i: (0, 0))],
        out_shape=[jax.ShapeDtypeStruct((TOP_K, NT), jnp.int32), jax.ShapeDtypeStruct((TOP_K, NT), jnp.int32),
                   jax.ShapeDtypeStruct((8, NT), F32), jax.ShapeDtypeStruct((N_EXPERTS, ROUTE_SUB), jnp.int32)],
        scratch_shapes=[pltpu.VMEM((N_EXPERTS, ROUTE_SUB), F32)],
        compiler_params=_params(),
        name="route",
    )(logits_t)


TM = 512
TM_SUB = 256
MOE_NBLK = NT * TOP_K // TM + N_EXPERTS
MOE_ROWS = MOE_NBLK * TM
HALF = D_MODEL // 2


def _moe_kernel(be_ref, nv_ref, nx_ref, x_ref, wgu_hbm, bgu_ref, wd_hbm, bd_ref, y_ref,
                wgu_st, wd_st, wgu_bf, wd_bf, sems, *, layer):
    i = pl.program_id(0)
    n_valid = nv_ref[i]

    def fetch(e):
        return (pltpu.make_async_copy(wgu_hbm.at[layer, e], wgu_st, sems.at[0]),
                pltpu.make_async_copy(wd_hbm.at[layer, e], wd_st, sems.at[1]))

    @pl.when(i == 0)
    def _():
        for cp in fetch(be_ref[0]):
            cp.start()

    @pl.when(n_valid > 0)
    def _():
        e = be_ref[i]
        changed = jnp.logical_or(i == 0, e != be_ref[jnp.maximum(i - 1, 0)])

        @pl.when(changed)
        def _():
            for cp in fetch(e):
                cp.wait()
            wgu_bf[...] = wgu_st[...].astype(BF16)
            wd_bf[...] = wd_st[...].astype(BF16)
            nxt = nx_ref[e]

            @pl.when(nxt >= 0)
            def _():
                for cp in fetch(nxt):
                    cp.start()

    for p in range(TM // TM_SUB):
        @pl.when(n_valid > p * TM_SUB)
        def _():
            rows = slice(p * TM_SUB, (p + 1) * TM_SUB)
            row_id = lax.broadcasted_iota(jnp.int32, (TM_SUB, 1), 0) + p * TM_SUB
            x_lo, x_hi = _unpack_rows(jnp.where(row_id < n_valid, x_ref[rows, :], jnp.uint32(0)))
            gu = (_dot(x_lo.astype(BF16), wgu_bf[:HALF, :]) + _dot(x_hi.astype(BF16), wgu_bf[HALF:, :])
                  + bgu_ref[...])
            gate = jnp.minimum(gu[:, :D_FF], SWIGLU_LIMIT)
            up = jnp.clip(gu[:, D_FF:], -SWIGLU_LIMIT, SWIGLU_LIMIT)
            hdn = gate * _sigmoid(SWIGLU_ALPHA * gate) * (up + 1.0)
            y_ref[rows, :] = _pack_rows(_dot(hdn.astype(BF16), wd_bf[...]) + bd_ref[...])


def _moe_experts(layer, block_e, n_valid, next_e, xs, w_gu, b_gu, w_down, b_down):
    grid_spec = pltpu.PrefetchScalarGridSpec(
        num_scalar_prefetch=3,
        grid=(MOE_NBLK,),
        in_specs=[pl.BlockSpec((TM, HALF), lambda i, be, nv, nx: (i, 0)),
                  pl.BlockSpec(memory_space=pl.ANY),
                  pl.BlockSpec((None, None, 1, 2 * D_FF), lambda i, be, nv, nx: (layer, be[i], 0, 0)),
                  pl.BlockSpec(memory_space=pl.ANY),
                  pl.BlockSpec((None, None, 1, D_MODEL), lambda i, be, nv, nx: (layer, be[i], 0, 0))],
        out_specs=pl.BlockSpec((TM, HALF), lambda i, be, nv, nx: (i, 0)),
        scratch_shapes=[pltpu.VMEM((D_MODEL, 2 * D_FF), F32), pltpu.VMEM((D_FF, D_MODEL), F32),
                        pltpu.VMEM((D_MODEL, 2 * D_FF), BF16), pltpu.VMEM((D_FF, D_MODEL), BF16),
                        pltpu.SemaphoreType.DMA((2,))],
    )
    return pl.pallas_call(
        functools.partial(_moe_kernel, layer=layer),
        grid_spec=grid_spec,
        out_shape=jax.ShapeDtypeStruct((MOE_ROWS, HALF), jnp.uint32),
        compiler_params=_params(),
        name="moe_experts",
    )(block_e, n_valid, next_e, xs, w_gu, b_gu.reshape(DEPTH, N_EXPERTS, 1, -1), w_down,
      b_down.reshape(DEPTH, N_EXPERTS, 1, -1))


SC_WORKERS = 32
SC_WIN = 64


def _sc_mesh():
    return plsc.VectorSubcoreMesh(core_axis_name="core", subcore_axis_name="subcore")


def _sc_worker():
    return lax.axis_index("core") * (SC_WORKERS // 2) + lax.axis_index("subcore")


def _sc_scatter_rows(x, dest_t, n_rows):
    n, width = x.shape
    kk = dest_t.shape[0]
    per = n // SC_WORKERS
    n_win = per // SC_WIN
    assert per * SC_WORKERS == n and n_win * SC_WIN == per and n_win % 2 == 0

    @pl.kernel(out_type=jax.ShapeDtypeStruct((n_rows, width), x.dtype), mesh=_sc_mesh(),
               scratch_types=[pltpu.VMEM((kk, per), jnp.int32), pltpu.VMEM((SC_WIN, width), x.dtype),
                              pltpu.VMEM((SC_WIN, width), x.dtype), pltpu.SemaphoreType.DMA((4,))])
    def scatter(x_hbm, i_hbm, o_hbm, idx_v, buf0, buf1, sems):
        base = _sc_worker() * per
        pltpu.sync_copy(i_hbm.at[:, pl.ds(base, per)], idx_v)

        def get(j, buf, s):
            return pltpu.make_async_copy(x_hbm.at[pl.ds(base + j * SC_WIN, SC_WIN)], buf, sems.at[s])

        def put(j, q, buf, s):
            return pltpu.make_async_copy(buf, o_hbm.at[idx_v.at[q, pl.ds(j * SC_WIN, SC_WIN)]], sems.at[s])

        get(0, buf0, 0).start()

        @pl.loop(0, n_win, step=2)
        def _(j):
            get(j, buf0, 0).wait()

            @pl.when(j > 0)
            def _():
                for q in range(kk):
                    put(j - 1, q, buf1, 3).wait()

            get(j + 1, buf1, 1).start()
            for q in range(kk):
                put(j, q, buf0, 2).start()
            get(j + 1, buf1, 1).wait()
            for q in range(kk):
                put(j, q, buf0, 2).wait()

            @pl.when(j + 2 < n_win)
            def _():
                get(j + 2, buf0, 0).start()

            for q in range(kk):
                put(j + 1, q, buf1, 3).start()

        for q in range(kk):
            put(n_win - 1, q, buf1, 3).wait()

    return scatter(x, dest_t)


def _sc_gather_rows(y, idx):
    n = idx.shape[0]
    width = y.shape[1]
    per = n // SC_WORKERS
    n_win = per // SC_WIN
    assert per * SC_WORKERS == n and n_win * SC_WIN == per and n_win % 2 == 0

    @pl.kernel(out_type=jax.ShapeDtypeStruct((n, width), y.dtype), mesh=_sc_mesh(),
               scratch_types=[pltpu.VMEM((per,), jnp.int32), pltpu.VMEM((SC_WIN, width), y.dtype),
                              pltpu.VMEM((SC_WIN, width), y.dtype), pltpu.SemaphoreType.DMA((4,))])
    def gather(y_hbm, i_hbm, o_hbm, idx_v, buf0, buf1, sems):
        base = _sc_worker() * per
        pltpu.sync_copy(i_hbm.at[pl.ds(base, per)], idx_v)

        def get(j, buf, s):
            return pltpu.make_async_copy(y_hbm.at[idx_v.at[pl.ds(j * SC_WIN, SC_WIN)]], buf, sems.at[s])

        def put(j, buf, s):
            return pltpu.make_async_copy(buf, o_hbm.at[pl.ds(base + j * SC_WIN, SC_WIN)], sems.at[s])

        get(0, buf0, 0).start()

        @pl.loop(0, n_win, step=2)
        def _(j):
            get(j, buf0, 0).wait()

            @pl.when(j > 0)
            def _():
                put(j - 1, buf1, 3).wait()

            get(j + 1, buf1, 1).start()
            put(j, buf0, 2).start()
            get(j + 1, buf1, 1).wait()
            put(j, buf0, 2).wait()

            @pl.when(j + 2 < n_win)
            def _():
                get(j + 2, buf0, 0).start()

            put(j + 1, buf1, 3).start()

        put(n_win - 1, buf1, 3).wait()

    return gather(y, idx)


def _gate_columns(gt_ref):
    r_i = lax.broadcasted_iota(jnp.int32, (RB, RB), 0)
    c_i = lax.broadcasted_iota(jnp.int32, (RB, RB), 1)
    eye = jnp.where(r_i == c_i, 1.0, 0.0).astype(BF16)
    g1, g2, g3 = _split3(gt_ref[...])
    return _dot_nt(eye, g1) + (_dot_nt(eye, g2) + _dot_nt(eye, g3))


def _combined_rows(x1_ref, yg_refs, gw, gate_ref, row, rows=slice(None)):
    acc_lo, acc_hi = None, None
    for kk in range(TOP_K):
        y_lo, y_hi = _unpack_rows(yg_refs[kk][rows, :])
        w = gw[rows, kk:kk + 1]
        acc_lo = y_lo * w if acc_lo is None else acc_lo + y_lo * w
        acc_hi = y_hi * w if acc_hi is None else acc_hi + y_hi * w
    x_lo = x1_ref[rows, :HALF] + gate_ref[pl.ds(row, 1), :HALF] * acc_lo
    x_hi = x1_ref[rows, HALF:] + gate_ref[pl.ds(row, 1), HALF:] * acc_hi
    return x_lo, x_hi


def _combine_specs():
    slot_spec = lambda k: pl.BlockSpec((RB, HALF), lambda i: (k * NBLK + i, 0))
    return ([pl.BlockSpec((RB, D_MODEL), lambda i: (i, 0))] + [slot_spec(k) for k in range(TOP_K)]
            + [pl.BlockSpec((8, RB), lambda i: (0, i)), pl.BlockSpec((MOD_ROWS, D_MODEL), lambda i: (0, 0))])


def _combine_final_kernel(x1_ref, *rest):
    yg_refs = rest[:TOP_K]
    gt_ref, gate_ref, fg_ref, op_ref, os_ref = rest[TOP_K:]
    i = pl.program_id(0)
    x_lo, x_hi = _combined_rows(x1_ref, yg_refs, _gate_columns(gt_ref), gate_ref, _mod_row(i))
    ms = (jnp.sum(x_lo * x_lo, axis=-1, keepdims=True) + jnp.sum(x_hi * x_hi, axis=-1, keepdims=True)) / D_MODEL
    scale = lax.rsqrt(ms + EPS)

    @pl.when(i < N_PROMPT_BLK)
    def _():
        op_ref[:, :HALF] = x_lo * scale * fg_ref[:, :HALF]
        op_ref[:, HALF:] = x_hi * scale * fg_ref[:, HALF:]

    @pl.when(i >= N_PROMPT_BLK)
    def _():
        os_ref[:, :HALF] = x_lo * scale * fg_ref[:, :HALF]
        os_ref[:, HALF:] = x_hi * scale * fg_ref[:, HALF:]


def _combine_final(x1, yg, gates_t, gate, final_g):
    return pl.pallas_call(
        _combine_final_kernel,
        grid=(NBLK,),
        in_specs=_combine_specs() + [pl.BlockSpec((1, D_MODEL), lambda i: (0, 0))],
        out_specs=[pl.BlockSpec((RB, D_MODEL), lambda i: (jnp.minimum(i, N_PROMPT_BLK - 1), 0)),
                   pl.BlockSpec((RB, D_MODEL), lambda i: (jnp.maximum(i - N_PROMPT_BLK, 0), 0))],
        out_shape=[jax.ShapeDtypeStruct((NT_PROMPT, D_MODEL), F32),
                   jax.ShapeDtypeStruct((NT - NT_PROMPT, D_MODEL), F32)],
        compiler_params=_params(),
        name="moe_combine_final",
    )(x1, *([yg] * TOP_K), gates_t, gate, final_g.reshape(1, D_MODEL))


def _routing_plan(counts, idx_t, rank_t):
    counts = counts[:, 0]
    padded = (counts + TM - 1) // TM * TM
    pad_end = jnp.cumsum(padded)
    pad_start = pad_end - padded
    blk_row = (jnp.arange(MOE_NBLK, dtype=jnp.int32) * TM)[:, None]
    ids = jnp.arange(N_EXPERTS, dtype=jnp.int32)
    owns = jnp.logical_and(pad_start[None, :] <= blk_row, blk_row < pad_end[None, :])
    last_used = jnp.max(jnp.where(counts > 0, ids, 0))
    block_e = jnp.where(jnp.any(owns, axis=1), jnp.sum(jnp.where(owns, ids[None, :], 0), axis=1), last_used)
    block_e = block_e.astype(jnp.int32)
    left = jnp.clip(counts[None, :] - (blk_row - pad_start[None, :]), 0, TM)
    n_valid = jnp.sum(jnp.where(owns, left, 0), axis=1).astype(jnp.int32)
    start = jnp.zeros(idx_t.shape, jnp.int32)
    for e in range(N_EXPERTS):
        start = jnp.where(idx_t == e, pad_start[e], start)
    dest_t = (start + rank_t).astype(jnp.int32)
    later = jnp.where(jnp.logical_and(counts[None, :] > 0, ids[None, :] > ids[:, None]), ids[None, :], N_EXPERTS)
    next_e = jnp.min(later, axis=1)
    next_e = jnp.where(next_e == N_EXPERTS, -1, next_e).astype(jnp.int32)
    return block_e, n_valid, next_e, dest_t


def _rope_tables():
    rows = DEC_SEQ // GRID_W
    row = jnp.repeat(jnp.arange(rows, dtype=F32), GRID_W)
    col = jnp.tile(jnp.arange(GRID_W, dtype=F32), rows)
    n_f = RET_DK // 4
    freqs = ROPE_THETA ** (-jnp.arange(n_f, dtype=F32) / n_f)
    ang = jnp.concatenate([row[:, None] * freqs, col[:, None] * freqs], axis=-1)
    cos = jnp.repeat(jnp.cos(ang), 2, axis=-1)
    sin = jnp.repeat(jnp.sin(ang), 2, axis=-1) * jnp.tile(jnp.asarray([-1.0, 1.0], F32), RET_DK // 2)
    cos = jnp.concatenate([jnp.ones((RB, RET_DK), F32), cos], axis=0)
    sin = jnp.concatenate([jnp.zeros((RB, RET_DK), F32), sin], axis=0)
    return jnp.tile(cos, (1, 2)), jnp.tile(sin, (1, 2))


def kernel(x_prompt, x_sample, state_ret, state_gla, c, c_ctx, w_mod, b_mod, norm1_g, norm2_g, final_g, even_w_in, ret_decay, ret_gn, conv_w, conv_b, conv_ln_g, conv_ln_b, even_w_out, odd_w_in, gla_w_a2, gla_b_a2, gla_gn, odd_w_out, router_w, router_b, exp_w_gu, exp_b_gu, exp_w_down, exp_b_down):
    x_src = ("split", x_prompt.reshape(NT_PROMPT, D_MODEL), x_sample.reshape(NT - NT_PROMPT, D_MODEL))
    cvec = jnp.concatenate([c_ctx[None, :], c, jnp.zeros((MOD_ROWS - 1 - DEC_BATCH, D_MODEL), F32)], axis=0)
    mods = _modulation(cvec, w_mod, b_mod).reshape(DEPTH, MOD_ROWS, N_MOD, D_MODEL)
    cos_tab, sin_tab = _rope_tables()
    new_ret = jnp.zeros((BATCH,) + state_ret.shape[1:], F32)
    new_gla = jnp.zeros((BATCH,) + state_gla.shape[1:], F32)
    for l in range(DEPTH):
        mod = [mods[l, :, j, :] for j in range(N_MOD)]
        if l % 2 == 0:
            e = l // 2
            qd, vd = RET_HEADS * RET_DK, RET_HEADS * RET_DV
            x, q, k, v, g, a, ga = _inproj(x_src, norm1_g[l], mod[0], mod[1], even_w_in, e,
                                           (qd, qd, vd, vd, CONV_CH, CONV_CH))
            o_f, new_ret = _retention(ret_decay[e], q, k, v, cos_tab, sin_tab, state_ret, new_ret, e,
                                      reverse=False)
            ret, new_ret = _retention(ret_decay[e], q, k, v, cos_tab, sin_tab, state_ret, new_ret, e,
                                      reverse=True, o_fwd=o_f, g=g, gn=ret_gn[e])
            u = _conv_module(a, ga, conv_w[e], conv_b[e], conv_ln_g[e], conv_ln_b[e])
            mixes, w_out, w_index = [ret, u], even_w_out, e
        else:
            o = l // 2
            qd, vd = GLA_HEADS * GLA_DK, GLA_HEADS * GLA_DV
            x, q, k, v, r, alr = _inproj(x_src, norm1_g[l], mod[0], mod[1], odd_w_in, o,
                                         (qd, qd, vd, vd, 2 * GLA_RANK))
            zeros = jnp.zeros((GLA_RANK, qd), F32)
            wa_f = jnp.concatenate([gla_w_a2[o, 0], zeros], axis=0)
            wa_b = jnp.concatenate([zeros, gla_w_a2[o, 1]], axis=0)
            o_f, new_gla = _gla(alr, q, k, v, wa_f, gla_b_a2[o, 0].reshape(1, qd), state_gla, new_gla, o,
                                reverse=False)
            y, new_gla = _gla(alr, q, k, v, wa_b, gla_b_a2[o, 1].reshape(1, qd), state_gla, new_gla, o,
                              reverse=True, o_fwd=o_f, r=r, gn=gla_gn[o])
            mixes, w_out, w_index = [y], odd_w_out, o
        x1, h2, logits_t = _outproj(mixes, w_out, w_index, x, mod[2], norm2_g[l], mod[3], mod[4],
                                    router_w[l], router_b[l])
        idx_t, rank_t, gates_t, counts = _route(logits_t)
        block_e, n_valid, next_e, dest_t = _routing_plan(counts, idx_t, rank_t)
        xs = _sc_scatter_rows(h2, dest_t, MOE_ROWS)
        yb = _moe_experts(l, block_e, n_valid, next_e, xs, exp_w_gu, exp_b_gu, exp_w_down, exp_b_down)
        yg = _sc_gather_rows(yb, dest_t.reshape(TOP_K * NT))
        x_src = ("moe", x1, yg, gates_t, mod[5])
    y_prompt, y_sample = _combine_final(*x_src[1:], final_g)
    y_prompt = y_prompt.reshape(BATCH, SEQ, D_MODEL)
    y_sample = y_sample.reshape(DEC_BATCH, DEC_SEQ, D_MODEL)
    return (y_prompt, y_sample, new_ret, new_gla)
```

```python
import functools

import jax
import jax.numpy as jnp
from jax import lax
from jax.experimental import pallas as pl
from jax.experimental.pallas import tpu as pltpu
from jax.experimental.pallas import tpu_sc as plsc

F32 = jnp.float32
BF16 = jnp.bfloat16

D_MODEL = 1024
BATCH = 16
SEQ = 256
DEPTH = 4
DEC_BATCH = 4
DEC_SEQ = 4096
GRID_W = 64
RET_HEADS = 4
RET_DK = 64
RET_DV = 128
RET_CHUNK = 128
CONV_CH = 512
CONV_WIDTH = 31
CONV_PAD = CONV_WIDTH // 2
GLA_HEADS = 4
GLA_DK = 128
GLA_DV = 256
GLA_RANK = 16
GLA_TAU = 16.0
GLA_CHUNK = 64
GLA_SUB = 16
N_EXPERTS = 32
TOP_K = 4
D_FF = 1024
SWIGLU_LIMIT = 7.0
SWIGLU_ALPHA = 1.702
MOE_BLOCK = 128
ROPE_THETA = 10000.0
EPS = 1e-6
N_MOD = 6

RB = 256
NT_PROMPT = BATCH * SEQ
NT = NT_PROMPT + DEC_BATCH * DEC_SEQ
NBLK = NT // RB
N_PROMPT_BLK = NT_PROMPT // RB
SAMPLE_BLK = DEC_SEQ // RB
NSEQ = BATCH + DEC_BATCH
MOD_ROWS = 8
HALO = 16
VMEM_LIMIT = 48 * 1024 * 1024

assert SEQ == RB and DEC_SEQ % RB == 0 and CONV_PAD < HALO


def _seq_of_block(i):
    return jnp.where(i < N_PROMPT_BLK, i, N_PROMPT_BLK + (i - N_PROMPT_BLK) // SAMPLE_BLK)


def _is_first_block(i):
    return jnp.logical_or(i < N_PROMPT_BLK, (i - N_PROMPT_BLK) % SAMPLE_BLK == 0)


def _is_last_block(i):
    return jnp.logical_or(i < N_PROMPT_BLK, (i - N_PROMPT_BLK) % SAMPLE_BLK == SAMPLE_BLK - 1)


def _mod_row(i):
    return jnp.where(i < N_PROMPT_BLK, 0, 1 + (i - N_PROMPT_BLK) // SAMPLE_BLK)


def _rope_block(i):
    return jnp.where(i < N_PROMPT_BLK, 0, 1 + (i - N_PROMPT_BLK) % SAMPLE_BLK)


def _dot(a, b):
    return jnp.dot(a, b, preferred_element_type=F32)


def _dot_nt(a, b):
    return lax.dot_general(a, b, (((1,), (1,)), ((), ())), preferred_element_type=F32)


def _dot_tn(a, b):
    return lax.dot_general(a, b, (((0,), (0,)), ((), ())), preferred_element_type=F32)


def _split2(a):
    hi = a.astype(BF16)
    lo = (a - hi.astype(F32)).astype(BF16)
    return hi, lo


def _dot_hi(a, b):
    a_hi, a_lo = _split2(a)
    b_hi, b_lo = _split2(b)
    return _dot(a_hi, b_hi) + (_dot(a_hi, b_lo) + _dot(a_lo, b_hi))


def _silu(x):
    return x * (1.0 / (1.0 + jnp.exp(-x)))


def _sigmoid(x):
    return 1.0 / (1.0 + jnp.exp(-x))


def _pack_rows(x):
    n = x.shape[1] // 2
    lo = pltpu.bitcast(x[:, :n].astype(BF16).astype(F32), jnp.uint32)
    hi = pltpu.bitcast(x[:, n:].astype(BF16).astype(F32), jnp.uint32)
    return hi | (lo >> 16)


def _unpack_rows(u):
    lo = pltpu.bitcast(u << 16, F32)
    hi = pltpu.bitcast(u & jnp.uint32(0xFFFF0000), F32)
    return lo, hi


def _params(n_axes=1, vmem=VMEM_LIMIT):
    return pltpu.CompilerParams(dimension_semantics=("arbitrary",) * n_axes, vmem_limit_bytes=vmem)


MOD_TN = 1536


def _mod_kernel(c_ref, w_ref, b_ref, o_ref):
    s = _silu(c_ref[...]).astype(BF16)
    o_ref[...] = _dot(s, w_ref[...].astype(BF16)) + b_ref[...]


def _modulation(cvec, w_mod, b_mod):
    n = N_MOD * D_MODEL
    return pl.pallas_call(
        _mod_kernel,
        grid=(DEPTH, n // MOD_TN),
        in_specs=[pl.BlockSpec((MOD_ROWS, D_MODEL), lambda l, j: (0, 0)),
                  pl.BlockSpec((None, D_MODEL, MOD_TN), lambda l, j: (l, 0, j)),
                  pl.BlockSpec((None, 1, MOD_TN), lambda l, j: (l, 0, j))],
        out_specs=pl.BlockSpec((None, MOD_ROWS, MOD_TN), lambda l, j: (l, 0, j)),
        out_shape=jax.ShapeDtypeStruct((DEPTH, MOD_ROWS, n), F32),
        compiler_params=_params(2),
        name="modulation",
    )(cvec, w_mod, b_mod.reshape(DEPTH, 1, n))


def _norm_mod(x, g_ref, shift_ref, scale_ref, row):
    y = x * lax.rsqrt(jnp.mean(x * x, axis=-1, keepdims=True) + EPS) * g_ref[...]
    return y * (1.0 + scale_ref[pl.ds(row, 1), :]) + shift_ref[pl.ds(row, 1), :]


N_SRC = {"split": 2, "moe": 3 + TOP_K}


def _inproj_kernel(*refs, widths, source):
    n_src = N_SRC[source]
    src = refs[:n_src]
    g_ref, shift_ref, scale_ref, w_ref = refs[n_src:n_src + 4]
    outs, w_bf = refs[n_src + 4:-1], refs[-1]
    step = pl.program_id(0)

    @pl.when(step == 0)
    def _():
        w_bf[...] = w_ref[...].astype(BF16)

    row = _mod_row(step)
    if source == "moe":
        x1_ref, yg_refs, gt_ref, gate_ref = src[0], src[1:1 + TOP_K], src[-2], src[-1]
        gw = _gate_columns(gt_ref)
    if source == "split":
        x = jnp.where(step < N_PROMPT_BLK, src[0][...], src[1][...])
    else:
        x = jnp.concatenate(_combined_rows(x1_ref, yg_refs, gw, gate_ref, row), axis=1)
    outs[0][...] = x
    hb = _norm_mod(x, g_ref, shift_ref, scale_ref, row).astype(BF16)
    off = 0
    for o_ref, width in zip(outs[1:], widths):
        o_ref[...] = _dot(hb, w_bf[:, off:off + width])
        off += width


def _inproj(x, g, shift, scale, w_all, index, widths):
    source, x_args = x[0], list(x[1:])
    n_in = w_all.shape[2]
    row_spec = lambda width: pl.BlockSpec((RB, width), lambda i: (i, 0))
    full = lambda shape: pl.BlockSpec(shape, lambda i: (0,) * len(shape))
    if source == "split":
        x_specs = [pl.BlockSpec((RB, D_MODEL), lambda i: (jnp.minimum(i, N_PROMPT_BLK - 1), 0)),
                   pl.BlockSpec((RB, D_MODEL), lambda i: (jnp.maximum(i - N_PROMPT_BLK, 0), 0))]
    else:
        x1, yg, gates_t, gate = x_args
        x_specs, x_args = _combine_specs(), [x1] + [yg] * TOP_K + [gates_t, gate]
    out_widths = (D_MODEL,) + tuple(widths)
    return pl.pallas_call(
        functools.partial(_inproj_kernel, widths=widths, source=source),
        grid=(NBLK,),
        in_specs=x_specs + [full((1, D_MODEL)), full((MOD_ROWS, D_MODEL)), full((MOD_ROWS, D_MODEL)),
                            pl.BlockSpec((None, D_MODEL, n_in), lambda i: (index, 0, 0))],
        out_specs=[row_spec(width) for width in out_widths],
        out_shape=[jax.ShapeDtypeStruct((NT, width), F32) for width in out_widths],
        scratch_shapes=[pltpu.VMEM((D_MODEL, n_in), BF16)],
        compiler_params=_params(),
        name="inproj",
    )(*x_args, g.reshape(1, D_MODEL), shift, scale, w_all)


RC = RET_CHUNK
RET_PAIR = 2 * RET_DK


def _rope(x, cos, sin_signed):
    lane = lax.broadcasted_iota(jnp.int32, x.shape, 1)
    swapped = jnp.where(lane % 2 == 0, pltpu.roll(x, x.shape[1] - 1, 1), pltpu.roll(x, 1, 1))
    return x * cos + swapped * sin_signed


def _ret_kernel(decay_ref, q_ref, k_ref, v_ref, cos_ref, sin_ref, s0_ref, acc_ref, *rest, reverse):
    del acc_ref
    if reverse:
        of_ref, g_ref, gn_ref, o_ref, sfin_ref, st_ref, dm_ref, dq_ref, dk_ref, ds_ref = rest
    else:
        o_ref, sfin_ref, st_ref, dm_ref, dq_ref, dk_ref, ds_ref = rest
    step = pl.program_id(0)
    blk = NBLK - 1 - step if reverse else step
    direction = 1 if reverse else 0

    @pl.when(step == 0)
    def _():
        row = lax.broadcasted_iota(jnp.int32, (RC, RC), 0).astype(F32)
        col = lax.broadcasted_iota(jnp.int32, (RC, RC), 1).astype(F32)
        for h in range(RET_HEADS):
            lg = -jnp.exp(jnp.full((RC, RC), decay_ref[direction, h], F32))
            if reverse:
                diff = col - row
                mask = diff > 0
                q_pow = RC - row
                k_pow = row
            else:
                diff = row - col
                mask = diff >= 0
                q_pow = row + 1.0
                k_pow = RC - 1.0 - row
            dm_ref[h] = jnp.where(mask, jnp.exp(lg * jnp.where(mask, diff, 0.0)), 0.0)
            dq_ref[h] = jnp.exp(lg * q_pow)
            dk_ref[h] = jnp.exp(lg * k_pow)
            ds_ref[h] = jnp.exp(lg * RC)

    starts = _is_last_block(blk) if reverse else _is_first_block(blk)

    is_prompt = blk < N_PROMPT_BLK

    @pl.when(starts)
    def _():
        st_ref[...] = jnp.zeros_like(st_ref)

    @pl.when(jnp.logical_and(starts, jnp.logical_not(is_prompt)))
    def _():
        for h in range(RET_HEADS):
            off = (h % 2) * RET_DK
            st_ref[h, off:off + RET_DK, :] = s0_ref[h]

    lane = lax.broadcasted_iota(jnp.int32, (1, RET_PAIR), 1)
    chunks = range(RB // RC)
    chunk_order = list(reversed(chunks) if reverse else chunks)
    units = [(h, c) for h in range(RET_HEADS) for c in chunk_order]
    rows_of = lambda c: slice(c * RC, (c + 1) * RC)
    vcols = lambda h: slice(h * RET_DV, (h + 1) * RET_DV)
    roped = {}
    for p in range(RET_HEADS // 2):
        cols = slice(p * RET_PAIR, (p + 1) * RET_PAIR)
        for c in chunk_order:
            rows = rows_of(c)
            cos, sin = cos_ref[rows, :], sin_ref[rows, :]
            roped[p, c] = (_rope(q_ref[rows, cols], cos, sin),
                           _rope(k_ref[rows, cols] * (RET_DK ** -0.5), cos, sin))
    qm_u, vh_u, att_u, kd_u = {}, {}, {}, {}
    for h, c in units:
        head_mask = (lane // RET_DK == h % 2).astype(F32)
        q2, k2 = roped[h // 2, c]
        vh_u[h, c] = v_ref[rows_of(c), vcols(h)].astype(BF16)
        qm_u[h, c] = (q2 * head_mask).astype(BF16)
        km = k2 * head_mask
        att_u[h, c] = (_dot_nt(qm_u[h, c], km.astype(BF16)) * dm_ref[h]).astype(BF16)
        kd_u[h, c] = (km * dk_ref[h]).astype(BF16)
    intra_u = {u: _dot(att_u[u], vh_u[u]) for u in units}
    delta_u = {u: _dot_tn(kd_u[u], vh_u[u]) for u in units}
    state_u = {}
    for h in range(RET_HEADS):
        st = st_ref[h]
        for c in chunk_order:
            state_u[h, c] = st.astype(BF16)
            st = st * ds_ref[h] + delta_u[h, c]
        st_ref[h] = st
    for h, c in units:
        rows, out_cols = rows_of(c), vcols(h)
        o = intra_u[h, c] + _dot(qm_u[h, c], state_u[h, c]) * dq_ref[h]
        if reverse:
            o = o + of_ref[rows, out_cols]
            o = o * lax.rsqrt(jnp.mean(o * o, axis=-1, keepdims=True) + EPS)
            o = o * gn_ref[:, out_cols] * _silu(g_ref[rows, out_cols])
            o_ref[rows, out_cols] = o.astype(o_ref.dtype)
        else:
            o_ref[rows, out_cols] = o

    ends = _is_first_block(blk) if reverse else _is_last_block(blk)

    @pl.when(jnp.logical_and(ends, is_prompt))
    def _():
        for h in range(RET_HEADS):
            off = (h % 2) * RET_DK
            sfin_ref[h] = st_ref[h, off:off + RET_DK, :]


def _sample_seq(blk):
    return jnp.clip(_seq_of_block(blk) - BATCH, 0, DEC_BATCH - 1)


def _prompt_seq(blk):
    return jnp.minimum(_seq_of_block(blk), BATCH - 1)


def _retention(decay, q, k, v, cos_tab, sin_tab, state_in, state_out, layer, *, reverse,
               o_fwd=None, g=None, gn=None):
    bmap = (lambda j: NBLK - 1 - j) if reverse else (lambda j: j)
    direction = 1 if reverse else 0
    qd, vd = RET_HEADS * RET_DK, RET_HEADS * RET_DV
    row_spec = lambda width: pl.BlockSpec((RB, width), lambda j: (bmap(j), 0))
    state_blk = (None, None, None, RET_HEADS, RET_DK, RET_DV)
    in_specs = [pl.BlockSpec(memory_space=pltpu.SMEM), row_spec(qd), row_spec(qd), row_spec(vd),
                pl.BlockSpec((RB, RET_PAIR), lambda j: (_rope_block(bmap(j)), 0)),
                pl.BlockSpec((RB, RET_PAIR), lambda j: (_rope_block(bmap(j)), 0)),
                pl.BlockSpec(state_blk, lambda j: (_sample_seq(bmap(j)), layer, direction, 0, 0, 0)),
                pl.BlockSpec(memory_space=pl.ANY)]
    args = [decay, q, k, v, cos_tab, sin_tab, state_in, state_out]
    if reverse:
        in_specs += [row_spec(vd), row_spec(vd), pl.BlockSpec((1, vd), lambda j: (0, 0))]
        args += [o_fwd, g, gn.reshape(1, vd)]
    tile = pltpu.VMEM((RET_HEADS, RC, RC), F32)
    return pl.pallas_call(
        functools.partial(_ret_kernel, reverse=reverse),
        grid=(NBLK,),
        in_specs=in_specs,
        out_specs=[row_spec(vd),
                   pl.BlockSpec(state_blk, lambda j: (_prompt_seq(bmap(j)), layer, direction, 0, 0, 0))],
        out_shape=[jax.ShapeDtypeStruct((NT, vd), BF16 if reverse else F32),
                   jax.ShapeDtypeStruct(state_out.shape, F32)],
        input_output_aliases={7: 1},
        scratch_shapes=[tile, tile, tile, tile, tile],
        compiler_params=_params(),
        name="retention_bwd" if reverse else "retention_fwd",
    )(*args)


CONV_RT = 32
CONV_CT = 128
CONV_SPAN = RB + 2 * HALO - 8


def _conv_kernel(a_ref, ga_ref, ap_ref, gap_ref, an_ref, gan_ref, cw_ref, cb_ref, lng_ref, lnb_ref,
                 o_ref, u_ref, y_ref, us_ref):
    blk = pl.program_id(0)
    keep_prev = jnp.where(_is_first_block(blk), 0.0, 1.0)
    keep_next = jnp.where(_is_last_block(blk), 0.0, 1.0)
    u_ref[0:HALO, :] = ap_ref[...] * _sigmoid(gap_ref[...]) * keep_prev
    u_ref[HALO:HALO + RB, :] = a_ref[...] * _sigmoid(ga_ref[...])
    u_ref[HALO + RB:HALO + RB + HALO, :] = an_ref[...] * _sigmoid(gan_ref[...]) * keep_next
    for r in range(1, 8):
        us_ref[r - 1] = u_ref[r:r + CONV_SPAN, :]
    for ct in range(CONV_CH // CONV_CT):
        cols = slice(ct * CONV_CT, (ct + 1) * CONV_CT)
        for rt in range(RB // CONV_RT):
            acc = jnp.zeros((CONV_RT, CONV_CT), F32)
            for w in range(CONV_WIDTH):
                tiles, r = divmod(HALO - CONV_PAD + w, 8)
                base = rt * CONV_RT + 8 * tiles
                src = u_ref if r == 0 else us_ref.at[r - 1]
                acc = acc + src[base:base + CONV_RT, cols] * cw_ref[w:w + 1, cols]
            y_ref[rt * CONV_RT:(rt + 1) * CONV_RT, cols] = acc + cb_ref[:, cols]
    y = y_ref[...]
    mu = jnp.mean(y, axis=-1, keepdims=True)
    var = jnp.mean(jnp.square(y - mu), axis=-1, keepdims=True)
    o_ref[...] = _silu((y - mu) * lax.rsqrt(var + EPS) * lng_ref[...] + lnb_ref[...]).astype(o_ref.dtype)


def _conv_module(a, ga, cw, cb, lng, lnb):
    per_blk = RB // HALO
    n_halo = NT // HALO
    row_spec = pl.BlockSpec((RB, CONV_CH), lambda i: (i, 0))
    prev_spec = pl.BlockSpec((HALO, CONV_CH), lambda i: (jnp.maximum(i * per_blk - 1, 0), 0))
    next_spec = pl.BlockSpec((HALO, CONV_CH), lambda i: (jnp.minimum((i + 1) * per_blk, n_halo - 1), 0))
    vec = pl.BlockSpec((1, CONV_CH), lambda i: (0, 0))
    return pl.pallas_call(
        _conv_kernel,
        grid=(NBLK,),
        in_specs=[row_spec, row_spec, prev_spec, prev_spec, next_spec, next_spec,
                  pl.BlockSpec((CONV_WIDTH, CONV_CH), lambda i: (0, 0)), vec, vec, vec],
        out_specs=row_spec,
        out_shape=jax.ShapeDtypeStruct((NT, CONV_CH), BF16),
        scratch_shapes=[pltpu.VMEM((RB + 2 * HALO, CONV_CH), F32), pltpu.VMEM((RB, CONV_CH), F32),
                        pltpu.VMEM((7, CONV_SPAN, CONV_CH), F32)],
        compiler_params=_params(),
        name="conv_module",
    )(a, ga, a, ga, a, ga, cw, cb.reshape(1, -1), lng.reshape(1, -1), lnb.reshape(1, -1))


GC = GLA_CHUNK
GLA_NSUB = GC // GLA_SUB


def _split3(a):
    p1 = a.astype(BF16)
    r1 = a - p1.astype(F32)
    p2 = r1.astype(BF16)
    p3 = (r1 - p2.astype(F32)).astype(BF16)
    return p1, p2, p3


def _gla_kernel(alr_ref, q_ref, k_ref, v_ref, wa_ref, ba_ref, s0_ref, acc_ref, *rest, reverse):
    del acc_ref
    if reverse:
        of_ref, r_ref, gn_ref, o_ref, sfin_ref, st_ref, b_ref, tri_ref, ob_ref = rest
    else:
        o_ref, sfin_ref, st_ref, b_ref, tri_ref = rest
        ob_ref = o_ref
    step = pl.program_id(0)
    blk = NBLK - 1 - step if reverse else step
    starts = _is_last_block(blk) if reverse else _is_first_block(blk)
    is_prompt = blk < N_PROMPT_BLK

    @pl.when(jnp.logical_and(starts, is_prompt))
    def _():
        st_ref[...] = jnp.zeros_like(st_ref)

    @pl.when(jnp.logical_and(starts, jnp.logical_not(is_prompt)))
    def _():
        for h in range(GLA_HEADS):
            st_ref[h] = s0_ref[h].T

    @pl.when(step == 0)
    def _():
        row = lax.broadcasted_iota(jnp.int32, (RB, RB), 0)
        col = lax.broadcasted_iota(jnp.int32, (RB, RB), 1)
        ordered = col >= row if reverse else col <= row
        tri_ref[...] = jnp.where(jnp.logical_and(row // GC == col // GC, ordered), 1.0, 0.0).astype(BF16)

    heads = range(GLA_HEADS)
    hcols = [slice(h * GLA_DK, (h + 1) * GLA_DK) for h in heads]
    alr = alr_ref[...]
    z = [_dot_hi(alr, wa_ref[:, hcols[h]]) + ba_ref[:, hcols[h]] for h in heads]
    log_a = [(jnp.minimum(zh, 0.0) - jnp.log(1.0 + jnp.exp(-jnp.abs(zh)))) * (1.0 / GLA_TAU) for zh in z]
    parts = [_split3(la) for la in log_a]
    tri = tri_ref[...]
    for h in heads:
        g1, g2, g3 = parts[h]
        b_ref[:, hcols[h]] = _dot(tri, g1) + (_dot(tri, g2) + _dot(tri, g3))

    c_row = lax.broadcasted_iota(jnp.int32, (GC, 1), 0)
    a_row = lax.broadcasted_iota(jnp.int32, (GC, GC), 0)
    a_col = lax.broadcasted_iota(jnp.int32, (GC, GC), 1)
    att_mask = a_col > a_row if reverse else a_col <= a_row
    chunks = range(RB // GC)
    chunk_order = list(reversed(chunks) if reverse else chunks)
    units = [(h, c) for h in range(GLA_HEADS) for c in chunk_order]
    kcols = lambda h: slice(h * GLA_DK, (h + 1) * GLA_DK)
    vcols = lambda h: slice(h * GLA_DV, (h + 1) * GLA_DV)
    rows_of = lambda c: slice(c * GC, (c + 1) * GC)

    vh_u, qe_u, ke_u, decay_u, att_u = {}, {}, {}, {}, {}
    for h, c in units:
        rows = rows_of(c)
        b = b_ref[rows, kcols(h)]
        qh = q_ref[rows, kcols(h)] * (GLA_DK ** -0.5)
        kh = k_ref[rows, kcols(h)]
        vh_u[h, c] = v_ref[rows, vcols(h)].astype(BF16)
        edge = b[0:1, :] if reverse else b[GC - 1:GC, :]
        bounds = []
        for s in range(GLA_NSUB):
            if reverse:
                hi = (s + 1) * GLA_SUB
                bounds.append(b[hi:hi + 1, :] if s < GLA_NSUB - 1 else jnp.zeros((1, GLA_DK), F32))
            else:
                lo = s * GLA_SUB
                bounds.append(b[lo - 1:lo, :] if s > 0 else jnp.zeros((1, GLA_DK), F32))
        own = jnp.concatenate([jnp.broadcast_to(bd, (GLA_SUB, GLA_DK)) for bd in bounds], axis=0)
        q_own = qh * jnp.exp(b - own)
        q_parts, k_parts = [], []
        for s, bd in enumerate(bounds):
            q_parts.append(jnp.where(c_row // GLA_SUB == s, q_own, 0.0))
            reach = c_row >= s * GLA_SUB if reverse else c_row < (s + 1) * GLA_SUB
            k_parts.append(kh * jnp.exp(jnp.where(reach, bd - b, -jnp.inf)))
        q_bd = jnp.concatenate(q_parts, axis=1).astype(BF16)
        k_cat = jnp.concatenate(k_parts, axis=1).astype(BF16)
        att_u[h, c] = jnp.where(att_mask, _dot_nt(q_bd, k_cat), 0.0).astype(BF16)
        qe_u[h, c] = (qh * jnp.exp(b)).astype(BF16)
        ke_u[h, c] = (kh * jnp.exp(edge - b)).astype(BF16)
        decay_u[h, c] = jnp.exp(edge)
    intra_u = {u: _dot(att_u[u], vh_u[u]) for u in units}
    delta_u = {u: _dot_tn(vh_u[u], ke_u[u]) for u in units}
    state_u = {}
    for h in range(GLA_HEADS):
        st = st_ref[h]
        for c in chunk_order:
            state_u[h, c] = st.astype(BF16)
            st = st * decay_u[h, c] + delta_u[h, c]
        st_ref[h] = st
    for h, c in units:
        ob_ref[rows_of(c), vcols(h)] = intra_u[h, c] + _dot_nt(qe_u[h, c], state_u[h, c])

    if reverse:
        for h in range(GLA_HEADS):
            cols = vcols(h)
            o = ob_ref[:, cols] + of_ref[:, cols]
            o = o * lax.rsqrt(jnp.mean(o * o, axis=-1, keepdims=True) + EPS)
            o_ref[:, cols] = (o * gn_ref[:, cols] * _silu(r_ref[:, cols])).astype(o_ref.dtype)

    ends = _is_first_block(blk) if reverse else _is_last_block(blk)

    @pl.when(jnp.logical_and(ends, is_prompt))
    def _():
        for h in range(GLA_HEADS):
            sfin_ref[h] = st_ref[h].T


def _gla(alr, q, k, v, wa, ba, state_in, state_out, layer, *, reverse, o_fwd=None, r=None, gn=None):
    bmap = (lambda j: NBLK - 1 - j) if reverse else (lambda j: j)
    direction = 1 if reverse else 0
    qd, vd = GLA_HEADS * GLA_DK, GLA_HEADS * GLA_DV
    row_spec = lambda width: pl.BlockSpec((RB, width), lambda j: (bmap(j), 0))
    state_blk = (None, None, None, GLA_HEADS, GLA_DK, GLA_DV)
    in_specs = [row_spec(2 * GLA_RANK), row_spec(qd), row_spec(qd), row_spec(vd),
                pl.BlockSpec((2 * GLA_RANK, qd), lambda j: (0, 0)),
                pl.BlockSpec((1, qd), lambda j: (0, 0)),
                pl.BlockSpec(state_blk, lambda j: (_sample_seq(bmap(j)), layer, direction, 0, 0, 0)),
                pl.BlockSpec(memory_space=pl.ANY)]
    args = [alr, q, k, v, wa, ba, state_in, state_out]
    scratch = [pltpu.VMEM((GLA_HEADS, GLA_DV, GLA_DK), F32), pltpu.VMEM((RB, qd), F32),
               pltpu.VMEM((RB, RB), BF16)]
    if reverse:
        in_specs += [row_spec(vd), row_spec(vd), pl.BlockSpec((1, vd), lambda j: (0, 0))]
        args += [o_fwd, r, gn.reshape(1, vd)]
        scratch += [pltpu.VMEM((RB, vd), F32)]
    return pl.pallas_call(
        functools.partial(_gla_kernel, reverse=reverse),
        grid=(NBLK,),
        in_specs=in_specs,
        out_specs=[row_spec(vd),
                   pl.BlockSpec(state_blk, lambda j: (_prompt_seq(bmap(j)), layer, direction, 0, 0, 0))],
        out_shape=[jax.ShapeDtypeStruct((NT, vd), BF16 if reverse else F32),
                   jax.ShapeDtypeStruct(state_out.shape, F32)],
        input_output_aliases={7: 1},
        scratch_shapes=scratch,
        compiler_params=_params(),
        name="gla_bwd" if reverse else "gla_fwd",
    )(*args)


OUT_SUB = 128


def _outproj_kernel(*refs, n_mix):
    mix_refs = refs[:n_mix]
    (w_ref, x_ref, gate_ref, g2_ref, shift_ref, scale_ref, rw_ref, rb_ref,
     x1_ref, h2_ref, logit_ref, w_bf) = refs[n_mix:]
    step = pl.program_id(0)

    @pl.when(step == 0)
    def _():
        w_bf[...] = w_ref[...].astype(BF16)

    row = _mod_row(step)
    r_hi, r_lo = _split2(rw_ref[...])
    groups = [slice(p * OUT_SUB, (p + 1) * OUT_SUB) for p in range(RB // OUT_SUB)]
    mixed = []
    for rows in groups:
        m, off = None, 0
        for mix_ref in mix_refs:
            width = mix_ref.shape[1]
            part = _dot(mix_ref[rows, :], w_bf[off:off + width, :])
            m = part if m is None else m + part
            off += width
        mixed.append(m)
    normed = []
    for rows, m in zip(groups, mixed):
        x1 = x_ref[rows, :] + gate_ref[pl.ds(row, 1), :] * m
        x1_ref[rows, :] = x1
        h2 = _norm_mod(x1, g2_ref, shift_ref, scale_ref, row)
        h2_ref[rows, :] = _pack_rows(h2)
        normed.append(_split2(h2))
    for rows, (h_hi, h_lo) in zip(groups, normed):
        logit_ref[:, rows] = _dot_nt(r_hi, h_hi) + (_dot_nt(r_hi, h_lo) + _dot_nt(r_lo, h_hi)) + rb_ref[...]


def _outproj(mixes, w_all, index, x, gate, g2, shift, scale, rw, rb):
    n_mix = len(mixes)
    row_spec = lambda width: pl.BlockSpec((RB, width), lambda i: (i, 0))
    full = lambda shape: pl.BlockSpec(shape, lambda i: (0,) * len(shape))
    mod_spec = full((MOD_ROWS, D_MODEL))
    n_mixed = w_all.shape[1]
    return pl.pallas_call(
        functools.partial(_outproj_kernel, n_mix=n_mix),
        grid=(NBLK,),
        in_specs=[row_spec(m.shape[1]) for m in mixes]
        + [pl.BlockSpec((None, n_mixed, D_MODEL), lambda i: (index, 0, 0)),
           row_spec(D_MODEL), mod_spec, full((1, D_MODEL)), mod_spec, mod_spec,
           full((N_EXPERTS, D_MODEL)), full((N_EXPERTS, 1))],
        out_specs=[row_spec(D_MODEL), row_spec(D_MODEL // 2), pl.BlockSpec((N_EXPERTS, RB), lambda i: (0, i))],
        out_shape=[jax.ShapeDtypeStruct((NT, D_MODEL), F32), jax.ShapeDtypeStruct((NT, D_MODEL // 2), jnp.uint32),
                   jax.ShapeDtypeStruct((N_EXPERTS, NT), F32)],
        scratch_shapes=[pltpu.VMEM((n_mixed, D_MODEL), BF16)],
        compiler_params=_params(),
        name="outproj",
    )(*mixes, w_all, x, gate, g2.reshape(1, D_MODEL), shift, scale, rw.T, rb.reshape(N_EXPERTS, 1))


ROUTE_BLK = 2048
ROUTE_SUB = 256


def _route_kernel(lg_ref, idx_ref, rank_ref, gt_ref, cnt_ref, carry_ref):
    @pl.when(pl.program_id(0) == 0)
    def _():
        carry_ref[...] = jnp.zeros_like(carry_ref)

    logits = lg_ref[...]
    eid = lax.broadcasted_iota(jnp.int32, logits.shape, 0).astype(F32)
    work = logits
    onehots, top_vals = [], []
    for kk in range(TOP_K):
        top = jnp.max(work, axis=0, keepdims=True)
        first = jnp.min(jnp.where(work == top, eid, float(N_EXPERTS)), axis=0, keepdims=True)
        onehot = eid == first
        idx_ref[kk:kk + 1, :] = first.astype(jnp.int32)
        onehots.append(onehot)
        top_vals.append(top)
        work = jnp.where(onehot, -jnp.inf, work)
    exps = [jnp.exp(v - top_vals[0]) for v in top_vals]
    denom = exps[0]
    for e in exps[1:]:
        denom = denom + e
    gt_ref[...] = jnp.zeros_like(gt_ref)
    for kk in range(TOP_K):
        gt_ref[kk:kk + 1, :] = exps[kk] / denom

    sel = jnp.zeros(logits.shape, F32)
    for onehot in onehots:
        sel = sel + jnp.where(onehot, 1.0, 0.0)
    sel = sel.astype(BF16)
    r_i = lax.broadcasted_iota(jnp.int32, (ROUTE_SUB, ROUTE_SUB), 0)
    c_i = lax.broadcasted_iota(jnp.int32, (ROUTE_SUB, ROUTE_SUB), 1)
    before = jnp.where(r_i < c_i, 1.0, 0.0).astype(BF16)
    ones = jnp.ones((ROUTE_SUB, ROUTE_SUB), BF16)
    carry = carry_ref[...]
    for s in range(ROUTE_BLK // ROUTE_SUB):
        cols = slice(s * ROUTE_SUB, (s + 1) * ROUTE_SUB)
        pos = _dot(sel[:, cols], before) + carry
        for kk in range(TOP_K):
            rank = jnp.sum(jnp.where(onehots[kk][:, cols], pos, 0.0), axis=0, keepdims=True)
            rank_ref[kk:kk + 1, cols] = rank.astype(jnp.int32)
        carry = carry + _dot(sel[:, cols], ones)
    carry_ref[...] = carry
    cnt_ref[...] = carry.astype(jnp.int32)


def _route(logits_t):
    col_spec = lambda rows: pl.BlockSpec((rows, ROUTE_BLK), lambda i: (0, i))
    return pl.pallas_call(
        _route_kernel,
        grid=(NT // ROUTE_BLK,),
        in_specs=[col_spec(N_EXPERTS)],
        out_specs=[col_spec(TOP_K), col_spec(TOP_K), col_spec(8),
                   pl.BlockSpec((N_EXPERTS, ROUTE_SUB), lambda i: (0, 0))],
        out_shape=[jax.ShapeDtypeStruct((TOP_K, NT), jnp.int32), jax.ShapeDtypeStruct((TOP_K, NT), jnp.int32),
                   jax.ShapeDtypeStruct((8, NT), F32), jax.ShapeDtypeStruct((N_EXPERTS, ROUTE_SUB), jnp.int32)],
        scratch_shapes=[pltpu.VMEM((N_EXPERTS, ROUTE_SUB), F32)],
        compiler_params=_params(),
        name="route",
    )(logits_t)


TM = 512
TM_SUB = 256
MOE_NBLK = NT * TOP_K // TM + N_EXPERTS
MOE_ROWS = MOE_NBLK * TM
HALF = D_MODEL // 2


def _moe_kernel(be_ref, nv_ref, nx_ref, x_ref, wgu_hbm, bgu_ref, wd_hbm, bd_ref, y_ref,
                wgu_st, wd_st, wgu_bf, wd_bf, sems, *, layer):
    i = pl.program_id(0)
    n_valid = nv_ref[i]

    def fetch(e):
        return (pltpu.make_async_copy(wgu_hbm.at[layer, e], wgu_st, sems.at[0]),
                pltpu.make_async_copy(wd_hbm.at[layer, e], wd_st, sems.at[1]))

    @pl.when(i == 0)
    def _():
        for cp in fetch(be_ref[0]):
            cp.start()

    @pl.when(n_valid > 0)
    def _():
        e = be_ref[i]
        changed = jnp.logical_or(i == 0, e != be_ref[jnp.maximum(i - 1, 0)])

        @pl.when(changed)
        def _():
            for cp in fetch(e):
                cp.wait()
            wgu_bf[...] = wgu_st[...].astype(BF16)
            wd_bf[...] = wd_st[...].astype(BF16)
            nxt = nx_ref[e]

            @pl.when(nxt >= 0)
            def _():
                for cp in fetch(nxt):
                    cp.start()

    for p in range(TM // TM_SUB):
        @pl.when(n_valid > p * TM_SUB)
        def _():
            rows = slice(p * TM_SUB, (p + 1) * TM_SUB)
            row_id = lax.broadcasted_iota(jnp.int32, (TM_SUB, 1), 0) + p * TM_SUB
            x_lo, x_hi = _unpack_rows(jnp.where(row_id < n_valid, x_ref[rows, :], jnp.uint32(0)))
            gu = (_dot(x_lo.astype(BF16), wgu_bf[:HALF, :]) + _dot(x_hi.astype(BF16), wgu_bf[HALF:, :])
                  + bgu_ref[...])
            gate = jnp.minimum(gu[:, :D_FF], SWIGLU_LIMIT)
            up = jnp.clip(gu[:, D_FF:], -SWIGLU_LIMIT, SWIGLU_LIMIT)
            hdn = gate * _sigmoid(SWIGLU_ALPHA * gate) * (up + 1.0)
            y_ref[rows, :] = _pack_rows(_dot(hdn.astype(BF16), wd_bf[...]) + bd_ref[...])


def _moe_experts(layer, block_e, n_valid, next_e, xs, w_gu, b_gu, w_down, b_down):
    grid_spec = pltpu.PrefetchScalarGridSpec(
        num_scalar_prefetch=3,
        grid=(MOE_NBLK,),
        in_specs=[pl.BlockSpec((TM, HALF), lambda i, be, nv, nx: (i, 0)),
                  pl.BlockSpec(memory_space=pl.ANY),
                  pl.BlockSpec((None, None, 1, 2 * D_FF), lambda i, be, nv, nx: (layer, be[i], 0, 0)),
                  pl.BlockSpec(memory_space=pl.ANY),
                  pl.BlockSpec((None, None, 1, D_MODEL), lambda i, be, nv, nx: (layer, be[i], 0, 0))],
        out_specs=pl.BlockSpec((TM, HALF), lambda i, be, nv, nx: (i, 0)),
        scratch_shapes=[pltpu.VMEM((D_MODEL, 2 * D_FF), F32), pltpu.VMEM((D_FF, D_MODEL), F32),
                        pltpu.VMEM((D_MODEL, 2 * D_FF), BF16), pltpu.VMEM((D_FF, D_MODEL), BF16),
                        pltpu.SemaphoreType.DMA((2,))],
    )
    return pl.pallas_call(
        functools.partial(_moe_kernel, layer=layer),
        grid_spec=grid_spec,
        out_shape=jax.ShapeDtypeStruct((MOE_ROWS, HALF), jnp.uint32),
        compiler_params=_params(),
        name="moe_experts",
    )(block_e, n_valid, next_e, xs, w_gu, b_gu.reshape(DEPTH, N_EXPERTS, 1, -1), w_down,
      b_down.reshape(DEPTH, N_EXPERTS, 1, -1))


SC_WORKERS = 32
SC_WIN = 64


def _sc_mesh():
    return plsc.VectorSubcoreMesh(core_axis_name="core", subcore_axis_name="subcore")


def _sc_worker():
    return lax.axis_index("core") * (SC_WORKERS // 2) + lax.axis_index("subcore")


def _sc_scatter_rows(x, dest_t, n_rows):
    n, width = x.shape
    kk = dest_t.shape[0]
    per = n // SC_WORKERS
    n_win = per // SC_WIN
    assert per * SC_WORKERS == n and n_win * SC_WIN == per and n_win % 2 == 0

    @pl.kernel(out_type=jax.ShapeDtypeStruct((n_rows, width), x.dtype), mesh=_sc_mesh(),
               scratch_types=[pltpu.VMEM((kk, per), jnp.int32), pltpu.VMEM((SC_WIN, width), x.dtype),
                              pltpu.VMEM((SC_WIN, width), x.dtype), pltpu.SemaphoreType.DMA((4,))])
    def scatter(x_hbm, i_hbm, o_hbm, idx_v, buf0, buf1, sems):
        base = _sc_worker() * per
        pltpu.sync_copy(i_hbm.at[:, pl.ds(base, per)], idx_v)

        def get(j, buf, s):
            return pltpu.make_async_copy(x_hbm.at[pl.ds(base + j * SC_WIN, SC_WIN)], buf, sems.at[s])

        def put(j, q, buf, s):
            return pltpu.make_async_copy(buf, o_hbm.at[idx_v.at[q, pl.ds(j * SC_WIN, SC_WIN)]], sems.at[s])

        get(0, buf0, 0).start()

        @pl.loop(0, n_win, step=2)
        def _(j):
            get(j, buf0, 0).wait()

            @pl.when(j > 0)
            def _():
                for q in range(kk):
                    put(j - 1, q, buf1, 3).wait()

            get(j + 1, buf1, 1).start()
            for q in range(kk):
                put(j, q, buf0, 2).start()
            get(j + 1, buf1, 1).wait()
            for q in range(kk):
                put(j, q, buf0, 2).wait()

            @pl.when(j + 2 < n_win)
            def _():
                get(j + 2, buf0, 0).start()

            for q in range(kk):
                put(j + 1, q, buf1, 3).start()

        for q in range(kk):
            put(n_win - 1, q, buf1, 3).wait()

    return scatter(x, dest_t)


def _sc_gather_rows(y, idx):
    n = idx.shape[0]
    width = y.shape[1]
    per = n // SC_WORKERS
    n_win = per // SC_WIN
    assert per * SC_WORKERS == n and n_win * SC_WIN == per and n_win % 2 == 0

    @pl.kernel(out_type=jax.ShapeDtypeStruct((n, width), y.dtype), mesh=_sc_mesh(),
               scratch_types=[pltpu.VMEM((per,), jnp.int32), pltpu.VMEM((SC_WIN, width), y.dtype),
                              pltpu.VMEM((SC_WIN, width), y.dtype), pltpu.SemaphoreType.DMA((4,))])
    def gather(y_hbm, i_hbm, o_hbm, idx_v, buf0, buf1, sems):
        base = _sc_worker() * per
        pltpu.sync_copy(i_hbm.at[pl.ds(base, per)], idx_v)

        def get(j, buf, s):
            return pltpu.make_async_copy(y_hbm.at[idx_v.at[pl.ds(j * SC_WIN, SC_WIN)]], buf, sems.at[s])

        def put(j, buf, s):
            return pltpu.make_async_copy(buf, o_hbm.at[pl.ds(base + j * SC_WIN, SC_WIN)], sems.at[s])

        get(0, buf0, 0).start()

        @pl.loop(0, n_win, step=2)
        def _(j):
            get(j, buf0, 0).wait()

            @pl.when(j > 0)
            def _():
                put(j - 1, buf1, 3).wait()

            get(j + 1, buf1, 1).start()
            put(j, buf0, 2).start()
            get(j + 1, buf1, 1).wait()
            put(j, buf0, 2).wait()

            @pl.when(j + 2 < n_win)
            def _():
                get(j + 2, buf0, 0).start()

            put(j + 1, buf1, 3).start()

        put(n_win - 1, buf1, 3).wait()

    return gather(y, idx)


def _gate_columns(gt_ref):
    r_i = lax.broadcasted_iota(jnp.int32, (RB, RB), 0)
    c_i = lax.broadcasted_iota(jnp.int32, (RB, RB), 1)
    eye = jnp.where(r_i == c_i, 1.0, 0.0).astype(BF16)
    g1, g2, g3 = _split3(gt_ref[...])
    return _dot_nt(eye, g1) + (_dot_nt(eye, g2) + _dot_nt(eye, g3))


def _combined_rows(x1_ref, yg_refs, gw, gate_ref, row, rows=slice(None)):
    acc_lo, acc_hi = None, None
    for kk in range(TOP_K):
        y_lo, y_hi = _unpack_rows(yg_refs[kk][rows, :])
        w = gw[rows, kk:kk + 1]
        acc_lo = y_lo * w if acc_lo is None else acc_lo + y_lo * w
        acc_hi = y_hi * w if acc_hi is None else acc_hi + y_hi * w
    x_lo = x1_ref[rows, :HALF] + gate_ref[pl.ds(row, 1), :HALF] * acc_lo
    x_hi = x1_ref[rows, HALF:] + gate_ref[pl.ds(row, 1), HALF:] * acc_hi
    return x_lo, x_hi


def _combine_specs():
    slot_spec = lambda k: pl.BlockSpec((RB, HALF), lambda i: (k * NBLK + i, 0))
    return ([pl.BlockSpec((RB, D_MODEL), lambda i: (i, 0))] + [slot_spec(k) for k in range(TOP_K)]
            + [pl.BlockSpec((8, RB), lambda i: (0, i)), pl.BlockSpec((MOD_ROWS, D_MODEL), lambda i: (0, 0))])


def _combine_final_kernel(x1_ref, *rest):
    yg_refs = rest[:TOP_K]
    gt_ref, gate_ref, fg_ref, op_ref, os_ref = rest[TOP_K:]
    i = pl.program_id(0)
    x_lo, x_hi = _combined_rows(x1_ref, yg_refs, _gate_columns(gt_ref), gate_ref, _mod_row(i))
    ms = (jnp.sum(x_lo * x_lo, axis=-1, keepdims=True) + jnp.sum(x_hi * x_hi, axis=-1, keepdims=True)) / D_MODEL
    scale = lax.rsqrt(ms + EPS)

    @pl.when(i < N_PROMPT_BLK)
    def _():
        op_ref[:, :HALF] = x_lo * scale * fg_ref[:, :HALF]
        op_ref[:, HALF:] = x_hi * scale * fg_ref[:, HALF:]

    @pl.when(i >= N_PROMPT_BLK)
    def _():
        os_ref[:, :HALF] = x_lo * scale * fg_ref[:, :HALF]
        os_ref[:, HALF:] = x_hi * scale * fg_ref[:, HALF:]


def _combine_final(x1, yg, gates_t, gate, final_g):
    return pl.pallas_call(
        _combine_final_kernel,
        grid=(NBLK,),
        in_specs=_combine_specs() + [pl.BlockSpec((1, D_MODEL), lambda i: (0, 0))],
        out_specs=[pl.BlockSpec((RB, D_MODEL), lambda i: (jnp.minimum(i, N_PROMPT_BLK - 1), 0)),
                   pl.BlockSpec((RB, D_MODEL), lambda i: (jnp.maximum(i - N_PROMPT_BLK, 0), 0))],
        out_shape=[jax.ShapeDtypeStruct((NT_PROMPT, D_MODEL), F32),
                   jax.ShapeDtypeStruct((NT - NT_PROMPT, D_MODEL), F32)],
        compiler_params=_params(),
        name="moe_combine_final",
    )(x1, *([yg] * TOP_K), gates_t, gate, final_g.reshape(1, D_MODEL))


def _routing_plan(counts, idx_t, rank_t):
    counts = counts[:, 0]
    padded = (counts + TM - 1) // TM * TM
    pad_end = jnp.cumsum(padded)
    pad_start = pad_end - padded
    blk_row = (jnp.arange(MOE_NBLK, dtype=jnp.int32) * TM)[:, None]
    ids = jnp.arange(N_EXPERTS, dtype=jnp.int32)
    owns = jnp.logical_and(pad_start[None, :] <= blk_row, blk_row < pad_end[None, :])
    last_used = jnp.max(jnp.where(counts > 0, ids, 0))
    block_e = jnp.where(jnp.any(owns, axis=1), jnp.sum(jnp.where(owns, ids[None, :], 0), axis=1), last_used)
    block_e = block_e.astype(jnp.int32)
    left = jnp.clip(counts[None, :] - (blk_row - pad_start[None, :]), 0, TM)
    n_valid = jnp.sum(jnp.where(owns, left, 0), axis=1).astype(jnp.int32)
    start = jnp.zeros(idx_t.shape, jnp.int32)
    for e in range(N_EXPERTS):
        start = jnp.where(idx_t == e, pad_start[e], start)
    dest_t = (start + rank_t).astype(jnp.int32)
    later = jnp.where(jnp.logical_and(counts[None, :] > 0, ids[None, :] > ids[:, None]), ids[None, :], N_EXPERTS)
    next_e = jnp.min(later, axis=1)
    next_e = jnp.where(next_e == N_EXPERTS, -1, next_e).astype(jnp.int32)
    return block_e, n_valid, next_e, dest_t


def _rope_tables():
    rows = DEC_SEQ // GRID_W
    row = jnp.repeat(jnp.arange(rows, dtype=F32), GRID_W)
    col = jnp.tile(jnp.arange(GRID_W, dtype=F32), rows)
    n_f = RET_DK // 4
    freqs = ROPE_THETA ** (-jnp.arange(n_f, dtype=F32) / n_f)
    ang = jnp.concatenate([row[:, None] * freqs, col[:, None] * freqs], axis=-1)
    cos = jnp.repeat(jnp.cos(ang), 2, axis=-1)
    sin = jnp.repeat(jnp.sin(ang), 2, axis=-1) * jnp.tile(jnp.asarray([-1.0, 1.0], F32), RET_DK // 2)
    cos = jnp.concatenate([jnp.ones((RB, RET_DK), F32), cos], axis=0)
    sin = jnp.concatenate([jnp.zeros((RB, RET_DK), F32), sin], axis=0)
    return jnp.tile(cos, (1, 2)), jnp.tile(sin, (1, 2))


def kernel(x_prompt, x_sample, state_ret, state_gla, c, c_ctx, w_mod, b_mod, norm1_g, norm2_g, final_g, even_w_in, ret_decay, ret_gn, conv_w, conv_b, conv_ln_g, conv_ln_b, even_w_out, odd_w_in, gla_w_a2, gla_b_a2, gla_gn, odd_w_out, router_w, router_b, exp_w_gu, exp_b_gu, exp_w_down, exp_b_down):
    x_src = ("split", x_prompt.reshape(NT_PROMPT, D_MODEL), x_sample.reshape(NT - NT_PROMPT, D_MODEL))
    cvec = jnp.concatenate([c_ctx[None, :], c, jnp.zeros((MOD_ROWS - 1 - DEC_BATCH, D_MODEL), F32)], axis=0)
    mods = _modulation(cvec, w_mod, b_mod).reshape(DEPTH, MOD_ROWS, N_MOD, D_MODEL)
    cos_tab, sin_tab = _rope_tables()
    new_ret = jnp.zeros((BATCH,) + state_ret.shape[1:], F32)
    new_gla = jnp.zeros((BATCH,) + state_gla.shape[1:], F32)
    for l in range(DEPTH):
        mod = [mods[l, :, j, :] for j in range(N_MOD)]
        if l % 2 == 0:
            e = l // 2
            qd, vd = RET_HEADS * RET_DK, RET_HEADS * RET_DV
            x, q, k, v, g, a, ga = _inproj(x_src, norm1_g[l], mod[0], mod[1], even_w_in, e,
                                           (qd, qd, vd, vd, CONV_CH, CONV_CH))
            o_f, new_ret = _retention(ret_decay[e], q, k, v, cos_tab, sin_tab, state_ret, new_ret, e,
                                      reverse=False)
            ret, new_ret = _retention(ret_decay[e], q, k, v, cos_tab, sin_tab, state_ret, new_ret, e,
                                      reverse=True, o_fwd=o_f, g=g, gn=ret_gn[e])
            u = _conv_module(a, ga, conv_w[e], conv_b[e], conv_ln_g[e], conv_ln_b[e])
            mixes, w_out, w_index = [ret, u], even_w_out, e
        else:
            o = l // 2
            qd, vd = GLA_HEADS * GLA_DK, GLA_HEADS * GLA_DV
            x, q, k, v, r, alr = _inproj(x_src, norm1_g[l], mod[0], mod[1], odd_w_in, o,
                                         (qd, qd, vd, vd, 2 * GLA_RANK))
            zeros = jnp.zeros((GLA_RANK, qd), F32)
            wa_f = jnp.concatenate([gla_w_a2[o, 0], zeros], axis=0)
            wa_b = jnp.concatenate([zeros, gla_w_a2[o, 1]], axis=0)
            o_f, new_gla = _gla(alr, q, k, v, wa_f, gla_b_a2[o, 0].reshape(1, qd), state_gla, new_gla, o,
                                reverse=False)
            y, new_gla = _gla(alr, q, k, v, wa_b, gla_b_a2[o, 1].reshape(1, qd), state_gla, new_gla, o,
                              reverse=True, o_fwd=o_f, r=r, gn=gla_gn[o])
            mixes, w_out, w_index = [y], odd_w_out, o
        x1, h2, logits_t = _outproj(mixes, w_out, w_index, x, mod[2], norm2_g[l], mod[3], mod[4],
                                    router_w[l], router_b[l])
        idx_t, rank_t, gates_t, counts = _route(logits_t)
        block_e, n_valid, next_e, dest_t = _routing_plan(counts, idx_t, rank_t)
        xs = _sc_scatter_rows(h2, dest_t, MOE_ROWS)
        yb = _moe_experts(l, block_e, n_valid, next_e, xs, exp_w_gu, exp_b_gu, exp_w_down, exp_b_down)
        yg = _sc_gather_rows(yb, dest_t.reshape(TOP_K * NT))
        x_src = ("moe", x1, yg, gates_t, mod[5])
    y_prompt, y_sample = _combine_final(*x_src[1:], final_g)
    y_prompt = y_prompt.reshape(BATCH, SEQ, D_MODEL)
    y_sample = y_sample.reshape(DEC_BATCH, DEC_SEQ, D_MODEL)
    return (y_prompt, y_sample, new_ret, new_gla)
```

```python
import functools

import jax
import jax.numpy as jnp
from jax import lax
from jax.experimental import pallas as pl
from jax.experimental.pallas import tpu as pltpu
from jax.experimental.pallas import tpu_sc as plsc

F32 = jnp.float32
BF16 = jnp.bfloat16

D_MODEL = 1024
BATCH = 16
SEQ = 256
DEPTH = 4
DEC_BATCH = 4
DEC_SEQ = 4096
GRID_W = 64
RET_HEADS = 4
RET_DK = 64
RET_DV = 128
RET_CHUNK = 128
CONV_CH = 512
CONV_WIDTH = 31
CONV_PAD = CONV_WIDTH // 2
GLA_HEADS = 4
GLA_DK = 128
GLA_DV = 256
GLA_RANK = 16
GLA_TAU = 16.0
GLA_CHUNK = 64
GLA_SUB = 16
N_EXPERTS = 32
TOP_K = 4
D_FF = 1024
SWIGLU_LIMIT = 7.0
SWIGLU_ALPHA = 1.702
MOE_BLOCK = 128
ROPE_THETA = 10000.0
EPS = 1e-6
N_MOD = 6

RB = 256
NT_PROMPT = BATCH * SEQ
NT = NT_PROMPT + DEC_BATCH * DEC_SEQ
NBLK = NT // RB
N_PROMPT_BLK = NT_PROMPT // RB
SAMPLE_BLK = DEC_SEQ // RB
NSEQ = BATCH + DEC_BATCH
MOD_ROWS = 8
HALO = 16
VMEM_LIMIT = 48 * 1024 * 1024

assert SEQ == RB and DEC_SEQ % RB == 0 and CONV_PAD < HALO


def _seq_of_block(i):
    return jnp.where(i < N_PROMPT_BLK, i, N_PROMPT_BLK + (i - N_PROMPT_BLK) // SAMPLE_BLK)


def _is_first_block(i):
    return jnp.logical_or(i < N_PROMPT_BLK, (i - N_PROMPT_BLK) % SAMPLE_BLK == 0)


def _is_last_block(i):
    return jnp.logical_or(i < N_PROMPT_BLK, (i - N_PROMPT_BLK) % SAMPLE_BLK == SAMPLE_BLK - 1)


def _mod_row(i):
    return jnp.where(i < N_PROMPT_BLK, 0, 1 + (i - N_PROMPT_BLK) // SAMPLE_BLK)


def _rope_block(i):
    return jnp.where(i < N_PROMPT_BLK, 0, 1 + (i - N_PROMPT_BLK) % SAMPLE_BLK)


def _dot(a, b):
    return jnp.dot(a, b, preferred_element_type=F32)


def _dot_nt(a, b):
    return lax.dot_general(a, b, (((1,), (1,)), ((), ())), preferred_element_type=F32)


def _dot_tn(a, b):
    return lax.dot_general(a, b, (((0,), (0,)), ((), ())), preferred_element_type=F32)


def _split2(a):
    hi = a.astype(BF16)
    lo = (a - hi.astype(F32)).astype(BF16)
    return hi, lo


def _dot_hi(a, b):
    a_hi, a_lo = _split2(a)
    b_hi, b_lo = _split2(b)
    return _dot(a_hi, b_hi) + (_dot(a_hi, b_lo) + _dot(a_lo, b_hi))


def _silu(x):
    return x * (1.0 / (1.0 + jnp.exp(-x)))


def _sigmoid(x):
    return 1.0 / (1.0 + jnp.exp(-x))


def _pack_rows(x):
    n = x.shape[1] // 2
    lo = pltpu.bitcast(x[:, :n].astype(BF16).astype(F32), jnp.uint32)
    hi = pltpu.bitcast(x[:, n:].astype(BF16).astype(F32), jnp.uint32)
    return hi | (lo >> 16)


def _unpack_rows(u):
    lo = pltpu.bitcast(u << 16, F32)
    hi = pltpu.bitcast(u & jnp.uint32(0xFFFF0000), F32)
    return lo, hi


def _params(n_axes=1, vmem=VMEM_LIMIT):
    return pltpu.CompilerParams(dimension_semantics=("arbitrary",) * n_axes, vmem_limit_bytes=vmem)


MOD_TN = 1536


def _mod_kernel(c_ref, w_ref, b_ref, o_ref):
    s = _silu(c_ref[...]).astype(BF16)
    o_ref[...] = _dot(s, w_ref[...].astype(BF16)) + b_ref[...]


def _modulation(cvec, w_mod, b_mod):
    n = N_MOD * D_MODEL
    return pl.pallas_call(
        _mod_kernel,
        grid=(DEPTH, n // MOD_TN),
        in_specs=[pl.BlockSpec((MOD_ROWS, D_MODEL), lambda l, j: (0, 0)),
                  pl.BlockSpec((None, D_MODEL, MOD_TN), lambda l, j: (l, 0, j)),
                  pl.BlockSpec((None, 1, MOD_TN), lambda l, j: (l, 0, j))],
        out_specs=pl.BlockSpec((None, MOD_ROWS, MOD_TN), lambda l, j: (l, 0, j)),
        out_shape=jax.ShapeDtypeStruct((DEPTH, MOD_ROWS, n), F32),
        compiler_params=_params(2),
        name="modulation",
    )(cvec, w_mod, b_mod.reshape(DEPTH, 1, n))


def _norm_mod(x, g_ref, shift_ref, scale_ref, row):
    y = x * lax.rsqrt(jnp.mean(x * x, axis=-1, keepdims=True) + EPS) * g_ref[...]
    return y * (1.0 + scale_ref[pl.ds(row, 1), :]) + shift_ref[pl.ds(row, 1), :]


N_SRC = {"split": 2, "moe": 3 + TOP_K}


def _inproj_kernel(*refs, widths, source):
    n_src = N_SRC[source]
    src = refs[:n_src]
    g_ref, shift_ref, scale_ref, w_ref = refs[n_src:n_src + 4]
    outs, w_bf = refs[n_src + 4:-1], refs[-1]
    step = pl.program_id(0)

    @pl.when(step == 0)
    def _():
        w_bf[...] = w_ref[...].astype(BF16)

    row = _mod_row(step)
    if source == "split":
        x = jnp.where(step < N_PROMPT_BLK, src[0][...], src[1][...])
    else:
        x1_ref, yg_refs, gt_ref, gate_ref = src[0], src[1:1 + TOP_K], src[-2], src[-1]
        x = jnp.concatenate(_combined_rows(x1_ref, yg_refs, _gate_columns(gt_ref), gate_ref, row), axis=1)
    outs[0][...] = x
    hb = _norm_mod(x, g_ref, shift_ref, scale_ref, row).astype(BF16)
    off = 0
    for o_ref, width in zip(outs[1:], widths):
        o_ref[...] = _dot(hb, w_bf[:, off:off + width])
        off += width


def _inproj(x, g, shift, scale, w_all, index, widths):
    source, x_args = x[0], list(x[1:])
    n_in = w_all.shape[2]
    row_spec = lambda width: pl.BlockSpec((RB, width), lambda i: (i, 0))
    full = lambda shape: pl.BlockSpec(shape, lambda i: (0,) * len(shape))
    if source == "split":
        x_specs = [pl.BlockSpec((RB, D_MODEL), lambda i: (jnp.minimum(i, N_PROMPT_BLK - 1), 0)),
                   pl.BlockSpec((RB, D_MODEL), lambda i: (jnp.maximum(i - N_PROMPT_BLK, 0), 0))]
    else:
        x1, yg, gates_t, gate = x_args
        x_specs, x_args = _combine_specs(), [x1] + [yg] * TOP_K + [gates_t, gate]
    out_widths = (D_MODEL,) + tuple(widths)
    return pl.pallas_call(
        functools.partial(_inproj_kernel, widths=widths, source=source),
        grid=(NBLK,),
        in_specs=x_specs + [full((1, D_MODEL)), full((MOD_ROWS, D_MODEL)), full((MOD_ROWS, D_MODEL)),
                            pl.BlockSpec((None, D_MODEL, n_in), lambda i: (index, 0, 0))],
        out_specs=[row_spec(width) for width in out_widths],
        out_shape=[jax.ShapeDtypeStruct((NT, width), F32) for width in out_widths],
        scratch_shapes=[pltpu.VMEM((D_MODEL, n_in), BF16)],
        compiler_params=_params(),
        name="inproj",
    )(*x_args, g.reshape(1, D_MODEL), shift, scale, w_all)


RC = RET_CHUNK
RET_PAIR = 2 * RET_DK


def _rope(x, cos, sin_signed):
    lane = lax.broadcasted_iota(jnp.int32, x.shape, 1)
    swapped = jnp.where(lane % 2 == 0, pltpu.roll(x, x.shape[1] - 1, 1), pltpu.roll(x, 1, 1))
    return x * cos + swapped * sin_signed


def _ret_kernel(decay_ref, q_ref, k_ref, v_ref, cos_ref, sin_ref, s0_ref, acc_ref, *rest, reverse):
    del acc_ref
    if reverse:
        of_ref, g_ref, gn_ref, o_ref, sfin_ref, st_ref, dm_ref, dq_ref, dk_ref, ds_ref = rest
    else:
        o_ref, sfin_ref, st_ref, dm_ref, dq_ref, dk_ref, ds_ref = rest
    step = pl.program_id(0)
    blk = NBLK - 1 - step if reverse else step
    direction = 1 if reverse else 0

    @pl.when(step == 0)
    def _():
        row = lax.broadcasted_iota(jnp.int32, (RC, RC), 0).astype(F32)
        col = lax.broadcasted_iota(jnp.int32, (RC, RC), 1).astype(F32)
        for h in range(RET_HEADS):
            lg = -jnp.exp(jnp.full((RC, RC), decay_ref[direction, h], F32))
            if reverse:
                diff = col - row
                mask = diff > 0
                q_pow = RC - row
                k_pow = row
            else:
                diff = row - col
                mask = diff >= 0
                q_pow = row + 1.0
                k_pow = RC - 1.0 - row
            dm_ref[h] = jnp.where(mask, jnp.exp(lg * jnp.where(mask, diff, 0.0)), 0.0)
            dq_ref[h] = jnp.exp(lg * q_pow)
            dk_ref[h] = jnp.exp(lg * k_pow)
            ds_ref[h] = jnp.exp(lg * RC)

    starts = _is_last_block(blk) if reverse else _is_first_block(blk)

    is_prompt = blk < N_PROMPT_BLK

    @pl.when(starts)
    def _():
        st_ref[...] = jnp.zeros_like(st_ref)

    @pl.when(jnp.logical_and(starts, jnp.logical_not(is_prompt)))
    def _():
        for h in range(RET_HEADS):
            off = (h % 2) * RET_DK
            st_ref[h, off:off + RET_DK, :] = s0_ref[h]

    lane = lax.broadcasted_iota(jnp.int32, (1, RET_PAIR), 1)
    chunks = range(RB // RC)
    chunk_order = list(reversed(chunks) if reverse else chunks)
    units = [(h, c) for h in range(RET_HEADS) for c in chunk_order]
    rows_of = lambda c: slice(c * RC, (c + 1) * RC)
    vcols = lambda h: slice(h * RET_DV, (h + 1) * RET_DV)
    roped = {}
    for p in range(RET_HEADS // 2):
        cols = slice(p * RET_PAIR, (p + 1) * RET_PAIR)
        for c in chunk_order:
            rows = rows_of(c)
            cos, sin = cos_ref[rows, :], sin_ref[rows, :]
            roped[p, c] = (_rope(q_ref[rows, cols], cos, sin),
                           _rope(k_ref[rows, cols] * (RET_DK ** -0.5), cos, sin))
    qm_u, vh_u, att_u, kd_u = {}, {}, {}, {}
    for h, c in units:
        head_mask = (lane // RET_DK == h % 2).astype(F32)
        q2, k2 = roped[h // 2, c]
        vh_u[h, c] = v_ref[rows_of(c), vcols(h)].astype(BF16)
        qm_u[h, c] = (q2 * head_mask).astype(BF16)
        km = k2 * head_mask
        att_u[h, c] = (_dot_nt(qm_u[h, c], km.astype(BF16)) * dm_ref[h]).astype(BF16)
        kd_u[h, c] = (km * dk_ref[h]).astype(BF16)
    intra_u = {u: _dot(att_u[u], vh_u[u]) for u in units}
    delta_u = {u: _dot_tn(kd_u[u], vh_u[u]) for u in units}
    state_u = {}
    for h in range(RET_HEADS):
        st = st_ref[h]
        for c in chunk_order:
            state_u[h, c] = st.astype(BF16)
            st = st * ds_ref[h] + delta_u[h, c]
        st_ref[h] = st
    for h, c in units:
        rows, out_cols = rows_of(c), vcols(h)
        o = intra_u[h, c] + _dot(qm_u[h, c], state_u[h, c]) * dq_ref[h]
        if reverse:
            o = o + of_ref[rows, out_cols]
            o = o * lax.rsqrt(jnp.mean(o * o, axis=-1, keepdims=True) + EPS)
            o = o * gn_ref[:, out_cols] * _silu(g_ref[rows, out_cols])
            o_ref[rows, out_cols] = o.astype(o_ref.dtype)
        else:
            o_ref[rows, out_cols] = o

    ends = _is_first_block(blk) if reverse else _is_last_block(blk)

    @pl.when(jnp.logical_and(ends, is_prompt))
    def _():
        for h in range(RET_HEADS):
            off = (h % 2) * RET_DK
            sfin_ref[h] = st_ref[h, off:off + RET_DK, :]


def _sample_seq(blk):
    return jnp.clip(_seq_of_block(blk) - BATCH, 0, DEC_BATCH - 1)


def _prompt_seq(blk):
    return jnp.minimum(_seq_of_block(blk), BATCH - 1)


def _retention(decay, q, k, v, cos_tab, sin_tab, state_in, state_out, layer, *, reverse,
               o_fwd=None, g=None, gn=None):
    bmap = (lambda j: NBLK - 1 - j) if reverse else (lambda j: j)
    direction = 1 if reverse else 0
    qd, vd = RET_HEADS * RET_DK, RET_HEADS * RET_DV
    row_spec = lambda width: pl.BlockSpec((RB, width), lambda j: (bmap(j), 0))
    state_blk = (None, None, None, RET_HEADS, RET_DK, RET_DV)
    in_specs = [pl.BlockSpec(memory_space=pltpu.SMEM), row_spec(qd), row_spec(qd), row_spec(vd),
                pl.BlockSpec((RB, RET_PAIR), lambda j: (_rope_block(bmap(j)), 0)),
                pl.BlockSpec((RB, RET_PAIR), lambda j: (_rope_block(bmap(j)), 0)),
                pl.BlockSpec(state_blk, lambda j: (_sample_seq(bmap(j)), layer, direction, 0, 0, 0)),
                pl.BlockSpec(memory_space=pl.ANY)]
    args = [decay, q, k, v, cos_tab, sin_tab, state_in, state_out]
    if reverse:
        in_specs += [row_spec(vd), row_spec(vd), pl.BlockSpec((1, vd), lambda j: (0, 0))]
        args += [o_fwd, g, gn.reshape(1, vd)]
    tile = pltpu.VMEM((RET_HEADS, RC, RC), F32)
    return pl.pallas_call(
        functools.partial(_ret_kernel, reverse=reverse),
        grid=(NBLK,),
        in_specs=in_specs,
        out_specs=[row_spec(vd),
                   pl.BlockSpec(state_blk, lambda j: (_prompt_seq(bmap(j)), layer, direction, 0, 0, 0))],
        out_shape=[jax.ShapeDtypeStruct((NT, vd), BF16 if reverse else F32),
                   jax.ShapeDtypeStruct(state_out.shape, F32)],
        input_output_aliases={7: 1},
        scratch_shapes=[tile, tile, tile, tile, tile],
        compiler_params=_params(),
        name="retention_bwd" if reverse else "retention_fwd",
    )(*args)


CONV_RT = 32
CONV_CT = 128
CONV_SPAN = RB + 2 * HALO - 8


def _conv_kernel(a_ref, ga_ref, ap_ref, gap_ref, an_ref, gan_ref, cw_ref, cb_ref, lng_ref, lnb_ref,
                 o_ref, u_ref, y_ref, us_ref):
    blk = pl.program_id(0)
    keep_prev = jnp.where(_is_first_block(blk), 0.0, 1.0)
    keep_next = jnp.where(_is_last_block(blk), 0.0, 1.0)
    u_ref[0:HALO, :] = ap_ref[...] * _sigmoid(gap_ref[...]) * keep_prev
    u_ref[HALO:HALO + RB, :] = a_ref[...] * _sigmoid(ga_ref[...])
    u_ref[HALO + RB:HALO + RB + HALO, :] = an_ref[...] * _sigmoid(gan_ref[...]) * keep_next
    for r in range(1, 8):
        us_ref[r - 1] = u_ref[r:r + CONV_SPAN, :]
    for ct in range(CONV_CH // CONV_CT):
        cols = slice(ct * CONV_CT, (ct + 1) * CONV_CT)
        for rt in range(RB // CONV_RT):
            acc = jnp.zeros((CONV_RT, CONV_CT), F32)
            for w in range(CONV_WIDTH):
                tiles, r = divmod(HALO - CONV_PAD + w, 8)
                base = rt * CONV_RT + 8 * tiles
                src = u_ref if r == 0 else us_ref.at[r - 1]
                acc = acc + src[base:base + CONV_RT, cols] * cw_ref[w:w + 1, cols]
            y_ref[rt * CONV_RT:(rt + 1) * CONV_RT, cols] = acc + cb_ref[:, cols]
    y = y_ref[...]
    mu = jnp.mean(y, axis=-1, keepdims=True)
    var = jnp.mean(jnp.square(y - mu), axis=-1, keepdims=True)
    o_ref[...] = _silu((y - mu) * lax.rsqrt(var + EPS) * lng_ref[...] + lnb_ref[...]).astype(o_ref.dtype)


def _conv_module(a, ga, cw, cb, lng, lnb):
    per_blk = RB // HALO
    n_halo = NT // HALO
    row_spec = pl.BlockSpec((RB, CONV_CH), lambda i: (i, 0))
    prev_spec = pl.BlockSpec((HALO, CONV_CH), lambda i: (jnp.maximum(i * per_blk - 1, 0), 0))
    next_spec = pl.BlockSpec((HALO, CONV_CH), lambda i: (jnp.minimum((i + 1) * per_blk, n_halo - 1), 0))
    vec = pl.BlockSpec((1, CONV_CH), lambda i: (0, 0))
    return pl.pallas_call(
        _conv_kernel,
        grid=(NBLK,),
        in_specs=[row_spec, row_spec, prev_spec, prev_spec, next_spec, next_spec,
                  pl.BlockSpec((CONV_WIDTH, CONV_CH), lambda i: (0, 0)), vec, vec, vec],
        out_specs=row_spec,
        out_shape=jax.ShapeDtypeStruct((NT, CONV_CH), BF16),
        scratch_shapes=[pltpu.VMEM((RB + 2 * HALO, CONV_CH), F32), pltpu.VMEM((RB, CONV_CH), F32),
                        pltpu.VMEM((7, CONV_SPAN, CONV_CH), F32)],
        compiler_params=_params(),
        name="conv_module",
    )(a, ga, a, ga, a, ga, cw, cb.reshape(1, -1), lng.reshape(1, -1), lnb.reshape(1, -1))


GC = GLA_CHUNK
GLA_NSUB = GC // GLA_SUB


def _split3(a):
    p1 = a.astype(BF16)
    r1 = a - p1.astype(F32)
    p2 = r1.astype(BF16)
    p3 = (r1 - p2.astype(F32)).astype(BF16)
    return p1, p2, p3


def _gla_kernel(alr_ref, q_ref, k_ref, v_ref, wa_ref, ba_ref, s0_ref, acc_ref, *rest, reverse):
    del acc_ref
    if reverse:
        of_ref, r_ref, gn_ref, o_ref, sfin_ref, st_ref, b_ref, tri_ref, ob_ref = rest
    else:
        o_ref, sfin_ref, st_ref, b_ref, tri_ref = rest
        ob_ref = o_ref
    step = pl.program_id(0)
    blk = NBLK - 1 - step if reverse else step
    starts = _is_last_block(blk) if reverse else _is_first_block(blk)
    is_prompt = blk < N_PROMPT_BLK

    @pl.when(jnp.logical_and(starts, is_prompt))
    def _():
        st_ref[...] = jnp.zeros_like(st_ref)

    @pl.when(jnp.logical_and(starts, jnp.logical_not(is_prompt)))
    def _():
        for h in range(GLA_HEADS):
            st_ref[h] = s0_ref[h].T

    @pl.when(step == 0)
    def _():
        row = lax.broadcasted_iota(jnp.int32, (RB, RB), 0)
        col = lax.broadcasted_iota(jnp.int32, (RB, RB), 1)
        ordered = col >= row if reverse else col <= row
        tri_ref[...] = jnp.where(jnp.logical_and(row // GC == col // GC, ordered), 1.0, 0.0).astype(BF16)

    heads = range(GLA_HEADS)
    hcols = [slice(h * GLA_DK, (h + 1) * GLA_DK) for h in heads]
    alr = alr_ref[...]
    z = [_dot_hi(alr, wa_ref[:, hcols[h]]) + ba_ref[:, hcols[h]] for h in heads]
    log_a = [(jnp.minimum(zh, 0.0) - jnp.log(1.0 + jnp.exp(-jnp.abs(zh)))) * (1.0 / GLA_TAU) for zh in z]
    parts = [_split3(la) for la in log_a]
    tri = tri_ref[...]
    for h in heads:
        g1, g2, g3 = parts[h]
        b_ref[:, hcols[h]] = _dot(tri, g1) + (_dot(tri, g2) + _dot(tri, g3))

    c_row = lax.broadcasted_iota(jnp.int32, (GC, 1), 0)
    a_row = lax.broadcasted_iota(jnp.int32, (GC, GC), 0)
    a_col = lax.broadcasted_iota(jnp.int32, (GC, GC), 1)
    att_mask = a_col > a_row if reverse else a_col <= a_row
    chunks = range(RB // GC)
    chunk_order = list(reversed(chunks) if reverse else chunks)
    units = [(h, c) for h in range(GLA_HEADS) for c in chunk_order]
    kcols = lambda h: slice(h * GLA_DK, (h + 1) * GLA_DK)
    vcols = lambda h: slice(h * GLA_DV, (h + 1) * GLA_DV)
    rows_of = lambda c: slice(c * GC, (c + 1) * GC)

    vh_u, qe_u, ke_u, decay_u, att_u = {}, {}, {}, {}, {}
    for h, c in units:
        rows = rows_of(c)
        b = b_ref[rows, kcols(h)]
        qh = q_ref[rows, kcols(h)] * (GLA_DK ** -0.5)
        kh = k_ref[rows, kcols(h)]
        vh_u[h, c] = v_ref[rows, vcols(h)].astype(BF16)
        edge = b[0:1, :] if reverse else b[GC - 1:GC, :]
        bounds = []
        for s in range(GLA_NSUB):
            if reverse:
                hi = (s + 1) * GLA_SUB
                bounds.append(b[hi:hi + 1, :] if s < GLA_NSUB - 1 else jnp.zeros((1, GLA_DK), F32))
            else:
                lo = s * GLA_SUB
                bounds.append(b[lo - 1:lo, :] if s > 0 else jnp.zeros((1, GLA_DK), F32))
        own = jnp.concatenate([jnp.broadcast_to(bd, (GLA_SUB, GLA_DK)) for bd in bounds], axis=0)
        q_own = qh * jnp.exp(b - own)
        q_parts, k_parts = [], []
        for s, bd in enumerate(bounds):
            q_parts.append(jnp.where(c_row // GLA_SUB == s, q_own, 0.0))
            reach = c_row >= s * GLA_SUB if reverse else c_row < (s + 1) * GLA_SUB
            k_parts.append(kh * jnp.exp(jnp.where(reach, bd - b, -jnp.inf)))
        q_bd = jnp.concatenate(q_parts, axis=1).astype(BF16)
        k_cat = jnp.concatenate(k_parts, axis=1).astype(BF16)
        att_u[h, c] = jnp.where(att_mask, _dot_nt(q_bd, k_cat), 0.0).astype(BF16)
        qe_u[h, c] = (qh * jnp.exp(b)).astype(BF16)
        ke_u[h, c] = (kh * jnp.exp(edge - b)).astype(BF16)
        decay_u[h, c] = jnp.exp(edge)
    intra_u = {u: _dot(att_u[u], vh_u[u]) for u in units}
    delta_u = {u: _dot_tn(vh_u[u], ke_u[u]) for u in units}
    state_u = {}
    for h in range(GLA_HEADS):
        st = st_ref[h]
        for c in chunk_order:
            state_u[h, c] = st.astype(BF16)
            st = st * decay_u[h, c] + delta_u[h, c]
        st_ref[h] = st
    for h, c in units:
        ob_ref[rows_of(c), vcols(h)] = intra_u[h, c] + _dot_nt(qe_u[h, c], state_u[h, c])

    if reverse:
        for h in range(GLA_HEADS):
            cols = vcols(h)
            o = ob_ref[:, cols] + of_ref[:, cols]
            o = o * lax.rsqrt(jnp.mean(o * o, axis=-1, keepdims=True) + EPS)
            o_ref[:, cols] = (o * gn_ref[:, cols] * _silu(r_ref[:, cols])).astype(o_ref.dtype)

    ends = _is_first_block(blk) if reverse else _is_last_block(blk)

    @pl.when(jnp.logical_and(ends, is_prompt))
    def _():
        for h in range(GLA_HEADS):
            sfin_ref[h] = st_ref[h].T


def _gla(alr, q, k, v, wa, ba, state_in, state_out, layer, *, reverse, o_fwd=None, r=None, gn=None):
    bmap = (lambda j: NBLK - 1 - j) if reverse else (lambda j: j)
    direction = 1 if reverse else 0
    qd, vd = GLA_HEADS * GLA_DK, GLA_HEADS * GLA_DV
    row_spec = lambda width: pl.BlockSpec((RB, width), lambda j: (bmap(j), 0))
    state_blk = (None, None, None, GLA_HEADS, GLA_DK, GLA_DV)
    in_specs = [row_spec(2 * GLA_RANK), row_spec(qd), row_spec(qd), row_spec(vd),
                pl.BlockSpec((2 * GLA_RANK, qd), lambda j: (0, 0)),
                pl.BlockSpec((1, qd), lambda j: (0, 0)),
                pl.BlockSpec(state_blk, lambda j: (_sample_seq(bmap(j)), layer, direction, 0, 0, 0)),
                pl.BlockSpec(memory_space=pl.ANY)]
    args = [alr, q, k, v, wa, ba, state_in, state_out]
    scratch = [pltpu.VMEM((GLA_HEADS, GLA_DV, GLA_DK), F32), pltpu.VMEM((RB, qd), F32),
               pltpu.VMEM((RB, RB), BF16)]
    if reverse:
        in_specs += [row_spec(vd), row_spec(vd), pl.BlockSpec((1, vd), lambda j: (0, 0))]
        args += [o_fwd, r, gn.reshape(1, vd)]
        scratch += [pltpu.VMEM((RB, vd), F32)]
    return pl.pallas_call(
        functools.partial(_gla_kernel, reverse=reverse),
        grid=(NBLK,),
        in_specs=in_specs,
        out_specs=[row_spec(vd),
                   pl.BlockSpec(state_blk, lambda j: (_prompt_seq(bmap(j)), layer, direction, 0, 0, 0))],
        out_shape=[jax.ShapeDtypeStruct((NT, vd), BF16 if reverse else F32),
                   jax.ShapeDtypeStruct(state_out.shape, F32)],
        input_output_aliases={7: 1},
        scratch_shapes=scratch,
        compiler_params=_params(),
        name="gla_bwd" if reverse else "gla_fwd",
    )(*args)


OUT_SUB = 128


def _outproj_kernel(*refs, n_mix):
    mix_refs = refs[:n_mix]
    (w_ref, x_ref, gate_ref, g2_ref, shift_ref, scale_ref, rw_ref, rb_ref,
     x1_ref, h2_ref, logit_ref, w_bf) = refs[n_mix:]
    step = pl.program_id(0)

    @pl.when(step == 0)
    def _():
        w_bf[...] = w_ref[...].astype(BF16)

    row = _mod_row(step)
    r_hi, r_lo = _split2(rw_ref[...])
    groups = [slice(p * OUT_SUB, (p + 1) * OUT_SUB) for p in range(RB // OUT_SUB)]
    mixed = []
    for rows in groups:
        m, off = None, 0
        for mix_ref in mix_refs:
            width = mix_ref.shape[1]
            part = _dot(mix_ref[rows, :], w_bf[off:off + width, :])
            m = part if m is None else m + part
            off += width
        mixed.append(m)
    normed = []
    for rows, m in zip(groups, mixed):
        x1 = x_ref[rows, :] + gate_ref[pl.ds(row, 1), :] * m
        x1_ref[rows, :] = x1
        h2 = _norm_mod(x1, g2_ref, shift_ref, scale_ref, row)
        h2_ref[rows, :] = _pack_rows(h2)
        normed.append(_split2(h2))
    for rows, (h_hi, h_lo) in zip(groups, normed):
        logit_ref[:, rows] = _dot_nt(r_hi, h_hi) + (_dot_nt(r_hi, h_lo) + _dot_nt(r_lo, h_hi)) + rb_ref[...]


def _outproj(mixes, w_all, index, x, gate, g2, shift, scale, rw, rb):
    n_mix = len(mixes)
    row_spec = lambda width: pl.BlockSpec((RB, width), lambda i: (i, 0))
    full = lambda shape: pl.BlockSpec(shape, lambda i: (0,) * len(shape))
    mod_spec = full((MOD_ROWS, D_MODEL))
    n_mixed = w_all.shape[1]
    return pl.pallas_call(
        functools.partial(_outproj_kernel, n_mix=n_mix),
        grid=(NBLK,),
        in_specs=[row_spec(m.shape[1]) for m in mixes]
        + [pl.BlockSpec((None, n_mixed, D_MODEL), lambda i: (index, 0, 0)),
           row_spec(D_MODEL), mod_spec, full((1, D_MODEL)), mod_spec, mod_spec,
           full((N_EXPERTS, D_MODEL)), full((N_EXPERTS, 1))],
        out_specs=[row_spec(D_MODEL), row_spec(D_MODEL // 2), pl.BlockSpec((N_EXPERTS, RB), lambda i: (0, i))],
        out_shape=[jax.ShapeDtypeStruct((NT, D_MODEL), F32), jax.ShapeDtypeStruct((NT, D_MODEL // 2), jnp.uint32),
                   jax.ShapeDtypeStruct((N_EXPERTS, NT), F32)],
        scratch_shapes=[pltpu.VMEM((n_mixed, D_MODEL), BF16)],
        compiler_params=_params(),
        name="outproj",
    )(*mixes, w_all, x, gate, g2.reshape(1, D_MODEL), shift, scale, rw.T, rb.reshape(N_EXPERTS, 1))


ROUTE_BLK = 2048
ROUTE_SUB = 256


def _route_kernel(lg_ref, idx_ref, rank_ref, gt_ref, cnt_ref, carry_ref):
    @pl.when(pl.program_id(0) == 0)
    def _():
        carry_ref[...] = jnp.zeros_like(carry_ref)

    logits = lg_ref[...]
    eid = lax.broadcasted_iota(jnp.int32, logits.shape, 0).astype(F32)
    work = logits
    onehots, top_vals = [], []
    for kk in range(TOP_K):
        top = jnp.max(work, axis=0, keepdims=True)
        first = jnp.min(jnp.where(work == top, eid, float(N_EXPERTS)), axis=0, keepdims=True)
        onehot = eid == first
        idx_ref[kk:kk + 1, :] = first.astype(jnp.int32)
        onehots.append(onehot)
        top_vals.append(top)
        work = jnp.where(onehot, -jnp.inf, work)
    exps = [jnp.exp(v - top_vals[0]) for v in top_vals]
    denom = exps[0]
    for e in exps[1:]:
        denom = denom + e
    gt_ref[...] = jnp.zeros_like(gt_ref)
    for kk in range(TOP_K):
        gt_ref[kk:kk + 1, :] = exps[kk] / denom

    sel = jnp.zeros(logits.shape, F32)
    for onehot in onehots:
        sel = sel + jnp.where(onehot, 1.0, 0.0)
    sel = sel.astype(BF16)
    r_i = lax.broadcasted_iota(jnp.int32, (ROUTE_SUB, ROUTE_SUB), 0)
    c_i = lax.broadcasted_iota(jnp.int32, (ROUTE_SUB, ROUTE_SUB), 1)
    before = jnp.where(r_i < c_i, 1.0, 0.0).astype(BF16)
    ones = jnp.ones((ROUTE_SUB, ROUTE_SUB), BF16)
    carry = carry_ref[...]
    for s in range(ROUTE_BLK // ROUTE_SUB):
        cols = slice(s * ROUTE_SUB, (s + 1) * ROUTE_SUB)
        pos = _dot(sel[:, cols], before) + carry
        for kk in range(TOP_K):
            rank = jnp.sum(jnp.where(onehots[kk][:, cols], pos, 0.0), axis=0, keepdims=True)
            rank_ref[kk:kk + 1, cols] = rank.astype(jnp.int32)
        carry = carry + _dot(sel[:, cols], ones)
    carry_ref[...] = carry
    cnt_ref[...] = carry.astype(jnp.int32)


def _route(logits_t):
    col_spec = lambda rows: pl.BlockSpec((rows, ROUTE_BLK), lambda i: (0, i))
    return pl.pallas_call(
        _route_kernel,
        grid=(NT // ROUTE_BLK,),
        in_specs=[col_spec(N_EXPERTS)],
        out_specs=[col_spec(TOP_K), col_spec(TOP_K), col_spec(8),
                   pl.BlockSpec((N_EXPERTS, ROUTE_SUB), lambda i: (0, 0))],
        out_shape=[jax.ShapeDtypeStruct((TOP_K, NT), jnp.int32), jax.ShapeDtypeStruct((TOP_K, NT), jnp.int32),
                   jax.ShapeDtypeStruct((8, NT), F32), jax.ShapeDtypeStruct((N_EXPERTS, ROUTE_SUB), jnp.int32)],
        scratch_shapes=[pltpu.VMEM((N_EXPERTS, ROUTE_SUB), F32)],
        compiler_params=_params(),
        name="route",
    )(logits_t)


TM = 1024
TM_SUB = 256
MOE_NBLK = NT * TOP_K // TM + N_EXPERTS
MOE_ROWS = MOE_NBLK * TM
HALF = D_MODEL // 2


def _moe_kernel(be_ref, nv_ref, nx_ref, x_ref, wgu_hbm, bgu_ref, wd_hbm, bd_ref, y_ref,
                wgu_st, wd_st, wgu_bf, wd_bf, sems, *, layer):
    i = pl.program_id(0)
    n_valid = nv_ref[i]

    def fetch(e):
        return (pltpu.make_async_copy(wgu_hbm.at[layer, e], wgu_st, sems.at[0]),
                pltpu.make_async_copy(wd_hbm.at[layer, e], wd_st, sems.at[1]))

    @pl.when(i == 0)
    def _():
        for cp in fetch(be_ref[0]):
            cp.start()

    @pl.when(n_valid > 0)
    def _():
        e = be_ref[i]
        changed = jnp.logical_or(i == 0, e != be_ref[jnp.maximum(i - 1, 0)])

        @pl.when(changed)
        def _():
            for cp in fetch(e):
                cp.wait()
            wgu_bf[...] = wgu_st[...].astype(BF16)
            wd_bf[...] = wd_st[...].astype(BF16)
            nxt = nx_ref[e]

            @pl.when(nxt >= 0)
            def _():
                for cp in fetch(nxt):
                    cp.start()

    for p in range(TM // TM_SUB):
        @pl.when(n_valid > p * TM_SUB)
        def _():
            rows = slice(p * TM_SUB, (p + 1) * TM_SUB)
            row_id = lax.broadcasted_iota(jnp.int32, (TM_SUB, 1), 0) + p * TM_SUB
            x_lo, x_hi = _unpack_rows(jnp.where(row_id < n_valid, x_ref[rows, :], jnp.uint32(0)))
            gu = (_dot(x_lo.astype(BF16), wgu_bf[:HALF, :]) + _dot(x_hi.astype(BF16), wgu_bf[HALF:, :])
                  + bgu_ref[...])
            gate = jnp.minimum(gu[:, :D_FF], SWIGLU_LIMIT)
            up = jnp.clip(gu[:, D_FF:], -SWIGLU_LIMIT, SWIGLU_LIMIT)
            hdn = gate * _sigmoid(SWIGLU_ALPHA * gate) * (up + 1.0)
            y_ref[rows, :] = _pack_rows(_dot(hdn.astype(BF16), wd_bf[...]) + bd_ref[...])


def _moe_experts(layer, block_e, n_valid, next_e, xs, w_gu, b_gu, w_down, b_down):
    grid_spec = pltpu.PrefetchScalarGridSpec(
        num_scalar_prefetch=3,
        grid=(MOE_NBLK,),
        in_specs=[pl.BlockSpec((TM, HALF), lambda i, be, nv, nx: (i, 0)),
                  pl.BlockSpec(memory_space=pl.ANY),
                  pl.BlockSpec((None, None, 1, 2 * D_FF), lambda i, be, nv, nx: (layer, be[i], 0, 0)),
                  pl.BlockSpec(memory_space=pl.ANY),
                  pl.BlockSpec((None, None, 1, D_MODEL), lambda i, be, nv, nx: (layer, be[i], 0, 0))],
        out_specs=pl.BlockSpec((TM, HALF), lambda i, be, nv, nx: (i, 0)),
        scratch_shapes=[pltpu.VMEM((D_MODEL, 2 * D_FF), F32), pltpu.VMEM((D_FF, D_MODEL), F32),
                        pltpu.VMEM((D_MODEL, 2 * D_FF), BF16), pltpu.VMEM((D_FF, D_MODEL), BF16),
                        pltpu.SemaphoreType.DMA((2,))],
    )
    return pl.pallas_call(
        functools.partial(_moe_kernel, layer=layer),
        grid_spec=grid_spec,
        out_shape=jax.ShapeDtypeStruct((MOE_ROWS, HALF), jnp.uint32),
        compiler_params=_params(),
        name="moe_experts",
    )(block_e, n_valid, next_e, xs, w_gu, b_gu.reshape(DEPTH, N_EXPERTS, 1, -1), w_down,
      b_down.reshape(DEPTH, N_EXPERTS, 1, -1))


SC_WORKERS = 32
SC_WIN = 64


def _sc_mesh():
    return plsc.VectorSubcoreMesh(core_axis_name="core", subcore_axis_name="subcore")


def _sc_worker():
    return lax.axis_index("core") * (SC_WORKERS // 2) + lax.axis_index("subcore")


def _sc_scatter_rows(x, dest_t, n_rows):
    n, width = x.shape
    kk = dest_t.shape[0]
    per = n // SC_WORKERS
    n_win = per // SC_WIN
    assert per * SC_WORKERS == n and n_win * SC_WIN == per and n_win % 2 == 0

    @pl.kernel(out_type=jax.ShapeDtypeStruct((n_rows, width), x.dtype), mesh=_sc_mesh(),
               scratch_types=[pltpu.VMEM((kk, per), jnp.int32), pltpu.VMEM((SC_WIN, width), x.dtype),
                              pltpu.VMEM((SC_WIN, width), x.dtype), pltpu.SemaphoreType.DMA((4,))])
    def scatter(x_hbm, i_hbm, o_hbm, idx_v, buf0, buf1, sems):
        base = _sc_worker() * per
        pltpu.sync_copy(i_hbm.at[:, pl.ds(base, per)], idx_v)

        def get(j, buf, s):
            return pltpu.make_async_copy(x_hbm.at[pl.ds(base + j * SC_WIN, SC_WIN)], buf, sems.at[s])

        def put(j, q, buf, s):
            return pltpu.make_async_copy(buf, o_hbm.at[idx_v.at[q, pl.ds(j * SC_WIN, SC_WIN)]], sems.at[s])

        get(0, buf0, 0).start()

        @pl.loop(0, n_win, step=2)
        def _(j):
            get(j, buf0, 0).wait()

            @pl.when(j > 0)
            def _():
                for q in range(kk):
                    put(j - 1, q, buf1, 3).wait()

            get(j + 1, buf1, 1).start()
            for q in range(kk):
                put(j, q, buf0, 2).start()
            get(j + 1, buf1, 1).wait()
            for q in range(kk):
                put(j, q, buf0, 2).wait()

            @pl.when(j + 2 < n_win)
            def _():
                get(j + 2, buf0, 0).start()

            for q in range(kk):
                put(j + 1, q, buf1, 3).start()

        for q in range(kk):
            put(n_win - 1, q, buf1, 3).wait()

    return scatter(x, dest_t)


def _sc_gather_rows(y, idx):
    n = idx.shape[0]
    width = y.shape[1]
    per = n // SC_WORKERS
    n_win = per // SC_WIN
    assert per * SC_WORKERS == n and n_win * SC_WIN == per and n_win % 2 == 0

    @pl.kernel(out_type=jax.ShapeDtypeStruct((n, width), y.dtype), mesh=_sc_mesh(),
               scratch_types=[pltpu.VMEM((per,), jnp.int32), pltpu.VMEM((SC_WIN, width), y.dtype),
                              pltpu.VMEM((SC_WIN, width), y.dtype), pltpu.SemaphoreType.DMA((4,))])
    def gather(y_hbm, i_hbm, o_hbm, idx_v, buf0, buf1, sems):
        base = _sc_worker() * per
        pltpu.sync_copy(i_hbm.at[pl.ds(base, per)], idx_v)

        def get(j, buf, s):
            return pltpu.make_async_copy(y_hbm.at[idx_v.at[pl.ds(j * SC_WIN, SC_WIN)]], buf, sems.at[s])

        def put(j, buf, s):
            return pltpu.make_async_copy(buf, o_hbm.at[pl.ds(base + j * SC_WIN, SC_WIN)], sems.at[s])

        get(0, buf0, 0).start()

        @pl.loop(0, n_win, step=2)
        def _(j):
            get(j, buf0, 0).wait()

            @pl.when(j > 0)
            def _():
                put(j - 1, buf1, 3).wait()

            get(j + 1, buf1, 1).start()
            put(j, buf0, 2).start()
            get(j + 1, buf1, 1).wait()
            put(j, buf0, 2).wait()

            @pl.when(j + 2 < n_win)
            def _():
                get(j + 2, buf0, 0).start()

            put(j + 1, buf1, 3).start()

        put(n_win - 1, buf1, 3).wait()

    return gather(y, idx)


def _gate_columns(gt_ref):
    r_i = lax.broadcasted_iota(jnp.int32, (RB, RB), 0)
    c_i = lax.broadcasted_iota(jnp.int32, (RB, RB), 1)
    eye = jnp.where(r_i == c_i, 1.0, 0.0).astype(BF16)
    g1, g2, g3 = _split3(gt_ref[...])
    return _dot_nt(eye, g1) + (_dot_nt(eye, g2) + _dot_nt(eye, g3))


def _combined_rows(x1_ref, yg_refs, gw, gate_ref, row, rows=slice(None)):
    acc_lo, acc_hi = None, None
    for kk in range(TOP_K):
        y_lo, y_hi = _unpack_rows(yg_refs[kk][rows, :])
        w = gw[rows, kk:kk + 1]
        acc_lo = y_lo * w if acc_lo is None else acc_lo + y_lo * w
        acc_hi = y_hi * w if acc_hi is None else acc_hi + y_hi * w
    x_lo = x1_ref[rows, :HALF] + gate_ref[pl.ds(row, 1), :HALF] * acc_lo
    x_hi = x1_ref[rows, HALF:] + gate_ref[pl.ds(row, 1), HALF:] * acc_hi
    return x_lo, x_hi


def _combine_specs():
    slot_spec = lambda k: pl.BlockSpec((RB, HALF), lambda i: (k * NBLK + i, 0))
    return ([pl.BlockSpec((RB, D_MODEL), lambda i: (i, 0))] + [slot_spec(k) for k in range(TOP_K)]
            + [pl.BlockSpec((8, RB), lambda i: (0, i)), pl.BlockSpec((MOD_ROWS, D_MODEL), lambda i: (0, 0))])


def _combine_final_kernel(x1_ref, *rest):
    yg_refs = rest[:TOP_K]
    gt_ref, gate_ref, fg_ref, op_ref, os_ref = rest[TOP_K:]
    i = pl.program_id(0)
    x_lo, x_hi = _combined_rows(x1_ref, yg_refs, _gate_columns(gt_ref), gate_ref, _mod_row(i))
    ms = (jnp.sum(x_lo * x_lo, axis=-1, keepdims=True) + jnp.sum(x_hi * x_hi, axis=-1, keepdims=True)) / D_MODEL
    scale = lax.rsqrt(ms + EPS)

    @pl.when(i < N_PROMPT_BLK)
    def _():
        op_ref[:, :HALF] = x_lo * scale * fg_ref[:, :HALF]
        op_ref[:, HALF:] = x_hi * scale * fg_ref[:, HALF:]

    @pl.when(i >= N_PROMPT_BLK)
    def _():
        os_ref[:, :HALF] = x_lo * scale * fg_ref[:, :HALF]
        os_ref[:, HALF:] = x_hi * scale * fg_ref[:, HALF:]


def _combine_final(x1, yg, gates_t, gate, final_g):
    return pl.pallas_call(
        _combine_final_kernel,
        grid=(NBLK,),
        in_specs=_combine_specs() + [pl.BlockSpec((1, D_MODEL), lambda i: (0, 0))],
        out_specs=[pl.BlockSpec((RB, D_MODEL), lambda i: (jnp.minimum(i, N_PROMPT_BLK - 1), 0)),
                   pl.BlockSpec((RB, D_MODEL), lambda i: (jnp.maximum(i - N_PROMPT_BLK, 0), 0))],
        out_shape=[jax.ShapeDtypeStruct((NT_PROMPT, D_MODEL), F32),
                   jax.ShapeDtypeStruct((NT - NT_PROMPT, D_MODEL), F32)],
        compiler_params=_params(),
        name="moe_combine_final",
    )(x1, *([yg] * TOP_K), gates_t, gate, final_g.reshape(1, D_MODEL))


def _routing_plan(counts, idx_t, rank_t):
    counts = counts[:, 0]
    padded = (counts + TM - 1) // TM * TM
    pad_end = jnp.cumsum(padded)
    pad_start = pad_end - padded
    blk_row = (jnp.arange(MOE_NBLK, dtype=jnp.int32) * TM)[:, None]
    ids = jnp.arange(N_EXPERTS, dtype=jnp.int32)
    owns = jnp.logical_and(pad_start[None, :] <= blk_row, blk_row < pad_end[None, :])
    last_used = jnp.max(jnp.where(counts > 0, ids, 0))
    block_e = jnp.where(jnp.any(owns, axis=1), jnp.sum(jnp.where(owns, ids[None, :], 0), axis=1), last_used)
    block_e = block_e.astype(jnp.int32)
    left = jnp.clip(counts[None, :] - (blk_row - pad_start[None, :]), 0, TM)
    n_valid = jnp.sum(jnp.where(owns, left, 0), axis=1).astype(jnp.int32)
    start = jnp.zeros(idx_t.shape, jnp.int32)
    for e in range(N_EXPERTS):
        start = jnp.where(idx_t == e, pad_start[e], start)
    dest_t = (start + rank_t).astype(jnp.int32)
    later = jnp.where(jnp.logical_and(counts[None, :] > 0, ids[None, :] > ids[:, None]), ids[None, :], N_EXPERTS)
    next_e = jnp.min(later, axis=1)
    next_e = jnp.where(next_e == N_EXPERTS, -1, next_e).astype(jnp.int32)
    return block_e, n_valid, next_e, dest_t


def _rope_tables():
    rows = DEC_SEQ // GRID_W
    row = jnp.repeat(jnp.arange(rows, dtype=F32), GRID_W)
    col = jnp.tile(jnp.arange(GRID_W, dtype=F32), rows)
    n_f = RET_DK // 4
    freqs = ROPE_THETA ** (-jnp.arange(n_f, dtype=F32) / n_f)
    ang = jnp.concatenate([row[:, None] * freqs, col[:, None] * freqs], axis=-1)
    cos = jnp.repeat(jnp.cos(ang), 2, axis=-1)
    sin = jnp.repeat(jnp.sin(ang), 2, axis=-1) * jnp.tile(jnp.asarray([-1.0, 1.0], F32), RET_DK // 2)
    cos = jnp.concatenate([jnp.ones((RB, RET_DK), F32), cos], axis=0)
    sin = jnp.concatenate([jnp.zeros((RB, RET_DK), F32), sin], axis=0)
    return jnp.tile(cos, (1, 2)), jnp.tile(sin, (1, 2))


def kernel(x_prompt, x_sample, state_ret, state_gla, c, c_ctx, w_mod, b_mod, norm1_g, norm2_g, final_g, even_w_in, ret_decay, ret_gn, conv_w, conv_b, conv_ln_g, conv_ln_b, even_w_out, odd_w_in, gla_w_a2, gla_b_a2, gla_gn, odd_w_out, router_w, router_b, exp_w_gu, exp_b_gu, exp_w_down, exp_b_down):
    x_src = ("split", x_prompt.reshape(NT_PROMPT, D_MODEL), x_sample.reshape(NT - NT_PROMPT, D_MODEL))
    cvec = jnp.concatenate([c_ctx[None, :], c, jnp.zeros((MOD_ROWS - 1 - DEC_BATCH, D_MODEL), F32)], axis=0)
    mods = _modulation(cvec, w_mod, b_mod).reshape(DEPTH, MOD_ROWS, N_MOD, D_MODEL)
    cos_tab, sin_tab = _rope_tables()
    new_ret = jnp.zeros((BATCH,) + state_ret.shape[1:], F32)
    new_gla = jnp.zeros((BATCH,) + state_gla.shape[1:], F32)
    for l in range(DEPTH):
        mod = [mods[l, :, j, :] for j in range(N_MOD)]
        if l % 2 == 0:
            e = l // 2
            qd, vd = RET_HEADS * RET_DK, RET_HEADS * RET_DV
            x, q, k, v, g, a, ga = _inproj(x_src, norm1_g[l], mod[0], mod[1], even_w_in, e,
                                           (qd, qd, vd, vd, CONV_CH, CONV_CH))
            o_f, new_ret = _retention(ret_decay[e], q, k, v, cos_tab, sin_tab, state_ret, new_ret, e,
                                      reverse=False)
            ret, new_ret = _retention(ret_decay[e], q, k, v, cos_tab, sin_tab, state_ret, new_ret, e,
                                      reverse=True, o_fwd=o_f, g=g, gn=ret_gn[e])
            u = _conv_module(a, ga, conv_w[e], conv_b[e], conv_ln_g[e], conv_ln_b[e])
            mixes, w_out, w_index = [ret, u], even_w_out, e
        else:
            o = l // 2
            qd, vd = GLA_HEADS * GLA_DK, GLA_HEADS * GLA_DV
            x, q, k, v, r, alr = _inproj(x_src, norm1_g[l], mod[0], mod[1], odd_w_in, o,
                                         (qd, qd, vd, vd, 2 * GLA_RANK))
            zeros = jnp.zeros((GLA_RANK, qd), F32)
            wa_f = jnp.concatenate([gla_w_a2[o, 0], zeros], axis=0)
            wa_b = jnp.concatenate([zeros, gla_w_a2[o, 1]], axis=0)
            o_f, new_gla = _gla(alr, q, k, v, wa_f, gla_b_a2[o, 0].reshape(1, qd), state_gla, new_gla, o,
                                reverse=False)
            y, new_gla = _gla(alr, q, k, v, wa_b, gla_b_a2[o, 1].reshape(1, qd), state_gla, new_gla, o,
                              reverse=True, o_fwd=o_f, r=r, gn=gla_gn[o])
            mixes, w_out, w_index = [y], odd_w_out, o
        x1, h2, logits_t = _outproj(mixes, w_out, w_index, x, mod[2], norm2_g[l], mod[3], mod[4],
                                    router_w[l], router_b[l])
        idx_t, rank_t, gates_t, counts = _route(logits_t)
        block_e, n_valid, next_e, dest_t = _routing_plan(counts, idx_t, rank_t)
        xs = _sc_scatter_rows(h2, dest_t, MOE_ROWS)
        yb = _moe_experts(l, block_e, n_valid, next_e, xs, exp_w_gu, exp_b_gu, exp_w_down, exp_b_down)
        yg = _sc_gather_rows(yb, dest_t.reshape(TOP_K * NT))
        x_src = ("moe", x1, yg, gates_t, mod[5])
    y_prompt, y_sample = _combine_final(*x_src[1:], final_g)
    y_prompt = y_prompt.reshape(BATCH, SEQ, D_MODEL)
    y_sample = y_sample.reshape(DEC_BATCH, DEC_SEQ, D_MODEL)
    return (y_prompt, y_sample, new_ret, new_gla)
```

```python
import functools

import jax
import jax.numpy as jnp
from jax import lax
from jax.experimental import pallas as pl
from jax.experimental.pallas import tpu as pltpu
from jax.experimental.pallas import tpu_sc as plsc

F32 = jnp.float32
BF16 = jnp.bfloat16

D_MODEL = 1024
BATCH = 16
SEQ = 256
DEPTH = 4
DEC_BATCH = 4
DEC_SEQ = 4096
GRID_W = 64
RET_HEADS = 4
RET_DK = 64
RET_DV = 128
RET_CHUNK = 128
CONV_CH = 512
CONV_WIDTH = 31
CONV_PAD = CONV_WIDTH // 2
GLA_HEADS = 4
GLA_DK = 128
GLA_DV = 256
GLA_RANK = 16
GLA_TAU = 16.0
GLA_CHUNK = 64
GLA_SUB = 16
N_EXPERTS = 32
TOP_K = 4
D_FF = 1024
SWIGLU_LIMIT = 7.0
SWIGLU_ALPHA = 1.702
MOE_BLOCK = 128
ROPE_THETA = 10000.0
EPS = 1e-6
N_MOD = 6

RB = 256
NT_PROMPT = BATCH * SEQ
NT = NT_PROMPT + DEC_BATCH * DEC_SEQ
NBLK = NT // RB
N_PROMPT_BLK = NT_PROMPT // RB
SAMPLE_BLK = DEC_SEQ // RB
NSEQ = BATCH + DEC_BATCH
MOD_ROWS = 8
HALO = 16
VMEM_LIMIT = 48 * 1024 * 1024

assert SEQ == RB and DEC_SEQ % RB == 0 and CONV_PAD < HALO


def _seq_of_block(i):
    return jnp.where(i < N_PROMPT_BLK, i, N_PROMPT_BLK + (i - N_PROMPT_BLK) // SAMPLE_BLK)


def _is_first_block(i):
    return jnp.logical_or(i < N_PROMPT_BLK, (i - N_PROMPT_BLK) % SAMPLE_BLK == 0)


def _is_last_block(i):
    return jnp.logical_or(i < N_PROMPT_BLK, (i - N_PROMPT_BLK) % SAMPLE_BLK == SAMPLE_BLK - 1)


def _mod_row(i):
    return jnp.where(i < N_PROMPT_BLK, 0, 1 + (i - N_PROMPT_BLK) // SAMPLE_BLK)


def _rope_block(i):
    return jnp.where(i < N_PROMPT_BLK, 0, 1 + (i - N_PROMPT_BLK) % SAMPLE_BLK)


def _dot(a, b):
    return jnp.dot(a, b, preferred_element_type=F32)


def _dot_nt(a, b):
    return lax.dot_general(a, b, (((1,), (1,)), ((), ())), preferred_element_type=F32)


def _dot_tn(a, b):
    return lax.dot_general(a, b, (((0,), (0,)), ((), ())), preferred_element_type=F32)


def _split2(a):
    hi = a.astype(BF16)
    lo = (a - hi.astype(F32)).astype(BF16)
    return hi, lo


def _dot_hi(a, b):
    a_hi, a_lo = _split2(a)
    b_hi, b_lo = _split2(b)
    return _dot(a_hi, b_hi) + (_dot(a_hi, b_lo) + _dot(a_lo, b_hi))


def _silu(x):
    return x * (1.0 / (1.0 + jnp.exp(-x)))


def _sigmoid(x):
    return 1.0 / (1.0 + jnp.exp(-x))


def _pack_rows(x):
    n = x.shape[1] // 2
    lo = pltpu.bitcast(x[:, :n].astype(BF16).astype(F32), jnp.uint32)
    hi = pltpu.bitcast(x[:, n:].astype(BF16).astype(F32), jnp.uint32)
    return hi | (lo >> 16)


def _unpack_rows(u):
    lo = pltpu.bitcast(u << 16, F32)
    hi = pltpu.bitcast(u & jnp.uint32(0xFFFF0000), F32)
    return lo, hi


def _params(n_axes=1, vmem=VMEM_LIMIT):
    return pltpu.CompilerParams(dimension_semantics=("arbitrary",) * n_axes, vmem_limit_bytes=vmem)


MOD_TN = 1536


def _mod_kernel(c_ref, w_ref, b_ref, o_ref):
    s = _silu(c_ref[...]).astype(BF16)
    o_ref[...] = _dot(s, w_ref[...].astype(BF16)) + b_ref[...]


def _modulation(cvec, w_mod, b_mod):
    n = N_MOD * D_MODEL
    return pl.pallas_call(
        _mod_kernel,
        grid=(DEPTH, n // MOD_TN),
        in_specs=[pl.BlockSpec((MOD_ROWS, D_MODEL), lambda l, j: (0, 0)),
                  pl.BlockSpec((None, D_MODEL, MOD_TN), lambda l, j: (l, 0, j)),
                  pl.BlockSpec((None, 1, MOD_TN), lambda l, j: (l, 0, j))],
        out_specs=pl.BlockSpec((None, MOD_ROWS, MOD_TN), lambda l, j: (l, 0, j)),
        out_shape=jax.ShapeDtypeStruct((DEPTH, MOD_ROWS, n), F32),
        compiler_params=_params(2),
        name="modulation",
    )(cvec, w_mod, b_mod.reshape(DEPTH, 1, n))


def _norm_mod(x, g_ref, shift_ref, scale_ref, row):
    y = x * lax.rsqrt(jnp.mean(x * x, axis=-1, keepdims=True) + EPS) * g_ref[...]
    return y * (1.0 + scale_ref[pl.ds(row, 1), :]) + shift_ref[pl.ds(row, 1), :]


N_SRC = {"split": 2, "moe": 3 + TOP_K}


def _inproj_kernel(*refs, widths, source):
    n_src = N_SRC[source]
    src = refs[:n_src]
    g_ref, shift_ref, scale_ref, w_ref = refs[n_src:n_src + 4]
    outs, w_bf = refs[n_src + 4:-1], refs[-1]
    step = pl.program_id(0)

    @pl.when(step == 0)
    def _():
        w_bf[...] = w_ref[...].astype(BF16)

    row = _mod_row(step)
    if source == "split":
        x = jnp.where(step < N_PROMPT_BLK, src[0][...], src[1][...])
    else:
        x1_ref, yg_refs, gt_ref, gate_ref = src[0], src[1:1 + TOP_K], src[-2], src[-1]
        x = jnp.concatenate(_combined_rows(x1_ref, yg_refs, _gate_columns(gt_ref), gate_ref, row), axis=1)
    outs[0][...] = x
    hb = _norm_mod(x, g_ref, shift_ref, scale_ref, row).astype(BF16)
    off = 0
    for o_ref, width in zip(outs[1:], widths):
        o_ref[...] = _dot(hb, w_bf[:, off:off + width])
        off += width


def _inproj(x, g, shift, scale, w_all, index, widths):
    source, x_args = x[0], list(x[1:])
    n_in = w_all.shape[2]
    row_spec = lambda width: pl.BlockSpec((RB, width), lambda i: (i, 0))
    full = lambda shape: pl.BlockSpec(shape, lambda i: (0,) * len(shape))
    if source == "split":
        x_specs = [pl.BlockSpec((RB, D_MODEL), lambda i: (jnp.minimum(i, N_PROMPT_BLK - 1), 0)),
                   pl.BlockSpec((RB, D_MODEL), lambda i: (jnp.maximum(i - N_PROMPT_BLK, 0), 0))]
    else:
        x1, yg, gates_t, gate = x_args
        x_specs, x_args = _combine_specs(), [x1] + [yg] * TOP_K + [gates_t, gate]
    out_widths = (D_MODEL,) + tuple(widths)
    return pl.pallas_call(
        functools.partial(_inproj_kernel, widths=widths, source=source),
        grid=(NBLK,),
        in_specs=x_specs + [full((1, D_MODEL)), full((MOD_ROWS, D_MODEL)), full((MOD_ROWS, D_MODEL)),
                            pl.BlockSpec((None, D_MODEL, n_in), lambda i: (index, 0, 0))],
        out_specs=[row_spec(width) for width in out_widths],
        out_shape=[jax.ShapeDtypeStruct((NT, width), F32) for width in out_widths],
        scratch_shapes=[pltpu.VMEM((D_MODEL, n_in), BF16)],
        compiler_params=_params(),
        name="inproj",
    )(*x_args, g.reshape(1, D_MODEL), shift, scale, w_all)


RC = RET_CHUNK
RET_PAIR = 2 * RET_DK


def _rope(x, cos, sin_signed):
    lane = lax.broadcasted_iota(jnp.int32, x.shape, 1)
    swapped = jnp.where(lane % 2 == 0, pltpu.roll(x, x.shape[1] - 1, 1), pltpu.roll(x, 1, 1))
    return x * cos + swapped * sin_signed


def _ret_kernel(*refs, reverse):
    for _ in _ret_stages(*refs, reverse=reverse):
        pass


def _ret_stages(decay_ref, q_ref, k_ref, v_ref, cos_ref, sin_ref, s0_ref, acc_ref, *rest, reverse):
    del acc_ref
    if reverse:
        of_ref, g_ref, gn_ref, o_ref, sfin_ref, st_ref, dm_ref, dq_ref, dk_ref, ds_ref = rest
    else:
        o_ref, sfin_ref, st_ref, dm_ref, dq_ref, dk_ref, ds_ref = rest
    step = pl.program_id(0)
    blk = NBLK - 1 - step if reverse else step
    direction = 1 if reverse else 0

    @pl.when(step == 0)
    def _():
        row = lax.broadcasted_iota(jnp.int32, (RC, RC), 0).astype(F32)
        col = lax.broadcasted_iota(jnp.int32, (RC, RC), 1).astype(F32)
        for h in range(RET_HEADS):
            lg = -jnp.exp(jnp.full((RC, RC), decay_ref[direction, h], F32))
            if reverse:
                diff = col - row
                mask = diff > 0
                q_pow = RC - row
                k_pow = row
            else:
                diff = row - col
                mask = diff >= 0
                q_pow = row + 1.0
                k_pow = RC - 1.0 - row
            dm_ref[h] = jnp.where(mask, jnp.exp(lg * jnp.where(mask, diff, 0.0)), 0.0)
            dq_ref[h] = jnp.exp(lg * q_pow)
            dk_ref[h] = jnp.exp(lg * k_pow)
            ds_ref[h] = jnp.exp(lg * RC)

    starts = _is_last_block(blk) if reverse else _is_first_block(blk)

    is_prompt = blk < N_PROMPT_BLK

    @pl.when(starts)
    def _():
        st_ref[...] = jnp.zeros_like(st_ref)

    @pl.when(jnp.logical_and(starts, jnp.logical_not(is_prompt)))
    def _():
        for h in range(RET_HEADS):
            off = (h % 2) * RET_DK
            st_ref[h, off:off + RET_DK, :] = s0_ref[h]

    lane = lax.broadcasted_iota(jnp.int32, (1, RET_PAIR), 1)
    chunks = range(RB // RC)
    chunk_order = list(reversed(chunks) if reverse else chunks)
    units = [(h, c) for h in range(RET_HEADS) for c in chunk_order]
    rows_of = lambda c: slice(c * RC, (c + 1) * RC)
    vcols = lambda h: slice(h * RET_DV, (h + 1) * RET_DV)
    roped = {}
    for p in range(RET_HEADS // 2):
        cols = slice(p * RET_PAIR, (p + 1) * RET_PAIR)
        for c in chunk_order:
            rows = rows_of(c)
            cos, sin = cos_ref[rows, :], sin_ref[rows, :]
            roped[p, c] = (_rope(q_ref[rows, cols], cos, sin),
                           _rope(k_ref[rows, cols] * (RET_DK ** -0.5), cos, sin))
    yield
    qm_u, vh_u, att_u, kd_u = {}, {}, {}, {}
    for h, c in units:
        head_mask = (lane // RET_DK == h % 2).astype(F32)
        q2, k2 = roped[h // 2, c]
        vh_u[h, c] = v_ref[rows_of(c), vcols(h)].astype(BF16)
        qm_u[h, c] = (q2 * head_mask).astype(BF16)
        km = k2 * head_mask
        att_u[h, c] = (_dot_nt(qm_u[h, c], km.astype(BF16)) * dm_ref[h]).astype(BF16)
        kd_u[h, c] = (km * dk_ref[h]).astype(BF16)
    yield
    intra_u = {u: _dot(att_u[u], vh_u[u]) for u in units}
    delta_u = {u: _dot_tn(kd_u[u], vh_u[u]) for u in units}
    yield
    state_u = {}
    for h in range(RET_HEADS):
        st = st_ref[h]
        for c in chunk_order:
            state_u[h, c] = st.astype(BF16)
            st = st * ds_ref[h] + delta_u[h, c]
        st_ref[h] = st
    yield
    for h, c in units:
        rows, out_cols = rows_of(c), vcols(h)
        o = intra_u[h, c] + _dot(qm_u[h, c], state_u[h, c]) * dq_ref[h]
        if reverse:
            o = o + of_ref[rows, out_cols]
            o = o * lax.rsqrt(jnp.mean(o * o, axis=-1, keepdims=True) + EPS)
            o = o * gn_ref[:, out_cols] * _silu(g_ref[rows, out_cols])
            o_ref[rows, out_cols] = o.astype(o_ref.dtype)
        else:
            o_ref[rows, out_cols] = o

    ends = _is_first_block(blk) if reverse else _is_last_block(blk)

    @pl.when(jnp.logical_and(ends, is_prompt))
    def _():
        for h in range(RET_HEADS):
            off = (h % 2) * RET_DK
            sfin_ref[h] = st_ref[h, off:off + RET_DK, :]


def _sample_seq(blk):
    return jnp.clip(_seq_of_block(blk) - BATCH, 0, DEC_BATCH - 1)


def _prompt_seq(blk):
    return jnp.minimum(_seq_of_block(blk), BATCH - 1)


def _retention(decay, q, k, v, cos_tab, sin_tab, state_in, state_out, layer, *, reverse,
               o_fwd=None, g=None, gn=None, conv=None):
    bmap = (lambda j: NBLK - 1 - j) if reverse else (lambda j: j)
    direction = 1 if reverse else 0
    qd, vd = RET_HEADS * RET_DK, RET_HEADS * RET_DV
    row_spec = lambda width: pl.BlockSpec((RB, width), lambda j: (bmap(j), 0))
    state_blk = (None, None, None, RET_HEADS, RET_DK, RET_DV)
    in_specs = [pl.BlockSpec(memory_space=pltpu.SMEM), row_spec(qd), row_spec(qd), row_spec(vd),
                pl.BlockSpec((RB, RET_PAIR), lambda j: (_rope_block(bmap(j)), 0)),
                pl.BlockSpec((RB, RET_PAIR), lambda j: (_rope_block(bmap(j)), 0)),
                pl.BlockSpec(state_blk, lambda j: (_sample_seq(bmap(j)), layer, direction, 0, 0, 0)),
                pl.BlockSpec(memory_space=pl.ANY)]
    args = [decay, q, k, v, cos_tab, sin_tab, state_in, state_out]
    if reverse:
        in_specs += [row_spec(vd), row_spec(vd), pl.BlockSpec((1, vd), lambda j: (0, 0))]
        args += [o_fwd, g, gn.reshape(1, vd)]
    tile = pltpu.VMEM((RET_HEADS, RC, RC), F32)
    out_specs = [row_spec(vd),
                 pl.BlockSpec(state_blk, lambda j: (_prompt_seq(bmap(j)), layer, direction, 0, 0, 0))]
    out_shape = [jax.ShapeDtypeStruct((NT, vd), BF16 if reverse else F32),
                 jax.ShapeDtypeStruct(state_out.shape, F32)]
    scratch = [tile, tile, tile, tile, tile]
    body = functools.partial(_ret_kernel, reverse=reverse)
    if conv is not None:
        assert not reverse
        n_ret = (len(in_specs), len(out_specs), len(scratch))
        c_in, c_args, c_out, c_shape, c_scratch = _conv_parts(*conv)
        in_specs, args = in_specs + c_in, args + c_args
        out_specs, out_shape, scratch = out_specs + c_out, out_shape + c_shape, scratch + c_scratch
        body = functools.partial(_ret_conv_kernel, n_ret=n_ret, n_conv=(len(c_in), len(c_out), len(c_scratch)))
    return pl.pallas_call(
        body,
        grid=(NBLK,),
        in_specs=in_specs,
        out_specs=out_specs,
        out_shape=out_shape,
        input_output_aliases={7: 1},
        scratch_shapes=scratch,
        compiler_params=_params(),
        name="retention_bwd" if reverse else "retention_fwd",
    )(*args)


def _ret_conv_kernel(*refs, n_ret, n_conv):
    (ri, ro, rs), (ci, co, cs) = n_ret, n_conv
    ins, outs, scr = refs[:ri + ci], refs[ri + ci:ri + ci + ro + co], refs[ri + ci + ro + co:]
    bodies = [_conv_stages(*ins[ri:], *outs[ro:], *scr[rs:]),
              _ret_stages(*ins[:ri], *outs[:ro], *scr[:rs], reverse=False)]
    while bodies:
        for body in list(bodies):
            if next(body, StopIteration) is StopIteration:
                bodies.remove(body)


CONV_RT = 32
CONV_CT = 128
CONV_SPAN = RB + 2 * HALO - 8


def _conv_stages(a_ref, ga_ref, ap_ref, gap_ref, an_ref, gan_ref, cw_ref, cb_ref, lng_ref, lnb_ref,
                 o_ref, u_ref, y_ref, us_ref):
    blk = pl.program_id(0)
    keep_prev = jnp.where(_is_first_block(blk), 0.0, 1.0)
    keep_next = jnp.where(_is_last_block(blk), 0.0, 1.0)
    u_ref[0:HALO, :] = ap_ref[...] * _sigmoid(gap_ref[...]) * keep_prev
    u_ref[HALO:HALO + RB, :] = a_ref[...] * _sigmoid(ga_ref[...])
    u_ref[HALO + RB:HALO + RB + HALO, :] = an_ref[...] * _sigmoid(gan_ref[...]) * keep_next
    for r in range(1, 8):
        us_ref[r - 1] = u_ref[r:r + CONV_SPAN, :]
    for ct in range(CONV_CH // CONV_CT):
        yield
        cols = slice(ct * CONV_CT, (ct + 1) * CONV_CT)
        for rt in range(RB // CONV_RT):
            acc = jnp.zeros((CONV_RT, CONV_CT), F32)
            for w in range(CONV_WIDTH):
                tiles, r = divmod(HALO - CONV_PAD + w, 8)
                base = rt * CONV_RT + 8 * tiles
                src = u_ref if r == 0 else us_ref.at[r - 1]
                acc = acc + src[base:base + CONV_RT, cols] * cw_ref[w:w + 1, cols]
            y_ref[rt * CONV_RT:(rt + 1) * CONV_RT, cols] = acc + cb_ref[:, cols]
    y = y_ref[...]
    mu = jnp.mean(y, axis=-1, keepdims=True)
    var = jnp.mean(jnp.square(y - mu), axis=-1, keepdims=True)
    o_ref[...] = _silu((y - mu) * lax.rsqrt(var + EPS) * lng_ref[...] + lnb_ref[...]).astype(o_ref.dtype)


def _conv_parts(a, ga, cw, cb, lng, lnb):
    per_blk = RB // HALO
    n_halo = NT // HALO
    row_spec = pl.BlockSpec((RB, CONV_CH), lambda i: (i, 0))
    prev_spec = pl.BlockSpec((HALO, CONV_CH), lambda i: (jnp.maximum(i * per_blk - 1, 0), 0))
    next_spec = pl.BlockSpec((HALO, CONV_CH), lambda i: (jnp.minimum((i + 1) * per_blk, n_halo - 1), 0))
    vec = pl.BlockSpec((1, CONV_CH), lambda i: (0, 0))
    in_specs = [row_spec, row_spec, prev_spec, prev_spec, next_spec, next_spec,
                pl.BlockSpec((CONV_WIDTH, CONV_CH), lambda i: (0, 0)), vec, vec, vec]
    args = [a, ga, a, ga, a, ga, cw, cb.reshape(1, -1), lng.reshape(1, -1), lnb.reshape(1, -1)]
    scratch = [pltpu.VMEM((RB + 2 * HALO, CONV_CH), F32), pltpu.VMEM((RB, CONV_CH), F32),
               pltpu.VMEM((7, CONV_SPAN, CONV_CH), F32)]
    return in_specs, args, [row_spec], [jax.ShapeDtypeStruct((NT, CONV_CH), BF16)], scratch


GC = GLA_CHUNK
GLA_NSUB = GC // GLA_SUB


def _split3(a):
    p1 = a.astype(BF16)
    r1 = a - p1.astype(F32)
    p2 = r1.astype(BF16)
    p3 = (r1 - p2.astype(F32)).astype(BF16)
    return p1, p2, p3


def _gla_kernel(alr_ref, q_ref, k_ref, v_ref, wa_ref, ba_ref, s0_ref, acc_ref, *rest, reverse):
    del acc_ref
    if reverse:
        of_ref, r_ref, gn_ref, o_ref, sfin_ref, st_ref, b_ref, tri_ref, ob_ref = rest
    else:
        o_ref, sfin_ref, st_ref, b_ref, tri_ref = rest
        ob_ref = o_ref
    step = pl.program_id(0)
    blk = NBLK - 1 - step if reverse else step
    starts = _is_last_block(blk) if reverse else _is_first_block(blk)
    is_prompt = blk < N_PROMPT_BLK

    @pl.when(jnp.logical_and(starts, is_prompt))
    def _():
        st_ref[...] = jnp.zeros_like(st_ref)

    @pl.when(jnp.logical_and(starts, jnp.logical_not(is_prompt)))
    def _():
        for h in range(GLA_HEADS):
            st_ref[h] = s0_ref[h].T

    @pl.when(step == 0)
    def _():
        row = lax.broadcasted_iota(jnp.int32, (RB, RB), 0)
        col = lax.broadcasted_iota(jnp.int32, (RB, RB), 1)
        ordered = col >= row if reverse else col <= row
        tri_ref[...] = jnp.where(jnp.logical_and(row // GC == col // GC, ordered), 1.0, 0.0).astype(BF16)

    heads = range(GLA_HEADS)
    hcols = [slice(h * GLA_DK, (h + 1) * GLA_DK) for h in heads]
    alr = alr_ref[...]
    z = [_dot_hi(alr, wa_ref[:, hcols[h]]) + ba_ref[:, hcols[h]] for h in heads]
    log_a = [(jnp.minimum(zh, 0.0) - jnp.log(1.0 + jnp.exp(-jnp.abs(zh)))) * (1.0 / GLA_TAU) for zh in z]
    parts = [_split3(la) for la in log_a]
    tri = tri_ref[...]
    for h in heads:
        g1, g2, g3 = parts[h]
        b_ref[:, hcols[h]] = _dot(tri, g1) + (_dot(tri, g2) + _dot(tri, g3))

    c_row = lax.broadcasted_iota(jnp.int32, (GC, 1), 0)
    a_row = lax.broadcasted_iota(jnp.int32, (GC, GC), 0)
    a_col = lax.broadcasted_iota(jnp.int32, (GC, GC), 1)
    att_mask = a_col > a_row if reverse else a_col <= a_row
    chunks = range(RB // GC)
    chunk_order = list(reversed(chunks) if reverse else chunks)
    units = [(h, c) for h in range(GLA_HEADS) for c in chunk_order]
    kcols = lambda h: slice(h * GLA_DK, (h + 1) * GLA_DK)
    vcols = lambda h: slice(h * GLA_DV, (h + 1) * GLA_DV)
    rows_of = lambda c: slice(c * GC, (c + 1) * GC)

    vh_u, qe_u, ke_u, decay_u, att_u = {}, {}, {}, {}, {}
    for h, c in units:
        rows = rows_of(c)
        b = b_ref[rows, kcols(h)]
        qh = q_ref[rows, kcols(h)] * (GLA_DK ** -0.5)
        kh = k_ref[rows, kcols(h)]
        vh_u[h, c] = v_ref[rows, vcols(h)].astype(BF16)
        edge = b[0:1, :] if reverse else b[GC - 1:GC, :]
        bounds = []
        for s in range(GLA_NSUB):
            if reverse:
                hi = (s + 1) * GLA_SUB
                bounds.append(b[hi:hi + 1, :] if s < GLA_NSUB - 1 else jnp.zeros((1, GLA_DK), F32))
            else:
                lo = s * GLA_SUB
                bounds.append(b[lo - 1:lo, :] if s > 0 else jnp.zeros((1, GLA_DK), F32))
        own = jnp.concatenate([jnp.broadcast_to(bd, (GLA_SUB, GLA_DK)) for bd in bounds], axis=0)
        q_own = qh * jnp.exp(b - own)
        q_parts, k_parts = [], []
        for s, bd in enumerate(bounds):
            q_parts.append(jnp.where(c_row // GLA_SUB == s, q_own, 0.0))
            reach = c_row >= s * GLA_SUB if reverse else c_row < (s + 1) * GLA_SUB
            k_parts.append(kh * jnp.exp(jnp.where(reach, bd - b, -jnp.inf)))
        q_bd = jnp.concatenate(q_parts, axis=1).astype(BF16)
        k_cat = jnp.concatenate(k_parts, axis=1).astype(BF16)
        att_u[h, c] = jnp.where(att_mask, _dot_nt(q_bd, k_cat), 0.0).astype(BF16)
        qe_u[h, c] = (qh * jnp.exp(b)).astype(BF16)
        ke_u[h, c] = (kh * jnp.exp(edge - b)).astype(BF16)
        decay_u[h, c] = jnp.exp(edge)
    intra_u = {u: _dot(att_u[u], vh_u[u]) for u in units}
    delta_u = {u: _dot_tn(vh_u[u], ke_u[u]) for u in units}
    state_u = {}
    for h in range(GLA_HEADS):
        st = st_ref[h]
        for c in chunk_order:
            state_u[h, c] = st.astype(BF16)
            st = st * decay_u[h, c] + delta_u[h, c]
        st_ref[h] = st
    for h, c in units:
        ob_ref[rows_of(c), vcols(h)] = intra_u[h, c] + _dot_nt(qe_u[h, c], state_u[h, c])

    if reverse:
        for h in range(GLA_HEADS):
            cols = vcols(h)
            o = ob_ref[:, cols] + of_ref[:, cols]
            o = o * lax.rsqrt(jnp.mean(o * o, axis=-1, keepdims=True) + EPS)
            o_ref[:, cols] = (o * gn_ref[:, cols] * _silu(r_ref[:, cols])).astype(o_ref.dtype)

    ends = _is_first_block(blk) if reverse else _is_last_block(blk)

    @pl.when(jnp.logical_and(ends, is_prompt))
    def _():
        for h in range(GLA_HEADS):
            sfin_ref[h] = st_ref[h].T


def _gla(alr, q, k, v, wa, ba, state_in, state_out, layer, *, reverse, o_fwd=None, r=None, gn=None):
    bmap = (lambda j: NBLK - 1 - j) if reverse else (lambda j: j)
    direction = 1 if reverse else 0
    qd, vd = GLA_HEADS * GLA_DK, GLA_HEADS * GLA_DV
    row_spec = lambda width: pl.BlockSpec((RB, width), lambda j: (bmap(j), 0))
    state_blk = (None, None, None, GLA_HEADS, GLA_DK, GLA_DV)
    in_specs = [row_spec(2 * GLA_RANK), row_spec(qd), row_spec(qd), row_spec(vd),
                pl.BlockSpec((2 * GLA_RANK, qd), lambda j: (0, 0)),
                pl.BlockSpec((1, qd), lambda j: (0, 0)),
                pl.BlockSpec(state_blk, lambda j: (_sample_seq(bmap(j)), layer, direction, 0, 0, 0)),
                pl.BlockSpec(memory_space=pl.ANY)]
    args = [alr, q, k, v, wa, ba, state_in, state_out]
    scratch = [pltpu.VMEM((GLA_HEADS, GLA_DV, GLA_DK), F32), pltpu.VMEM((RB, qd), F32),
               pltpu.VMEM((RB, RB), BF16)]
    if reverse:
        in_specs += [row_spec(vd), row_spec(vd), pl.BlockSpec((1, vd), lambda j: (0, 0))]
        args += [o_fwd, r, gn.reshape(1, vd)]
        scratch += [pltpu.VMEM((RB, vd), F32)]
    return pl.pallas_call(
        functools.partial(_gla_kernel, reverse=reverse),
        grid=(NBLK,),
        in_specs=in_specs,
        out_specs=[row_spec(vd),
                   pl.BlockSpec(state_blk, lambda j: (_prompt_seq(bmap(j)), layer, direction, 0, 0, 0))],
        out_shape=[jax.ShapeDtypeStruct((NT, vd), BF16 if reverse else F32),
                   jax.ShapeDtypeStruct(state_out.shape, F32)],
        input_output_aliases={7: 1},
        scratch_shapes=scratch,
        compiler_params=_params(),
        name="gla_bwd" if reverse else "gla_fwd",
    )(*args)


OUT_SUB = 128


def _outproj_kernel(*refs, n_mix):
    mix_refs = refs[:n_mix]
    (w_ref, x_ref, gate_ref, g2_ref, shift_ref, scale_ref, rw_ref, rb_ref,
     x1_ref, h2_ref, logit_ref, w_bf) = refs[n_mix:]
    step = pl.program_id(0)

    @pl.when(step == 0)
    def _():
        w_bf[...] = w_ref[...].astype(BF16)

    row = _mod_row(step)
    r_hi, r_lo = _split2(rw_ref[...])
    groups = [slice(p * OUT_SUB, (p + 1) * OUT_SUB) for p in range(RB // OUT_SUB)]
    mixed = []
    for rows in groups:
        m, off = None, 0
        for mix_ref in mix_refs:
            width = mix_ref.shape[1]
            part = _dot(mix_ref[rows, :], w_bf[off:off + width, :])
            m = part if m is None else m + part
            off += width
        mixed.append(m)
    normed = []
    for rows, m in zip(groups, mixed):
        x1 = x_ref[rows, :] + gate_ref[pl.ds(row, 1), :] * m
        x1_ref[rows, :] = x1
        h2 = _norm_mod(x1, g2_ref, shift_ref, scale_ref, row)
        h2_ref[rows, :] = _pack_rows(h2)
        normed.append(_split2(h2))
    for rows, (h_hi, h_lo) in zip(groups, normed):
        logit_ref[:, rows] = _dot_nt(r_hi, h_hi) + (_dot_nt(r_hi, h_lo) + _dot_nt(r_lo, h_hi)) + rb_ref[...]


def _outproj(mixes, w_all, index, x, gate, g2, shift, scale, rw, rb):
    n_mix = len(mixes)
    row_spec = lambda width: pl.BlockSpec((RB, width), lambda i: (i, 0))
    full = lambda shape: pl.BlockSpec(shape, lambda i: (0,) * len(shape))
    mod_spec = full((MOD_ROWS, D_MODEL))
    n_mixed = w_all.shape[1]
    return pl.pallas_call(
        functools.partial(_outproj_kernel, n_mix=n_mix),
        grid=(NBLK,),
        in_specs=[row_spec(m.shape[1]) for m in mixes]
        + [pl.BlockSpec((None, n_mixed, D_MODEL), lambda i: (index, 0, 0)),
           row_spec(D_MODEL), mod_spec, full((1, D_MODEL)), mod_spec, mod_spec,
           full((N_EXPERTS, D_MODEL)), full((N_EXPERTS, 1))],
        out_specs=[row_spec(D_MODEL), row_spec(D_MODEL // 2), pl.BlockSpec((N_EXPERTS, RB), lambda i: (0, i))],
        out_shape=[jax.ShapeDtypeStruct((NT, D_MODEL), F32), jax.ShapeDtypeStruct((NT, D_MODEL // 2), jnp.uint32),
                   jax.ShapeDtypeStruct((N_EXPERTS, NT), F32)],
        scratch_shapes=[pltpu.VMEM((n_mixed, D_MODEL), BF16)],
        compiler_params=_params(),
        name="outproj",
    )(*mixes, w_all, x, gate, g2.reshape(1, D_MODEL), shift, scale, rw.T, rb.reshape(N_EXPERTS, 1))


ROUTE_BLK = 2048
ROUTE_SUB = 256


def _route_kernel(lg_ref, idx_ref, rank_ref, gt_ref, cnt_ref, carry_ref):
    @pl.when(pl.program_id(0) == 0)
    def _():
        carry_ref[...] = jnp.zeros_like(carry_ref)

    logits = lg_ref[...]
    eid = lax.broadcasted_iota(jnp.int32, logits.shape, 0).astype(F32)
    work = logits
    onehots, top_vals = [], []
    for kk in range(TOP_K):
        top = jnp.max(work, axis=0, keepdims=True)
        first = jnp.min(jnp.where(work == top, eid, float(N_EXPERTS)), axis=0, keepdims=True)
        onehot = eid == first
        idx_ref[kk:kk + 1, :] = first.astype(jnp.int32)
        onehots.append(onehot)
        top_vals.append(top)
        work = jnp.where(onehot, -jnp.inf, work)
    exps = [jnp.exp(v - top_vals[0]) for v in top_vals]
    denom = exps[0]
    for e in exps[1:]:
        denom = denom + e
    gt_ref[...] = jnp.zeros_like(gt_ref)
    for kk in range(TOP_K):
        gt_ref[kk:kk + 1, :] = exps[kk] / denom

    sel = jnp.zeros(logits.shape, F32)
    for onehot in onehots:
        sel = sel + jnp.where(onehot, 1.0, 0.0)
    sel = sel.astype(BF16)
    r_i = lax.broadcasted_iota(jnp.int32, (ROUTE_SUB, ROUTE_SUB), 0)
    c_i = lax.broadcasted_iota(jnp.int32, (ROUTE_SUB, ROUTE_SUB), 1)
    before = jnp.where(r_i < c_i, 1.0, 0.0).astype(BF16)
    ones = jnp.ones((ROUTE_SUB, ROUTE_SUB), BF16)
    carry = carry_ref[...]
    for s in range(ROUTE_BLK // ROUTE_SUB):
        cols = slice(s * ROUTE_SUB, (s + 1) * ROUTE_SUB)
        pos = _dot(sel[:, cols], before) + carry
        for kk in range(TOP_K):
            rank = jnp.sum(jnp.where(onehots[kk][:, cols], pos, 0.0), axis=0, keepdims=True)
            rank_ref[kk:kk + 1, cols] = rank.astype(jnp.int32)
        carry = carry + _dot(sel[:, cols], ones)
    carry_ref[...] = carry
    cnt_ref[...] = carry.astype(jnp.int32)


def _route(logits_t):
    col_spec = lambda rows: pl.BlockSpec((rows, ROUTE_BLK), lambda i: (0, i))
    return pl.pallas_call(
        _route_kernel,
        grid=(NT // ROUTE_BLK,),
        in_specs=[col_spec(N_EXPERTS)],
        out_specs=[col_spec(TOP_K), col_spec(TOP_K), col_spec(8),
                   pl.BlockSpec((N_EXPERTS, ROUTE_SUB), lambda i: (0, 0))],
        out_shape=[jax.ShapeDtypeStruct((TOP_K, NT), jnp.int32), jax.ShapeDtypeStruct((TOP_K, NT), jnp.int32),
                   jax.ShapeDtypeStruct((8, NT), F32), jax.ShapeDtypeStruct((N_EXPERTS, ROUTE_SUB), jnp.int32)],
        scratch_shapes=[pltpu.VMEM((N_EXPERTS, ROUTE_SUB), F32)],
        compiler_params=_params(),
        name="route",
    )(logits_t)


TM = 1024
TM_SUB = 256
MOE_NBLK = NT * TOP_K // TM + N_EXPERTS
MOE_ROWS = MOE_NBLK * TM
HALF = D_MODEL // 2


def _moe_kernel(be_ref, nv_ref, nx_ref, x_ref, wgu_hbm, bgu_ref, wd_hbm, bd_ref, y_ref,
                wgu_st, wd_st, wgu_bf, wd_bf, sems, *, layer):
    i = pl.program_id(0)
    n_valid = nv_ref[i]

    def fetch(e):
        return (pltpu.make_async_copy(wgu_hbm.at[layer, e], wgu_st, sems.at[0]),
                pltpu.make_async_copy(wd_hbm.at[layer, e], wd_st, sems.at[1]))

    @pl.when(i == 0)
    def _():
        for cp in fetch(be_ref[0]):
            cp.start()

    @pl.when(n_valid > 0)
    def _():
        e = be_ref[i]
        changed = jnp.logical_or(i == 0, e != be_ref[jnp.maximum(i - 1, 0)])

        @pl.when(changed)
        def _():
            for cp in fetch(e):
                cp.wait()
            wgu_bf[...] = wgu_st[...].astype(BF16)
            wd_bf[...] = wd_st[...].astype(BF16)
            nxt = nx_ref[e]

            @pl.when(nxt >= 0)
            def _():
                for cp in fetch(nxt):
                    cp.start()

    for p in range(TM // TM_SUB):
        @pl.when(n_valid > p * TM_SUB)
        def _():
            rows = slice(p * TM_SUB, (p + 1) * TM_SUB)
            row_id = lax.broadcasted_iota(jnp.int32, (TM_SUB, 1), 0) + p * TM_SUB
            x_lo, x_hi = _unpack_rows(jnp.where(row_id < n_valid, x_ref[rows, :], jnp.uint32(0)))
            gu = (_dot(x_lo.astype(BF16), wgu_bf[:HALF, :]) + _dot(x_hi.astype(BF16), wgu_bf[HALF:, :])
                  + bgu_ref[...])
            gate = jnp.minimum(gu[:, :D_FF], SWIGLU_LIMIT)
            up = jnp.clip(gu[:, D_FF:], -SWIGLU_LIMIT, SWIGLU_LIMIT)
            hdn = gate * _sigmoid(SWIGLU_ALPHA * gate) * (up + 1.0)
            y_ref[rows, :] = _pack_rows(_dot(hdn.astype(BF16), wd_bf[...]) + bd_ref[...])


def _moe_experts(layer, block_e, n_valid, next_e, xs, w_gu, b_gu, w_down, b_down):
    grid_spec = pltpu.PrefetchScalarGridSpec(
        num_scalar_prefetch=3,
        grid=(MOE_NBLK,),
        in_specs=[pl.BlockSpec((TM, HALF), lambda i, be, nv, nx: (i, 0)),
                  pl.BlockSpec(memory_space=pl.ANY),
                  pl.BlockSpec((None, None, 1, 2 * D_FF), lambda i, be, nv, nx: (layer, be[i], 0, 0)),
                  pl.BlockSpec(memory_space=pl.ANY),
                  pl.BlockSpec((None, None, 1, D_MODEL), lambda i, be, nv, nx: (layer, be[i], 0, 0))],
        out_specs=pl.BlockSpec((TM, HALF), lambda i, be, nv, nx: (i, 0)),
        scratch_shapes=[pltpu.VMEM((D_MODEL, 2 * D_FF), F32), pltpu.VMEM((D_FF, D_MODEL), F32),
                        pltpu.VMEM((D_MODEL, 2 * D_FF), BF16), pltpu.VMEM((D_FF, D_MODEL), BF16),
                        pltpu.SemaphoreType.DMA((2,))],
    )
    return pl.pallas_call(
        functools.partial(_moe_kernel, layer=layer),
        grid_spec=grid_spec,
        out_shape=jax.ShapeDtypeStruct((MOE_ROWS, HALF), jnp.uint32),
        compiler_params=_params(),
        name="moe_experts",
    )(block_e, n_valid, next_e, xs, w_gu, b_gu.reshape(DEPTH, N_EXPERTS, 1, -1), w_down,
      b_down.reshape(DEPTH, N_EXPERTS, 1, -1))


SC_WORKERS = 32
SC_WIN = 64


def _sc_mesh():
    return plsc.VectorSubcoreMesh(core_axis_name="core", subcore_axis_name="subcore")


def _sc_worker():
    return lax.axis_index("core") * (SC_WORKERS // 2) + lax.axis_index("subcore")


def _sc_scatter_rows(x, dest_t, n_rows):
    n, width = x.shape
    kk = dest_t.shape[0]
    per = n // SC_WORKERS
    n_win = per // SC_WIN
    assert per * SC_WORKERS == n and n_win * SC_WIN == per and n_win % 2 == 0

    @pl.kernel(out_type=jax.ShapeDtypeStruct((n_rows, width), x.dtype), mesh=_sc_mesh(),
               scratch_types=[pltpu.VMEM((kk, per), jnp.int32), pltpu.VMEM((SC_WIN, width), x.dtype),
                              pltpu.VMEM((SC_WIN, width), x.dtype), pltpu.SemaphoreType.DMA((4,))])
    def scatter(x_hbm, i_hbm, o_hbm, idx_v, buf0, buf1, sems):
        base = _sc_worker() * per
        pltpu.sync_copy(i_hbm.at[:, pl.ds(base, per)], idx_v)

        def get(j, buf, s):
            return pltpu.make_async_copy(x_hbm.at[pl.ds(base + j * SC_WIN, SC_WIN)], buf, sems.at[s])

        def put(j, q, buf, s):
            return pltpu.make_async_copy(buf, o_hbm.at[idx_v.at[q, pl.ds(j * SC_WIN, SC_WIN)]], sems.at[s])

        get(0, buf0, 0).start()

        @pl.loop(0, n_win, step=2)
        def _(j):
            get(j, buf0, 0).wait()

            @pl.when(j > 0)
            def _():
                for q in range(kk):
                    put(j - 1, q, buf1, 3).wait()

            get(j + 1, buf1, 1).start()
            for q in range(kk):
                put(j, q, buf0, 2).start()
            get(j + 1, buf1, 1).wait()
            for q in range(kk):
                put(j, q, buf0, 2).wait()

            @pl.when(j + 2 < n_win)
            def _():
                get(j + 2, buf0, 0).start()

            for q in range(kk):
                put(j + 1, q, buf1, 3).start()

        for q in range(kk):
            put(n_win - 1, q, buf1, 3).wait()

    return scatter(x, dest_t)


def _sc_gather_rows(y, idx):
    n = idx.shape[0]
    width = y.shape[1]
    per = n // SC_WORKERS
    n_win = per // SC_WIN
    assert per * SC_WORKERS == n and n_win * SC_WIN == per and n_win % 2 == 0

    @pl.kernel(out_type=jax.ShapeDtypeStruct((n, width), y.dtype), mesh=_sc_mesh(),
               scratch_types=[pltpu.VMEM((per,), jnp.int32), pltpu.VMEM((SC_WIN, width), y.dtype),
                              pltpu.VMEM((SC_WIN, width), y.dtype), pltpu.SemaphoreType.DMA((4,))])
    def gather(y_hbm, i_hbm, o_hbm, idx_v, buf0, buf1, sems):
        base = _sc_worker() * per
        pltpu.sync_copy(i_hbm.at[pl.ds(base, per)], idx_v)

        def get(j, buf, s):
            return pltpu.make_async_copy(y_hbm.at[idx_v.at[pl.ds(j * SC_WIN, SC_WIN)]], buf, sems.at[s])

        def put(j, buf, s):
            return pltpu.make_async_copy(buf, o_hbm.at[pl.ds(base + j * SC_WIN, SC_WIN)], sems.at[s])

        get(0, buf0, 0).start()

        @pl.loop(0, n_win, step=2)
        def _(j):
            get(j, buf0, 0).wait()

            @pl.when(j > 0)
            def _():
                put(j - 1, buf1, 3).wait()

            get(j + 1, buf1, 1).start()
            put(j, buf0, 2).start()
            get(j + 1, buf1, 1).wait()
            put(j, buf0, 2).wait()

            @pl.when(j + 2 < n_win)
            def _():
                get(j + 2, buf0, 0).start()

            put(j + 1, buf1, 3).start()

        put(n_win - 1, buf1, 3).wait()

    return gather(y, idx)


def _gate_columns(gt_ref):
    r_i = lax.broadcasted_iota(jnp.int32, (RB, RB), 0)
    c_i = lax.broadcasted_iota(jnp.int32, (RB, RB), 1)
    eye = jnp.where(r_i == c_i, 1.0, 0.0).astype(BF16)
    g1, g2, g3 = _split3(gt_ref[...])
    return _dot_nt(eye, g1) + (_dot_nt(eye, g2) + _dot_nt(eye, g3))


def _combined_rows(x1_ref, yg_refs, gw, gate_ref, row, rows=slice(None)):
    acc_lo, acc_hi = None, None
    for kk in range(TOP_K):
        y_lo, y_hi = _unpack_rows(yg_refs[kk][rows, :])
        w = gw[rows, kk:kk + 1]
        acc_lo = y_lo * w if acc_lo is None else acc_lo + y_lo * w
        acc_hi = y_hi * w if acc_hi is None else acc_hi + y_hi * w
    x_lo = x1_ref[rows, :HALF] + gate_ref[pl.ds(row, 1), :HALF] * acc_lo
    x_hi = x1_ref[rows, HALF:] + gate_ref[pl.ds(row, 1), HALF:] * acc_hi
    return x_lo, x_hi


def _combine_specs():
    slot_spec = lambda k: pl.BlockSpec((RB, HALF), lambda i: (k * NBLK + i, 0))
    return ([pl.BlockSpec((RB, D_MODEL), lambda i: (i, 0))] + [slot_spec(k) for k in range(TOP_K)]
            + [pl.BlockSpec((8, RB), lambda i: (0, i)), pl.BlockSpec((MOD_ROWS, D_MODEL), lambda i: (0, 0))])


def _combine_final_kernel(x1_ref, *rest):
    yg_refs = rest[:TOP_K]
    gt_ref, gate_ref, fg_ref, op_ref, os_ref = rest[TOP_K:]
    i = pl.program_id(0)
    x_lo, x_hi = _combined_rows(x1_ref, yg_refs, _gate_columns(gt_ref), gate_ref, _mod_row(i))
    ms = (jnp.sum(x_lo * x_lo, axis=-1, keepdims=True) + jnp.sum(x_hi * x_hi, axis=-1, keepdims=True)) / D_MODEL
    scale = lax.rsqrt(ms + EPS)

    @pl.when(i < N_PROMPT_BLK)
    def _():
        op_ref[:, :HALF] = x_lo * scale * fg_ref[:, :HALF]
        op_ref[:, HALF:] = x_hi * scale * fg_ref[:, HALF:]

    @pl.when(i >= N_PROMPT_BLK)
    def _():
        os_ref[:, :HALF] = x_lo * scale * fg_ref[:, :HALF]
        os_ref[:, HALF:] = x_hi * scale * fg_ref[:, HALF:]


def _combine_final(x1, yg, gates_t, gate, final_g):
    return pl.pallas_call(
        _combine_final_kernel,
        grid=(NBLK,),
        in_specs=_combine_specs() + [pl.BlockSpec((1, D_MODEL), lambda i: (0, 0))],
        out_specs=[pl.BlockSpec((RB, D_MODEL), lambda i: (jnp.minimum(i, N_PROMPT_BLK - 1), 0)),
                   pl.BlockSpec((RB, D_MODEL), lambda i: (jnp.maximum(i - N_PROMPT_BLK, 0), 0))],
        out_shape=[jax.ShapeDtypeStruct((NT_PROMPT, D_MODEL), F32),
                   jax.ShapeDtypeStruct((NT - NT_PROMPT, D_MODEL), F32)],
        compiler_params=_params(),
        name="moe_combine_final",
    )(x1, *([yg] * TOP_K), gates_t, gate, final_g.reshape(1, D_MODEL))


def _routing_plan(counts, idx_t, rank_t):
    counts = counts[:, 0]
    padded = (counts + TM - 1) // TM * TM
    pad_end = jnp.cumsum(padded)
    pad_start = pad_end - padded
    blk_row = (jnp.arange(MOE_NBLK, dtype=jnp.int32) * TM)[:, None]
    ids = jnp.arange(N_EXPERTS, dtype=jnp.int32)
    owns = jnp.logical_and(pad_start[None, :] <= blk_row, blk_row < pad_end[None, :])
    last_used = jnp.max(jnp.where(counts > 0, ids, 0))
    block_e = jnp.where(jnp.any(owns, axis=1), jnp.sum(jnp.where(owns, ids[None, :], 0), axis=1), last_used)
    block_e = block_e.astype(jnp.int32)
    left = jnp.clip(counts[None, :] - (blk_row - pad_start[None, :]), 0, TM)
    n_valid = jnp.sum(jnp.where(owns, left, 0), axis=1).astype(jnp.int32)
    start = jnp.zeros(idx_t.shape, jnp.int32)
    for e in range(N_EXPERTS):
        start = jnp.where(idx_t == e, pad_start[e], start)
    dest_t = (start + rank_t).astype(jnp.int32)
    later = jnp.where(jnp.logical_and(counts[None, :] > 0, ids[None, :] > ids[:, None]), ids[None, :], N_EXPERTS)
    next_e = jnp.min(later, axis=1)
    next_e = jnp.where(next_e == N_EXPERTS, -1, next_e).astype(jnp.int32)
    return block_e, n_valid, next_e, dest_t


def _rope_tables():
    rows = DEC_SEQ // GRID_W
    row = jnp.repeat(jnp.arange(rows, dtype=F32), GRID_W)
    col = jnp.tile(jnp.arange(GRID_W, dtype=F32), rows)
    n_f = RET_DK // 4
    freqs = ROPE_THETA ** (-jnp.arange(n_f, dtype=F32) / n_f)
    ang = jnp.concatenate([row[:, None] * freqs, col[:, None] * freqs], axis=-1)
    cos = jnp.repeat(jnp.cos(ang), 2, axis=-1)
    sin = jnp.repeat(jnp.sin(ang), 2, axis=-1) * jnp.tile(jnp.asarray([-1.0, 1.0], F32), RET_DK // 2)
    cos = jnp.concatenate([jnp.ones((RB, RET_DK), F32), cos], axis=0)
    sin = jnp.concatenate([jnp.zeros((RB, RET_DK), F32), sin], axis=0)
    return jnp.tile(cos, (1, 2)), jnp.tile(sin, (1, 2))


def kernel(x_prompt, x_sample, state_ret, state_gla, c, c_ctx, w_mod, b_mod, norm1_g, norm2_g, final_g, even_w_in, ret_decay, ret_gn, conv_w, conv_b, conv_ln_g, conv_ln_b, even_w_out, odd_w_in, gla_w_a2, gla_b_a2, gla_gn, odd_w_out, router_w, router_b, exp_w_gu, exp_b_gu, exp_w_down, exp_b_down):
    x_src = ("split", x_prompt.reshape(NT_PROMPT, D_MODEL), x_sample.reshape(NT - NT_PROMPT, D_MODEL))
    cvec = jnp.concatenate([c_ctx[None, :], c, jnp.zeros((MOD_ROWS - 1 - DEC_BATCH, D_MODEL), F32)], axis=0)
    mods = _modulation(cvec, w_mod, b_mod).reshape(DEPTH, MOD_ROWS, N_MOD, D_MODEL)
    cos_tab, sin_tab = _rope_tables()
    new_ret = jnp.zeros((BATCH,) + state_ret.shape[1:], F32)
    new_gla = jnp.zeros((BATCH,) + state_gla.shape[1:], F32)
    for l in range(DEPTH):
        mod = [mods[l, :, j, :] for j in range(N_MOD)]
        if l % 2 == 0:
            e = l // 2
            qd, vd = RET_HEADS * RET_DK, RET_HEADS * RET_DV
            x, q, k, v, g, a, ga = _inproj(x_src, norm1_g[l], mod[0], mod[1], even_w_in, e,
                                           (qd, qd, vd, vd, CONV_CH, CONV_CH))
            conv = (a, ga, conv_w[e], conv_b[e], conv_ln_g[e], conv_ln_b[e])
            o_f, new_ret, u = _retention(ret_decay[e], q, k, v, cos_tab, sin_tab, state_ret, new_ret, e,
                                         reverse=False, conv=conv)
            ret, new_ret = _retention(ret_decay[e], q, k, v, cos_tab, sin_tab, state_ret, new_ret, e,
                                      reverse=True, o_fwd=o_f, g=g, gn=ret_gn[e])
            mixes, w_out, w_index = [ret, u], even_w_out, e
        else:
            o = l // 2
            qd, vd = GLA_HEADS * GLA_DK, GLA_HEADS * GLA_DV
            x, q, k, v, r, alr = _inproj(x_src, norm1_g[l], mod[0], mod[1], odd_w_in, o,
                                         (qd, qd, vd, vd, 2 * GLA_RANK))
            zeros = jnp.zeros((GLA_RANK, qd), F32)
            wa_f = jnp.concatenate([gla_w_a2[o, 0], zeros], axis=0)
            wa_b = jnp.concatenate([zeros, gla_w_a2[o, 1]], axis=0)
            o_f, new_gla = _gla(alr, q, k, v, wa_f, gla_b_a2[o, 0].reshape(1, qd), state_gla, new_gla, o,
                                reverse=False)
            y, new_gla = _gla(alr, q, k, v, wa_b, gla_b_a2[o, 1].reshape(1, qd), state_gla, new_gla, o,
                              reverse=True, o_fwd=o_f, r=r, gn=gla_gn[o])
            mixes, w_out, w_index = [y], odd_w_out, o
        x1, h2, logits_t = _outproj(mixes, w_out, w_index, x, mod[2], norm2_g[l], mod[3], mod[4],
                                    router_w[l], router_b[l])
        idx_t, rank_t, gates_t, counts = _route(logits_t)
        block_e, n_valid, next_e, dest_t = _routing_plan(counts, idx_t, rank_t)
        xs = _sc_scatter_rows(h2, dest_t, MOE_ROWS)
        yb = _moe_experts(l, block_e, n_valid, next_e, xs, exp_w_gu, exp_b_gu, exp_w_down, exp_b_down)
        yg = _sc_gather_rows(yb, dest_t.reshape(TOP_K * NT))
        x_src = ("moe", x1, yg, gates_t, mod[5])
    y_prompt, y_sample = _combine_final(*x_src[1:], final_g)
    y_prompt = y_prompt.reshape(BATCH, SEQ, D_MODEL)
    y_sample = y_sample.reshape(DEC_BATCH, DEC_SEQ, D_MODEL)
    return (y_prompt, y_sample, new_ret, new_gla)
```

```python
import functools

import jax
import jax.numpy as jnp
from jax import lax
from jax.experimental import pallas as pl
from jax.experimental.pallas import tpu as pltpu
from jax.experimental.pallas import tpu_sc as plsc

F32 = jnp.float32
BF16 = jnp.bfloat16

D_MODEL = 1024
BATCH = 16
SEQ = 256
DEPTH = 4
DEC_BATCH = 4
DEC_SEQ = 4096
GRID_W = 64
RET_HEADS = 4
RET_DK = 64
RET_DV = 128
RET_CHUNK = 128
CONV_CH = 512
CONV_WIDTH = 31
CONV_PAD = CONV_WIDTH // 2
GLA_HEADS = 4
GLA_DK = 128
GLA_DV = 256
GLA_RANK = 16
GLA_TAU = 16.0
GLA_CHUNK = 64
GLA_SUB = 16
N_EXPERTS = 32
TOP_K = 4
D_FF = 1024
SWIGLU_LIMIT = 7.0
SWIGLU_ALPHA = 1.702
ROPE_THETA = 10000.0
EPS = 1e-6
N_MOD = 6

RB = 256
NT_PROMPT = BATCH * SEQ
NT = NT_PROMPT + DEC_BATCH * DEC_SEQ
NBLK = NT // RB
N_PROMPT_BLK = NT_PROMPT // RB
SAMPLE_BLK = DEC_SEQ // RB
MOD_ROWS = 8
HALO = 16
VMEM_LIMIT = 48 * 1024 * 1024

assert SEQ == RB and DEC_SEQ % RB == 0 and CONV_PAD < HALO


def _seq_of_block(i):
    return jnp.where(i < N_PROMPT_BLK, i, N_PROMPT_BLK + (i - N_PROMPT_BLK) // SAMPLE_BLK)


def _is_first_block(i):
    return jnp.logical_or(i < N_PROMPT_BLK, (i - N_PROMPT_BLK) % SAMPLE_BLK == 0)


def _is_last_block(i):
    return jnp.logical_or(i < N_PROMPT_BLK, (i - N_PROMPT_BLK) % SAMPLE_BLK == SAMPLE_BLK - 1)


def _mod_row(i):
    return jnp.where(i < N_PROMPT_BLK, 0, 1 + (i - N_PROMPT_BLK) // SAMPLE_BLK)


def _rope_block(i):
    return jnp.where(i < N_PROMPT_BLK, 0, 1 + (i - N_PROMPT_BLK) % SAMPLE_BLK)


def _dot(a, b):
    return jnp.dot(a, b, preferred_element_type=F32)


def _dot_nt(a, b):
    return lax.dot_general(a, b, (((1,), (1,)), ((), ())), preferred_element_type=F32)


def _dot_tn(a, b):
    return lax.dot_general(a, b, (((0,), (0,)), ((), ())), preferred_element_type=F32)


def _split2(a):
    hi = a.astype(BF16)
    lo = (a - hi.astype(F32)).astype(BF16)
    return hi, lo


def _dot_hi(a, b):
    a_hi, a_lo = _split2(a)
    b_hi, b_lo = _split2(b)
    return _dot(a_hi, b_hi) + (_dot(a_hi, b_lo) + _dot(a_lo, b_hi))


def _silu(x):
    return x * (1.0 / (1.0 + jnp.exp(-x)))


def _sigmoid(x):
    return 1.0 / (1.0 + jnp.exp(-x))


def _pack_rows(x):
    n = x.shape[1] // 2
    lo = pltpu.bitcast(x[:, :n].astype(BF16).astype(F32), jnp.uint32)
    hi = pltpu.bitcast(x[:, n:].astype(BF16).astype(F32), jnp.uint32)
    return hi | (lo >> 16)


def _unpack_rows(u):
    lo = pltpu.bitcast(u << 16, F32)
    hi = pltpu.bitcast(u & jnp.uint32(0xFFFF0000), F32)
    return lo, hi


def _params(n_axes=1, vmem=VMEM_LIMIT):
    return pltpu.CompilerParams(dimension_semantics=("arbitrary",) * n_axes, vmem_limit_bytes=vmem)


MOD_TN = 1536


def _mod_kernel(c_ref, w_ref, b_ref, o_ref):
    s = _silu(c_ref[...]).astype(BF16)
    o_ref[...] = _dot(s, w_ref[...].astype(BF16)) + b_ref[...]


def _modulation(cvec, w_mod, b_mod):
    n = N_MOD * D_MODEL
    return pl.pallas_call(
        _mod_kernel,
        grid=(DEPTH, n // MOD_TN),
        in_specs=[pl.BlockSpec((MOD_ROWS, D_MODEL), lambda l, j: (0, 0)),
                  pl.BlockSpec((None, D_MODEL, MOD_TN), lambda l, j: (l, 0, j)),
                  pl.BlockSpec((None, 1, MOD_TN), lambda l, j: (l, 0, j))],
        out_specs=pl.BlockSpec((None, MOD_ROWS, MOD_TN), lambda l, j: (l, 0, j)),
        out_shape=jax.ShapeDtypeStruct((DEPTH, MOD_ROWS, n), F32),
        compiler_params=_params(2),
        name="modulation",
    )(cvec, w_mod, b_mod.reshape(DEPTH, 1, n))


def _norm_mod(x, g_ref, shift_ref, scale_ref, row):
    y = x * lax.rsqrt(jnp.mean(x * x, axis=-1, keepdims=True) + EPS) * g_ref[...]
    return y * (1.0 + scale_ref[pl.ds(row, 1), :]) + shift_ref[pl.ds(row, 1), :]


N_SRC = {"split": 2, "moe": 3 + TOP_K}


def _inproj_kernel(*refs, widths, source):
    n_src = N_SRC[source]
    src = refs[:n_src]
    g_ref, shift_ref, scale_ref, w_ref = refs[n_src:n_src + 4]
    outs, w_bf = refs[n_src + 4:-1], refs[-1]
    step = pl.program_id(0)

    @pl.when(step == 0)
    def _():
        w_bf[...] = w_ref[...].astype(BF16)

    row = _mod_row(step)
    if source == "split":
        x = jnp.where(step < N_PROMPT_BLK, src[0][...], src[1][...])
    else:
        x1_ref, yg_refs, gt_ref, gate_ref = src[0], src[1:1 + TOP_K], src[-2], src[-1]
        x = jnp.concatenate(_combined_rows(x1_ref, yg_refs, _gate_columns(gt_ref), gate_ref, row), axis=1)
    outs[0][...] = x
    hb = _norm_mod(x, g_ref, shift_ref, scale_ref, row).astype(BF16)
    off = 0
    for o_ref, width in zip(outs[1:], widths):
        o_ref[...] = _dot(hb, w_bf[:, off:off + width])
        off += width


def _inproj(x, g, shift, scale, w_all, index, widths):
    source, x_args = x[0], list(x[1:])
    n_in = w_all.shape[2]
    row_spec = lambda width: pl.BlockSpec((RB, width), lambda i: (i, 0))
    full = lambda shape: pl.BlockSpec(shape, lambda i: (0,) * len(shape))
    if source == "split":
        x_specs = [pl.BlockSpec((RB, D_MODEL), lambda i: (jnp.minimum(i, N_PROMPT_BLK - 1), 0)),
                   pl.BlockSpec((RB, D_MODEL), lambda i: (jnp.maximum(i - N_PROMPT_BLK, 0), 0))]
    else:
        x1, yg, gates_t, gate = x_args
        x_specs, x_args = _combine_specs(), [x1] + [yg] * TOP_K + [gates_t, gate]
    out_widths = (D_MODEL,) + tuple(widths)
    return pl.pallas_call(
        functools.partial(_inproj_kernel, widths=widths, source=source),
        grid=(NBLK,),
        in_specs=x_specs + [full((1, D_MODEL)), full((MOD_ROWS, D_MODEL)), full((MOD_ROWS, D_MODEL)),
                            pl.BlockSpec((None, D_MODEL, n_in), lambda i: (index, 0, 0))],
        out_specs=[row_spec(width) for width in out_widths],
        out_shape=[jax.ShapeDtypeStruct((NT, width), F32) for width in out_widths],
        scratch_shapes=[pltpu.VMEM((D_MODEL, n_in), BF16)],
        compiler_params=_params(),
        name="inproj",
    )(*x_args, g.reshape(1, D_MODEL), shift, scale, w_all)


RC = RET_CHUNK
RET_PAIR = 2 * RET_DK


def _rope(x, cos, sin_signed):
    lane = lax.broadcasted_iota(jnp.int32, x.shape, 1)
    swapped = jnp.where(lane % 2 == 0, pltpu.roll(x, x.shape[1] - 1, 1), pltpu.roll(x, 1, 1))
    return x * cos + swapped * sin_signed


def _ret_kernel(*refs, reverse):
    for _ in _ret_stages(*refs, reverse=reverse):
        pass


def _ret_stages(decay_ref, q_ref, k_ref, v_ref, cos_ref, sin_ref, s0_ref, acc_ref, *rest, reverse):
    del acc_ref
    if reverse:
        of_ref, g_ref, gn_ref, o_ref, sfin_ref, st_ref, dm_ref, dq_ref, dk_ref, ds_ref = rest
    else:
        o_ref, sfin_ref, st_ref, dm_ref, dq_ref, dk_ref, ds_ref = rest
    step = pl.program_id(0)
    blk = NBLK - 1 - step if reverse else step
    direction = 1 if reverse else 0

    @pl.when(step == 0)
    def _():
        row = lax.broadcasted_iota(jnp.int32, (RC, RC), 0).astype(F32)
        col = lax.broadcasted_iota(jnp.int32, (RC, RC), 1).astype(F32)
        for h in range(RET_HEADS):
            lg = -jnp.exp(jnp.full((RC, RC), decay_ref[direction, h], F32))
            if reverse:
                diff = col - row
                mask = diff > 0
                q_pow = RC - row
                k_pow = row
            else:
                diff = row - col
                mask = diff >= 0
                q_pow = row + 1.0
                k_pow = RC - 1.0 - row
            dm_ref[h] = jnp.where(mask, jnp.exp(lg * jnp.where(mask, diff, 0.0)), 0.0)
            dq_ref[h] = jnp.exp(lg * q_pow)
            dk_ref[h] = jnp.exp(lg * k_pow)
            ds_ref[h] = jnp.exp(lg * RC)

    starts = _is_last_block(blk) if reverse else _is_first_block(blk)

    is_prompt = blk < N_PROMPT_BLK

    @pl.when(starts)
    def _():
        st_ref[...] = jnp.zeros_like(st_ref)

    @pl.when(jnp.logical_and(starts, jnp.logical_not(is_prompt)))
    def _():
        for h in range(RET_HEADS):
            off = (h % 2) * RET_DK
            st_ref[h, off:off + RET_DK, :] = s0_ref[h]

    lane = lax.broadcasted_iota(jnp.int32, (1, RET_PAIR), 1)
    chunks = range(RB // RC)
    chunk_order = list(reversed(chunks) if reverse else chunks)
    units = [(h, c) for h in range(RET_HEADS) for c in chunk_order]
    rows_of = lambda c: slice(c * RC, (c + 1) * RC)
    vcols = lambda h: slice(h * RET_DV, (h + 1) * RET_DV)
    roped = {}
    for p in range(RET_HEADS // 2):
        cols = slice(p * RET_PAIR, (p + 1) * RET_PAIR)
        for c in chunk_order:
            rows = rows_of(c)
            cos, sin = cos_ref[rows, :], sin_ref[rows, :]
            roped[p, c] = (_rope(q_ref[rows, cols], cos, sin),
                           _rope(k_ref[rows, cols] * (RET_DK ** -0.5), cos, sin))
    yield
    qm_u, vh_u, att_u, kd_u = {}, {}, {}, {}
    for h, c in units:
        head_mask = (lane // RET_DK == h % 2).astype(F32)
        q2, k2 = roped[h // 2, c]
        vh_u[h, c] = v_ref[rows_of(c), vcols(h)].astype(BF16)
        qm_u[h, c] = (q2 * head_mask).astype(BF16)
        km = k2 * head_mask
        att_u[h, c] = (_dot_nt(qm_u[h, c], km.astype(BF16)) * dm_ref[h]).astype(BF16)
        kd_u[h, c] = (km * dk_ref[h]).astype(BF16)
    yield
    intra_u = {u: _dot(att_u[u], vh_u[u]) for u in units}
    delta_u = {u: _dot_tn(kd_u[u], vh_u[u]) for u in units}
    yield
    state_u = {}
    for h in range(RET_HEADS):
        st = st_ref[h]
        for c in chunk_order:
            state_u[h, c] = st.astype(BF16)
            st = st * ds_ref[h] + delta_u[h, c]
        st_ref[h] = st
    yield
    for h, c in units:
        rows, out_cols = rows_of(c), vcols(h)
        o = intra_u[h, c] + _dot(qm_u[h, c], state_u[h, c]) * dq_ref[h]
        if reverse:
            o = o + of_ref[rows, out_cols]
            o = o * lax.rsqrt(jnp.mean(o * o, axis=-1, keepdims=True) + EPS)
            o = o * gn_ref[:, out_cols] * _silu(g_ref[rows, out_cols])
            o_ref[rows, out_cols] = o.astype(o_ref.dtype)
        else:
            o_ref[rows, out_cols] = o

    ends = _is_first_block(blk) if reverse else _is_last_block(blk)

    @pl.when(jnp.logical_and(ends, is_prompt))
    def _():
        for h in range(RET_HEADS):
            off = (h % 2) * RET_DK
            sfin_ref[h] = st_ref[h, off:off + RET_DK, :]


def _sample_seq(blk):
    return jnp.clip(_seq_of_block(blk) - BATCH, 0, DEC_BATCH - 1)


def _prompt_seq(blk):
    return jnp.minimum(_seq_of_block(blk), BATCH - 1)


def _retention(decay, q, k, v, cos_tab, sin_tab, state_in, state_out, layer, *, reverse,
               o_fwd=None, g=None, gn=None, conv=None):
    bmap = (lambda j: NBLK - 1 - j) if reverse else (lambda j: j)
    direction = 1 if reverse else 0
    qd, vd = RET_HEADS * RET_DK, RET_HEADS * RET_DV
    row_spec = lambda width: pl.BlockSpec((RB, width), lambda j: (bmap(j), 0))
    state_blk = (None, None, None, RET_HEADS, RET_DK, RET_DV)
    in_specs = [pl.BlockSpec(memory_space=pltpu.SMEM), row_spec(qd), row_spec(qd), row_spec(vd),
                pl.BlockSpec((RB, RET_PAIR), lambda j: (_rope_block(bmap(j)), 0)),
                pl.BlockSpec((RB, RET_PAIR), lambda j: (_rope_block(bmap(j)), 0)),
                pl.BlockSpec(state_blk, lambda j: (_sample_seq(bmap(j)), layer, direction, 0, 0, 0)),
                pl.BlockSpec(memory_space=pl.ANY)]
    args = [decay, q, k, v, cos_tab, sin_tab, state_in, state_out]
    if reverse:
        in_specs += [row_spec(vd), row_spec(vd), pl.BlockSpec((1, vd), lambda j: (0, 0))]
        args += [o_fwd, g, gn.reshape(1, vd)]
    tile = pltpu.VMEM((RET_HEADS, RC, RC), F32)
    out_specs = [row_spec(vd),
                 pl.BlockSpec(state_blk, lambda j: (_prompt_seq(bmap(j)), layer, direction, 0, 0, 0))]
    out_shape = [jax.ShapeDtypeStruct((NT, vd), BF16 if reverse else F32),
                 jax.ShapeDtypeStruct(state_out.shape, F32)]
    scratch = [tile, tile, tile, tile, tile]
    body = functools.partial(_ret_kernel, reverse=reverse)
    if conv is not None:
        assert not reverse
        n_ret = (len(in_specs), len(out_specs), len(scratch))
        c_in, c_args, c_out, c_shape, c_scratch = _conv_parts(*conv)
        in_specs, args = in_specs + c_in, args + c_args
        out_specs, out_shape, scratch = out_specs + c_out, out_shape + c_shape, scratch + c_scratch
        body = functools.partial(_ret_conv_kernel, n_ret=n_ret, n_conv=(len(c_in), len(c_out), len(c_scratch)))
    return pl.pallas_call(
        body,
        grid=(NBLK,),
        in_specs=in_specs,
        out_specs=out_specs,
        out_shape=out_shape,
        input_output_aliases={7: 1},
        scratch_shapes=scratch,
        compiler_params=_params(),
        name="retention_bwd" if reverse else "retention_fwd",
    )(*args)


def _ret_conv_kernel(*refs, n_ret, n_conv):
    (ri, ro, rs), (ci, co, cs) = n_ret, n_conv
    ins, outs, scr = refs[:ri + ci], refs[ri + ci:ri + ci + ro + co], refs[ri + ci + ro + co:]
    bodies = [_conv_stages(*ins[ri:], *outs[ro:], *scr[rs:]),
              _ret_stages(*ins[:ri], *outs[:ro], *scr[:rs], reverse=False)]
    while bodies:
        for body in list(bodies):
            if next(body, StopIteration) is StopIteration:
                bodies.remove(body)


CONV_RT = 32
CONV_CT = 128
CONV_SPAN = RB + 2 * HALO - 8


def _conv_stages(a_ref, ga_ref, ap_ref, gap_ref, an_ref, gan_ref, cw_ref, cb_ref, lng_ref, lnb_ref,
                 o_ref, u_ref, y_ref, us_ref):
    blk = pl.program_id(0)
    keep_prev = jnp.where(_is_first_block(blk), 0.0, 1.0)
    keep_next = jnp.where(_is_last_block(blk), 0.0, 1.0)
    u_ref[0:HALO, :] = ap_ref[...] * _sigmoid(gap_ref[...]) * keep_prev
    u_ref[HALO:HALO + RB, :] = a_ref[...] * _sigmoid(ga_ref[...])
    u_ref[HALO + RB:HALO + RB + HALO, :] = an_ref[...] * _sigmoid(gan_ref[...]) * keep_next
    for r in range(1, 8):
        us_ref[r - 1] = u_ref[r:r + CONV_SPAN, :]
    for ct in range(CONV_CH // CONV_CT):
        yield
        cols = slice(ct * CONV_CT, (ct + 1) * CONV_CT)
        for rt in range(RB // CONV_RT):
            acc = jnp.zeros((CONV_RT, CONV_CT), F32)
            for w in range(CONV_WIDTH):
                tiles, r = divmod(HALO - CONV_PAD + w, 8)
                base = rt * CONV_RT + 8 * tiles
                src = u_ref if r == 0 else us_ref.at[r - 1]
                acc = acc + src[base:base + CONV_RT, cols] * cw_ref[w:w + 1, cols]
            y_ref[rt * CONV_RT:(rt + 1) * CONV_RT, cols] = acc + cb_ref[:, cols]
    y = y_ref[...]
    mu = jnp.mean(y, axis=-1, keepdims=True)
    var = jnp.mean(jnp.square(y - mu), axis=-1, keepdims=True)
    o_ref[...] = _silu((y - mu) * lax.rsqrt(var + EPS) * lng_ref[...] + lnb_ref[...]).astype(o_ref.dtype)


def _conv_parts(a, ga, cw, cb, lng, lnb):
    per_blk = RB // HALO
    n_halo = NT // HALO
    row_spec = pl.BlockSpec((RB, CONV_CH), lambda i: (i, 0))
    prev_spec = pl.BlockSpec((HALO, CONV_CH), lambda i: (jnp.maximum(i * per_blk - 1, 0), 0))
    next_spec = pl.BlockSpec((HALO, CONV_CH), lambda i: (jnp.minimum((i + 1) * per_blk, n_halo - 1), 0))
    vec = pl.BlockSpec((1, CONV_CH), lambda i: (0, 0))
    in_specs = [row_spec, row_spec, prev_spec, prev_spec, next_spec, next_spec,
                pl.BlockSpec((CONV_WIDTH, CONV_CH), lambda i: (0, 0)), vec, vec, vec]
    args = [a, ga, a, ga, a, ga, cw, cb.reshape(1, -1), lng.reshape(1, -1), lnb.reshape(1, -1)]
    scratch = [pltpu.VMEM((RB + 2 * HALO, CONV_CH), F32), pltpu.VMEM((RB, CONV_CH), F32),
               pltpu.VMEM((7, CONV_SPAN, CONV_CH), F32)]
    return in_specs, args, [row_spec], [jax.ShapeDtypeStruct((NT, CONV_CH), BF16)], scratch


GC = GLA_CHUNK
GLA_NSUB = GC // GLA_SUB


def _split3(a):
    p1 = a.astype(BF16)
    r1 = a - p1.astype(F32)
    p2 = r1.astype(BF16)
    p3 = (r1 - p2.astype(F32)).astype(BF16)
    return p1, p2, p3


def _gla_kernel(alr_ref, q_ref, k_ref, v_ref, wa_ref, ba_ref, s0_ref, acc_ref, *rest, reverse):
    del acc_ref
    if reverse:
        of_ref, r_ref, gn_ref, o_ref, sfin_ref, st_ref, b_ref, tri_ref, ob_ref = rest
    else:
        o_ref, sfin_ref, st_ref, b_ref, tri_ref = rest
        ob_ref = o_ref
    step = pl.program_id(0)
    blk = NBLK - 1 - step if reverse else step
    starts = _is_last_block(blk) if reverse else _is_first_block(blk)
    is_prompt = blk < N_PROMPT_BLK

    @pl.when(jnp.logical_and(starts, is_prompt))
    def _():
        st_ref[...] = jnp.zeros_like(st_ref)

    @pl.when(jnp.logical_and(starts, jnp.logical_not(is_prompt)))
    def _():
        for h in range(GLA_HEADS):
            st_ref[h] = s0_ref[h].T

    @pl.when(step == 0)
    def _():
        row = lax.broadcasted_iota(jnp.int32, (RB, RB), 0)
        col = lax.broadcasted_iota(jnp.int32, (RB, RB), 1)
        ordered = col >= row if reverse else col <= row
        tri_ref[...] = jnp.where(jnp.logical_and(row // GC == col // GC, ordered), 1.0, 0.0).astype(BF16)

    heads = range(GLA_HEADS)
    hcols = [slice(h * GLA_DK, (h + 1) * GLA_DK) for h in heads]
    alr = alr_ref[...]
    z = [_dot_hi(alr, wa_ref[:, hcols[h]]) + ba_ref[:, hcols[h]] for h in heads]
    log_a = [(jnp.minimum(zh, 0.0) - jnp.log(1.0 + jnp.exp(-jnp.abs(zh)))) * (1.0 / GLA_TAU) for zh in z]
    parts = [_split3(la) for la in log_a]
    tri = tri_ref[...]
    for h in heads:
        g1, g2, g3 = parts[h]
        b_ref[:, hcols[h]] = _dot(tri, g1) + (_dot(tri, g2) + _dot(tri, g3))

    c_row = lax.broadcasted_iota(jnp.int32, (GC, 1), 0)
    a_row = lax.broadcasted_iota(jnp.int32, (GC, GC), 0)
    a_col = lax.broadcasted_iota(jnp.int32, (GC, GC), 1)
    att_mask = a_col > a_row if reverse else a_col <= a_row
    chunks = range(RB // GC)
    chunk_order = list(reversed(chunks) if reverse else chunks)
    units = [(h, c) for h in range(GLA_HEADS) for c in chunk_order]
    kcols = lambda h: slice(h * GLA_DK, (h + 1) * GLA_DK)
    vcols = lambda h: slice(h * GLA_DV, (h + 1) * GLA_DV)
    rows_of = lambda c: slice(c * GC, (c + 1) * GC)

    vh_u, qe_u, ke_u, decay_u, att_u = {}, {}, {}, {}, {}
    for h, c in units:
        rows = rows_of(c)
        b = b_ref[rows, kcols(h)]
        qh = q_ref[rows, kcols(h)] * (GLA_DK ** -0.5)
        kh = k_ref[rows, kcols(h)]
        vh_u[h, c] = v_ref[rows, vcols(h)].astype(BF16)
        edge = b[0:1, :] if reverse else b[GC - 1:GC, :]
        bounds = []
        for s in range(GLA_NSUB):
            if reverse:
                hi = (s + 1) * GLA_SUB
                bounds.append(b[hi:hi + 1, :] if s < GLA_NSUB - 1 else jnp.zeros((1, GLA_DK), F32))
            else:
                lo = s * GLA_SUB
                bounds.append(b[lo - 1:lo, :] if s > 0 else jnp.zeros((1, GLA_DK), F32))
        own = jnp.concatenate([jnp.broadcast_to(bd, (GLA_SUB, GLA_DK)) for bd in bounds], axis=0)
        q_own = qh * jnp.exp(b - own)
        q_parts, k_parts = [], []
        for s, bd in enumerate(bounds):
            q_parts.append(jnp.where(c_row // GLA_SUB == s, q_own, 0.0))
            reach = c_row >= s * GLA_SUB if reverse else c_row < (s + 1) * GLA_SUB
            k_parts.append(kh * jnp.exp(jnp.where(reach, bd - b, -jnp.inf)))
        q_bd = jnp.concatenate(q_parts, axis=1).astype(BF16)
        k_cat = jnp.concatenate(k_parts, axis=1).astype(BF16)
        att_u[h, c] = jnp.where(att_mask, _dot_nt(q_bd, k_cat), 0.0).astype(BF16)
        qe_u[h, c] = (qh * jnp.exp(b)).astype(BF16)
        ke_u[h, c] = (kh * jnp.exp(edge - b)).astype(BF16)
        decay_u[h, c] = jnp.exp(edge)
    intra_u = {u: _dot(att_u[u], vh_u[u]) for u in units}
    delta_u = {u: _dot_tn(vh_u[u], ke_u[u]) for u in units}
    state_u = {}
    for h in range(GLA_HEADS):
        st = st_ref[h]
        for c in chunk_order:
            state_u[h, c] = st.astype(BF16)
            st = st * decay_u[h, c] + delta_u[h, c]
        st_ref[h] = st
    for h, c in units:
        ob_ref[rows_of(c), vcols(h)] = intra_u[h, c] + _dot_nt(qe_u[h, c], state_u[h, c])

    if reverse:
        for h in range(GLA_HEADS):
            cols = vcols(h)
            o = ob_ref[:, cols] + of_ref[:, cols]
            o = o * lax.rsqrt(jnp.mean(o * o, axis=-1, keepdims=True) + EPS)
            o_ref[:, cols] = (o * gn_ref[:, cols] * _silu(r_ref[:, cols])).astype(o_ref.dtype)

    ends = _is_first_block(blk) if reverse else _is_last_block(blk)

    @pl.when(jnp.logical_and(ends, is_prompt))
    def _():
        for h in range(GLA_HEADS):
            sfin_ref[h] = st_ref[h].T


def _gla(alr, q, k, v, wa, ba, state_in, state_out, layer, *, reverse, o_fwd=None, r=None, gn=None):
    bmap = (lambda j: NBLK - 1 - j) if reverse else (lambda j: j)
    direction = 1 if reverse else 0
    qd, vd = GLA_HEADS * GLA_DK, GLA_HEADS * GLA_DV
    row_spec = lambda width: pl.BlockSpec((RB, width), lambda j: (bmap(j), 0))
    state_blk = (None, None, None, GLA_HEADS, GLA_DK, GLA_DV)
    in_specs = [row_spec(2 * GLA_RANK), row_spec(qd), row_spec(qd), row_spec(vd),
                pl.BlockSpec((2 * GLA_RANK, qd), lambda j: (0, 0)),
                pl.BlockSpec((1, qd), lambda j: (0, 0)),
                pl.BlockSpec(state_blk, lambda j: (_sample_seq(bmap(j)), layer, direction, 0, 0, 0)),
                pl.BlockSpec(memory_space=pl.ANY)]
    args = [alr, q, k, v, wa, ba, state_in, state_out]
    scratch = [pltpu.VMEM((GLA_HEADS, GLA_DV, GLA_DK), F32), pltpu.VMEM((RB, qd), F32),
               pltpu.VMEM((RB, RB), BF16)]
    if reverse:
        in_specs += [row_spec(vd), row_spec(vd), pl.BlockSpec((1, vd), lambda j: (0, 0))]
        args += [o_fwd, r, gn.reshape(1, vd)]
        scratch += [pltpu.VMEM((RB, vd), F32)]
    return pl.pallas_call(
        functools.partial(_gla_kernel, reverse=reverse),
        grid=(NBLK,),
        in_specs=in_specs,
        out_specs=[row_spec(vd),
                   pl.BlockSpec(state_blk, lambda j: (_prompt_seq(bmap(j)), layer, direction, 0, 0, 0))],
        out_shape=[jax.ShapeDtypeStruct((NT, vd), BF16 if reverse else F32),
                   jax.ShapeDtypeStruct(state_out.shape, F32)],
        input_output_aliases={7: 1},
        scratch_shapes=scratch,
        compiler_params=_params(),
        name="gla_bwd" if reverse else "gla_fwd",
    )(*args)


OUT_SUB = 128


def _outproj_kernel(*refs, n_mix):
    mix_refs = refs[:n_mix]
    (w_ref, x_ref, gate_ref, g2_ref, shift_ref, scale_ref, rw_ref, rb_ref,
     x1_ref, h2_ref, logit_ref, w_bf) = refs[n_mix:]
    step = pl.program_id(0)

    @pl.when(step == 0)
    def _():
        w_bf[...] = w_ref[...].astype(BF16)

    row = _mod_row(step)
    r_hi, r_lo = _split2(rw_ref[...])
    groups = [slice(p * OUT_SUB, (p + 1) * OUT_SUB) for p in range(RB // OUT_SUB)]
    mixed = []
    for rows in groups:
        m, off = None, 0
        for mix_ref in mix_refs:
            width = mix_ref.shape[1]
            part = _dot(mix_ref[rows, :], w_bf[off:off + width, :])
            m = part if m is None else m + part
            off += width
        mixed.append(m)
    normed = []
    for rows, m in zip(groups, mixed):
        x1 = x_ref[rows, :] + gate_ref[pl.ds(row, 1), :] * m
        x1_ref[rows, :] = x1
        h2 = _norm_mod(x1, g2_ref, shift_ref, scale_ref, row)
        h2_ref[rows, :] = _pack_rows(h2)
        normed.append(_split2(h2))
    for rows, (h_hi, h_lo) in zip(groups, normed):
        logit_ref[:, rows] = _dot_nt(r_hi, h_hi) + (_dot_nt(r_hi, h_lo) + _dot_nt(r_lo, h_hi)) + rb_ref[...]


def _outproj(mixes, w_all, index, x, gate, g2, shift, scale, rw, rb):
    n_mix = len(mixes)
    row_spec = lambda width: pl.BlockSpec((RB, width), lambda i: (i, 0))
    full = lambda shape: pl.BlockSpec(shape, lambda i: (0,) * len(shape))
    mod_spec = full((MOD_ROWS, D_MODEL))
    n_mixed = w_all.shape[1]
    return pl.pallas_call(
        functools.partial(_outproj_kernel, n_mix=n_mix),
        grid=(NBLK,),
        in_specs=[row_spec(m.shape[1]) for m in mixes]
        + [pl.BlockSpec((None, n_mixed, D_MODEL), lambda i: (index, 0, 0)),
           row_spec(D_MODEL), mod_spec, full((1, D_MODEL)), mod_spec, mod_spec,
           full((N_EXPERTS, D_MODEL)), full((N_EXPERTS, 1))],
        out_specs=[row_spec(D_MODEL), row_spec(D_MODEL // 2), pl.BlockSpec((N_EXPERTS, RB), lambda i: (0, i))],
        out_shape=[jax.ShapeDtypeStruct((NT, D_MODEL), F32), jax.ShapeDtypeStruct((NT, D_MODEL // 2), jnp.uint32),
                   jax.ShapeDtypeStruct((N_EXPERTS, NT), F32)],
        scratch_shapes=[pltpu.VMEM((n_mixed, D_MODEL), BF16)],
        compiler_params=_params(),
        name="outproj",
    )(*mixes, w_all, x, gate, g2.reshape(1, D_MODEL), shift, scale, rw.T, rb.reshape(N_EXPERTS, 1))


ROUTE_BLK = 2048
ROUTE_SUB = 256


def _route_kernel(lg_ref, idx_ref, rank_ref, gt_ref, cnt_ref, carry_ref):
    @pl.when(pl.program_id(0) == 0)
    def _():
        carry_ref[...] = jnp.zeros_like(carry_ref)

    logits = lg_ref[...]
    eid = lax.broadcasted_iota(jnp.int32, logits.shape, 0).astype(F32)
    work = logits
    onehots, top_vals = [], []
    for kk in range(TOP_K):
        top = jnp.max(work, axis=0, keepdims=True)
        first = jnp.min(jnp.where(work == top, eid, float(N_EXPERTS)), axis=0, keepdims=True)
        onehot = eid == first
        idx_ref[kk:kk + 1, :] = first.astype(jnp.int32)
        onehots.append(onehot)
        top_vals.append(top)
        work = jnp.where(onehot, -jnp.inf, work)
    exps = [jnp.exp(v - top_vals[0]) for v in top_vals]
    denom = exps[0]
    for e in exps[1:]:
        denom = denom + e
    gt_ref[...] = jnp.zeros_like(gt_ref)
    for kk in range(TOP_K):
        gt_ref[kk:kk + 1, :] = exps[kk] / denom

    sel = jnp.zeros(logits.shape, F32)
    for onehot in onehots:
        sel = sel + jnp.where(onehot, 1.0, 0.0)
    sel = sel.astype(BF16)
    r_i = lax.broadcasted_iota(jnp.int32, (ROUTE_SUB, ROUTE_SUB), 0)
    c_i = lax.broadcasted_iota(jnp.int32, (ROUTE_SUB, ROUTE_SUB), 1)
    before = jnp.where(r_i < c_i, 1.0, 0.0).astype(BF16)
    ones = jnp.ones((ROUTE_SUB, ROUTE_SUB), BF16)
    carry = carry_ref[...]
    for s in range(ROUTE_BLK // ROUTE_SUB):
        cols = slice(s * ROUTE_SUB, (s + 1) * ROUTE_SUB)
        pos = _dot(sel[:, cols], before) + carry
        for kk in range(TOP_K):
            rank = jnp.sum(jnp.where(onehots[kk][:, cols], pos, 0.0), axis=0, keepdims=True)
            rank_ref[kk:kk + 1, cols] = rank.astype(jnp.int32)
        carry = carry + _dot(sel[:, cols], ones)
    carry_ref[...] = carry
    cnt_ref[...] = carry.astype(jnp.int32)


def _route(logits_t):
    col_spec = lambda rows: pl.BlockSpec((rows, ROUTE_BLK), lambda i: (0, i))
    return pl.pallas_call(
        _route_kernel,
        grid=(NT // ROUTE_BLK,),
        in_specs=[col_spec(N_EXPERTS)],
        out_specs=[col_spec(TOP_K), col_spec(TOP_K), col_spec(8),
                   pl.BlockSpec((N_EXPERTS, ROUTE_SUB), lambda i: (0, 0))],
        out_shape=[jax.ShapeDtypeStruct((TOP_K, NT), jnp.int32), jax.ShapeDtypeStruct((TOP_K, NT), jnp.int32),
                   jax.ShapeDtypeStruct((8, NT), F32), jax.ShapeDtypeStruct((N_EXPERTS, ROUTE_SUB), jnp.int32)],
        scratch_shapes=[pltpu.VMEM((N_EXPERTS, ROUTE_SUB), F32)],
        compiler_params=_params(),
        name="route",
    )(logits_t)


TM = 1024
TM_SUB = 256
MOE_NBLK = NT * TOP_K // TM + N_EXPERTS
MOE_ROWS = MOE_NBLK * TM
HALF = D_MODEL // 2


def _moe_kernel(be_ref, nv_ref, nx_ref, x_ref, wgu_hbm, bgu_ref, wd_hbm, bd_ref, y_ref,
                wgu_st, wd_st, wgu_bf, wd_bf, sems, *, layer):
    i = pl.program_id(0)
    n_valid = nv_ref[i]

    def fetch(e):
        return (pltpu.make_async_copy(wgu_hbm.at[layer, e], wgu_st, sems.at[0]),
                pltpu.make_async_copy(wd_hbm.at[layer, e], wd_st, sems.at[1]))

    @pl.when(i == 0)
    def _():
        for cp in fetch(be_ref[0]):
            cp.start()

    @pl.when(n_valid > 0)
    def _():
        e = be_ref[i]
        changed = jnp.logical_or(i == 0, e != be_ref[jnp.maximum(i - 1, 0)])

        @pl.when(changed)
        def _():
            for cp in fetch(e):
                cp.wait()
            wgu_bf[...] = wgu_st[...].astype(BF16)
            wd_bf[...] = wd_st[...].astype(BF16)
            nxt = nx_ref[e]

            @pl.when(nxt >= 0)
            def _():
                for cp in fetch(nxt):
                    cp.start()

    for p in range(TM // TM_SUB):
        @pl.when(n_valid > p * TM_SUB)
        def _():
            rows = slice(p * TM_SUB, (p + 1) * TM_SUB)
            row_id = lax.broadcasted_iota(jnp.int32, (TM_SUB, 1), 0) + p * TM_SUB
            x_lo, x_hi = _unpack_rows(jnp.where(row_id < n_valid, x_ref[rows, :], jnp.uint32(0)))
            gu = (_dot(x_lo.astype(BF16), wgu_bf[:HALF, :]) + _dot(x_hi.astype(BF16), wgu_bf[HALF:, :])
                  + bgu_ref[...])
            gate = jnp.minimum(gu[:, :D_FF], SWIGLU_LIMIT)
            up = jnp.clip(gu[:, D_FF:], -SWIGLU_LIMIT, SWIGLU_LIMIT)
            hdn = gate * _sigmoid(SWIGLU_ALPHA * gate) * (up + 1.0)
            y_ref[rows, :] = _pack_rows(_dot(hdn.astype(BF16), wd_bf[...]) + bd_ref[...])


def _moe_experts(layer, block_e, n_valid, next_e, xs, w_gu, b_gu, w_down, b_down):
    grid_spec = pltpu.PrefetchScalarGridSpec(
        num_scalar_prefetch=3,
        grid=(MOE_NBLK,),
        in_specs=[pl.BlockSpec((TM, HALF), lambda i, be, nv, nx: (i, 0)),
                  pl.BlockSpec(memory_space=pl.ANY),
                  pl.BlockSpec((None, None, 1, 2 * D_FF), lambda i, be, nv, nx: (layer, be[i], 0, 0)),
                  pl.BlockSpec(memory_space=pl.ANY),
                  pl.BlockSpec((None, None, 1, D_MODEL), lambda i, be, nv, nx: (layer, be[i], 0, 0))],
        out_specs=pl.BlockSpec((TM, HALF), lambda i, be, nv, nx: (i, 0)),
        scratch_shapes=[pltpu.VMEM((D_MODEL, 2 * D_FF), F32), pltpu.VMEM((D_FF, D_MODEL), F32),
                        pltpu.VMEM((D_MODEL, 2 * D_FF), BF16), pltpu.VMEM((D_FF, D_MODEL), BF16),
                        pltpu.SemaphoreType.DMA((2,))],
    )
    return pl.pallas_call(
        functools.partial(_moe_kernel, layer=layer),
        grid_spec=grid_spec,
        out_shape=jax.ShapeDtypeStruct((MOE_ROWS, HALF), jnp.uint32),
        compiler_params=_params(),
        name="moe_experts",
    )(block_e, n_valid, next_e, xs, w_gu, b_gu.reshape(DEPTH, N_EXPERTS, 1, -1), w_down,
      b_down.reshape(DEPTH, N_EXPERTS, 1, -1))


SC_WORKERS = 32
SC_WIN = 64


def _sc_mesh():
    return plsc.VectorSubcoreMesh(core_axis_name="core", subcore_axis_name="subcore")


def _sc_worker():
    return lax.axis_index("core") * (SC_WORKERS // 2) + lax.axis_index("subcore")


def _sc_scatter_rows(x, dest_t, n_rows):
    n, width = x.shape
    kk = dest_t.shape[0]
    per = n // SC_WORKERS
    n_win = per // SC_WIN
    assert per * SC_WORKERS == n and n_win * SC_WIN == per and n_win % 2 == 0

    @pl.kernel(out_type=jax.ShapeDtypeStruct((n_rows, width), x.dtype), mesh=_sc_mesh(),
               scratch_types=[pltpu.VMEM((kk, per), jnp.int32), pltpu.VMEM((SC_WIN, width), x.dtype),
                              pltpu.VMEM((SC_WIN, width), x.dtype), pltpu.SemaphoreType.DMA((4,))])
    def scatter(x_hbm, i_hbm, o_hbm, idx_v, buf0, buf1, sems):
        base = _sc_worker() * per
        pltpu.sync_copy(i_hbm.at[:, pl.ds(base, per)], idx_v)

        def get(j, buf, s):
            return pltpu.make_async_copy(x_hbm.at[pl.ds(base + j * SC_WIN, SC_WIN)], buf, sems.at[s])

        def put(j, q, buf, s):
            return pltpu.make_async_copy(buf, o_hbm.at[idx_v.at[q, pl.ds(j * SC_WIN, SC_WIN)]], sems.at[s])

        get(0, buf0, 0).start()

        @pl.loop(0, n_win, step=2)
        def _(j):
            get(j, buf0, 0).wait()

            @pl.when(j > 0)
            def _():
                for q in range(kk):
                    put(j - 1, q, buf1, 3).wait()

            get(j + 1, buf1, 1).start()
            for q in range(kk):
                put(j, q, buf0, 2).start()
            get(j + 1, buf1, 1).wait()
            for q in range(kk):
                put(j, q, buf0, 2).wait()

            @pl.when(j + 2 < n_win)
            def _():
                get(j + 2, buf0, 0).start()

            for q in range(kk):
                put(j + 1, q, buf1, 3).start()

        for q in range(kk):
            put(n_win - 1, q, buf1, 3).wait()

    return scatter(x, dest_t)


def _sc_gather_rows(y, idx):
    n = idx.shape[0]
    width = y.shape[1]
    per = n // SC_WORKERS
    n_win = per // SC_WIN
    assert per * SC_WORKERS == n and n_win * SC_WIN == per and n_win % 2 == 0

    @pl.kernel(out_type=jax.ShapeDtypeStruct((n, width), y.dtype), mesh=_sc_mesh(),
               scratch_types=[pltpu.VMEM((per,), jnp.int32), pltpu.VMEM((SC_WIN, width), y.dtype),
                              pltpu.VMEM((SC_WIN, width), y.dtype), pltpu.SemaphoreType.DMA((4,))])
    def gather(y_hbm, i_hbm, o_hbm, idx_v, buf0, buf1, sems):
        base = _sc_worker() * per
        pltpu.sync_copy(i_hbm.at[pl.ds(base, per)], idx_v)

        def get(j, buf, s):
            return pltpu.make_async_copy(y_hbm.at[idx_v.at[pl.ds(j * SC_WIN, SC_WIN)]], buf, sems.at[s])

        def put(j, buf, s):
            return pltpu.make_async_copy(buf, o_hbm.at[pl.ds(base + j * SC_WIN, SC_WIN)], sems.at[s])

        get(0, buf0, 0).start()

        @pl.loop(0, n_win, step=2)
        def _(j):
            get(j, buf0, 0).wait()

            @pl.when(j > 0)
            def _():
                put(j - 1, buf1, 3).wait()

            get(j + 1, buf1, 1).start()
            put(j, buf0, 2).start()
            get(j + 1, buf1, 1).wait()
            put(j, buf0, 2).wait()

            @pl.when(j + 2 < n_win)
            def _():
                get(j + 2, buf0, 0).start()

            put(j + 1, buf1, 3).start()

        put(n_win - 1, buf1, 3).wait()

    return gather(y, idx)


def _gate_columns(gt_ref):
    r_i = lax.broadcasted_iota(jnp.int32, (RB, RB), 0)
    c_i = lax.broadcasted_iota(jnp.int32, (RB, RB), 1)
    eye = jnp.where(r_i == c_i, 1.0, 0.0).astype(BF16)
    g1, g2, g3 = _split3(gt_ref[...])
    return _dot_nt(eye, g1) + (_dot_nt(eye, g2) + _dot_nt(eye, g3))


def _combined_rows(x1_ref, yg_refs, gw, gate_ref, row, rows=slice(None)):
    acc_lo, acc_hi = None, None
    for kk in range(TOP_K):
        y_lo, y_hi = _unpack_rows(yg_refs[kk][rows, :])
        w = gw[rows, kk:kk + 1]
        acc_lo = y_lo * w if acc_lo is None else acc_lo + y_lo * w
        acc_hi = y_hi * w if acc_hi is None else acc_hi + y_hi * w
    x_lo = x1_ref[rows, :HALF] + gate_ref[pl.ds(row, 1), :HALF] * acc_lo
    x_hi = x1_ref[rows, HALF:] + gate_ref[pl.ds(row, 1), HALF:] * acc_hi
    return x_lo, x_hi


def _combine_specs():
    slot_spec = lambda k: pl.BlockSpec((RB, HALF), lambda i: (k * NBLK + i, 0))
    return ([pl.BlockSpec((RB, D_MODEL), lambda i: (i, 0))] + [slot_spec(k) for k in range(TOP_K)]
            + [pl.BlockSpec((8, RB), lambda i: (0, i)), pl.BlockSpec((MOD_ROWS, D_MODEL), lambda i: (0, 0))])


def _combine_final_kernel(x1_ref, *rest):
    yg_refs = rest[:TOP_K]
    gt_ref, gate_ref, fg_ref, op_ref, os_ref = rest[TOP_K:]
    i = pl.program_id(0)
    x_lo, x_hi = _combined_rows(x1_ref, yg_refs, _gate_columns(gt_ref), gate_ref, _mod_row(i))
    ms = (jnp.sum(x_lo * x_lo, axis=-1, keepdims=True) + jnp.sum(x_hi * x_hi, axis=-1, keepdims=True)) / D_MODEL
    scale = lax.rsqrt(ms + EPS)

    @pl.when(i < N_PROMPT_BLK)
    def _():
        op_ref[:, :HALF] = x_lo * scale * fg_ref[:, :HALF]
        op_ref[:, HALF:] = x_hi * scale * fg_ref[:, HALF:]

    @pl.when(i >= N_PROMPT_BLK)
    def _():
        os_ref[:, :HALF] = x_lo * scale * fg_ref[:, :HALF]
        os_ref[:, HALF:] = x_hi * scale * fg_ref[:, HALF:]


def _combine_final(x1, yg, gates_t, gate, final_g):
    return pl.pallas_call(
        _combine_final_kernel,
        grid=(NBLK,),
        in_specs=_combine_specs() + [pl.BlockSpec((1, D_MODEL), lambda i: (0, 0))],
        out_specs=[pl.BlockSpec((RB, D_MODEL), lambda i: (jnp.minimum(i, N_PROMPT_BLK - 1), 0)),
                   pl.BlockSpec((RB, D_MODEL), lambda i: (jnp.maximum(i - N_PROMPT_BLK, 0), 0))],
        out_shape=[jax.ShapeDtypeStruct((NT_PROMPT, D_MODEL), F32),
                   jax.ShapeDtypeStruct((NT - NT_PROMPT, D_MODEL), F32)],
        compiler_params=_params(),
        name="moe_combine_final",
    )(x1, *([yg] * TOP_K), gates_t, gate, final_g.reshape(1, D_MODEL))


def _routing_plan(counts, idx_t, rank_t):
    counts = counts[:, 0]
    padded = (counts + TM - 1) // TM * TM
    pad_end = jnp.cumsum(padded)
    pad_start = pad_end - padded
    blk_row = (jnp.arange(MOE_NBLK, dtype=jnp.int32) * TM)[:, None]
    ids = jnp.arange(N_EXPERTS, dtype=jnp.int32)
    owns = jnp.logical_and(pad_start[None, :] <= blk_row, blk_row < pad_end[None, :])
    last_used = jnp.max(jnp.where(counts > 0, ids, 0))
    block_e = jnp.where(jnp.any(owns, axis=1), jnp.sum(jnp.where(owns, ids[None, :], 0), axis=1), last_used)
    block_e = block_e.astype(jnp.int32)
    left = jnp.clip(counts[None, :] - (blk_row - pad_start[None, :]), 0, TM)
    n_valid = jnp.sum(jnp.where(owns, left, 0), axis=1).astype(jnp.int32)
    start = jnp.zeros(idx_t.shape, jnp.int32)
    for e in range(N_EXPERTS):
        start = jnp.where(idx_t == e, pad_start[e], start)
    dest_t = (start + rank_t).astype(jnp.int32)
    later = jnp.where(jnp.logical_and(counts[None, :] > 0, ids[None, :] > ids[:, None]), ids[None, :], N_EXPERTS)
    next_e = jnp.min(later, axis=1)
    next_e = jnp.where(next_e == N_EXPERTS, -1, next_e).astype(jnp.int32)
    return block_e, n_valid, next_e, dest_t


def _rope_tables():
    rows = DEC_SEQ // GRID_W
    row = jnp.repeat(jnp.arange(rows, dtype=F32), GRID_W)
    col = jnp.tile(jnp.arange(GRID_W, dtype=F32), rows)
    n_f = RET_DK // 4
    freqs = ROPE_THETA ** (-jnp.arange(n_f, dtype=F32) / n_f)
    ang = jnp.concatenate([row[:, None] * freqs, col[:, None] * freqs], axis=-1)
    cos = jnp.repeat(jnp.cos(ang), 2, axis=-1)
    sin = jnp.repeat(jnp.sin(ang), 2, axis=-1) * jnp.tile(jnp.asarray([-1.0, 1.0], F32), RET_DK // 2)
    cos = jnp.concatenate([jnp.ones((RB, RET_DK), F32), cos], axis=0)
    sin = jnp.concatenate([jnp.zeros((RB, RET_DK), F32), sin], axis=0)
    return jnp.tile(cos, (1, 2)), jnp.tile(sin, (1, 2))


def kernel(x_prompt, x_sample, state_ret, state_gla, c, c_ctx, w_mod, b_mod, norm1_g, norm2_g, final_g, even_w_in, ret_decay, ret_gn, conv_w, conv_b, conv_ln_g, conv_ln_b, even_w_out, odd_w_in, gla_w_a2, gla_b_a2, gla_gn, odd_w_out, router_w, router_b, exp_w_gu, exp_b_gu, exp_w_down, exp_b_down):
    x_src = ("split", x_prompt.reshape(NT_PROMPT, D_MODEL), x_sample.reshape(NT - NT_PROMPT, D_MODEL))
    cvec = jnp.concatenate([c_ctx[None, :], c, jnp.zeros((MOD_ROWS - 1 - DEC_BATCH, D_MODEL), F32)], axis=0)
    mods = _modulation(cvec, w_mod, b_mod).reshape(DEPTH, MOD_ROWS, N_MOD, D_MODEL)
    cos_tab, sin_tab = _rope_tables()
    new_ret = jnp.zeros((BATCH,) + state_ret.shape[1:], F32)
    new_gla = jnp.zeros((BATCH,) + state_gla.shape[1:], F32)
    for l in range(DEPTH):
        mod = [mods[l, :, j, :] for j in range(N_MOD)]
        if l % 2 == 0:
            e = l // 2
            qd, vd = RET_HEADS * RET_DK, RET_HEADS * RET_DV
            x, q, k, v, g, a, ga = _inproj(x_src, norm1_g[l], mod[0], mod[1], even_w_in, e,
                                           (qd, qd, vd, vd, CONV_CH, CONV_CH))
            conv = (a, ga, conv_w[e], conv_b[e], conv_ln_g[e], conv_ln_b[e])
            o_f, new_ret, u = _retention(ret_decay[e], q, k, v, cos_tab, sin_tab, state_ret, new_ret, e,
                                         reverse=False, conv=conv)
            ret, new_ret = _retention(ret_decay[e], q, k, v, cos_tab, sin_tab, state_ret, new_ret, e,
                                      reverse=True, o_fwd=o_f, g=g, gn=ret_gn[e])
            mixes, w_out, w_index = [ret, u], even_w_out, e
        else:
            o = l // 2
            qd, vd = GLA_HEADS * GLA_DK, GLA_HEADS * GLA_DV
            x, q, k, v, r, alr = _inproj(x_src, norm1_g[l], mod[0], mod[1], odd_w_in, o,
                                         (qd, qd, vd, vd, 2 * GLA_RANK))
            zeros = jnp.zeros((GLA_RANK, qd), F32)
            wa_f = jnp.concatenate([gla_w_a2[o, 0], zeros], axis=0)
            wa_b = jnp.concatenate([zeros, gla_w_a2[o, 1]], axis=0)
            o_f, new_gla = _gla(alr, q, k, v, wa_f, gla_b_a2[o, 0].reshape(1, qd), state_gla, new_gla, o,
                                reverse=False)
            y, new_gla = _gla(alr, q, k, v, wa_b, gla_b_a2[o, 1].reshape(1, qd), state_gla, new_gla, o,
                              reverse=True, o_fwd=o_f, r=r, gn=gla_gn[o])
            mixes, w_out, w_index = [y], odd_w_out, o
        x1, h2, logits_t = _outproj(mixes, w_out, w_index, x, mod[2], norm2_g[l], mod[3], mod[4],
                                    router_w[l], router_b[l])
        idx_t, rank_t, gates_t, counts = _route(logits_t)
        block_e, n_valid, next_e, dest_t = _routing_plan(counts, idx_t, rank_t)
        xs = _sc_scatter_rows(h2, dest_t, MOE_ROWS)
        yb = _moe_experts(l, block_e, n_valid, next_e, xs, exp_w_gu, exp_b_gu, exp_w_down, exp_b_down)
        yg = _sc_gather_rows(yb, dest_t.reshape(TOP_K * NT))
        x_src = ("moe", x1, yg, gates_t, mod[5])
    y_prompt, y_sample = _combine_final(*x_src[1:], final_g)
    y_prompt = y_prompt.reshape(BATCH, SEQ, D_MODEL)
    y_sample = y_sample.reshape(DEC_BATCH, DEC_SEQ, D_MODEL)
    return (y_prompt, y_sample, new_ret, new_gla)
```

```python
import functools

import jax
import jax.numpy as jnp
from jax import lax
from jax.experimental import pallas as pl
from jax.experimental.pallas import tpu as pltpu
from jax.experimental.pallas import tpu_sc as plsc

F32 = jnp.float32
BF16 = jnp.bfloat16

D_MODEL = 1024
BATCH = 16
SEQ = 256
DEPTH = 4
DEC_BATCH = 4
DEC_SEQ = 4096
GRID_W = 64
RET_HEADS = 4
RET_DK = 64
RET_DV = 128
RET_CHUNK = 128
CONV_CH = 512
CONV_WIDTH = 31
CONV_PAD = CONV_WIDTH // 2
GLA_HEADS = 4
GLA_DK = 128
GLA_DV = 256
GLA_RANK = 16
GLA_TAU = 16.0
GLA_CHUNK = 64
GLA_SUB = 16
N_EXPERTS = 32
TOP_K = 4
D_FF = 1024
SWIGLU_LIMIT = 7.0
SWIGLU_ALPHA = 1.702
ROPE_THETA = 10000.0
EPS = 1e-6
N_MOD = 6

RB = 256
NT_PROMPT = BATCH * SEQ
NT = NT_PROMPT + DEC_BATCH * DEC_SEQ
NBLK = NT // RB
N_PROMPT_BLK = NT_PROMPT // RB
SAMPLE_BLK = DEC_SEQ // RB
MOD_ROWS = 8
HALO = 16
VMEM_LIMIT = 48 * 1024 * 1024

assert SEQ == RB and DEC_SEQ % RB == 0 and CONV_PAD < HALO


def _seq_of_block(i):
    return jnp.where(i < N_PROMPT_BLK, i, N_PROMPT_BLK + (i - N_PROMPT_BLK) // SAMPLE_BLK)


def _is_first_block(i):
    return jnp.logical_or(i < N_PROMPT_BLK, (i - N_PROMPT_BLK) % SAMPLE_BLK == 0)


def _is_last_block(i):
    return jnp.logical_or(i < N_PROMPT_BLK, (i - N_PROMPT_BLK) % SAMPLE_BLK == SAMPLE_BLK - 1)


def _mod_row(i):
    return jnp.where(i < N_PROMPT_BLK, 0, 1 + (i - N_PROMPT_BLK) // SAMPLE_BLK)


def _rope_block(i):
    return jnp.where(i < N_PROMPT_BLK, 0, 1 + (i - N_PROMPT_BLK) % SAMPLE_BLK)


def _dot(a, b):
    return jnp.dot(a, b, preferred_element_type=F32)


def _dot_nt(a, b):
    return lax.dot_general(a, b, (((1,), (1,)), ((), ())), preferred_element_type=F32)


def _dot_tn(a, b):
    return lax.dot_general(a, b, (((0,), (0,)), ((), ())), preferred_element_type=F32)


def _split2(a):
    hi = a.astype(BF16)
    lo = (a - hi.astype(F32)).astype(BF16)
    return hi, lo


def _dot_hi(a, b):
    a_hi, a_lo = _split2(a)
    b_hi, b_lo = _split2(b)
    return _dot(a_hi, b_hi) + (_dot(a_hi, b_lo) + _dot(a_lo, b_hi))


def _silu(x):
    return x * (1.0 / (1.0 + jnp.exp(-x)))


def _sigmoid(x):
    return 1.0 / (1.0 + jnp.exp(-x))


def _pack_rows(x):
    n = x.shape[1] // 2
    lo = pltpu.bitcast(x[:, :n].astype(BF16).astype(F32), jnp.uint32)
    hi = pltpu.bitcast(x[:, n:].astype(BF16).astype(F32), jnp.uint32)
    return hi | (lo >> 16)


def _unpack_rows(u):
    lo = pltpu.bitcast(u << 16, F32)
    hi = pltpu.bitcast(u & jnp.uint32(0xFFFF0000), F32)
    return lo, hi


def _params(n_axes=1, vmem=VMEM_LIMIT):
    return pltpu.CompilerParams(dimension_semantics=("arbitrary",) * n_axes, vmem_limit_bytes=vmem)


MOD_TN = 1536


def _mod_kernel(c_ref, w_ref, b_ref, o_ref):
    s = _silu(c_ref[...]).astype(BF16)
    o_ref[...] = _dot(s, w_ref[...].astype(BF16)) + b_ref[...]


def _modulation(cvec, w_mod, b_mod):
    n = N_MOD * D_MODEL
    return pl.pallas_call(
        _mod_kernel,
        grid=(DEPTH, n // MOD_TN),
        in_specs=[pl.BlockSpec((MOD_ROWS, D_MODEL), lambda l, j: (0, 0)),
                  pl.BlockSpec((None, D_MODEL, MOD_TN), lambda l, j: (l, 0, j)),
                  pl.BlockSpec((None, 1, MOD_TN), lambda l, j: (l, 0, j))],
        out_specs=pl.BlockSpec((None, MOD_ROWS, MOD_TN), lambda l, j: (l, 0, j)),
        out_shape=jax.ShapeDtypeStruct((DEPTH, MOD_ROWS, n), F32),
        compiler_params=_params(2),
        name="modulation",
    )(cvec, w_mod, b_mod.reshape(DEPTH, 1, n))


def _norm_mod(x, g_ref, shift_ref, scale_ref, row):
    y = x * lax.rsqrt(jnp.mean(x * x, axis=-1, keepdims=True) + EPS) * g_ref[...]
    return y * (1.0 + scale_ref[pl.ds(row, 1), :]) + shift_ref[pl.ds(row, 1), :]


N_SRC = {"split": 2, "moe": 3 + TOP_K}


def _inproj_kernel(*refs, widths, source):
    n_src = N_SRC[source]
    src = refs[:n_src]
    g_ref, shift_ref, scale_ref, w_ref = refs[n_src:n_src + 4]
    outs, w_bf = refs[n_src + 4:-1], refs[-1]
    step = pl.program_id(0)

    @pl.when(step == 0)
    def _():
        w_bf[...] = w_ref[...].astype(BF16)

    row = _mod_row(step)
    if source == "split":
        x = jnp.where(step < N_PROMPT_BLK, src[0][...], src[1][...])
    else:
        x1_ref, yg_refs, gt_ref, gate_ref = src[0], src[1:1 + TOP_K], src[-2], src[-1]
        x = jnp.concatenate(_combined_rows(x1_ref, yg_refs, _gate_columns(gt_ref), gate_ref, row), axis=1)
    outs[0][...] = x
    hb = _norm_mod(x, g_ref, shift_ref, scale_ref, row).astype(BF16)
    off = 0
    for o_ref, width in zip(outs[1:], widths):
        o_ref[...] = _dot(hb, w_bf[:, off:off + width])
        off += width


def _inproj(x, g, shift, scale, w_all, index, widths):
    source, x_args = x[0], list(x[1:])
    n_in = w_all.shape[2]
    row_spec = lambda width: pl.BlockSpec((RB, width), lambda i: (i, 0))
    full = lambda shape: pl.BlockSpec(shape, lambda i: (0,) * len(shape))
    if source == "split":
        x_specs = [pl.BlockSpec((RB, D_MODEL), lambda i: (jnp.minimum(i, N_PROMPT_BLK - 1), 0)),
                   pl.BlockSpec((RB, D_MODEL), lambda i: (jnp.maximum(i - N_PROMPT_BLK, 0), 0))]
    else:
        x1, yg, gates_t, gate = x_args
        x_specs, x_args = _combine_specs(), [x1] + [yg] * TOP_K + [gates_t, gate]
    out_widths = (D_MODEL,) + tuple(widths)
    return pl.pallas_call(
        functools.partial(_inproj_kernel, widths=widths, source=source),
        grid=(NBLK,),
        in_specs=x_specs + [full((1, D_MODEL)), full((MOD_ROWS, D_MODEL)), full((MOD_ROWS, D_MODEL)),
                            pl.BlockSpec((None, D_MODEL, n_in), lambda i: (index, 0, 0))],
        out_specs=[row_spec(width) for width in out_widths],
        out_shape=[jax.ShapeDtypeStruct((NT, width), F32) for width in out_widths],
        scratch_shapes=[pltpu.VMEM((D_MODEL, n_in), BF16)],
        compiler_params=_params(),
        name="inproj",
    )(*x_args, g.reshape(1, D_MODEL), shift, scale, w_all)


RC = RET_CHUNK
RET_PAIR = 2 * RET_DK


def _rope(x, cos, sin_signed):
    lane = lax.broadcasted_iota(jnp.int32, x.shape, 1)
    swapped = jnp.where(lane % 2 == 0, pltpu.roll(x, x.shape[1] - 1, 1), pltpu.roll(x, 1, 1))
    return x * cos + swapped * sin_signed


def _ret_kernel(*refs, reverse):
    for _ in _ret_stages(*refs, reverse=reverse):
        pass


def _ret_stages(decay_ref, q_ref, k_ref, v_ref, cos_ref, sin_ref, s0_ref, acc_ref, *rest, reverse):
    del acc_ref
    if reverse:
        of_ref, g_ref, gn_ref, o_ref, sfin_ref, st_ref, dm_ref, dq_ref, dk_ref, ds_ref = rest
    else:
        o_ref, sfin_ref, st_ref, dm_ref, dq_ref, dk_ref, ds_ref = rest
    step = pl.program_id(0)
    blk = NBLK - 1 - step if reverse else step
    direction = 1 if reverse else 0

    @pl.when(step == 0)
    def _():
        row = lax.broadcasted_iota(jnp.int32, (RC, RC), 0).astype(F32)
        col = lax.broadcasted_iota(jnp.int32, (RC, RC), 1).astype(F32)
        for h in range(RET_HEADS):
            lg = -jnp.exp(jnp.full((RC, RC), decay_ref[direction, h], F32))
            if reverse:
                diff = col - row
                mask = diff > 0
                q_pow = RC - row
                k_pow = row
            else:
                diff = row - col
                mask = diff >= 0
                q_pow = row + 1.0
                k_pow = RC - 1.0 - row
            dm_ref[h] = jnp.where(mask, jnp.exp(lg * jnp.where(mask, diff, 0.0)), 0.0)
            dq_ref[h] = jnp.exp(lg * q_pow)
            dk_ref[h] = jnp.exp(lg * k_pow)
            ds_ref[h] = jnp.exp(lg * RC)

    starts = _is_last_block(blk) if reverse else _is_first_block(blk)

    is_prompt = blk < N_PROMPT_BLK

    @pl.when(starts)
    def _():
        st_ref[...] = jnp.zeros_like(st_ref)

    @pl.when(jnp.logical_and(starts, jnp.logical_not(is_prompt)))
    def _():
        for h in range(RET_HEADS):
            off = (h % 2) * RET_DK
            st_ref[h, off:off + RET_DK, :] = s0_ref[h]

    lane = lax.broadcasted_iota(jnp.int32, (1, RET_PAIR), 1)
    chunks = range(RB // RC)
    chunk_order = list(reversed(chunks) if reverse else chunks)
    units = [(h, c) for h in range(RET_HEADS) for c in chunk_order]
    rows_of = lambda c: slice(c * RC, (c + 1) * RC)
    vcols = lambda h: slice(h * RET_DV, (h + 1) * RET_DV)
    roped = {}
    for p in range(RET_HEADS // 2):
        cols = slice(p * RET_PAIR, (p + 1) * RET_PAIR)
        for c in chunk_order:
            rows = rows_of(c)
            cos, sin = cos_ref[rows, :], sin_ref[rows, :]
            roped[p, c] = (_rope(q_ref[rows, cols], cos, sin),
                           _rope(k_ref[rows, cols] * (RET_DK ** -0.5), cos, sin))
    yield
    qm_u, vh_u, att_u, kd_u = {}, {}, {}, {}
    for h, c in units:
        head_mask = (lane // RET_DK == h % 2).astype(F32)
        q2, k2 = roped[h // 2, c]
        vh_u[h, c] = v_ref[rows_of(c), vcols(h)].astype(BF16)
        qm_u[h, c] = (q2 * head_mask).astype(BF16)
        km = k2 * head_mask
        att_u[h, c] = (_dot_nt(qm_u[h, c], km.astype(BF16)) * dm_ref[h]).astype(BF16)
        kd_u[h, c] = (km * dk_ref[h]).astype(BF16)
    yield
    intra_u = {u: _dot(att_u[u], vh_u[u]) for u in units}
    delta_u = {u: _dot_tn(kd_u[u], vh_u[u]) for u in units}
    yield
    state_u = {}
    for h in range(RET_HEADS):
        st = st_ref[h]
        for c in chunk_order:
            state_u[h, c] = st.astype(BF16)
            st = st * ds_ref[h] + delta_u[h, c]
        st_ref[h] = st
    yield
    for h, c in units:
        rows, out_cols = rows_of(c), vcols(h)
        o = intra_u[h, c] + _dot(qm_u[h, c], state_u[h, c]) * dq_ref[h]
        if reverse:
            o = o + of_ref[rows, out_cols]
            o = o * lax.rsqrt(jnp.mean(o * o, axis=-1, keepdims=True) + EPS)
            o = o * gn_ref[:, out_cols] * _silu(g_ref[rows, out_cols])
            o_ref[rows, out_cols] = o.astype(o_ref.dtype)
        else:
            o_ref[rows, out_cols] = o

    ends = _is_first_block(blk) if reverse else _is_last_block(blk)

    @pl.when(jnp.logical_and(ends, is_prompt))
    def _():
        for h in range(RET_HEADS):
            off = (h % 2) * RET_DK
            sfin_ref[h] = st_ref[h, off:off + RET_DK, :]


def _sample_seq(blk):
    return jnp.clip(_seq_of_block(blk) - BATCH, 0, DEC_BATCH - 1)


def _prompt_seq(blk):
    return jnp.minimum(_seq_of_block(blk), BATCH - 1)


def _retention(decay, q, k, v, cos_tab, sin_tab, state_in, state_out, layer, *, reverse,
               o_fwd=None, g=None, gn=None, conv=None):
    bmap = (lambda j: NBLK - 1 - j) if reverse else (lambda j: j)
    direction = 1 if reverse else 0
    qd, vd = RET_HEADS * RET_DK, RET_HEADS * RET_DV
    row_spec = lambda width: pl.BlockSpec((RB, width), lambda j: (bmap(j), 0))
    state_blk = (None, None, None, RET_HEADS, RET_DK, RET_DV)
    in_specs = [pl.BlockSpec(memory_space=pltpu.SMEM), row_spec(qd), row_spec(qd), row_spec(vd),
                pl.BlockSpec((RB, RET_PAIR), lambda j: (_rope_block(bmap(j)), 0)),
                pl.BlockSpec((RB, RET_PAIR), lambda j: (_rope_block(bmap(j)), 0)),
                pl.BlockSpec(state_blk, lambda j: (_sample_seq(bmap(j)), layer, direction, 0, 0, 0)),
                pl.BlockSpec(memory_space=pl.ANY)]
    args = [decay, q, k, v, cos_tab, sin_tab, state_in, state_out]
    if reverse:
        in_specs += [row_spec(vd), row_spec(vd), pl.BlockSpec((1, vd), lambda j: (0, 0))]
        args += [o_fwd, g, gn.reshape(1, vd)]
    tile = pltpu.VMEM((RET_HEADS, RC, RC), F32)
    out_specs = [row_spec(vd),
                 pl.BlockSpec(state_blk, lambda j: (_prompt_seq(bmap(j)), layer, direction, 0, 0, 0))]
    out_shape = [jax.ShapeDtypeStruct((NT, vd), BF16 if reverse else F32),
                 jax.ShapeDtypeStruct(state_out.shape, F32)]
    scratch = [tile, tile, tile, tile, tile]
    body = functools.partial(_ret_kernel, reverse=reverse)
    if conv is not None:
        assert not reverse
        n_ret = (len(in_specs), len(out_specs), len(scratch))
        c_in, c_args, c_out, c_shape, c_scratch = _conv_parts(*conv)
        in_specs, args = in_specs + c_in, args + c_args
        out_specs, out_shape, scratch = out_specs + c_out, out_shape + c_shape, scratch + c_scratch
        body = functools.partial(_ret_conv_kernel, n_ret=n_ret, n_conv=(len(c_in), len(c_out), len(c_scratch)))
    return pl.pallas_call(
        body,
        grid=(NBLK,),
        in_specs=in_specs,
        out_specs=out_specs,
        out_shape=out_shape,
        input_output_aliases={7: 1},
        scratch_shapes=scratch,
        compiler_params=_params(),
        name="retention_bwd" if reverse else "retention_fwd",
    )(*args)


def _ret_conv_kernel(*refs, n_ret, n_conv):
    (ri, ro, rs), (ci, co, cs) = n_ret, n_conv
    ins, outs, scr = refs[:ri + ci], refs[ri + ci:ri + ci + ro + co], refs[ri + ci + ro + co:]
    bodies = [_conv_stages(*ins[ri:], *outs[ro:], *scr[rs:]),
              _ret_stages(*ins[:ri], *outs[:ro], *scr[:rs], reverse=False)]
    while bodies:
        for body in list(bodies):
            if next(body, StopIteration) is StopIteration:
                bodies.remove(body)


CONV_RT = 32
CONV_CT = 128
CONV_SPAN = RB + 2 * HALO - 8


def _conv_stages(a_ref, ga_ref, ap_ref, gap_ref, an_ref, gan_ref, cw_ref, cb_ref, lng_ref, lnb_ref,
                 o_ref, u_ref, y_ref, us_ref):
    blk = pl.program_id(0)
    keep_prev = jnp.where(_is_first_block(blk), 0.0, 1.0)
    keep_next = jnp.where(_is_last_block(blk), 0.0, 1.0)
    u_ref[0:HALO, :] = ap_ref[...] * _sigmoid(gap_ref[...]) * keep_prev
    u_ref[HALO:HALO + RB, :] = a_ref[...] * _sigmoid(ga_ref[...])
    u_ref[HALO + RB:HALO + RB + HALO, :] = an_ref[...] * _sigmoid(gan_ref[...]) * keep_next
    for r in range(1, 8):
        us_ref[r - 1] = u_ref[r:r + CONV_SPAN, :]
    for ct in range(CONV_CH // CONV_CT):
        yield
        cols = slice(ct * CONV_CT, (ct + 1) * CONV_CT)
        for rt in range(RB // CONV_RT):
            acc = jnp.zeros((CONV_RT, CONV_CT), F32)
            for w in range(CONV_WIDTH):
                tiles, r = divmod(HALO - CONV_PAD + w, 8)
                base = rt * CONV_RT + 8 * tiles
                src = u_ref if r == 0 else us_ref.at[r - 1]
                acc = acc + src[base:base + CONV_RT, cols] * cw_ref[w:w + 1, cols]
            y_ref[rt * CONV_RT:(rt + 1) * CONV_RT, cols] = acc + cb_ref[:, cols]
    y = y_ref[...]
    mu = jnp.mean(y, axis=-1, keepdims=True)
    var = jnp.mean(jnp.square(y - mu), axis=-1, keepdims=True)
    o_ref[...] = _silu((y - mu) * lax.rsqrt(var + EPS) * lng_ref[...] + lnb_ref[...]).astype(o_ref.dtype)


def _conv_parts(a, ga, cw, cb, lng, lnb):
    per_blk = RB // HALO
    n_halo = NT // HALO
    row_spec = pl.BlockSpec((RB, CONV_CH), lambda i: (i, 0))
    prev_spec = pl.BlockSpec((HALO, CONV_CH), lambda i: (jnp.maximum(i * per_blk - 1, 0), 0))
    next_spec = pl.BlockSpec((HALO, CONV_CH), lambda i: (jnp.minimum((i + 1) * per_blk, n_halo - 1), 0))
    vec = pl.BlockSpec((1, CONV_CH), lambda i: (0, 0))
    in_specs = [row_spec, row_spec, prev_spec, prev_spec, next_spec, next_spec,
                pl.BlockSpec((CONV_WIDTH, CONV_CH), lambda i: (0, 0)), vec, vec, vec]
    args = [a, ga, a, ga, a, ga, cw, cb.reshape(1, -1), lng.reshape(1, -1), lnb.reshape(1, -1)]
    scratch = [pltpu.VMEM((RB + 2 * HALO, CONV_CH), F32), pltpu.VMEM((RB, CONV_CH), F32),
               pltpu.VMEM((7, CONV_SPAN, CONV_CH), F32)]
    return in_specs, args, [row_spec], [jax.ShapeDtypeStruct((NT, CONV_CH), BF16)], scratch


GC = GLA_CHUNK
GLA_NSUB = GC // GLA_SUB


def _split3(a):
    p1 = a.astype(BF16)
    r1 = a - p1.astype(F32)
    p2 = r1.astype(BF16)
    p3 = (r1 - p2.astype(F32)).astype(BF16)
    return p1, p2, p3


def _gla_kernel(alr_ref, q_ref, k_ref, v_ref, wa_ref, ba_ref, s0_ref, acc_ref, *rest, reverse):
    del acc_ref
    if reverse:
        of_ref, r_ref, gn_ref, o_ref, sfin_ref, st_ref, b_ref, tri_ref, ob_ref = rest
    else:
        o_ref, sfin_ref, st_ref, b_ref, tri_ref = rest
        ob_ref = o_ref
    step = pl.program_id(0)
    blk = NBLK - 1 - step if reverse else step
    starts = _is_last_block(blk) if reverse else _is_first_block(blk)
    is_prompt = blk < N_PROMPT_BLK

    @pl.when(jnp.logical_and(starts, is_prompt))
    def _():
        st_ref[...] = jnp.zeros_like(st_ref)

    @pl.when(jnp.logical_and(starts, jnp.logical_not(is_prompt)))
    def _():
        for h in range(GLA_HEADS):
            st_ref[h] = s0_ref[h].T

    @pl.when(step == 0)
    def _():
        row = lax.broadcasted_iota(jnp.int32, (RB, RB), 0)
        col = lax.broadcasted_iota(jnp.int32, (RB, RB), 1)
        ordered = col >= row if reverse else col <= row
        tri_ref[...] = jnp.where(jnp.logical_and(row // GC == col // GC, ordered), 1.0, 0.0).astype(BF16)

    heads = range(GLA_HEADS)
    hcols = [slice(h * GLA_DK, (h + 1) * GLA_DK) for h in heads]
    alr = alr_ref[...]
    z = [_dot_hi(alr, wa_ref[:, hcols[h]]) + ba_ref[:, hcols[h]] for h in heads]
    log_a = [(jnp.minimum(zh, 0.0) - jnp.log(1.0 + jnp.exp(-jnp.abs(zh)))) * (1.0 / GLA_TAU) for zh in z]
    parts = [_split3(la) for la in log_a]
    tri = tri_ref[...]
    for h in heads:
        g1, g2, g3 = parts[h]
        b_ref[:, hcols[h]] = _dot(tri, g1) + (_dot(tri, g2) + _dot(tri, g3))

    c_row = lax.broadcasted_iota(jnp.int32, (GC, 1), 0)
    a_row = lax.broadcasted_iota(jnp.int32, (GC, GC), 0)
    a_col = lax.broadcasted_iota(jnp.int32, (GC, GC), 1)
    att_mask = a_col > a_row if reverse else a_col <= a_row
    chunks = range(RB // GC)
    chunk_order = list(reversed(chunks) if reverse else chunks)
    units = [(h, c) for h in range(GLA_HEADS) for c in chunk_order]
    kcols = lambda h: slice(h * GLA_DK, (h + 1) * GLA_DK)
    vcols = lambda h: slice(h * GLA_DV, (h + 1) * GLA_DV)
    rows_of = lambda c: slice(c * GC, (c + 1) * GC)

    vh_u, qe_u, ke_u, decay_u, att_u = {}, {}, {}, {}, {}
    for h, c in units:
        rows = rows_of(c)
        b = b_ref[rows, kcols(h)]
        qh = q_ref[rows, kcols(h)] * (GLA_DK ** -0.5)
        kh = k_ref[rows, kcols(h)]
        vh_u[h, c] = v_ref[rows, vcols(h)].astype(BF16)
        edge = b[0:1, :] if reverse else b[GC - 1:GC, :]
        bounds = []
        for s in range(GLA_NSUB):
            if reverse:
                hi = (s + 1) * GLA_SUB
                bounds.append(b[hi:hi + 1, :] if s < GLA_NSUB - 1 else jnp.zeros((1, GLA_DK), F32))
            else:
                lo = s * GLA_SUB
                bounds.append(b[lo - 1:lo, :] if s > 0 else jnp.zeros((1, GLA_DK), F32))
        own = jnp.concatenate([jnp.broadcast_to(bd, (GLA_SUB, GLA_DK)) for bd in bounds], axis=0)
        q_own = qh * jnp.exp(b - own)
        q_parts, k_parts = [], []
        for s, bd in enumerate(bounds):
            q_parts.append(jnp.where(c_row // GLA_SUB == s, q_own, 0.0))
            reach = c_row >= s * GLA_SUB if reverse else c_row < (s + 1) * GLA_SUB
            k_parts.append(kh * jnp.exp(jnp.where(reach, bd - b, -jnp.inf)))
        q_bd = jnp.concatenate(q_parts, axis=1).astype(BF16)
        k_cat = jnp.concatenate(k_parts, axis=1).astype(BF16)
        att_u[h, c] = jnp.where(att_mask, _dot_nt(q_bd, k_cat), 0.0).astype(BF16)
        qe_u[h, c] = (qh * jnp.exp(b)).astype(BF16)
        ke_u[h, c] = (kh * jnp.exp(edge - b)).astype(BF16)
        decay_u[h, c] = jnp.exp(edge)
    intra_u = {u: _dot(att_u[u], vh_u[u]) for u in units}
    delta_u = {u: _dot_tn(vh_u[u], ke_u[u]) for u in units}
    state_u = {}
    for h in range(GLA_HEADS):
        st = st_ref[h]
        for c in chunk_order:
            state_u[h, c] = st.astype(BF16)
            st = st * decay_u[h, c] + delta_u[h, c]
        st_ref[h] = st
    for h, c in units:
        ob_ref[rows_of(c), vcols(h)] = intra_u[h, c] + _dot_nt(qe_u[h, c], state_u[h, c])

    if reverse:
        for h in range(GLA_HEADS):
            cols = vcols(h)
            o = ob_ref[:, cols] + of_ref[:, cols]
            o = o * lax.rsqrt(jnp.mean(o * o, axis=-1, keepdims=True) + EPS)
            o_ref[:, cols] = (o * gn_ref[:, cols] * _silu(r_ref[:, cols])).astype(o_ref.dtype)

    ends = _is_first_block(blk) if reverse else _is_last_block(blk)

    @pl.when(jnp.logical_and(ends, is_prompt))
    def _():
        for h in range(GLA_HEADS):
            sfin_ref[h] = st_ref[h].T


def _gla(alr, q, k, v, wa, ba, state_in, state_out, layer, *, reverse, o_fwd=None, r=None, gn=None):
    bmap = (lambda j: NBLK - 1 - j) if reverse else (lambda j: j)
    direction = 1 if reverse else 0
    qd, vd = GLA_HEADS * GLA_DK, GLA_HEADS * GLA_DV
    row_spec = lambda width: pl.BlockSpec((RB, width), lambda j: (bmap(j), 0))
    state_blk = (None, None, None, GLA_HEADS, GLA_DK, GLA_DV)
    in_specs = [row_spec(2 * GLA_RANK), row_spec(qd), row_spec(qd), row_spec(vd),
                pl.BlockSpec((2 * GLA_RANK, qd), lambda j: (0, 0)),
                pl.BlockSpec((1, qd), lambda j: (0, 0)),
                pl.BlockSpec(state_blk, lambda j: (_sample_seq(bmap(j)), layer, direction, 0, 0, 0)),
                pl.BlockSpec(memory_space=pl.ANY)]
    args = [alr, q, k, v, wa, ba, state_in, state_out]
    scratch = [pltpu.VMEM((GLA_HEADS, GLA_DV, GLA_DK), F32), pltpu.VMEM((RB, qd), F32),
               pltpu.VMEM((RB, RB), BF16)]
    if reverse:
        in_specs += [row_spec(vd), row_spec(vd), pl.BlockSpec((1, vd), lambda j: (0, 0))]
        args += [o_fwd, r, gn.reshape(1, vd)]
        scratch += [pltpu.VMEM((RB, vd), F32)]
    return pl.pallas_call(
        functools.partial(_gla_kernel, reverse=reverse),
        grid=(NBLK,),
        in_specs=in_specs,
        out_specs=[row_spec(vd),
                   pl.BlockSpec(state_blk, lambda j: (_prompt_seq(bmap(j)), layer, direction, 0, 0, 0))],
        out_shape=[jax.ShapeDtypeStruct((NT, vd), BF16 if reverse else F32),
                   jax.ShapeDtypeStruct(state_out.shape, F32)],
        input_output_aliases={7: 1},
        scratch_shapes=scratch,
        compiler_params=_params(),
        name="gla_bwd" if reverse else "gla_fwd",
    )(*args)


OUT_SUB = 128


def _outproj_kernel(*refs, n_mix):
    mix_refs = refs[:n_mix]
    (w_ref, x_ref, gate_ref, g2_ref, shift_ref, scale_ref, rw_ref, rb_ref,
     x1_ref, h2_ref, logit_ref, w_bf) = refs[n_mix:]
    step = pl.program_id(0)

    @pl.when(step == 0)
    def _():
        w_bf[...] = w_ref[...].astype(BF16)

    row = _mod_row(step)
    r_hi, r_lo = _split2(rw_ref[...])
    groups = [slice(p * OUT_SUB, (p + 1) * OUT_SUB) for p in range(RB // OUT_SUB)]
    mixed = []
    for rows in groups:
        m, off = None, 0
        for mix_ref in mix_refs:
            width = mix_ref.shape[1]
            part = _dot(mix_ref[rows, :], w_bf[off:off + width, :])
            m = part if m is None else m + part
            off += width
        mixed.append(m)
    normed = []
    for rows, m in zip(groups, mixed):
        x1 = x_ref[rows, :] + gate_ref[pl.ds(row, 1), :] * m
        x1_ref[rows, :] = x1
        h2 = _norm_mod(x1, g2_ref, shift_ref, scale_ref, row)
        h2_ref[rows, :] = _pack_rows(h2)
        normed.append(_split2(h2))
    for rows, (h_hi, h_lo) in zip(groups, normed):
        logit_ref[:, rows] = _dot_nt(r_hi, h_hi) + (_dot_nt(r_hi, h_lo) + _dot_nt(r_lo, h_hi)) + rb_ref[...]


def _outproj(mixes, w_all, index, x, gate, g2, shift, scale, rw, rb):
    n_mix = len(mixes)
    row_spec = lambda width: pl.BlockSpec((RB, width), lambda i: (i, 0))
    full = lambda shape: pl.BlockSpec(shape, lambda i: (0,) * len(shape))
    mod_spec = full((MOD_ROWS, D_MODEL))
    n_mixed = w_all.shape[1]
    return pl.pallas_call(
        functools.partial(_outproj_kernel, n_mix=n_mix),
        grid=(NBLK,),
        in_specs=[row_spec(m.shape[1]) for m in mixes]
        + [pl.BlockSpec((None, n_mixed, D_MODEL), lambda i: (index, 0, 0)),
           row_spec(D_MODEL), mod_spec, full((1, D_MODEL)), mod_spec, mod_spec,
           full((N_EXPERTS, D_MODEL)), full((N_EXPERTS, 1))],
        out_specs=[row_spec(D_MODEL), row_spec(D_MODEL // 2), pl.BlockSpec((N_EXPERTS, RB), lambda i: (0, i))],
        out_shape=[jax.ShapeDtypeStruct((NT, D_MODEL), F32), jax.ShapeDtypeStruct((NT, D_MODEL // 2), jnp.uint32),
                   jax.ShapeDtypeStruct((N_EXPERTS, NT), F32)],
        scratch_shapes=[pltpu.VMEM((n_mixed, D_MODEL), BF16)],
        compiler_params=_params(),
        name="outproj",
    )(*mixes, w_all, x, gate, g2.reshape(1, D_MODEL), shift, scale, rw.T, rb.reshape(N_EXPERTS, 1))


ROUTE_BLK = 2048
ROUTE_SUB = 256


def _route_kernel(lg_ref, idx_ref, rank_ref, gt_ref, cnt_ref, carry_ref):
    @pl.when(pl.program_id(0) == 0)
    def _():
        carry_ref[...] = jnp.zeros_like(carry_ref)

    logits = lg_ref[...]
    eid = lax.broadcasted_iota(jnp.int32, logits.shape, 0).astype(F32)
    work = logits
    onehots, top_vals = [], []
    for kk in range(TOP_K):
        top = jnp.max(work, axis=0, keepdims=True)
        first = jnp.min(jnp.where(work == top, eid, float(N_EXPERTS)), axis=0, keepdims=True)
        onehot = eid == first
        idx_ref[kk:kk + 1, :] = first.astype(jnp.int32)
        onehots.append(onehot)
        top_vals.append(top)
        work = jnp.where(onehot, -jnp.inf, work)
    exps = [jnp.exp(v - top_vals[0]) for v in top_vals]
    denom = exps[0]
    for e in exps[1:]:
        denom = denom + e
    gt_ref[...] = jnp.zeros_like(gt_ref)
    for kk in range(TOP_K):
        gt_ref[kk:kk + 1, :] = exps[kk] / denom

    sel = jnp.zeros(logits.shape, F32)
    for onehot in onehots:
        sel = sel + jnp.where(onehot, 1.0, 0.0)
    sel = sel.astype(BF16)
    r_i = lax.broadcasted_iota(jnp.int32, (ROUTE_SUB, ROUTE_SUB), 0)
    c_i = lax.broadcasted_iota(jnp.int32, (ROUTE_SUB, ROUTE_SUB), 1)
    before = jnp.where(r_i < c_i, 1.0, 0.0).astype(BF16)
    ones = jnp.ones((ROUTE_SUB, ROUTE_SUB), BF16)
    carry = carry_ref[...]
    for s in range(ROUTE_BLK // ROUTE_SUB):
        cols = slice(s * ROUTE_SUB, (s + 1) * ROUTE_SUB)
        pos = _dot(sel[:, cols], before) + carry
        for kk in range(TOP_K):
            rank = jnp.sum(jnp.where(onehots[kk][:, cols], pos, 0.0), axis=0, keepdims=True)
            rank_ref[kk:kk + 1, cols] = rank.astype(jnp.int32)
        carry = carry + _dot(sel[:, cols], ones)
    carry_ref[...] = carry
    cnt_ref[...] = carry.astype(jnp.int32)


def _route(logits_t):
    col_spec = lambda rows: pl.BlockSpec((rows, ROUTE_BLK), lambda i: (0, i))
    return pl.pallas_call(
        _route_kernel,
        grid=(NT // ROUTE_BLK,),
        in_specs=[col_spec(N_EXPERTS)],
        out_specs=[col_spec(TOP_K), col_spec(TOP_K), col_spec(8),
                   pl.BlockSpec((N_EXPERTS, ROUTE_SUB), lambda i: (0, 0))],
        out_shape=[jax.ShapeDtypeStruct((TOP_K, NT), jnp.int32), jax.ShapeDtypeStruct((TOP_K, NT), jnp.int32),
                   jax.ShapeDtypeStruct((8, NT), F32), jax.ShapeDtypeStruct((N_EXPERTS, ROUTE_SUB), jnp.int32)],
        scratch_shapes=[pltpu.VMEM((N_EXPERTS, ROUTE_SUB), F32)],
        compiler_params=_params(),
        name="route",
    )(logits_t)


TM = 1024
TM_SUB = 256
MOE_NBLK = NT * TOP_K // TM + N_EXPERTS
MOE_ROWS = MOE_NBLK * TM
HALF = D_MODEL // 2


def _moe_kernel(be_ref, nv_ref, nx_ref, x_ref, wgu_hbm, bgu_ref, wd_hbm, bd_ref, y_ref,
                wgu_st, wd_st, wgu_bf, wd_bf, sems, *, layer):
    i = pl.program_id(0)
    n_valid = nv_ref[i]

    def fetch(e):
        return (pltpu.make_async_copy(wgu_hbm.at[layer, e], wgu_st, sems.at[0]),
                pltpu.make_async_copy(wd_hbm.at[layer, e], wd_st, sems.at[1]))

    @pl.when(i == 0)
    def _():
        for cp in fetch(be_ref[0]):
            cp.start()

    @pl.when(n_valid > 0)
    def _():
        e = be_ref[i]
        changed = jnp.logical_or(i == 0, e != be_ref[jnp.maximum(i - 1, 0)])

        @pl.when(changed)
        def _():
            for cp in fetch(e):
                cp.wait()
            wgu_bf[...] = wgu_st[...].astype(BF16)
            wd_bf[...] = wd_st[...].astype(BF16)
            nxt = nx_ref[e]

            @pl.when(nxt >= 0)
            def _():
                for cp in fetch(nxt):
                    cp.start()

    def ffn_pass(p, masked):
        rows = slice(p * TM_SUB, (p + 1) * TM_SUB)
        x = x_ref[rows, :]
        if masked:
            row_id = lax.broadcasted_iota(jnp.int32, (TM_SUB, 1), 0) + p * TM_SUB
            x = jnp.where(row_id < n_valid, x, jnp.uint32(0))
        x_lo, x_hi = _unpack_rows(x)
        x_lo, x_hi = x_lo.astype(BF16), x_hi.astype(BF16)
        yield
        gu = _dot(x_lo, wgu_bf[:HALF, :]) + _dot(x_hi, wgu_bf[HALF:, :]) + bgu_ref[...]
        yield
        gate = jnp.minimum(gu[:, :D_FF], SWIGLU_LIMIT)
        up = jnp.clip(gu[:, D_FF:], -SWIGLU_LIMIT, SWIGLU_LIMIT)
        hdn = (gate * _sigmoid(SWIGLU_ALPHA * gate) * (up + 1.0)).astype(BF16)
        yield
        y = _dot(hdn, wd_bf[...]) + bd_ref[...]
        yield
        y_ref[rows, :] = _pack_rows(y)

    @pl.when(n_valid == TM)
    def _():
        passes = [ffn_pass(p, masked=False) for p in range(TM // TM_SUB)]
        n_stages = 5
        for t in range(n_stages + len(passes) - 1):
            for lag, body in enumerate(passes):
                if 0 <= t - lag < n_stages:
                    next(body, None)

    for p in range(TM // TM_SUB):
        @pl.when(jnp.logical_and(n_valid < TM, n_valid > p * TM_SUB))
        def _():
            for _ in ffn_pass(p, masked=True):
                pass


def _moe_experts(layer, block_e, n_valid, next_e, xs, w_gu, b_gu, w_down, b_down):
    grid_spec = pltpu.PrefetchScalarGridSpec(
        num_scalar_prefetch=3,
        grid=(MOE_NBLK,),
        in_specs=[pl.BlockSpec((TM, HALF), lambda i, be, nv, nx: (i, 0)),
                  pl.BlockSpec(memory_space=pl.ANY),
                  pl.BlockSpec((None, None, 1, 2 * D_FF), lambda i, be, nv, nx: (layer, be[i], 0, 0)),
                  pl.BlockSpec(memory_space=pl.ANY),
                  pl.BlockSpec((None, None, 1, D_MODEL), lambda i, be, nv, nx: (layer, be[i], 0, 0))],
        out_specs=pl.BlockSpec((TM, HALF), lambda i, be, nv, nx: (i, 0)),
        scratch_shapes=[pltpu.VMEM((D_MODEL, 2 * D_FF), F32), pltpu.VMEM((D_FF, D_MODEL), F32),
                        pltpu.VMEM((D_MODEL, 2 * D_FF), BF16), pltpu.VMEM((D_FF, D_MODEL), BF16),
                        pltpu.SemaphoreType.DMA((2,))],
    )
    return pl.pallas_call(
        functools.partial(_moe_kernel, layer=layer),
        grid_spec=grid_spec,
        out_shape=jax.ShapeDtypeStruct((MOE_ROWS, HALF), jnp.uint32),
        compiler_params=_params(),
        name="moe_experts",
    )(block_e, n_valid, next_e, xs, w_gu, b_gu.reshape(DEPTH, N_EXPERTS, 1, -1), w_down,
      b_down.reshape(DEPTH, N_EXPERTS, 1, -1))


SC_WORKERS = 32
SC_WIN = 64


def _sc_mesh():
    return plsc.VectorSubcoreMesh(core_axis_name="core", subcore_axis_name="subcore")


def _sc_worker():
    return lax.axis_index("core") * (SC_WORKERS // 2) + lax.axis_index("subcore")


def _sc_scatter_rows(x, dest_t, n_rows):
    n, width = x.shape
    kk = dest_t.shape[0]
    per = n // SC_WORKERS
    n_win = per // SC_WIN
    assert per * SC_WORKERS == n and n_win * SC_WIN == per and n_win % 2 == 0

    @pl.kernel(out_type=jax.ShapeDtypeStruct((n_rows, width), x.dtype), mesh=_sc_mesh(),
               scratch_types=[pltpu.VMEM((kk, per), jnp.int32), pltpu.VMEM((SC_WIN, width), x.dtype),
                              pltpu.VMEM((SC_WIN, width), x.dtype), pltpu.SemaphoreType.DMA((4,))])
    def scatter(x_hbm, i_hbm, o_hbm, idx_v, buf0, buf1, sems):
        base = _sc_worker() * per
        pltpu.sync_copy(i_hbm.at[:, pl.ds(base, per)], idx_v)

        def get(j, buf, s):
            return pltpu.make_async_copy(x_hbm.at[pl.ds(base + j * SC_WIN, SC_WIN)], buf, sems.at[s])

        def put(j, q, buf, s):
            return pltpu.make_async_copy(buf, o_hbm.at[idx_v.at[q, pl.ds(j * SC_WIN, SC_WIN)]], sems.at[s])

        get(0, buf0, 0).start()

        @pl.loop(0, n_win, step=2)
        def _(j):
            get(j, buf0, 0).wait()

            @pl.when(j > 0)
            def _():
                for q in range(kk):
                    put(j - 1, q, buf1, 3).wait()

            get(j + 1, buf1, 1).start()
            for q in range(kk):
                put(j, q, buf0, 2).start()
            get(j + 1, buf1, 1).wait()
            for q in range(kk):
                put(j, q, buf0, 2).wait()

            @pl.when(j + 2 < n_win)
            def _():
                get(j + 2, buf0, 0).start()

            for q in range(kk):
                put(j + 1, q, buf1, 3).start()

        for q in range(kk):
            put(n_win - 1, q, buf1, 3).wait()

    return scatter(x, dest_t)


def _sc_gather_rows(y, idx):
    n = idx.shape[0]
    width = y.shape[1]
    per = n // SC_WORKERS
    n_win = per // SC_WIN
    assert per * SC_WORKERS == n and n_win * SC_WIN == per and n_win % 2 == 0

    @pl.kernel(out_type=jax.ShapeDtypeStruct((n, width), y.dtype), mesh=_sc_mesh(),
               scratch_types=[pltpu.VMEM((per,), jnp.int32), pltpu.VMEM((SC_WIN, width), y.dtype),
                              pltpu.VMEM((SC_WIN, width), y.dtype), pltpu.SemaphoreType.DMA((4,))])
    def gather(y_hbm, i_hbm, o_hbm, idx_v, buf0, buf1, sems):
        base = _sc_worker() * per
        pltpu.sync_copy(i_hbm.at[pl.ds(base, per)], idx_v)

        def get(j, buf, s):
            return pltpu.make_async_copy(y_hbm.at[idx_v.at[pl.ds(j * SC_WIN, SC_WIN)]], buf, sems.at[s])

        def put(j, buf, s):
            return pltpu.make_async_copy(buf, o_hbm.at[pl.ds(base + j * SC_WIN, SC_WIN)], sems.at[s])

        get(0, buf0, 0).start()

        @pl.loop(0, n_win, step=2)
        def _(j):
            get(j, buf0, 0).wait()

            @pl.when(j > 0)
            def _():
                put(j - 1, buf1, 3).wait()

            get(j + 1, buf1, 1).start()
            put(j, buf0, 2).start()
            get(j + 1, buf1, 1).wait()
            put(j, buf0, 2).wait()

            @pl.when(j + 2 < n_win)
            def _():
                get(j + 2, buf0, 0).start()

            put(j + 1, buf1, 3).start()

        put(n_win - 1, buf1, 3).wait()

    return gather(y, idx)


def _gate_columns(gt_ref):
    r_i = lax.broadcasted_iota(jnp.int32, (RB, RB), 0)
    c_i = lax.broadcasted_iota(jnp.int32, (RB, RB), 1)
    eye = jnp.where(r_i == c_i, 1.0, 0.0).astype(BF16)
    g1, g2, g3 = _split3(gt_ref[...])
    return _dot_nt(eye, g1) + (_dot_nt(eye, g2) + _dot_nt(eye, g3))


def _combined_rows(x1_ref, yg_refs, gw, gate_ref, row, rows=slice(None)):
    acc_lo, acc_hi = None, None
    for kk in range(TOP_K):
        y_lo, y_hi = _unpack_rows(yg_refs[kk][rows, :])
        w = gw[rows, kk:kk + 1]
        acc_lo = y_lo * w if acc_lo is None else acc_lo + y_lo * w
        acc_hi = y_hi * w if acc_hi is None else acc_hi + y_hi * w
    x_lo = x1_ref[rows, :HALF] + gate_ref[pl.ds(row, 1), :HALF] * acc_lo
    x_hi = x1_ref[rows, HALF:] + gate_ref[pl.ds(row, 1), HALF:] * acc_hi
    return x_lo, x_hi


def _combine_specs():
    slot_spec = lambda k: pl.BlockSpec((RB, HALF), lambda i: (k * NBLK + i, 0))
    return ([pl.BlockSpec((RB, D_MODEL), lambda i: (i, 0))] + [slot_spec(k) for k in range(TOP_K)]
            + [pl.BlockSpec((8, RB), lambda i: (0, i)), pl.BlockSpec((MOD_ROWS, D_MODEL), lambda i: (0, 0))])


def _combine_final_kernel(x1_ref, *rest):
    yg_refs = rest[:TOP_K]
    gt_ref, gate_ref, fg_ref, op_ref, os_ref = rest[TOP_K:]
    i = pl.program_id(0)
    x_lo, x_hi = _combined_rows(x1_ref, yg_refs, _gate_columns(gt_ref), gate_ref, _mod_row(i))
    ms = (jnp.sum(x_lo * x_lo, axis=-1, keepdims=True) + jnp.sum(x_hi * x_hi, axis=-1, keepdims=True)) / D_MODEL
    scale = lax.rsqrt(ms + EPS)

    @pl.when(i < N_PROMPT_BLK)
    def _():
        op_ref[:, :HALF] = x_lo * scale * fg_ref[:, :HALF]
        op_ref[:, HALF:] = x_hi * scale * fg_ref[:, HALF:]

    @pl.when(i >= N_PROMPT_BLK)
    def _():
        os_ref[:, :HALF] = x_lo * scale * fg_ref[:, :HALF]
        os_ref[:, HALF:] = x_hi * scale * fg_ref[:, HALF:]


def _combine_final(x1, yg, gates_t, gate, final_g):
    return pl.pallas_call(
        _combine_final_kernel,
        grid=(NBLK,),
        in_specs=_combine_specs() + [pl.BlockSpec((1, D_MODEL), lambda i: (0, 0))],
        out_specs=[pl.BlockSpec((RB, D_MODEL), lambda i: (jnp.minimum(i, N_PROMPT_BLK - 1), 0)),
                   pl.BlockSpec((RB, D_MODEL), lambda i: (jnp.maximum(i - N_PROMPT_BLK, 0), 0))],
        out_shape=[jax.ShapeDtypeStruct((NT_PROMPT, D_MODEL), F32),
                   jax.ShapeDtypeStruct((NT - NT_PROMPT, D_MODEL), F32)],
        compiler_params=_params(),
        name="moe_combine_final",
    )(x1, *([yg] * TOP_K), gates_t, gate, final_g.reshape(1, D_MODEL))


def _routing_plan(counts, idx_t, rank_t):
    counts = counts[:, 0]
    padded = (counts + TM - 1) // TM * TM
    pad_end = jnp.cumsum(padded)
    pad_start = pad_end - padded
    blk_row = (jnp.arange(MOE_NBLK, dtype=jnp.int32) * TM)[:, None]
    ids = jnp.arange(N_EXPERTS, dtype=jnp.int32)
    owns = jnp.logical_and(pad_start[None, :] <= blk_row, blk_row < pad_end[None, :])
    last_used = jnp.max(jnp.where(counts > 0, ids, 0))
    block_e = jnp.where(jnp.any(owns, axis=1), jnp.sum(jnp.where(owns, ids[None, :], 0), axis=1), last_used)
    block_e = block_e.astype(jnp.int32)
    left = jnp.clip(counts[None, :] - (blk_row - pad_start[None, :]), 0, TM)
    n_valid = jnp.sum(jnp.where(owns, left, 0), axis=1).astype(jnp.int32)
    start = jnp.zeros(idx_t.shape, jnp.int32)
    for e in range(N_EXPERTS):
        start = jnp.where(idx_t == e, pad_start[e], start)
    dest_t = (start + rank_t).astype(jnp.int32)
    later = jnp.where(jnp.logical_and(counts[None, :] > 0, ids[None, :] > ids[:, None]), ids[None, :], N_EXPERTS)
    next_e = jnp.min(later, axis=1)
    next_e = jnp.where(next_e == N_EXPERTS, -1, next_e).astype(jnp.int32)
    return block_e, n_valid, next_e, dest_t


def _rope_tables():
    rows = DEC_SEQ // GRID_W
    row = jnp.repeat(jnp.arange(rows, dtype=F32), GRID_W)
    col = jnp.tile(jnp.arange(GRID_W, dtype=F32), rows)
    n_f = RET_DK // 4
    freqs = ROPE_THETA ** (-jnp.arange(n_f, dtype=F32) / n_f)
    ang = jnp.concatenate([row[:, None] * freqs, col[:, None] * freqs], axis=-1)
    cos = jnp.repeat(jnp.cos(ang), 2, axis=-1)
    sin = jnp.repeat(jnp.sin(ang), 2, axis=-1) * jnp.tile(jnp.asarray([-1.0, 1.0], F32), RET_DK // 2)
    cos = jnp.concatenate([jnp.ones((RB, RET_DK), F32), cos], axis=0)
    sin = jnp.concatenate([jnp.zeros((RB, RET_DK), F32), sin], axis=0)
    return jnp.tile(cos, (1, 2)), jnp.tile(sin, (1, 2))


def kernel(x_prompt, x_sample, state_ret, state_gla, c, c_ctx, w_mod, b_mod, norm1_g, norm2_g, final_g, even_w_in, ret_decay, ret_gn, conv_w, conv_b, conv_ln_g, conv_ln_b, even_w_out, odd_w_in, gla_w_a2, gla_b_a2, gla_gn, odd_w_out, router_w, router_b, exp_w_gu, exp_b_gu, exp_w_down, exp_b_down):
    x_src = ("split", x_prompt.reshape(NT_PROMPT, D_MODEL), x_sample.reshape(NT - NT_PROMPT, D_MODEL))
    cvec = jnp.concatenate([c_ctx[None, :], c, jnp.zeros((MOD_ROWS - 1 - DEC_BATCH, D_MODEL), F32)], axis=0)
    mods = _modulation(cvec, w_mod, b_mod).reshape(DEPTH, MOD_ROWS, N_MOD, D_MODEL)
    cos_tab, sin_tab = _rope_tables()
    new_ret = jnp.zeros((BATCH,) + state_ret.shape[1:], F32)
    new_gla = jnp.zeros((BATCH,) + state_gla.shape[1:], F32)
    for l in range(DEPTH):
        mod = [mods[l, :, j, :] for j in range(N_MOD)]
        if l % 2 == 0:
            e = l // 2
            qd, vd = RET_HEADS * RET_DK, RET_HEADS * RET_DV
            x, q, k, v, g, a, ga = _inproj(x_src, norm1_g[l], mod[0], mod[1], even_w_in, e,
                                           (qd, qd, vd, vd, CONV_CH, CONV_CH))
            conv = (a, ga, conv_w[e], conv_b[e], conv_ln_g[e], conv_ln_b[e])
            o_f, new_ret, u = _retention(ret_decay[e], q, k, v, cos_tab, sin_tab, state_ret, new_ret, e,
                                         reverse=False, conv=conv)
            ret, new_ret = _retention(ret_decay[e], q, k, v, cos_tab, sin_tab, state_ret, new_ret, e,
                                      reverse=True, o_fwd=o_f, g=g, gn=ret_gn[e])
            mixes, w_out, w_index = [ret, u], even_w_out, e
        else:
            o = l // 2
            qd, vd = GLA_HEADS * GLA_DK, GLA_HEADS * GLA_DV
            x, q, k, v, r, alr = _inproj(x_src, norm1_g[l], mod[0], mod[1], odd_w_in, o,
                                         (qd, qd, vd, vd, 2 * GLA_RANK))
            zeros = jnp.zeros((GLA_RANK, qd), F32)
            wa_f = jnp.concatenate([gla_w_a2[o, 0], zeros], axis=0)
            wa_b = jnp.concatenate([zeros, gla_w_a2[o, 1]], axis=0)
            o_f, new_gla = _gla(alr, q, k, v, wa_f, gla_b_a2[o, 0].reshape(1, qd), state_gla, new_gla, o,
                                reverse=False)
            y, new_gla = _gla(alr, q, k, v, wa_b, gla_b_a2[o, 1].reshape(1, qd), state_gla, new_gla, o,
                              reverse=True, o_fwd=o_f, r=r, gn=gla_gn[o])
            mixes, w_out, w_index = [y], odd_w_out, o
        x1, h2, logits_t = _outproj(mixes, w_out, w_index, x, mod[2], norm2_g[l], mod[3], mod[4],
                                    router_w[l], router_b[l])
        idx_t, rank_t, gates_t, counts = _route(logits_t)
        block_e, n_valid, next_e, dest_t = _routing_plan(counts, idx_t, rank_t)
        xs = _sc_scatter_rows(h2, dest_t, MOE_ROWS)
        yb = _moe_experts(l, block_e, n_valid, next_e, xs, exp_w_gu, exp_b_gu, exp_w_down, exp_b_down)
        yg = _sc_gather_rows(yb, dest_t.reshape(TOP_K * NT))
        x_src = ("moe", x1, yg, gates_t, mod[5])
    y_prompt, y_sample = _combine_final(*x_src[1:], final_g)
    y_prompt = y_prompt.reshape(BATCH, SEQ, D_MODEL)
    y_sample = y_sample.reshape(DEC_BATCH, DEC_SEQ, D_MODEL)
    return (y_prompt, y_sample, new_ret, new_gla)
```

```python
import functools

import jax
import jax.numpy as jnp
from jax import lax
from jax.experimental import pallas as pl
from jax.experimental.pallas import tpu as pltpu
from jax.experimental.pallas import tpu_sc as plsc

F32 = jnp.float32
BF16 = jnp.bfloat16

D_MODEL = 1024
BATCH = 16
SEQ = 256
DEPTH = 4
DEC_BATCH = 4
DEC_SEQ = 4096
GRID_W = 64
RET_HEADS = 4
RET_DK = 64
RET_DV = 128
RET_CHUNK = 128
CONV_CH = 512
CONV_WIDTH = 31
CONV_PAD = CONV_WIDTH // 2
GLA_HEADS = 4
GLA_DK = 128
GLA_DV = 256
GLA_RANK = 16
GLA_TAU = 16.0
GLA_CHUNK = 64
GLA_SUB = 16
N_EXPERTS = 32
TOP_K = 4
D_FF = 1024
SWIGLU_LIMIT = 7.0
SWIGLU_ALPHA = 1.702
ROPE_THETA = 10000.0
EPS = 1e-6
N_MOD = 6

RB = 256
NT_PROMPT = BATCH * SEQ
NT = NT_PROMPT + DEC_BATCH * DEC_SEQ
NBLK = NT // RB
N_PROMPT_BLK = NT_PROMPT // RB
SAMPLE_BLK = DEC_SEQ // RB
MOD_ROWS = 8
HALO = 16
VMEM_LIMIT = 48 * 1024 * 1024

assert SEQ == RB and DEC_SEQ % RB == 0 and CONV_PAD < HALO


def _seq_of_block(i):
    return jnp.where(i < N_PROMPT_BLK, i, N_PROMPT_BLK + (i - N_PROMPT_BLK) // SAMPLE_BLK)


def _is_first_block(i):
    return jnp.logical_or(i < N_PROMPT_BLK, (i - N_PROMPT_BLK) % SAMPLE_BLK == 0)


def _is_last_block(i):
    return jnp.logical_or(i < N_PROMPT_BLK, (i - N_PROMPT_BLK) % SAMPLE_BLK == SAMPLE_BLK - 1)


def _mod_row(i):
    return jnp.where(i < N_PROMPT_BLK, 0, 1 + (i - N_PROMPT_BLK) // SAMPLE_BLK)


def _rope_block(i):
    return jnp.where(i < N_PROMPT_BLK, 0, 1 + (i - N_PROMPT_BLK) % SAMPLE_BLK)


def _dot(a, b):
    return jnp.dot(a, b, preferred_element_type=F32)


def _dot_nt(a, b):
    return lax.dot_general(a, b, (((1,), (1,)), ((), ())), preferred_element_type=F32)


def _dot_tn(a, b):
    return lax.dot_general(a, b, (((0,), (0,)), ((), ())), preferred_element_type=F32)


def _split2(a):
    hi = a.astype(BF16)
    lo = (a - hi.astype(F32)).astype(BF16)
    return hi, lo


def _dot_hi(a, b):
    a_hi, a_lo = _split2(a)
    b_hi, b_lo = _split2(b)
    return _dot(a_hi, b_hi) + (_dot(a_hi, b_lo) + _dot(a_lo, b_hi))


def _silu(x):
    return x * (1.0 / (1.0 + jnp.exp(-x)))


def _sigmoid(x):
    return 1.0 / (1.0 + jnp.exp(-x))


def _pack_rows(x):
    n = x.shape[1] // 2
    lo = pltpu.bitcast(x[:, :n].astype(BF16).astype(F32), jnp.uint32)
    hi = pltpu.bitcast(x[:, n:].astype(BF16).astype(F32), jnp.uint32)
    return hi | (lo >> 16)


def _unpack_rows(u):
    lo = pltpu.bitcast(u << 16, F32)
    hi = pltpu.bitcast(u & jnp.uint32(0xFFFF0000), F32)
    return lo, hi


def _params(n_axes=1, vmem=VMEM_LIMIT):
    return pltpu.CompilerParams(dimension_semantics=("arbitrary",) * n_axes, vmem_limit_bytes=vmem)


MOD_TN = 1536


def _mod_kernel(c_ref, w_ref, b_ref, o_ref):
    s = _silu(c_ref[...]).astype(BF16)
    o_ref[...] = _dot(s, w_ref[...].astype(BF16)) + b_ref[...]


def _modulation(cvec, w_mod, b_mod):
    n = N_MOD * D_MODEL
    return pl.pallas_call(
        _mod_kernel,
        grid=(DEPTH, n // MOD_TN),
        in_specs=[pl.BlockSpec((MOD_ROWS, D_MODEL), lambda l, j: (0, 0)),
                  pl.BlockSpec((None, D_MODEL, MOD_TN), lambda l, j: (l, 0, j)),
                  pl.BlockSpec((None, 1, MOD_TN), lambda l, j: (l, 0, j))],
        out_specs=pl.BlockSpec((None, MOD_ROWS, MOD_TN), lambda l, j: (l, 0, j)),
        out_shape=jax.ShapeDtypeStruct((DEPTH, MOD_ROWS, n), F32),
        compiler_params=_params(2),
        name="modulation",
    )(cvec, w_mod, b_mod.reshape(DEPTH, 1, n))


def _norm_mod(x, g_ref, shift_ref, scale_ref, row):
    y = x * lax.rsqrt(jnp.mean(x * x, axis=-1, keepdims=True) + EPS) * g_ref[...]
    return y * (1.0 + scale_ref[pl.ds(row, 1), :]) + shift_ref[pl.ds(row, 1), :]


N_SRC = {"split": 2, "moe": 3 + TOP_K}
IN_SUB = 64


def _inproj_kernel(*refs, widths, source):
    n_src = N_SRC[source]
    src = refs[:n_src]
    g_ref, shift_ref, scale_ref, w_ref = refs[n_src:n_src + 4]
    outs, w_bf, hb_ref = refs[n_src + 4:-2], refs[-2], refs[-1]
    step = pl.program_id(0)

    @pl.when(step == 0)
    def _():
        w_bf[...] = w_ref[...].astype(BF16)
        hb_ref[...] = jnp.zeros_like(hb_ref)

    hb_prev = hb_ref[(step + 1) % 2]

    def project():
        off = 0
        for o_ref, width in zip(outs[1:], widths):
            o_ref[...] = _dot(hb_prev, w_bf[:, off:off + width])
            off += width
            yield

    def assemble():
        blk = jnp.minimum(step, NBLK - 1)
        row = _mod_row(blk)
        if source == "moe":
            x1_ref, yg_refs, gt_ref, gate_ref = src[0], src[1:1 + TOP_K], src[-2], src[-1]
            gw = _gate_columns(gt_ref)
            yield
        for p in range(RB // IN_SUB):
            rows = slice(p * IN_SUB, (p + 1) * IN_SUB)
            if source == "split":
                x = jnp.where(blk < N_PROMPT_BLK, src[0][rows, :], src[1][rows, :])
            else:
                x = jnp.concatenate(_combined_rows(x1_ref, yg_refs, gw, gate_ref, row, rows), axis=1)
            outs[0][rows, :] = x
            hb_ref[step % 2, rows, :] = _norm_mod(x, g_ref, shift_ref, scale_ref, row).astype(BF16)
            yield

    bodies = [project(), assemble()]
    while bodies:
        for body in list(bodies):
            if next(body, StopIteration) is StopIteration:
                bodies.remove(body)


def _inproj(x, g, shift, scale, w_all, index, widths):
    source, x_args = x[0], list(x[1:])
    n_in = w_all.shape[2]
    full = lambda shape: pl.BlockSpec(shape, lambda i: (0,) * len(shape))
    assembled = lambda i: jnp.minimum(i, NBLK - 1)
    projected = lambda i: jnp.maximum(i - 1, 0)
    if source == "split":
        x_specs = [pl.BlockSpec((RB, D_MODEL), lambda i: (jnp.minimum(i, N_PROMPT_BLK - 1), 0)),
                   pl.BlockSpec((RB, D_MODEL), lambda i: (jnp.maximum(assembled(i) - N_PROMPT_BLK, 0), 0))]
    else:
        x1, yg, gates_t, gate = x_args
        x_specs, x_args = _combine_specs(assembled), [x1] + [yg] * TOP_K + [gates_t, gate]
    return pl.pallas_call(
        functools.partial(_inproj_kernel, widths=widths, source=source),
        grid=(NBLK + 1,),
        in_specs=x_specs + [full((1, D_MODEL)), full((MOD_ROWS, D_MODEL)), full((MOD_ROWS, D_MODEL)),
                            pl.BlockSpec((None, D_MODEL, n_in), lambda i: (index, 0, 0))],
        out_specs=[pl.BlockSpec((RB, D_MODEL), lambda i: (assembled(i), 0))]
        + [pl.BlockSpec((RB, width), lambda i: (projected(i), 0)) for width in widths],
        out_shape=[jax.ShapeDtypeStruct((NT, width), F32) for width in (D_MODEL,) + tuple(widths)],
        scratch_shapes=[pltpu.VMEM((D_MODEL, n_in), BF16), pltpu.VMEM((2, RB, D_MODEL), BF16)],
        compiler_params=_params(),
        name="inproj",
    )(*x_args, g.reshape(1, D_MODEL), shift, scale, w_all)


RC = RET_CHUNK
RET_PAIR = 2 * RET_DK


def _rope(x, cos, sin_signed):
    lane = lax.broadcasted_iota(jnp.int32, x.shape, 1)
    swapped = jnp.where(lane % 2 == 0, pltpu.roll(x, x.shape[1] - 1, 1), pltpu.roll(x, 1, 1))
    return x * cos + swapped * sin_signed


def _ret_kernel(*refs, reverse):
    for _ in _ret_stages(*refs, reverse=reverse):
        pass


def _ret_stages(decay_ref, q_ref, k_ref, v_ref, cos_ref, sin_ref, s0_ref, acc_ref, *rest, reverse):
    del acc_ref
    if reverse:
        of_ref, g_ref, gn_ref, o_ref, sfin_ref, st_ref, dm_ref, dq_ref, dk_ref, ds_ref = rest
    else:
        o_ref, sfin_ref, st_ref, dm_ref, dq_ref, dk_ref, ds_ref = rest
    step = pl.program_id(0)
    blk = NBLK - 1 - step if reverse else step
    direction = 1 if reverse else 0

    @pl.when(step == 0)
    def _():
        row = lax.broadcasted_iota(jnp.int32, (RC, RC), 0).astype(F32)
        col = lax.broadcasted_iota(jnp.int32, (RC, RC), 1).astype(F32)
        for h in range(RET_HEADS):
            lg = -jnp.exp(jnp.full((RC, RC), decay_ref[direction, h], F32))
            if reverse:
                diff = col - row
                mask = diff > 0
                q_pow = RC - row
                k_pow = row
            else:
                diff = row - col
                mask = diff >= 0
                q_pow = row + 1.0
                k_pow = RC - 1.0 - row
            dm_ref[h] = jnp.where(mask, jnp.exp(lg * jnp.where(mask, diff, 0.0)), 0.0)
            dq_ref[h] = jnp.exp(lg * q_pow)
            dk_ref[h] = jnp.exp(lg * k_pow)
            ds_ref[h] = jnp.exp(lg * RC)

    starts = _is_last_block(blk) if reverse else _is_first_block(blk)

    is_prompt = blk < N_PROMPT_BLK

    @pl.when(starts)
    def _():
        st_ref[...] = jnp.zeros_like(st_ref)

    @pl.when(jnp.logical_and(starts, jnp.logical_not(is_prompt)))
    def _():
        for h in range(RET_HEADS):
            off = (h % 2) * RET_DK
            st_ref[h, off:off + RET_DK, :] = s0_ref[h]

    lane = lax.broadcasted_iota(jnp.int32, (1, RET_PAIR), 1)
    chunks = range(RB // RC)
    chunk_order = list(reversed(chunks) if reverse else chunks)
    units = [(h, c) for h in range(RET_HEADS) for c in chunk_order]
    rows_of = lambda c: slice(c * RC, (c + 1) * RC)
    vcols = lambda h: slice(h * RET_DV, (h + 1) * RET_DV)
    roped = {}
    for p in range(RET_HEADS // 2):
        cols = slice(p * RET_PAIR, (p + 1) * RET_PAIR)
        for c in chunk_order:
            rows = rows_of(c)
            cos, sin = cos_ref[rows, :], sin_ref[rows, :]
            roped[p, c] = (_rope(q_ref[rows, cols], cos, sin),
                           _rope(k_ref[rows, cols] * (RET_DK ** -0.5), cos, sin))
    yield
    qm_u, vh_u, att_u, kd_u = {}, {}, {}, {}
    for h, c in units:
        head_mask = (lane // RET_DK == h % 2).astype(F32)
        q2, k2 = roped[h // 2, c]
        vh_u[h, c] = v_ref[rows_of(c), vcols(h)].astype(BF16)
        qm_u[h, c] = (q2 * head_mask).astype(BF16)
        km = k2 * head_mask
        att_u[h, c] = (_dot_nt(qm_u[h, c], km.astype(BF16)) * dm_ref[h]).astype(BF16)
        kd_u[h, c] = (km * dk_ref[h]).astype(BF16)
    yield
    intra_u = {u: _dot(att_u[u], vh_u[u]) for u in units}
    delta_u = {u: _dot_tn(kd_u[u], vh_u[u]) for u in units}
    yield
    state_u = {}
    for h in range(RET_HEADS):
        st = st_ref[h]
        for c in chunk_order:
            state_u[h, c] = st.astype(BF16)
            st = st * ds_ref[h] + delta_u[h, c]
        st_ref[h] = st
    yield
    for h, c in units:
        rows, out_cols = rows_of(c), vcols(h)
        o = intra_u[h, c] + _dot(qm_u[h, c], state_u[h, c]) * dq_ref[h]
        if reverse:
            o = o + of_ref[rows, out_cols]
            o = o * lax.rsqrt(jnp.mean(o * o, axis=-1, keepdims=True) + EPS)
            o = o * gn_ref[:, out_cols] * _silu(g_ref[rows, out_cols])
            o_ref[rows, out_cols] = o.astype(o_ref.dtype)
        else:
            o_ref[rows, out_cols] = o

    ends = _is_first_block(blk) if reverse else _is_last_block(blk)

    @pl.when(jnp.logical_and(ends, is_prompt))
    def _():
        for h in range(RET_HEADS):
            off = (h % 2) * RET_DK
            sfin_ref[h] = st_ref[h, off:off + RET_DK, :]


def _sample_seq(blk):
    return jnp.clip(_seq_of_block(blk) - BATCH, 0, DEC_BATCH - 1)


def _prompt_seq(blk):
    return jnp.minimum(_seq_of_block(blk), BATCH - 1)


def _retention(decay, q, k, v, cos_tab, sin_tab, state_in, state_out, layer, *, reverse,
               o_fwd=None, g=None, gn=None, conv=None):
    bmap = (lambda j: NBLK - 1 - j) if reverse else (lambda j: j)
    direction = 1 if reverse else 0
    qd, vd = RET_HEADS * RET_DK, RET_HEADS * RET_DV
    row_spec = lambda width: pl.BlockSpec((RB, width), lambda j: (bmap(j), 0))
    state_blk = (None, None, None, RET_HEADS, RET_DK, RET_DV)
    in_specs = [pl.BlockSpec(memory_space=pltpu.SMEM), row_spec(qd), row_spec(qd), row_spec(vd),
                pl.BlockSpec((RB, RET_PAIR), lambda j: (_rope_block(bmap(j)), 0)),
                pl.BlockSpec((RB, RET_PAIR), lambda j: (_rope_block(bmap(j)), 0)),
                pl.BlockSpec(state_blk, lambda j: (_sample_seq(bmap(j)), layer, direction, 0, 0, 0)),
                pl.BlockSpec(memory_space=pl.ANY)]
    args = [decay, q, k, v, cos_tab, sin_tab, state_in, state_out]
    if reverse:
        in_specs += [row_spec(vd), row_spec(vd), pl.BlockSpec((1, vd), lambda j: (0, 0))]
        args += [o_fwd, g, gn.reshape(1, vd)]
    tile = pltpu.VMEM((RET_HEADS, RC, RC), F32)
    out_specs = [row_spec(vd),
                 pl.BlockSpec(state_blk, lambda j: (_prompt_seq(bmap(j)), layer, direction, 0, 0, 0))]
    out_shape = [jax.ShapeDtypeStruct((NT, vd), BF16 if reverse else F32),
                 jax.ShapeDtypeStruct(state_out.shape, F32)]
    scratch = [tile, tile, tile, tile, tile]
    body = functools.partial(_ret_kernel, reverse=reverse)
    if conv is not None:
        assert not reverse
        n_ret = (len(in_specs), len(out_specs), len(scratch))
        c_in, c_args, c_out, c_shape, c_scratch = _conv_parts(*conv)
        in_specs, args = in_specs + c_in, args + c_args
        out_specs, out_shape, scratch = out_specs + c_out, out_shape + c_shape, scratch + c_scratch
        body = functools.partial(_ret_conv_kernel, n_ret=n_ret, n_conv=(len(c_in), len(c_out), len(c_scratch)))
    return pl.pallas_call(
        body,
        grid=(NBLK,),
        in_specs=in_specs,
        out_specs=out_specs,
        out_shape=out_shape,
        input_output_aliases={7: 1},
        scratch_shapes=scratch,
        compiler_params=_params(),
        name="retention_bwd" if reverse else "retention_fwd",
    )(*args)


def _ret_conv_kernel(*refs, n_ret, n_conv):
    (ri, ro, rs), (ci, co, cs) = n_ret, n_conv
    ins, outs, scr = refs[:ri + ci], refs[ri + ci:ri + ci + ro + co], refs[ri + ci + ro + co:]
    bodies = [_conv_stages(*ins[ri:], *outs[ro:], *scr[rs:]),
              _ret_stages(*ins[:ri], *outs[:ro], *scr[:rs], reverse=False)]
    while bodies:
        for body in list(bodies):
            if next(body, StopIteration) is StopIteration:
                bodies.remove(body)


CONV_RT = 32
CONV_CT = 128
CONV_SPAN = RB + 2 * HALO - 8


def _conv_stages(a_ref, ga_ref, ap_ref, gap_ref, an_ref, gan_ref, cw_ref, cb_ref, lng_ref, lnb_ref,
                 o_ref, u_ref, y_ref, us_ref):
    blk = pl.program_id(0)
    keep_prev = jnp.where(_is_first_block(blk), 0.0, 1.0)
    keep_next = jnp.where(_is_last_block(blk), 0.0, 1.0)
    u_ref[0:HALO, :] = ap_ref[...] * _sigmoid(gap_ref[...]) * keep_prev
    u_ref[HALO:HALO + RB, :] = a_ref[...] * _sigmoid(ga_ref[...])
    u_ref[HALO + RB:HALO + RB + HALO, :] = an_ref[...] * _sigmoid(gan_ref[...]) * keep_next
    for r in range(1, 8):
        us_ref[r - 1] = u_ref[r:r + CONV_SPAN, :]
    for ct in range(CONV_CH // CONV_CT):
        yield
        cols = slice(ct * CONV_CT, (ct + 1) * CONV_CT)
        for rt in range(RB // CONV_RT):
            acc = jnp.zeros((CONV_RT, CONV_CT), F32)
            for w in range(CONV_WIDTH):
                tiles, r = divmod(HALO - CONV_PAD + w, 8)
                base = rt * CONV_RT + 8 * tiles
                src = u_ref if r == 0 else us_ref.at[r - 1]
                acc = acc + src[base:base + CONV_RT, cols] * cw_ref[w:w + 1, cols]
            y_ref[rt * CONV_RT:(rt + 1) * CONV_RT, cols] = acc + cb_ref[:, cols]
    y = y_ref[...]
    mu = jnp.mean(y, axis=-1, keepdims=True)
    var = jnp.mean(jnp.square(y - mu), axis=-1, keepdims=True)
    o_ref[...] = _silu((y - mu) * lax.rsqrt(var + EPS) * lng_ref[...] + lnb_ref[...]).astype(o_ref.dtype)


def _conv_parts(a, ga, cw, cb, lng, lnb):
    per_blk = RB // HALO
    n_halo = NT // HALO
    row_spec = pl.BlockSpec((RB, CONV_CH), lambda i: (i, 0))
    prev_spec = pl.BlockSpec((HALO, CONV_CH), lambda i: (jnp.maximum(i * per_blk - 1, 0), 0))
    next_spec = pl.BlockSpec((HALO, CONV_CH), lambda i: (jnp.minimum((i + 1) * per_blk, n_halo - 1), 0))
    vec = pl.BlockSpec((1, CONV_CH), lambda i: (0, 0))
    in_specs = [row_spec, row_spec, prev_spec, prev_spec, next_spec, next_spec,
                pl.BlockSpec((CONV_WIDTH, CONV_CH), lambda i: (0, 0)), vec, vec, vec]
    args = [a, ga, a, ga, a, ga, cw, cb.reshape(1, -1), lng.reshape(1, -1), lnb.reshape(1, -1)]
    scratch = [pltpu.VMEM((RB + 2 * HALO, CONV_CH), F32), pltpu.VMEM((RB, CONV_CH), F32),
               pltpu.VMEM((7, CONV_SPAN, CONV_CH), F32)]
    return in_specs, args, [row_spec], [jax.ShapeDtypeStruct((NT, CONV_CH), BF16)], scratch


GC = GLA_CHUNK
GLA_NSUB = GC // GLA_SUB


def _split3(a):
    p1 = a.astype(BF16)
    r1 = a - p1.astype(F32)
    p2 = r1.astype(BF16)
    p3 = (r1 - p2.astype(F32)).astype(BF16)
    return p1, p2, p3


def _gla_kernel(alr_ref, q_ref, k_ref, v_ref, wa_ref, ba_ref, s0_ref, acc_ref, *rest, reverse):
    del acc_ref
    if reverse:
        of_ref, r_ref, gn_ref, o_ref, sfin_ref, st_ref, b_ref, tri_ref, ob_ref = rest
    else:
        o_ref, sfin_ref, st_ref, b_ref, tri_ref = rest
        ob_ref = o_ref
    step = pl.program_id(0)
    blk = NBLK - 1 - step if reverse else step
    starts = _is_last_block(blk) if reverse else _is_first_block(blk)
    is_prompt = blk < N_PROMPT_BLK

    @pl.when(jnp.logical_and(starts, is_prompt))
    def _():
        st_ref[...] = jnp.zeros_like(st_ref)

    @pl.when(jnp.logical_and(starts, jnp.logical_not(is_prompt)))
    def _():
        for h in range(GLA_HEADS):
            st_ref[h] = s0_ref[h].T

    @pl.when(step == 0)
    def _():
        row = lax.broadcasted_iota(jnp.int32, (RB, RB), 0)
        col = lax.broadcasted_iota(jnp.int32, (RB, RB), 1)
        ordered = col >= row if reverse else col <= row
        tri_ref[...] = jnp.where(jnp.logical_and(row // GC == col // GC, ordered), 1.0, 0.0).astype(BF16)

    heads = range(GLA_HEADS)
    hcols = [slice(h * GLA_DK, (h + 1) * GLA_DK) for h in heads]
    alr = alr_ref[...]
    z = [_dot_hi(alr, wa_ref[:, hcols[h]]) + ba_ref[:, hcols[h]] for h in heads]
    log_a = [(jnp.minimum(zh, 0.0) - jnp.log(1.0 + jnp.exp(-jnp.abs(zh)))) * (1.0 / GLA_TAU) for zh in z]
    parts = [_split3(la) for la in log_a]
    tri = tri_ref[...]
    for h in heads:
        g1, g2, g3 = parts[h]
        b_ref[:, hcols[h]] = _dot(tri, g1) + (_dot(tri, g2) + _dot(tri, g3))

    c_row = lax.broadcasted_iota(jnp.int32, (GC, 1), 0)
    a_row = lax.broadcasted_iota(jnp.int32, (GC, GC), 0)
    a_col = lax.broadcasted_iota(jnp.int32, (GC, GC), 1)
    att_mask = a_col > a_row if reverse else a_col <= a_row
    chunks = range(RB // GC)
    chunk_order = list(reversed(chunks) if reverse else chunks)
    units = [(h, c) for h in range(GLA_HEADS) for c in chunk_order]
    kcols = lambda h: slice(h * GLA_DK, (h + 1) * GLA_DK)
    vcols = lambda h: slice(h * GLA_DV, (h + 1) * GLA_DV)
    rows_of = lambda c: slice(c * GC, (c + 1) * GC)

    vh_u, qe_u, ke_u, decay_u, att_u = {}, {}, {}, {}, {}
    for h, c in units:
        rows = rows_of(c)
        b = b_ref[rows, kcols(h)]
        qh = q_ref[rows, kcols(h)] * (GLA_DK ** -0.5)
        kh = k_ref[rows, kcols(h)]
        vh_u[h, c] = v_ref[rows, vcols(h)].astype(BF16)
        edge = b[0:1, :] if reverse else b[GC - 1:GC, :]
        bounds = []
        for s in range(GLA_NSUB):
            if reverse:
                hi = (s + 1) * GLA_SUB
                bounds.append(b[hi:hi + 1, :] if s < GLA_NSUB - 1 else jnp.zeros((1, GLA_DK), F32))
            else:
                lo = s * GLA_SUB
                bounds.append(b[lo - 1:lo, :] if s > 0 else jnp.zeros((1, GLA_DK), F32))
        own = jnp.concatenate([jnp.broadcast_to(bd, (GLA_SUB, GLA_DK)) for bd in bounds], axis=0)
        q_own = qh * jnp.exp(b - own)
        q_parts, k_parts = [], []
        for s, bd in enumerate(bounds):
            q_parts.append(jnp.where(c_row // GLA_SUB == s, q_own, 0.0))
            reach = c_row >= s * GLA_SUB if reverse else c_row < (s + 1) * GLA_SUB
            k_parts.append(kh * jnp.exp(jnp.where(reach, bd - b, -jnp.inf)))
        q_bd = jnp.concatenate(q_parts, axis=1).astype(BF16)
        k_cat = jnp.concatenate(k_parts, axis=1).astype(BF16)
        att_u[h, c] = jnp.where(att_mask, _dot_nt(q_bd, k_cat), 0.0).astype(BF16)
        qe_u[h, c] = (qh * jnp.exp(b)).astype(BF16)
        ke_u[h, c] = (kh * jnp.exp(edge - b)).astype(BF16)
        decay_u[h, c] = jnp.exp(edge)
    intra_u = {u: _dot(att_u[u], vh_u[u]) for u in units}
    delta_u = {u: _dot_tn(vh_u[u], ke_u[u]) for u in units}
    state_u = {}
    for h in range(GLA_HEADS):
        st = st_ref[h]
        for c in chunk_order:
            state_u[h, c] = st.astype(BF16)
            st = st * decay_u[h, c] + delta_u[h, c]
        st_ref[h] = st
    for h, c in units:
        ob_ref[rows_of(c), vcols(h)] = intra_u[h, c] + _dot_nt(qe_u[h, c], state_u[h, c])

    if reverse:
        for h in range(GLA_HEADS):
            cols = vcols(h)
            o = ob_ref[:, cols] + of_ref[:, cols]
            o = o * lax.rsqrt(jnp.mean(o * o, axis=-1, keepdims=True) + EPS)
            o_ref[:, cols] = (o * gn_ref[:, cols] * _silu(r_ref[:, cols])).astype(o_ref.dtype)

    ends = _is_first_block(blk) if reverse else _is_last_block(blk)

    @pl.when(jnp.logical_and(ends, is_prompt))
    def _():
        for h in range(GLA_HEADS):
            sfin_ref[h] = st_ref[h].T


def _gla(alr, q, k, v, wa, ba, state_in, state_out, layer, *, reverse, o_fwd=None, r=None, gn=None):
    bmap = (lambda j: NBLK - 1 - j) if reverse else (lambda j: j)
    direction = 1 if reverse else 0
    qd, vd = GLA_HEADS * GLA_DK, GLA_HEADS * GLA_DV
    row_spec = lambda width: pl.BlockSpec((RB, width), lambda j: (bmap(j), 0))
    state_blk = (None, None, None, GLA_HEADS, GLA_DK, GLA_DV)
    in_specs = [row_spec(2 * GLA_RANK), row_spec(qd), row_spec(qd), row_spec(vd),
                pl.BlockSpec((2 * GLA_RANK, qd), lambda j: (0, 0)),
                pl.BlockSpec((1, qd), lambda j: (0, 0)),
                pl.BlockSpec(state_blk, lambda j: (_sample_seq(bmap(j)), layer, direction, 0, 0, 0)),
                pl.BlockSpec(memory_space=pl.ANY)]
    args = [alr, q, k, v, wa, ba, state_in, state_out]
    scratch = [pltpu.VMEM((GLA_HEADS, GLA_DV, GLA_DK), F32), pltpu.VMEM((RB, qd), F32),
               pltpu.VMEM((RB, RB), BF16)]
    if reverse:
        in_specs += [row_spec(vd), row_spec(vd), pl.BlockSpec((1, vd), lambda j: (0, 0))]
        args += [o_fwd, r, gn.reshape(1, vd)]
        scratch += [pltpu.VMEM((RB, vd), F32)]
    return pl.pallas_call(
        functools.partial(_gla_kernel, reverse=reverse),
        grid=(NBLK,),
        in_specs=in_specs,
        out_specs=[row_spec(vd),
                   pl.BlockSpec(state_blk, lambda j: (_prompt_seq(bmap(j)), layer, direction, 0, 0, 0))],
        out_shape=[jax.ShapeDtypeStruct((NT, vd), BF16 if reverse else F32),
                   jax.ShapeDtypeStruct(state_out.shape, F32)],
        input_output_aliases={7: 1},
        scratch_shapes=scratch,
        compiler_params=_params(),
        name="gla_bwd" if reverse else "gla_fwd",
    )(*args)


OUT_SUB = 128


def _outproj_kernel(*refs, n_mix):
    mix_refs = refs[:n_mix]
    (w_ref, x_ref, gate_ref, g2_ref, shift_ref, scale_ref, rw_ref, rb_ref,
     x1_ref, h2_ref, logit_ref, w_bf) = refs[n_mix:]
    step = pl.program_id(0)

    @pl.when(step == 0)
    def _():
        w_bf[...] = w_ref[...].astype(BF16)

    row = _mod_row(step)
    r_hi, r_lo = _split2(rw_ref[...])
    groups = [slice(p * OUT_SUB, (p + 1) * OUT_SUB) for p in range(RB // OUT_SUB)]
    mixed = []
    for rows in groups:
        m, off = None, 0
        for mix_ref in mix_refs:
            width = mix_ref.shape[1]
            part = _dot(mix_ref[rows, :], w_bf[off:off + width, :])
            m = part if m is None else m + part
            off += width
        mixed.append(m)
    normed = []
    for rows, m in zip(groups, mixed):
        x1 = x_ref[rows, :] + gate_ref[pl.ds(row, 1), :] * m
        x1_ref[rows, :] = x1
        h2 = _norm_mod(x1, g2_ref, shift_ref, scale_ref, row)
        h2_ref[rows, :] = _pack_rows(h2)
        normed.append(_split2(h2))
    for rows, (h_hi, h_lo) in zip(groups, normed):
        logit_ref[:, rows] = _dot_nt(r_hi, h_hi) + (_dot_nt(r_hi, h_lo) + _dot_nt(r_lo, h_hi)) + rb_ref[...]


def _outproj(mixes, w_all, index, x, gate, g2, shift, scale, rw, rb):
    n_mix = len(mixes)
    row_spec = lambda width: pl.BlockSpec((RB, width), lambda i: (i, 0))
    full = lambda shape: pl.BlockSpec(shape, lambda i: (0,) * len(shape))
    mod_spec = full((MOD_ROWS, D_MODEL))
    n_mixed = w_all.shape[1]
    return pl.pallas_call(
        functools.partial(_outproj_kernel, n_mix=n_mix),
        grid=(NBLK,),
        in_specs=[row_spec(m.shape[1]) for m in mixes]
        + [pl.BlockSpec((None, n_mixed, D_MODEL), lambda i: (index, 0, 0)),
           row_spec(D_MODEL), mod_spec, full((1, D_MODEL)), mod_spec, mod_spec,
           full((N_EXPERTS, D_MODEL)), full((N_EXPERTS, 1))],
        out_specs=[row_spec(D_MODEL), row_spec(D_MODEL // 2), pl.BlockSpec((N_EXPERTS, RB), lambda i: (0, i))],
        out_shape=[jax.ShapeDtypeStruct((NT, D_MODEL), F32), jax.ShapeDtypeStruct((NT, D_MODEL // 2), jnp.uint32),
                   jax.ShapeDtypeStruct((N_EXPERTS, NT), F32)],
        scratch_shapes=[pltpu.VMEM((n_mixed, D_MODEL), BF16)],
        compiler_params=_params(),
        name="outproj",
    )(*mixes, w_all, x, gate, g2.reshape(1, D_MODEL), shift, scale, rw.T, rb.reshape(N_EXPERTS, 1))


ROUTE_BLK = 2048
ROUTE_SUB = 256


def _route_kernel(lg_ref, idx_ref, rank_ref, gt_ref, cnt_ref, carry_ref):
    @pl.when(pl.program_id(0) == 0)
    def _():
        carry_ref[...] = jnp.zeros_like(carry_ref)

    logits = lg_ref[...]
    eid = lax.broadcasted_iota(jnp.int32, logits.shape, 0).astype(F32)
    work = logits
    onehots, top_vals = [], []
    for kk in range(TOP_K):
        top = jnp.max(work, axis=0, keepdims=True)
        first = jnp.min(jnp.where(work == top, eid, float(N_EXPERTS)), axis=0, keepdims=True)
        onehot = eid == first
        idx_ref[kk:kk + 1, :] = first.astype(jnp.int32)
        onehots.append(onehot)
        top_vals.append(top)
        work = jnp.where(onehot, -jnp.inf, work)
    exps = [jnp.exp(v - top_vals[0]) for v in top_vals]
    denom = exps[0]
    for e in exps[1:]:
        denom = denom + e
    gt_ref[...] = jnp.zeros_like(gt_ref)
    for kk in range(TOP_K):
        gt_ref[kk:kk + 1, :] = exps[kk] / denom

    sel = jnp.zeros(logits.shape, F32)
    for onehot in onehots:
        sel = sel + jnp.where(onehot, 1.0, 0.0)
    sel = sel.astype(BF16)
    r_i = lax.broadcasted_iota(jnp.int32, (ROUTE_SUB, ROUTE_SUB), 0)
    c_i = lax.broadcasted_iota(jnp.int32, (ROUTE_SUB, ROUTE_SUB), 1)
    before = jnp.where(r_i < c_i, 1.0, 0.0).astype(BF16)
    ones = jnp.ones((ROUTE_SUB, ROUTE_SUB), BF16)
    carry = carry_ref[...]
    for s in range(ROUTE_BLK // ROUTE_SUB):
        cols = slice(s * ROUTE_SUB, (s + 1) * ROUTE_SUB)
        pos = _dot(sel[:, cols], before) + carry
        for kk in range(TOP_K):
            rank = jnp.sum(jnp.where(onehots[kk][:, cols], pos, 0.0), axis=0, keepdims=True)
            rank_ref[kk:kk + 1, cols] = rank.astype(jnp.int32)
        carry = carry + _dot(sel[:, cols], ones)
    carry_ref[...] = carry
    cnt_ref[...] = carry.astype(jnp.int32)


def _route(logits_t):
    col_spec = lambda rows: pl.BlockSpec((rows, ROUTE_BLK), lambda i: (0, i))
    return pl.pallas_call(
        _route_kernel,
        grid=(NT // ROUTE_BLK,),
        in_specs=[col_spec(N_EXPERTS)],
        out_specs=[col_spec(TOP_K), col_spec(TOP_K), col_spec(8),
                   pl.BlockSpec((N_EXPERTS, ROUTE_SUB), lambda i: (0, 0))],
        out_shape=[jax.ShapeDtypeStruct((TOP_K, NT), jnp.int32), jax.ShapeDtypeStruct((TOP_K, NT), jnp.int32),
                   jax.ShapeDtypeStruct((8, NT), F32), jax.ShapeDtypeStruct((N_EXPERTS, ROUTE_SUB), jnp.int32)],
        scratch_shapes=[pltpu.VMEM((N_EXPERTS, ROUTE_SUB), F32)],
        compiler_params=_params(),
        name="route",
    )(logits_t)


TM = 1024
TM_SUB = 256
MOE_NBLK = NT * TOP_K // TM + N_EXPERTS
MOE_ROWS = MOE_NBLK * TM
HALF = D_MODEL // 2


def _moe_kernel(be_ref, nv_ref, nx_ref, x_ref, wgu_hbm, bgu_ref, wd_hbm, bd_ref, y_ref,
                wgu_st, wd_st, wgu_bf, wd_bf, sems, *, layer):
    i = pl.program_id(0)
    n_valid = nv_ref[i]

    def fetch(e):
        return (pltpu.make_async_copy(wgu_hbm.at[layer, e], wgu_st, sems.at[0]),
                pltpu.make_async_copy(wd_hbm.at[layer, e], wd_st, sems.at[1]))

    @pl.when(i == 0)
    def _():
        for cp in fetch(be_ref[0]):
            cp.start()

    @pl.when(n_valid > 0)
    def _():
        e = be_ref[i]
        changed = jnp.logical_or(i == 0, e != be_ref[jnp.maximum(i - 1, 0)])

        @pl.when(changed)
        def _():
            for cp in fetch(e):
                cp.wait()
            wgu_bf[...] = wgu_st[...].astype(BF16)
            wd_bf[...] = wd_st[...].astype(BF16)
            nxt = nx_ref[e]

            @pl.when(nxt >= 0)
            def _():
                for cp in fetch(nxt):
                    cp.start()

    def ffn_pass(p, masked):
        rows = slice(p * TM_SUB, (p + 1) * TM_SUB)
        x = x_ref[rows, :]
        if masked:
            row_id = lax.broadcasted_iota(jnp.int32, (TM_SUB, 1), 0) + p * TM_SUB
            x = jnp.where(row_id < n_valid, x, jnp.uint32(0))
        x_lo, x_hi = _unpack_rows(x)
        x_lo, x_hi = x_lo.astype(BF16), x_hi.astype(BF16)
        yield
        gu = _dot(x_lo, wgu_bf[:HALF, :]) + _dot(x_hi, wgu_bf[HALF:, :]) + bgu_ref[...]
        yield
        gate = jnp.minimum(gu[:, :D_FF], SWIGLU_LIMIT)
        up = jnp.clip(gu[:, D_FF:], -SWIGLU_LIMIT, SWIGLU_LIMIT)
        hdn = (gate * _sigmoid(SWIGLU_ALPHA * gate) * (up + 1.0)).astype(BF16)
        yield
        y = _dot(hdn, wd_bf[...]) + bd_ref[...]
        yield
        y_ref[rows, :] = _pack_rows(y)

    @pl.when(n_valid == TM)
    def _():
        passes = [ffn_pass(p, masked=False) for p in range(TM // TM_SUB)]
        n_stages = 5
        for t in range(n_stages + len(passes) - 1):
            for lag, body in enumerate(passes):
                if 0 <= t - lag < n_stages:
                    next(body, None)

    for p in range(TM // TM_SUB):
        @pl.when(jnp.logical_and(n_valid < TM, n_valid > p * TM_SUB))
        def _():
            for _ in ffn_pass(p, masked=True):
                pass


def _moe_experts(layer, block_e, n_valid, next_e, xs, w_gu, b_gu, w_down, b_down):
    grid_spec = pltpu.PrefetchScalarGridSpec(
        num_scalar_prefetch=3,
        grid=(MOE_NBLK,),
        in_specs=[pl.BlockSpec((TM, HALF), lambda i, be, nv, nx: (i, 0)),
                  pl.BlockSpec(memory_space=pl.ANY),
                  pl.BlockSpec((None, None, 1, 2 * D_FF), lambda i, be, nv, nx: (layer, be[i], 0, 0)),
                  pl.BlockSpec(memory_space=pl.ANY),
                  pl.BlockSpec((None, None, 1, D_MODEL), lambda i, be, nv, nx: (layer, be[i], 0, 0))],
        out_specs=pl.BlockSpec((TM, HALF), lambda i, be, nv, nx: (i, 0)),
        scratch_shapes=[pltpu.VMEM((D_MODEL, 2 * D_FF), F32), pltpu.VMEM((D_FF, D_MODEL), F32),
                        pltpu.VMEM((D_MODEL, 2 * D_FF), BF16), pltpu.VMEM((D_FF, D_MODEL), BF16),
                        pltpu.SemaphoreType.DMA((2,))],
    )
    return pl.pallas_call(
        functools.partial(_moe_kernel, layer=layer),
        grid_spec=grid_spec,
        out_shape=jax.ShapeDtypeStruct((MOE_ROWS, HALF), jnp.uint32),
        compiler_params=_params(),
        name="moe_experts",
    )(block_e, n_valid, next_e, xs, w_gu, b_gu.reshape(DEPTH, N_EXPERTS, 1, -1), w_down,
      b_down.reshape(DEPTH, N_EXPERTS, 1, -1))


SC_WORKERS = 32
SC_WIN = 64


def _sc_mesh():
    return plsc.VectorSubcoreMesh(core_axis_name="core", subcore_axis_name="subcore")


def _sc_worker():
    return lax.axis_index("core") * (SC_WORKERS // 2) + lax.axis_index("subcore")


def _sc_scatter_rows(x, dest_t, n_rows):
    n, width = x.shape
    kk = dest_t.shape[0]
    per = n // SC_WORKERS
    n_win = per // SC_WIN
    assert per * SC_WORKERS == n and n_win * SC_WIN == per and n_win % 2 == 0

    @pl.kernel(out_type=jax.ShapeDtypeStruct((n_rows, width), x.dtype), mesh=_sc_mesh(),
               scratch_types=[pltpu.VMEM((kk, per), jnp.int32), pltpu.VMEM((SC_WIN, width), x.dtype),
                              pltpu.VMEM((SC_WIN, width), x.dtype), pltpu.SemaphoreType.DMA((4,))])
    def scatter(x_hbm, i_hbm, o_hbm, idx_v, buf0, buf1, sems):
        base = _sc_worker() * per
        pltpu.sync_copy(i_hbm.at[:, pl.ds(base, per)], idx_v)

        def get(j, buf, s):
            return pltpu.make_async_copy(x_hbm.at[pl.ds(base + j * SC_WIN, SC_WIN)], buf, sems.at[s])

        def put(j, q, buf, s):
            return pltpu.make_async_copy(buf, o_hbm.at[idx_v.at[q, pl.ds(j * SC_WIN, SC_WIN)]], sems.at[s])

        get(0, buf0, 0).start()

        @pl.loop(0, n_win, step=2)
        def _(j):
            get(j, buf0, 0).wait()

            @pl.when(j > 0)
            def _():
                for q in range(kk):
                    put(j - 1, q, buf1, 3).wait()

            get(j + 1, buf1, 1).start()
            for q in range(kk):
                put(j, q, buf0, 2).start()
            get(j + 1, buf1, 1).wait()
            for q in range(kk):
                put(j, q, buf0, 2).wait()

            @pl.when(j + 2 < n_win)
            def _():
                get(j + 2, buf0, 0).start()

            for q in range(kk):
                put(j + 1, q, buf1, 3).start()

        for q in range(kk):
            put(n_win - 1, q, buf1, 3).wait()

    return scatter(x, dest_t)


def _sc_gather_rows(y, idx):
    n = idx.shape[0]
    width = y.shape[1]
    per = n // SC_WORKERS
    n_win = per // SC_WIN
    assert per * SC_WORKERS == n and n_win * SC_WIN == per and n_win % 2 == 0

    @pl.kernel(out_type=jax.ShapeDtypeStruct((n, width), y.dtype), mesh=_sc_mesh(),
               scratch_types=[pltpu.VMEM((per,), jnp.int32), pltpu.VMEM((SC_WIN, width), y.dtype),
                              pltpu.VMEM((SC_WIN, width), y.dtype), pltpu.SemaphoreType.DMA((4,))])
    def gather(y_hbm, i_hbm, o_hbm, idx_v, buf0, buf1, sems):
        base = _sc_worker() * per
        pltpu.sync_copy(i_hbm.at[pl.ds(base, per)], idx_v)

        def get(j, buf, s):
            return pltpu.make_async_copy(y_hbm.at[idx_v.at[pl.ds(j * SC_WIN, SC_WIN)]], buf, sems.at[s])

        def put(j, buf, s):
            return pltpu.make_async_copy(buf, o_hbm.at[pl.ds(base + j * SC_WIN, SC_WIN)], sems.at[s])

        get(0, buf0, 0).start()

        @pl.loop(0, n_win, step=2)
        def _(j):
            get(j, buf0, 0).wait()

            @pl.when(j > 0)
            def _():
                put(j - 1, buf1, 3).wait()

            get(j + 1, buf1, 1).start()
            put(j, buf0, 2).start()
            get(j + 1, buf1, 1).wait()
            put(j, buf0, 2).wait()

            @pl.when(j + 2 < n_win)
            def _():
                get(j + 2, buf0, 0).start()

            put(j + 1, buf1, 3).start()

        put(n_win - 1, buf1, 3).wait()

    return gather(y, idx)


def _gate_columns(gt_ref):
    r_i = lax.broadcasted_iota(jnp.int32, (RB, RB), 0)
    c_i = lax.broadcasted_iota(jnp.int32, (RB, RB), 1)
    eye = jnp.where(r_i == c_i, 1.0, 0.0).astype(BF16)
    g1, g2, g3 = _split3(gt_ref[...])
    return _dot_nt(eye, g1) + (_dot_nt(eye, g2) + _dot_nt(eye, g3))


def _combined_rows(x1_ref, yg_refs, gw, gate_ref, row, rows=slice(None)):
    acc_lo, acc_hi = None, None
    for kk in range(TOP_K):
        y_lo, y_hi = _unpack_rows(yg_refs[kk][rows, :])
        w = gw[rows, kk:kk + 1]
        acc_lo = y_lo * w if acc_lo is None else acc_lo + y_lo * w
        acc_hi = y_hi * w if acc_hi is None else acc_hi + y_hi * w
    x_lo = x1_ref[rows, :HALF] + gate_ref[pl.ds(row, 1), :HALF] * acc_lo
    x_hi = x1_ref[rows, HALF:] + gate_ref[pl.ds(row, 1), HALF:] * acc_hi
    return x_lo, x_hi


def _combine_specs(block=lambda i: i):
    slot_spec = lambda k: pl.BlockSpec((RB, HALF), lambda i: (k * NBLK + block(i), 0))
    return ([pl.BlockSpec((RB, D_MODEL), lambda i: (block(i), 0))] + [slot_spec(k) for k in range(TOP_K)]
            + [pl.BlockSpec((8, RB), lambda i: (0, block(i))),
               pl.BlockSpec((MOD_ROWS, D_MODEL), lambda i: (0, 0))])


def _combine_final_kernel(x1_ref, *rest):
    yg_refs = rest[:TOP_K]
    gt_ref, gate_ref, fg_ref, op_ref, os_ref = rest[TOP_K:]
    i = pl.program_id(0)
    x_lo, x_hi = _combined_rows(x1_ref, yg_refs, _gate_columns(gt_ref), gate_ref, _mod_row(i))
    ms = (jnp.sum(x_lo * x_lo, axis=-1, keepdims=True) + jnp.sum(x_hi * x_hi, axis=-1, keepdims=True)) / D_MODEL
    scale = lax.rsqrt(ms + EPS)

    @pl.when(i < N_PROMPT_BLK)
    def _():
        op_ref[:, :HALF] = x_lo * scale * fg_ref[:, :HALF]
        op_ref[:, HALF:] = x_hi * scale * fg_ref[:, HALF:]

    @pl.when(i >= N_PROMPT_BLK)
    def _():
        os_ref[:, :HALF] = x_lo * scale * fg_ref[:, :HALF]
        os_ref[:, HALF:] = x_hi * scale * fg_ref[:, HALF:]


def _combine_final(x1, yg, gates_t, gate, final_g):
    return pl.pallas_call(
        _combine_final_kernel,
        grid=(NBLK,),
        in_specs=_combine_specs() + [pl.BlockSpec((1, D_MODEL), lambda i: (0, 0))],
        out_specs=[pl.BlockSpec((RB, D_MODEL), lambda i: (jnp.minimum(i, N_PROMPT_BLK - 1), 0)),
                   pl.BlockSpec((RB, D_MODEL), lambda i: (jnp.maximum(i - N_PROMPT_BLK, 0), 0))],
        out_shape=[jax.ShapeDtypeStruct((NT_PROMPT, D_MODEL), F32),
                   jax.ShapeDtypeStruct((NT - NT_PROMPT, D_MODEL), F32)],
        compiler_params=_params(),
        name="moe_combine_final",
    )(x1, *([yg] * TOP_K), gates_t, gate, final_g.reshape(1, D_MODEL))


def _routing_plan(counts, idx_t, rank_t):
    counts = counts[:, 0]
    padded = (counts + TM - 1) // TM * TM
    pad_end = jnp.cumsum(padded)
    pad_start = pad_end - padded
    blk_row = (jnp.arange(MOE_NBLK, dtype=jnp.int32) * TM)[:, None]
    ids = jnp.arange(N_EXPERTS, dtype=jnp.int32)
    owns = jnp.logical_and(pad_start[None, :] <= blk_row, blk_row < pad_end[None, :])
    last_used = jnp.max(jnp.where(counts > 0, ids, 0))
    block_e = jnp.where(jnp.any(owns, axis=1), jnp.sum(jnp.where(owns, ids[None, :], 0), axis=1), last_used)
    block_e = block_e.astype(jnp.int32)
    left = jnp.clip(counts[None, :] - (blk_row - pad_start[None, :]), 0, TM)
    n_valid = jnp.sum(jnp.where(owns, left, 0), axis=1).astype(jnp.int32)
    start = jnp.zeros(idx_t.shape, jnp.int32)
    for e in range(N_EXPERTS):
        start = jnp.where(idx_t == e, pad_start[e], start)
    dest_t = (start + rank_t).astype(jnp.int32)
    later = jnp.where(jnp.logical_and(counts[None, :] > 0, ids[None, :] > ids[:, None]), ids[None, :], N_EXPERTS)
    next_e = jnp.min(later, axis=1)
    next_e = jnp.where(next_e == N_EXPERTS, -1, next_e).astype(jnp.int32)
    return block_e, n_valid, next_e, dest_t


def _rope_tables():
    rows = DEC_SEQ // GRID_W
    row = jnp.repeat(jnp.arange(rows, dtype=F32), GRID_W)
    col = jnp.tile(jnp.arange(GRID_W, dtype=F32), rows)
    n_f = RET_DK // 4
    freqs = ROPE_THETA ** (-jnp.arange(n_f, dtype=F32) / n_f)
    ang = jnp.concatenate([row[:, None] * freqs, col[:, None] * freqs], axis=-1)
    cos = jnp.repeat(jnp.cos(ang), 2, axis=-1)
    sin = jnp.repeat(jnp.sin(ang), 2, axis=-1) * jnp.tile(jnp.asarray([-1.0, 1.0], F32), RET_DK // 2)
    cos = jnp.concatenate([jnp.ones((RB, RET_DK), F32), cos], axis=0)
    sin = jnp.concatenate([jnp.zeros((RB, RET_DK), F32), sin], axis=0)
    return jnp.tile(cos, (1, 2)), jnp.tile(sin, (1, 2))


def kernel(x_prompt, x_sample, state_ret, state_gla, c, c_ctx, w_mod, b_mod, norm1_g, norm2_g, final_g, even_w_in, ret_decay, ret_gn, conv_w, conv_b, conv_ln_g, conv_ln_b, even_w_out, odd_w_in, gla_w_a2, gla_b_a2, gla_gn, odd_w_out, router_w, router_b, exp_w_gu, exp_b_gu, exp_w_down, exp_b_down):
    x_src = ("split", x_prompt.reshape(NT_PROMPT, D_MODEL), x_sample.reshape(NT - NT_PROMPT, D_MODEL))
    cvec = jnp.concatenate([c_ctx[None, :], c, jnp.zeros((MOD_ROWS - 1 - DEC_BATCH, D_MODEL), F32)], axis=0)
    mods = _modulation(cvec, w_mod, b_mod).reshape(DEPTH, MOD_ROWS, N_MOD, D_MODEL)
    cos_tab, sin_tab = _rope_tables()
    new_ret = jnp.zeros((BATCH,) + state_ret.shape[1:], F32)
    new_gla = jnp.zeros((BATCH,) + state_gla.shape[1:], F32)
    for l in range(DEPTH):
        mod = [mods[l, :, j, :] for j in range(N_MOD)]
        if l % 2 == 0:
            e = l // 2
            qd, vd = RET_HEADS * RET_DK, RET_HEADS * RET_DV
            x, q, k, v, g, a, ga = _inproj(x_src, norm1_g[l], mod[0], mod[1], even_w_in, e,
                                           (qd, qd, vd, vd, CONV_CH, CONV_CH))
            conv = (a, ga, conv_w[e], conv_b[e], conv_ln_g[e], conv_ln_b[e])
            o_f, new_ret, u = _retention(ret_decay[e], q, k, v, cos_tab, sin_tab, state_ret, new_ret, e,
                                         reverse=False, conv=conv)
            ret, new_ret = _retention(ret_decay[e], q, k, v, cos_tab, sin_tab, state_ret, new_ret, e,
                                      reverse=True, o_fwd=o_f, g=g, gn=ret_gn[e])
            mixes, w_out, w_index = [ret, u], even_w_out, e
        else:
            o = l // 2
            qd, vd = GLA_HEADS * GLA_DK, GLA_HEADS * GLA_DV
            x, q, k, v, r, alr = _inproj(x_src, norm1_g[l], mod[0], mod[1], odd_w_in, o,
                                         (qd, qd, vd, vd, 2 * GLA_RANK))
            zeros = jnp.zeros((GLA_RANK, qd), F32)
            wa_f = jnp.concatenate([gla_w_a2[o, 0], zeros], axis=0)
            wa_b = jnp.concatenate([zeros, gla_w_a2[o, 1]], axis=0)
            o_f, new_gla = _gla(alr, q, k, v, wa_f, gla_b_a2[o, 0].reshape(1, qd), state_gla, new_gla, o,
                                reverse=False)
            y, new_gla = _gla(alr, q, k, v, wa_b, gla_b_a2[o, 1].reshape(1, qd), state_gla, new_gla, o,
                              reverse=True, o_fwd=o_f, r=r, gn=gla_gn[o])
            mixes, w_out, w_index = [y], odd_w_out, o
        x1, h2, logits_t = _outproj(mixes, w_out, w_index, x, mod[2], norm2_g[l], mod[3], mod[4],
                                    router_w[l], router_b[l])
        idx_t, rank_t, gates_t, counts = _route(logits_t)
        block_e, n_valid, next_e, dest_t = _routing_plan(counts, idx_t, rank_t)
        xs = _sc_scatter_rows(h2, dest_t, MOE_ROWS)
        yb = _moe_experts(l, block_e, n_valid, next_e, xs, exp_w_gu, exp_b_gu, exp_w_down, exp_b_down)
        yg = _sc_gather_rows(yb, dest_t.reshape(TOP_K * NT))
        x_src = ("moe", x1, yg, gates_t, mod[5])
    y_prompt, y_sample = _combine_final(*x_src[1:], final_g)
    y_prompt = y_prompt.reshape(BATCH, SEQ, D_MODEL)
    y_sample = y_sample.reshape(DEC_BATCH, DEC_SEQ, D_MODEL)
    return (y_prompt, y_sample, new_ret, new_gla)
```

```python
import functools

import jax
import jax.numpy as jnp
from jax import lax
from jax.experimental import pallas as pl
from jax.experimental.pallas import tpu as pltpu
from jax.experimental.pallas import tpu_sc as plsc

F32 = jnp.float32
BF16 = jnp.bfloat16

D_MODEL = 1024
BATCH = 16
SEQ = 256
DEPTH = 4
DEC_BATCH = 4
DEC_SEQ = 4096
GRID_W = 64
RET_HEADS = 4
RET_DK = 64
RET_DV = 128
RET_CHUNK = 128
CONV_CH = 512
CONV_WIDTH = 31
CONV_PAD = CONV_WIDTH // 2
GLA_HEADS = 4
GLA_DK = 128
GLA_DV = 256
GLA_RANK = 16
GLA_TAU = 16.0
GLA_CHUNK = 64
GLA_SUB = 16
N_EXPERTS = 32
TOP_K = 4
D_FF = 1024
SWIGLU_LIMIT = 7.0
SWIGLU_ALPHA = 1.702
ROPE_THETA = 10000.0
EPS = 1e-6
N_MOD = 6

RB = 256
NT_PROMPT = BATCH * SEQ
NT = NT_PROMPT + DEC_BATCH * DEC_SEQ
NBLK = NT // RB
N_PROMPT_BLK = NT_PROMPT // RB
SAMPLE_BLK = DEC_SEQ // RB
MOD_ROWS = 8
HALO = 16
VMEM_LIMIT = 48 * 1024 * 1024

assert SEQ == RB and DEC_SEQ % RB == 0 and CONV_PAD < HALO


def _seq_of_block(i):
    return jnp.where(i < N_PROMPT_BLK, i, N_PROMPT_BLK + (i - N_PROMPT_BLK) // SAMPLE_BLK)


def _is_first_block(i):
    return jnp.logical_or(i < N_PROMPT_BLK, (i - N_PROMPT_BLK) % SAMPLE_BLK == 0)


def _is_last_block(i):
    return jnp.logical_or(i < N_PROMPT_BLK, (i - N_PROMPT_BLK) % SAMPLE_BLK == SAMPLE_BLK - 1)


def _mod_row(i):
    return jnp.where(i < N_PROMPT_BLK, 0, 1 + (i - N_PROMPT_BLK) // SAMPLE_BLK)


def _rope_block(i):
    return jnp.where(i < N_PROMPT_BLK, 0, 1 + (i - N_PROMPT_BLK) % SAMPLE_BLK)


def _dot(a, b):
    return jnp.dot(a, b, preferred_element_type=F32)


def _dot_nt(a, b):
    return lax.dot_general(a, b, (((1,), (1,)), ((), ())), preferred_element_type=F32)


def _dot_tn(a, b):
    return lax.dot_general(a, b, (((0,), (0,)), ((), ())), preferred_element_type=F32)


def _split2(a):
    hi = a.astype(BF16)
    lo = (a - hi.astype(F32)).astype(BF16)
    return hi, lo


def _dot_hi(a, b):
    a_hi, a_lo = _split2(a)
    b_hi, b_lo = _split2(b)
    return _dot(a_hi, b_hi) + (_dot(a_hi, b_lo) + _dot(a_lo, b_hi))


def _silu(x):
    return x * (1.0 / (1.0 + jnp.exp(-x)))


def _sigmoid(x):
    return 1.0 / (1.0 + jnp.exp(-x))


def _pack_rows(x):
    n = x.shape[1] // 2
    lo = pltpu.bitcast(x[:, :n].astype(BF16).astype(F32), jnp.uint32)
    hi = pltpu.bitcast(x[:, n:].astype(BF16).astype(F32), jnp.uint32)
    return hi | (lo >> 16)


def _unpack_rows(u):
    lo = pltpu.bitcast(u << 16, F32)
    hi = pltpu.bitcast(u & jnp.uint32(0xFFFF0000), F32)
    return lo, hi


def _params(n_axes=1, vmem=VMEM_LIMIT):
    return pltpu.CompilerParams(dimension_semantics=("arbitrary",) * n_axes, vmem_limit_bytes=vmem)


MOD_TN = 1536


def _mod_kernel(c_ref, w_ref, b_ref, o_ref):
    s = _silu(c_ref[...]).astype(BF16)
    o_ref[...] = _dot(s, w_ref[...].astype(BF16)) + b_ref[...]


def _modulation(cvec, w_mod, b_mod):
    n = N_MOD * D_MODEL
    return pl.pallas_call(
        _mod_kernel,
        grid=(DEPTH, n // MOD_TN),
        in_specs=[pl.BlockSpec((MOD_ROWS, D_MODEL), lambda l, j: (0, 0)),
                  pl.BlockSpec((None, D_MODEL, MOD_TN), lambda l, j: (l, 0, j)),
                  pl.BlockSpec((None, 1, MOD_TN), lambda l, j: (l, 0, j))],
        out_specs=pl.BlockSpec((None, MOD_ROWS, MOD_TN), lambda l, j: (l, 0, j)),
        out_shape=jax.ShapeDtypeStruct((DEPTH, MOD_ROWS, n), F32),
        compiler_params=_params(2),
        name="modulation",
    )(cvec, w_mod, b_mod.reshape(DEPTH, 1, n))


def _norm_mod(x, g_ref, shift_ref, scale_ref, row):
    y = x * lax.rsqrt(jnp.mean(x * x, axis=-1, keepdims=True) + EPS) * g_ref[...]
    return y * (1.0 + scale_ref[pl.ds(row, 1), :]) + shift_ref[pl.ds(row, 1), :]


N_SRC = {"split": 2, "moe": 3 + TOP_K}


def _inproj_kernel(*refs, widths, source):
    n_src = N_SRC[source]
    src = refs[:n_src]
    g_ref, shift_ref, scale_ref, w_ref = refs[n_src:n_src + 4]
    outs, w_bf = refs[n_src + 4:-1], refs[-1]
    step = pl.program_id(0)

    @pl.when(step == 0)
    def _():
        w_bf[...] = w_ref[...].astype(BF16)

    row = _mod_row(step)
    if source == "split":
        x = jnp.where(step < N_PROMPT_BLK, src[0][...], src[1][...])
    else:
        x1_ref, yg_refs, gt_ref, gate_ref = src[0], src[1:1 + TOP_K], src[-2], src[-1]
        x = jnp.concatenate(_combined_rows(x1_ref, yg_refs, _gate_columns(gt_ref), gate_ref, row), axis=1)
    outs[0][...] = x
    hb = _norm_mod(x, g_ref, shift_ref, scale_ref, row).astype(BF16)
    off = 0
    for o_ref, width in zip(outs[1:], widths):
        o_ref[...] = _dot(hb, w_bf[:, off:off + width])
        off += width


def _inproj(x, g, shift, scale, w_all, index, widths):
    source, x_args = x[0], list(x[1:])
    n_in = w_all.shape[2]
    row_spec = lambda width: pl.BlockSpec((RB, width), lambda i: (i, 0))
    full = lambda shape: pl.BlockSpec(shape, lambda i: (0,) * len(shape))
    if source == "split":
        x_specs = [pl.BlockSpec((RB, D_MODEL), lambda i: (jnp.minimum(i, N_PROMPT_BLK - 1), 0)),
                   pl.BlockSpec((RB, D_MODEL), lambda i: (jnp.maximum(i - N_PROMPT_BLK, 0), 0))]
    else:
        x1, yg, gates_t, gate = x_args
        x_specs, x_args = _combine_specs(), [x1] + [yg] * TOP_K + [gates_t, gate]
    out_widths = (D_MODEL,) + tuple(widths)
    return pl.pallas_call(
        functools.partial(_inproj_kernel, widths=widths, source=source),
        grid=(NBLK,),
        in_specs=x_specs + [full((1, D_MODEL)), full((MOD_ROWS, D_MODEL)), full((MOD_ROWS, D_MODEL)),
                            pl.BlockSpec((None, D_MODEL, n_in), lambda i: (index, 0, 0))],
        out_specs=[row_spec(width) for width in out_widths],
        out_shape=[jax.ShapeDtypeStruct((NT, width), F32) for width in out_widths],
        scratch_shapes=[pltpu.VMEM((D_MODEL, n_in), BF16)],
        compiler_params=_params(),
        name="inproj",
    )(*x_args, g.reshape(1, D_MODEL), shift, scale, w_all)


RC = RET_CHUNK
RET_PAIR = 2 * RET_DK


def _rope(x, cos, sin_signed):
    lane = lax.broadcasted_iota(jnp.int32, x.shape, 1)
    swapped = jnp.where(lane % 2 == 0, pltpu.roll(x, x.shape[1] - 1, 1), pltpu.roll(x, 1, 1))
    return x * cos + swapped * sin_signed


def _ret_kernel(*refs, reverse):
    for _ in _ret_stages(*refs, reverse=reverse):
        pass


def _ret_stages(decay_ref, q_ref, k_ref, v_ref, cos_ref, sin_ref, s0_ref, acc_ref, *rest, reverse):
    del acc_ref
    if reverse:
        of_ref, g_ref, gn_ref, o_ref, sfin_ref, st_ref, dm_ref, dq_ref, dk_ref, ds_ref = rest
    else:
        o_ref, sfin_ref, st_ref, dm_ref, dq_ref, dk_ref, ds_ref = rest
    step = pl.program_id(0)
    blk = NBLK - 1 - step if reverse else step
    direction = 1 if reverse else 0

    @pl.when(step == 0)
    def _():
        row = lax.broadcasted_iota(jnp.int32, (RC, RC), 0).astype(F32)
        col = lax.broadcasted_iota(jnp.int32, (RC, RC), 1).astype(F32)
        for h in range(RET_HEADS):
            lg = -jnp.exp(jnp.full((RC, RC), decay_ref[direction, h], F32))
            if reverse:
                diff = col - row
                mask = diff > 0
                q_pow = RC - row
                k_pow = row
            else:
                diff = row - col
                mask = diff >= 0
                q_pow = row + 1.0
                k_pow = RC - 1.0 - row
            dm_ref[h] = jnp.where(mask, jnp.exp(lg * jnp.where(mask, diff, 0.0)), 0.0)
            dq_ref[h] = jnp.exp(lg * q_pow)
            dk_ref[h] = jnp.exp(lg * k_pow)
            ds_ref[h] = jnp.exp(lg * RC)

    starts = _is_last_block(blk) if reverse else _is_first_block(blk)

    is_prompt = blk < N_PROMPT_BLK

    @pl.when(starts)
    def _():
        st_ref[...] = jnp.zeros_like(st_ref)

    @pl.when(jnp.logical_and(starts, jnp.logical_not(is_prompt)))
    def _():
        for h in range(RET_HEADS):
            off = (h % 2) * RET_DK
            st_ref[h, off:off + RET_DK, :] = s0_ref[h]

    lane = lax.broadcasted_iota(jnp.int32, (1, RET_PAIR), 1)
    chunks = range(RB // RC)
    chunk_order = list(reversed(chunks) if reverse else chunks)
    units = [(h, c) for h in range(RET_HEADS) for c in chunk_order]
    rows_of = lambda c: slice(c * RC, (c + 1) * RC)
    vcols = lambda h: slice(h * RET_DV, (h + 1) * RET_DV)
    roped = {}
    for p in range(RET_HEADS // 2):
        cols = slice(p * RET_PAIR, (p + 1) * RET_PAIR)
        for c in chunk_order:
            rows = rows_of(c)
            cos, sin = cos_ref[rows, :], sin_ref[rows, :]
            roped[p, c] = (_rope(q_ref[rows, cols], cos, sin),
                           _rope(k_ref[rows, cols] * (RET_DK ** -0.5), cos, sin))
    yield
    qm_u, vh_u, att_u, kd_u = {}, {}, {}, {}
    for h, c in units:
        head_mask = (lane // RET_DK == h % 2).astype(F32)
        q2, k2 = roped[h // 2, c]
        vh_u[h, c] = v_ref[rows_of(c), vcols(h)].astype(BF16)
        qm_u[h, c] = (q2 * head_mask).astype(BF16)
        km = k2 * head_mask
        att_u[h, c] = (_dot_nt(qm_u[h, c], km.astype(BF16)) * dm_ref[h]).astype(BF16)
        kd_u[h, c] = (km * dk_ref[h]).astype(BF16)
    yield
    intra_u = {u: _dot(att_u[u], vh_u[u]) for u in units}
    delta_u = {u: _dot_tn(kd_u[u], vh_u[u]) for u in units}
    yield
    state_u = {}
    for h in range(RET_HEADS):
        st = st_ref[h]
        for c in chunk_order:
            state_u[h, c] = st.astype(BF16)
            st = st * ds_ref[h] + delta_u[h, c]
        st_ref[h] = st
    yield
    for h, c in units:
        rows, out_cols = rows_of(c), vcols(h)
        o = intra_u[h, c] + _dot(qm_u[h, c], state_u[h, c]) * dq_ref[h]
        if reverse:
            o = o + of_ref[rows, out_cols]
            o = o * lax.rsqrt(jnp.mean(o * o, axis=-1, keepdims=True) + EPS)
            o = o * gn_ref[:, out_cols] * _silu(g_ref[rows, out_cols])
            o_ref[rows, out_cols] = o.astype(o_ref.dtype)
        else:
            o_ref[rows, out_cols] = o

    ends = _is_first_block(blk) if reverse else _is_last_block(blk)

    @pl.when(jnp.logical_and(ends, is_prompt))
    def _():
        for h in range(RET_HEADS):
            off = (h % 2) * RET_DK
            sfin_ref[h] = st_ref[h, off:off + RET_DK, :]


def _sample_seq(blk):
    return jnp.clip(_seq_of_block(blk) - BATCH, 0, DEC_BATCH - 1)


def _prompt_seq(blk):
    return jnp.minimum(_seq_of_block(blk), BATCH - 1)


def _retention(decay, q, k, v, cos_tab, sin_tab, state_in, state_out, layer, *, reverse,
               o_fwd=None, g=None, gn=None, conv=None, u=None, outproj=None):
    bmap = (lambda j: NBLK - 1 - j) if reverse else (lambda j: j)
    direction = 1 if reverse else 0
    qd, vd = RET_HEADS * RET_DK, RET_HEADS * RET_DV
    row_spec = lambda width: pl.BlockSpec((RB, width), lambda j: (bmap(j), 0))
    state_blk = (None, None, None, RET_HEADS, RET_DK, RET_DV)
    in_specs = [pl.BlockSpec(memory_space=pltpu.SMEM), row_spec(qd), row_spec(qd), row_spec(vd),
                pl.BlockSpec((RB, RET_PAIR), lambda j: (_rope_block(bmap(j)), 0)),
                pl.BlockSpec((RB, RET_PAIR), lambda j: (_rope_block(bmap(j)), 0)),
                pl.BlockSpec(state_blk, lambda j: (_sample_seq(bmap(j)), layer, direction, 0, 0, 0)),
                pl.BlockSpec(memory_space=pl.ANY)]
    args = [decay, q, k, v, cos_tab, sin_tab, state_in, state_out]
    if reverse:
        in_specs += [row_spec(vd), row_spec(vd), pl.BlockSpec((1, vd), lambda j: (0, 0))]
        args += [o_fwd, g, gn.reshape(1, vd)]
    tile = pltpu.VMEM((RET_HEADS, RC, RC), F32)
    out_specs = [row_spec(vd),
                 pl.BlockSpec(state_blk, lambda j: (_prompt_seq(bmap(j)), layer, direction, 0, 0, 0))]
    out_shape = [jax.ShapeDtypeStruct((NT, vd), BF16 if reverse else F32),
                 jax.ShapeDtypeStruct(state_out.shape, F32)]
    scratch = [tile, tile, tile, tile, tile]
    body = functools.partial(_ret_kernel, reverse=reverse)
    if conv is not None:
        assert not reverse
        n_ret = (len(in_specs), len(out_specs), len(scratch))
        c_in, c_args, c_out, c_shape, c_scratch = _conv_parts(*conv)
        in_specs, args = in_specs + c_in, args + c_args
        out_specs, out_shape, scratch = out_specs + c_out, out_shape + c_shape, scratch + c_scratch
        body = functools.partial(_ret_conv_kernel, n_ret=n_ret, n_conv=(len(c_in), len(c_out), len(c_scratch)))
    state_output = 1
    if outproj is not None:
        assert reverse
        call = _with_outproj(body, in_specs, args, out_specs, out_shape, scratch, [u], outproj)
        body, in_specs, args = call["kernel"], call["in_specs"], call["args"]
        out_specs, out_shape, scratch = call["out_specs"], call["out_shape"], call["scratch"]
        state_output = 0
    return pl.pallas_call(
        body,
        grid=(NBLK,),
        in_specs=in_specs,
        out_specs=out_specs,
        out_shape=out_shape,
        input_output_aliases={7: state_output},
        scratch_shapes=scratch,
        compiler_params=_params(),
        name="retention_bwd" if reverse else "retention_fwd",
    )(*args)


def _ret_conv_kernel(*refs, n_ret, n_conv):
    (ri, ro, rs), (ci, co, cs) = n_ret, n_conv
    ins, outs, scr = refs[:ri + ci], refs[ri + ci:ri + ci + ro + co], refs[ri + ci + ro + co:]
    bodies = [_conv_stages(*ins[ri:], *outs[ro:], *scr[rs:]),
              _ret_stages(*ins[:ri], *outs[:ro], *scr[:rs], reverse=False)]
    while bodies:
        for body in list(bodies):
            if next(body, StopIteration) is StopIteration:
                bodies.remove(body)


CONV_RT = 32
CONV_CT = 128
CONV_SPAN = RB + 2 * HALO - 8


def _conv_stages(a_ref, ga_ref, ap_ref, gap_ref, an_ref, gan_ref, cw_ref, cb_ref, lng_ref, lnb_ref,
                 o_ref, u_ref, y_ref, us_ref):
    blk = pl.program_id(0)
    keep_prev = jnp.where(_is_first_block(blk), 0.0, 1.0)
    keep_next = jnp.where(_is_last_block(blk), 0.0, 1.0)
    u_ref[0:HALO, :] = ap_ref[...] * _sigmoid(gap_ref[...]) * keep_prev
    u_ref[HALO:HALO + RB, :] = a_ref[...] * _sigmoid(ga_ref[...])
    u_ref[HALO + RB:HALO + RB + HALO, :] = an_ref[...] * _sigmoid(gan_ref[...]) * keep_next
    for r in range(1, 8):
        us_ref[r - 1] = u_ref[r:r + CONV_SPAN, :]
    for ct in range(CONV_CH // CONV_CT):
        yield
        cols = slice(ct * CONV_CT, (ct + 1) * CONV_CT)
        for rt in range(RB // CONV_RT):
            acc = jnp.zeros((CONV_RT, CONV_CT), F32)
            for w in range(CONV_WIDTH):
                tiles, r = divmod(HALO - CONV_PAD + w, 8)
                base = rt * CONV_RT + 8 * tiles
                src = u_ref if r == 0 else us_ref.at[r - 1]
                acc = acc + src[base:base + CONV_RT, cols] * cw_ref[w:w + 1, cols]
            y_ref[rt * CONV_RT:(rt + 1) * CONV_RT, cols] = acc + cb_ref[:, cols]
    y = y_ref[...]
    mu = jnp.mean(y, axis=-1, keepdims=True)
    var = jnp.mean(jnp.square(y - mu), axis=-1, keepdims=True)
    o_ref[...] = _silu((y - mu) * lax.rsqrt(var + EPS) * lng_ref[...] + lnb_ref[...]).astype(o_ref.dtype)


def _conv_parts(a, ga, cw, cb, lng, lnb):
    per_blk = RB // HALO
    n_halo = NT // HALO
    row_spec = pl.BlockSpec((RB, CONV_CH), lambda i: (i, 0))
    prev_spec = pl.BlockSpec((HALO, CONV_CH), lambda i: (jnp.maximum(i * per_blk - 1, 0), 0))
    next_spec = pl.BlockSpec((HALO, CONV_CH), lambda i: (jnp.minimum((i + 1) * per_blk, n_halo - 1), 0))
    vec = pl.BlockSpec((1, CONV_CH), lambda i: (0, 0))
    in_specs = [row_spec, row_spec, prev_spec, prev_spec, next_spec, next_spec,
                pl.BlockSpec((CONV_WIDTH, CONV_CH), lambda i: (0, 0)), vec, vec, vec]
    args = [a, ga, a, ga, a, ga, cw, cb.reshape(1, -1), lng.reshape(1, -1), lnb.reshape(1, -1)]
    scratch = [pltpu.VMEM((RB + 2 * HALO, CONV_CH), F32), pltpu.VMEM((RB, CONV_CH), F32),
               pltpu.VMEM((7, CONV_SPAN, CONV_CH), F32)]
    return in_specs, args, [row_spec], [jax.ShapeDtypeStruct((NT, CONV_CH), BF16)], scratch


GC = GLA_CHUNK
GLA_NSUB = GC // GLA_SUB


def _split3(a):
    p1 = a.astype(BF16)
    r1 = a - p1.astype(F32)
    p2 = r1.astype(BF16)
    p3 = (r1 - p2.astype(F32)).astype(BF16)
    return p1, p2, p3


def _gla_kernel(alr_ref, q_ref, k_ref, v_ref, wa_ref, ba_ref, s0_ref, acc_ref, *rest, reverse):
    del acc_ref
    if reverse:
        of_ref, r_ref, gn_ref, o_ref, sfin_ref, st_ref, b_ref, tri_ref, ob_ref = rest
    else:
        o_ref, sfin_ref, st_ref, b_ref, tri_ref = rest
        ob_ref = o_ref
    step = pl.program_id(0)
    blk = NBLK - 1 - step if reverse else step
    starts = _is_last_block(blk) if reverse else _is_first_block(blk)
    is_prompt = blk < N_PROMPT_BLK

    @pl.when(jnp.logical_and(starts, is_prompt))
    def _():
        st_ref[...] = jnp.zeros_like(st_ref)

    @pl.when(jnp.logical_and(starts, jnp.logical_not(is_prompt)))
    def _():
        for h in range(GLA_HEADS):
            st_ref[h] = s0_ref[h].T

    @pl.when(step == 0)
    def _():
        row = lax.broadcasted_iota(jnp.int32, (RB, RB), 0)
        col = lax.broadcasted_iota(jnp.int32, (RB, RB), 1)
        ordered = col >= row if reverse else col <= row
        tri_ref[...] = jnp.where(jnp.logical_and(row // GC == col // GC, ordered), 1.0, 0.0).astype(BF16)

    heads = range(GLA_HEADS)
    hcols = [slice(h * GLA_DK, (h + 1) * GLA_DK) for h in heads]
    alr = alr_ref[...]
    z = [_dot_hi(alr, wa_ref[:, hcols[h]]) + ba_ref[:, hcols[h]] for h in heads]
    log_a = [(jnp.minimum(zh, 0.0) - jnp.log(1.0 + jnp.exp(-jnp.abs(zh)))) * (1.0 / GLA_TAU) for zh in z]
    parts = [_split3(la) for la in log_a]
    tri = tri_ref[...]
    for h in heads:
        g1, g2, g3 = parts[h]
        b_ref[:, hcols[h]] = _dot(tri, g1) + (_dot(tri, g2) + _dot(tri, g3))

    c_row = lax.broadcasted_iota(jnp.int32, (GC, 1), 0)
    a_row = lax.broadcasted_iota(jnp.int32, (GC, GC), 0)
    a_col = lax.broadcasted_iota(jnp.int32, (GC, GC), 1)
    att_mask = a_col > a_row if reverse else a_col <= a_row
    chunks = range(RB // GC)
    chunk_order = list(reversed(chunks) if reverse else chunks)
    units = [(h, c) for h in range(GLA_HEADS) for c in chunk_order]
    kcols = lambda h: slice(h * GLA_DK, (h + 1) * GLA_DK)
    vcols = lambda h: slice(h * GLA_DV, (h + 1) * GLA_DV)
    rows_of = lambda c: slice(c * GC, (c + 1) * GC)

    vh_u, qe_u, ke_u, decay_u, att_u = {}, {}, {}, {}, {}
    for h, c in units:
        rows = rows_of(c)
        b = b_ref[rows, kcols(h)]
        qh = q_ref[rows, kcols(h)] * (GLA_DK ** -0.5)
        kh = k_ref[rows, kcols(h)]
        vh_u[h, c] = v_ref[rows, vcols(h)].astype(BF16)
        edge = b[0:1, :] if reverse else b[GC - 1:GC, :]
        bounds = []
        for s in range(GLA_NSUB):
            if reverse:
                hi = (s + 1) * GLA_SUB
                bounds.append(b[hi:hi + 1, :] if s < GLA_NSUB - 1 else jnp.zeros((1, GLA_DK), F32))
            else:
                lo = s * GLA_SUB
                bounds.append(b[lo - 1:lo, :] if s > 0 else jnp.zeros((1, GLA_DK), F32))
        own = jnp.concatenate([jnp.broadcast_to(bd, (GLA_SUB, GLA_DK)) for bd in bounds], axis=0)
        q_own = qh * jnp.exp(b - own)
        q_parts, k_parts = [], []
        for s, bd in enumerate(bounds):
            q_parts.append(jnp.where(c_row // GLA_SUB == s, q_own, 0.0))
            reach = c_row >= s * GLA_SUB if reverse else c_row < (s + 1) * GLA_SUB
            k_parts.append(kh * jnp.exp(jnp.where(reach, bd - b, -jnp.inf)))
        q_bd = jnp.concatenate(q_parts, axis=1).astype(BF16)
        k_cat = jnp.concatenate(k_parts, axis=1).astype(BF16)
        att_u[h, c] = jnp.where(att_mask, _dot_nt(q_bd, k_cat), 0.0).astype(BF16)
        qe_u[h, c] = (qh * jnp.exp(b)).astype(BF16)
        ke_u[h, c] = (kh * jnp.exp(edge - b)).astype(BF16)
        decay_u[h, c] = jnp.exp(edge)
    intra_u = {u: _dot(att_u[u], vh_u[u]) for u in units}
    delta_u = {u: _dot_tn(vh_u[u], ke_u[u]) for u in units}
    state_u = {}
    for h in range(GLA_HEADS):
        st = st_ref[h]
        for c in chunk_order:
            state_u[h, c] = st.astype(BF16)
            st = st * decay_u[h, c] + delta_u[h, c]
        st_ref[h] = st
    for h, c in units:
        ob_ref[rows_of(c), vcols(h)] = intra_u[h, c] + _dot_nt(qe_u[h, c], state_u[h, c])

    if reverse:
        for h in range(GLA_HEADS):
            cols = vcols(h)
            o = ob_ref[:, cols] + of_ref[:, cols]
            o = o * lax.rsqrt(jnp.mean(o * o, axis=-1, keepdims=True) + EPS)
            o_ref[:, cols] = (o * gn_ref[:, cols] * _silu(r_ref[:, cols])).astype(o_ref.dtype)

    ends = _is_first_block(blk) if reverse else _is_last_block(blk)

    @pl.when(jnp.logical_and(ends, is_prompt))
    def _():
        for h in range(GLA_HEADS):
            sfin_ref[h] = st_ref[h].T


def _gla(alr, q, k, v, wa, ba, state_in, state_out, layer, *, reverse, o_fwd=None, r=None, gn=None,
         outproj=None):
    bmap = (lambda j: NBLK - 1 - j) if reverse else (lambda j: j)
    direction = 1 if reverse else 0
    qd, vd = GLA_HEADS * GLA_DK, GLA_HEADS * GLA_DV
    row_spec = lambda width: pl.BlockSpec((RB, width), lambda j: (bmap(j), 0))
    state_blk = (None, None, None, GLA_HEADS, GLA_DK, GLA_DV)
    in_specs = [row_spec(2 * GLA_RANK), row_spec(qd), row_spec(qd), row_spec(vd),
                pl.BlockSpec((2 * GLA_RANK, qd), lambda j: (0, 0)),
                pl.BlockSpec((1, qd), lambda j: (0, 0)),
                pl.BlockSpec(state_blk, lambda j: (_sample_seq(bmap(j)), layer, direction, 0, 0, 0)),
                pl.BlockSpec(memory_space=pl.ANY)]
    args = [alr, q, k, v, wa, ba, state_in, state_out]
    scratch = [pltpu.VMEM((GLA_HEADS, GLA_DV, GLA_DK), F32), pltpu.VMEM((RB, qd), F32),
               pltpu.VMEM((RB, RB), BF16)]
    if reverse:
        in_specs += [row_spec(vd), row_spec(vd), pl.BlockSpec((1, vd), lambda j: (0, 0))]
        args += [o_fwd, r, gn.reshape(1, vd)]
        scratch += [pltpu.VMEM((RB, vd), F32)]
    call = dict(kernel=functools.partial(_gla_kernel, reverse=reverse), in_specs=in_specs, args=args,
                out_specs=[row_spec(vd),
                           pl.BlockSpec(state_blk, lambda j: (_prompt_seq(bmap(j)), layer, direction, 0, 0, 0))],
                out_shape=[jax.ShapeDtypeStruct((NT, vd), BF16 if reverse else F32),
                           jax.ShapeDtypeStruct(state_out.shape, F32)],
                scratch=scratch)
    state_output = 1
    if outproj is not None:
        assert reverse
        call = _with_outproj(call["kernel"], in_specs, args, call["out_specs"], call["out_shape"], scratch,
                             [], outproj)
        state_output = 0
    return pl.pallas_call(
        call["kernel"],
        grid=(NBLK,),
        in_specs=call["in_specs"],
        out_specs=call["out_specs"],
        out_shape=call["out_shape"],
        input_output_aliases={7: state_output},
        scratch_shapes=call["scratch"],
        compiler_params=_params(),
        name="gla_bwd" if reverse else "gla_fwd",
    )(*call["args"])


OUT_SUB = 128


def _outproj_kernel(*refs, n_mix):
    mix_refs = refs[:n_mix]
    (w_ref, x_ref, gate_ref, g2_ref, shift_ref, scale_ref, rw_ref, rb_ref,
     x1_ref, h2_ref, logit_ref, w_bf) = refs[n_mix:]
    step = pl.program_id(0)

    @pl.when(step == 0)
    def _():
        w_bf[...] = w_ref[...].astype(BF16)

    row = _mod_row(NBLK - 1 - step)
    r_hi, r_lo = _split2(rw_ref[...])
    groups = [slice(p * OUT_SUB, (p + 1) * OUT_SUB) for p in range(RB // OUT_SUB)]
    mixed = []
    for rows in groups:
        m, off = None, 0
        for mix_ref in mix_refs:
            width = mix_ref.shape[1]
            part = _dot(mix_ref[rows, :], w_bf[off:off + width, :])
            m = part if m is None else m + part
            off += width
        mixed.append(m)
    normed = []
    for rows, m in zip(groups, mixed):
        x1 = x_ref[rows, :] + gate_ref[pl.ds(row, 1), :] * m
        x1_ref[rows, :] = x1
        h2 = _norm_mod(x1, g2_ref, shift_ref, scale_ref, row)
        h2_ref[rows, :] = _pack_rows(h2)
        normed.append(_split2(h2))
    for rows, (h_hi, h_lo) in zip(groups, normed):
        logit_ref[:, rows] = _dot_nt(r_hi, h_hi) + (_dot_nt(r_hi, h_lo) + _dot_nt(r_lo, h_hi)) + rb_ref[...]


def _with_outproj(scan_body, in_specs, args, out_specs, out_shape, scratch, ext_mixes, outproj):
    w_all, index, x, gate, g2, shift, scale, rw, rb = outproj
    blk = lambda j: NBLK - 1 - j
    row_spec = lambda width: pl.BlockSpec((RB, width), lambda j: (blk(j), 0))
    full = lambda shape: pl.BlockSpec(shape, lambda j: (0,) * len(shape))
    mod_spec = full((MOD_ROWS, D_MODEL))
    n_mixed = w_all.shape[1]
    o_in = ([row_spec(m.shape[1]) for m in ext_mixes]
            + [pl.BlockSpec((None, n_mixed, D_MODEL), lambda j: (index, 0, 0)),
               row_spec(D_MODEL), mod_spec, full((1, D_MODEL)), mod_spec, mod_spec,
               full((N_EXPERTS, D_MODEL)), full((N_EXPERTS, 1))])
    o_args = list(ext_mixes) + [w_all, x, gate, g2.reshape(1, D_MODEL), shift, scale, rw.T,
                                rb.reshape(N_EXPERTS, 1)]
    o_out = [row_spec(D_MODEL), row_spec(D_MODEL // 2), pl.BlockSpec((N_EXPERTS, RB), lambda j: (0, blk(j)))]
    o_shape = [jax.ShapeDtypeStruct((NT, D_MODEL), F32), jax.ShapeDtypeStruct((NT, D_MODEL // 2), jnp.uint32),
               jax.ShapeDtypeStruct((N_EXPERTS, NT), F32)]
    n_si, n_oi, n_oo, n_ss = len(in_specs), len(o_in), len(o_out), len(scratch)
    mix = out_shape[0]

    def fused(*refs):
        scan_in, out_in = refs[:n_si], refs[n_si:n_si + n_oi]
        sfin_ref, out_out = refs[n_si + n_oi], refs[n_si + n_oi + 1:n_si + n_oi + 1 + n_oo]
        rest = refs[n_si + n_oi + 1 + n_oo:]
        scan_scr, mix_ref, w_bf = rest[:n_ss], rest[n_ss], rest[n_ss + 1]
        scan_body(*scan_in, mix_ref, sfin_ref, *scan_scr)
        _outproj_kernel(mix_ref, *out_in, *out_out, w_bf, n_mix=1 + len(ext_mixes))

    return dict(kernel=fused, in_specs=in_specs + o_in, args=args + o_args,
                out_specs=out_specs[1:] + o_out, out_shape=out_shape[1:] + o_shape,
                scratch=scratch + [pltpu.VMEM((RB, mix.shape[1]), mix.dtype),
                                   pltpu.VMEM((n_mixed, D_MODEL), BF16)])


ROUTE_BLK = 2048
ROUTE_SUB = 256


def _route_kernel(lg_ref, idx_ref, rank_ref, gt_ref, cnt_ref, carry_ref):
    @pl.when(pl.program_id(0) == 0)
    def _():
        carry_ref[...] = jnp.zeros_like(carry_ref)

    logits = lg_ref[...]
    eid = lax.broadcasted_iota(jnp.int32, logits.shape, 0).astype(F32)
    work = logits
    onehots, top_vals = [], []
    for kk in range(TOP_K):
        top = jnp.max(work, axis=0, keepdims=True)
        first = jnp.min(jnp.where(work == top, eid, float(N_EXPERTS)), axis=0, keepdims=True)
        onehot = eid == first
        idx_ref[kk:kk + 1, :] = first.astype(jnp.int32)
        onehots.append(onehot)
        top_vals.append(top)
        work = jnp.where(onehot, -jnp.inf, work)
    exps = [jnp.exp(v - top_vals[0]) for v in top_vals]
    denom = exps[0]
    for e in exps[1:]:
        denom = denom + e
    gt_ref[...] = jnp.zeros_like(gt_ref)
    for kk in range(TOP_K):
        gt_ref[kk:kk + 1, :] = exps[kk] / denom

    sel = jnp.zeros(logits.shape, F32)
    for onehot in onehots:
        sel = sel + jnp.where(onehot, 1.0, 0.0)
    sel = sel.astype(BF16)
    r_i = lax.broadcasted_iota(jnp.int32, (ROUTE_SUB, ROUTE_SUB), 0)
    c_i = lax.broadcasted_iota(jnp.int32, (ROUTE_SUB, ROUTE_SUB), 1)
    before = jnp.where(r_i < c_i, 1.0, 0.0).astype(BF16)
    ones = jnp.ones((ROUTE_SUB, ROUTE_SUB), BF16)
    carry = carry_ref[...]
    for s in range(ROUTE_BLK // ROUTE_SUB):
        cols = slice(s * ROUTE_SUB, (s + 1) * ROUTE_SUB)
        pos = _dot(sel[:, cols], before) + carry
        for kk in range(TOP_K):
            rank = jnp.sum(jnp.where(onehots[kk][:, cols], pos, 0.0), axis=0, keepdims=True)
            rank_ref[kk:kk + 1, cols] = rank.astype(jnp.int32)
        carry = carry + _dot(sel[:, cols], ones)
    carry_ref[...] = carry
    cnt_ref[...] = carry.astype(jnp.int32)


def _route(logits_t):
    col_spec = lambda rows: pl.BlockSpec((rows, ROUTE_BLK), lambda i: (0, i))
    return pl.pallas_call(
        _route_kernel,
        grid=(NT // ROUTE_BLK,),
        in_specs=[col_spec(N_EXPERTS)],
        out_specs=[col_spec(TOP_K), col_spec(TOP_K), col_spec(8),
                   pl.BlockSpec((N_EXPERTS, ROUTE_SUB), lambda i: (0, 0))],
        out_shape=[jax.ShapeDtypeStruct((TOP_K, NT), jnp.int32), jax.ShapeDtypeStruct((TOP_K, NT), jnp.int32),
                   jax.ShapeDtypeStruct((8, NT), F32), jax.ShapeDtypeStruct((N_EXPERTS, ROUTE_SUB), jnp.int32)],
        scratch_shapes=[pltpu.VMEM((N_EXPERTS, ROUTE_SUB), F32)],
        compiler_params=_params(),
        name="route",
    )(logits_t)


TM = 1024
TM_SUB = 256
MOE_NBLK = NT * TOP_K // TM + N_EXPERTS
MOE_ROWS = MOE_NBLK * TM
HALF = D_MODEL // 2


def _moe_kernel(be_ref, nv_ref, nx_ref, x_ref, wgu_hbm, bgu_ref, wd_hbm, bd_ref, y_ref,
                wgu_st, wd_st, wgu_bf, wd_bf, sems, *, layer):
    i = pl.program_id(0)
    n_valid = nv_ref[i]

    def fetch(e):
        return (pltpu.make_async_copy(wgu_hbm.at[layer, e], wgu_st, sems.at[0]),
                pltpu.make_async_copy(wd_hbm.at[layer, e], wd_st, sems.at[1]))

    @pl.when(i == 0)
    def _():
        for cp in fetch(be_ref[0]):
            cp.start()

    @pl.when(n_valid > 0)
    def _():
        e = be_ref[i]
        changed = jnp.logical_or(i == 0, e != be_ref[jnp.maximum(i - 1, 0)])

        @pl.when(changed)
        def _():
            for cp in fetch(e):
                cp.wait()
            wgu_bf[...] = wgu_st[...].astype(BF16)
            wd_bf[...] = wd_st[...].astype(BF16)
            nxt = nx_ref[e]

            @pl.when(nxt >= 0)
            def _():
                for cp in fetch(nxt):
                    cp.start()

    def ffn_pass(p, masked):
        rows = slice(p * TM_SUB, (p + 1) * TM_SUB)
        x = x_ref[rows, :]
        if masked:
            row_id = lax.broadcasted_iota(jnp.int32, (TM_SUB, 1), 0) + p * TM_SUB
            x = jnp.where(row_id < n_valid, x, jnp.uint32(0))
        x_lo, x_hi = _unpack_rows(x)
        x_lo, x_hi = x_lo.astype(BF16), x_hi.astype(BF16)
        yield
        gu = _dot(x_lo, wgu_bf[:HALF, :]) + _dot(x_hi, wgu_bf[HALF:, :]) + bgu_ref[...]
        yield
        gate = jnp.minimum(gu[:, :D_FF], SWIGLU_LIMIT)
        up = jnp.clip(gu[:, D_FF:], -SWIGLU_LIMIT, SWIGLU_LIMIT)
        hdn = (gate * _sigmoid(SWIGLU_ALPHA * gate) * (up + 1.0)).astype(BF16)
        yield
        y = _dot(hdn, wd_bf[...]) + bd_ref[...]
        yield
        y_ref[rows, :] = _pack_rows(y)

    @pl.when(n_valid == TM)
    def _():
        passes = [ffn_pass(p, masked=False) for p in range(TM // TM_SUB)]
        n_stages = 5
        for t in range(n_stages + len(passes) - 1):
            for lag, body in enumerate(passes):
                if 0 <= t - lag < n_stages:
                    next(body, None)

    for p in range(TM // TM_SUB):
        @pl.when(jnp.logical_and(n_valid < TM, n_valid > p * TM_SUB))
        def _():
            for _ in ffn_pass(p, masked=True):
                pass


def _moe_experts(layer, block_e, n_valid, next_e, xs, w_gu, b_gu, w_down, b_down):
    grid_spec = pltpu.PrefetchScalarGridSpec(
        num_scalar_prefetch=3,
        grid=(MOE_NBLK,),
        in_specs=[pl.BlockSpec((TM, HALF), lambda i, be, nv, nx: (i, 0)),
                  pl.BlockSpec(memory_space=pl.ANY),
                  pl.BlockSpec((None, None, 1, 2 * D_FF), lambda i, be, nv, nx: (layer, be[i], 0, 0)),
                  pl.BlockSpec(memory_space=pl.ANY),
                  pl.BlockSpec((None, None, 1, D_MODEL), lambda i, be, nv, nx: (layer, be[i], 0, 0))],
        out_specs=pl.BlockSpec((TM, HALF), lambda i, be, nv, nx: (i, 0)),
        scratch_shapes=[pltpu.VMEM((D_MODEL, 2 * D_FF), F32), pltpu.VMEM((D_FF, D_MODEL), F32),
                        pltpu.VMEM((D_MODEL, 2 * D_FF), BF16), pltpu.VMEM((D_FF, D_MODEL), BF16),
                        pltpu.SemaphoreType.DMA((2,))],
    )
    return pl.pallas_call(
        functools.partial(_moe_kernel, layer=layer),
        grid_spec=grid_spec,
        out_shape=jax.ShapeDtypeStruct((MOE_ROWS, HALF), jnp.uint32),
        compiler_params=_params(),
        name="moe_experts",
    )(block_e, n_valid, next_e, xs, w_gu, b_gu.reshape(DEPTH, N_EXPERTS, 1, -1), w_down,
      b_down.reshape(DEPTH, N_EXPERTS, 1, -1))


SC_WORKERS = 32
SC_WIN = 64


def _sc_mesh():
    return plsc.VectorSubcoreMesh(core_axis_name="core", subcore_axis_name="subcore")


def _sc_worker():
    return lax.axis_index("core") * (SC_WORKERS // 2) + lax.axis_index("subcore")


def _sc_scatter_rows(x, dest_t, n_rows):
    n, width = x.shape
    kk = dest_t.shape[0]
    per = n // SC_WORKERS
    n_win = per // SC_WIN
    assert per * SC_WORKERS == n and n_win * SC_WIN == per and n_win % 2 == 0

    @pl.kernel(out_type=jax.ShapeDtypeStruct((n_rows, width), x.dtype), mesh=_sc_mesh(),
               scratch_types=[pltpu.VMEM((kk, per), jnp.int32), pltpu.VMEM((SC_WIN, width), x.dtype),
                              pltpu.VMEM((SC_WIN, width), x.dtype), pltpu.SemaphoreType.DMA((4,))])
    def scatter(x_hbm, i_hbm, o_hbm, idx_v, buf0, buf1, sems):
        base = _sc_worker() * per
        pltpu.sync_copy(i_hbm.at[:, pl.ds(base, per)], idx_v)

        def get(j, buf, s):
            return pltpu.make_async_copy(x_hbm.at[pl.ds(base + j * SC_WIN, SC_WIN)], buf, sems.at[s])

        def put(j, q, buf, s):
            return pltpu.make_async_copy(buf, o_hbm.at[idx_v.at[q, pl.ds(j * SC_WIN, SC_WIN)]], sems.at[s])

        get(0, buf0, 0).start()

        @pl.loop(0, n_win, step=2)
        def _(j):
            get(j, buf0, 0).wait()

            @pl.when(j > 0)
            def _():
                for q in range(kk):
                    put(j - 1, q, buf1, 3).wait()

            get(j + 1, buf1, 1).start()
            for q in range(kk):
                put(j, q, buf0, 2).start()
            get(j + 1, buf1, 1).wait()
            for q in range(kk):
                put(j, q, buf0, 2).wait()

            @pl.when(j + 2 < n_win)
            def _():
                get(j + 2, buf0, 0).start()

            for q in range(kk):
                put(j + 1, q, buf1, 3).start()

        for q in range(kk):
            put(n_win - 1, q, buf1, 3).wait()

    return scatter(x, dest_t)


def _sc_gather_rows(y, idx):
    n = idx.shape[0]
    width = y.shape[1]
    per = n // SC_WORKERS
    n_win = per // SC_WIN
    assert per * SC_WORKERS == n and n_win * SC_WIN == per and n_win % 2 == 0

    @pl.kernel(out_type=jax.ShapeDtypeStruct((n, width), y.dtype), mesh=_sc_mesh(),
               scratch_types=[pltpu.VMEM((per,), jnp.int32), pltpu.VMEM((SC_WIN, width), y.dtype),
                              pltpu.VMEM((SC_WIN, width), y.dtype), pltpu.SemaphoreType.DMA((4,))])
    def gather(y_hbm, i_hbm, o_hbm, idx_v, buf0, buf1, sems):
        base = _sc_worker() * per
        pltpu.sync_copy(i_hbm.at[pl.ds(base, per)], idx_v)

        def get(j, buf, s):
            return pltpu.make_async_copy(y_hbm.at[idx_v.at[pl.ds(j * SC_WIN, SC_WIN)]], buf, sems.at[s])

        def put(j, buf, s):
            return pltpu.make_async_copy(buf, o_hbm.at[pl.ds(base + j * SC_WIN, SC_WIN)], sems.at[s])

        get(0, buf0, 0).start()

        @pl.loop(0, n_win, step=2)
        def _(j):
            get(j, buf0, 0).wait()

            @pl.when(j > 0)
            def _():
                put(j - 1, buf1, 3).wait()

            get(j + 1, buf1, 1).start()
            put(j, buf0, 2).start()
            get(j + 1, buf1, 1).wait()
            put(j, buf0, 2).wait()

            @pl.when(j + 2 < n_win)
            def _():
                get(j + 2, buf0, 0).start()

            put(j + 1, buf1, 3).start()

        put(n_win - 1, buf1, 3).wait()

    return gather(y, idx)


def _gate_columns(gt_ref):
    r_i = lax.broadcasted_iota(jnp.int32, (RB, RB), 0)
    c_i = lax.broadcasted_iota(jnp.int32, (RB, RB), 1)
    eye = jnp.where(r_i == c_i, 1.0, 0.0).astype(BF16)
    g1, g2, g3 = _split3(gt_ref[...])
    return _dot_nt(eye, g1) + (_dot_nt(eye, g2) + _dot_nt(eye, g3))


def _combined_rows(x1_ref, yg_refs, gw, gate_ref, row, rows=slice(None)):
    acc_lo, acc_hi = None, None
    for kk in range(TOP_K):
        y_lo, y_hi = _unpack_rows(yg_refs[kk][rows, :])
        w = gw[rows, kk:kk + 1]
        acc_lo = y_lo * w if acc_lo is None else acc_lo + y_lo * w
        acc_hi = y_hi * w if acc_hi is None else acc_hi + y_hi * w
    x_lo = x1_ref[rows, :HALF] + gate_ref[pl.ds(row, 1), :HALF] * acc_lo
    x_hi = x1_ref[rows, HALF:] + gate_ref[pl.ds(row, 1), HALF:] * acc_hi
    return x_lo, x_hi


def _combine_specs():
    slot_spec = lambda k: pl.BlockSpec((RB, HALF), lambda i: (k * NBLK + i, 0))
    return ([pl.BlockSpec((RB, D_MODEL), lambda i: (i, 0))] + [slot_spec(k) for k in range(TOP_K)]
            + [pl.BlockSpec((8, RB), lambda i: (0, i)), pl.BlockSpec((MOD_ROWS, D_MODEL), lambda i: (0, 0))])


def _combine_final_kernel(x1_ref, *rest):
    yg_refs = rest[:TOP_K]
    gt_ref, gate_ref, fg_ref, op_ref, os_ref = rest[TOP_K:]
    i = pl.program_id(0)
    x_lo, x_hi = _combined_rows(x1_ref, yg_refs, _gate_columns(gt_ref), gate_ref, _mod_row(i))
    ms = (jnp.sum(x_lo * x_lo, axis=-1, keepdims=True) + jnp.sum(x_hi * x_hi, axis=-1, keepdims=True)) / D_MODEL
    scale = lax.rsqrt(ms + EPS)

    @pl.when(i < N_PROMPT_BLK)
    def _():
        op_ref[:, :HALF] = x_lo * scale * fg_ref[:, :HALF]
        op_ref[:, HALF:] = x_hi * scale * fg_ref[:, HALF:]

    @pl.when(i >= N_PROMPT_BLK)
    def _():
        os_ref[:, :HALF] = x_lo * scale * fg_ref[:, :HALF]
        os_ref[:, HALF:] = x_hi * scale * fg_ref[:, HALF:]


def _combine_final(x1, yg, gates_t, gate, final_g):
    return pl.pallas_call(
        _combine_final_kernel,
        grid=(NBLK,),
        in_specs=_combine_specs() + [pl.BlockSpec((1, D_MODEL), lambda i: (0, 0))],
        out_specs=[pl.BlockSpec((RB, D_MODEL), lambda i: (jnp.minimum(i, N_PROMPT_BLK - 1), 0)),
                   pl.BlockSpec((RB, D_MODEL), lambda i: (jnp.maximum(i - N_PROMPT_BLK, 0), 0))],
        out_shape=[jax.ShapeDtypeStruct((NT_PROMPT, D_MODEL), F32),
                   jax.ShapeDtypeStruct((NT - NT_PROMPT, D_MODEL), F32)],
        compiler_params=_params(),
        name="moe_combine_final",
    )(x1, *([yg] * TOP_K), gates_t, gate, final_g.reshape(1, D_MODEL))


def _routing_plan(counts, idx_t, rank_t):
    counts = counts[:, 0]
    padded = (counts + TM - 1) // TM * TM
    pad_end = jnp.cumsum(padded)
    pad_start = pad_end - padded
    blk_row = (jnp.arange(MOE_NBLK, dtype=jnp.int32) * TM)[:, None]
    ids = jnp.arange(N_EXPERTS, dtype=jnp.int32)
    owns = jnp.logical_and(pad_start[None, :] <= blk_row, blk_row < pad_end[None, :])
    last_used = jnp.max(jnp.where(counts > 0, ids, 0))
    block_e = jnp.where(jnp.any(owns, axis=1), jnp.sum(jnp.where(owns, ids[None, :], 0), axis=1), last_used)
    block_e = block_e.astype(jnp.int32)
    left = jnp.clip(counts[None, :] - (blk_row - pad_start[None, :]), 0, TM)
    n_valid = jnp.sum(jnp.where(owns, left, 0), axis=1).astype(jnp.int32)
    start = jnp.zeros(idx_t.shape, jnp.int32)
    for e in range(N_EXPERTS):
        start = jnp.where(idx_t == e, pad_start[e], start)
    dest_t = (start + rank_t).astype(jnp.int32)
    later = jnp.where(jnp.logical_and(counts[None, :] > 0, ids[None, :] > ids[:, None]), ids[None, :], N_EXPERTS)
    next_e = jnp.min(later, axis=1)
    next_e = jnp.where(next_e == N_EXPERTS, -1, next_e).astype(jnp.int32)
    return block_e, n_valid, next_e, dest_t


def _rope_tables():
    rows = DEC_SEQ // GRID_W
    row = jnp.repeat(jnp.arange(rows, dtype=F32), GRID_W)
    col = jnp.tile(jnp.arange(GRID_W, dtype=F32), rows)
    n_f = RET_DK // 4
    freqs = ROPE_THETA ** (-jnp.arange(n_f, dtype=F32) / n_f)
    ang = jnp.concatenate([row[:, None] * freqs, col[:, None] * freqs], axis=-1)
    cos = jnp.repeat(jnp.cos(ang), 2, axis=-1)
    sin = jnp.repeat(jnp.sin(ang), 2, axis=-1) * jnp.tile(jnp.asarray([-1.0, 1.0], F32), RET_DK // 2)
    cos = jnp.concatenate([jnp.ones((RB, RET_DK), F32), cos], axis=0)
    sin = jnp.concatenate([jnp.zeros((RB, RET_DK), F32), sin], axis=0)
    return jnp.tile(cos, (1, 2)), jnp.tile(sin, (1, 2))


def kernel(x_prompt, x_sample, state_ret, state_gla, c, c_ctx, w_mod, b_mod, norm1_g, norm2_g, final_g, even_w_in, ret_decay, ret_gn, conv_w, conv_b, conv_ln_g, conv_ln_b, even_w_out, odd_w_in, gla_w_a2, gla_b_a2, gla_gn, odd_w_out, router_w, router_b, exp_w_gu, exp_b_gu, exp_w_down, exp_b_down):
    x_src = ("split", x_prompt.reshape(NT_PROMPT, D_MODEL), x_sample.reshape(NT - NT_PROMPT, D_MODEL))
    cvec = jnp.concatenate([c_ctx[None, :], c, jnp.zeros((MOD_ROWS - 1 - DEC_BATCH, D_MODEL), F32)], axis=0)
    mods = _modulation(cvec, w_mod, b_mod).reshape(DEPTH, MOD_ROWS, N_MOD, D_MODEL)
    cos_tab, sin_tab = _rope_tables()
    new_ret = jnp.zeros((BATCH,) + state_ret.shape[1:], F32)
    new_gla = jnp.zeros((BATCH,) + state_gla.shape[1:], F32)
    for l in range(DEPTH):
        mod = [mods[l, :, j, :] for j in range(N_MOD)]
        if l % 2 == 0:
            e = l // 2
            qd, vd = RET_HEADS * RET_DK, RET_HEADS * RET_DV
            x, q, k, v, g, a, ga = _inproj(x_src, norm1_g[l], mod[0], mod[1], even_w_in, e,
                                           (qd, qd, vd, vd, CONV_CH, CONV_CH))
            conv = (a, ga, conv_w[e], conv_b[e], conv_ln_g[e], conv_ln_b[e])
            o_f, new_ret, u = _retention(ret_decay[e], q, k, v, cos_tab, sin_tab, state_ret, new_ret, e,
                                         reverse=False, conv=conv)
            outproj = (even_w_out, e, x, mod[2], norm2_g[l], mod[3], mod[4], router_w[l], router_b[l])
            new_ret, x1, h2, logits_t = _retention(ret_decay[e], q, k, v, cos_tab, sin_tab, state_ret, new_ret, e,
                                                   reverse=True, o_fwd=o_f, g=g, gn=ret_gn[e], u=u,
                                                   outproj=outproj)
        else:
            o = l // 2
            qd, vd = GLA_HEADS * GLA_DK, GLA_HEADS * GLA_DV
            x, q, k, v, r, alr = _inproj(x_src, norm1_g[l], mod[0], mod[1], odd_w_in, o,
                                         (qd, qd, vd, vd, 2 * GLA_RANK))
            zeros = jnp.zeros((GLA_RANK, qd), F32)
            wa_f = jnp.concatenate([gla_w_a2[o, 0], zeros], axis=0)
            wa_b = jnp.concatenate([zeros, gla_w_a2[o, 1]], axis=0)
            o_f, new_gla = _gla(alr, q, k, v, wa_f, gla_b_a2[o, 0].reshape(1, qd), state_gla, new_gla, o,
                                reverse=False)
            outproj = (odd_w_out, o, x, mod[2], norm2_g[l], mod[3], mod[4], router_w[l], router_b[l])
            new_gla, x1, h2, logits_t = _gla(alr, q, k, v, wa_b, gla_b_a2[o, 1].reshape(1, qd), state_gla, new_gla,
                                             o, reverse=True, o_fwd=o_f, r=r, gn=gla_gn[o], outproj=outproj)
        idx_t, rank_t, gates_t, counts = _route(logits_t)
        block_e, n_valid, next_e, dest_t = _routing_plan(counts, idx_t, rank_t)
        xs = _sc_scatter_rows(h2, dest_t, MOE_ROWS)
        yb = _moe_experts(l, block_e, n_valid, next_e, xs, exp_w_gu, exp_b_gu, exp_w_down, exp_b_down)
        yg = _sc_gather_rows(yb, dest_t.reshape(TOP_K * NT))
        x_src = ("moe", x1, yg, gates_t, mod[5])
    y_prompt, y_sample = _combine_final(*x_src[1:], final_g)
    y_prompt = y_prompt.reshape(BATCH, SEQ, D_MODEL)
    y_sample = y_sample.reshape(DEC_BATCH, DEC_SEQ, D_MODEL)
    return (y_prompt, y_sample, new_ret, new_gla)
```

```python
import functools

import jax
import jax.numpy as jnp
from jax import lax
from jax.experimental import pallas as pl
from jax.experimental.pallas import tpu as pltpu
from jax.experimental.pallas import tpu_sc as plsc

F32 = jnp.float32
BF16 = jnp.bfloat16

D_MODEL = 1024
BATCH = 16
SEQ = 256
DEPTH = 4
DEC_BATCH = 4
DEC_SEQ = 4096
GRID_W = 64
RET_HEADS = 4
RET_DK = 64
RET_DV = 128
RET_CHUNK = 128
CONV_CH = 512
CONV_WIDTH = 31
CONV_PAD = CONV_WIDTH // 2
GLA_HEADS = 4
GLA_DK = 128
GLA_DV = 256
GLA_RANK = 16
GLA_TAU = 16.0
GLA_CHUNK = 64
GLA_SUB = 16
N_EXPERTS = 32
TOP_K = 4
D_FF = 1024
SWIGLU_LIMIT = 7.0
SWIGLU_ALPHA = 1.702
ROPE_THETA = 10000.0
EPS = 1e-6
N_MOD = 6

RB = 256
NT_PROMPT = BATCH * SEQ
NT = NT_PROMPT + DEC_BATCH * DEC_SEQ
NBLK = NT // RB
N_PROMPT_BLK = NT_PROMPT // RB
SAMPLE_BLK = DEC_SEQ // RB
MOD_ROWS = 8
HALO = 16
VMEM_LIMIT = 48 * 1024 * 1024

assert SEQ == RB and DEC_SEQ % RB == 0 and CONV_PAD < HALO


def _seq_of_block(i):
    return jnp.where(i < N_PROMPT_BLK, i, N_PROMPT_BLK + (i - N_PROMPT_BLK) // SAMPLE_BLK)


def _is_first_block(i):
    return jnp.logical_or(i < N_PROMPT_BLK, (i - N_PROMPT_BLK) % SAMPLE_BLK == 0)


def _is_last_block(i):
    return jnp.logical_or(i < N_PROMPT_BLK, (i - N_PROMPT_BLK) % SAMPLE_BLK == SAMPLE_BLK - 1)


def _mod_row(i):
    return jnp.where(i < N_PROMPT_BLK, 0, 1 + (i - N_PROMPT_BLK) // SAMPLE_BLK)


def _rope_block(i):
    return jnp.where(i < N_PROMPT_BLK, 0, 1 + (i - N_PROMPT_BLK) % SAMPLE_BLK)


def _dot(a, b):
    return jnp.dot(a, b, preferred_element_type=F32)


def _dot_nt(a, b):
    return lax.dot_general(a, b, (((1,), (1,)), ((), ())), preferred_element_type=F32)


def _dot_tn(a, b):
    return lax.dot_general(a, b, (((0,), (0,)), ((), ())), preferred_element_type=F32)


def _split2(a):
    hi = a.astype(BF16)
    lo = (a - hi.astype(F32)).astype(BF16)
    return hi, lo


def _dot_hi(a, b):
    a_hi, a_lo = _split2(a)
    b_hi, b_lo = _split2(b)
    return _dot(a_hi, b_hi) + (_dot(a_hi, b_lo) + _dot(a_lo, b_hi))


def _silu(x):
    return x * (1.0 / (1.0 + jnp.exp(-x)))


def _sigmoid(x):
    return 1.0 / (1.0 + jnp.exp(-x))


def _pack_rows(x):
    n = x.shape[1] // 2
    lo = pltpu.bitcast(x[:, :n].astype(BF16).astype(F32), jnp.uint32)
    hi = pltpu.bitcast(x[:, n:].astype(BF16).astype(F32), jnp.uint32)
    return hi | (lo >> 16)


def _unpack_rows(u):
    lo = pltpu.bitcast(u << 16, F32)
    hi = pltpu.bitcast(u & jnp.uint32(0xFFFF0000), F32)
    return lo, hi


def _params(n_axes=1, vmem=VMEM_LIMIT):
    return pltpu.CompilerParams(dimension_semantics=("arbitrary",) * n_axes, vmem_limit_bytes=vmem)


MOD_TN = 1536


def _mod_kernel(c_ref, w_ref, b_ref, o_ref):
    s = _silu(c_ref[...]).astype(BF16)
    o_ref[...] = _dot(s, w_ref[...].astype(BF16)) + b_ref[...]


def _modulation(cvec, w_mod, b_mod):
    n = N_MOD * D_MODEL
    return pl.pallas_call(
        _mod_kernel,
        grid=(DEPTH, n // MOD_TN),
        in_specs=[pl.BlockSpec((MOD_ROWS, D_MODEL), lambda l, j: (0, 0)),
                  pl.BlockSpec((None, D_MODEL, MOD_TN), lambda l, j: (l, 0, j)),
                  pl.BlockSpec((None, 1, MOD_TN), lambda l, j: (l, 0, j))],
        out_specs=pl.BlockSpec((None, MOD_ROWS, MOD_TN), lambda l, j: (l, 0, j)),
        out_shape=jax.ShapeDtypeStruct((DEPTH, MOD_ROWS, n), F32),
        compiler_params=_params(2),
        name="modulation",
    )(cvec, w_mod, b_mod.reshape(DEPTH, 1, n))


def _norm_mod(x, g_ref, shift_ref, scale_ref, row):
    y = x * lax.rsqrt(jnp.mean(x * x, axis=-1, keepdims=True) + EPS) * g_ref[...]
    return y * (1.0 + scale_ref[pl.ds(row, 1), :]) + shift_ref[pl.ds(row, 1), :]


N_SRC = {"split": 2, "moe": 3 + TOP_K}


def _inproj_kernel(*refs, widths, source):
    n_src = N_SRC[source]
    src = refs[:n_src]
    g_ref, shift_ref, scale_ref, w_ref = refs[n_src:n_src + 4]
    outs, w_bf = refs[n_src + 4:-1], refs[-1]
    step = pl.program_id(0)

    @pl.when(step == 0)
    def _():
        w_bf[...] = w_ref[...].astype(BF16)

    row = _mod_row(step)
    if source == "split":
        x = jnp.where(step < N_PROMPT_BLK, src[0][...], src[1][...])
    else:
        x1_ref, yg_refs, gt_ref, gate_ref = src[0], src[1:1 + TOP_K], src[-2], src[-1]
        x = jnp.concatenate(_combined_rows(x1_ref, yg_refs, _gate_columns(gt_ref), gate_ref, row), axis=1)
    outs[0][...] = x
    hb = _norm_mod(x, g_ref, shift_ref, scale_ref, row).astype(BF16)
    off = 0
    for o_ref, width in zip(outs[1:], widths):
        o_ref[...] = _dot(hb, w_bf[:, off:off + width])
        off += width


def _inproj(x, g, shift, scale, w_all, index, widths):
    source, x_args = x[0], list(x[1:])
    n_in = w_all.shape[2]
    row_spec = lambda width: pl.BlockSpec((RB, width), lambda i: (i, 0))
    full = lambda shape: pl.BlockSpec(shape, lambda i: (0,) * len(shape))
    if source == "split":
        x_specs = [pl.BlockSpec((RB, D_MODEL), lambda i: (jnp.minimum(i, N_PROMPT_BLK - 1), 0)),
                   pl.BlockSpec((RB, D_MODEL), lambda i: (jnp.maximum(i - N_PROMPT_BLK, 0), 0))]
    else:
        x1, yg, gates_t, gate = x_args
        x_specs, x_args = _combine_specs(), [x1] + [yg] * TOP_K + [gates_t, gate]
    out_widths = (D_MODEL,) + tuple(widths)
    return pl.pallas_call(
        functools.partial(_inproj_kernel, widths=widths, source=source),
        grid=(NBLK,),
        in_specs=x_specs + [full((1, D_MODEL)), full((MOD_ROWS, D_MODEL)), full((MOD_ROWS, D_MODEL)),
                            pl.BlockSpec((None, D_MODEL, n_in), lambda i: (index, 0, 0))],
        out_specs=[row_spec(width) for width in out_widths],
        out_shape=[jax.ShapeDtypeStruct((NT, width), F32) for width in out_widths],
        scratch_shapes=[pltpu.VMEM((D_MODEL, n_in), BF16)],
        compiler_params=_params(),
        name="inproj",
    )(*x_args, g.reshape(1, D_MODEL), shift, scale, w_all)


RC = RET_CHUNK
RET_PAIR = 2 * RET_DK


def _rope(x, cos, sin_signed):
    lane = lax.broadcasted_iota(jnp.int32, x.shape, 1)
    swapped = jnp.where(lane % 2 == 0, pltpu.roll(x, x.shape[1] - 1, 1), pltpu.roll(x, 1, 1))
    return x * cos + swapped * sin_signed


def _ret_kernel(*refs, reverse):
    for _ in _ret_stages(*refs, reverse=reverse):
        pass


def _ret_stages(decay_ref, q_ref, k_ref, v_ref, cos_ref, sin_ref, s0_ref, acc_ref, *rest, reverse):
    del acc_ref
    if reverse:
        of_ref, g_ref, gn_ref, o_ref, sfin_ref, st_ref, dm_ref, dq_ref, dk_ref, ds_ref = rest
    else:
        o_ref, sfin_ref, st_ref, dm_ref, dq_ref, dk_ref, ds_ref = rest
    step = pl.program_id(0)
    blk = NBLK - 1 - step if reverse else step
    direction = 1 if reverse else 0

    @pl.when(step == 0)
    def _():
        row = lax.broadcasted_iota(jnp.int32, (RC, RC), 0).astype(F32)
        col = lax.broadcasted_iota(jnp.int32, (RC, RC), 1).astype(F32)
        for h in range(RET_HEADS):
            lg = -jnp.exp(jnp.full((RC, RC), decay_ref[direction, h], F32))
            if reverse:
                diff = col - row
                mask = diff > 0
                q_pow = RC - row
                k_pow = row
            else:
                diff = row - col
                mask = diff >= 0
                q_pow = row + 1.0
                k_pow = RC - 1.0 - row
            dm_ref[h] = jnp.where(mask, jnp.exp(lg * jnp.where(mask, diff, 0.0)), 0.0)
            dq_ref[h] = jnp.exp(lg * q_pow)
            dk_ref[h] = jnp.exp(lg * k_pow)
            ds_ref[h] = jnp.exp(lg * RC)

    starts = _is_last_block(blk) if reverse else _is_first_block(blk)

    is_prompt = blk < N_PROMPT_BLK

    @pl.when(starts)
    def _():
        st_ref[...] = jnp.zeros_like(st_ref)

    @pl.when(jnp.logical_and(starts, jnp.logical_not(is_prompt)))
    def _():
        for h in range(RET_HEADS):
            off = (h % 2) * RET_DK
            st_ref[h, off:off + RET_DK, :] = s0_ref[h]

    lane = lax.broadcasted_iota(jnp.int32, (1, RET_PAIR), 1)
    chunks = range(RB // RC)
    chunk_order = list(reversed(chunks) if reverse else chunks)
    units = [(h, c) for h in range(RET_HEADS) for c in chunk_order]
    rows_of = lambda c: slice(c * RC, (c + 1) * RC)
    vcols = lambda h: slice(h * RET_DV, (h + 1) * RET_DV)
    roped = {}
    for p in range(RET_HEADS // 2):
        cols = slice(p * RET_PAIR, (p + 1) * RET_PAIR)
        for c in chunk_order:
            rows = rows_of(c)
            cos, sin = cos_ref[rows, :], sin_ref[rows, :]
            roped[p, c] = (_rope(q_ref[rows, cols], cos, sin),
                           _rope(k_ref[rows, cols] * (RET_DK ** -0.5), cos, sin))
    yield
    qm_u, vh_u, att_u, kd_u = {}, {}, {}, {}
    for h, c in units:
        head_mask = (lane // RET_DK == h % 2).astype(F32)
        q2, k2 = roped[h // 2, c]
        vh_u[h, c] = v_ref[rows_of(c), vcols(h)].astype(BF16)
        qm_u[h, c] = (q2 * head_mask).astype(BF16)
        km = k2 * head_mask
        att_u[h, c] = (_dot_nt(qm_u[h, c], km.astype(BF16)) * dm_ref[h]).astype(BF16)
        kd_u[h, c] = (km * dk_ref[h]).astype(BF16)
    yield
    intra_u = {u: _dot(att_u[u], vh_u[u]) for u in units}
    delta_u = {u: _dot_tn(kd_u[u], vh_u[u]) for u in units}
    yield
    state_u = {}
    for h in range(RET_HEADS):
        st = st_ref[h]
        for c in chunk_order:
            state_u[h, c] = st.astype(BF16)
            st = st * ds_ref[h] + delta_u[h, c]
        st_ref[h] = st
    yield
    for h, c in units:
        rows, out_cols = rows_of(c), vcols(h)
        o = intra_u[h, c] + _dot(qm_u[h, c], state_u[h, c]) * dq_ref[h]
        if reverse:
            o = o + of_ref[rows, out_cols]
            o = o * lax.rsqrt(jnp.mean(o * o, axis=-1, keepdims=True) + EPS)
            o = o * gn_ref[:, out_cols] * _silu(g_ref[rows, out_cols])
            o_ref[rows, out_cols] = o.astype(o_ref.dtype)
        else:
            o_ref[rows, out_cols] = o

    ends = _is_first_block(blk) if reverse else _is_last_block(blk)

    @pl.when(jnp.logical_and(ends, is_prompt))
    def _():
        for h in range(RET_HEADS):
            off = (h % 2) * RET_DK
            sfin_ref[h] = st_ref[h, off:off + RET_DK, :]


def _sample_seq(blk):
    return jnp.clip(_seq_of_block(blk) - BATCH, 0, DEC_BATCH - 1)


def _prompt_seq(blk):
    return jnp.minimum(_seq_of_block(blk), BATCH - 1)


def _retention(decay, q, k, v, cos_tab, sin_tab, state_in, state_out, layer, *, reverse,
               o_fwd=None, g=None, gn=None, conv=None, u=None, outproj=None):
    bmap = (lambda j: NBLK - 1 - j) if reverse else (lambda j: j)
    direction = 1 if reverse else 0
    qd, vd = RET_HEADS * RET_DK, RET_HEADS * RET_DV
    row_spec = lambda width: pl.BlockSpec((RB, width), lambda j: (bmap(j), 0))
    state_blk = (None, None, None, RET_HEADS, RET_DK, RET_DV)
    in_specs = [pl.BlockSpec(memory_space=pltpu.SMEM), row_spec(qd), row_spec(qd), row_spec(vd),
                pl.BlockSpec((RB, RET_PAIR), lambda j: (_rope_block(bmap(j)), 0)),
                pl.BlockSpec((RB, RET_PAIR), lambda j: (_rope_block(bmap(j)), 0)),
                pl.BlockSpec(state_blk, lambda j: (_sample_seq(bmap(j)), layer, direction, 0, 0, 0)),
                pl.BlockSpec(memory_space=pl.ANY)]
    args = [decay, q, k, v, cos_tab, sin_tab, state_in, state_out]
    if reverse:
        in_specs += [row_spec(vd), row_spec(vd), pl.BlockSpec((1, vd), lambda j: (0, 0))]
        args += [o_fwd, g, gn.reshape(1, vd)]
    tile = pltpu.VMEM((RET_HEADS, RC, RC), F32)
    out_specs = [row_spec(vd),
                 pl.BlockSpec(state_blk, lambda j: (_prompt_seq(bmap(j)), layer, direction, 0, 0, 0))]
    out_shape = [jax.ShapeDtypeStruct((NT, vd), BF16 if reverse else F32),
                 jax.ShapeDtypeStruct(state_out.shape, F32)]
    scratch = [tile, tile, tile, tile, tile]
    body = functools.partial(_ret_kernel, reverse=reverse)
    if conv is not None:
        assert not reverse
        n_ret = (len(in_specs), len(out_specs), len(scratch))
        c_in, c_args, c_out, c_shape, c_scratch = _conv_parts(*conv)
        in_specs, args = in_specs + c_in, args + c_args
        out_specs, out_shape, scratch = out_specs + c_out, out_shape + c_shape, scratch + c_scratch
        body = functools.partial(_ret_conv_kernel, n_ret=n_ret, n_conv=(len(c_in), len(c_out), len(c_scratch)))
    state_output = 1
    if outproj is not None:
        assert reverse
        call = _with_outproj(body, in_specs, args, out_specs, out_shape, scratch, [u], outproj)
        body, in_specs, args = call["kernel"], call["in_specs"], call["args"]
        out_specs, out_shape, scratch = call["out_specs"], call["out_shape"], call["scratch"]
        state_output = 0
    return pl.pallas_call(
        body,
        grid=(NBLK,),
        in_specs=in_specs,
        out_specs=out_specs,
        out_shape=out_shape,
        input_output_aliases={7: state_output},
        scratch_shapes=scratch,
        compiler_params=_params(),
        name="retention_bwd" if reverse else "retention_fwd",
    )(*args)


def _ret_conv_kernel(*refs, n_ret, n_conv):
    (ri, ro, rs), (ci, co, cs) = n_ret, n_conv
    ins, outs, scr = refs[:ri + ci], refs[ri + ci:ri + ci + ro + co], refs[ri + ci + ro + co:]
    bodies = [_conv_stages(*ins[ri:], *outs[ro:], *scr[rs:]),
              _ret_stages(*ins[:ri], *outs[:ro], *scr[:rs], reverse=False)]
    while bodies:
        for body in list(bodies):
            if next(body, StopIteration) is StopIteration:
                bodies.remove(body)


CONV_RT = 32
CONV_CT = 128
CONV_SPAN = RB + 2 * HALO - 8


def _conv_stages(a_ref, ga_ref, ap_ref, gap_ref, an_ref, gan_ref, cw_ref, cb_ref, lng_ref, lnb_ref,
                 o_ref, u_ref, y_ref, us_ref):
    blk = pl.program_id(0)
    keep_prev = jnp.where(_is_first_block(blk), 0.0, 1.0)
    keep_next = jnp.where(_is_last_block(blk), 0.0, 1.0)
    u_ref[0:HALO, :] = ap_ref[...] * _sigmoid(gap_ref[...]) * keep_prev
    u_ref[HALO:HALO + RB, :] = a_ref[...] * _sigmoid(ga_ref[...])
    u_ref[HALO + RB:HALO + RB + HALO, :] = an_ref[...] * _sigmoid(gan_ref[...]) * keep_next
    for r in range(1, 8):
        us_ref[r - 1] = u_ref[r:r + CONV_SPAN, :]
    for ct in range(CONV_CH // CONV_CT):
        yield
        cols = slice(ct * CONV_CT, (ct + 1) * CONV_CT)
        for rt in range(RB // CONV_RT):
            acc = jnp.zeros((CONV_RT, CONV_CT), F32)
            for w in range(CONV_WIDTH):
                tiles, r = divmod(HALO - CONV_PAD + w, 8)
                base = rt * CONV_RT + 8 * tiles
                src = u_ref if r == 0 else us_ref.at[r - 1]
                acc = acc + src[base:base + CONV_RT, cols] * cw_ref[w:w + 1, cols]
            y_ref[rt * CONV_RT:(rt + 1) * CONV_RT, cols] = acc + cb_ref[:, cols]
    y = y_ref[...]
    mu = jnp.mean(y, axis=-1, keepdims=True)
    var = jnp.mean(jnp.square(y - mu), axis=-1, keepdims=True)
    o_ref[...] = _silu((y - mu) * lax.rsqrt(var + EPS) * lng_ref[...] + lnb_ref[...]).astype(o_ref.dtype)


def _conv_parts(a, ga, cw, cb, lng, lnb):
    per_blk = RB // HALO
    n_halo = NT // HALO
    row_spec = pl.BlockSpec((RB, CONV_CH), lambda i: (i, 0))
    prev_spec = pl.BlockSpec((HALO, CONV_CH), lambda i: (jnp.maximum(i * per_blk - 1, 0), 0))
    next_spec = pl.BlockSpec((HALO, CONV_CH), lambda i: (jnp.minimum((i + 1) * per_blk, n_halo - 1), 0))
    vec = pl.BlockSpec((1, CONV_CH), lambda i: (0, 0))
    in_specs = [row_spec, row_spec, prev_spec, prev_spec, next_spec, next_spec,
                pl.BlockSpec((CONV_WIDTH, CONV_CH), lambda i: (0, 0)), vec, vec, vec]
    args = [a, ga, a, ga, a, ga, cw, cb.reshape(1, -1), lng.reshape(1, -1), lnb.reshape(1, -1)]
    scratch = [pltpu.VMEM((RB + 2 * HALO, CONV_CH), F32), pltpu.VMEM((RB, CONV_CH), F32),
               pltpu.VMEM((7, CONV_SPAN, CONV_CH), F32)]
    return in_specs, args, [row_spec], [jax.ShapeDtypeStruct((NT, CONV_CH), BF16)], scratch


GC = GLA_CHUNK
GLA_NSUB = GC // GLA_SUB


def _split3(a):
    p1 = a.astype(BF16)
    r1 = a - p1.astype(F32)
    p2 = r1.astype(BF16)
    p3 = (r1 - p2.astype(F32)).astype(BF16)
    return p1, p2, p3


def _gla_kernel(alr_ref, q_ref, k_ref, v_ref, wa_ref, ba_ref, s0_ref, acc_ref, *rest, reverse):
    del acc_ref
    if reverse:
        of_ref, r_ref, gn_ref, o_ref, sfin_ref, st_ref, b_ref, tri_ref, ob_ref = rest
    else:
        o_ref, sfin_ref, st_ref, b_ref, tri_ref = rest
        ob_ref = o_ref
    step = pl.program_id(0)
    blk = NBLK - 1 - step if reverse else step
    starts = _is_last_block(blk) if reverse else _is_first_block(blk)
    is_prompt = blk < N_PROMPT_BLK

    @pl.when(jnp.logical_and(starts, is_prompt))
    def _():
        st_ref[...] = jnp.zeros_like(st_ref)

    @pl.when(jnp.logical_and(starts, jnp.logical_not(is_prompt)))
    def _():
        for h in range(GLA_HEADS):
            st_ref[h] = s0_ref[h].T

    @pl.when(step == 0)
    def _():
        row = lax.broadcasted_iota(jnp.int32, (RB, RB), 0)
        col = lax.broadcasted_iota(jnp.int32, (RB, RB), 1)
        ordered = col >= row if reverse else col <= row
        tri_ref[...] = jnp.where(jnp.logical_and(row // GC == col // GC, ordered), 1.0, 0.0).astype(BF16)

    heads = range(GLA_HEADS)
    hcols = [slice(h * GLA_DK, (h + 1) * GLA_DK) for h in heads]
    alr = alr_ref[...]
    z = [_dot_hi(alr, wa_ref[:, hcols[h]]) + ba_ref[:, hcols[h]] for h in heads]
    log_a = [(jnp.minimum(zh, 0.0) - jnp.log(1.0 + jnp.exp(-jnp.abs(zh)))) * (1.0 / GLA_TAU) for zh in z]
    parts = [_split3(la) for la in log_a]
    tri = tri_ref[...]
    for h in heads:
        g1, g2, g3 = parts[h]
        b_ref[:, hcols[h]] = _dot(tri, g1) + (_dot(tri, g2) + _dot(tri, g3))

    c_row = lax.broadcasted_iota(jnp.int32, (GC, 1), 0)
    a_row = lax.broadcasted_iota(jnp.int32, (GC, GC), 0)
    a_col = lax.broadcasted_iota(jnp.int32, (GC, GC), 1)
    att_mask = a_col > a_row if reverse else a_col <= a_row
    chunks = range(RB // GC)
    chunk_order = list(reversed(chunks) if reverse else chunks)
    units = [(h, c) for h in range(GLA_HEADS) for c in chunk_order]
    kcols = lambda h: slice(h * GLA_DK, (h + 1) * GLA_DK)
    vcols = lambda h: slice(h * GLA_DV, (h + 1) * GLA_DV)
    rows_of = lambda c: slice(c * GC, (c + 1) * GC)

    vh_u, qe_u, ke_u, decay_u, att_u = {}, {}, {}, {}, {}
    for h, c in units:
        rows = rows_of(c)
        b = b_ref[rows, kcols(h)]
        qh = q_ref[rows, kcols(h)] * (GLA_DK ** -0.5)
        kh = k_ref[rows, kcols(h)]
        vh_u[h, c] = v_ref[rows, vcols(h)].astype(BF16)
        edge = b[0:1, :] if reverse else b[GC - 1:GC, :]
        bounds = []
        for s in range(GLA_NSUB):
            if reverse:
                hi = (s + 1) * GLA_SUB
                bounds.append(b[hi:hi + 1, :] if s < GLA_NSUB - 1 else jnp.zeros((1, GLA_DK), F32))
            else:
                lo = s * GLA_SUB
                bounds.append(b[lo - 1:lo, :] if s > 0 else jnp.zeros((1, GLA_DK), F32))
        own = jnp.concatenate([jnp.broadcast_to(bd, (GLA_SUB, GLA_DK)) for bd in bounds], axis=0)
        q_own = qh * jnp.exp(b - own)
        q_parts, k_parts = [], []
        for s, bd in enumerate(bounds):
            q_parts.append(jnp.where(c_row // GLA_SUB == s, q_own, 0.0))
            reach = c_row >= s * GLA_SUB if reverse else c_row < (s + 1) * GLA_SUB
            k_parts.append(kh * jnp.exp(jnp.where(reach, bd - b, -jnp.inf)))
        q_bd = jnp.concatenate(q_parts, axis=1).astype(BF16)
        k_cat = jnp.concatenate(k_parts, axis=1).astype(BF16)
        att_u[h, c] = jnp.where(att_mask, _dot_nt(q_bd, k_cat), 0.0).astype(BF16)
        qe_u[h, c] = (qh * jnp.exp(b)).astype(BF16)
        ke_u[h, c] = (kh * jnp.exp(edge - b)).astype(BF16)
        decay_u[h, c] = jnp.exp(edge)
    intra_u = {u: _dot(att_u[u], vh_u[u]) for u in units}
    delta_u = {u: _dot_tn(vh_u[u], ke_u[u]) for u in units}
    state_u = {}
    for h in range(GLA_HEADS):
        st = st_ref[h]
        for c in chunk_order:
            state_u[h, c] = st.astype(BF16)
            st = st * decay_u[h, c] + delta_u[h, c]
        st_ref[h] = st
    for h, c in units:
        ob_ref[rows_of(c), vcols(h)] = intra_u[h, c] + _dot_nt(qe_u[h, c], state_u[h, c])

    if reverse:
        for h in range(GLA_HEADS):
            cols = vcols(h)
            o = ob_ref[:, cols] + of_ref[:, cols]
            o = o * lax.rsqrt(jnp.mean(o * o, axis=-1, keepdims=True) + EPS)
            o_ref[:, cols] = (o * gn_ref[:, cols] * _silu(r_ref[:, cols])).astype(o_ref.dtype)

    ends = _is_first_block(blk) if reverse else _is_last_block(blk)

    @pl.when(jnp.logical_and(ends, is_prompt))
    def _():
        for h in range(GLA_HEADS):
            sfin_ref[h] = st_ref[h].T


def _gla(alr, q, k, v, wa, ba, state_in, state_out, layer, *, reverse, o_fwd=None, r=None, gn=None,
         outproj=None):
    bmap = (lambda j: NBLK - 1 - j) if reverse else (lambda j: j)
    direction = 1 if reverse else 0
    qd, vd = GLA_HEADS * GLA_DK, GLA_HEADS * GLA_DV
    row_spec = lambda width: pl.BlockSpec((RB, width), lambda j: (bmap(j), 0))
    state_blk = (None, None, None, GLA_HEADS, GLA_DK, GLA_DV)
    in_specs = [row_spec(2 * GLA_RANK), row_spec(qd), row_spec(qd), row_spec(vd),
                pl.BlockSpec((2 * GLA_RANK, qd), lambda j: (0, 0)),
                pl.BlockSpec((1, qd), lambda j: (0, 0)),
                pl.BlockSpec(state_blk, lambda j: (_sample_seq(bmap(j)), layer, direction, 0, 0, 0)),
                pl.BlockSpec(memory_space=pl.ANY)]
    args = [alr, q, k, v, wa, ba, state_in, state_out]
    scratch = [pltpu.VMEM((GLA_HEADS, GLA_DV, GLA_DK), F32), pltpu.VMEM((RB, qd), F32),
               pltpu.VMEM((RB, RB), BF16)]
    if reverse:
        in_specs += [row_spec(vd), row_spec(vd), pl.BlockSpec((1, vd), lambda j: (0, 0))]
        args += [o_fwd, r, gn.reshape(1, vd)]
        scratch += [pltpu.VMEM((RB, vd), F32)]
    call = dict(kernel=functools.partial(_gla_kernel, reverse=reverse), in_specs=in_specs, args=args,
                out_specs=[row_spec(vd),
                           pl.BlockSpec(state_blk, lambda j: (_prompt_seq(bmap(j)), layer, direction, 0, 0, 0))],
                out_shape=[jax.ShapeDtypeStruct((NT, vd), BF16 if reverse else F32),
                           jax.ShapeDtypeStruct(state_out.shape, F32)],
                scratch=scratch)
    state_output = 1
    if outproj is not None:
        assert reverse
        call = _with_outproj(call["kernel"], in_specs, args, call["out_specs"], call["out_shape"], scratch,
                             [], outproj)
        state_output = 0
    return pl.pallas_call(
        call["kernel"],
        grid=(NBLK,),
        in_specs=call["in_specs"],
        out_specs=call["out_specs"],
        out_shape=call["out_shape"],
        input_output_aliases={7: state_output},
        scratch_shapes=call["scratch"],
        compiler_params=_params(),
        name="gla_bwd" if reverse else "gla_fwd",
    )(*call["args"])


OUT_SUB = 128


def _outproj_kernel(*refs, n_mix):
    mix_refs = refs[:n_mix]
    (w_ref, x_ref, gate_ref, g2_ref, shift_ref, scale_ref, rw_ref, rb_ref,
     x1_ref, h2_ref, logit_ref, w_bf) = refs[n_mix:]
    step = pl.program_id(0)

    @pl.when(step == 0)
    def _():
        w_bf[...] = w_ref[...].astype(BF16)

    row = _mod_row(NBLK - 1 - step)
    r_hi, r_lo = _split2(rw_ref[...])
    groups = [slice(p * OUT_SUB, (p + 1) * OUT_SUB) for p in range(RB // OUT_SUB)]
    mixed = []
    for rows in groups:
        m, off = None, 0
        for mix_ref in mix_refs:
            width = mix_ref.shape[1]
            part = _dot(mix_ref[rows, :], w_bf[off:off + width, :])
            m = part if m is None else m + part
            off += width
        mixed.append(m)
    normed = []
    for rows, m in zip(groups, mixed):
        x1 = x_ref[rows, :] + gate_ref[pl.ds(row, 1), :] * m
        x1_ref[rows, :] = x1
        h2 = _norm_mod(x1, g2_ref, shift_ref, scale_ref, row)
        h2_ref[rows, :] = _pack_rows(h2)
        normed.append(_split2(h2))
    for rows, (h_hi, h_lo) in zip(groups, normed):
        logit_ref[:, rows] = _dot_nt(r_hi, h_hi) + (_dot_nt(r_hi, h_lo) + _dot_nt(r_lo, h_hi)) + rb_ref[...]


def _with_outproj(scan_body, in_specs, args, out_specs, out_shape, scratch, ext_mixes, outproj):
    w_all, index, x, gate, g2, shift, scale, rw, rb = outproj
    blk = lambda j: NBLK - 1 - j
    row_spec = lambda width: pl.BlockSpec((RB, width), lambda j: (blk(j), 0))
    full = lambda shape: pl.BlockSpec(shape, lambda j: (0,) * len(shape))
    mod_spec = full((MOD_ROWS, D_MODEL))
    n_mixed = w_all.shape[1]
    o_in = ([row_spec(m.shape[1]) for m in ext_mixes]
            + [pl.BlockSpec((None, n_mixed, D_MODEL), lambda j: (index, 0, 0)),
               row_spec(D_MODEL), mod_spec, full((1, D_MODEL)), mod_spec, mod_spec,
               full((N_EXPERTS, D_MODEL)), full((N_EXPERTS, 1))])
    o_args = list(ext_mixes) + [w_all, x, gate, g2.reshape(1, D_MODEL), shift, scale, rw.T,
                                rb.reshape(N_EXPERTS, 1)]
    o_out = [row_spec(D_MODEL), row_spec(D_MODEL // 2), pl.BlockSpec((N_EXPERTS, RB), lambda j: (0, blk(j)))]
    o_shape = [jax.ShapeDtypeStruct((NT, D_MODEL), F32), jax.ShapeDtypeStruct((NT, D_MODEL // 2), jnp.uint32),
               jax.ShapeDtypeStruct((N_EXPERTS, NT), F32)]
    n_si, n_oi, n_oo, n_ss = len(in_specs), len(o_in), len(o_out), len(scratch)
    mix = out_shape[0]

    def fused(*refs):
        scan_in, out_in = refs[:n_si], refs[n_si:n_si + n_oi]
        sfin_ref, out_out = refs[n_si + n_oi], refs[n_si + n_oi + 1:n_si + n_oi + 1 + n_oo]
        rest = refs[n_si + n_oi + 1 + n_oo:]
        scan_scr, mix_ref, w_bf = rest[:n_ss], rest[n_ss], rest[n_ss + 1]
        scan_body(*scan_in, mix_ref, sfin_ref, *scan_scr)
        _outproj_kernel(mix_ref, *out_in, *out_out, w_bf, n_mix=1 + len(ext_mixes))

    return dict(kernel=fused, in_specs=in_specs + o_in, args=args + o_args,
                out_specs=out_specs[1:] + o_out, out_shape=out_shape[1:] + o_shape,
                scratch=scratch + [pltpu.VMEM((RB, mix.shape[1]), mix.dtype),
                                   pltpu.VMEM((n_mixed, D_MODEL), BF16)])


ROUTE_BLK = 2048
ROUTE_SUB = 256


def _route_kernel(lg_ref, idx_ref, rank_ref, gt_ref, cnt_ref, carry_ref):
    @pl.when(pl.program_id(0) == 0)
    def _():
        carry_ref[...] = jnp.zeros_like(carry_ref)

    logits = lg_ref[...]
    eid = lax.broadcasted_iota(jnp.int32, logits.shape, 0).astype(F32)
    work = logits
    onehots, top_vals = [], []
    for kk in range(TOP_K):
        top = jnp.max(work, axis=0, keepdims=True)
        first = jnp.min(jnp.where(work == top, eid, float(N_EXPERTS)), axis=0, keepdims=True)
        onehot = eid == first
        idx_ref[kk:kk + 1, :] = first.astype(jnp.int32)
        onehots.append(onehot)
        top_vals.append(top)
        work = jnp.where(onehot, -jnp.inf, work)
    exps = [jnp.exp(v - top_vals[0]) for v in top_vals]
    denom = exps[0]
    for e in exps[1:]:
        denom = denom + e
    gt_ref[...] = jnp.zeros_like(gt_ref)
    for kk in range(TOP_K):
        gt_ref[kk:kk + 1, :] = exps[kk] / denom

    sel = jnp.zeros(logits.shape, F32)
    for onehot in onehots:
        sel = sel + jnp.where(onehot, 1.0, 0.0)
    sel = sel.astype(BF16)
    r_i = lax.broadcasted_iota(jnp.int32, (ROUTE_SUB, ROUTE_SUB), 0)
    c_i = lax.broadcasted_iota(jnp.int32, (ROUTE_SUB, ROUTE_SUB), 1)
    before = jnp.where(r_i < c_i, 1.0, 0.0).astype(BF16)
    ones = jnp.ones((ROUTE_SUB, ROUTE_SUB), BF16)
    carry = carry_ref[...]
    for s in range(ROUTE_BLK // ROUTE_SUB):
        cols = slice(s * ROUTE_SUB, (s + 1) * ROUTE_SUB)
        pos = _dot(sel[:, cols], before) + carry
        for kk in range(TOP_K):
            rank = jnp.sum(jnp.where(onehots[kk][:, cols], pos, 0.0), axis=0, keepdims=True)
            rank_ref[kk:kk + 1, cols] = rank.astype(jnp.int32)
        carry = carry + _dot(sel[:, cols], ones)
    carry_ref[...] = carry
    cnt_ref[...] = carry.astype(jnp.int32)


def _route(logits_t):
    col_spec = lambda rows: pl.BlockSpec((rows, ROUTE_BLK), lambda i: (0, i))
    return pl.pallas_call(
        _route_kernel,
        grid=(NT // ROUTE_BLK,),
        in_specs=[col_spec(N_EXPERTS)],
        out_specs=[col_spec(TOP_K), col_spec(TOP_K), col_spec(8),
                   pl.BlockSpec((N_EXPERTS, ROUTE_SUB), lambda i: (0, 0))],
        out_shape=[jax.ShapeDtypeStruct((TOP_K, NT), jnp.int32), jax.ShapeDtypeStruct((TOP_K, NT), jnp.int32),
                   jax.ShapeDtypeStruct((8, NT), F32), jax.ShapeDtypeStruct((N_EXPERTS, ROUTE_SUB), jnp.int32)],
        scratch_shapes=[pltpu.VMEM((N_EXPERTS, ROUTE_SUB), F32)],
        compiler_params=_params(),
        name="route",
    )(logits_t)


TM = 1024
TM_SUB = 256
MOE_NBLK = NT * TOP_K // TM + N_EXPERTS
MOE_ROWS = MOE_NBLK * TM
HALF = D_MODEL // 2


def _moe_kernel(be_ref, nv_ref, nx_ref, x_ref, wgu_hbm, bgu_ref, wd_hbm, bd_ref, y_ref,
                wgu_st, wd_st, wgu_bf, wd_bf, sems, *, layer):
    i = pl.program_id(0)
    n_valid = nv_ref[i]

    def fetch(e):
        return (pltpu.make_async_copy(wgu_hbm.at[layer, e], wgu_st, sems.at[0]),
                pltpu.make_async_copy(wd_hbm.at[layer, e], wd_st, sems.at[1]))

    @pl.when(i == 0)
    def _():
        for cp in fetch(be_ref[0]):
            cp.start()

    @pl.when(n_valid > 0)
    def _():
        e = be_ref[i]
        changed = jnp.logical_or(i == 0, e != be_ref[jnp.maximum(i - 1, 0)])

        @pl.when(changed)
        def _():
            for cp in fetch(e):
                cp.wait()
            wgu_bf[...] = wgu_st[...].astype(BF16)
            wd_bf[...] = wd_st[...].astype(BF16)
            nxt = nx_ref[e]

            @pl.when(nxt >= 0)
            def _():
                for cp in fetch(nxt):
                    cp.start()

    def ffn_pass(p, masked):
        rows = slice(p * TM_SUB, (p + 1) * TM_SUB)
        x = x_ref[rows, :]
        if masked:
            row_id = lax.broadcasted_iota(jnp.int32, (TM_SUB, 1), 0) + p * TM_SUB
            x = jnp.where(row_id < n_valid, x, jnp.uint32(0))
        x_lo, x_hi = _unpack_rows(x)
        x_lo, x_hi = x_lo.astype(BF16), x_hi.astype(BF16)
        yield
        gu = _dot(x_lo, wgu_bf[:HALF, :]) + _dot(x_hi, wgu_bf[HALF:, :]) + bgu_ref[...]
        yield
        gate = jnp.minimum(gu[:, :D_FF], SWIGLU_LIMIT)
        up = jnp.clip(gu[:, D_FF:], -SWIGLU_LIMIT, SWIGLU_LIMIT)
        hdn = (gate * _sigmoid(SWIGLU_ALPHA * gate) * (up + 1.0)).astype(BF16)
        yield
        y = _dot(hdn, wd_bf[...]) + bd_ref[...]
        yield
        y_ref[rows, :] = _pack_rows(y)

    @pl.when(n_valid == TM)
    def _():
        passes = [ffn_pass(p, masked=False) for p in range(TM // TM_SUB)]
        n_stages = 5
        for t in range(n_stages + len(passes) - 1):
            for lag, body in enumerate(passes):
                if 0 <= t - lag < n_stages:
                    next(body, None)

    for p in range(TM // TM_SUB):
        @pl.when(jnp.logical_and(n_valid < TM, n_valid > p * TM_SUB))
        def _():
            for _ in ffn_pass(p, masked=True):
                pass


def _moe_experts(layer, block_e, n_valid, next_e, xs, w_gu, b_gu, w_down, b_down):
    grid_spec = pltpu.PrefetchScalarGridSpec(
        num_scalar_prefetch=3,
        grid=(MOE_NBLK,),
        in_specs=[pl.BlockSpec((TM, HALF), lambda i, be, nv, nx: (i, 0)),
                  pl.BlockSpec(memory_space=pl.ANY),
                  pl.BlockSpec((None, None, 1, 2 * D_FF), lambda i, be, nv, nx: (layer, be[i], 0, 0)),
                  pl.BlockSpec(memory_space=pl.ANY),
                  pl.BlockSpec((None, None, 1, D_MODEL), lambda i, be, nv, nx: (layer, be[i], 0, 0))],
        out_specs=pl.BlockSpec((TM, HALF), lambda i, be, nv, nx: (i, 0)),
        scratch_shapes=[pltpu.VMEM((D_MODEL, 2 * D_FF), F32), pltpu.VMEM((D_FF, D_MODEL), F32),
                        pltpu.VMEM((D_MODEL, 2 * D_FF), BF16), pltpu.VMEM((D_FF, D_MODEL), BF16),
                        pltpu.SemaphoreType.DMA((2,))],
    )
    return pl.pallas_call(
        functools.partial(_moe_kernel, layer=layer),
        grid_spec=grid_spec,
        out_shape=jax.ShapeDtypeStruct((MOE_ROWS, HALF), jnp.uint32),
        compiler_params=_params(),
        name="moe_experts",
    )(block_e, n_valid, next_e, xs, w_gu, b_gu.reshape(DEPTH, N_EXPERTS, 1, -1), w_down,
      b_down.reshape(DEPTH, N_EXPERTS, 1, -1))


SC_WORKERS = 32
SC_WIN = 64


def _sc_mesh():
    return plsc.VectorSubcoreMesh(core_axis_name="core", subcore_axis_name="subcore")


def _sc_worker():
    return lax.axis_index("core") * (SC_WORKERS // 2) + lax.axis_index("subcore")


def _sc_scatter_rows(x, dest_t, n_rows):
    n, width = x.shape
    kk = dest_t.shape[0]
    per = n // SC_WORKERS
    n_win = per // SC_WIN
    assert per * SC_WORKERS == n and n_win * SC_WIN == per and n_win % 2 == 0

    @pl.kernel(out_type=jax.ShapeDtypeStruct((n_rows, width), x.dtype), mesh=_sc_mesh(),
               scratch_types=[pltpu.VMEM((kk, per), jnp.int32), pltpu.VMEM((SC_WIN, width), x.dtype),
                              pltpu.VMEM((SC_WIN, width), x.dtype), pltpu.SemaphoreType.DMA((4,))])
    def scatter(x_hbm, i_hbm, o_hbm, idx_v, buf0, buf1, sems):
        base = _sc_worker() * per
        pltpu.sync_copy(i_hbm.at[:, pl.ds(base, per)], idx_v)

        def get(j, buf, s):
            return pltpu.make_async_copy(x_hbm.at[pl.ds(base + j * SC_WIN, SC_WIN)], buf, sems.at[s])

        def put(j, q, buf, s):
            return pltpu.make_async_copy(buf, o_hbm.at[idx_v.at[q, pl.ds(j * SC_WIN, SC_WIN)]], sems.at[s])

        get(0, buf0, 0).start()

        @pl.loop(0, n_win, step=2)
        def _(j):
            get(j, buf0, 0).wait()

            @pl.when(j > 0)
            def _():
                for q in range(kk):
                    put(j - 1, q, buf1, 3).wait()

            get(j + 1, buf1, 1).start()
            for q in range(kk):
                put(j, q, buf0, 2).start()
            get(j + 1, buf1, 1).wait()
            for q in range(kk):
                put(j, q, buf0, 2).wait()

            @pl.when(j + 2 < n_win)
            def _():
                get(j + 2, buf0, 0).start()

            for q in range(kk):
                put(j + 1, q, buf1, 3).start()

        for q in range(kk):
            put(n_win - 1, q, buf1, 3).wait()

    return scatter(x, dest_t)


def _sc_gather_rows(y, idx):
    n = idx.shape[0]
    width = y.shape[1]
    per = n // SC_WORKERS
    n_win = per // SC_WIN
    assert per * SC_WORKERS == n and n_win * SC_WIN == per and n_win % 2 == 0

    @pl.kernel(out_type=jax.ShapeDtypeStruct((n, width), y.dtype), mesh=_sc_mesh(),
               scratch_types=[pltpu.VMEM((per,), jnp.int32), pltpu.VMEM((SC_WIN, width), y.dtype),
                              pltpu.VMEM((SC_WIN, width), y.dtype), pltpu.SemaphoreType.DMA((4,))])
    def gather(y_hbm, i_hbm, o_hbm, idx_v, buf0, buf1, sems):
        base = _sc_worker() * per
        pltpu.sync_copy(i_hbm.at[pl.ds(base, per)], idx_v)

        def get(j, buf, s):
            return pltpu.make_async_copy(y_hbm.at[idx_v.at[pl.ds(j * SC_WIN, SC_WIN)]], buf, sems.at[s])

        def put(j, buf, s):
            return pltpu.make_async_copy(buf, o_hbm.at[pl.ds(base + j * SC_WIN, SC_WIN)], sems.at[s])

        get(0, buf0, 0).start()

        @pl.loop(0, n_win, step=2)
        def _(j):
            get(j, buf0, 0).wait()

            @pl.when(j > 0)
            def _():
                put(j - 1, buf1, 3).wait()

            get(j + 1, buf1, 1).start()
            put(j, buf0, 2).start()
            get(j + 1, buf1, 1).wait()
            put(j, buf0, 2).wait()

            @pl.when(j + 2 < n_win)
            def _():
                get(j + 2, buf0, 0).start()

            put(j + 1, buf1, 3).start()

        put(n_win - 1, buf1, 3).wait()

    return gather(y, idx)


def _gate_columns(gt_ref):
    r_i = lax.broadcasted_iota(jnp.int32, (RB, RB), 0)
    c_i = lax.broadcasted_iota(jnp.int32, (RB, RB), 1)
    eye = jnp.where(r_i == c_i, 1.0, 0.0).astype(BF16)
    g1, g2, g3 = _split3(gt_ref[...])
    return _dot_nt(eye, g1) + (_dot_nt(eye, g2) + _dot_nt(eye, g3))


def _combined_rows(x1_ref, yg_refs, gw, gate_ref, row, rows=slice(None)):
    acc_lo, acc_hi = None, None
    for kk in range(TOP_K):
        y_lo, y_hi = _unpack_rows(yg_refs[kk][rows, :])
        w = gw[rows, kk:kk + 1]
        acc_lo = y_lo * w if acc_lo is None else acc_lo + y_lo * w
        acc_hi = y_hi * w if acc_hi is None else acc_hi + y_hi * w
    x_lo = x1_ref[rows, :HALF] + gate_ref[pl.ds(row, 1), :HALF] * acc_lo
    x_hi = x1_ref[rows, HALF:] + gate_ref[pl.ds(row, 1), HALF:] * acc_hi
    return x_lo, x_hi


def _combine_specs():
    slot_spec = lambda k: pl.BlockSpec((RB, HALF), lambda i: (k * NBLK + i, 0))
    return ([pl.BlockSpec((RB, D_MODEL), lambda i: (i, 0))] + [slot_spec(k) for k in range(TOP_K)]
            + [pl.BlockSpec((8, RB), lambda i: (0, i)), pl.BlockSpec((MOD_ROWS, D_MODEL), lambda i: (0, 0))])


RING = 3


def _combine_final_kernel(x1_hbm, yg_hbm, gt_ref, gate_ref, fg_ref, op_ref, os_ref, x1_buf, yg_buf, sems):
    i = pl.program_id(0)

    def fetch(blk):
        slot = blk % RING
        row0 = pl.multiple_of(blk * RB, RB)
        copies = [pltpu.make_async_copy(x1_hbm.at[pl.ds(row0, RB), :], x1_buf.at[slot], sems.at[slot, 0])]
        for kk in range(TOP_K):
            copies.append(pltpu.make_async_copy(yg_hbm.at[pl.ds(kk * NT + row0, RB), :], yg_buf.at[slot, kk],
                                                sems.at[slot, 1 + kk]))
        return copies

    @pl.when(i == 0)
    def _():
        for b in range(RING - 1):
            for cp in fetch(b):
                cp.start()

    @pl.when(i + RING - 1 < NBLK)
    def _():
        for cp in fetch(i + RING - 1):
            cp.start()

    for cp in fetch(i):
        cp.wait()
    slot = i % RING
    x1_ref = x1_buf.at[slot]
    yg_refs = [yg_buf.at[slot, kk] for kk in range(TOP_K)]
    x_lo, x_hi = _combined_rows(x1_ref, yg_refs, _gate_columns(gt_ref), gate_ref, _mod_row(i))
    ms = (jnp.sum(x_lo * x_lo, axis=-1, keepdims=True) + jnp.sum(x_hi * x_hi, axis=-1, keepdims=True)) / D_MODEL
    scale = lax.rsqrt(ms + EPS)

    @pl.when(i < N_PROMPT_BLK)
    def _():
        op_ref[:, :HALF] = x_lo * scale * fg_ref[:, :HALF]
        op_ref[:, HALF:] = x_hi * scale * fg_ref[:, HALF:]

    @pl.when(i >= N_PROMPT_BLK)
    def _():
        os_ref[:, :HALF] = x_lo * scale * fg_ref[:, :HALF]
        os_ref[:, HALF:] = x_hi * scale * fg_ref[:, HALF:]


def _combine_final(x1, yg, gates_t, gate, final_g):
    return pl.pallas_call(
        _combine_final_kernel,
        grid=(NBLK,),
        in_specs=[pl.BlockSpec(memory_space=pl.ANY), pl.BlockSpec(memory_space=pl.ANY),
                  pl.BlockSpec((8, RB), lambda i: (0, i)), pl.BlockSpec((MOD_ROWS, D_MODEL), lambda i: (0, 0)),
                  pl.BlockSpec((1, D_MODEL), lambda i: (0, 0))],
        out_specs=[pl.BlockSpec((RB, D_MODEL), lambda i: (jnp.minimum(i, N_PROMPT_BLK - 1), 0)),
                   pl.BlockSpec((RB, D_MODEL), lambda i: (jnp.maximum(i - N_PROMPT_BLK, 0), 0))],
        out_shape=[jax.ShapeDtypeStruct((NT_PROMPT, D_MODEL), F32),
                   jax.ShapeDtypeStruct((NT - NT_PROMPT, D_MODEL), F32)],
        scratch_shapes=[pltpu.VMEM((RING, RB, D_MODEL), F32), pltpu.VMEM((RING, TOP_K, RB, HALF), jnp.uint32),
                        pltpu.SemaphoreType.DMA((RING, 1 + TOP_K))],
        compiler_params=_params(),
        name="moe_combine_final",
    )(x1, yg, gates_t, gate, final_g.reshape(1, D_MODEL))


def _routing_plan(counts, idx_t, rank_t):
    counts = counts[:, 0]
    padded = (counts + TM - 1) // TM * TM
    pad_end = jnp.cumsum(padded)
    pad_start = pad_end - padded
    blk_row = (jnp.arange(MOE_NBLK, dtype=jnp.int32) * TM)[:, None]
    ids = jnp.arange(N_EXPERTS, dtype=jnp.int32)
    owns = jnp.logical_and(pad_start[None, :] <= blk_row, blk_row < pad_end[None, :])
    last_used = jnp.max(jnp.where(counts > 0, ids, 0))
    block_e = jnp.where(jnp.any(owns, axis=1), jnp.sum(jnp.where(owns, ids[None, :], 0), axis=1), last_used)
    block_e = block_e.astype(jnp.int32)
    left = jnp.clip(counts[None, :] - (blk_row - pad_start[None, :]), 0, TM)
    n_valid = jnp.sum(jnp.where(owns, left, 0), axis=1).astype(jnp.int32)
    start = jnp.zeros(idx_t.shape, jnp.int32)
    for e in range(N_EXPERTS):
        start = jnp.where(idx_t == e, pad_start[e], start)
    dest_t = (start + rank_t).astype(jnp.int32)
    later = jnp.where(jnp.logical_and(counts[None, :] > 0, ids[None, :] > ids[:, None]), ids[None, :], N_EXPERTS)
    next_e = jnp.min(later, axis=1)
    next_e = jnp.where(next_e == N_EXPERTS, -1, next_e).astype(jnp.int32)
    return block_e, n_valid, next_e, dest_t


def _rope_tables():
    rows = DEC_SEQ // GRID_W
    row = jnp.repeat(jnp.arange(rows, dtype=F32), GRID_W)
    col = jnp.tile(jnp.arange(GRID_W, dtype=F32), rows)
    n_f = RET_DK // 4
    freqs = ROPE_THETA ** (-jnp.arange(n_f, dtype=F32) / n_f)
    ang = jnp.concatenate([row[:, None] * freqs, col[:, None] * freqs], axis=-1)
    cos = jnp.repeat(jnp.cos(ang), 2, axis=-1)
    sin = jnp.repeat(jnp.sin(ang), 2, axis=-1) * jnp.tile(jnp.asarray([-1.0, 1.0], F32), RET_DK // 2)
    cos = jnp.concatenate([jnp.ones((RB, RET_DK), F32), cos], axis=0)
    sin = jnp.concatenate([jnp.zeros((RB, RET_DK), F32), sin], axis=0)
    return jnp.tile(cos, (1, 2)), jnp.tile(sin, (1, 2))


def kernel(x_prompt, x_sample, state_ret, state_gla, c, c_ctx, w_mod, b_mod, norm1_g, norm2_g, final_g, even_w_in, ret_decay, ret_gn, conv_w, conv_b, conv_ln_g, conv_ln_b, even_w_out, odd_w_in, gla_w_a2, gla_b_a2, gla_gn, odd_w_out, router_w, router_b, exp_w_gu, exp_b_gu, exp_w_down, exp_b_down):
    x_src = ("split", x_prompt.reshape(NT_PROMPT, D_MODEL), x_sample.reshape(NT - NT_PROMPT, D_MODEL))
    cvec = jnp.concatenate([c_ctx[None, :], c, jnp.zeros((MOD_ROWS - 1 - DEC_BATCH, D_MODEL), F32)], axis=0)
    mods = _modulation(cvec, w_mod, b_mod).reshape(DEPTH, MOD_ROWS, N_MOD, D_MODEL)
    cos_tab, sin_tab = _rope_tables()
    new_ret = jnp.zeros((BATCH,) + state_ret.shape[1:], F32)
    new_gla = jnp.zeros((BATCH,) + state_gla.shape[1:], F32)
    for l in range(DEPTH):
        mod = [mods[l, :, j, :] for j in range(N_MOD)]
        if l % 2 == 0:
            e = l // 2
            qd, vd = RET_HEADS * RET_DK, RET_HEADS * RET_DV
            x, q, k, v, g, a, ga = _inproj(x_src, norm1_g[l], mod[0], mod[1], even_w_in, e,
                                           (qd, qd, vd, vd, CONV_CH, CONV_CH))
            conv = (a, ga, conv_w[e], conv_b[e], conv_ln_g[e], conv_ln_b[e])
            o_f, new_ret, u = _retention(ret_decay[e], q, k, v, cos_tab, sin_tab, state_ret, new_ret, e,
                                         reverse=False, conv=conv)
            outproj = (even_w_out, e, x, mod[2], norm2_g[l], mod[3], mod[4], router_w[l], router_b[l])
            new_ret, x1, h2, logits_t = _retention(ret_decay[e], q, k, v, cos_tab, sin_tab, state_ret, new_ret, e,
                                                   reverse=True, o_fwd=o_f, g=g, gn=ret_gn[e], u=u,
                                                   outproj=outproj)
        else:
            o = l // 2
            qd, vd = GLA_HEADS * GLA_DK, GLA_HEADS * GLA_DV
            x, q, k, v, r, alr = _inproj(x_src, norm1_g[l], mod[0], mod[1], odd_w_in, o,
                                         (qd, qd, vd, vd, 2 * GLA_RANK))
            zeros = jnp.zeros((GLA_RANK, qd), F32)
            wa_f = jnp.concatenate([gla_w_a2[o, 0], zeros], axis=0)
            wa_b = jnp.concatenate([zeros, gla_w_a2[o, 1]], axis=0)
            o_f, new_gla = _gla(alr, q, k, v, wa_f, gla_b_a2[o, 0].reshape(1, qd), state_gla, new_gla, o,
                                reverse=False)
            outproj = (odd_w_out, o, x, mod[2], norm2_g[l], mod[3], mod[4], router_w[l], router_b[l])
            new_gla, x1, h2, logits_t = _gla(alr, q, k, v, wa_b, gla_b_a2[o, 1].reshape(1, qd), state_gla, new_gla,
                                             o, reverse=True, o_fwd=o_f, r=r, gn=gla_gn[o], outproj=outproj)
        idx_t, rank_t, gates_t, counts = _route(logits_t)
        block_e, n_valid, next_e, dest_t = _routing_plan(counts, idx_t, rank_t)
        xs = _sc_scatter_rows(h2, dest_t, MOE_ROWS)
        yb = _moe_experts(l, block_e, n_valid, next_e, xs, exp_w_gu, exp_b_gu, exp_w_down, exp_b_down)
        yg = _sc_gather_rows(yb, dest_t.reshape(TOP_K * NT))
        x_src = ("moe", x1, yg, gates_t, mod[5])
    y_prompt, y_sample = _combine_final(*x_src[1:], final_g)
    y_prompt = y_prompt.reshape(BATCH, SEQ, D_MODEL)
    y_sample = y_sample.reshape(DEC_BATCH, DEC_SEQ, D_MODEL)
    return (y_prompt, y_sample, new_ret, new_gla)
```

```python
import functools

import jax
import jax.numpy as jnp
from jax import lax
from jax.experimental import pallas as pl
from jax.experimental.pallas import tpu as pltpu
from jax.experimental.pallas import tpu_sc as plsc

F32 = jnp.float32
BF16 = jnp.bfloat16

D_MODEL = 1024
BATCH = 16
SEQ = 256
DEPTH = 4
DEC_BATCH = 4
DEC_SEQ = 4096
GRID_W = 64
RET_HEADS = 4
RET_DK = 64
RET_DV = 128
RET_CHUNK = 128
CONV_CH = 512
CONV_WIDTH = 31
CONV_PAD = CONV_WIDTH // 2
GLA_HEADS = 4
GLA_DK = 128
GLA_DV = 256
GLA_RANK = 16
GLA_TAU = 16.0
GLA_CHUNK = 64
GLA_SUB = 16
N_EXPERTS = 32
TOP_K = 4
D_FF = 1024
SWIGLU_LIMIT = 7.0
SWIGLU_ALPHA = 1.702
ROPE_THETA = 10000.0
EPS = 1e-6
N_MOD = 6

RB = 256
NT_PROMPT = BATCH * SEQ
NT = NT_PROMPT + DEC_BATCH * DEC_SEQ
NBLK = NT // RB
N_PROMPT_BLK = NT_PROMPT // RB
SAMPLE_BLK = DEC_SEQ // RB
MOD_ROWS = 8
HALO = 16
VMEM_LIMIT = 48 * 1024 * 1024

assert SEQ == RB and DEC_SEQ % RB == 0 and CONV_PAD < HALO


def _seq_of_block(i):
    return jnp.where(i < N_PROMPT_BLK, i, N_PROMPT_BLK + (i - N_PROMPT_BLK) // SAMPLE_BLK)


def _is_first_block(i):
    return jnp.logical_or(i < N_PROMPT_BLK, (i - N_PROMPT_BLK) % SAMPLE_BLK == 0)


def _is_last_block(i):
    return jnp.logical_or(i < N_PROMPT_BLK, (i - N_PROMPT_BLK) % SAMPLE_BLK == SAMPLE_BLK - 1)


def _mod_row(i):
    return jnp.where(i < N_PROMPT_BLK, 0, 1 + (i - N_PROMPT_BLK) // SAMPLE_BLK)


def _rope_block(i):
    return jnp.where(i < N_PROMPT_BLK, 0, 1 + (i - N_PROMPT_BLK) % SAMPLE_BLK)


def _dot(a, b):
    return jnp.dot(a, b, preferred_element_type=F32)


def _dot_nt(a, b):
    return lax.dot_general(a, b, (((1,), (1,)), ((), ())), preferred_element_type=F32)


def _dot_tn(a, b):
    return lax.dot_general(a, b, (((0,), (0,)), ((), ())), preferred_element_type=F32)


def _split2(a):
    hi = a.astype(BF16)
    lo = (a - hi.astype(F32)).astype(BF16)
    return hi, lo


def _dot_hi(a, b):
    a_hi, a_lo = _split2(a)
    b_hi, b_lo = _split2(b)
    return _dot(a_hi, b_hi) + (_dot(a_hi, b_lo) + _dot(a_lo, b_hi))


def _silu(x):
    return x * (1.0 / (1.0 + jnp.exp(-x)))


def _sigmoid(x):
    return 1.0 / (1.0 + jnp.exp(-x))


def _pack_rows(x):
    n = x.shape[1] // 2
    lo = pltpu.bitcast(x[:, :n].astype(BF16).astype(F32), jnp.uint32)
    hi = pltpu.bitcast(x[:, n:].astype(BF16).astype(F32), jnp.uint32)
    return hi | (lo >> 16)


def _unpack_rows(u):
    lo = pltpu.bitcast(u << 16, F32)
    hi = pltpu.bitcast(u & jnp.uint32(0xFFFF0000), F32)
    return lo, hi


def _params(n_axes=1, vmem=VMEM_LIMIT):
    return pltpu.CompilerParams(dimension_semantics=("arbitrary",) * n_axes, vmem_limit_bytes=vmem)


MOD_TN = 1536


def _mod_kernel(c_ref, w_ref, b_ref, o_ref):
    s = _silu(c_ref[...]).astype(BF16)
    o_ref[...] = _dot(s, w_ref[...].astype(BF16)) + b_ref[...]


def _modulation(cvec, w_mod, b_mod):
    n = N_MOD * D_MODEL
    return pl.pallas_call(
        _mod_kernel,
        grid=(DEPTH, n // MOD_TN),
        in_specs=[pl.BlockSpec((MOD_ROWS, D_MODEL), lambda l, j: (0, 0)),
                  pl.BlockSpec((None, D_MODEL, MOD_TN), lambda l, j: (l, 0, j)),
                  pl.BlockSpec((None, 1, MOD_TN), lambda l, j: (l, 0, j))],
        out_specs=pl.BlockSpec((None, MOD_ROWS, MOD_TN), lambda l, j: (l, 0, j)),
        out_shape=jax.ShapeDtypeStruct((DEPTH, MOD_ROWS, n), F32),
        compiler_params=_params(2),
        name="modulation",
    )(cvec, w_mod, b_mod.reshape(DEPTH, 1, n))


def _norm_mod(x, g_ref, shift_ref, scale_ref, row):
    y = x * lax.rsqrt(jnp.mean(x * x, axis=-1, keepdims=True) + EPS) * g_ref[...]
    return y * (1.0 + scale_ref[pl.ds(row, 1), :]) + shift_ref[pl.ds(row, 1), :]


N_SRC = {"split": 2, "moe": 4}


def _inproj_kernel(*refs, widths, source):
    n_src = N_SRC[source]
    src = refs[:n_src]
    g_ref, shift_ref, scale_ref, w_ref = refs[n_src:n_src + 4]
    if source == "moe":
        outs, w_bf, ring = refs[n_src + 4:-4], refs[-4], refs[-3:]
    else:
        outs, w_bf = refs[n_src + 4:-1], refs[-1]
    step = pl.program_id(0)

    @pl.when(step == 0)
    def _():
        w_bf[...] = w_ref[...].astype(BF16)

    row = _mod_row(step)
    if source == "split":
        x = jnp.where(step < N_PROMPT_BLK, src[0][...], src[1][...])
    else:
        x1_hbm, yg_hbm, gt_ref, gate_ref = src
        x1_ref, yg_refs = _ring_rows(x1_hbm, yg_hbm, *ring)
        x = jnp.concatenate(_combined_rows(x1_ref, yg_refs, _gate_columns(gt_ref), gate_ref, row), axis=1)
    outs[0][...] = x
    hb = _norm_mod(x, g_ref, shift_ref, scale_ref, row).astype(BF16)
    off = 0
    for o_ref, width in zip(outs[1:], widths):
        o_ref[...] = _dot(hb, w_bf[:, off:off + width])
        off += width


def _inproj(x, g, shift, scale, w_all, index, widths):
    source, x_args = x[0], list(x[1:])
    n_in = w_all.shape[2]
    row_spec = lambda width: pl.BlockSpec((RB, width), lambda i: (i, 0))
    full = lambda shape: pl.BlockSpec(shape, lambda i: (0,) * len(shape))
    if source == "split":
        x_specs = [pl.BlockSpec((RB, D_MODEL), lambda i: (jnp.minimum(i, N_PROMPT_BLK - 1), 0)),
                   pl.BlockSpec((RB, D_MODEL), lambda i: (jnp.maximum(i - N_PROMPT_BLK, 0), 0))]
    else:
        x_specs = _combine_specs()
    scratch = [pltpu.VMEM((D_MODEL, n_in), BF16)] + (_ring_scratch() if source == "moe" else [])
    out_widths = (D_MODEL,) + tuple(widths)
    return pl.pallas_call(
        functools.partial(_inproj_kernel, widths=widths, source=source),
        grid=(NBLK,),
        in_specs=x_specs + [full((1, D_MODEL)), full((MOD_ROWS, D_MODEL)), full((MOD_ROWS, D_MODEL)),
                            pl.BlockSpec((None, D_MODEL, n_in), lambda i: (index, 0, 0))],
        out_specs=[row_spec(width) for width in out_widths],
        out_shape=[jax.ShapeDtypeStruct((NT, width), F32) for width in out_widths],
        scratch_shapes=scratch,
        compiler_params=_params(),
        name="inproj",
    )(*x_args, g.reshape(1, D_MODEL), shift, scale, w_all)


RC = RET_CHUNK
RET_PAIR = 2 * RET_DK


def _rope(x, cos, sin_signed):
    lane = lax.broadcasted_iota(jnp.int32, x.shape, 1)
    swapped = jnp.where(lane % 2 == 0, pltpu.roll(x, x.shape[1] - 1, 1), pltpu.roll(x, 1, 1))
    return x * cos + swapped * sin_signed


def _ret_kernel(*refs, reverse):
    for _ in _ret_stages(*refs, reverse=reverse):
        pass


def _ret_stages(decay_ref, q_ref, k_ref, v_ref, cos_ref, sin_ref, s0_ref, acc_ref, *rest, reverse):
    del acc_ref
    if reverse:
        of_ref, g_ref, gn_ref, o_ref, sfin_ref, st_ref, dm_ref, dq_ref, dk_ref, ds_ref = rest
    else:
        o_ref, sfin_ref, st_ref, dm_ref, dq_ref, dk_ref, ds_ref = rest
    step = pl.program_id(0)
    blk = NBLK - 1 - step if reverse else step
    direction = 1 if reverse else 0

    @pl.when(step == 0)
    def _():
        row = lax.broadcasted_iota(jnp.int32, (RC, RC), 0).astype(F32)
        col = lax.broadcasted_iota(jnp.int32, (RC, RC), 1).astype(F32)
        for h in range(RET_HEADS):
            lg = -jnp.exp(jnp.full((RC, RC), decay_ref[direction, h], F32))
            if reverse:
                diff = col - row
                mask = diff > 0
                q_pow = RC - row
                k_pow = row
            else:
                diff = row - col
                mask = diff >= 0
                q_pow = row + 1.0
                k_pow = RC - 1.0 - row
            dm_ref[h] = jnp.where(mask, jnp.exp(lg * jnp.where(mask, diff, 0.0)), 0.0)
            dq_ref[h] = jnp.exp(lg * q_pow)
            dk_ref[h] = jnp.exp(lg * k_pow)
            ds_ref[h] = jnp.exp(lg * RC)

    starts = _is_last_block(blk) if reverse else _is_first_block(blk)

    is_prompt = blk < N_PROMPT_BLK

    @pl.when(starts)
    def _():
        st_ref[...] = jnp.zeros_like(st_ref)

    @pl.when(jnp.logical_and(starts, jnp.logical_not(is_prompt)))
    def _():
        for h in range(RET_HEADS):
            off = (h % 2) * RET_DK
            st_ref[h, off:off + RET_DK, :] = s0_ref[h]

    lane = lax.broadcasted_iota(jnp.int32, (1, RET_PAIR), 1)
    chunks = range(RB // RC)
    chunk_order = list(reversed(chunks) if reverse else chunks)
    units = [(h, c) for h in range(RET_HEADS) for c in chunk_order]
    rows_of = lambda c: slice(c * RC, (c + 1) * RC)
    vcols = lambda h: slice(h * RET_DV, (h + 1) * RET_DV)
    roped = {}
    for p in range(RET_HEADS // 2):
        cols = slice(p * RET_PAIR, (p + 1) * RET_PAIR)
        for c in chunk_order:
            rows = rows_of(c)
            cos, sin = cos_ref[rows, :], sin_ref[rows, :]
            roped[p, c] = (_rope(q_ref[rows, cols], cos, sin),
                           _rope(k_ref[rows, cols] * (RET_DK ** -0.5), cos, sin))
    yield
    qm_u, vh_u, att_u, kd_u = {}, {}, {}, {}
    for h, c in units:
        head_mask = (lane // RET_DK == h % 2).astype(F32)
        q2, k2 = roped[h // 2, c]
        vh_u[h, c] = v_ref[rows_of(c), vcols(h)].astype(BF16)
        qm_u[h, c] = (q2 * head_mask).astype(BF16)
        km = k2 * head_mask
        att_u[h, c] = (_dot_nt(qm_u[h, c], km.astype(BF16)) * dm_ref[h]).astype(BF16)
        kd_u[h, c] = (km * dk_ref[h]).astype(BF16)
    yield
    intra_u = {u: _dot(att_u[u], vh_u[u]) for u in units}
    delta_u = {u: _dot_tn(kd_u[u], vh_u[u]) for u in units}
    yield
    state_u = {}
    for h in range(RET_HEADS):
        st = st_ref[h]
        for c in chunk_order:
            state_u[h, c] = st.astype(BF16)
            st = st * ds_ref[h] + delta_u[h, c]
        st_ref[h] = st
    yield
    for h, c in units:
        rows, out_cols = rows_of(c), vcols(h)
        o = intra_u[h, c] + _dot(qm_u[h, c], state_u[h, c]) * dq_ref[h]
        if reverse:
            o = o + of_ref[rows, out_cols]
            o = o * lax.rsqrt(jnp.mean(o * o, axis=-1, keepdims=True) + EPS)
            o = o * gn_ref[:, out_cols] * _silu(g_ref[rows, out_cols])
            o_ref[rows, out_cols] = o.astype(o_ref.dtype)
        else:
            o_ref[rows, out_cols] = o

    ends = _is_first_block(blk) if reverse else _is_last_block(blk)

    @pl.when(jnp.logical_and(ends, is_prompt))
    def _():
        for h in range(RET_HEADS):
            off = (h % 2) * RET_DK
            sfin_ref[h] = st_ref[h, off:off + RET_DK, :]


def _sample_seq(blk):
    return jnp.clip(_seq_of_block(blk) - BATCH, 0, DEC_BATCH - 1)


def _prompt_seq(blk):
    return jnp.minimum(_seq_of_block(blk), BATCH - 1)


def _retention(decay, q, k, v, cos_tab, sin_tab, state_in, state_out, layer, *, reverse,
               o_fwd=None, g=None, gn=None, conv=None, u=None, outproj=None):
    bmap = (lambda j: NBLK - 1 - j) if reverse else (lambda j: j)
    direction = 1 if reverse else 0
    qd, vd = RET_HEADS * RET_DK, RET_HEADS * RET_DV
    row_spec = lambda width: pl.BlockSpec((RB, width), lambda j: (bmap(j), 0))
    state_blk = (None, None, None, RET_HEADS, RET_DK, RET_DV)
    in_specs = [pl.BlockSpec(memory_space=pltpu.SMEM), row_spec(qd), row_spec(qd), row_spec(vd),
                pl.BlockSpec((RB, RET_PAIR), lambda j: (_rope_block(bmap(j)), 0)),
                pl.BlockSpec((RB, RET_PAIR), lambda j: (_rope_block(bmap(j)), 0)),
                pl.BlockSpec(state_blk, lambda j: (_sample_seq(bmap(j)), layer, direction, 0, 0, 0)),
                pl.BlockSpec(memory_space=pl.ANY)]
    args = [decay, q, k, v, cos_tab, sin_tab, state_in, state_out]
    if reverse:
        in_specs += [row_spec(vd), row_spec(vd), pl.BlockSpec((1, vd), lambda j: (0, 0))]
        args += [o_fwd, g, gn.reshape(1, vd)]
    tile = pltpu.VMEM((RET_HEADS, RC, RC), F32)
    out_specs = [row_spec(vd),
                 pl.BlockSpec(state_blk, lambda j: (_prompt_seq(bmap(j)), layer, direction, 0, 0, 0))]
    out_shape = [jax.ShapeDtypeStruct((NT, vd), BF16 if reverse else F32),
                 jax.ShapeDtypeStruct(state_out.shape, F32)]
    scratch = [tile, tile, tile, tile, tile]
    body = functools.partial(_ret_kernel, reverse=reverse)
    if conv is not None:
        assert not reverse
        n_ret = (len(in_specs), len(out_specs), len(scratch))
        c_in, c_args, c_out, c_shape, c_scratch = _conv_parts(*conv)
        in_specs, args = in_specs + c_in, args + c_args
        out_specs, out_shape, scratch = out_specs + c_out, out_shape + c_shape, scratch + c_scratch
        body = functools.partial(_ret_conv_kernel, n_ret=n_ret, n_conv=(len(c_in), len(c_out), len(c_scratch)))
    state_output = 1
    if outproj is not None:
        assert reverse
        call = _with_outproj(body, in_specs, args, out_specs, out_shape, scratch, [u], outproj)
        body, in_specs, args = call["kernel"], call["in_specs"], call["args"]
        out_specs, out_shape, scratch = call["out_specs"], call["out_shape"], call["scratch"]
        state_output = 0
    return pl.pallas_call(
        body,
        grid=(NBLK,),
        in_specs=in_specs,
        out_specs=out_specs,
        out_shape=out_shape,
        input_output_aliases={7: state_output},
        scratch_shapes=scratch,
        compiler_params=_params(),
        name="retention_bwd" if reverse else "retention_fwd",
    )(*args)


def _ret_conv_kernel(*refs, n_ret, n_conv):
    (ri, ro, rs), (ci, co, cs) = n_ret, n_conv
    ins, outs, scr = refs[:ri + ci], refs[ri + ci:ri + ci + ro + co], refs[ri + ci + ro + co:]
    bodies = [_conv_stages(*ins[ri:], *outs[ro:], *scr[rs:]),
              _ret_stages(*ins[:ri], *outs[:ro], *scr[:rs], reverse=False)]
    while bodies:
        for body in list(bodies):
            if next(body, StopIteration) is StopIteration:
                bodies.remove(body)


CONV_RT = 32
CONV_CT = 128
CONV_SPAN = RB + 2 * HALO - 8


def _conv_stages(a_ref, ga_ref, ap_ref, gap_ref, an_ref, gan_ref, cw_ref, cb_ref, lng_ref, lnb_ref,
                 o_ref, u_ref, y_ref, us_ref):
    blk = pl.program_id(0)
    keep_prev = jnp.where(_is_first_block(blk), 0.0, 1.0)
    keep_next = jnp.where(_is_last_block(blk), 0.0, 1.0)
    u_ref[0:HALO, :] = ap_ref[...] * _sigmoid(gap_ref[...]) * keep_prev
    u_ref[HALO:HALO + RB, :] = a_ref[...] * _sigmoid(ga_ref[...])
    u_ref[HALO + RB:HALO + RB + HALO, :] = an_ref[...] * _sigmoid(gan_ref[...]) * keep_next
    for r in range(1, 8):
        us_ref[r - 1] = u_ref[r:r + CONV_SPAN, :]
    for ct in range(CONV_CH // CONV_CT):
        yield
        cols = slice(ct * CONV_CT, (ct + 1) * CONV_CT)
        for rt in range(RB // CONV_RT):
            acc = jnp.zeros((CONV_RT, CONV_CT), F32)
            for w in range(CONV_WIDTH):
                tiles, r = divmod(HALO - CONV_PAD + w, 8)
                base = rt * CONV_RT + 8 * tiles
                src = u_ref if r == 0 else us_ref.at[r - 1]
                acc = acc + src[base:base + CONV_RT, cols] * cw_ref[w:w + 1, cols]
            y_ref[rt * CONV_RT:(rt + 1) * CONV_RT, cols] = acc + cb_ref[:, cols]
    y = y_ref[...]
    mu = jnp.mean(y, axis=-1, keepdims=True)
    var = jnp.mean(jnp.square(y - mu), axis=-1, keepdims=True)
    o_ref[...] = _silu((y - mu) * lax.rsqrt(var + EPS) * lng_ref[...] + lnb_ref[...]).astype(o_ref.dtype)


def _conv_parts(a, ga, cw, cb, lng, lnb):
    per_blk = RB // HALO
    n_halo = NT // HALO
    row_spec = pl.BlockSpec((RB, CONV_CH), lambda i: (i, 0))
    prev_spec = pl.BlockSpec((HALO, CONV_CH), lambda i: (jnp.maximum(i * per_blk - 1, 0), 0))
    next_spec = pl.BlockSpec((HALO, CONV_CH), lambda i: (jnp.minimum((i + 1) * per_blk, n_halo - 1), 0))
    vec = pl.BlockSpec((1, CONV_CH), lambda i: (0, 0))
    in_specs = [row_spec, row_spec, prev_spec, prev_spec, next_spec, next_spec,
                pl.BlockSpec((CONV_WIDTH, CONV_CH), lambda i: (0, 0)), vec, vec, vec]
    args = [a, ga, a, ga, a, ga, cw, cb.reshape(1, -1), lng.reshape(1, -1), lnb.reshape(1, -1)]
    scratch = [pltpu.VMEM((RB + 2 * HALO, CONV_CH), F32), pltpu.VMEM((RB, CONV_CH), F32),
               pltpu.VMEM((7, CONV_SPAN, CONV_CH), F32)]
    return in_specs, args, [row_spec], [jax.ShapeDtypeStruct((NT, CONV_CH), BF16)], scratch


GC = GLA_CHUNK
GLA_NSUB = GC // GLA_SUB


def _split3(a):
    p1 = a.astype(BF16)
    r1 = a - p1.astype(F32)
    p2 = r1.astype(BF16)
    p3 = (r1 - p2.astype(F32)).astype(BF16)
    return p1, p2, p3


def _gla_kernel(alr_ref, q_ref, k_ref, v_ref, wa_ref, ba_ref, s0_ref, acc_ref, *rest, reverse):
    del acc_ref
    if reverse:
        of_ref, r_ref, gn_ref, o_ref, sfin_ref, st_ref, b_ref, tri_ref, ob_ref = rest
    else:
        o_ref, sfin_ref, st_ref, b_ref, tri_ref = rest
        ob_ref = o_ref
    step = pl.program_id(0)
    blk = NBLK - 1 - step if reverse else step
    starts = _is_last_block(blk) if reverse else _is_first_block(blk)
    is_prompt = blk < N_PROMPT_BLK

    @pl.when(jnp.logical_and(starts, is_prompt))
    def _():
        st_ref[...] = jnp.zeros_like(st_ref)

    @pl.when(jnp.logical_and(starts, jnp.logical_not(is_prompt)))
    def _():
        for h in range(GLA_HEADS):
            st_ref[h] = s0_ref[h].T

    @pl.when(step == 0)
    def _():
        row = lax.broadcasted_iota(jnp.int32, (RB, RB), 0)
        col = lax.broadcasted_iota(jnp.int32, (RB, RB), 1)
        ordered = col >= row if reverse else col <= row
        tri_ref[...] = jnp.where(jnp.logical_and(row // GC == col // GC, ordered), 1.0, 0.0).astype(BF16)

    heads = range(GLA_HEADS)
    hcols = [slice(h * GLA_DK, (h + 1) * GLA_DK) for h in heads]
    alr = alr_ref[...]
    z = [_dot_hi(alr, wa_ref[:, hcols[h]]) + ba_ref[:, hcols[h]] for h in heads]
    log_a = [(jnp.minimum(zh, 0.0) - jnp.log(1.0 + jnp.exp(-jnp.abs(zh)))) * (1.0 / GLA_TAU) for zh in z]
    parts = [_split3(la) for la in log_a]
    tri = tri_ref[...]
    for h in heads:
        g1, g2, g3 = parts[h]
        b_ref[:, hcols[h]] = _dot(tri, g1) + (_dot(tri, g2) + _dot(tri, g3))

    c_row = lax.broadcasted_iota(jnp.int32, (GC, 1), 0)
    a_row = lax.broadcasted_iota(jnp.int32, (GC, GC), 0)
    a_col = lax.broadcasted_iota(jnp.int32, (GC, GC), 1)
    att_mask = a_col > a_row if reverse else a_col <= a_row
    chunks = range(RB // GC)
    chunk_order = list(reversed(chunks) if reverse else chunks)
    units = [(h, c) for h in range(GLA_HEADS) for c in chunk_order]
    kcols = lambda h: slice(h * GLA_DK, (h + 1) * GLA_DK)
    vcols = lambda h: slice(h * GLA_DV, (h + 1) * GLA_DV)
    rows_of = lambda c: slice(c * GC, (c + 1) * GC)

    vh_u, qe_u, ke_u, decay_u, att_u = {}, {}, {}, {}, {}
    for h, c in units:
        rows = rows_of(c)
        b = b_ref[rows, kcols(h)]
        qh = q_ref[rows, kcols(h)] * (GLA_DK ** -0.5)
        kh = k_ref[rows, kcols(h)]
        vh_u[h, c] = v_ref[rows, vcols(h)].astype(BF16)
        edge = b[0:1, :] if reverse else b[GC - 1:GC, :]
        bounds = []
        for s in range(GLA_NSUB):
            if reverse:
                hi = (s + 1) * GLA_SUB
                bounds.append(b[hi:hi + 1, :] if s < GLA_NSUB - 1 else jnp.zeros((1, GLA_DK), F32))
            else:
                lo = s * GLA_SUB
                bounds.append(b[lo - 1:lo, :] if s > 0 else jnp.zeros((1, GLA_DK), F32))
        own = jnp.concatenate([jnp.broadcast_to(bd, (GLA_SUB, GLA_DK)) for bd in bounds], axis=0)
        q_own = qh * jnp.exp(b - own)
        q_parts, k_parts = [], []
        for s, bd in enumerate(bounds):
            q_parts.append(jnp.where(c_row // GLA_SUB == s, q_own, 0.0))
            reach = c_row >= s * GLA_SUB if reverse else c_row < (s + 1) * GLA_SUB
            k_parts.append(kh * jnp.exp(jnp.where(reach, bd - b, -jnp.inf)))
        q_bd = jnp.concatenate(q_parts, axis=1).astype(BF16)
        k_cat = jnp.concatenate(k_parts, axis=1).astype(BF16)
        att_u[h, c] = jnp.where(att_mask, _dot_nt(q_bd, k_cat), 0.0).astype(BF16)
        qe_u[h, c] = (qh * jnp.exp(b)).astype(BF16)
        ke_u[h, c] = (kh * jnp.exp(edge - b)).astype(BF16)
        decay_u[h, c] = jnp.exp(edge)
    intra_u = {u: _dot(att_u[u], vh_u[u]) for u in units}
    delta_u = {u: _dot_tn(vh_u[u], ke_u[u]) for u in units}
    state_u = {}
    for h in range(GLA_HEADS):
        st = st_ref[h]
        for c in chunk_order:
            state_u[h, c] = st.astype(BF16)
            st = st * decay_u[h, c] + delta_u[h, c]
        st_ref[h] = st
    for h, c in units:
        ob_ref[rows_of(c), vcols(h)] = intra_u[h, c] + _dot_nt(qe_u[h, c], state_u[h, c])

    if reverse:
        for h in range(GLA_HEADS):
            cols = vcols(h)
            o = ob_ref[:, cols] + of_ref[:, cols]
            o = o * lax.rsqrt(jnp.mean(o * o, axis=-1, keepdims=True) + EPS)
            o_ref[:, cols] = (o * gn_ref[:, cols] * _silu(r_ref[:, cols])).astype(o_ref.dtype)

    ends = _is_first_block(blk) if reverse else _is_last_block(blk)

    @pl.when(jnp.logical_and(ends, is_prompt))
    def _():
        for h in range(GLA_HEADS):
            sfin_ref[h] = st_ref[h].T


def _gla(alr, q, k, v, wa, ba, state_in, state_out, layer, *, reverse, o_fwd=None, r=None, gn=None,
         outproj=None):
    bmap = (lambda j: NBLK - 1 - j) if reverse else (lambda j: j)
    direction = 1 if reverse else 0
    qd, vd = GLA_HEADS * GLA_DK, GLA_HEADS * GLA_DV
    row_spec = lambda width: pl.BlockSpec((RB, width), lambda j: (bmap(j), 0))
    state_blk = (None, None, None, GLA_HEADS, GLA_DK, GLA_DV)
    in_specs = [row_spec(2 * GLA_RANK), row_spec(qd), row_spec(qd), row_spec(vd),
                pl.BlockSpec((2 * GLA_RANK, qd), lambda j: (0, 0)),
                pl.BlockSpec((1, qd), lambda j: (0, 0)),
                pl.BlockSpec(state_blk, lambda j: (_sample_seq(bmap(j)), layer, direction, 0, 0, 0)),
                pl.BlockSpec(memory_space=pl.ANY)]
    args = [alr, q, k, v, wa, ba, state_in, state_out]
    scratch = [pltpu.VMEM((GLA_HEADS, GLA_DV, GLA_DK), F32), pltpu.VMEM((RB, qd), F32),
               pltpu.VMEM((RB, RB), BF16)]
    if reverse:
        in_specs += [row_spec(vd), row_spec(vd), pl.BlockSpec((1, vd), lambda j: (0, 0))]
        args += [o_fwd, r, gn.reshape(1, vd)]
        scratch += [pltpu.VMEM((RB, vd), F32)]
    call = dict(kernel=functools.partial(_gla_kernel, reverse=reverse), in_specs=in_specs, args=args,
                out_specs=[row_spec(vd),
                           pl.BlockSpec(state_blk, lambda j: (_prompt_seq(bmap(j)), layer, direction, 0, 0, 0))],
                out_shape=[jax.ShapeDtypeStruct((NT, vd), BF16 if reverse else F32),
                           jax.ShapeDtypeStruct(state_out.shape, F32)],
                scratch=scratch)
    state_output = 1
    if outproj is not None:
        assert reverse
        call = _with_outproj(call["kernel"], in_specs, args, call["out_specs"], call["out_shape"], scratch,
                             [], outproj)
        state_output = 0
    return pl.pallas_call(
        call["kernel"],
        grid=(NBLK,),
        in_specs=call["in_specs"],
        out_specs=call["out_specs"],
        out_shape=call["out_shape"],
        input_output_aliases={7: state_output},
        scratch_shapes=call["scratch"],
        compiler_params=_params(),
        name="gla_bwd" if reverse else "gla_fwd",
    )(*call["args"])


OUT_SUB = 128


def _outproj_kernel(*refs, n_mix):
    mix_refs = refs[:n_mix]
    (w_ref, x_ref, gate_ref, g2_ref, shift_ref, scale_ref, rw_ref, rb_ref,
     x1_ref, h2_ref, logit_ref, w_bf) = refs[n_mix:]
    step = pl.program_id(0)

    @pl.when(step == 0)
    def _():
        w_bf[...] = w_ref[...].astype(BF16)

    row = _mod_row(NBLK - 1 - step)
    r_hi, r_lo = _split2(rw_ref[...])
    groups = [slice(p * OUT_SUB, (p + 1) * OUT_SUB) for p in range(RB // OUT_SUB)]
    mixed = []
    for rows in groups:
        m, off = None, 0
        for mix_ref in mix_refs:
            width = mix_ref.shape[1]
            part = _dot(mix_ref[rows, :], w_bf[off:off + width, :])
            m = part if m is None else m + part
            off += width
        mixed.append(m)
    normed = []
    for rows, m in zip(groups, mixed):
        x1 = x_ref[rows, :] + gate_ref[pl.ds(row, 1), :] * m
        x1_ref[rows, :] = x1
        h2 = _norm_mod(x1, g2_ref, shift_ref, scale_ref, row)
        h2_ref[rows, :] = _pack_rows(h2)
        normed.append(_split2(h2))
    for rows, (h_hi, h_lo) in zip(groups, normed):
        logit_ref[:, rows] = _dot_nt(r_hi, h_hi) + (_dot_nt(r_hi, h_lo) + _dot_nt(r_lo, h_hi)) + rb_ref[...]


def _with_outproj(scan_body, in_specs, args, out_specs, out_shape, scratch, ext_mixes, outproj):
    w_all, index, x, gate, g2, shift, scale, rw, rb = outproj
    blk = lambda j: NBLK - 1 - j
    row_spec = lambda width: pl.BlockSpec((RB, width), lambda j: (blk(j), 0))
    full = lambda shape: pl.BlockSpec(shape, lambda j: (0,) * len(shape))
    mod_spec = full((MOD_ROWS, D_MODEL))
    n_mixed = w_all.shape[1]
    o_in = ([row_spec(m.shape[1]) for m in ext_mixes]
            + [pl.BlockSpec((None, n_mixed, D_MODEL), lambda j: (index, 0, 0)),
               row_spec(D_MODEL), mod_spec, full((1, D_MODEL)), mod_spec, mod_spec,
               full((N_EXPERTS, D_MODEL)), full((N_EXPERTS, 1))])
    o_args = list(ext_mixes) + [w_all, x, gate, g2.reshape(1, D_MODEL), shift, scale, rw.T,
                                rb.reshape(N_EXPERTS, 1)]
    o_out = [row_spec(D_MODEL), row_spec(D_MODEL // 2), pl.BlockSpec((N_EXPERTS, RB), lambda j: (0, blk(j)))]
    o_shape = [jax.ShapeDtypeStruct((NT, D_MODEL), F32), jax.ShapeDtypeStruct((NT, D_MODEL // 2), jnp.uint32),
               jax.ShapeDtypeStruct((N_EXPERTS, NT), F32)]
    n_si, n_oi, n_oo, n_ss = len(in_specs), len(o_in), len(o_out), len(scratch)
    mix = out_shape[0]

    def fused(*refs):
        scan_in, out_in = refs[:n_si], refs[n_si:n_si + n_oi]
        sfin_ref, out_out = refs[n_si + n_oi], refs[n_si + n_oi + 1:n_si + n_oi + 1 + n_oo]
        rest = refs[n_si + n_oi + 1 + n_oo:]
        scan_scr, mix_ref, w_bf = rest[:n_ss], rest[n_ss], rest[n_ss + 1]
        scan_body(*scan_in, mix_ref, sfin_ref, *scan_scr)
        _outproj_kernel(mix_ref, *out_in, *out_out, w_bf, n_mix=1 + len(ext_mixes))

    return dict(kernel=fused, in_specs=in_specs + o_in, args=args + o_args,
                out_specs=out_specs[1:] + o_out, out_shape=out_shape[1:] + o_shape,
                scratch=scratch + [pltpu.VMEM((RB, mix.shape[1]), mix.dtype),
                                   pltpu.VMEM((n_mixed, D_MODEL), BF16)])


ROUTE_BLK = 2048
ROUTE_SUB = 256


def _route_kernel(lg_ref, idx_ref, rank_ref, gt_ref, cnt_ref, carry_ref):
    @pl.when(pl.program_id(0) == 0)
    def _():
        carry_ref[...] = jnp.zeros_like(carry_ref)

    logits = lg_ref[...]
    eid = lax.broadcasted_iota(jnp.int32, logits.shape, 0).astype(F32)
    work = logits
    onehots, top_vals = [], []
    for kk in range(TOP_K):
        top = jnp.max(work, axis=0, keepdims=True)
        first = jnp.min(jnp.where(work == top, eid, float(N_EXPERTS)), axis=0, keepdims=True)
        onehot = eid == first
        idx_ref[kk:kk + 1, :] = first.astype(jnp.int32)
        onehots.append(onehot)
        top_vals.append(top)
        work = jnp.where(onehot, -jnp.inf, work)
    exps = [jnp.exp(v - top_vals[0]) for v in top_vals]
    denom = exps[0]
    for e in exps[1:]:
        denom = denom + e
    gt_ref[...] = jnp.zeros_like(gt_ref)
    for kk in range(TOP_K):
        gt_ref[kk:kk + 1, :] = exps[kk] / denom

    sel = jnp.zeros(logits.shape, F32)
    for onehot in onehots:
        sel = sel + jnp.where(onehot, 1.0, 0.0)
    sel = sel.astype(BF16)
    r_i = lax.broadcasted_iota(jnp.int32, (ROUTE_SUB, ROUTE_SUB), 0)
    c_i = lax.broadcasted_iota(jnp.int32, (ROUTE_SUB, ROUTE_SUB), 1)
    before = jnp.where(r_i < c_i, 1.0, 0.0).astype(BF16)
    ones = jnp.ones((ROUTE_SUB, ROUTE_SUB), BF16)
    carry = carry_ref[...]
    for s in range(ROUTE_BLK // ROUTE_SUB):
        cols = slice(s * ROUTE_SUB, (s + 1) * ROUTE_SUB)
        pos = _dot(sel[:, cols], before) + carry
        for kk in range(TOP_K):
            rank = jnp.sum(jnp.where(onehots[kk][:, cols], pos, 0.0), axis=0, keepdims=True)
            rank_ref[kk:kk + 1, cols] = rank.astype(jnp.int32)
        carry = carry + _dot(sel[:, cols], ones)
    carry_ref[...] = carry
    cnt_ref[...] = carry.astype(jnp.int32)


def _route(logits_t):
    col_spec = lambda rows: pl.BlockSpec((rows, ROUTE_BLK), lambda i: (0, i))
    return pl.pallas_call(
        _route_kernel,
        grid=(NT // ROUTE_BLK,),
        in_specs=[col_spec(N_EXPERTS)],
        out_specs=[col_spec(TOP_K), col_spec(TOP_K), col_spec(8),
                   pl.BlockSpec((N_EXPERTS, ROUTE_SUB), lambda i: (0, 0))],
        out_shape=[jax.ShapeDtypeStruct((TOP_K, NT), jnp.int32), jax.ShapeDtypeStruct((TOP_K, NT), jnp.int32),
                   jax.ShapeDtypeStruct((8, NT), F32), jax.ShapeDtypeStruct((N_EXPERTS, ROUTE_SUB), jnp.int32)],
        scratch_shapes=[pltpu.VMEM((N_EXPERTS, ROUTE_SUB), F32)],
        compiler_params=_params(),
        name="route",
    )(logits_t)


TM = 1024
TM_SUB = 256
MOE_NBLK = NT * TOP_K // TM + N_EXPERTS
MOE_ROWS = MOE_NBLK * TM
HALF = D_MODEL // 2


def _moe_kernel(be_ref, nv_ref, nx_ref, x_ref, wgu_hbm, bgu_ref, wd_hbm, bd_ref, y_ref,
                wgu_st, wd_st, wgu_bf, wd_bf, sems, *, layer):
    i = pl.program_id(0)
    n_valid = nv_ref[i]

    def fetch(e):
        return (pltpu.make_async_copy(wgu_hbm.at[layer, e], wgu_st, sems.at[0]),
                pltpu.make_async_copy(wd_hbm.at[layer, e], wd_st, sems.at[1]))

    @pl.when(i == 0)
    def _():
        for cp in fetch(be_ref[0]):
            cp.start()

    @pl.when(n_valid > 0)
    def _():
        e = be_ref[i]
        changed = jnp.logical_or(i == 0, e != be_ref[jnp.maximum(i - 1, 0)])

        @pl.when(changed)
        def _():
            for cp in fetch(e):
                cp.wait()
            wgu_bf[...] = wgu_st[...].astype(BF16)
            wd_bf[...] = wd_st[...].astype(BF16)
            nxt = nx_ref[e]

            @pl.when(nxt >= 0)
            def _():
                for cp in fetch(nxt):
                    cp.start()

    def ffn_pass(p, masked):
        rows = slice(p * TM_SUB, (p + 1) * TM_SUB)
        x = x_ref[rows, :]
        if masked:
            row_id = lax.broadcasted_iota(jnp.int32, (TM_SUB, 1), 0) + p * TM_SUB
            x = jnp.where(row_id < n_valid, x, jnp.uint32(0))
        x_lo, x_hi = _unpack_rows(x)
        x_lo, x_hi = x_lo.astype(BF16), x_hi.astype(BF16)
        yield
        gu = _dot(x_lo, wgu_bf[:HALF, :]) + _dot(x_hi, wgu_bf[HALF:, :]) + bgu_ref[...]
        yield
        gate = jnp.minimum(gu[:, :D_FF], SWIGLU_LIMIT)
        up = jnp.clip(gu[:, D_FF:], -SWIGLU_LIMIT, SWIGLU_LIMIT)
        hdn = (gate * _sigmoid(SWIGLU_ALPHA * gate) * (up + 1.0)).astype(BF16)
        yield
        y = _dot(hdn, wd_bf[...]) + bd_ref[...]
        yield
        y_ref[rows, :] = _pack_rows(y)

    @pl.when(n_valid == TM)
    def _():
        passes = [ffn_pass(p, masked=False) for p in range(TM // TM_SUB)]
        n_stages = 5
        for t in range(n_stages + len(passes) - 1):
            for lag, body in enumerate(passes):
                if 0 <= t - lag < n_stages:
                    next(body, None)

    for p in range(TM // TM_SUB):
        @pl.when(jnp.logical_and(n_valid < TM, n_valid > p * TM_SUB))
        def _():
            for _ in ffn_pass(p, masked=True):
                pass


def _moe_experts(layer, block_e, n_valid, next_e, xs, w_gu, b_gu, w_down, b_down):
    grid_spec = pltpu.PrefetchScalarGridSpec(
        num_scalar_prefetch=3,
        grid=(MOE_NBLK,),
        in_specs=[pl.BlockSpec((TM, HALF), lambda i, be, nv, nx: (i, 0)),
                  pl.BlockSpec(memory_space=pl.ANY),
                  pl.BlockSpec((None, None, 1, 2 * D_FF), lambda i, be, nv, nx: (layer, be[i], 0, 0)),
                  pl.BlockSpec(memory_space=pl.ANY),
                  pl.BlockSpec((None, None, 1, D_MODEL), lambda i, be, nv, nx: (layer, be[i], 0, 0))],
        out_specs=pl.BlockSpec((TM, HALF), lambda i, be, nv, nx: (i, 0)),
        scratch_shapes=[pltpu.VMEM((D_MODEL, 2 * D_FF), F32), pltpu.VMEM((D_FF, D_MODEL), F32),
                        pltpu.VMEM((D_MODEL, 2 * D_FF), BF16), pltpu.VMEM((D_FF, D_MODEL), BF16),
                        pltpu.SemaphoreType.DMA((2,))],
    )
    return pl.pallas_call(
        functools.partial(_moe_kernel, layer=layer),
        grid_spec=grid_spec,
        out_shape=jax.ShapeDtypeStruct((MOE_ROWS, HALF), jnp.uint32),
        compiler_params=_params(),
        name="moe_experts",
    )(block_e, n_valid, next_e, xs, w_gu, b_gu.reshape(DEPTH, N_EXPERTS, 1, -1), w_down,
      b_down.reshape(DEPTH, N_EXPERTS, 1, -1))


SC_WORKERS = 32
SC_WIN = 64


def _sc_mesh():
    return plsc.VectorSubcoreMesh(core_axis_name="core", subcore_axis_name="subcore")


def _sc_worker():
    return lax.axis_index("core") * (SC_WORKERS // 2) + lax.axis_index("subcore")


def _sc_scatter_rows(x, dest_t, n_rows):
    n, width = x.shape
    kk = dest_t.shape[0]
    per = n // SC_WORKERS
    n_win = per // SC_WIN
    assert per * SC_WORKERS == n and n_win * SC_WIN == per and n_win % 2 == 0

    @pl.kernel(out_type=jax.ShapeDtypeStruct((n_rows, width), x.dtype), mesh=_sc_mesh(),
               scratch_types=[pltpu.VMEM((kk, per), jnp.int32), pltpu.VMEM((SC_WIN, width), x.dtype),
                              pltpu.VMEM((SC_WIN, width), x.dtype), pltpu.SemaphoreType.DMA((4,))])
    def scatter(x_hbm, i_hbm, o_hbm, idx_v, buf0, buf1, sems):
        base = _sc_worker() * per
        pltpu.sync_copy(i_hbm.at[:, pl.ds(base, per)], idx_v)

        def get(j, buf, s):
            return pltpu.make_async_copy(x_hbm.at[pl.ds(base + j * SC_WIN, SC_WIN)], buf, sems.at[s])

        def put(j, q, buf, s):
            return pltpu.make_async_copy(buf, o_hbm.at[idx_v.at[q, pl.ds(j * SC_WIN, SC_WIN)]], sems.at[s])

        get(0, buf0, 0).start()

        @pl.loop(0, n_win, step=2)
        def _(j):
            get(j, buf0, 0).wait()

            @pl.when(j > 0)
            def _():
                for q in range(kk):
                    put(j - 1, q, buf1, 3).wait()

            get(j + 1, buf1, 1).start()
            for q in range(kk):
                put(j, q, buf0, 2).start()
            get(j + 1, buf1, 1).wait()
            for q in range(kk):
                put(j, q, buf0, 2).wait()

            @pl.when(j + 2 < n_win)
            def _():
                get(j + 2, buf0, 0).start()

            for q in range(kk):
                put(j + 1, q, buf1, 3).start()

        for q in range(kk):
            put(n_win - 1, q, buf1, 3).wait()

    return scatter(x, dest_t)


def _sc_gather_rows(y, idx):
    n = idx.shape[0]
    width = y.shape[1]
    per = n // SC_WORKERS
    n_win = per // SC_WIN
    assert per * SC_WORKERS == n and n_win * SC_WIN == per and n_win % 2 == 0

    @pl.kernel(out_type=jax.ShapeDtypeStruct((n, width), y.dtype), mesh=_sc_mesh(),
               scratch_types=[pltpu.VMEM((per,), jnp.int32), pltpu.VMEM((SC_WIN, width), y.dtype),
                              pltpu.VMEM((SC_WIN, width), y.dtype), pltpu.SemaphoreType.DMA((4,))])
    def gather(y_hbm, i_hbm, o_hbm, idx_v, buf0, buf1, sems):
        base = _sc_worker() * per
        pltpu.sync_copy(i_hbm.at[pl.ds(base, per)], idx_v)

        def get(j, buf, s):
            return pltpu.make_async_copy(y_hbm.at[idx_v.at[pl.ds(j * SC_WIN, SC_WIN)]], buf, sems.at[s])

        def put(j, buf, s):
            return pltpu.make_async_copy(buf, o_hbm.at[pl.ds(base + j * SC_WIN, SC_WIN)], sems.at[s])

        get(0, buf0, 0).start()

        @pl.loop(0, n_win, step=2)
        def _(j):
            get(j, buf0, 0).wait()

            @pl.when(j > 0)
            def _():
                put(j - 1, buf1, 3).wait()

            get(j + 1, buf1, 1).start()
            put(j, buf0, 2).start()
            get(j + 1, buf1, 1).wait()
            put(j, buf0, 2).wait()

            @pl.when(j + 2 < n_win)
            def _():
                get(j + 2, buf0, 0).start()

            put(j + 1, buf1, 3).start()

        put(n_win - 1, buf1, 3).wait()

    return gather(y, idx)


def _gate_columns(gt_ref):
    r_i = lax.broadcasted_iota(jnp.int32, (RB, RB), 0)
    c_i = lax.broadcasted_iota(jnp.int32, (RB, RB), 1)
    eye = jnp.where(r_i == c_i, 1.0, 0.0).astype(BF16)
    g1, g2, g3 = _split3(gt_ref[...])
    return _dot_nt(eye, g1) + (_dot_nt(eye, g2) + _dot_nt(eye, g3))


def _combined_rows(x1_ref, yg_refs, gw, gate_ref, row, rows=slice(None)):
    acc_lo, acc_hi = None, None
    for kk in range(TOP_K):
        y_lo, y_hi = _unpack_rows(yg_refs[kk][rows, :])
        w = gw[rows, kk:kk + 1]
        acc_lo = y_lo * w if acc_lo is None else acc_lo + y_lo * w
        acc_hi = y_hi * w if acc_hi is None else acc_hi + y_hi * w
    x_lo = x1_ref[rows, :HALF] + gate_ref[pl.ds(row, 1), :HALF] * acc_lo
    x_hi = x1_ref[rows, HALF:] + gate_ref[pl.ds(row, 1), HALF:] * acc_hi
    return x_lo, x_hi


def _combine_specs():
    return [pl.BlockSpec(memory_space=pl.ANY), pl.BlockSpec(memory_space=pl.ANY),
            pl.BlockSpec((8, RB), lambda i: (0, i)), pl.BlockSpec((MOD_ROWS, D_MODEL), lambda i: (0, 0))]


RING = 3


def _ring_scratch():
    return [pltpu.VMEM((RING, RB, D_MODEL), F32), pltpu.VMEM((RING, TOP_K, RB, HALF), jnp.uint32),
            pltpu.SemaphoreType.DMA((RING, 1 + TOP_K))]


def _ring_rows(x1_hbm, yg_hbm, x1_buf, yg_buf, sems):
    i = pl.program_id(0)

    def fetch(blk):
        slot = blk % RING
        row0 = pl.multiple_of(blk * RB, RB)
        copies = [pltpu.make_async_copy(x1_hbm.at[pl.ds(row0, RB), :], x1_buf.at[slot], sems.at[slot, 0])]
        for kk in range(TOP_K):
            copies.append(pltpu.make_async_copy(yg_hbm.at[pl.ds(kk * NT + row0, RB), :], yg_buf.at[slot, kk],
                                                sems.at[slot, 1 + kk]))
        return copies

    @pl.when(i == 0)
    def _():
        for b in range(RING - 1):
            for cp in fetch(b):
                cp.start()

    @pl.when(i + RING - 1 < NBLK)
    def _():
        for cp in fetch(i + RING - 1):
            cp.start()

    for cp in fetch(i):
        cp.wait()
    slot = i % RING
    return x1_buf.at[slot], [yg_buf.at[slot, kk] for kk in range(TOP_K)]


def _combine_final_kernel(x1_hbm, yg_hbm, gt_ref, gate_ref, fg_ref, op_ref, os_ref, x1_buf, yg_buf, sems):
    i = pl.program_id(0)
    x1_ref, yg_refs = _ring_rows(x1_hbm, yg_hbm, x1_buf, yg_buf, sems)
    x_lo, x_hi = _combined_rows(x1_ref, yg_refs, _gate_columns(gt_ref), gate_ref, _mod_row(i))
    ms = (jnp.sum(x_lo * x_lo, axis=-1, keepdims=True) + jnp.sum(x_hi * x_hi, axis=-1, keepdims=True)) / D_MODEL
    scale = lax.rsqrt(ms + EPS)

    @pl.when(i < N_PROMPT_BLK)
    def _():
        op_ref[:, :HALF] = x_lo * scale * fg_ref[:, :HALF]
        op_ref[:, HALF:] = x_hi * scale * fg_ref[:, HALF:]

    @pl.when(i >= N_PROMPT_BLK)
    def _():
        os_ref[:, :HALF] = x_lo * scale * fg_ref[:, :HALF]
        os_ref[:, HALF:] = x_hi * scale * fg_ref[:, HALF:]


def _combine_final(x1, yg, gates_t, gate, final_g):
    return pl.pallas_call(
        _combine_final_kernel,
        grid=(NBLK,),
        in_specs=_combine_specs() + [pl.BlockSpec((1, D_MODEL), lambda i: (0, 0))],
        out_specs=[pl.BlockSpec((RB, D_MODEL), lambda i: (jnp.minimum(i, N_PROMPT_BLK - 1), 0)),
                   pl.BlockSpec((RB, D_MODEL), lambda i: (jnp.maximum(i - N_PROMPT_BLK, 0), 0))],
        out_shape=[jax.ShapeDtypeStruct((NT_PROMPT, D_MODEL), F32),
                   jax.ShapeDtypeStruct((NT - NT_PROMPT, D_MODEL), F32)],
        scratch_shapes=_ring_scratch(),
        compiler_params=_params(),
        name="moe_combine_final",
    )(x1, yg, gates_t, gate, final_g.reshape(1, D_MODEL))


def _routing_plan(counts, idx_t, rank_t):
    counts = counts[:, 0]
    padded = (counts + TM - 1) // TM * TM
    pad_end = jnp.cumsum(padded)
    pad_start = pad_end - padded
    blk_row = (jnp.arange(MOE_NBLK, dtype=jnp.int32) * TM)[:, None]
    ids = jnp.arange(N_EXPERTS, dtype=jnp.int32)
    owns = jnp.logical_and(pad_start[None, :] <= blk_row, blk_row < pad_end[None, :])
    last_used = jnp.max(jnp.where(counts > 0, ids, 0))
    block_e = jnp.where(jnp.any(owns, axis=1), jnp.sum(jnp.where(owns, ids[None, :], 0), axis=1), last_used)
    block_e = block_e.astype(jnp.int32)
    left = jnp.clip(counts[None, :] - (blk_row - pad_start[None, :]), 0, TM)
    n_valid = jnp.sum(jnp.where(owns, left, 0), axis=1).astype(jnp.int32)
    start = jnp.zeros(idx_t.shape, jnp.int32)
    for e in range(N_EXPERTS):
        start = jnp.where(idx_t == e, pad_start[e], start)
    dest_t = (start + rank_t).astype(jnp.int32)
    later = jnp.where(jnp.logical_and(counts[None, :] > 0, ids[None, :] > ids[:, None]), ids[None, :], N_EXPERTS)
    next_e = jnp.min(later, axis=1)
    next_e = jnp.where(next_e == N_EXPERTS, -1, next_e).astype(jnp.int32)
    return block_e, n_valid, next_e, dest_t


def _rope_tables():
    rows = DEC_SEQ // GRID_W
    row = jnp.repeat(jnp.arange(rows, dtype=F32), GRID_W)
    col = jnp.tile(jnp.arange(GRID_W, dtype=F32), rows)
    n_f = RET_DK // 4
    freqs = ROPE_THETA ** (-jnp.arange(n_f, dtype=F32) / n_f)
    ang = jnp.concatenate([row[:, None] * freqs, col[:, None] * freqs], axis=-1)
    cos = jnp.repeat(jnp.cos(ang), 2, axis=-1)
    sin = jnp.repeat(jnp.sin(ang), 2, axis=-1) * jnp.tile(jnp.asarray([-1.0, 1.0], F32), RET_DK // 2)
    cos = jnp.concatenate([jnp.ones((RB, RET_DK), F32), cos], axis=0)
    sin = jnp.concatenate([jnp.zeros((RB, RET_DK), F32), sin], axis=0)
    return jnp.tile(cos, (1, 2)), jnp.tile(sin, (1, 2))


def kernel(x_prompt, x_sample, state_ret, state_gla, c, c_ctx, w_mod, b_mod, norm1_g, norm2_g, final_g, even_w_in, ret_decay, ret_gn, conv_w, conv_b, conv_ln_g, conv_ln_b, even_w_out, odd_w_in, gla_w_a2, gla_b_a2, gla_gn, odd_w_out, router_w, router_b, exp_w_gu, exp_b_gu, exp_w_down, exp_b_down):
    x_src = ("split", x_prompt.reshape(NT_PROMPT, D_MODEL), x_sample.reshape(NT - NT_PROMPT, D_MODEL))
    cvec = jnp.concatenate([c_ctx[None, :], c, jnp.zeros((MOD_ROWS - 1 - DEC_BATCH, D_MODEL), F32)], axis=0)
    mods = _modulation(cvec, w_mod, b_mod).reshape(DEPTH, MOD_ROWS, N_MOD, D_MODEL)
    cos_tab, sin_tab = _rope_tables()
    new_ret = jnp.zeros((BATCH,) + state_ret.shape[1:], F32)
    new_gla = jnp.zeros((BATCH,) + state_gla.shape[1:], F32)
    for l in range(DEPTH):
        mod = [mods[l, :, j, :] for j in range(N_MOD)]
        if l % 2 == 0:
            e = l // 2
            qd, vd = RET_HEADS * RET_DK, RET_HEADS * RET_DV
            x, q, k, v, g, a, ga = _inproj(x_src, norm1_g[l], mod[0], mod[1], even_w_in, e,
                                           (qd, qd, vd, vd, CONV_CH, CONV_CH))
            conv = (a, ga, conv_w[e], conv_b[e], conv_ln_g[e], conv_ln_b[e])
            o_f, new_ret, u = _retention(ret_decay[e], q, k, v, cos_tab, sin_tab, state_ret, new_ret, e,
                                         reverse=False, conv=conv)
            outproj = (even_w_out, e, x, mod[2], norm2_g[l], mod[3], mod[4], router_w[l], router_b[l])
            new_ret, x1, h2, logits_t = _retention(ret_decay[e], q, k, v, cos_tab, sin_tab, state_ret, new_ret, e,
                                                   reverse=True, o_fwd=o_f, g=g, gn=ret_gn[e], u=u,
                                                   outproj=outproj)
        else:
            o = l // 2
            qd, vd = GLA_HEADS * GLA_DK, GLA_HEADS * GLA_DV
            x, q, k, v, r, alr = _inproj(x_src, norm1_g[l], mod[0], mod[1], odd_w_in, o,
                                         (qd, qd, vd, vd, 2 * GLA_RANK))
            zeros = jnp.zeros((GLA_RANK, qd), F32)
            wa_f = jnp.concatenate([gla_w_a2[o, 0], zeros], axis=0)
            wa_b = jnp.concatenate([zeros, gla_w_a2[o, 1]], axis=0)
            o_f, new_gla = _gla(alr, q, k, v, wa_f, gla_b_a2[o, 0].reshape(1, qd), state_gla, new_gla, o,
                                reverse=False)
            outproj = (odd_w_out, o, x, mod[2], norm2_g[l], mod[3], mod[4], router_w[l], router_b[l])
            new_gla, x1, h2, logits_t = _gla(alr, q, k, v, wa_b, gla_b_a2[o, 1].reshape(1, qd), state_gla, new_gla,
                                             o, reverse=True, o_fwd=o_f, r=r, gn=gla_gn[o], outproj=outproj)
        idx_t, rank_t, gates_t, counts = _route(logits_t)
        block_e, n_valid, next_e, dest_t = _routing_plan(counts, idx_t, rank_t)
        xs = _sc_scatter_rows(h2, dest_t, MOE_ROWS)
        yb = _moe_experts(l, block_e, n_valid, next_e, xs, exp_w_gu, exp_b_gu, exp_w_down, exp_b_down)
        yg = _sc_gather_rows(yb, dest_t.reshape(TOP_K * NT))
        x_src = ("moe", x1, yg, gates_t, mod[5])
    y_prompt, y_sample = _combine_final(*x_src[1:], final_g)
    y_prompt = y_prompt.reshape(BATCH, SEQ, D_MODEL)
    y_sample = y_sample.reshape(DEC_BATCH, DEC_SEQ, D_MODEL)
    return (y_prompt, y_sample, new_ret, new_gla)
```

```python
import functools

import jax
import jax.numpy as jnp
from jax import lax
from jax.experimental import pallas as pl
from jax.experimental.pallas import tpu as pltpu
from jax.experimental.pallas import tpu_sc as plsc

F32 = jnp.float32
BF16 = jnp.bfloat16

D_MODEL = 1024
BATCH = 16
SEQ = 256
DEPTH = 4
DEC_BATCH = 4
DEC_SEQ = 4096
GRID_W = 64
RET_HEADS = 4
RET_DK = 64
RET_DV = 128
RET_CHUNK = 128
CONV_CH = 512
CONV_WIDTH = 31
CONV_PAD = CONV_WIDTH // 2
GLA_HEADS = 4
GLA_DK = 128
GLA_DV = 256
GLA_RANK = 16
GLA_TAU = 16.0
GLA_CHUNK = 64
GLA_SUB = 16
N_EXPERTS = 32
TOP_K = 4
D_FF = 1024
SWIGLU_LIMIT = 7.0
SWIGLU_ALPHA = 1.702
ROPE_THETA = 10000.0
EPS = 1e-6
N_MOD = 6

RB = 256
NT_PROMPT = BATCH * SEQ
NT = NT_PROMPT + DEC_BATCH * DEC_SEQ
NBLK = NT // RB
N_PROMPT_BLK = NT_PROMPT // RB
SAMPLE_BLK = DEC_SEQ // RB
MOD_ROWS = 8
HALO = 16
VMEM_LIMIT = 48 * 1024 * 1024

assert SEQ == RB and DEC_SEQ % RB == 0 and CONV_PAD < HALO


def _seq_of_block(i):
    return jnp.where(i < N_PROMPT_BLK, i, N_PROMPT_BLK + (i - N_PROMPT_BLK) // SAMPLE_BLK)


def _is_first_block(i):
    return jnp.logical_or(i < N_PROMPT_BLK, (i - N_PROMPT_BLK) % SAMPLE_BLK == 0)


def _is_last_block(i):
    return jnp.logical_or(i < N_PROMPT_BLK, (i - N_PROMPT_BLK) % SAMPLE_BLK == SAMPLE_BLK - 1)


def _mod_row(i):
    return jnp.where(i < N_PROMPT_BLK, 0, 1 + (i - N_PROMPT_BLK) // SAMPLE_BLK)


def _rope_block(i):
    return jnp.where(i < N_PROMPT_BLK, 0, 1 + (i - N_PROMPT_BLK) % SAMPLE_BLK)


def _dot(a, b):
    return jnp.dot(a, b, preferred_element_type=F32)


def _dot_nt(a, b):
    return lax.dot_general(a, b, (((1,), (1,)), ((), ())), preferred_element_type=F32)


def _dot_tn(a, b):
    return lax.dot_general(a, b, (((0,), (0,)), ((), ())), preferred_element_type=F32)


def _split2(a):
    hi = a.astype(BF16)
    lo = (a - hi.astype(F32)).astype(BF16)
    return hi, lo


def _dot_hi(a, b):
    a_hi, a_lo = _split2(a)
    b_hi, b_lo = _split2(b)
    return _dot(a_hi, b_hi) + (_dot(a_hi, b_lo) + _dot(a_lo, b_hi))


def _silu(x):
    return x * (1.0 / (1.0 + jnp.exp(-x)))


def _sigmoid(x):
    return 1.0 / (1.0 + jnp.exp(-x))


def _pack_rows(x):
    n = x.shape[1] // 2
    lo = pltpu.bitcast(x[:, :n].astype(BF16).astype(F32), jnp.uint32)
    hi = pltpu.bitcast(x[:, n:].astype(BF16).astype(F32), jnp.uint32)
    return hi | (lo >> 16)


def _unpack_rows(u):
    lo = pltpu.bitcast(u << 16, F32)
    hi = pltpu.bitcast(u & jnp.uint32(0xFFFF0000), F32)
    return lo, hi


def _params(n_axes=1, vmem=VMEM_LIMIT):
    return pltpu.CompilerParams(dimension_semantics=("arbitrary",) * n_axes, vmem_limit_bytes=vmem)


MOD_TN = 1536


def _mod_kernel(c_ref, w_ref, b_ref, o_ref):
    s = _silu(c_ref[...]).astype(BF16)
    o_ref[...] = _dot(s, w_ref[...].astype(BF16)) + b_ref[...]


def _modulation(cvec, w_mod, b_mod):
    n = N_MOD * D_MODEL
    return pl.pallas_call(
        _mod_kernel,
        grid=(DEPTH, n // MOD_TN),
        in_specs=[pl.BlockSpec((MOD_ROWS, D_MODEL), lambda l, j: (0, 0)),
                  pl.BlockSpec((None, D_MODEL, MOD_TN), lambda l, j: (l, 0, j)),
                  pl.BlockSpec((None, 1, MOD_TN), lambda l, j: (l, 0, j))],
        out_specs=pl.BlockSpec((None, MOD_ROWS, MOD_TN), lambda l, j: (l, 0, j)),
        out_shape=jax.ShapeDtypeStruct((DEPTH, MOD_ROWS, n), F32),
        compiler_params=_params(2),
        name="modulation",
    )(cvec, w_mod, b_mod.reshape(DEPTH, 1, n))


def _norm_mod(x, g_ref, shift_ref, scale_ref, row):
    y = x * lax.rsqrt(jnp.mean(x * x, axis=-1, keepdims=True) + EPS) * g_ref[...]
    return y * (1.0 + scale_ref[pl.ds(row, 1), :]) + shift_ref[pl.ds(row, 1), :]


N_SRC = {"split": 2, "moe": 4}


def _inproj_kernel(*refs, widths, source):
    n_src = N_SRC[source]
    src = refs[:n_src]
    g_ref, shift_ref, scale_ref, w_ref = refs[n_src:n_src + 4]
    if source == "moe":
        outs, w_bf, ring = refs[n_src + 4:-4], refs[-4], refs[-3:]
    else:
        outs, w_bf = refs[n_src + 4:-1], refs[-1]
    step = pl.program_id(0)

    @pl.when(step == 0)
    def _():
        w_bf[...] = w_ref[...].astype(BF16)

    row = _mod_row(step)
    if source == "split":
        x = jnp.where(step < N_PROMPT_BLK, src[0][...], src[1][...])
    else:
        x1_hbm, yg_hbm, gt_ref, gate_ref = src
        x1_ref, yg_refs = _ring_rows(x1_hbm, yg_hbm, *ring)
        x = jnp.concatenate(_combined_rows(x1_ref, yg_refs, _gate_columns(gt_ref), gate_ref, row), axis=1)
    outs[0][...] = x
    hb = _norm_mod(x, g_ref, shift_ref, scale_ref, row).astype(BF16)
    off = 0
    for o_ref, width in zip(outs[1:], widths):
        o_ref[...] = _dot(hb, w_bf[:, off:off + width])
        off += width


def _inproj(x, g, shift, scale, w_all, index, widths):
    source, x_args = x[0], list(x[1:])
    n_in = w_all.shape[2]
    row_spec = lambda width: pl.BlockSpec((RB, width), lambda i: (i, 0))
    full = lambda shape: pl.BlockSpec(shape, lambda i: (0,) * len(shape))
    if source == "split":
        x_specs = [pl.BlockSpec((RB, D_MODEL), lambda i: (jnp.minimum(i, N_PROMPT_BLK - 1), 0)),
                   pl.BlockSpec((RB, D_MODEL), lambda i: (jnp.maximum(i - N_PROMPT_BLK, 0), 0))]
    else:
        x_specs = _combine_specs()
    scratch = [pltpu.VMEM((D_MODEL, n_in), BF16)] + (_ring_scratch() if source == "moe" else [])
    out_widths = (D_MODEL,) + tuple(widths)
    return pl.pallas_call(
        functools.partial(_inproj_kernel, widths=widths, source=source),
        grid=(NBLK,),
        in_specs=x_specs + [full((1, D_MODEL)), full((MOD_ROWS, D_MODEL)), full((MOD_ROWS, D_MODEL)),
                            pl.BlockSpec((None, D_MODEL, n_in), lambda i: (index, 0, 0))],
        out_specs=[row_spec(width) for width in out_widths],
        out_shape=[jax.ShapeDtypeStruct((NT, width), F32) for width in out_widths],
        scratch_shapes=scratch,
        compiler_params=_params(),
        name="inproj",
    )(*x_args, g.reshape(1, D_MODEL), shift, scale, w_all)


RC = RET_CHUNK
RET_PAIR = 2 * RET_DK


def _rope(x, cos, sin_signed):
    lane = lax.broadcasted_iota(jnp.int32, x.shape, 1)
    swapped = jnp.where(lane % 2 == 0, pltpu.roll(x, x.shape[1] - 1, 1), pltpu.roll(x, 1, 1))
    return x * cos + swapped * sin_signed


def _ret_kernel(*refs, reverse):
    for _ in _ret_stages(*refs, reverse=reverse):
        pass


def _ret_stages(decay_ref, q_ref, k_ref, v_ref, cos_ref, sin_ref, s0_ref, acc_ref, *rest, reverse):
    del acc_ref
    if reverse:
        of_ref, g_ref, gn_ref, o_ref, sfin_ref, st_ref, dm_ref, dq_ref, dk_ref, ds_ref = rest
    else:
        o_ref, sfin_ref, st_ref, dm_ref, dq_ref, dk_ref, ds_ref = rest
    step = pl.program_id(0)
    blk = NBLK - 1 - step if reverse else step
    direction = 1 if reverse else 0

    @pl.when(step == 0)
    def _():
        row = lax.broadcasted_iota(jnp.int32, (RC, RC), 0).astype(F32)
        col = lax.broadcasted_iota(jnp.int32, (RC, RC), 1).astype(F32)
        for h in range(RET_HEADS):
            lg = -jnp.exp(jnp.full((RC, RC), decay_ref[direction, h], F32))
            if reverse:
                diff = col - row
                mask = diff > 0
                q_pow = RC - row
                k_pow = row
            else:
                diff = row - col
                mask = diff >= 0
                q_pow = row + 1.0
                k_pow = RC - 1.0 - row
            dm_ref[h] = jnp.where(mask, jnp.exp(lg * jnp.where(mask, diff, 0.0)), 0.0)
            dq_ref[h] = jnp.exp(lg * q_pow)
            dk_ref[h] = jnp.exp(lg * k_pow)
            ds_ref[h] = jnp.exp(lg * RC)

    starts = _is_last_block(blk) if reverse else _is_first_block(blk)

    is_prompt = blk < N_PROMPT_BLK

    @pl.when(starts)
    def _():
        st_ref[...] = jnp.zeros_like(st_ref)

    @pl.when(jnp.logical_and(starts, jnp.logical_not(is_prompt)))
    def _():
        for h in range(RET_HEADS):
            off = (h % 2) * RET_DK
            st_ref[h, off:off + RET_DK, :] = s0_ref[h]

    lane = lax.broadcasted_iota(jnp.int32, (1, RET_PAIR), 1)
    chunks = range(RB // RC)
    chunk_order = list(reversed(chunks) if reverse else chunks)
    units = [(h, c) for h in range(RET_HEADS) for c in chunk_order]
    rows_of = lambda c: slice(c * RC, (c + 1) * RC)
    vcols = lambda h: slice(h * RET_DV, (h + 1) * RET_DV)
    roped = {}
    for p in range(RET_HEADS // 2):
        cols = slice(p * RET_PAIR, (p + 1) * RET_PAIR)
        for c in chunk_order:
            rows = rows_of(c)
            cos, sin = cos_ref[rows, :], sin_ref[rows, :]
            roped[p, c] = (_rope(q_ref[rows, cols], cos, sin),
                           _rope(k_ref[rows, cols] * (RET_DK ** -0.5), cos, sin))
    yield
    qm_u, vh_u, att_u, kd_u = {}, {}, {}, {}
    for h, c in units:
        head_mask = (lane // RET_DK == h % 2).astype(F32)
        q2, k2 = roped[h // 2, c]
        vh_u[h, c] = v_ref[rows_of(c), vcols(h)].astype(BF16)
        qm_u[h, c] = (q2 * head_mask).astype(BF16)
        km = k2 * head_mask
        att_u[h, c] = (_dot_nt(qm_u[h, c], km.astype(BF16)) * dm_ref[h]).astype(BF16)
        kd_u[h, c] = (km * dk_ref[h]).astype(BF16)
    yield
    intra_u = {u: _dot(att_u[u], vh_u[u]) for u in units}
    delta_u = {u: _dot_tn(kd_u[u], vh_u[u]) for u in units}
    yield
    state_u = {}
    for h in range(RET_HEADS):
        st = st_ref[h]
        for c in chunk_order:
            state_u[h, c] = st.astype(BF16)
            st = st * ds_ref[h] + delta_u[h, c]
        st_ref[h] = st
    yield
    for h, c in units:
        rows, out_cols = rows_of(c), vcols(h)
        o = intra_u[h, c] + _dot(qm_u[h, c], state_u[h, c]) * dq_ref[h]
        if reverse:
            o = o + of_ref[rows, out_cols]
            o = o * lax.rsqrt(jnp.mean(o * o, axis=-1, keepdims=True) + EPS)
            o = o * gn_ref[:, out_cols] * _silu(g_ref[rows, out_cols])
            o_ref[rows, out_cols] = o.astype(o_ref.dtype)
        else:
            o_ref[rows, out_cols] = o

    ends = _is_first_block(blk) if reverse else _is_last_block(blk)

    @pl.when(jnp.logical_and(ends, is_prompt))
    def _():
        for h in range(RET_HEADS):
            off = (h % 2) * RET_DK
            sfin_ref[h] = st_ref[h, off:off + RET_DK, :]


def _sample_seq(blk):
    return jnp.clip(_seq_of_block(blk) - BATCH, 0, DEC_BATCH - 1)


def _prompt_seq(blk):
    return jnp.minimum(_seq_of_block(blk), BATCH - 1)


def _retention(decay, q, k, v, cos_tab, sin_tab, state_in, state_out, layer, *, reverse,
               o_fwd=None, g=None, gn=None, conv=None, u=None, outproj=None):
    bmap = (lambda j: NBLK - 1 - j) if reverse else (lambda j: j)
    direction = 1 if reverse else 0
    qd, vd = RET_HEADS * RET_DK, RET_HEADS * RET_DV
    row_spec = lambda width: pl.BlockSpec((RB, width), lambda j: (bmap(j), 0))
    state_blk = (None, None, None, RET_HEADS, RET_DK, RET_DV)
    in_specs = [pl.BlockSpec(memory_space=pltpu.SMEM), row_spec(qd), row_spec(qd), row_spec(vd),
                pl.BlockSpec((RB, RET_PAIR), lambda j: (_rope_block(bmap(j)), 0)),
                pl.BlockSpec((RB, RET_PAIR), lambda j: (_rope_block(bmap(j)), 0)),
                pl.BlockSpec(state_blk, lambda j: (_sample_seq(bmap(j)), layer, direction, 0, 0, 0)),
                pl.BlockSpec(memory_space=pl.ANY)]
    args = [decay, q, k, v, cos_tab, sin_tab, state_in, state_out]
    if reverse:
        in_specs += [row_spec(vd), row_spec(vd), pl.BlockSpec((1, vd), lambda j: (0, 0))]
        args += [o_fwd, g, gn.reshape(1, vd)]
    tile = pltpu.VMEM((RET_HEADS, RC, RC), F32)
    out_specs = [row_spec(vd),
                 pl.BlockSpec(state_blk, lambda j: (_prompt_seq(bmap(j)), layer, direction, 0, 0, 0))]
    out_shape = [jax.ShapeDtypeStruct((NT, vd), BF16 if reverse else F32),
                 jax.ShapeDtypeStruct(state_out.shape, F32)]
    scratch = [tile, tile, tile, tile, tile]
    body = functools.partial(_ret_kernel, reverse=reverse)
    if conv is not None:
        assert not reverse
        n_ret = (len(in_specs), len(out_specs), len(scratch))
        c_in, c_args, c_out, c_shape, c_scratch = _conv_parts(*conv)
        in_specs, args = in_specs + c_in, args + c_args
        out_specs, out_shape, scratch = out_specs + c_out, out_shape + c_shape, scratch + c_scratch
        body = functools.partial(_ret_conv_kernel, n_ret=n_ret, n_conv=(len(c_in), len(c_out), len(c_scratch)))
    state_output = 1
    if outproj is not None:
        assert reverse
        call = _with_outproj(body, in_specs, args, out_specs, out_shape, scratch, [u], outproj)
        body, in_specs, args = call["kernel"], call["in_specs"], call["args"]
        out_specs, out_shape, scratch = call["out_specs"], call["out_shape"], call["scratch"]
        state_output = 0
    return pl.pallas_call(
        body,
        grid=(NBLK,),
        in_specs=in_specs,
        out_specs=out_specs,
        out_shape=out_shape,
        input_output_aliases={7: state_output},
        scratch_shapes=scratch,
        compiler_params=_params(),
        name="retention_bwd" if reverse else "retention_fwd",
    )(*args)


def _ret_conv_kernel(*refs, n_ret, n_conv):
    (ri, ro, rs), (ci, co, cs) = n_ret, n_conv
    ins, outs, scr = refs[:ri + ci], refs[ri + ci:ri + ci + ro + co], refs[ri + ci + ro + co:]
    bodies = [_conv_stages(*ins[ri:], *outs[ro:], *scr[rs:]),
              _ret_stages(*ins[:ri], *outs[:ro], *scr[:rs], reverse=False)]
    while bodies:
        for body in list(bodies):
            if next(body, StopIteration) is StopIteration:
                bodies.remove(body)


CONV_RT = 32
CONV_CT = 128
CONV_SPAN = RB + 2 * HALO - 8


def _conv_stages(a_ref, ga_ref, ap_ref, gap_ref, an_ref, gan_ref, cw_ref, cb_ref, lng_ref, lnb_ref,
                 o_ref, u_ref, y_ref, us_ref):
    blk = pl.program_id(0)
    keep_prev = jnp.where(_is_first_block(blk), 0.0, 1.0)
    keep_next = jnp.where(_is_last_block(blk), 0.0, 1.0)
    u_ref[0:HALO, :] = ap_ref[...] * _sigmoid(gap_ref[...]) * keep_prev
    u_ref[HALO:HALO + RB, :] = a_ref[...] * _sigmoid(ga_ref[...])
    u_ref[HALO + RB:HALO + RB + HALO, :] = an_ref[...] * _sigmoid(gan_ref[...]) * keep_next
    for r in range(1, 8):
        us_ref[r - 1] = u_ref[r:r + CONV_SPAN, :]
    for ct in range(CONV_CH // CONV_CT):
        yield
        cols = slice(ct * CONV_CT, (ct + 1) * CONV_CT)
        for rt in range(RB // CONV_RT):
            acc = jnp.zeros((CONV_RT, CONV_CT), F32)
            for w in range(CONV_WIDTH):
                tiles, r = divmod(HALO - CONV_PAD + w, 8)
                base = rt * CONV_RT + 8 * tiles
                src = u_ref if r == 0 else us_ref.at[r - 1]
                acc = acc + src[base:base + CONV_RT, cols] * cw_ref[w:w + 1, cols]
            y_ref[rt * CONV_RT:(rt + 1) * CONV_RT, cols] = acc + cb_ref[:, cols]
    y = y_ref[...]
    mu = jnp.mean(y, axis=-1, keepdims=True)
    var = jnp.mean(jnp.square(y - mu), axis=-1, keepdims=True)
    o_ref[...] = _silu((y - mu) * lax.rsqrt(var + EPS) * lng_ref[...] + lnb_ref[...]).astype(o_ref.dtype)


def _conv_parts(a, ga, cw, cb, lng, lnb):
    per_blk = RB // HALO
    n_halo = NT // HALO
    row_spec = pl.BlockSpec((RB, CONV_CH), lambda i: (i, 0))
    prev_spec = pl.BlockSpec((HALO, CONV_CH), lambda i: (jnp.maximum(i * per_blk - 1, 0), 0))
    next_spec = pl.BlockSpec((HALO, CONV_CH), lambda i: (jnp.minimum((i + 1) * per_blk, n_halo - 1), 0))
    vec = pl.BlockSpec((1, CONV_CH), lambda i: (0, 0))
    in_specs = [row_spec, row_spec, prev_spec, prev_spec, next_spec, next_spec,
                pl.BlockSpec((CONV_WIDTH, CONV_CH), lambda i: (0, 0)), vec, vec, vec]
    args = [a, ga, a, ga, a, ga, cw, cb.reshape(1, -1), lng.reshape(1, -1), lnb.reshape(1, -1)]
    scratch = [pltpu.VMEM((RB + 2 * HALO, CONV_CH), F32), pltpu.VMEM((RB, CONV_CH), F32),
               pltpu.VMEM((7, CONV_SPAN, CONV_CH), F32)]
    return in_specs, args, [row_spec], [jax.ShapeDtypeStruct((NT, CONV_CH), BF16)], scratch


GC = GLA_CHUNK
GLA_NSUB = GC // GLA_SUB


def _split3(a):
    p1 = a.astype(BF16)
    r1 = a - p1.astype(F32)
    p2 = r1.astype(BF16)
    p3 = (r1 - p2.astype(F32)).astype(BF16)
    return p1, p2, p3


def _gla_kernel(alr_ref, q_ref, k_ref, v_ref, wa_ref, ba_ref, s0_ref, acc_ref, *rest, reverse):
    del acc_ref
    if reverse:
        of_ref, r_ref, gn_ref, o_ref, sfin_ref, st_ref, b_ref, tri_ref, ob_ref = rest
    else:
        o_ref, sfin_ref, st_ref, b_ref, tri_ref = rest
        ob_ref = o_ref
    step = pl.program_id(0)
    blk = NBLK - 1 - step if reverse else step
    starts = _is_last_block(blk) if reverse else _is_first_block(blk)
    is_prompt = blk < N_PROMPT_BLK

    @pl.when(jnp.logical_and(starts, is_prompt))
    def _():
        st_ref[...] = jnp.zeros_like(st_ref)

    @pl.when(jnp.logical_and(starts, jnp.logical_not(is_prompt)))
    def _():
        for h in range(GLA_HEADS):
            st_ref[h] = s0_ref[h].T

    @pl.when(step == 0)
    def _():
        row = lax.broadcasted_iota(jnp.int32, (RB, RB), 0)
        col = lax.broadcasted_iota(jnp.int32, (RB, RB), 1)
        ordered = col >= row if reverse else col <= row
        tri_ref[...] = jnp.where(jnp.logical_and(row // GC == col // GC, ordered), 1.0, 0.0).astype(BF16)

    heads = range(GLA_HEADS)
    hcols = [slice(h * GLA_DK, (h + 1) * GLA_DK) for h in heads]
    alr = alr_ref[...]
    z = [_dot_hi(alr, wa_ref[:, hcols[h]]) + ba_ref[:, hcols[h]] for h in heads]
    log_a = [(jnp.minimum(zh, 0.0) - jnp.log(1.0 + jnp.exp(-jnp.abs(zh)))) * (1.0 / GLA_TAU) for zh in z]
    parts = [_split3(la) for la in log_a]
    tri = tri_ref[...]
    for h in heads:
        g1, g2, g3 = parts[h]
        b_ref[:, hcols[h]] = _dot(tri, g1) + (_dot(tri, g2) + _dot(tri, g3))

    c_row = lax.broadcasted_iota(jnp.int32, (GC, 1), 0)
    a_row = lax.broadcasted_iota(jnp.int32, (GC, GC), 0)
    a_col = lax.broadcasted_iota(jnp.int32, (GC, GC), 1)
    att_mask = a_col > a_row if reverse else a_col <= a_row
    chunks = range(RB // GC)
    chunk_order = list(reversed(chunks) if reverse else chunks)
    units = [(h, c) for h in range(GLA_HEADS) for c in chunk_order]
    kcols = lambda h: slice(h * GLA_DK, (h + 1) * GLA_DK)
    vcols = lambda h: slice(h * GLA_DV, (h + 1) * GLA_DV)
    rows_of = lambda c: slice(c * GC, (c + 1) * GC)

    vh_u, qe_u, ke_u, decay_u, att_u = {}, {}, {}, {}, {}
    for h, c in units:
        rows = rows_of(c)
        b = b_ref[rows, kcols(h)]
        qh = q_ref[rows, kcols(h)] * (GLA_DK ** -0.5)
        kh = k_ref[rows, kcols(h)]
        vh_u[h, c] = v_ref[rows, vcols(h)].astype(BF16)
        edge = b[0:1, :] if reverse else b[GC - 1:GC, :]
        bounds = []
        for s in range(GLA_NSUB):
            if reverse:
                hi = (s + 1) * GLA_SUB
                bounds.append(b[hi:hi + 1, :] if s < GLA_NSUB - 1 else jnp.zeros((1, GLA_DK), F32))
            else:
                lo = s * GLA_SUB
                bounds.append(b[lo - 1:lo, :] if s > 0 else jnp.zeros((1, GLA_DK), F32))
        own = jnp.concatenate([jnp.broadcast_to(bd, (GLA_SUB, GLA_DK)) for bd in bounds], axis=0)
        q_own = qh * jnp.exp(b - own)
        q_parts, k_parts = [], []
        for s, bd in enumerate(bounds):
            q_parts.append(jnp.where(c_row // GLA_SUB == s, q_own, 0.0))
            reach = c_row >= s * GLA_SUB if reverse else c_row < (s + 1) * GLA_SUB
            k_parts.append(kh * jnp.exp(jnp.where(reach, bd - b, -jnp.inf)))
        q_bd = jnp.concatenate(q_parts, axis=1).astype(BF16)
        k_cat = jnp.concatenate(k_parts, axis=1).astype(BF16)
        att_u[h, c] = jnp.where(att_mask, _dot_nt(q_bd, k_cat), 0.0).astype(BF16)
        qe_u[h, c] = (qh * jnp.exp(b)).astype(BF16)
        ke_u[h, c] = (kh * jnp.exp(edge - b)).astype(BF16)
        decay_u[h, c] = jnp.exp(edge)
    intra_u = {u: _dot(att_u[u], vh_u[u]) for u in units}
    delta_u = {u: _dot_tn(vh_u[u], ke_u[u]) for u in units}
    state_u = {}
    for h in range(GLA_HEADS):
        st = st_ref[h]
        for c in chunk_order:
            state_u[h, c] = st.astype(BF16)
            st = st * decay_u[h, c] + delta_u[h, c]
        st_ref[h] = st
    for h, c in units:
        ob_ref[rows_of(c), vcols(h)] = intra_u[h, c] + _dot_nt(qe_u[h, c], state_u[h, c])

    if reverse:
        for h in range(GLA_HEADS):
            cols = vcols(h)
            o = ob_ref[:, cols] + of_ref[:, cols]
            o = o * lax.rsqrt(jnp.mean(o * o, axis=-1, keepdims=True) + EPS)
            o_ref[:, cols] = (o * gn_ref[:, cols] * _silu(r_ref[:, cols])).astype(o_ref.dtype)

    ends = _is_first_block(blk) if reverse else _is_last_block(blk)

    @pl.when(jnp.logical_and(ends, is_prompt))
    def _():
        for h in range(GLA_HEADS):
            sfin_ref[h] = st_ref[h].T


def _gla(alr, q, k, v, wa, ba, state_in, state_out, layer, *, reverse, o_fwd=None, r=None, gn=None,
         outproj=None):
    bmap = (lambda j: NBLK - 1 - j) if reverse else (lambda j: j)
    direction = 1 if reverse else 0
    qd, vd = GLA_HEADS * GLA_DK, GLA_HEADS * GLA_DV
    row_spec = lambda width: pl.BlockSpec((RB, width), lambda j: (bmap(j), 0))
    state_blk = (None, None, None, GLA_HEADS, GLA_DK, GLA_DV)
    in_specs = [row_spec(2 * GLA_RANK), row_spec(qd), row_spec(qd), row_spec(vd),
                pl.BlockSpec((2 * GLA_RANK, qd), lambda j: (0, 0)),
                pl.BlockSpec((1, qd), lambda j: (0, 0)),
                pl.BlockSpec(state_blk, lambda j: (_sample_seq(bmap(j)), layer, direction, 0, 0, 0)),
                pl.BlockSpec(memory_space=pl.ANY)]
    args = [alr, q, k, v, wa, ba, state_in, state_out]
    scratch = [pltpu.VMEM((GLA_HEADS, GLA_DV, GLA_DK), F32), pltpu.VMEM((RB, qd), F32),
               pltpu.VMEM((RB, RB), BF16)]
    if reverse:
        in_specs += [row_spec(vd), row_spec(vd), pl.BlockSpec((1, vd), lambda j: (0, 0))]
        args += [o_fwd, r, gn.reshape(1, vd)]
        scratch += [pltpu.VMEM((RB, vd), F32)]
    call = dict(kernel=functools.partial(_gla_kernel, reverse=reverse), in_specs=in_specs, args=args,
                out_specs=[row_spec(vd),
                           pl.BlockSpec(state_blk, lambda j: (_prompt_seq(bmap(j)), layer, direction, 0, 0, 0))],
                out_shape=[jax.ShapeDtypeStruct((NT, vd), BF16 if reverse else F32),
                           jax.ShapeDtypeStruct(state_out.shape, F32)],
                scratch=scratch)
    state_output = 1
    if outproj is not None:
        assert reverse
        call = _with_outproj(call["kernel"], in_specs, args, call["out_specs"], call["out_shape"], scratch,
                             [], outproj)
        state_output = 0
    return pl.pallas_call(
        call["kernel"],
        grid=(NBLK,),
        in_specs=call["in_specs"],
        out_specs=call["out_specs"],
        out_shape=call["out_shape"],
        input_output_aliases={7: state_output},
        scratch_shapes=call["scratch"],
        compiler_params=_params(),
        name="gla_bwd" if reverse else "gla_fwd",
    )(*call["args"])


OUT_SUB = 128


def _outproj_kernel(*refs, n_mix):
    mix_refs = refs[:n_mix]
    (w_ref, x_ref, gate_ref, g2_ref, shift_ref, scale_ref, rw_ref, rb_ref,
     x1_ref, h2_ref, logit_ref, w_bf) = refs[n_mix:]
    step = pl.program_id(0)

    @pl.when(step == 0)
    def _():
        w_bf[...] = w_ref[...].astype(BF16)

    row = _mod_row(NBLK - 1 - step)
    r_hi, r_lo = _split2(rw_ref[...])
    groups = [slice(p * OUT_SUB, (p + 1) * OUT_SUB) for p in range(RB // OUT_SUB)]
    mixed = []
    for rows in groups:
        m, off = None, 0
        for mix_ref in mix_refs:
            width = mix_ref.shape[1]
            part = _dot(mix_ref[rows, :], w_bf[off:off + width, :])
            m = part if m is None else m + part
            off += width
        mixed.append(m)
    normed = []
    for rows, m in zip(groups, mixed):
        x1 = x_ref[rows, :] + gate_ref[pl.ds(row, 1), :] * m
        x1_ref[rows, :] = x1
        h2 = _norm_mod(x1, g2_ref, shift_ref, scale_ref, row)
        h2_ref[rows, :] = _pack_rows(h2)
        normed.append(_split2(h2))
    for rows, (h_hi, h_lo) in zip(groups, normed):
        logit_ref[:, rows] = _dot_nt(r_hi, h_hi) + (_dot_nt(r_hi, h_lo) + _dot_nt(r_lo, h_hi)) + rb_ref[...]


def _with_outproj(scan_body, in_specs, args, out_specs, out_shape, scratch, ext_mixes, outproj):
    w_all, index, x, gate, g2, shift, scale, rw, rb = outproj
    blk = lambda j: NBLK - 1 - j
    row_spec = lambda width: pl.BlockSpec((RB, width), lambda j: (blk(j), 0))
    full = lambda shape: pl.BlockSpec(shape, lambda j: (0,) * len(shape))
    mod_spec = full((MOD_ROWS, D_MODEL))
    n_mixed = w_all.shape[1]
    o_in = ([row_spec(m.shape[1]) for m in ext_mixes]
            + [pl.BlockSpec((None, n_mixed, D_MODEL), lambda j: (index, 0, 0)),
               row_spec(D_MODEL), mod_spec, full((1, D_MODEL)), mod_spec, mod_spec,
               full((N_EXPERTS, D_MODEL)), full((N_EXPERTS, 1))])
    o_args = list(ext_mixes) + [w_all, x, gate, g2.reshape(1, D_MODEL), shift, scale, rw.T,
                                rb.reshape(N_EXPERTS, 1)]
    o_out = [row_spec(D_MODEL), row_spec(D_MODEL // 2), pl.BlockSpec((N_EXPERTS, RB), lambda j: (0, blk(j)))]
    o_shape = [jax.ShapeDtypeStruct((NT, D_MODEL), F32), jax.ShapeDtypeStruct((NT, D_MODEL // 2), jnp.uint32),
               jax.ShapeDtypeStruct((N_EXPERTS, NT), F32)]
    n_si, n_oi, n_oo, n_ss = len(in_specs), len(o_in), len(o_out), len(scratch)
    mix = out_shape[0]

    def fused(*refs):
        scan_in, out_in = refs[:n_si], refs[n_si:n_si + n_oi]
        sfin_ref, out_out = refs[n_si + n_oi], refs[n_si + n_oi + 1:n_si + n_oi + 1 + n_oo]
        rest = refs[n_si + n_oi + 1 + n_oo:]
        scan_scr, mix_ref, w_bf = rest[:n_ss], rest[n_ss], rest[n_ss + 1]
        scan_body(*scan_in, mix_ref, sfin_ref, *scan_scr)
        _outproj_kernel(mix_ref, *out_in, *out_out, w_bf, n_mix=1 + len(ext_mixes))

    return dict(kernel=fused, in_specs=in_specs + o_in, args=args + o_args,
                out_specs=out_specs[1:] + o_out, out_shape=out_shape[1:] + o_shape,
                scratch=scratch + [pltpu.VMEM((RB, mix.shape[1]), mix.dtype),
                                   pltpu.VMEM((n_mixed, D_MODEL), BF16)])


ROUTE_BLK = 2048
ROUTE_SUB = 256


def _route_kernel(lg_ref, idx_ref, rank_ref, gt_ref, cnt_ref, carry_ref):
    @pl.when(pl.program_id(0) == 0)
    def _():
        carry_ref[...] = jnp.zeros_like(carry_ref)

    logits = lg_ref[...]
    eid = lax.broadcasted_iota(jnp.int32, logits.shape, 0).astype(F32)
    work = logits
    onehots, top_vals = [], []
    for kk in range(TOP_K):
        top = jnp.max(work, axis=0, keepdims=True)
        first = jnp.min(jnp.where(work == top, eid, float(N_EXPERTS)), axis=0, keepdims=True)
        onehot = eid == first
        idx_ref[kk:kk + 1, :] = first.astype(jnp.int32)
        onehots.append(onehot)
        top_vals.append(top)
        work = jnp.where(onehot, -jnp.inf, work)
    exps = [jnp.exp(v - top_vals[0]) for v in top_vals]
    denom = exps[0]
    for e in exps[1:]:
        denom = denom + e
    gt_ref[...] = jnp.zeros_like(gt_ref)
    for kk in range(TOP_K):
        gt_ref[kk:kk + 1, :] = exps[kk] / denom

    sel = jnp.zeros(logits.shape, F32)
    for onehot in onehots:
        sel = sel + jnp.where(onehot, 1.0, 0.0)
    sel = sel.astype(BF16)
    r_i = lax.broadcasted_iota(jnp.int32, (ROUTE_SUB, ROUTE_SUB), 0)
    c_i = lax.broadcasted_iota(jnp.int32, (ROUTE_SUB, ROUTE_SUB), 1)
    before = jnp.where(r_i < c_i, 1.0, 0.0).astype(BF16)
    ones = jnp.ones((ROUTE_SUB, ROUTE_SUB), BF16)
    carry = carry_ref[...]
    for s in range(ROUTE_BLK // ROUTE_SUB):
        cols = slice(s * ROUTE_SUB, (s + 1) * ROUTE_SUB)
        pos = _dot(sel[:, cols], before) + carry
        for kk in range(TOP_K):
            rank = jnp.sum(jnp.where(onehots[kk][:, cols], pos, 0.0), axis=0, keepdims=True)
            rank_ref[kk:kk + 1, cols] = rank.astype(jnp.int32)
        carry = carry + _dot(sel[:, cols], ones)
    carry_ref[...] = carry
    cnt_ref[...] = carry.astype(jnp.int32)


def _route(logits_t):
    col_spec = lambda rows: pl.BlockSpec((rows, ROUTE_BLK), lambda i: (0, i))
    return pl.pallas_call(
        _route_kernel,
        grid=(NT // ROUTE_BLK,),
        in_specs=[col_spec(N_EXPERTS)],
        out_specs=[col_spec(TOP_K), col_spec(TOP_K), col_spec(8),
                   pl.BlockSpec((N_EXPERTS, ROUTE_SUB), lambda i: (0, 0))],
        out_shape=[jax.ShapeDtypeStruct((TOP_K, NT), jnp.int32), jax.ShapeDtypeStruct((TOP_K, NT), jnp.int32),
                   jax.ShapeDtypeStruct((8, NT), F32), jax.ShapeDtypeStruct((N_EXPERTS, ROUTE_SUB), jnp.int32)],
        scratch_shapes=[pltpu.VMEM((N_EXPERTS, ROUTE_SUB), F32)],
        compiler_params=_params(),
        name="route",
    )(logits_t)


TM = 1024
TM_SUB = 256
MOE_NBLK = NT * TOP_K // TM + N_EXPERTS
MOE_ROWS = MOE_NBLK * TM
HALF = D_MODEL // 2


def _moe_kernel(be_ref, nv_ref, nx_ref, x_ref, wgu_hbm, bgu_ref, wd_hbm, bd_ref, y_ref,
                wgu_st, wd_st, wgu_bf, wd_bf, sems, *, layer):
    i = pl.program_id(0)
    n_valid = nv_ref[i]

    def fetch(e):
        return (pltpu.make_async_copy(wgu_hbm.at[layer, e], wgu_st, sems.at[0]),
                pltpu.make_async_copy(wd_hbm.at[layer, e], wd_st, sems.at[1]))

    @pl.when(i == 0)
    def _():
        for cp in fetch(be_ref[0]):
            cp.start()

    @pl.when(n_valid > 0)
    def _():
        e = be_ref[i]
        changed = jnp.logical_or(i == 0, e != be_ref[jnp.maximum(i - 1, 0)])

        @pl.when(changed)
        def _():
            for cp in fetch(e):
                cp.wait()
            wgu_bf[...] = wgu_st[...].astype(BF16)
            wd_bf[...] = wd_st[...].astype(BF16)
            nxt = nx_ref[e]

            @pl.when(nxt >= 0)
            def _():
                for cp in fetch(nxt):
                    cp.start(priority=1)

    def ffn_pass(p, masked):
        rows = slice(p * TM_SUB, (p + 1) * TM_SUB)
        x = x_ref[rows, :]
        if masked:
            row_id = lax.broadcasted_iota(jnp.int32, (TM_SUB, 1), 0) + p * TM_SUB
            x = jnp.where(row_id < n_valid, x, jnp.uint32(0))
        x_lo, x_hi = _unpack_rows(x)
        x_lo, x_hi = x_lo.astype(BF16), x_hi.astype(BF16)
        yield
        gu = _dot(x_lo, wgu_bf[:HALF, :]) + _dot(x_hi, wgu_bf[HALF:, :]) + bgu_ref[...]
        yield
        gate = jnp.minimum(gu[:, :D_FF], SWIGLU_LIMIT)
        up = jnp.clip(gu[:, D_FF:], -SWIGLU_LIMIT, SWIGLU_LIMIT)
        hdn = (gate * _sigmoid(SWIGLU_ALPHA * gate) * (up + 1.0)).astype(BF16)
        yield
        y = _dot(hdn, wd_bf[...]) + bd_ref[...]
        yield
        y_ref[rows, :] = _pack_rows(y)

    @pl.when(n_valid == TM)
    def _():
        passes = [ffn_pass(p, masked=False) for p in range(TM // TM_SUB)]
        n_stages = 5
        for t in range(n_stages + len(passes) - 1):
            for lag, body in enumerate(passes):
                if 0 <= t - lag < n_stages:
                    next(body, None)

    for p in range(TM // TM_SUB):
        @pl.when(jnp.logical_and(n_valid < TM, n_valid > p * TM_SUB))
        def _():
            for _ in ffn_pass(p, masked=True):
                pass


def _moe_experts(layer, block_e, n_valid, next_e, xs, w_gu, b_gu, w_down, b_down):
    grid_spec = pltpu.PrefetchScalarGridSpec(
        num_scalar_prefetch=3,
        grid=(MOE_NBLK,),
        in_specs=[pl.BlockSpec((TM, HALF), lambda i, be, nv, nx: (i, 0)),
                  pl.BlockSpec(memory_space=pl.ANY),
                  pl.BlockSpec((None, None, 1, 2 * D_FF), lambda i, be, nv, nx: (layer, be[i], 0, 0)),
                  pl.BlockSpec(memory_space=pl.ANY),
                  pl.BlockSpec((None, None, 1, D_MODEL), lambda i, be, nv, nx: (layer, be[i], 0, 0))],
        out_specs=pl.BlockSpec((TM, HALF), lambda i, be, nv, nx: (i, 0)),
        scratch_shapes=[pltpu.VMEM((D_MODEL, 2 * D_FF), F32), pltpu.VMEM((D_FF, D_MODEL), F32),
                        pltpu.VMEM((D_MODEL, 2 * D_FF), BF16), pltpu.VMEM((D_FF, D_MODEL), BF16),
                        pltpu.SemaphoreType.DMA((2,))],
    )
    return pl.pallas_call(
        functools.partial(_moe_kernel, layer=layer),
        grid_spec=grid_spec,
        out_shape=jax.ShapeDtypeStruct((MOE_ROWS, HALF), jnp.uint32),
        compiler_params=_params(),
        name="moe_experts",
    )(block_e, n_valid, next_e, xs, w_gu, b_gu.reshape(DEPTH, N_EXPERTS, 1, -1), w_down,
      b_down.reshape(DEPTH, N_EXPERTS, 1, -1))


SC_WORKERS = 32
SC_WIN = 64


def _sc_mesh():
    return plsc.VectorSubcoreMesh(core_axis_name="core", subcore_axis_name="subcore")


def _sc_worker():
    return lax.axis_index("core") * (SC_WORKERS // 2) + lax.axis_index("subcore")


def _sc_scatter_rows(x, dest_t, n_rows):
    n, width = x.shape
    kk = dest_t.shape[0]
    per = n // SC_WORKERS
    n_win = per // SC_WIN
    assert per * SC_WORKERS == n and n_win * SC_WIN == per and n_win % 2 == 0

    @pl.kernel(out_type=jax.ShapeDtypeStruct((n_rows, width), x.dtype), mesh=_sc_mesh(),
               scratch_types=[pltpu.VMEM((kk, per), jnp.int32), pltpu.VMEM((SC_WIN, width), x.dtype),
                              pltpu.VMEM((SC_WIN, width), x.dtype), pltpu.SemaphoreType.DMA((4,))])
    def scatter(x_hbm, i_hbm, o_hbm, idx_v, buf0, buf1, sems):
        base = _sc_worker() * per
        pltpu.sync_copy(i_hbm.at[:, pl.ds(base, per)], idx_v)

        def get(j, buf, s):
            return pltpu.make_async_copy(x_hbm.at[pl.ds(base + j * SC_WIN, SC_WIN)], buf, sems.at[s])

        def put(j, q, buf, s):
            return pltpu.make_async_copy(buf, o_hbm.at[idx_v.at[q, pl.ds(j * SC_WIN, SC_WIN)]], sems.at[s])

        get(0, buf0, 0).start()

        @pl.loop(0, n_win, step=2)
        def _(j):
            get(j, buf0, 0).wait()

            @pl.when(j > 0)
            def _():
                for q in range(kk):
                    put(j - 1, q, buf1, 3).wait()

            get(j + 1, buf1, 1).start()
            for q in range(kk):
                put(j, q, buf0, 2).start()
            get(j + 1, buf1, 1).wait()
            for q in range(kk):
                put(j, q, buf0, 2).wait()

            @pl.when(j + 2 < n_win)
            def _():
                get(j + 2, buf0, 0).start()

            for q in range(kk):
                put(j + 1, q, buf1, 3).start()

        for q in range(kk):
            put(n_win - 1, q, buf1, 3).wait()

    return scatter(x, dest_t)


def _sc_gather_rows(y, idx):
    n = idx.shape[0]
    width = y.shape[1]
    per = n // SC_WORKERS
    n_win = per // SC_WIN
    assert per * SC_WORKERS == n and n_win * SC_WIN == per and n_win % 2 == 0

    @pl.kernel(out_type=jax.ShapeDtypeStruct((n, width), y.dtype), mesh=_sc_mesh(),
               scratch_types=[pltpu.VMEM((per,), jnp.int32), pltpu.VMEM((SC_WIN, width), y.dtype),
                              pltpu.VMEM((SC_WIN, width), y.dtype), pltpu.SemaphoreType.DMA((4,))])
    def gather(y_hbm, i_hbm, o_hbm, idx_v, buf0, buf1, sems):
        base = _sc_worker() * per
        pltpu.sync_copy(i_hbm.at[pl.ds(base, per)], idx_v)

        def get(j, buf, s):
            return pltpu.make_async_copy(y_hbm.at[idx_v.at[pl.ds(j * SC_WIN, SC_WIN)]], buf, sems.at[s])

        def put(j, buf, s):
            return pltpu.make_async_copy(buf, o_hbm.at[pl.ds(base + j * SC_WIN, SC_WIN)], sems.at[s])

        get(0, buf0, 0).start()

        @pl.loop(0, n_win, step=2)
        def _(j):
            get(j, buf0, 0).wait()

            @pl.when(j > 0)
            def _():
                put(j - 1, buf1, 3).wait()

            get(j + 1, buf1, 1).start()
            put(j, buf0, 2).start()
            get(j + 1, buf1, 1).wait()
            put(j, buf0, 2).wait()

            @pl.when(j + 2 < n_win)
            def _():
                get(j + 2, buf0, 0).start()

            put(j + 1, buf1, 3).start()

        put(n_win - 1, buf1, 3).wait()

    return gather(y, idx)


def _gate_columns(gt_ref):
    r_i = lax.broadcasted_iota(jnp.int32, (RB, RB), 0)
    c_i = lax.broadcasted_iota(jnp.int32, (RB, RB), 1)
    eye = jnp.where(r_i == c_i, 1.0, 0.0).astype(BF16)
    g1, g2, g3 = _split3(gt_ref[...])
    return _dot_nt(eye, g1) + (_dot_nt(eye, g2) + _dot_nt(eye, g3))


def _combined_rows(x1_ref, yg_refs, gw, gate_ref, row, rows=slice(None)):
    acc_lo, acc_hi = None, None
    for kk in range(TOP_K):
        y_lo, y_hi = _unpack_rows(yg_refs[kk][rows, :])
        w = gw[rows, kk:kk + 1]
        acc_lo = y_lo * w if acc_lo is None else acc_lo + y_lo * w
        acc_hi = y_hi * w if acc_hi is None else acc_hi + y_hi * w
    x_lo = x1_ref[rows, :HALF] + gate_ref[pl.ds(row, 1), :HALF] * acc_lo
    x_hi = x1_ref[rows, HALF:] + gate_ref[pl.ds(row, 1), HALF:] * acc_hi
    return x_lo, x_hi


def _combine_specs():
    return [pl.BlockSpec(memory_space=pl.ANY), pl.BlockSpec(memory_space=pl.ANY),
            pl.BlockSpec((8, RB), lambda i: (0, i)), pl.BlockSpec((MOD_ROWS, D_MODEL), lambda i: (0, 0))]


RING = 3


def _ring_scratch():
    return [pltpu.VMEM((RING, RB, D_MODEL), F32), pltpu.VMEM((RING, TOP_K, RB, HALF), jnp.uint32),
            pltpu.SemaphoreType.DMA((RING, 1 + TOP_K))]


def _ring_rows(x1_hbm, yg_hbm, x1_buf, yg_buf, sems):
    i = pl.program_id(0)

    def fetch(blk):
        slot = blk % RING
        row0 = pl.multiple_of(blk * RB, RB)
        copies = [pltpu.make_async_copy(x1_hbm.at[pl.ds(row0, RB), :], x1_buf.at[slot], sems.at[slot, 0])]
        for kk in range(TOP_K):
            copies.append(pltpu.make_async_copy(yg_hbm.at[pl.ds(kk * NT + row0, RB), :], yg_buf.at[slot, kk],
                                                sems.at[slot, 1 + kk]))
        return copies

    @pl.when(i == 0)
    def _():
        for b in range(RING - 1):
            for cp in fetch(b):
                cp.start()

    @pl.when(i + RING - 1 < NBLK)
    def _():
        for cp in fetch(i + RING - 1):
            cp.start()

    for cp in fetch(i):
        cp.wait()
    slot = i % RING
    return x1_buf.at[slot], [yg_buf.at[slot, kk] for kk in range(TOP_K)]


def _combine_final_kernel(x1_hbm, yg_hbm, gt_ref, gate_ref, fg_ref, op_ref, os_ref, x1_buf, yg_buf, sems):
    i = pl.program_id(0)
    x1_ref, yg_refs = _ring_rows(x1_hbm, yg_hbm, x1_buf, yg_buf, sems)
    x_lo, x_hi = _combined_rows(x1_ref, yg_refs, _gate_columns(gt_ref), gate_ref, _mod_row(i))
    ms = (jnp.sum(x_lo * x_lo, axis=-1, keepdims=True) + jnp.sum(x_hi * x_hi, axis=-1, keepdims=True)) / D_MODEL
    scale = lax.rsqrt(ms + EPS)

    @pl.when(i < N_PROMPT_BLK)
    def _():
        op_ref[:, :HALF] = x_lo * scale * fg_ref[:, :HALF]
        op_ref[:, HALF:] = x_hi * scale * fg_ref[:, HALF:]

    @pl.when(i >= N_PROMPT_BLK)
    def _():
        os_ref[:, :HALF] = x_lo * scale * fg_ref[:, :HALF]
        os_ref[:, HALF:] = x_hi * scale * fg_ref[:, HALF:]


def _combine_final(x1, yg, gates_t, gate, final_g):
    return pl.pallas_call(
        _combine_final_kernel,
        grid=(NBLK,),
        in_specs=_combine_specs() + [pl.BlockSpec((1, D_MODEL), lambda i: (0, 0))],
        out_specs=[pl.BlockSpec((RB, D_MODEL), lambda i: (jnp.minimum(i, N_PROMPT_BLK - 1), 0)),
                   pl.BlockSpec((RB, D_MODEL), lambda i: (jnp.maximum(i - N_PROMPT_BLK, 0), 0))],
        out_shape=[jax.ShapeDtypeStruct((NT_PROMPT, D_MODEL), F32),
                   jax.ShapeDtypeStruct((NT - NT_PROMPT, D_MODEL), F32)],
        scratch_shapes=_ring_scratch(),
        compiler_params=_params(),
        name="moe_combine_final",
    )(x1, yg, gates_t, gate, final_g.reshape(1, D_MODEL))


def _routing_plan(counts, idx_t, rank_t):
    counts = counts[:, 0]
    padded = (counts + TM - 1) // TM * TM
    pad_end = jnp.cumsum(padded)
    pad_start = pad_end - padded
    blk_row = (jnp.arange(MOE_NBLK, dtype=jnp.int32) * TM)[:, None]
    ids = jnp.arange(N_EXPERTS, dtype=jnp.int32)
    owns = jnp.logical_and(pad_start[None, :] <= blk_row, blk_row < pad_end[None, :])
    last_used = jnp.max(jnp.where(counts > 0, ids, 0))
    block_e = jnp.where(jnp.any(owns, axis=1), jnp.sum(jnp.where(owns, ids[None, :], 0), axis=1), last_used)
    block_e = block_e.astype(jnp.int32)
    left = jnp.clip(counts[None, :] - (blk_row - pad_start[None, :]), 0, TM)
    n_valid = jnp.sum(jnp.where(owns, left, 0), axis=1).astype(jnp.int32)
    start = jnp.zeros(idx_t.shape, jnp.int32)
    for e in range(N_EXPERTS):
        start = jnp.where(idx_t == e, pad_start[e], start)
    dest_t = (start + rank_t).astype(jnp.int32)
    later = jnp.where(jnp.logical_and(counts[None, :] > 0, ids[None, :] > ids[:, None]), ids[None, :], N_EXPERTS)
    next_e = jnp.min(later, axis=1)
    next_e = jnp.where(next_e == N_EXPERTS, -1, next_e).astype(jnp.int32)
    return block_e, n_valid, next_e, dest_t


def _rope_tables():
    rows = DEC_SEQ // GRID_W
    row = jnp.repeat(jnp.arange(rows, dtype=F32), GRID_W)
    col = jnp.tile(jnp.arange(GRID_W, dtype=F32), rows)
    n_f = RET_DK // 4
    freqs = ROPE_THETA ** (-jnp.arange(n_f, dtype=F32) / n_f)
    ang = jnp.concatenate([row[:, None] * freqs, col[:, None] * freqs], axis=-1)
    cos = jnp.repeat(jnp.cos(ang), 2, axis=-1)
    sin = jnp.repeat(jnp.sin(ang), 2, axis=-1) * jnp.tile(jnp.asarray([-1.0, 1.0], F32), RET_DK // 2)
    cos = jnp.concatenate([jnp.ones((RB, RET_DK), F32), cos], axis=0)
    sin = jnp.concatenate([jnp.zeros((RB, RET_DK), F32), sin], axis=0)
    return jnp.tile(cos, (1, 2)), jnp.tile(sin, (1, 2))


def kernel(x_prompt, x_sample, state_ret, state_gla, c, c_ctx, w_mod, b_mod, norm1_g, norm2_g, final_g, even_w_in, ret_decay, ret_gn, conv_w, conv_b, conv_ln_g, conv_ln_b, even_w_out, odd_w_in, gla_w_a2, gla_b_a2, gla_gn, odd_w_out, router_w, router_b, exp_w_gu, exp_b_gu, exp_w_down, exp_b_down):
    x_src = ("split", x_prompt.reshape(NT_PROMPT, D_MODEL), x_sample.reshape(NT - NT_PROMPT, D_MODEL))
    cvec = jnp.concatenate([c_ctx[None, :], c, jnp.zeros((MOD_ROWS - 1 - DEC_BATCH, D_MODEL), F32)], axis=0)
    mods = _modulation(cvec, w_mod, b_mod).reshape(DEPTH, MOD_ROWS, N_MOD, D_MODEL)
    cos_tab, sin_tab = _rope_tables()
    new_ret = jnp.zeros((BATCH,) + state_ret.shape[1:], F32)
    new_gla = jnp.zeros((BATCH,) + state_gla.shape[1:], F32)
    for l in range(DEPTH):
        mod = [mods[l, :, j, :] for j in range(N_MOD)]
        if l % 2 == 0:
            e = l // 2
            qd, vd = RET_HEADS * RET_DK, RET_HEADS * RET_DV
            x, q, k, v, g, a, ga = _inproj(x_src, norm1_g[l], mod[0], mod[1], even_w_in, e,
                                           (qd, qd, vd, vd, CONV_CH, CONV_CH))
            conv = (a, ga, conv_w[e], conv_b[e], conv_ln_g[e], conv_ln_b[e])
            o_f, new_ret, u = _retention(ret_decay[e], q, k, v, cos_tab, sin_tab, state_ret, new_ret, e,
                                         reverse=False, conv=conv)
            outproj = (even_w_out, e, x, mod[2], norm2_g[l], mod[3], mod[4], router_w[l], router_b[l])
            new_ret, x1, h2, logits_t = _retention(ret_decay[e], q, k, v, cos_tab, sin_tab, state_ret, new_ret, e,
                                                   reverse=True, o_fwd=o_f, g=g, gn=ret_gn[e], u=u,
                                                   outproj=outproj)
        else:
            o = l // 2
            qd, vd = GLA_HEADS * GLA_DK, GLA_HEADS * GLA_DV
            x, q, k, v, r, alr = _inproj(x_src, norm1_g[l], mod[0], mod[1], odd_w_in, o,
                                         (qd, qd, vd, vd, 2 * GLA_RANK))
            zeros = jnp.zeros((GLA_RANK, qd), F32)
            wa_f = jnp.concatenate([gla_w_a2[o, 0], zeros], axis=0)
            wa_b = jnp.concatenate([zeros, gla_w_a2[o, 1]], axis=0)
            o_f, new_gla = _gla(alr, q, k, v, wa_f, gla_b_a2[o, 0].reshape(1, qd), state_gla, new_gla, o,
                                reverse=False)
            outproj = (odd_w_out, o, x, mod[2], norm2_g[l], mod[3], mod[4], router_w[l], router_b[l])
            new_gla, x1, h2, logits_t = _gla(alr, q, k, v, wa_b, gla_b_a2[o, 1].reshape(1, qd), state_gla, new_gla,
                                             o, reverse=True, o_fwd=o_f, r=r, gn=gla_gn[o], outproj=outproj)
        idx_t, rank_t, gates_t, counts = _route(logits_t)
        block_e, n_valid, next_e, dest_t = _routing_plan(counts, idx_t, rank_t)
        xs = _sc_scatter_rows(h2, dest_t, MOE_ROWS)
        yb = _moe_experts(l, block_e, n_valid, next_e, xs, exp_w_gu, exp_b_gu, exp_w_down, exp_b_down)
        yg = _sc_gather_rows(yb, dest_t.reshape(TOP_K * NT))
        x_src = ("moe", x1, yg, gates_t, mod[5])
    y_prompt, y_sample = _combine_final(*x_src[1:], final_g)
    y_prompt = y_prompt.reshape(BATCH, SEQ, D_MODEL)
    y_sample = y_sample.reshape(DEC_BATCH, DEC_SEQ, D_MODEL)
    return (y_prompt, y_sample, new_ret, new_gla)
```
